```python
import math
import jax, jax.numpy as jnp
from jax import lax
import numpy as np

D_MODEL = 1024
BATCH = 8
SEQ = 4096
DEPTH = 2

D_MIX = 1024
MLA_HEADS = 6
MLA_Q_RANK = 256
MLA_KV_RANK = 128
MLA_NOPE = 64
MLA_ROPE = 32
MLA_V = 64
ROPE_THETA = 10000.0
POOL_WINDOWS = (2, 4, 8, 16)
POOL_GROUP = 64
POOL_WIDTH = POOL_GROUP * len(POOL_WINDOWS)
FOX_HEADS = 6
FOX_HEAD_DIM = 64
FOX_GATE_BIAS_INIT = 2.0
BLOCK_Q = 128
D_FF = 2816
EPS = 1e-6
IN_Q_A = MLA_Q_RANK
IN_KV_A = MLA_KV_RANK
IN_K_ROPE = MLA_ROPE
IN_POOL = POOL_WIDTH
IN_FOX_QKV = 3 * FOX_HEADS * FOX_HEAD_DIM
IN_FOX_F = FOX_HEADS
N_IN = IN_Q_A + IN_KV_A + IN_K_ROPE + IN_POOL + IN_FOX_QKV + IN_FOX_F

kernel_name = "hybrid_mla_pool_fox_macaron"


def rmsnorm(x, g):
    xf = x.astype(jnp.float32)
    y = xf * lax.rsqrt(jnp.mean(xf * xf, axis=-1, keepdims=True) + EPS)
    return y.astype(x.dtype) * g


def rope(x, pos):
    r = x.shape[-1]
    inv_freq = ROPE_THETA ** (-jnp.arange(0, r, 2, dtype=jnp.float32) / r)
    ang = pos.astype(jnp.float32)[:, None] * inv_freq[None, :]
    cos = jnp.cos(ang).astype(x.dtype)
    sin = jnp.sin(ang).astype(x.dtype)
    x1, x2 = x[..., : r // 2], x[..., r // 2:]
    return jnp.concatenate([x1 * cos - x2 * sin, x2 * cos + x1 * sin], axis=-1)


def causal_block_attention(q, k, v, scale, log_decay_cum=None):
    b, h, s, dk = q.shape
    dv = v.shape[-1]
    nb = s // BLOCK_Q
    qb = q.reshape(b, h, nb, BLOCK_Q, dk).transpose(2, 0, 1, 3, 4)
    kpos = jnp.arange(s)
    xs = (jnp.arange(nb), qb)
    if log_decay_cum is not None:
        xs = xs + (log_decay_cum.reshape(b, h, nb, BLOCK_Q).transpose(2, 0, 1, 3),)

    def one_block(args):
        i, q_blk = args[0], args[1]
        sc = jnp.einsum('bhqd,bhkd->bhqk', q_blk, k, preferred_element_type=jnp.float32) * scale
        if log_decay_cum is not None:
            c_blk = args[2]
            sc = sc + c_blk[..., :, None] - log_decay_cum[..., None, :].astype(jnp.float32)
        qpos = i * BLOCK_Q + jnp.arange(BLOCK_Q)
        sc = jnp.where(kpos[None, :] <= qpos[:, None], sc, -jnp.inf)
        p = jax.nn.softmax(sc, axis=-1).astype(v.dtype)
        return jnp.einsum('bhqk,bhkd->bhqd', p, v)

    out = lax.map(one_block, xs)
    return out.transpose(1, 2, 0, 3, 4).reshape(b, h, s, dv)


def mla_mixer(q_a, kv_a, k_rope, q_a_norm, w_q_b, kv_a_norm, w_kv_b, pos):
    b, s, _ = q_a.shape
    q = (rmsnorm(q_a, q_a_norm) @ w_q_b).reshape(b, s, MLA_HEADS, MLA_NOPE + MLA_ROPE).transpose(0, 2, 1, 3)
    q_nope, q_pe = q[..., :MLA_NOPE], rope(q[..., MLA_NOPE:], pos)
    kv = (rmsnorm(kv_a, kv_a_norm) @ w_kv_b).reshape(b, s, MLA_HEADS, MLA_NOPE + MLA_V).transpose(0, 2, 1, 3)
    k_nope, v = kv[..., :MLA_NOPE], kv[..., MLA_NOPE:]
    k_pe = jnp.broadcast_to(rope(k_rope, pos)[:, None], (b, MLA_HEADS, s, MLA_ROPE))
    qf = jnp.concatenate([q_nope, q_pe], axis=-1)
    kf = jnp.concatenate([k_nope, k_pe], axis=-1)
    o = causal_block_attention(qf, kf, v, 1.0 / math.sqrt(MLA_NOPE + MLA_ROPE))
    return o.transpose(0, 2, 1, 3).reshape(b, s, MLA_HEADS * MLA_V)


def pool_mixer(u, pool_w, pool_scale):
    b, s, _ = u.shape
    ng = len(POOL_WINDOWS)
    ug = u.reshape(b, s, ng, POOL_GROUP)
    cs = jnp.cumsum(ug.astype(jnp.float32), axis=1)
    count = jnp.arange(1, s + 1, dtype=jnp.float32)
    means = []
    for g, w in enumerate(POOL_WINDOWS):
        c = cs[:, :, g]
        prev = jnp.pad(c[:, : s - w], ((0, 0), (w, 0), (0, 0)))
        means.append((c - prev) / jnp.minimum(count, float(w))[None, :, None])
    pooled = jnp.stack(means, axis=2).astype(u.dtype) - ug
    y = jnp.einsum('bsgc,gcd->bsgd', pooled, pool_w)
    return y.reshape(b, s, POOL_WIDTH) * pool_scale


def fox_mixer(qkv, f_logit, fox_b_f):
    b, s, _ = qkv.shape
    qkv = qkv.reshape(b, s, 3, FOX_HEADS, FOX_HEAD_DIM).transpose(2, 0, 3, 1, 4)
    q, k, v = qkv[0], qkv[1], qkv[2]
    log_f = jax.nn.log_sigmoid((f_logit + fox_b_f).astype(jnp.float32))
    cum = jnp.cumsum(log_f, axis=1).transpose(0, 2, 1)
    o = causal_block_attention(q, k, v, 1.0 / math.sqrt(FOX_HEAD_DIM), cum)
    return o.transpose(0, 2, 1, 3).reshape(b, s, FOX_HEADS * FOX_HEAD_DIM)


def swiglu(h, w_gu, w_down):
    gu = h @ w_gu
    g, u = gu[..., :D_FF], gu[..., D_FF:]
    return (jax.nn.silu(g) * u) @ w_down


def hybrid_mixing(h, w_in, q_a_norm, w_q_b, kv_a_norm, w_kv_b, pool_w, pool_scale, fox_b_f, w_out, pos):
    z = h @ w_in
    o0 = 0
    o1 = o0 + IN_Q_A
    o2 = o1 + IN_KV_A
    o3 = o2 + IN_K_ROPE
    o4 = o3 + IN_POOL
    o5 = o4 + IN_FOX_QKV
    o6 = o5 + IN_FOX_F
    ya = mla_mixer(z[..., o0:o1], z[..., o1:o2], z[..., o2:o3], q_a_norm, w_q_b, kv_a_norm, w_kv_b, pos)
    yb = pool_mixer(z[..., o3:o4], pool_w, pool_scale)
    yc = fox_mixer(z[..., o4:o5], z[..., o5:o6], fox_b_f)
    return jnp.concatenate([ya, yb, yc], axis=-1) @ w_out


def _fwd_setup_inputs(seed: int = 0) -> dict:
    key = jax.random.key(seed)
    ks = jax.random.split(key, 24)
    L, D, F = DEPTH, D_MODEL, D_FF
    f32 = jnp.float32

    def nrm(k, shape, fan_in):
        return jax.random.normal(k, shape, f32) * (fan_in ** -0.5)

    def gain(k, shape):
        return 1.0 + 0.02 * jax.random.normal(k, shape, f32)

    return {
        "x": jax.random.normal(ks[0], (BATCH, SEQ, D), f32),
        "ffn1_norm": gain(ks[1], (L, D)),
        "ffn1_w_gu": nrm(ks[2], (L, D, 2 * F), D),
        "ffn1_w_down": nrm(ks[3], (L, F, D), F),
        "mix_norm": gain(ks[4], (L, D)),
        "w_in": nrm(ks[5], (L, D, N_IN), D),
        "q_a_norm": gain(ks[6], (L, MLA_Q_RANK)),
        "w_q_b": nrm(ks[7], (L, MLA_Q_RANK, MLA_HEADS * (MLA_NOPE + MLA_ROPE)), MLA_Q_RANK),
        "kv_a_norm": gain(ks[8], (L, MLA_KV_RANK)),
        "w_kv_b": nrm(ks[9], (L, MLA_KV_RANK, MLA_HEADS * (MLA_NOPE + MLA_V)), MLA_KV_RANK),
        "pool_w": nrm(ks[10], (L, len(POOL_WINDOWS), POOL_GROUP, POOL_GROUP), POOL_GROUP),
        "pool_scale": gain(ks[11], (L, POOL_WIDTH)),
        "fox_b_f": FOX_GATE_BIAS_INIT + 0.5 * jax.random.normal(ks[12], (L, FOX_HEADS), f32),
        "w_out": nrm(ks[13], (L, D_MIX, D), D_MIX),
        "ffn2_norm": gain(ks[14], (L, D)),
        "ffn2_w_gu": nrm(ks[15], (L, D, 2 * F), D),
        "ffn2_w_down": nrm(ks[16], (L, F, D), F),
        "final_norm": gain(ks[17], (D,)),
    }


def _fwd_reference(x, ffn1_norm, ffn1_w_gu, ffn1_w_down, mix_norm, w_in, q_a_norm, w_q_b, kv_a_norm, w_kv_b,
              pool_w, pool_scale, fox_b_f, w_out, ffn2_norm, ffn2_w_gu, ffn2_w_down, final_norm):
    pos = jnp.arange(x.shape[1], dtype=jnp.int32)
    for l in range(DEPTH):
        x = x + 0.5 * swiglu(rmsnorm(x, ffn1_norm[l]), ffn1_w_gu[l], ffn1_w_down[l])
        x = x + hybrid_mixing(rmsnorm(x, mix_norm[l]), w_in[l], q_a_norm[l], w_q_b[l], kv_a_norm[l], w_kv_b[l],
                              pool_w[l], pool_scale[l], fox_b_f[l], w_out[l], pos)
        x = x + 0.5 * swiglu(rmsnorm(x, ffn2_norm[l]), ffn2_w_gu[l], ffn2_w_down[l])
    return rmsnorm(x, final_norm)


import jax as _jax
import jax.numpy as _jnp

TWIN_FORMAT = 'train_step'
FWD_PARAMS = ['x', 'ffn1_norm', 'ffn1_w_gu', 'ffn1_w_down', 'mix_norm', 'w_in', 'q_a_norm', 'w_q_b', 'kv_a_norm', 'w_kv_b', 'pool_w', 'pool_scale', 'fox_b_f', 'w_out', 'ffn2_norm', 'ffn2_w_gu', 'ffn2_w_down', 'final_norm']
TWIN_WEIGHTS = ['ffn1_norm', 'ffn1_w_gu', 'ffn1_w_down', 'mix_norm', 'w_in', 'q_a_norm', 'w_q_b', 'kv_a_norm', 'w_kv_b', 'pool_w', 'pool_scale', 'fox_b_f', 'w_out', 'ffn2_norm', 'ffn2_w_gu', 'ffn2_w_down', 'final_norm']
TWIN_DIFF_INPUT = 'x'
TWIN_INPUTS = ['x', 'ffn1_norm', 'ffn1_w_gu', 'ffn1_w_down', 'mix_norm', 'w_in', 'q_a_norm', 'w_q_b', 'kv_a_norm', 'w_kv_b', 'pool_w', 'pool_scale', 'fox_b_f', 'w_out', 'ffn2_norm', 'ffn2_w_gu', 'ffn2_w_down', 'final_norm', 'loss_target', 'm_ffn1_norm', 'm_ffn1_w_gu', 'm_ffn1_w_down', 'm_mix_norm', 'm_w_in', 'm_q_a_norm', 'm_w_q_b', 'm_kv_a_norm', 'm_w_kv_b', 'm_pool_w', 'm_pool_scale', 'm_fox_b_f', 'm_w_out', 'm_ffn2_norm', 'm_ffn2_w_gu', 'm_ffn2_w_down', 'm_final_norm', 'v_ffn1_norm', 'v_ffn1_w_gu', 'v_ffn1_w_down', 'v_mix_norm', 'v_w_in', 'v_q_a_norm', 'v_w_q_b', 'v_kv_a_norm', 'v_w_kv_b', 'v_pool_w', 'v_pool_scale', 'v_fox_b_f', 'v_w_out', 'v_ffn2_norm', 'v_ffn2_w_gu', 'v_ffn2_w_down', 'v_final_norm']
TWIN_OUTPUTS = ['loss', 'grad_x', 'grad_ffn1_norm', 'grad_ffn1_w_gu', 'grad_ffn1_w_down', 'grad_mix_norm', 'grad_w_in', 'grad_q_a_norm', 'grad_w_q_b', 'grad_kv_a_norm', 'grad_w_kv_b', 'grad_pool_w', 'grad_pool_scale', 'grad_fox_b_f', 'grad_w_out', 'grad_ffn2_norm', 'grad_ffn2_w_gu', 'grad_ffn2_w_down', 'grad_final_norm', 'delta_ffn1_norm', 'delta_ffn1_w_gu', 'delta_ffn1_w_down', 'delta_mix_norm', 'delta_w_in', 'delta_q_a_norm', 'delta_w_q_b', 'delta_kv_a_norm', 'delta_w_kv_b', 'delta_pool_w', 'delta_pool_scale', 'delta_fox_b_f', 'delta_w_out', 'delta_ffn2_norm', 'delta_ffn2_w_gu', 'delta_ffn2_w_down', 'delta_final_norm', 'new_m_ffn1_norm', 'new_m_ffn1_w_gu', 'new_m_ffn1_w_down', 'new_m_mix_norm', 'new_m_w_in', 'new_m_q_a_norm', 'new_m_w_q_b', 'new_m_kv_a_norm', 'new_m_w_kv_b', 'new_m_pool_w', 'new_m_pool_scale', 'new_m_fox_b_f', 'new_m_w_out', 'new_m_ffn2_norm', 'new_m_ffn2_w_gu', 'new_m_ffn2_w_down', 'new_m_final_norm', 'new_v_ffn1_norm', 'new_v_ffn1_w_gu', 'new_v_ffn1_w_down', 'new_v_mix_norm', 'new_v_w_in', 'new_v_q_a_norm', 'new_v_w_q_b', 'new_v_kv_a_norm', 'new_v_w_kv_b', 'new_v_pool_w', 'new_v_pool_scale', 'new_v_fox_b_f', 'new_v_w_out', 'new_v_ffn2_norm', 'new_v_ffn2_w_gu', 'new_v_ffn2_w_down', 'new_v_final_norm']
TWIN_LEAF_KINDS = {'loss': 'loss', 'grad_x': 'grad_x', 'grad_ffn1_norm': 'grad_w', 'grad_ffn1_w_gu': 'grad_w', 'grad_ffn1_w_down': 'grad_w', 'grad_mix_norm': 'grad_w', 'grad_w_in': 'grad_w', 'grad_q_a_norm': 'grad_w', 'grad_w_q_b': 'grad_w', 'grad_kv_a_norm': 'grad_w', 'grad_w_kv_b': 'grad_w', 'grad_pool_w': 'grad_w', 'grad_pool_scale': 'grad_w', 'grad_fox_b_f': 'grad_w', 'grad_w_out': 'grad_w', 'grad_ffn2_norm': 'grad_w', 'grad_ffn2_w_gu': 'grad_w', 'grad_ffn2_w_down': 'grad_w', 'grad_final_norm': 'grad_w', 'delta_ffn1_norm': 'delta_w', 'delta_ffn1_w_gu': 'delta_w', 'delta_ffn1_w_down': 'delta_w', 'delta_mix_norm': 'delta_w', 'delta_w_in': 'delta_w', 'delta_q_a_norm': 'delta_w', 'delta_w_q_b': 'delta_w', 'delta_kv_a_norm': 'delta_w', 'delta_w_kv_b': 'delta_w', 'delta_pool_w': 'delta_w', 'delta_pool_scale': 'delta_w', 'delta_fox_b_f': 'delta_w', 'delta_w_out': 'delta_w', 'delta_ffn2_norm': 'delta_w', 'delta_ffn2_w_gu': 'delta_w', 'delta_ffn2_w_down': 'delta_w', 'delta_final_norm': 'delta_w', 'new_m_ffn1_norm': 'new_m', 'new_m_ffn1_w_gu': 'new_m', 'new_m_ffn1_w_down': 'new_m', 'new_m_mix_norm': 'new_m', 'new_m_w_in': 'new_m', 'new_m_q_a_norm': 'new_m', 'new_m_w_q_b': 'new_m', 'new_m_kv_a_norm': 'new_m', 'new_m_w_kv_b': 'new_m', 'new_m_pool_w': 'new_m', 'new_m_pool_scale': 'new_m', 'new_m_fox_b_f': 'new_m', 'new_m_w_out': 'new_m', 'new_m_ffn2_norm': 'new_m', 'new_m_ffn2_w_gu': 'new_m', 'new_m_ffn2_w_down': 'new_m', 'new_m_final_norm': 'new_m', 'new_v_ffn1_norm': 'new_v', 'new_v_ffn1_w_gu': 'new_v', 'new_v_ffn1_w_down': 'new_v', 'new_v_mix_norm': 'new_v', 'new_v_w_in': 'new_v', 'new_v_q_a_norm': 'new_v', 'new_v_w_q_b': 'new_v', 'new_v_kv_a_norm': 'new_v', 'new_v_w_kv_b': 'new_v', 'new_v_pool_w': 'new_v', 'new_v_pool_scale': 'new_v', 'new_v_fox_b_f': 'new_v', 'new_v_w_out': 'new_v', 'new_v_ffn2_norm': 'new_v', 'new_v_ffn2_w_gu': 'new_v', 'new_v_ffn2_w_down': 'new_v', 'new_v_final_norm': 'new_v'}


def _forward(args):
    return _fwd_reference(*[args[k] for k in FWD_PARAMS])


def _output_shape():
    def fwd():
        inp = _fwd_setup_inputs(0)
        return _fwd_reference(*[inp[k] for k in FWD_PARAMS])
    out = _jax.eval_shape(fwd)
    return out.shape, out.dtype

N_MICROBATCH = 1
ADAM_LR = 0.001
ADAM_B1 = 0.9
ADAM_B2 = 0.999
ADAM_EPS = 1e-08
ADAM_WD = 0.01
ADAM_STEP = 10
PER_EXAMPLE_BATCH_AXIS = {'x': 0, 'loss_target': 0}
SHARED_INPUTS = []
_WEIGHT_DTYPES = {'ffn1_norm': _jnp.float32, 'ffn1_w_gu': _jnp.float32, 'ffn1_w_down': _jnp.float32, 'mix_norm': _jnp.float32, 'w_in': _jnp.float32, 'q_a_norm': _jnp.float32, 'w_q_b': _jnp.float32, 'kv_a_norm': _jnp.float32, 'w_kv_b': _jnp.float32, 'pool_w': _jnp.float32, 'pool_scale': _jnp.float32, 'fox_b_f': _jnp.float32, 'w_out': _jnp.float32, 'ffn2_norm': _jnp.float32, 'ffn2_w_gu': _jnp.float32, 'ffn2_w_down': _jnp.float32, 'final_norm': _jnp.float32}
MOMENT_SCALE = {'ffn1_norm': 7.860245e-02, 'ffn1_w_gu': 3.202540e-02, 'ffn1_w_down': 5.215353e-02, 'mix_norm': 9.494010e-02, 'w_in': 7.029901e-02, 'q_a_norm': 3.298526e-02, 'w_q_b': 2.301908e-02, 'kv_a_norm': 7.574960e-02, 'w_kv_b': 3.047228e-02, 'pool_w': 1.355593e-01, 'pool_scale': 1.299123e-01, 'fox_b_f': 6.043595e-01, 'w_out': 8.068283e-02, 'ffn2_norm': 6.722552e-02, 'ffn2_w_gu': 2.666178e-02, 'ffn2_w_down': 4.355097e-02, 'final_norm': 3.204565e+01}


def _to_microbatches(a, axis):
    t = _jnp.moveaxis(a, axis, 0)
    t = t.reshape((N_MICROBATCH, t.shape[0] // N_MICROBATCH) + t.shape[1:])
    return _jnp.moveaxis(t, 1, axis + 1)


def setup_inputs(seed: int = 0) -> dict:
    inp = _fwd_setup_inputs(seed)
    key = _jax.random.fold_in(_jax.random.key(seed), 7919)
    shape, _ = _output_shape()
    out = dict(inp)
    out["loss_target"] = _jax.random.normal(_jax.random.fold_in(key, 0), shape, _jnp.float32)
    for i, name in enumerate(TWIN_WEIGHTS):
        w = inp[name].astype(_jnp.float32)
        if MOMENT_SCALE is None:
            s = _jnp.sqrt(_jnp.mean(_jnp.square(w)) + 1e-30)
        else:
            s = MOMENT_SCALE[name]
        km, kv = _jax.random.split(_jax.random.fold_in(key, i + 1))
        out[name] = w
        out["m_" + name] = s * _jax.random.normal(km, w.shape, _jnp.float32)
        out["v_" + name] = (s * s) * _jax.random.uniform(kv, w.shape, _jnp.float32, 0.5, 1.5)
    if N_MICROBATCH > 1:
        for name, axis in PER_EXAMPLE_BATCH_AXIS.items():
            out[name] = _to_microbatches(out[name], axis)
    return {'x': out['x'], 'ffn1_norm': out['ffn1_norm'], 'ffn1_w_gu': out['ffn1_w_gu'], 'ffn1_w_down': out['ffn1_w_down'], 'mix_norm': out['mix_norm'], 'w_in': out['w_in'], 'q_a_norm': out['q_a_norm'], 'w_q_b': out['w_q_b'], 'kv_a_norm': out['kv_a_norm'], 'w_kv_b': out['w_kv_b'], 'pool_w': out['pool_w'], 'pool_scale': out['pool_scale'], 'fox_b_f': out['fox_b_f'], 'w_out': out['w_out'], 'ffn2_norm': out['ffn2_norm'], 'ffn2_w_gu': out['ffn2_w_gu'], 'ffn2_w_down': out['ffn2_w_down'], 'final_norm': out['final_norm'], 'loss_target': out['loss_target'], 'm_ffn1_norm': out['m_ffn1_norm'], 'm_ffn1_w_gu': out['m_ffn1_w_gu'], 'm_ffn1_w_down': out['m_ffn1_w_down'], 'm_mix_norm': out['m_mix_norm'], 'm_w_in': out['m_w_in'], 'm_q_a_norm': out['m_q_a_norm'], 'm_w_q_b': out['m_w_q_b'], 'm_kv_a_norm': out['m_kv_a_norm'], 'm_w_kv_b': out['m_w_kv_b'], 'm_pool_w': out['m_pool_w'], 'm_pool_scale': out['m_pool_scale'], 'm_fox_b_f': out['m_fox_b_f'], 'm_w_out': out['m_w_out'], 'm_ffn2_norm': out['m_ffn2_norm'], 'm_ffn2_w_gu': out['m_ffn2_w_gu'], 'm_ffn2_w_down': out['m_ffn2_w_down'], 'm_final_norm': out['m_final_norm'], 'v_ffn1_norm': out['v_ffn1_norm'], 'v_ffn1_w_gu': out['v_ffn1_w_gu'], 'v_ffn1_w_down': out['v_ffn1_w_down'], 'v_mix_norm': out['v_mix_norm'], 'v_w_in': out['v_w_in'], 'v_q_a_norm': out['v_q_a_norm'], 'v_w_q_b': out['v_w_q_b'], 'v_kv_a_norm': out['v_kv_a_norm'], 'v_w_kv_b': out['v_w_kv_b'], 'v_pool_w': out['v_pool_w'], 'v_pool_scale': out['v_pool_scale'], 'v_fox_b_f': out['v_fox_b_f'], 'v_w_out': out['v_w_out'], 'v_ffn2_norm': out['v_ffn2_norm'], 'v_ffn2_w_gu': out['v_ffn2_w_gu'], 'v_ffn2_w_down': out['v_ffn2_w_down'], 'v_final_norm': out['v_final_norm']}


def _loss(weights, diff, rest, loss_target):
    with _jax.named_scope("forward"):
        args = {**rest, TWIN_DIFF_INPUT: diff, **{k: w.astype(_WEIGHT_DTYPES[k]) for k, w in weights.items()}}
        y = _forward(args)
    with _jax.named_scope("loss_head"):
        err = _jnp.square(y.astype(_jnp.float32) - loss_target)
        return 0.5 * _jnp.sum(_jnp.mean(err, axis=-1)) if err.ndim else 0.5 * err


def _adamw(w, g, m, v):
    m = ADAM_B1 * m + (1.0 - ADAM_B1) * g
    v = ADAM_B2 * v + (1.0 - ADAM_B2) * _jnp.square(g)
    m_hat = m / (1.0 - ADAM_B1 ** ADAM_STEP)
    v_hat = v / (1.0 - ADAM_B2 ** ADAM_STEP)
    delta = -ADAM_LR * (m_hat / (_jnp.sqrt(v_hat) + ADAM_EPS) + ADAM_WD * w)
    return delta, m, v


def reference(x, ffn1_norm, ffn1_w_gu, ffn1_w_down, mix_norm, w_in, q_a_norm, w_q_b, kv_a_norm, w_kv_b, pool_w, pool_scale, fox_b_f, w_out, ffn2_norm, ffn2_w_gu, ffn2_w_down, final_norm, loss_target, m_ffn1_norm, m_ffn1_w_gu, m_ffn1_w_down, m_mix_norm, m_w_in, m_q_a_norm, m_w_q_b, m_kv_a_norm, m_w_kv_b, m_pool_w, m_pool_scale, m_fox_b_f, m_w_out, m_ffn2_norm, m_ffn2_w_gu, m_ffn2_w_down, m_final_norm, v_ffn1_norm, v_ffn1_w_gu, v_ffn1_w_down, v_mix_norm, v_w_in, v_q_a_norm, v_w_q_b, v_kv_a_norm, v_w_kv_b, v_pool_w, v_pool_scale, v_fox_b_f, v_w_out, v_ffn2_norm, v_ffn2_w_gu, v_ffn2_w_down, v_final_norm):
    given = dict(x=x, ffn1_norm=ffn1_norm, ffn1_w_gu=ffn1_w_gu, ffn1_w_down=ffn1_w_down, mix_norm=mix_norm, w_in=w_in, q_a_norm=q_a_norm, w_q_b=w_q_b, kv_a_norm=kv_a_norm, w_kv_b=w_kv_b, pool_w=pool_w, pool_scale=pool_scale, fox_b_f=fox_b_f, w_out=w_out, ffn2_norm=ffn2_norm, ffn2_w_gu=ffn2_w_gu, ffn2_w_down=ffn2_w_down, final_norm=final_norm, loss_target=loss_target, m_ffn1_norm=m_ffn1_norm, m_ffn1_w_gu=m_ffn1_w_gu, m_ffn1_w_down=m_ffn1_w_down, m_mix_norm=m_mix_norm, m_w_in=m_w_in, m_q_a_norm=m_q_a_norm, m_w_q_b=m_w_q_b, m_kv_a_norm=m_kv_a_norm, m_w_kv_b=m_w_kv_b, m_pool_w=m_pool_w, m_pool_scale=m_pool_scale, m_fox_b_f=m_fox_b_f, m_w_out=m_w_out, m_ffn2_norm=m_ffn2_norm, m_ffn2_w_gu=m_ffn2_w_gu, m_ffn2_w_down=m_ffn2_w_down, m_final_norm=m_final_norm, v_ffn1_norm=v_ffn1_norm, v_ffn1_w_gu=v_ffn1_w_gu, v_ffn1_w_down=v_ffn1_w_down, v_mix_norm=v_mix_norm, v_w_in=v_w_in, v_q_a_norm=v_q_a_norm, v_w_q_b=v_w_q_b, v_kv_a_norm=v_kv_a_norm, v_w_kv_b=v_w_kv_b, v_pool_w=v_pool_w, v_pool_scale=v_pool_scale, v_fox_b_f=v_fox_b_f, v_w_out=v_w_out, v_ffn2_norm=v_ffn2_norm, v_ffn2_w_gu=v_ffn2_w_gu, v_ffn2_w_down=v_ffn2_w_down, v_final_norm=v_final_norm)
    weights = {n: given[n] for n in TWIN_WEIGHTS}
    shared = {n: given[n] for n in SHARED_INPUTS}
    per_example = {n: given[n] for n in ['x']}
    grad_fn = _jax.value_and_grad(_loss, argnums=(0, 1))

    def one_microbatch(ex, loss_target):
        ex = dict(ex)
        diff = ex.pop(TWIN_DIFF_INPUT)
        return grad_fn(weights, diff, {**shared, **ex}, loss_target)

    if N_MICROBATCH == 1:
        loss, (grad_w, grad_x) = one_microbatch(per_example, given["loss_target"])
    else:
        def body(carry, xs):
            loss_sum, grad_sum = carry
            l_k, (gw_k, gx_k) = one_microbatch(xs[0], xs[1])
            with _jax.named_scope("update"):
                return (loss_sum + l_k, _jax.tree.map(_jnp.add, grad_sum, gw_k)), gx_k

        init = (_jnp.zeros((), _jnp.float32), _jax.tree.map(_jnp.zeros_like, weights))
        (loss, grad_w), grad_x = _jax.lax.scan(body, init, (per_example, given["loss_target"]))
    with _jax.named_scope("update"):
        delta_w, new_m, new_v = {}, {}, {}
        for n in TWIN_WEIGHTS:
            delta_w[n], new_m[n], new_v[n] = _adamw(weights[n], grad_w[n], given["m_" + n], given["v_" + n])
    return (loss, grad_x, *[grad_w[n] for n in TWIN_WEIGHTS], *[delta_w[n] for n in TWIN_WEIGHTS],
            *[new_m[n] for n in TWIN_WEIGHTS], *[new_v[n] for n in TWIN_WEIGHTS])
```

```python
import functools
import math

import numpy as np
import jax
import jax.numpy as jnp
from jax import lax
from jax.experimental import pallas as pl
from jax.experimental.pallas import tpu as pltpu

F32 = jnp.float32
BF16 = jnp.bfloat16
MESH_ID = pl.DeviceIdType.MESH

N_DEV = 8
EPS = 1e-6
DEPTH = 2

MLA_HEADS = 6
MLA_Q_RANK = 256
MLA_KV_RANK = 128
MLA_NOPE = 64
MLA_ROPE = 32
MLA_V = 64
ROPE_THETA = 10000.0
POOL_WINDOWS = (2, 4, 8, 16)
POOL_GROUP = 64
POOL_WIDTH = 256
FOX_HEADS = 6
FOX_HEAD_DIM = 64
N_IN = 1830

ADAM_LR = 0.001
ADAM_B1 = 0.9
ADAM_B2 = 0.999
ADAM_EPS = 1e-08
ADAM_WD = 0.01
ADAM_STEP = 10

LANES = 128
HEAD_BLOCK = 128
VMEM_LIMIT = 48 * 1024 * 1024
NEG = -1e30

ZA = 768
ZF = 1152
N_PAD = ZA + ZF
TAIL0 = 640
ROPE_LANE0 = 64


def _f_lane(h):
    return 8 * (h // 2) + (h % 2)


def _in_perm():
    perm = -np.ones(N_PAD, np.int32)
    perm[0:256] = np.arange(0, 256)
    perm[256:384] = np.arange(256, 384)
    perm[384:640] = np.arange(416, 672)
    for h in range(FOX_HEADS):
        perm[TAIL0 + _f_lane(h)] = 1824 + h
    perm[TAIL0 + ROPE_LANE0:TAIL0 + ROPE_LANE0 + MLA_ROPE] = np.arange(384, 416)
    perm[ZA:N_PAD] = np.arange(672, 1824)
    inv = np.zeros(N_IN, np.int32)
    for new, old in enumerate(perm):
        if old >= 0:
            inv[old] = new
    return perm, inv


_IN_PERM, _IN_INV = _in_perm()
_F_LANES = np.array([_f_lane(h) for h in range(FOX_HEADS)], np.int32)


def _dot(a, b):
    return jnp.dot(a, b, preferred_element_type=F32)


def _dot_nt(a, b):
    return lax.dot_general(a, b, (((1,), (1,)), ((), ())), preferred_element_type=F32)


def _dot_tn(a, b):
    return lax.dot_general(a, b, (((0,), (0,)), ((), ())), preferred_element_type=F32)


def _rms(x, gam):
    r = lax.rsqrt(jnp.mean(x * x, axis=-1, keepdims=True) + EPS)
    return x * r * gam


def _rms_bwd(dy, x, gam):
    r = lax.rsqrt(jnp.mean(x * x, axis=-1, keepdims=True) + EPS)
    xh = x * r
    dxh = dy * gam
    dx = r * (dxh - xh * jnp.mean(dxh * xh, axis=-1, keepdims=True))
    return dx, jnp.sum(dy * xh, axis=0, keepdims=True)


def _accum_out(ref, first, val):
    @pl.when(first)
    def _():
        ref[...] = val

    @pl.when(jnp.logical_not(first))
    def _():
        ref[...] += val


def _bs(shape, fn):
    return pl.BlockSpec(shape, fn)


def _params(dims):
    return pltpu.CompilerParams(dimension_semantics=dims, vmem_limit_bytes=VMEM_LIMIT)


def _tile(n, t):
    t = min(n, t)
    assert n % t == 0, (n, t)
    return t


def _ffn_fwd(x, gam, wgu, wd, name):
    S, D = x.shape
    _, nj, _, n = wgu.shape
    tm = _tile(S, 512)

    def body(x_ref, gam_ref, wgu_ref, wd_ref, xo_ref, h_ref, gu_ref, hs, acc):
        j = pl.program_id(1)

        @pl.when(j == 0)
        def _():
            hb = _rms(x_ref[...], gam_ref[...]).astype(BF16)
            hs[...] = hb
            h_ref[...] = hb
            acc[...] = jnp.zeros_like(acc)

        h = hs[...]
        g = _dot(h, wgu_ref[0])
        u = _dot(h, wgu_ref[1])
        gu_ref[0] = g.astype(BF16)
        gu_ref[1] = u.astype(BF16)
        a = (g * jax.nn.sigmoid(g) * u).astype(BF16)
        acc[...] += _dot(a, wd_ref[...])

        @pl.when(j == nj - 1)
        def _():
            xo_ref[...] = x_ref[...] + 0.5 * acc[...]

    return pl.pallas_call(
        body, name=name, grid=(S // tm, nj),
        in_specs=[_bs((tm, D), lambda i, j: (i, 0)), _bs((1, D), lambda i, j: (0, 0)),
                  _bs((2, None, D, n), lambda i, j: (0, j, 0, 0)), _bs((None, n, D), lambda i, j: (j, 0, 0))],
        out_specs=[_bs((tm, D), lambda i, j: (i, 0)), _bs((tm, D), lambda i, j: (i, 0)),
                   _bs((2, None, tm, n), lambda i, j: (0, j, i, 0))],
        out_shape=[jax.ShapeDtypeStruct((S, D), F32), jax.ShapeDtypeStruct((S, D), BF16),
                   jax.ShapeDtypeStruct((2, nj, S, n), BF16)],
        scratch_shapes=[pltpu.VMEM((tm, D), BF16), pltpu.VMEM((tm, D), F32)],
        compiler_params=_params(("parallel", "arbitrary")),
    )(x, gam, wgu, wd)


def _ffn_bwd(dy, x, gam, gu, wgu, wd, name):
    S, D = x.shape
    _, nj, _, n = wgu.shape
    tm = _tile(S, 512)

    def body(dy_ref, x_ref, gam_ref, gu_ref, wgu_ref, wd_ref, dx_ref, dgam_ref, dgu_ref, a_ref, dyb, acc):
        i = pl.program_id(0)
        j = pl.program_id(1)

        @pl.when(j == 0)
        def _():
            dyb[...] = dy_ref[...].astype(BF16)
            acc[...] = jnp.zeros_like(acc)

        da = 0.5 * _dot_nt(dyb[...], wd_ref[...])
        g = gu_ref[0].astype(F32)
        u = gu_ref[1].astype(F32)
        sig = jax.nn.sigmoid(g)
        sl = g * sig
        dg = (da * u * (sig * (1.0 + g * (1.0 - sig)))).astype(BF16)
        du = (da * sl).astype(BF16)
        dgu_ref[0] = dg
        dgu_ref[1] = du
        a_ref[...] = (0.5 * sl * u).astype(BF16)
        acc[...] += _dot_nt(dg, wgu_ref[0]) + _dot_nt(du, wgu_ref[1])

        @pl.when(j == nj - 1)
        def _():
            dxn, dgam = _rms_bwd(acc[...], x_ref[...], gam_ref[...])
            dx_ref[...] = dy_ref[...] + dxn
            _accum_out(dgam_ref, i == 0, dgam)

    return pl.pallas_call(
        body, name=name, grid=(S // tm, nj),
        in_specs=[_bs((tm, D), lambda i, j: (i, 0)), _bs((tm, D), lambda i, j: (i, 0)), _bs((1, D), lambda i, j: (0, 0)),
                  _bs((2, None, tm, n), lambda i, j: (0, j, i, 0)),
                  _bs((2, None, D, n), lambda i, j: (0, j, 0, 0)), _bs((None, n, D), lambda i, j: (j, 0, 0))],
        out_specs=[_bs((tm, D), lambda i, j: (i, 0)), _bs((1, D), lambda i, j: (0, 0)),
                   _bs((2, None, tm, n), lambda i, j: (0, j, i, 0)), _bs((None, tm, n), lambda i, j: (j, i, 0))],
        out_shape=[jax.ShapeDtypeStruct((S, D), F32), jax.ShapeDtypeStruct((1, D), F32),
                   jax.ShapeDtypeStruct((2, nj, S, n), BF16), jax.ShapeDtypeStruct((nj, S, n), BF16)],
        scratch_shapes=[pltpu.VMEM((tm, D), BF16), pltpu.VMEM((tm, D), F32)],
        compiler_params=_params(("arbitrary", "arbitrary")),
    )(dy, x, gam, gu, wgu, wd)


def _mm_tn(a, b, nb, a_of, b_of, tm, tn, name):
    _, S, M = a.shape
    N = b.shape[2]
    tm = _tile(M, tm)
    tn = _tile(N, tn)
    ts = _tile(S, 512)
    nk = S // ts

    def body(a_ref, b_ref, o_ref, acc):
        k = pl.program_id(3)

        @pl.when(k == 0)
        def _():
            acc[...] = jnp.zeros_like(acc)

        acc[...] += _dot_tn(a_ref[...].astype(BF16), b_ref[...].astype(BF16))

        @pl.when(k == nk - 1)
        def _():
            o_ref[...] = acc[...].astype(o_ref.dtype)

    return pl.pallas_call(
        body, name=name, grid=(nb, M // tm, N // tn, nk),
        in_specs=[_bs((None, ts, tm), lambda p, i, j, k: (a_of(p), k, i)),
                  _bs((None, ts, tn), lambda p, i, j, k: (b_of(p), k, j))],
        out_specs=_bs((None, tm, tn), lambda p, i, j, k: (p, i, j)),
        out_shape=jax.ShapeDtypeStruct((nb, M, N), BF16),
        scratch_shapes=[pltpu.VMEM((tm, tn), F32)],
        compiler_params=_params(("parallel", "parallel", "parallel", "arbitrary")),
    )(a, b)


def _mix_in_fwd(x, gam, w_in, name):
    S, D = x.shape
    tm = _tile(S, 512)

    def body(x_ref, gam_ref, w_ref, h_ref, za_ref, zf_ref):
        hb = _rms(x_ref[...], gam_ref[...]).astype(BF16)
        h_ref[...] = hb
        za_ref[...] = _dot(hb, w_ref[:, 0:ZA])
        zf_ref[...] = _dot(hb, w_ref[:, ZA:N_PAD]).astype(BF16)

    return pl.pallas_call(
        body, name=name, grid=(S // tm,),
        in_specs=[_bs((tm, D), lambda i: (i, 0)), _bs((1, D), lambda i: (0, 0)), _bs((D, N_PAD), lambda i: (0, 0))],
        out_specs=[_bs((tm, D), lambda i: (i, 0)), _bs((tm, ZA), lambda i: (i, 0)), _bs((tm, ZF), lambda i: (i, 0))],
        out_shape=[jax.ShapeDtypeStruct((S, D), BF16), jax.ShapeDtypeStruct((S, ZA), F32),
                   jax.ShapeDtypeStruct((S, ZF), BF16)],
        compiler_params=_params(("parallel",)),
    )(x, gam, w_in)


def _mix_in_bwd(dy, x, gam, dza_mla, du, dtail_f, dqf, dkf, dvf, w_in, name):
    S, D = x.shape
    tm = _tile(S, 512)

    def body(dy_ref, x_ref, gam_ref, dza_ref, du_ref, dt_ref, dq_ref, dk_ref, dv_ref, w_ref, dx_ref, dgam_ref, dz_ref):
        i = pl.program_id(0)
        dza = dza_ref[...]
        dz = jnp.concatenate([dza[:, 0:384], du_ref[...], dza[:, TAIL0:ZA] + dt_ref[...],
                              dq_ref[...], dk_ref[...], dv_ref[...]], axis=1).astype(BF16)
        dz_ref[...] = dz
        dh = _dot_nt(dz, w_ref[...])
        dxn, dgam = _rms_bwd(dh, x_ref[...], gam_ref[...])
        dx_ref[...] = dy_ref[...] + dxn
        _accum_out(dgam_ref, i == 0, dgam)

    row = lambda i: (i, 0)
    fix = lambda i: (0, 0)
    return pl.pallas_call(
        body, name=name, grid=(S // tm,),
        in_specs=[_bs((tm, D), row), _bs((tm, D), row), _bs((1, D), fix), _bs((tm, ZA), row), _bs((tm, 256), row),
                  _bs((tm, 128), row), _bs((tm, 384), row), _bs((tm, 384), row), _bs((tm, 384), row),
                  _bs((D, N_PAD), fix)],
        out_specs=[_bs((tm, D), row), _bs((1, D), fix), _bs((tm, N_PAD), row)],
        out_shape=[jax.ShapeDtypeStruct((S, D), F32), jax.ShapeDtypeStruct((1, D), F32),
                   jax.ShapeDtypeStruct((S, N_PAD), BF16)],
        compiler_params=_params(("arbitrary",)),
    )(dy, x, gam, dza_mla, du, dtail_f, dqf, dkf, dvf, w_in)


def _rope_tables(S):
    half = MLA_ROPE // 2
    inv_freq = ROPE_THETA ** (-jnp.arange(0, MLA_ROPE, 2, dtype=F32) / MLA_ROPE)
    ang = jnp.arange(S, dtype=jnp.int32).astype(F32)[:, None] * inv_freq[None, :]
    cos, sin = jnp.cos(ang), jnp.sin(ang)
    one = jnp.ones((S, ROPE_LANE0), F32)
    zero = jnp.zeros((S, ROPE_LANE0), F32)
    pad1 = jnp.ones((S, LANES - ROPE_LANE0 - MLA_ROPE), F32)
    pad0 = jnp.zeros((S, LANES - ROPE_LANE0 - MLA_ROPE), F32)
    zh = jnp.zeros((S, half), F32)
    tab_c = jnp.concatenate([one, cos, cos, pad1], axis=1)
    tab_ck = jnp.concatenate([zero, cos, cos, pad0], axis=1)
    tab_s1 = jnp.concatenate([zero, -sin, zh, pad0], axis=1)
    tab_s2 = jnp.concatenate([zero, zh, sin, pad0], axis=1)
    return tab_c, tab_ck, tab_s1, tab_s2


def _rope(x, c, s1, s2):
    return x * c + pltpu.roll(x, LANES - 16, 1) * s1 + pltpu.roll(x, 16, 1) * s2


def _rope_t(dy, c, s1, s2):
    return dy * c + pltpu.roll(dy * s1, 16, 1) + pltpu.roll(dy * s2, LANES - 16, 1)


_MLA_SCALE = 1.0 / math.sqrt(MLA_NOPE + MLA_ROPE)


def _mla_prep(za, gq, gkv, wq, wkv, tabs, name):
    S = za.shape[0]
    tm = _tile(S, 512)
    H = MLA_HEADS

    def body(zq_ref, tail_ref, gq_ref, gkv_ref, wq_ref, wkv_ref, c_ref, ck_ref, s1_ref, s2_ref, qf_ref, kf_ref, v_ref):
        zq = zq_ref[...]
        c, s1, s2 = c_ref[...], s1_ref[...], s2_ref[...]
        qn = _rms(zq[:, 0:256], gq_ref[...]).astype(BF16)
        q = _dot(qn, wq_ref[...])
        for h in range(H):
            blk = _rope(q[:, h * LANES:(h + 1) * LANES], c, s1, s2)
            qf_ref[:, h * LANES:(h + 1) * LANES] = (blk * _MLA_SCALE).astype(BF16)
        kvn = _rms(zq[:, 256:384], gkv_ref[...]).astype(BF16)
        kv = _dot(kvn, wkv_ref[...])
        kt = _rope(tail_ref[...], ck_ref[...], s1, s2)
        for h in range(H):
            kf_ref[:, h * LANES:(h + 1) * LANES] = (kv[:, h * LANES:(h + 1) * LANES] + kt).astype(BF16)
        v_ref[...] = kv[:, H * LANES:].astype(BF16)

    row = lambda i: (i, 0)
    fix = lambda i: (0, 0)
    return pl.pallas_call(
        body, name=name, grid=(S // tm,),
        in_specs=[_bs((tm, 384), row), _bs((tm, 128), lambda i: (i, TAIL0 // 128)), _bs((1, 256), fix), _bs((1, 128), fix),
                  _bs((256, 768), fix), _bs((128, 1152), fix),
                  _bs((tm, 128), row), _bs((tm, 128), row), _bs((tm, 128), row), _bs((tm, 128), row)],
        out_specs=[_bs((tm, 768), row), _bs((tm, 768), row), _bs((tm, 384), row)],
        out_shape=[jax.ShapeDtypeStruct((S, 768), BF16), jax.ShapeDtypeStruct((S, 768), BF16),
                   jax.ShapeDtypeStruct((S, 384), BF16)],
        compiler_params=_params(("parallel",)),
    )(za, za, gq, gkv, wq, wkv, *tabs)


def _mla_prep_bwd(za, gq, gkv, wq, wkv, tabs, dqf, dkf, dvm, name):
    S = za.shape[0]
    tm = _tile(S, 512)
    H = MLA_HEADS

    def body(zq_ref, gq_ref, gkv_ref, wq_ref, wkv_ref, c_ref, ck_ref, s1_ref, s2_ref, dqf_ref, dkf_ref, dvm_ref,
             dza_ref, dwq_ref, dwkv_ref, dgq_ref, dgkv_ref):
        i = pl.program_id(0)
        first = i == 0
        zq = zq_ref[...]
        c, s1, s2 = c_ref[...], s1_ref[...], s2_ref[...]
        lane = lax.broadcasted_iota(jnp.int32, (1, LANES), 1)
        nope = lane < MLA_NOPE
        rope = jnp.logical_and(lane >= ROPE_LANE0, lane < ROPE_LANE0 + MLA_ROPE)

        qa = zq[:, 0:256]
        qn = _rms(qa, gq_ref[...]).astype(BF16)
        dqf = dqf_ref[...]
        dq_pre = jnp.concatenate(
            [_rope_t(dqf[:, h * LANES:(h + 1) * LANES] * _MLA_SCALE, c, s1, s2) for h in range(H)], axis=1).astype(BF16)
        _accum_out(dwq_ref, first, _dot_tn(qn, dq_pre))
        dqa, dgq = _rms_bwd(_dot_nt(dq_pre, wq_ref[...]), qa, gq_ref[...])
        _accum_out(dgq_ref, first, dgq)

        kva = zq[:, 256:384]
        kvn = _rms(kva, gkv_ref[...]).astype(BF16)
        dkf = dkf_ref[...]
        parts = []
        dkt = jnp.zeros((tm, LANES), F32)
        for h in range(H):
            blk = dkf[:, h * LANES:(h + 1) * LANES]
            parts.append(jnp.where(nope, blk, 0.0))
            dkt = dkt + jnp.where(rope, blk, 0.0)
        dkv_pre = jnp.concatenate(parts + [dvm_ref[...]], axis=1).astype(BF16)
        _accum_out(dwkv_ref, first, _dot_tn(kvn, dkv_pre))
        dkva, dgkv = _rms_bwd(_dot_nt(dkv_pre, wkv_ref[...]), kva, gkv_ref[...])
        _accum_out(dgkv_ref, first, dgkv)

        dtail = _rope_t(dkt, ck_ref[...], s1, s2)
        dza_ref[...] = jnp.concatenate([dqa, dkva, jnp.zeros((tm, 256), F32), dtail], axis=1)

    row = lambda i: (i, 0)
    fix = lambda i: (0, 0)
    return pl.pallas_call(
        body, name=name, grid=(S // tm,),
        in_specs=[_bs((tm, 384), row), _bs((1, 256), fix), _bs((1, 128), fix), _bs((256, 768), fix), _bs((128, 1152), fix),
                  _bs((tm, 128), row), _bs((tm, 128), row), _bs((tm, 128), row), _bs((tm, 128), row),
                  _bs((tm, 768), row), _bs((tm, 768), row), _bs((tm, 384), row)],
        out_specs=[_bs((tm, ZA), row), _bs((256, 768), fix), _bs((128, 1152), fix), _bs((1, 256), fix), _bs((1, 128), fix)],
        out_shape=[jax.ShapeDtypeStruct((S, ZA), F32), jax.ShapeDtypeStruct((256, 768), F32),
                   jax.ShapeDtypeStruct((128, 1152), F32), jax.ShapeDtypeStruct((1, 256), F32),
                   jax.ShapeDtypeStruct((1, 128), F32)],
        compiler_params=_params(("arbitrary",)),
    )(za, gq, gkv, wq, wkv, *tabs, dqf, dkf, dvm)


def _head_views(qb, kb, r, dkb, sel):
    if dkb == LANES:
        sl = slice(r * LANES, (r + 1) * LANES)
        return qb[:, sl], kb[:, sl], kb[:, sl]
    return jnp.where(sel, qb, jnp.zeros_like(qb)), kb, jnp.where(sel, kb, jnp.zeros_like(kb))


def _attn_fwd(q_arr, k_arr, v_arr, *, nhp, dkb, qoff, koff, voff, scale, cum, cumT, name):
    S = q_arr.shape[0]
    T = _tile(S, 512)
    nq = S // T
    W = 2 * dkb
    bias = cum is not None

    def body(*refs):
        if bias:
            q_ref, k_ref, v_ref, cq_ref, ck_ref, o_ref, lse_ref, m_s, l_s, acc_s, cq_s = refs
        else:
            q_ref, k_ref, v_ref, o_ref, lse_ref, m_s, l_s, acc_s = refs
        hp, qi, ki = pl.program_id(0), pl.program_id(1), pl.program_id(2)
        lo = lax.broadcasted_iota(jnp.int32, (1, LANES), 1) < 64

        @pl.when(ki == 0)
        def _():
            m_s[...] = jnp.full_like(m_s, NEG)
            l_s[...] = jnp.zeros_like(l_s)
            acc_s[...] = jnp.zeros_like(acc_s)
            if bias:
                li = lax.broadcasted_iota(jnp.int32, (T, LANES), 1)
                for r in range(2):
                    cq_s[r] = jnp.sum(jnp.where(li == 8 * hp + r, cq_ref[...], 0.0), axis=1, keepdims=True)

        @pl.when(ki <= qi)
        def _():
            qb, kb, vb = q_ref[...], k_ref[...], v_ref[...]
            row = qi * T + lax.broadcasted_iota(jnp.int32, (T, T), 0)
            col = ki * T + lax.broadcasted_iota(jnp.int32, (T, T), 1)
            mask = col <= row
            for r in range(2):
                sel = lo if r == 0 else jnp.logical_not(lo)
                q, k, _ = _head_views(qb, kb, r, dkb, sel)
                if scale != 1.0:
                    q = q * jnp.asarray(scale, q.dtype)
                s = _dot_nt(q, k)
                if bias:
                    s = s + (cq_s[r] - ck_ref[r:r + 1, :])
                s = jnp.where(mask, s, NEG)
                m_prev = m_s[r]
                m_new = jnp.maximum(m_prev, jnp.max(s, axis=1, keepdims=True))
                alpha = jnp.exp(m_prev - m_new)
                p = jnp.exp(s - m_new)
                l_s[r] = alpha * l_s[r] + jnp.sum(p, axis=1, keepdims=True)
                m_s[r] = m_new
                pv = _dot(p.astype(BF16), jnp.where(sel, vb, jnp.zeros_like(vb)))
                acc_s[...] = acc_s[...] * jnp.where(sel, alpha, 1.0) + pv

        @pl.when(ki == nq - 1)
        def _():
            inv = jnp.where(lo, 1.0 / l_s[0], 1.0 / l_s[1])
            o_ref[...] = (acc_s[...] * inv).astype(BF16)
            for r in range(2):
                lse_ref[r] = jnp.broadcast_to(m_s[r] + jnp.log(l_s[r]), (T, LANES))

    in_specs = [_bs((T, W), lambda hp, qi, ki: (qi, qoff + hp)),
                _bs((T, W), lambda hp, qi, ki: (jnp.minimum(ki, qi), koff + hp)),
                _bs((T, LANES), lambda hp, qi, ki: (jnp.minimum(ki, qi), voff + hp))]
    args = [q_arr, k_arr, v_arr]
    scratch = [pltpu.VMEM((2, T, 1), F32), pltpu.VMEM((2, T, 1), F32), pltpu.VMEM((T, LANES), F32)]
    if bias:
        in_specs += [_bs((T, LANES), lambda hp, qi, ki: (qi, 0)), _bs((8, T), lambda hp, qi, ki: (hp, jnp.minimum(ki, qi)))]
        args += [cum, cumT]
        scratch += [pltpu.VMEM((2, T, 1), F32)]
    return pl.pallas_call(
        body, name=name, grid=(nhp, nq, nq),
        in_specs=in_specs,
        out_specs=[_bs((T, LANES), lambda hp, qi, ki: (qi, hp)), _bs((2, T, LANES), lambda hp, qi, ki: (hp, qi, 0))],
        out_shape=[jax.ShapeDtypeStruct((S, nhp * LANES), BF16), jax.ShapeDtypeStruct((2 * nhp, S, LANES), F32)],
        scratch_shapes=scratch,
        compiler_params=_params(("parallel", "parallel", "arbitrary")),
    )(*args)


def _attn_bwd(q_arr, k_arr, v_arr, o_arr, do_arr, lse, *, nhp, dkb, qoff, koff, voff, scale, cum, cumT, name):
    S = q_arr.shape[0]
    T = _tile(S, 512)
    nq = S // T
    W = 2 * dkb
    bias = cum is not None

    def body(*refs):
        if bias:
            (q_ref, k_ref, v_ref, o_ref, do_ref, lse_ref, cq_ref, ck_ref,
             dq_ref, dk_ref, dv_ref, dck_ref, dcq_ref, dk_s, dv_s, dck_s) = refs
        else:
            q_ref, k_ref, v_ref, o_ref, do_ref, lse_ref, dq_ref, dk_ref, dv_ref, dk_s, dv_s = refs
        hp, ki, qi = pl.program_id(0), pl.program_id(1), pl.program_id(2)
        lo = lax.broadcasted_iota(jnp.int32, (1, LANES), 1) < 64

        @pl.when(jnp.logical_and(ki == 0, qi == 0))
        def _():
            dq_ref[...] = jnp.zeros_like(dq_ref)

        if bias:
            @pl.when(jnp.logical_and(hp == 0, jnp.logical_and(ki == 0, qi == 0)))
            def _():
                dcq_ref[...] = jnp.zeros_like(dcq_ref)

        @pl.when(qi == 0)
        def _():
            dk_s[...] = jnp.zeros_like(dk_s)
            dv_s[...] = jnp.zeros_like(dv_s)
            if bias:
                dck_s[...] = jnp.zeros_like(dck_s)

        @pl.when(qi >= ki)
        def _():
            qb, kb, vb = q_ref[...], k_ref[...], v_ref[...]
            dob = do_ref[...]
            prod = dob.astype(F32) * o_ref[...].astype(F32)
            row = qi * T + lax.broadcasted_iota(jnp.int32, (T, T), 0)
            col = ki * T + lax.broadcasted_iota(jnp.int32, (T, T), 1)
            mask = col <= row
            rows = pl.ds(pl.multiple_of(qi * T, T), T)
            if bias:
                li = lax.broadcasted_iota(jnp.int32, (T, LANES), 1)
                cqb = cq_ref[...]
            for r in range(2):
                sel = lo if r == 0 else jnp.logical_not(lo)
                q, k, k_dq = _head_views(qb, kb, r, dkb, sel)
                if scale != 1.0:
                    q = q * jnp.asarray(scale, q.dtype)
                s = _dot_nt(q, k)
                if bias:
                    cq = jnp.sum(jnp.where(li == 8 * hp + r, cqb, 0.0), axis=1, keepdims=True)
                    s = s + (cq - ck_ref[r:r + 1, :])
                p = jnp.where(mask, jnp.exp(s - lse_ref[r][:, 0:1]), 0.0)
                dvec = jnp.sum(jnp.where(sel, prod, 0.0), axis=1, keepdims=True)
                do_r = jnp.where(sel, dob, jnp.zeros_like(dob))
                dp = _dot_nt(do_r, vb)
                ds = p * (dp - dvec)
                pb = p.astype(BF16)
                dsb = ds.astype(BF16)
                dv_s[...] += _dot_tn(pb, do_r)
                if dkb == LANES:
                    sl = slice(r * LANES, (r + 1) * LANES)
                    dk_s[:, sl] += _dot_tn(dsb, q)
                    dq_ref[rows, sl] += _dot(dsb, k_dq) * scale
                else:
                    dk_s[...] += _dot_tn(dsb, q)
                    dq_ref[rows, :] += _dot(dsb, k_dq) * scale
                if bias:
                    dck_s[r:r + 1, :] -= jnp.sum(ds, axis=0, keepdims=True)
                    dcq_ref[rows, :] += jnp.where(li == 8 * hp + r, jnp.sum(ds, axis=1, keepdims=True), 0.0)

        @pl.when(qi == nq - 1)
        def _():
            dk_ref[...] = dk_s[...]
            dv_ref[...] = dv_s[...]
            if bias:
                dck_ref[...] = dck_s[...]

    qmap = lambda hp, ki, qi: jnp.maximum(qi, ki)
    in_specs = [_bs((T, W), lambda hp, ki, qi: (qmap(hp, ki, qi), qoff + hp)),
                _bs((T, W), lambda hp, ki, qi: (ki, koff + hp)),
                _bs((T, LANES), lambda hp, ki, qi: (ki, voff + hp)),
                _bs((T, LANES), lambda hp, ki, qi: (qmap(hp, ki, qi), hp)),
                _bs((T, LANES), lambda hp, ki, qi: (qmap(hp, ki, qi), hp)),
                _bs((2, T, LANES), lambda hp, ki, qi: (hp, qmap(hp, ki, qi), 0))]
    args = [q_arr, k_arr, v_arr, o_arr, do_arr, lse]
    out_specs = [_bs((S, W), lambda hp, ki, qi: (0, hp)), _bs((T, W), lambda hp, ki, qi: (ki, hp)),
                 _bs((T, LANES), lambda hp, ki, qi: (ki, hp))]
    out_shape = [jax.ShapeDtypeStruct((S, nhp * W), F32), jax.ShapeDtypeStruct((S, nhp * W), F32),
                 jax.ShapeDtypeStruct((S, nhp * LANES), F32)]
    scratch = [pltpu.VMEM((T, W), F32), pltpu.VMEM((T, LANES), F32)]
    if bias:
        in_specs += [_bs((T, LANES), lambda hp, ki, qi: (qmap(hp, ki, qi), 0)), _bs((8, T), lambda hp, ki, qi: (hp, ki))]
        args += [cum, cumT]
        out_specs += [_bs((None, 8, T), lambda hp, ki, qi: (hp, 0, ki)), _bs((S, LANES), lambda hp, ki, qi: (0, 0))]
        out_shape += [jax.ShapeDtypeStruct((nhp, 8, S), F32), jax.ShapeDtypeStruct((S, LANES), F32)]
        scratch += [pltpu.VMEM((8, T), F32)]
    return pl.pallas_call(
        body, name=name, grid=(nhp, nq, nq),
        in_specs=in_specs, out_specs=out_specs, out_shape=out_shape, scratch_shapes=scratch,
        compiler_params=_params(("arbitrary", "arbitrary", "arbitrary")),
    )(*args)


def _gate_lanes(shape):
    lane = lax.broadcasted_iota(jnp.int32, shape, 1)
    return jnp.logical_and(lane < 8 * (FOX_HEADS // 2), lane % 8 < 2)


def _fox_prep(za, b_row, name):
    S = za.shape[0]
    nrow = 8 * (FOX_HEADS // 2)

    def body(tail_ref, b_ref, cum_ref, cumt_ref):
        x = tail_ref[...] + b_ref[...]
        logf = jnp.minimum(x, 0.0) - jnp.log(1.0 + jnp.exp(-jnp.abs(x)))
        y = jnp.where(_gate_lanes((S, LANES)), logf, 0.0)
        row = lax.broadcasted_iota(jnp.int32, (S, LANES), 0)
        k = 1
        while k < S:
            y = y + jnp.where(row >= k, pltpu.roll(y, k, 0), 0.0)
            k *= 2
        cum_ref[...] = y
        cumt_ref[...] = y.T[0:nrow, :]

    return pl.pallas_call(
        body, name=name, grid=(1,),
        in_specs=[_bs((S, LANES), lambda i: (0, TAIL0 // LANES)), _bs((1, LANES), lambda i: (0, 0))],
        out_specs=[_bs((S, LANES), lambda i: (0, 0)), _bs((nrow, S), lambda i: (0, 0))],
        out_shape=[jax.ShapeDtypeStruct((S, LANES), F32), jax.ShapeDtypeStruct((nrow, S), F32)],
        compiler_params=_params(("arbitrary",)),
    )(za, b_row)


def _fox_prep_bwd(za, b_row, dck, dcq, name):
    S = za.shape[0]
    nrow = 8 * (FOX_HEADS // 2)
    dck2 = dck.reshape(nrow, S)

    def body(tail_ref, b_ref, dck_ref, dcq_ref, dt_ref, db_ref):
        x = tail_ref[...] + b_ref[...]
        d = jnp.concatenate([dck_ref[...], jnp.zeros((LANES - nrow, S), F32)], axis=0).T + dcq_ref[...]
        row = lax.broadcasted_iota(jnp.int32, (S, LANES), 0)
        k = 1
        while k < S:
            d = d + jnp.where(row < S - k, pltpu.roll(d, S - k, 0), 0.0)
            k *= 2
        df = jnp.where(_gate_lanes((S, LANES)), d * jax.nn.sigmoid(-x), 0.0)
        dt_ref[...] = df
        db_ref[...] = jnp.sum(df, axis=0, keepdims=True)

    return pl.pallas_call(
        body, name=name, grid=(1,),
        in_specs=[_bs((S, LANES), lambda i: (0, TAIL0 // LANES)), _bs((1, LANES), lambda i: (0, 0)),
                  _bs((nrow, S), lambda i: (0, 0)), _bs((S, LANES), lambda i: (0, 0))],
        out_specs=[_bs((S, LANES), lambda i: (0, 0)), _bs((1, LANES), lambda i: (0, 0))],
        out_shape=[jax.ShapeDtypeStruct((S, LANES), F32), jax.ShapeDtypeStruct((1, LANES), F32)],
        compiler_params=_params(("arbitrary",)),
    )(za, b_row, dck2, dcq)


def _pool_select(half, lane_lo, vals):
    return jnp.where(lane_lo, jnp.where(half == 0, vals[0], vals[2]), jnp.where(half == 0, vals[1], vals[3]))


def _pool_den(S, half, lane_lo):
    cnt = (lax.broadcasted_iota(jnp.int32, (S, LANES), 0) + 1).astype(F32)
    w = _pool_select(half, lane_lo, [float(x) for x in POOL_WINDOWS])
    return jnp.minimum(cnt, w)


def _pool_fwd(za, wbd, scale, name):
    S = za.shape[0]

    def body(u_ref, w_ref, sc_ref, y_ref, pd_ref):
        half = pl.program_id(0)
        u = u_ref[...]
        row = lax.broadcasted_iota(jnp.int32, (S, LANES), 0)
        lane_lo = lax.broadcasted_iota(jnp.int32, (S, LANES), 1) < POOL_GROUP
        sums = []
        acc = u
        k = 1
        while k < POOL_WINDOWS[-1]:
            acc = acc + jnp.where(row >= k, pltpu.roll(acc, k, 0), 0.0)
            sums.append(acc)
            k *= 2
        pooled = _pool_select(half, lane_lo, sums) / _pool_den(S, half, lane_lo)
        pd = (pooled - u).astype(BF16)
        pd_ref[...] = pd
        y_ref[...] = (_dot(pd, w_ref[...]) * sc_ref[...]).astype(BF16)

    return pl.pallas_call(
        body, name=name, grid=(2,),
        in_specs=[_bs((S, LANES), lambda i: (0, 384 // LANES + i)), _bs((None, LANES, LANES), lambda i: (i, 0, 0)),
                  _bs((1, LANES), lambda i: (0, i))],
        out_specs=[_bs((S, LANES), lambda i: (0, i)), _bs((S, LANES), lambda i: (0, i))],
        out_shape=[jax.ShapeDtypeStruct((S, POOL_WIDTH), BF16), jax.ShapeDtypeStruct((S, POOL_WIDTH), BF16)],
        compiler_params=_params(("parallel",)),
    )(za, wbd, scale)


def _pool_bwd(dyb, pd, wbd, scale, name):
    S = pd.shape[0]

    def body(dy_ref, pd_ref, w_ref, sc_ref, du_ref, dw_ref, dsc_ref):
        half = pl.program_id(0)
        dy = dy_ref[...]
        pd = pd_ref[...]
        w = w_ref[...]
        ypre = _dot(pd, w)
        dsc_ref[...] = jnp.sum(dy * ypre, axis=0, keepdims=True)
        dyp = (dy * sc_ref[...]).astype(BF16)
        dw_ref[...] = _dot_tn(pd, dyp)
        dpd = _dot_nt(dyp, w)
        row = lax.broadcasted_iota(jnp.int32, (S, LANES), 0)
        lane_lo = lax.broadcasted_iota(jnp.int32, (S, LANES), 1) < POOL_GROUP
        acc = dpd / _pool_den(S, half, lane_lo)
        sums = []
        k = 1
        while k < POOL_WINDOWS[-1]:
            acc = acc + jnp.where(row < S - k, pltpu.roll(acc, S - k, 0), 0.0)
            sums.append(acc)
            k *= 2
        du_ref[...] = _pool_select(half, lane_lo, sums) - dpd

    return pl.pallas_call(
        body, name=name, grid=(2,),
        in_specs=[_bs((S, LANES), lambda i: (0, i)), _bs((S, LANES), lambda i: (0, i)),
                  _bs((None, LANES, LANES), lambda i: (i, 0, 0)), _bs((1, LANES), lambda i: (0, i))],
        out_specs=[_bs((S, LANES), lambda i: (0, i)), _bs((None, LANES, LANES), lambda i: (i, 0, 0)),
                   _bs((1, LANES), lambda i: (0, i))],
        out_shape=[jax.ShapeDtypeStruct((S, POOL_WIDTH), F32), jax.ShapeDtypeStruct((2, LANES, LANES), F32),
                   jax.ShapeDtypeStruct((1, POOL_WIDTH), F32)],
        compiler_params=_params(("parallel",)),
    )(dyb, pd, wbd, scale)


def _mix_out_fwd(x, ya, yb, yc, w_out, name):
    S, D = x.shape
    tm = _tile(S, 512)
    K = w_out.shape[0]

    def body(x_ref, ya_ref, yb_ref, yc_ref, w_ref, xo_ref, yc_out):
        ycat = jnp.concatenate([ya_ref[...], yb_ref[...], yc_ref[...]], axis=1)
        yc_out[...] = ycat
        xo_ref[...] = x_ref[...] + _dot(ycat, w_ref[...])

    row = lambda i: (i, 0)
    return pl.pallas_call(
        body, name=name, grid=(S // tm,),
        in_specs=[_bs((tm, D), row), _bs((tm, 384), row), _bs((tm, 256), row), _bs((tm, 384), row),
                  _bs((K, D), lambda i: (0, 0))],
        out_specs=[_bs((tm, D), row), _bs((tm, K), row)],
        out_shape=[jax.ShapeDtypeStruct((S, D), F32), jax.ShapeDtypeStruct((S, K), BF16)],
        compiler_params=_params(("parallel",)),
    )(x, ya, yb, yc, w_out)


def _mix_out_bwd(dy, w_out, name):
    S, D = dy.shape
    tm = _tile(S, 512)
    K = w_out.shape[0]

    def body(dy_ref, w_ref, da_ref, db_ref, dc_ref):
        d = _dot_nt(dy_ref[...].astype(BF16), w_ref[...])
        da_ref[...] = d[:, 0:384].astype(BF16)
        db_ref[...] = d[:, 384:640]
        dc_ref[...] = d[:, 640:1024].astype(BF16)

    row = lambda i: (i, 0)
    return pl.pallas_call(
        body, name=name, grid=(S // tm,),
        in_specs=[_bs((tm, D), row), _bs((K, D), lambda i: (0, 0))],
        out_specs=[_bs((tm, 384), row), _bs((tm, 256), row), _bs((tm, 384), row)],
        out_shape=[jax.ShapeDtypeStruct((S, 384), BF16), jax.ShapeDtypeStruct((S, 256), F32),
                   jax.ShapeDtypeStruct((S, 384), BF16)],
        compiler_params=_params(("parallel",)),
    )(dy, w_out)


def _loss_head(x, gam, target, name):
    S, D = x.shape
    tm = _tile(S, 512)

    def body(x_ref, gam_ref, t_ref, dx_ref, dgam_ref, loss_ref):
        i = pl.program_id(0)
        xv = x_ref[...]
        err = _rms(xv, gam_ref[...]) - t_ref[...]
        part = 0.5 * jnp.sum(jnp.mean(err * err, axis=-1, keepdims=True), axis=0, keepdims=True)
        dxn, dgam = _rms_bwd(err * (1.0 / D), xv, gam_ref[...])
        dx_ref[...] = dxn
        _accum_out(dgam_ref, i == 0, dgam)
        _accum_out(loss_ref, i == 0, jnp.broadcast_to(part, (1, LANES)))

    row = lambda i: (i, 0)
    fix = lambda i: (0, 0)
    return pl.pallas_call(
        body, name=name, grid=(S // tm,),
        in_specs=[_bs((tm, D), row), _bs((1, D), fix), _bs((tm, D), row)],
        out_specs=[_bs((tm, D), row), _bs((1, D), fix), _bs((1, LANES), fix)],
        out_shape=[jax.ShapeDtypeStruct((S, D), F32), jax.ShapeDtypeStruct((1, D), F32),
                   jax.ShapeDtypeStruct((1, LANES), F32)],
        compiler_params=_params(("arbitrary",)),
    )(x, gam, target)


def _adam_math(g, w, m, v):
    m = ADAM_B1 * m + (1.0 - ADAM_B1) * g
    v = ADAM_B2 * v + (1.0 - ADAM_B2) * (g * g)
    m_hat = m / (1.0 - ADAM_B1 ** ADAM_STEP)
    v_hat = v / (1.0 - ADAM_B2 ** ADAM_STEP)
    delta = -ADAM_LR * (m_hat / (jnp.sqrt(v_hat) + ADAM_EPS) + ADAM_WD * w)
    return delta, m, v


def _adam_sum(recv, w, m, v, tr, name):
    L, R, C = w.shape
    Cp = recv.shape[3]
    tr = _tile(R, tr)

    def body(r_ref, w_ref, m_ref, v_ref, g_out, d_out, m_out, v_out):
        g = r_ref[0, :, 0:C].astype(F32)
        for p in range(1, N_DEV):
            g = g + r_ref[p, :, 0:C].astype(F32)
        delta, mn, vn = _adam_math(g, w_ref[...], m_ref[...], v_ref[...])
        g_out[...] = g
        d_out[...] = delta
        m_out[...] = mn
        v_out[...] = vn

    blk = _bs((None, tr, C), lambda l, i: (l, i, 0))
    shp = jax.ShapeDtypeStruct((L, R, C), F32)
    return pl.pallas_call(
        body, name=name, grid=(L, R // tr),
        in_specs=[_bs((None, N_DEV, tr, Cp), lambda l, i: (l, 0, i, 0)), blk, blk, blk],
        out_specs=[blk, blk, blk, blk], out_shape=[shp, shp, shp, shp],
        compiler_params=_params(("parallel", "parallel")),
    )(recv, w, m, v)


def _dev_index(px, py, pc):
    return 4 * px + 2 * py + pc


HBM_SPEC = pl.BlockSpec(memory_space=pl.ANY)


def _all_gather(shards, out_shapes, views, zero_src, zero_views, name):
    n = len(shards)
    nz = len(zero_views)

    def body(*refs):
        ins = refs[:n]
        zsrc = refs[n]
        outs = refs[n + 1:2 * n + 1]
        send_sems, recv_sems, local_sems, zero_sems = refs[2 * n + 1:]
        x, y, c = lax.axis_index("x"), lax.axis_index("y"), lax.axis_index("c")
        me, sibling = (x, y, c), (x, y, 1 - c)
        chips = [(1 - x, y), (x, 1 - y), (1 - x, 1 - y)]

        def rows(a, blk):
            return views[a](outs[a], _dev_index(*blk))

        def copy(a, k, blk, to, src=None):
            return pltpu.make_async_remote_copy(
                src_ref=rows(a, blk) if src is None else src, dst_ref=rows(a, blk),
                send_sem=send_sems.at[a, k], recv_sem=recv_sems.at[a, k], device_id=to, device_id_type=MESH_ID)

        mine = [pltpu.make_async_copy(ins[a], rows(a, me), local_sems.at[a]) for a in range(n)]
        fills = [pltpu.make_async_copy(zsrc, view(outs[a]), zero_sems.at[i]) for i, (a, view) in enumerate(zero_views)]
        for cp in mine + fills:
            cp.start()
        first = []
        for a in range(n):
            first.append(copy(a, 0, me, sibling, src=ins[a]))
            first += [copy(a, 1 + j, me, (*chip, c), src=ins[a]) for j, chip in enumerate(chips)]
        for cp in first:
            cp.start()
        passed = []
        for j, chip in enumerate(chips):
            for a in range(n):
                copy(a, 1 + j, (*chip, c), me).wait_recv()
                cp = copy(a, 4 + j, (*chip, c), sibling)
                cp.start()
                passed.append(cp)
        for a in range(n):
            copy(a, 0, sibling, me).wait_recv()
            for j, chip in enumerate(chips):
                copy(a, 4 + j, (*chip, 1 - c), me).wait_recv()
        for cp in first + passed:
            cp.wait_send()
        for cp in mine + fills:
            cp.wait()

    return pl.pallas_call(
        body, name=name,
        in_specs=[HBM_SPEC] * (n + 1), out_specs=[HBM_SPEC] * n, out_shape=out_shapes,
        scratch_shapes=[pltpu.SemaphoreType.DMA((n, 7)), pltpu.SemaphoreType.DMA((n, 7)),
                        pltpu.SemaphoreType.DMA((n,)), pltpu.SemaphoreType.DMA((max(nz, 1),))],
        compiler_params=pltpu.CompilerParams(has_side_effects=True),
    )(*shards, zero_src)


def _reduce_scatter(srcs, src_views, recv_shapes, recv_of, packs, name):
    n = len(srcs)
    nr = len(recv_shapes)
    masks = [(kx, ky, kc) for kx in (0, 1) for ky in (0, 1) for kc in (0, 1)][1:]

    def body(*refs):
        ins = refs[:n]
        pk = refs[n]
        outs = refs[n + 1:n + 1 + nr]
        pk_out = refs[n + 1 + nr]
        send_sems, recv_sems, local_sems = refs[n + 2 + nr:]
        x, y, c = lax.axis_index("x"), lax.axis_index("y"), lax.axis_index("c")
        my = _dev_index(x, y, c)

        def peer(mask):
            return (1 - x if mask[0] else x, 1 - y if mask[1] else y, 1 - c if mask[2] else c)

        def dst(i, slot):
            if i == n:
                return pk_out.at[slot]
            a, l = recv_of[i]
            return outs[a].at[l, slot]

        def src(i, p):
            return pk if i == n else src_views[i](ins[i], p)

        def copy(i, k, to):
            return pltpu.make_async_remote_copy(
                src_ref=src(i, _dev_index(*to)), dst_ref=dst(i, my),
                send_sem=send_sems.at[i, k], recv_sem=recv_sems.at[i, k], device_id=to, device_id_type=MESH_ID)

        def arrival(i, k, frm):
            slot = dst(i, _dev_index(*frm))
            return pltpu.make_async_remote_copy(
                src_ref=slot, dst_ref=slot, send_sem=send_sems.at[i, k], recv_sem=recv_sems.at[i, k],
                device_id=frm, device_id_type=MESH_ID)

        local = [pltpu.make_async_copy(src(i, my), dst(i, my), local_sems.at[i]) for i in range(n + 1)]
        for cp in local:
            cp.start()
        sends = [copy(i, k, peer(mask)) for k, mask in enumerate(masks) for i in range(n + 1)]
        for cp in sends:
            cp.start()
        for k, mask in enumerate(masks):
            for i in range(n + 1):
                arrival(i, k, peer(mask)).wait_recv()
        for cp in sends:
            cp.wait_send()
        for cp in local:
            cp.wait()

    pk_shape = jax.ShapeDtypeStruct((N_DEV,) + packs.shape, packs.dtype)
    return pl.pallas_call(
        body, name=name,
        in_specs=[HBM_SPEC] * (n + 1), out_specs=[HBM_SPEC] * (nr + 1), out_shape=list(recv_shapes) + [pk_shape],
        scratch_shapes=[pltpu.SemaphoreType.DMA((n + 1, 7)), pltpu.SemaphoreType.DMA((n + 1, 7)),
                        pltpu.SemaphoreType.DMA((n + 1,))],
        compiler_params=pltpu.CompilerParams(has_side_effects=True),
    )(*srcs, packs)


def _pad_w_in(w):
    take = jnp.take(w, np.maximum(_IN_PERM, 0), axis=-1)
    return jnp.where(_IN_PERM >= 0, take, jnp.zeros_like(take))


def _unpad_w_in(g):
    return jnp.take(g, _IN_INV, axis=-1)


def _small_pack(w_q_b, w_kv_b):
    a = jnp.pad(w_q_b, ((0, 0), (0, 0), (0, LANES - w_q_b.shape[2])))
    b = jnp.pad(w_kv_b, ((0, 0), (0, 0), (0, LANES - w_kv_b.shape[2])))
    return jnp.concatenate([a, b], axis=1)


def _small_unpack(p, cq, ckv):
    return p[:, 0:MLA_Q_RANK, 0:cq], p[:, MLA_Q_RANK:, 0:ckv]


def _mla_weights(wsm):
    H = MLA_HEADS
    cq = H * (MLA_NOPE + MLA_ROPE) // N_DEV
    ckv = H * (MLA_NOPE + MLA_V) // N_DEV
    wq = wsm[:, 0:MLA_Q_RANK, 0:cq].transpose(1, 0, 2).reshape(MLA_Q_RANK, H, MLA_NOPE + MLA_ROPE)
    wq = jnp.pad(wq, ((0, 0), (0, 0), (0, HEAD_BLOCK - MLA_NOPE - MLA_ROPE))).reshape(MLA_Q_RANK, H * HEAD_BLOCK)
    wkv = wsm[:, MLA_Q_RANK:, 0:ckv].transpose(1, 0, 2).reshape(MLA_KV_RANK, H, MLA_NOPE + MLA_V)
    wk = jnp.pad(wkv[:, :, 0:MLA_NOPE], ((0, 0), (0, 0), (0, HEAD_BLOCK - MLA_NOPE))).reshape(MLA_KV_RANK, H * HEAD_BLOCK)
    wv = wkv[:, :, MLA_NOPE:].reshape(MLA_KV_RANK, H * MLA_V)
    return wq, jnp.concatenate([wk, wv], axis=1)


def _mla_grads_to_blocks(dwq, dwkv):
    H = MLA_HEADS
    gq = dwq.reshape(MLA_Q_RANK, H, HEAD_BLOCK)[:, :, 0:MLA_NOPE + MLA_ROPE].reshape(MLA_Q_RANK, N_DEV, -1)
    gk = dwkv[:, 0:H * HEAD_BLOCK].reshape(MLA_KV_RANK, H, HEAD_BLOCK)[:, :, 0:MLA_NOPE]
    gv = dwkv[:, H * HEAD_BLOCK:].reshape(MLA_KV_RANK, H, MLA_V)
    gkv = jnp.concatenate([gk, gv], axis=2).reshape(MLA_KV_RANK, N_DEV, -1)
    return _small_pack(gq.transpose(1, 0, 2), gkv.transpose(1, 0, 2)).astype(BF16)


def _pool_blockdiag(pool_w):
    z = jnp.zeros((POOL_GROUP, POOL_GROUP), pool_w.dtype)
    halves = [jnp.concatenate([jnp.concatenate([pool_w[2 * i], z], axis=1),
                               jnp.concatenate([z, pool_w[2 * i + 1]], axis=1)], axis=0) for i in range(2)]
    return jnp.stack(halves)


def _pool_blockdiag_t(dw):
    g = POOL_GROUP
    return jnp.stack([dw[0, 0:g, 0:g], dw[0, g:, g:], dw[1, 0:g, 0:g], dw[1, g:, g:]])


def _gate_row(b):
    return jnp.zeros((LANES,), b.dtype).at[_F_LANES].set(b).reshape(1, LANES)


_SMALL = ("ffn1_norm", "mix_norm", "q_a_norm", "kv_a_norm", "pool_w", "pool_scale", "fox_b_f", "ffn2_norm", "final_norm")


def _pack_small(tree):
    rows, recipe = [], []
    for name in _SMALL:
        a = tree[name]
        flat = a.reshape(-1)
        n = flat.shape[0]
        nrow = -(-n // LANES)
        flat = jnp.pad(flat, (0, nrow * LANES - n))
        rows.append(flat.reshape(nrow, LANES))
        recipe.append((name, a.shape, n, nrow))
    return jnp.concatenate(rows, axis=0), recipe


def _unpack_small(packed, recipe):
    out, r0 = {}, 0
    for name, shape, n, nrow in recipe:
        out[name] = packed[r0:r0 + nrow].reshape(-1)[0:n].reshape(shape)
        r0 += nrow
    return out


def _adam_small(packs, w, m, v, name):
    R = w.shape[0]

    def body(p_ref, w_ref, m_ref, v_ref, g_out, d_out, m_out, v_out):
        g = p_ref[0]
        for p in range(1, N_DEV):
            g = g + p_ref[p]
        delta, mn, vn = _adam_math(g, w_ref[...], m_ref[...], v_ref[...])
        g_out[...] = g
        d_out[...] = delta
        m_out[...] = mn
        v_out[...] = vn

    blk = _bs((R, LANES), lambda i: (0, 0))
    shp = jax.ShapeDtypeStruct((R, LANES), F32)
    return pl.pallas_call(
        body, name=name, grid=(1,),
        in_specs=[_bs((N_DEV, R, LANES), lambda i: (0, 0, 0)), blk, blk, blk],
        out_specs=[blk, blk, blk, blk], out_shape=[shp, shp, shp, shp],
        compiler_params=_params(("arbitrary",)),
    )(packs, w, m, v)


def _local_step(x, target, gw, small):
    S, D = x.shape
    tabs = _rope_tables(S)
    nhp_a, nhp_c = MLA_HEADS // 2, FOX_HEADS // 2
    fox_scale = 1.0 / math.sqrt(FOX_HEAD_DIM)
    saved = []
    for l in range(DEPTH):
        s = {}
        s["x0"] = x
        wgu1, wd1 = gw["ffn1_w_gu"][l], gw["ffn1_w_down"][l]
        x1, s["h1"], s["gu1"] = _ffn_fwd(x, small["ffn1_norm"][l][None], wgu1, wd1, f"ffn1_fwd_l{l}")
        s["x1"] = x1
        w_in = gw["w_in"][l]
        s["h2"], za, zf = _mix_in_fwd(x1, small["mix_norm"][l][None], w_in, f"mix_in_fwd_l{l}")
        s["za"], s["zf"] = za, zf
        wq, wkv = _mla_weights(gw["w_small"][l])
        s["wq"], s["wkv"] = wq, wkv
        gq, gkv = small["q_a_norm"][l][None], small["kv_a_norm"][l][None]
        qf, kf, vm = _mla_prep(za, gq, gkv, wq, wkv, tabs, f"mla_prep_l{l}")
        s["qf"], s["kf"], s["vm"] = qf, kf, vm
        ya, lse_a = _attn_fwd(qf, kf, vm, nhp=nhp_a, dkb=LANES, qoff=0, koff=0, voff=0, scale=1.0,
                              cum=None, cumT=None, name=f"mla_attn_fwd_l{l}")
        s["ya"], s["lse_a"] = ya, lse_a
        b_row = _gate_row(small["fox_b_f"][l])
        s["b_row"] = b_row
        cum, cumT = _fox_prep(za, b_row, f"fox_prep_l{l}")
        s["cum"], s["cumT"] = cum, cumT
        yc, lse_c = _attn_fwd(zf, zf, zf, nhp=nhp_c, dkb=64, qoff=0, koff=nhp_c, voff=2 * nhp_c, scale=fox_scale,
                              cum=cum, cumT=cumT, name=f"fox_attn_fwd_l{l}")
        s["yc"], s["lse_c"] = yc, lse_c
        wbd = _pool_blockdiag(small["pool_w"][l]).astype(BF16)
        s["wbd"] = wbd
        psc = small["pool_scale"][l][None]
        yb, s["pd"] = _pool_fwd(za, wbd, psc, f"pool_fwd_l{l}")
        w_out = gw["w_out"][l]
        x2, s["ycat"] = _mix_out_fwd(x1, ya, yb, yc, w_out, f"mix_out_fwd_l{l}")
        s["x2"] = x2
        wgu2, wd2 = gw["ffn2_w_gu"][l], gw["ffn2_w_down"][l]
        x, s["h3"], s["gu2"] = _ffn_fwd(x2, small["ffn2_norm"][l][None], wgu2, wd2, f"ffn2_fwd_l{l}")
        saved.append(s)

    dx, d_final, loss = _loss_head(x, small["final_norm"][None], target, "loss_head")

    grads = [None] * DEPTH
    for l in reversed(range(DEPTH)):
        s = saved[l]
        g = {}
        wgu2, wd2 = gw["ffn2_w_gu"][l], gw["ffn2_w_down"][l]
        nj = wd2.shape[0]
        dy3 = dx
        dx, g["ffn2_norm"], dgu, ah = _ffn_bwd(dy3, s["x2"], small["ffn2_norm"][l][None], s["gu2"], wgu2, wd2,
                                               f"ffn2_bwd_l{l}")
        g["ffn2_w_gu"] = _mm_tn(s["h3"][None], dgu.reshape((2 * nj,) + dgu.shape[2:]), 2 * nj, lambda p: 0, lambda p: p,
                                1024, 768, f"ffn2_dwgu_l{l}")
        g["ffn2_w_down"] = _mm_tn(ah, dy3[None], nj, lambda p: p, lambda p: 0, 768, 1024, f"ffn2_dwd_l{l}")

        w_out = gw["w_out"][l]
        dya, dyb, dyc = _mix_out_bwd(dx, w_out, f"mix_out_bwd_l{l}")
        g["w_out"] = _mm_tn(s["ycat"][None], dx[None], 1, lambda p: 0, lambda p: 0, 1024, 1024, f"dwout_l{l}")[0]

        dqf, dkf, dvm = _attn_bwd(s["qf"], s["kf"], s["vm"], s["ya"], dya, s["lse_a"], nhp=nhp_a, dkb=LANES,
                                  qoff=0, koff=0, voff=0, scale=1.0, cum=None, cumT=None, name=f"mla_attn_bwd_l{l}")
        zf = s["zf"]
        dqc, dkc, dvc, dck, dcq = _attn_bwd(zf, zf, zf, s["yc"], dyc, s["lse_c"], nhp=nhp_c, dkb=64, qoff=0, koff=nhp_c,
                                       voff=2 * nhp_c, scale=fox_scale, cum=s["cum"], cumT=s["cumT"],
                                       name=f"fox_attn_bwd_l{l}")
        dtail_f, db = _fox_prep_bwd(s["za"], s["b_row"], dck, dcq, f"fox_prep_bwd_l{l}")
        g["fox_b_f"] = db[0, _F_LANES]
        psc = small["pool_scale"][l][None]
        du, dwbd, dpsc = _pool_bwd(dyb, s["pd"], s["wbd"], psc, f"pool_bwd_l{l}")
        g["pool_w"] = _pool_blockdiag_t(dwbd)
        g["pool_scale"] = dpsc[0]
        gq, gkv = small["q_a_norm"][l][None], small["kv_a_norm"][l][None]
        dza, dwq, dwkv, dgq, dgkv = _mla_prep_bwd(s["za"], gq, gkv, s["wq"], s["wkv"], tabs, dqf, dkf, dvm,
                                                   f"mla_prep_bwd_l{l}")
        g["q_a_norm"], g["kv_a_norm"] = dgq[0], dgkv[0]
        g["w_small"] = _mla_grads_to_blocks(dwq, dwkv)
        w_in = gw["w_in"][l]
        dx, g["mix_norm"], dz = _mix_in_bwd(dx, s["x1"], small["mix_norm"][l][None], dza, du, dtail_f, dqc, dkc, dvc,
                                            w_in, f"mix_in_bwd_l{l}")
        g["w_in"] = _unpad_w_in(_mm_tn(s["h2"][None], dz[None], 1, lambda p: 0, lambda p: 0, 1024, 640, f"dwin_l{l}")[0])

        wgu1, wd1 = gw["ffn1_w_gu"][l], gw["ffn1_w_down"][l]
        dy1 = dx
        dx, g["ffn1_norm"], dgu, ah = _ffn_bwd(dy1, s["x0"], small["ffn1_norm"][l][None], s["gu1"], wgu1, wd1,
                                               f"ffn1_bwd_l{l}")
        g["ffn1_w_gu"] = _mm_tn(s["h1"][None], dgu.reshape((2 * nj,) + dgu.shape[2:]), 2 * nj, lambda p: 0, lambda p: p,
                                1024, 768, f"ffn1_dwgu_l{l}")
        g["ffn1_w_down"] = _mm_tn(ah, dy1[None], nj, lambda p: p, lambda p: 0, 768, 1024, f"ffn1_dwd_l{l}")
        for k in ("ffn1_norm", "ffn2_norm", "mix_norm"):
            g[k] = g[k][0]
        grads[l] = g
    return loss, dx, grads, d_final[0]


_BIG = ("ffn1_w_gu", "ffn1_w_down", "w_in", "w_small", "w_out", "ffn2_w_gu", "ffn2_w_down")


def _pad_cols(w, n):
    return jnp.pad(w, ((0, 0),) * (w.ndim - 1) + ((0, n - w.shape[-1]),))


def kernel(x, ffn1_norm, ffn1_w_gu, ffn1_w_down, mix_norm, w_in, q_a_norm, w_q_b, kv_a_norm, w_kv_b, pool_w, pool_scale, fox_b_f, w_out, ffn2_norm, ffn2_w_gu, ffn2_w_down, final_norm, loss_target, m_ffn1_norm, m_ffn1_w_gu, m_ffn1_w_down, m_mix_norm, m_w_in, m_q_a_norm, m_w_q_b, m_kv_a_norm, m_w_kv_b, m_pool_w, m_pool_scale, m_fox_b_f, m_w_out, m_ffn2_norm, m_ffn2_w_gu, m_ffn2_w_down, m_final_norm, v_ffn1_norm, v_ffn1_w_gu, v_ffn1_w_down, v_mix_norm, v_w_in, v_q_a_norm, v_w_q_b, v_kv_a_norm, v_w_kv_b, v_pool_w, v_pool_scale, v_fox_b_f, v_w_out, v_ffn2_norm, v_ffn2_w_gu, v_ffn2_w_down, v_final_norm):
    W = dict(ffn1_norm=ffn1_norm, ffn1_w_gu=ffn1_w_gu, ffn1_w_down=ffn1_w_down, mix_norm=mix_norm, w_in=w_in,
             q_a_norm=q_a_norm, w_q_b=w_q_b, kv_a_norm=kv_a_norm, w_kv_b=w_kv_b, pool_w=pool_w, pool_scale=pool_scale,
             fox_b_f=fox_b_f, w_out=w_out, ffn2_norm=ffn2_norm, ffn2_w_gu=ffn2_w_gu, ffn2_w_down=ffn2_w_down,
             final_norm=final_norm)
    M = dict(ffn1_norm=m_ffn1_norm, ffn1_w_gu=m_ffn1_w_gu, ffn1_w_down=m_ffn1_w_down, mix_norm=m_mix_norm, w_in=m_w_in,
             q_a_norm=m_q_a_norm, w_q_b=m_w_q_b, kv_a_norm=m_kv_a_norm, w_kv_b=m_w_kv_b, pool_w=m_pool_w,
             pool_scale=m_pool_scale, fox_b_f=m_fox_b_f, w_out=m_w_out, ffn2_norm=m_ffn2_norm, ffn2_w_gu=m_ffn2_w_gu,
             ffn2_w_down=m_ffn2_w_down, final_norm=m_final_norm)
    V = dict(ffn1_norm=v_ffn1_norm, ffn1_w_gu=v_ffn1_w_gu, ffn1_w_down=v_ffn1_w_down, mix_norm=v_mix_norm, w_in=v_w_in,
             q_a_norm=v_q_a_norm, w_q_b=v_w_q_b, kv_a_norm=v_kv_a_norm, w_kv_b=v_w_kv_b, pool_w=v_pool_w,
             pool_scale=v_pool_scale, fox_b_f=v_fox_b_f, w_out=v_w_out, ffn2_norm=v_ffn2_norm, ffn2_w_gu=v_ffn2_w_gu,
             ffn2_w_down=v_ffn2_w_down, final_norm=v_final_norm)
    L, D, n_sh = ffn1_w_gu.shape
    f_sh = ffn1_w_down.shape[1]
    assert n_sh == 2 * f_sh and L == DEPTH
    n_pad = -(-n_sh // LANES) * LANES
    nj = N_DEV // 2
    S = x.shape[1]

    shards = [
        _pad_cols(ffn1_w_gu, n_pad).astype(BF16),
        ffn1_w_down.astype(BF16),
        _pad_w_in(w_in).astype(BF16),
        _small_pack(w_q_b, w_kv_b).astype(BF16),
        w_out.astype(BF16),
        _pad_cols(ffn2_w_gu, n_pad).astype(BF16),
        ffn2_w_down.astype(BF16),
    ]
    r_in, r_out = w_in.shape[1], w_out.shape[1]
    out_shapes = [
        jax.ShapeDtypeStruct((L, N_DEV, D, n_pad), BF16),
        jax.ShapeDtypeStruct((L, nj, n_pad, D), BF16),
        jax.ShapeDtypeStruct((L, N_DEV, r_in, N_PAD), BF16),
        jax.ShapeDtypeStruct((L, N_DEV, MLA_Q_RANK + MLA_KV_RANK, LANES), BF16),
        jax.ShapeDtypeStruct((L, N_DEV, r_out, D), BF16),
        jax.ShapeDtypeStruct((L, N_DEV, D, n_pad), BF16),
        jax.ShapeDtypeStruct((L, nj, n_pad, D), BF16),
    ]
    by_dev = lambda ref, p: ref.at[:, p]
    by_half = lambda ref, p: ref.at[:, p // 2, pl.ds(pl.multiple_of((p % 2) * f_sh, 16), f_sh)]
    views = [by_dev, by_half, by_dev, by_dev, by_dev, by_dev, by_half]
    n_zero = n_pad - 2 * f_sh
    zero_src = jnp.zeros((L, max(n_zero, 16), D), BF16)
    zero_views = [(a, (lambda ref, j=j: ref.at[:, j, pl.ds(2 * f_sh, n_zero)])) for a in (1, 6) for j in range(nj)]
    if n_zero == 0:
        zero_views = []
    gathered = _all_gather(shards, out_shapes, views, zero_src, zero_views, "gather_weights")
    gw = dict(zip(_BIG, gathered))
    gw["ffn1_w_gu"] = gw["ffn1_w_gu"].reshape(L, 2, nj, D, n_pad)
    gw["ffn2_w_gu"] = gw["ffn2_w_gu"].reshape(L, 2, nj, D, n_pad)
    gw["w_in"] = gw["w_in"].reshape(L, N_DEV * r_in, N_PAD)
    gw["w_out"] = gw["w_out"].reshape(L, N_DEV * r_out, D)

    small = {k: W[k] for k in _SMALL}
    loss, dx, grads, d_final = _local_step(x[0], loss_target[0], gw, small)

    small_g = {k: jnp.stack([grads[l][k] for l in range(L)]) for k in _SMALL if k != "final_norm"}
    small_g["final_norm"] = d_final
    pack_g, recipe = _pack_small(small_g)
    n_small = pack_g.shape[0]
    loss_row = -(-n_small // 8) * 8
    pack_g = jnp.concatenate([pack_g, jnp.zeros((loss_row - n_small, LANES), F32), jnp.broadcast_to(loss, (8, LANES))],
                             axis=0)

    srcs, src_views, recv_of = [], [], []
    from_dev = lambda ref, p: ref.at[p]
    from_half = lambda ref, p: ref.at[p // 2, pl.ds(pl.multiple_of((p % 2) * f_sh, 16), f_sh)]
    kinds = [("ffn1_w_gu", from_dev), ("ffn1_w_down", from_half), ("w_in", from_dev), ("w_small", from_dev),
             ("w_out", from_dev), ("ffn2_w_gu", from_dev), ("ffn2_w_down", from_half)]
    for a, (k, view) in enumerate(kinds):
        for l in range(L):
            gk = grads[l][k]
            if k in ("w_in", "w_out"):
                gk = gk.reshape((N_DEV, gk.shape[0] // N_DEV) + gk.shape[1:])
            srcs.append(gk)
            src_views.append(view)
            recv_of.append((a, l))
    recv_shapes = [
        jax.ShapeDtypeStruct((L, N_DEV, D, n_pad), BF16),
        jax.ShapeDtypeStruct((L, N_DEV, f_sh, D), BF16),
        jax.ShapeDtypeStruct((L, N_DEV, r_in, N_IN), BF16),
        jax.ShapeDtypeStruct((L, N_DEV, MLA_Q_RANK + MLA_KV_RANK, LANES), BF16),
        jax.ShapeDtypeStruct((L, N_DEV, r_out, D), BF16),
        jax.ShapeDtypeStruct((L, N_DEV, D, n_pad), BF16),
        jax.ShapeDtypeStruct((L, N_DEV, f_sh, D), BF16),
    ]
    *recv, packs = _reduce_scatter(srcs, src_views, recv_shapes, recv_of, pack_g, "scatter_grads")

    out = {}
    sm_w, sm_m, sm_v = (_small_pack(t["w_q_b"], t["w_kv_b"]) for t in (W, M, V))
    big = [("ffn1_w_gu", W["ffn1_w_gu"], M["ffn1_w_gu"], V["ffn1_w_gu"], 256),
           ("ffn1_w_down", W["ffn1_w_down"], M["ffn1_w_down"], V["ffn1_w_down"], 352),
           ("w_in", W["w_in"], M["w_in"], V["w_in"], 128),
           ("w_small", sm_w, sm_m, sm_v, 384),
           ("w_out", W["w_out"], M["w_out"], V["w_out"], 128),
           ("ffn2_w_gu", W["ffn2_w_gu"], M["ffn2_w_gu"], V["ffn2_w_gu"], 256),
           ("ffn2_w_down", W["ffn2_w_down"], M["ffn2_w_down"], V["ffn2_w_down"], 352)]
    for a, (k, w_, m_, v_, tr) in enumerate(big):
        res = _adam_sum(recv[a], w_, m_, v_, tr, f"adam_{k}")
        if k == "w_small":
            cq, ckv = w_q_b.shape[2], w_kv_b.shape[2]
            parts = [_small_unpack(r, cq, ckv) for r in res]
            out["w_q_b"] = [p[0] for p in parts]
            out["w_kv_b"] = [p[1] for p in parts]
        else:
            out[k] = res

    pw, _ = _pack_small({k: W[k] for k in _SMALL})
    pm, _ = _pack_small({k: M[k] for k in _SMALL})
    pv, _ = _pack_small({k: V[k] for k in _SMALL})
    extra = ((0, loss_row + 8 - n_small), (0, 0))
    res = _adam_small(packs, jnp.pad(pw, extra), jnp.pad(pm, extra), jnp.pad(pv, extra), "adam_small")
    loss_total = res[0][loss_row, 0]
    small_out = [_unpack_small(r, recipe) for r in res]
    for k in _SMALL:
        out[k] = [t[k] for t in small_out]

    names = ["ffn1_norm", "ffn1_w_gu", "ffn1_w_down", "mix_norm", "w_in", "q_a_norm", "w_q_b", "kv_a_norm", "w_kv_b",
             "pool_w", "pool_scale", "fox_b_f", "w_out", "ffn2_norm", "ffn2_w_gu", "ffn2_w_down", "final_norm"]
    outs = [loss_total, dx[None]]
    for which in range(4):
        outs += [out[k][which] for k in names]
    return tuple(outs)
```

```python
import functools
import math

import numpy as np
import jax
import jax.numpy as jnp
from jax import lax
from jax.experimental import pallas as pl
from jax.experimental.pallas import tpu as pltpu

F32 = jnp.float32
BF16 = jnp.bfloat16
MESH_ID = pl.DeviceIdType.MESH

N_DEV = 8
EPS = 1e-6
DEPTH = 2

MLA_HEADS = 6
MLA_Q_RANK = 256
MLA_KV_RANK = 128
MLA_NOPE = 64
MLA_ROPE = 32
MLA_V = 64
ROPE_THETA = 10000.0
POOL_WINDOWS = (2, 4, 8, 16)
POOL_GROUP = 64
POOL_WIDTH = 256
FOX_HEADS = 6
FOX_HEAD_DIM = 64
N_IN = 1830

ADAM_LR = 0.001
ADAM_B1 = 0.9
ADAM_B2 = 0.999
ADAM_EPS = 1e-08
ADAM_WD = 0.01
ADAM_STEP = 10

LANES = 128
HEAD_BLOCK = 128
VMEM_LIMIT = 48 * 1024 * 1024
NEG = -1e30

ZA = 768
ZF = 1152
N_PAD = ZA + ZF
TAIL0 = 640
ROPE_LANE0 = 64


def _f_lane(h):
    return 8 * (h // 2) + (h % 2)


def _in_perm():
    perm = -np.ones(N_PAD, np.int32)
    perm[0:256] = np.arange(0, 256)
    perm[256:384] = np.arange(256, 384)
    perm[384:640] = np.arange(416, 672)
    for h in range(FOX_HEADS):
        perm[TAIL0 + _f_lane(h)] = 1824 + h
    perm[TAIL0 + ROPE_LANE0:TAIL0 + ROPE_LANE0 + MLA_ROPE] = np.arange(384, 416)
    perm[ZA:N_PAD] = np.arange(672, 1824)
    inv = np.zeros(N_IN, np.int32)
    for new, old in enumerate(perm):
        if old >= 0:
            inv[old] = new
    return perm, inv


_IN_PERM, _IN_INV = _in_perm()
_F_LANES = np.array([_f_lane(h) for h in range(FOX_HEADS)], np.int32)


def _dot(a, b):
    return jnp.dot(a, b, preferred_element_type=F32)


def _dot_nt(a, b):
    return lax.dot_general(a, b, (((1,), (1,)), ((), ())), preferred_element_type=F32)


def _dot_tn(a, b):
    return lax.dot_general(a, b, (((0,), (0,)), ((), ())), preferred_element_type=F32)


def _rms(x, gam):
    r = lax.rsqrt(jnp.mean(x * x, axis=-1, keepdims=True) + EPS)
    return x * r * gam


def _rms_bwd(dy, x, gam):
    r = lax.rsqrt(jnp.mean(x * x, axis=-1, keepdims=True) + EPS)
    xh = x * r
    dxh = dy * gam
    dx = r * (dxh - xh * jnp.mean(dxh * xh, axis=-1, keepdims=True))
    return dx, jnp.sum(dy * xh, axis=0, keepdims=True)


def _accum_out(ref, first, val):
    @pl.when(first)
    def _():
        ref[...] = val

    @pl.when(jnp.logical_not(first))
    def _():
        ref[...] += val


def _bs(shape, fn):
    return pl.BlockSpec(shape, fn)


def _params(dims):
    return pltpu.CompilerParams(dimension_semantics=dims, vmem_limit_bytes=VMEM_LIMIT)


def _tile(n, t):
    t = min(n, t)
    assert n % t == 0, (n, t)
    return t


def _ffn_fwd(x, gam, wgu, wd, name):
    S, D = x.shape
    _, nj, _, n = wgu.shape
    tm = _tile(S, 512)

    def body(x_ref, gam_ref, wgu_ref, wd_ref, xo_ref, h_ref, gu_ref, hs, acc):
        j = pl.program_id(1)

        @pl.when(j == 0)
        def _():
            hb = _rms(x_ref[...], gam_ref[...]).astype(BF16)
            hs[...] = hb
            h_ref[...] = hb
            acc[...] = jnp.zeros_like(acc)

        h = hs[...]
        g = _dot(h, wgu_ref[0])
        u = _dot(h, wgu_ref[1])
        gu_ref[0] = g.astype(BF16)
        gu_ref[1] = u.astype(BF16)
        a = (g * jax.nn.sigmoid(g) * u).astype(BF16)
        acc[...] += _dot(a, wd_ref[...])

        @pl.when(j == nj - 1)
        def _():
            xo_ref[...] = x_ref[...] + 0.5 * acc[...]

    return pl.pallas_call(
        body, name=name, grid=(S // tm, nj),
        in_specs=[_bs((tm, D), lambda i, j: (i, 0)), _bs((1, D), lambda i, j: (0, 0)),
                  _bs((2, None, D, n), lambda i, j: (0, j, 0, 0)), _bs((None, n, D), lambda i, j: (j, 0, 0))],
        out_specs=[_bs((tm, D), lambda i, j: (i, 0)), _bs((tm, D), lambda i, j: (i, 0)),
                   _bs((2, None, tm, n), lambda i, j: (0, j, i, 0))],
        out_shape=[jax.ShapeDtypeStruct((S, D), F32), jax.ShapeDtypeStruct((S, D), BF16),
                   jax.ShapeDtypeStruct((2, nj, S, n), BF16)],
        scratch_shapes=[pltpu.VMEM((tm, D), BF16), pltpu.VMEM((tm, D), F32)],
        compiler_params=_params(("parallel", "arbitrary")),
    )(x, gam, wgu, wd)


def _ffn_bwd(dy, x, gam, gu, wgu, wd, name):
    S, D = x.shape
    _, nj, _, n = wgu.shape
    tm = _tile(S, 512)

    def body(dy_ref, x_ref, gam_ref, gu_ref, wgu_ref, wd_ref, dx_ref, dgam_ref, dgu_ref, a_ref, dyb, acc):
        i = pl.program_id(0)
        j = pl.program_id(1)

        @pl.when(j == 0)
        def _():
            dyb[...] = dy_ref[...].astype(BF16)
            acc[...] = jnp.zeros_like(acc)

        da = 0.5 * _dot_nt(dyb[...], wd_ref[...])
        g = gu_ref[0].astype(F32)
        u = gu_ref[1].astype(F32)
        sig = jax.nn.sigmoid(g)
        sl = g * sig
        dg = (da * u * (sig * (1.0 + g * (1.0 - sig)))).astype(BF16)
        du = (da * sl).astype(BF16)
        dgu_ref[0] = dg
        dgu_ref[1] = du
        a_ref[...] = (0.5 * sl * u).astype(BF16)
        acc[...] += _dot_nt(dg, wgu_ref[0]) + _dot_nt(du, wgu_ref[1])

        @pl.when(j == nj - 1)
        def _():
            dxn, dgam = _rms_bwd(acc[...], x_ref[...], gam_ref[...])
            dx_ref[...] = dy_ref[...] + dxn
            _accum_out(dgam_ref, i == 0, dgam)

    return pl.pallas_call(
        body, name=name, grid=(S // tm, nj),
        in_specs=[_bs((tm, D), lambda i, j: (i, 0)), _bs((tm, D), lambda i, j: (i, 0)), _bs((1, D), lambda i, j: (0, 0)),
                  _bs((2, None, tm, n), lambda i, j: (0, j, i, 0)),
                  _bs((2, None, D, n), lambda i, j: (0, j, 0, 0)), _bs((None, n, D), lambda i, j: (j, 0, 0))],
        out_specs=[_bs((tm, D), lambda i, j: (i, 0)), _bs((1, D), lambda i, j: (0, 0)),
                   _bs((2, None, tm, n), lambda i, j: (0, j, i, 0)), _bs((None, tm, n), lambda i, j: (j, i, 0))],
        out_shape=[jax.ShapeDtypeStruct((S, D), F32), jax.ShapeDtypeStruct((1, D), F32),
                   jax.ShapeDtypeStruct((2, nj, S, n), BF16), jax.ShapeDtypeStruct((nj, S, n), BF16)],
        scratch_shapes=[pltpu.VMEM((tm, D), BF16), pltpu.VMEM((tm, D), F32)],
        compiler_params=_params(("arbitrary", "arbitrary")),
    )(dy, x, gam, gu, wgu, wd)


def _mm_tn(a, b, nb, a_of, b_of, tm, tn, name):
    _, S, M = a.shape
    N = b.shape[2]
    tm = _tile(M, tm)
    tn = _tile(N, tn)
    ts = _tile(S, 512)
    nk = S // ts

    def body(a_ref, b_ref, o_ref, acc):
        k = pl.program_id(3)

        @pl.when(k == 0)
        def _():
            acc[...] = jnp.zeros_like(acc)

        acc[...] += _dot_tn(a_ref[...].astype(BF16), b_ref[...].astype(BF16))

        @pl.when(k == nk - 1)
        def _():
            o_ref[...] = acc[...].astype(o_ref.dtype)

    return pl.pallas_call(
        body, name=name, grid=(nb, M // tm, N // tn, nk),
        in_specs=[_bs((None, ts, tm), lambda p, i, j, k: (a_of(p), k, i)),
                  _bs((None, ts, tn), lambda p, i, j, k: (b_of(p), k, j))],
        out_specs=_bs((None, tm, tn), lambda p, i, j, k: (p, i, j)),
        out_shape=jax.ShapeDtypeStruct((nb, M, N), BF16),
        scratch_shapes=[pltpu.VMEM((tm, tn), F32)],
        compiler_params=_params(("parallel", "parallel", "parallel", "arbitrary")),
    )(a, b)


def _mix_in_fwd(x, gam, w_in, name):
    S, D = x.shape
    tm = _tile(S, 512)
    nkv = (ZF - 384) // LANES

    def body(x_ref, gam_ref, w_ref, h_ref, za_ref, zf_ref, zt_ref):
        hb = _rms(x_ref[...], gam_ref[...]).astype(BF16)
        h_ref[...] = hb
        za_ref[...] = _dot(hb, w_ref[:, 0:ZA])
        zf = _dot(hb, w_ref[:, ZA:N_PAD])
        zf_ref[...] = zf.astype(BF16)
        for c in range(nkv):
            zt_ref[c * LANES:(c + 1) * LANES, :] = zf[:, 384 + c * LANES:384 + (c + 1) * LANES].T.astype(BF16)

    return pl.pallas_call(
        body, name=name, grid=(S // tm,),
        in_specs=[_bs((tm, D), lambda i: (i, 0)), _bs((1, D), lambda i: (0, 0)), _bs((D, N_PAD), lambda i: (0, 0))],
        out_specs=[_bs((tm, D), lambda i: (i, 0)), _bs((tm, ZA), lambda i: (i, 0)), _bs((tm, ZF), lambda i: (i, 0)),
                   _bs((nkv * LANES, tm), lambda i: (0, i))],
        out_shape=[jax.ShapeDtypeStruct((S, D), BF16), jax.ShapeDtypeStruct((S, ZA), F32),
                   jax.ShapeDtypeStruct((S, ZF), BF16), jax.ShapeDtypeStruct((nkv * LANES, S), BF16)],
        compiler_params=_params(("parallel",)),
    )(x, gam, w_in)


def _mix_in_bwd(dy, x, gam, dza_mla, du, dtail_f, dqf, dkf, dvf, w_in, name):
    S, D = x.shape
    tm = _tile(S, 512)

    def body(dy_ref, x_ref, gam_ref, dza_ref, du_ref, dt_ref, dq_ref, dk_ref, dv_ref, w_ref, dx_ref, dgam_ref, dz_ref):
        i = pl.program_id(0)
        dza = dza_ref[...]
        dz = jnp.concatenate([dza[:, 0:384], du_ref[...], dza[:, TAIL0:ZA] + dt_ref[...],
                              dq_ref[...], dk_ref[...], dv_ref[...]], axis=1).astype(BF16)
        dz_ref[...] = dz
        dh = _dot_nt(dz, w_ref[...])
        dxn, dgam = _rms_bwd(dh, x_ref[...], gam_ref[...])
        dx_ref[...] = dy_ref[...] + dxn
        _accum_out(dgam_ref, i == 0, dgam)

    row = lambda i: (i, 0)
    fix = lambda i: (0, 0)
    return pl.pallas_call(
        body, name=name, grid=(S // tm,),
        in_specs=[_bs((tm, D), row), _bs((tm, D), row), _bs((1, D), fix), _bs((tm, ZA), row), _bs((tm, 256), row),
                  _bs((tm, 128), row), _bs((tm, 384), row), _bs((tm, 384), row), _bs((tm, 384), row),
                  _bs((D, N_PAD), fix)],
        out_specs=[_bs((tm, D), row), _bs((1, D), fix), _bs((tm, N_PAD), row)],
        out_shape=[jax.ShapeDtypeStruct((S, D), F32), jax.ShapeDtypeStruct((1, D), F32),
                   jax.ShapeDtypeStruct((S, N_PAD), BF16)],
        compiler_params=_params(("arbitrary",)),
    )(dy, x, gam, dza_mla, du, dtail_f, dqf, dkf, dvf, w_in)


def _rope_tables(S):
    half = MLA_ROPE // 2
    inv_freq = ROPE_THETA ** (-jnp.arange(0, MLA_ROPE, 2, dtype=F32) / MLA_ROPE)
    ang = jnp.arange(S, dtype=jnp.int32).astype(F32)[:, None] * inv_freq[None, :]
    cos, sin = jnp.cos(ang), jnp.sin(ang)
    one = jnp.ones((S, ROPE_LANE0), F32)
    zero = jnp.zeros((S, ROPE_LANE0), F32)
    pad1 = jnp.ones((S, LANES - ROPE_LANE0 - MLA_ROPE), F32)
    pad0 = jnp.zeros((S, LANES - ROPE_LANE0 - MLA_ROPE), F32)
    zh = jnp.zeros((S, half), F32)
    tab_c = jnp.concatenate([one, cos, cos, pad1], axis=1)
    tab_ck = jnp.concatenate([zero, cos, cos, pad0], axis=1)
    tab_s1 = jnp.concatenate([zero, -sin, zh, pad0], axis=1)
    tab_s2 = jnp.concatenate([zero, zh, sin, pad0], axis=1)
    return tab_c, tab_ck, tab_s1, tab_s2


def _rope(x, c, s1, s2):
    return x * c + pltpu.roll(x, LANES - 16, 1) * s1 + pltpu.roll(x, 16, 1) * s2


def _rope_t(dy, c, s1, s2):
    return dy * c + pltpu.roll(dy * s1, 16, 1) + pltpu.roll(dy * s2, LANES - 16, 1)


_MLA_SCALE = 1.0 / math.sqrt(MLA_NOPE + MLA_ROPE)


def _mla_prep(za, gq, gkv, wq, wkv, tabs, name):
    S = za.shape[0]
    tm = _tile(S, 512)
    H = MLA_HEADS

    def body(zq_ref, tail_ref, gq_ref, gkv_ref, wq_ref, wkv_ref, c_ref, ck_ref, s1_ref, s2_ref,
             qf_ref, kf_ref, v_ref, kft_ref, vt_ref):
        zq = zq_ref[...]
        c, s1, s2 = c_ref[...], s1_ref[...], s2_ref[...]
        qn = _rms(zq[:, 0:256], gq_ref[...]).astype(BF16)
        q = _dot(qn, wq_ref[...])
        for h in range(H):
            blk = _rope(q[:, h * LANES:(h + 1) * LANES], c, s1, s2)
            qf_ref[:, h * LANES:(h + 1) * LANES] = (blk * _MLA_SCALE).astype(BF16)
        kvn = _rms(zq[:, 256:384], gkv_ref[...]).astype(BF16)
        kv = _dot(kvn, wkv_ref[...])
        kt = _rope(tail_ref[...], ck_ref[...], s1, s2)
        for h in range(H):
            sl = slice(h * LANES, (h + 1) * LANES)
            kblk = kv[:, sl] + kt
            kf_ref[:, sl] = kblk.astype(BF16)
            kft_ref[sl, :] = kblk.T.astype(BF16)
        v_ref[...] = kv[:, H * LANES:].astype(BF16)
        for cblk in range(H * MLA_V // LANES):
            sl = slice(cblk * LANES, (cblk + 1) * LANES)
            vt_ref[sl, :] = kv[:, H * LANES + cblk * LANES:H * LANES + (cblk + 1) * LANES].T.astype(BF16)

    row = lambda i: (i, 0)
    col = lambda i: (0, i)
    fix = lambda i: (0, 0)
    return pl.pallas_call(
        body, name=name, grid=(S // tm,),
        in_specs=[_bs((tm, 384), row), _bs((tm, 128), lambda i: (i, TAIL0 // 128)), _bs((1, 256), fix), _bs((1, 128), fix),
                  _bs((256, 768), fix), _bs((128, 1152), fix),
                  _bs((tm, 128), row), _bs((tm, 128), row), _bs((tm, 128), row), _bs((tm, 128), row)],
        out_specs=[_bs((tm, 768), row), _bs((tm, 768), row), _bs((tm, 384), row), _bs((768, tm), col), _bs((384, tm), col)],
        out_shape=[jax.ShapeDtypeStruct((S, 768), BF16), jax.ShapeDtypeStruct((S, 768), BF16),
                   jax.ShapeDtypeStruct((S, 384), BF16), jax.ShapeDtypeStruct((768, S), BF16),
                   jax.ShapeDtypeStruct((384, S), BF16)],
        compiler_params=_params(("parallel",)),
    )(za, za, gq, gkv, wq, wkv, *tabs)


def _mla_prep_bwd(za, gq, gkv, wq, wkv, tabs, dqf, dkf, dvm, name):
    S = za.shape[0]
    tm = _tile(S, 512)
    H = MLA_HEADS

    def body(zq_ref, gq_ref, gkv_ref, wq_ref, wkv_ref, c_ref, ck_ref, s1_ref, s2_ref, dqf_ref, dkf_ref, dvm_ref,
             dza_ref, dwq_ref, dwkv_ref, dgq_ref, dgkv_ref):
        i = pl.program_id(0)
        first = i == 0
        zq = zq_ref[...]
        c, s1, s2 = c_ref[...], s1_ref[...], s2_ref[...]
        lane = lax.broadcasted_iota(jnp.int32, (1, LANES), 1)
        nope = lane < MLA_NOPE
        rope = jnp.logical_and(lane >= ROPE_LANE0, lane < ROPE_LANE0 + MLA_ROPE)

        qa = zq[:, 0:256]
        qn = _rms(qa, gq_ref[...]).astype(BF16)
        dqf = dqf_ref[...]
        dq_pre = jnp.concatenate(
            [_rope_t(dqf[:, h * LANES:(h + 1) * LANES] * _MLA_SCALE, c, s1, s2) for h in range(H)], axis=1).astype(BF16)
        _accum_out(dwq_ref, first, _dot_tn(qn, dq_pre))
        dqa, dgq = _rms_bwd(_dot_nt(dq_pre, wq_ref[...]), qa, gq_ref[...])
        _accum_out(dgq_ref, first, dgq)

        kva = zq[:, 256:384]
        kvn = _rms(kva, gkv_ref[...]).astype(BF16)
        dkf = dkf_ref[...]
        parts = []
        dkt = jnp.zeros((tm, LANES), F32)
        for h in range(H):
            blk = dkf[:, h * LANES:(h + 1) * LANES]
            parts.append(jnp.where(nope, blk, 0.0))
            dkt = dkt + jnp.where(rope, blk, 0.0)
        dkv_pre = jnp.concatenate(parts + [dvm_ref[...]], axis=1).astype(BF16)
        _accum_out(dwkv_ref, first, _dot_tn(kvn, dkv_pre))
        dkva, dgkv = _rms_bwd(_dot_nt(dkv_pre, wkv_ref[...]), kva, gkv_ref[...])
        _accum_out(dgkv_ref, first, dgkv)

        dtail = _rope_t(dkt, ck_ref[...], s1, s2)
        dza_ref[...] = jnp.concatenate([dqa, dkva, jnp.zeros((tm, 256), F32), dtail], axis=1)

    row = lambda i: (i, 0)
    fix = lambda i: (0, 0)
    return pl.pallas_call(
        body, name=name, grid=(S // tm,),
        in_specs=[_bs((tm, 384), row), _bs((1, 256), fix), _bs((1, 128), fix), _bs((256, 768), fix), _bs((128, 1152), fix),
                  _bs((tm, 128), row), _bs((tm, 128), row), _bs((tm, 128), row), _bs((tm, 128), row),
                  _bs((tm, 768), row), _bs((tm, 768), row), _bs((tm, 384), row)],
        out_specs=[_bs((tm, ZA), row), _bs((256, 768), fix), _bs((128, 1152), fix), _bs((1, 256), fix), _bs((1, 128), fix)],
        out_shape=[jax.ShapeDtypeStruct((S, ZA), F32), jax.ShapeDtypeStruct((256, 768), F32),
                   jax.ShapeDtypeStruct((128, 1152), F32), jax.ShapeDtypeStruct((1, 256), F32),
                   jax.ShapeDtypeStruct((1, 128), F32)],
        compiler_params=_params(("arbitrary",)),
    )(za, gq, gkv, wq, wkv, *tabs, dqf, dkf, dvm)


def _head_views(qb, kb, r, dkb, sel):
    if dkb == LANES:
        sl = slice(r * LANES, (r + 1) * LANES)
        return qb[:, sl], kb[:, sl], kb[:, sl]
    return jnp.where(sel, qb, jnp.zeros_like(qb)), kb, jnp.where(sel, kb, jnp.zeros_like(kb))


def _attn_fwd_t(q_arr, k_arr, vt_arr, *, nhp, dkb, qoff, koff, vtoff, scale, cum, cumT, name):
    S = q_arr.shape[0]
    T = _tile(S, 512)
    nq = S // T
    W = 2 * dkb
    bias = cum is not None

    def body(*refs):
        if bias:
            q_ref, k_ref, vt_ref, cq_ref, ck_ref, o_ref, lse_ref, m_s, l_s, acc_s = refs
        else:
            q_ref, k_ref, vt_ref, o_ref, lse_ref, m_s, l_s, acc_s = refs
        hp, qi, ki = pl.program_id(0), pl.program_id(1), pl.program_id(2)
        lo_lane = lax.broadcasted_iota(jnp.int32, (1, LANES), 1) < 64
        lo_row = lax.broadcasted_iota(jnp.int32, (LANES, 1), 0) < 64

        @pl.when(ki == 0)
        def _():
            m_s[...] = jnp.full_like(m_s, NEG)
            l_s[...] = jnp.zeros_like(l_s)
            acc_s[...] = jnp.zeros_like(acc_s)

        def step(masked):
            qb, kb, vtb = q_ref[...], k_ref[...], vt_ref[...]
            if masked:
                mask = lax.broadcasted_iota(jnp.int32, (T, T), 0) <= lax.broadcasted_iota(jnp.int32, (T, T), 1)
            if bias:
                li = lax.broadcasted_iota(jnp.int32, (T, LANES), 1)
                ckb = ck_ref[...]
            for r in range(2):
                sel = lo_lane if r == 0 else jnp.logical_not(lo_lane)
                rsel = lo_row if r == 0 else jnp.logical_not(lo_row)
                q, k, _ = _head_views(qb, kb, r, dkb, sel)
                if scale != 1.0:
                    q = q * jnp.asarray(scale, q.dtype)
                s = _dot_nt(k, q)
                if bias:
                    ck = jnp.sum(jnp.where(li == 8 * hp + r, ckb, 0.0), axis=1, keepdims=True)
                    s = s + (cq_ref[r:r + 1, :] - ck)
                if masked:
                    s = jnp.where(mask, s, NEG)
                m_prev = m_s[r:r + 1, :]
                m_new = jnp.maximum(m_prev, jnp.max(s, axis=0, keepdims=True))
                alpha = jnp.exp(m_prev - m_new)
                p = jnp.exp(s - m_new)
                l_s[r:r + 1, :] = alpha * l_s[r:r + 1, :] + jnp.sum(p, axis=0, keepdims=True)
                m_s[r:r + 1, :] = m_new
                pv = _dot(jnp.where(rsel, vtb, jnp.zeros_like(vtb)), p.astype(BF16))
                acc_s[...] = acc_s[...] * jnp.where(rsel, alpha, 1.0) + pv

        @pl.when(ki < qi)
        def _():
            step(False)

        @pl.when(ki == qi)
        def _():
            step(True)

        @pl.when(ki == nq - 1)
        def _():
            inv = jnp.where(lo_row, 1.0 / l_s[0:1, :], 1.0 / l_s[1:2, :])
            o_ref[...] = (acc_s[...] * inv).T.astype(BF16)
            used = lax.broadcasted_iota(jnp.int32, (8, T), 0) < 2
            lse_ref[...] = jnp.where(used, m_s[...] + jnp.log(jnp.where(used, l_s[...], 1.0)), 0.0)

    kmap = lambda hp, qi, ki: jnp.minimum(ki, qi)
    in_specs = [_bs((T, W), lambda hp, qi, ki: (qi, qoff + hp)),
                _bs((T, W), lambda hp, qi, ki: (kmap(hp, qi, ki), koff + hp)),
                _bs((LANES, T), lambda hp, qi, ki: (vtoff + hp, kmap(hp, qi, ki)))]
    args = [q_arr, k_arr, vt_arr]
    if bias:
        in_specs += [_bs((8, T), lambda hp, qi, ki: (hp, qi)), _bs((T, LANES), lambda hp, qi, ki: (kmap(hp, qi, ki), 0))]
        args += [cumT, cum]
    return pl.pallas_call(
        body, name=name, grid=(nhp, nq, nq),
        in_specs=in_specs,
        out_specs=[_bs((T, LANES), lambda hp, qi, ki: (qi, hp)), _bs((None, 8, T), lambda hp, qi, ki: (hp, 0, qi))],
        out_shape=[jax.ShapeDtypeStruct((S, nhp * LANES), BF16), jax.ShapeDtypeStruct((nhp, 8, S), F32)],
        scratch_shapes=[pltpu.VMEM((8, T), F32), pltpu.VMEM((8, T), F32), pltpu.VMEM((LANES, T), F32)],
        compiler_params=_params(("parallel", "parallel", "arbitrary")),
    )(*args)


def _attn_dvec(o_arr, do_arr, nhp, name):
    S = o_arr.shape[0]
    T = _tile(S, 512)

    def body(o_ref, do_ref, d_ref):
        prod = do_ref[...].astype(F32) * o_ref[...].astype(F32)
        li = lax.broadcasted_iota(jnp.int32, (T, LANES), 1)
        d0 = jnp.sum(jnp.where(li < 64, prod, 0.0), axis=1, keepdims=True)
        d1 = jnp.sum(jnp.where(li >= 64, prod, 0.0), axis=1, keepdims=True)
        d_ref[...] = jnp.where(li == 0, d0, jnp.where(li == 1, d1, 0.0)).T[0:8, :]

    return pl.pallas_call(
        body, name=name, grid=(nhp, S // T),
        in_specs=[_bs((T, LANES), lambda hp, i: (i, hp)), _bs((T, LANES), lambda hp, i: (i, hp))],
        out_specs=_bs((None, 8, T), lambda hp, i: (hp, 0, i)),
        out_shape=jax.ShapeDtypeStruct((nhp, 8, S), F32),
        compiler_params=_params(("parallel", "parallel")),
    )(o_arr, do_arr)


def _attn_bwd_t(q_arr, k_arr, kt_arr, v_arr, do_arr, lse, dvec, *, nhp, dkb, qoff, koff, ktoff, voff, scale, cum, cumT,
                name):
    S = q_arr.shape[0]
    T = _tile(S, 512)
    nq = S // T
    W = 2 * dkb
    bias = cum is not None

    def body(*refs):
        if bias:
            (q_ref, k_ref, kt_ref, v_ref, do_ref, lse_ref, dvec_ref, cq_ref, ck_ref,
             dq_ref, dk_ref, dv_ref, dcq_ref, dck_ref, dqt_s, dk_s, dv_s, dcq_s, dck_s) = refs
        else:
            (q_ref, k_ref, kt_ref, v_ref, do_ref, lse_ref, dvec_ref,
             dq_ref, dk_ref, dv_ref, dqt_s, dk_s, dv_s) = refs
        hp, ki, qi = pl.program_id(0), pl.program_id(1), pl.program_id(2)
        lo_lane = lax.broadcasted_iota(jnp.int32, (1, LANES), 1) < 64
        lo_row = lax.broadcasted_iota(jnp.int32, (LANES, 1), 0) < 64

        @pl.when(jnp.logical_and(ki == 0, qi == 0))
        def _():
            dqt_s[...] = jnp.zeros_like(dqt_s)
            if bias:
                dcq_s[...] = jnp.zeros_like(dcq_s)

        @pl.when(qi == 0)
        def _():
            dk_s[...] = jnp.zeros_like(dk_s)
            dv_s[...] = jnp.zeros_like(dv_s)
            if bias:
                dck_s[...] = jnp.zeros_like(dck_s)

        def step(masked):
            qb, kb, ktb, vb, dob = q_ref[...], k_ref[...], kt_ref[...], v_ref[...], do_ref[...]
            if masked:
                mask = lax.broadcasted_iota(jnp.int32, (T, T), 0) <= lax.broadcasted_iota(jnp.int32, (T, T), 1)
            if bias:
                li = lax.broadcasted_iota(jnp.int32, (T, LANES), 1)
                ckb = ck_ref[...]
            for r in range(2):
                sel = lo_lane if r == 0 else jnp.logical_not(lo_lane)
                rsel = lo_row if r == 0 else jnp.logical_not(lo_row)
                q, k, _ = _head_views(qb, kb, r, dkb, sel)
                if scale != 1.0:
                    q = q * jnp.asarray(scale, q.dtype)
                s = _dot_nt(k, q)
                if bias:
                    ck = jnp.sum(jnp.where(li == 8 * hp + r, ckb, 0.0), axis=1, keepdims=True)
                    s = s + (cq_ref[r:r + 1, :] - ck)
                p = jnp.exp(s - lse_ref[r:r + 1, :])
                if masked:
                    p = jnp.where(mask, p, 0.0)
                do_r = jnp.where(sel, dob, jnp.zeros_like(dob))
                dp = _dot_nt(vb, do_r)
                ds = p * (dp - dvec_ref[r:r + 1, :])
                pb = p.astype(BF16)
                dsb = ds.astype(BF16)
                dv_s[...] += _dot(pb, do_r)
                if dkb == LANES:
                    sl = slice(r * LANES, (r + 1) * LANES)
                    dk_s[:, sl] += _dot(dsb, q)
                    dqt_s[qi, sl, :] += _dot(ktb[sl, :], dsb) * scale
                else:
                    dk_s[...] += _dot(dsb, q)
                    dqt_s[qi] += _dot(jnp.where(rsel, ktb, jnp.zeros_like(ktb)), dsb) * scale
                if bias:
                    dcq_s[qi, r:r + 1, :] += jnp.sum(ds, axis=0, keepdims=True)
                    dck_s[...] -= jnp.where(li == 8 * hp + r, jnp.sum(ds, axis=1, keepdims=True), 0.0)

        @pl.when(qi > ki)
        def _():
            step(False)

        @pl.when(qi == ki)
        def _():
            step(True)

        @pl.when(qi == nq - 1)
        def _():
            dk_ref[...] = dk_s[...]
            dv_ref[...] = dv_s[...]
            if bias:
                dck_ref[...] = dck_s[...]

        @pl.when(jnp.logical_and(ki == nq - 1, qi == nq - 1))
        def _():
            for c in range(nq):
                dq_ref[c * T:(c + 1) * T, :] = dqt_s[c].T
                if bias:
                    dcq_ref[:, c * T:(c + 1) * T] = dcq_s[c]

    qmap = lambda hp, ki, qi: jnp.maximum(qi, ki)
    in_specs = [_bs((T, W), lambda hp, ki, qi: (qmap(hp, ki, qi), qoff + hp)),
                _bs((T, W), lambda hp, ki, qi: (ki, koff + hp)),
                _bs((W, T), lambda hp, ki, qi: (ktoff + hp, ki)),
                _bs((T, LANES), lambda hp, ki, qi: (ki, voff + hp)),
                _bs((T, LANES), lambda hp, ki, qi: (qmap(hp, ki, qi), hp)),
                _bs((None, 8, T), lambda hp, ki, qi: (hp, 0, qmap(hp, ki, qi))),
                _bs((None, 8, T), lambda hp, ki, qi: (hp, 0, qmap(hp, ki, qi)))]
    args = [q_arr, k_arr, kt_arr, v_arr, do_arr, lse, dvec]
    out_specs = [_bs((S, W), lambda hp, ki, qi: (0, hp)), _bs((T, W), lambda hp, ki, qi: (ki, hp)),
                 _bs((T, LANES), lambda hp, ki, qi: (ki, hp))]
    out_shape = [jax.ShapeDtypeStruct((S, nhp * W), F32), jax.ShapeDtypeStruct((S, nhp * W), F32),
                 jax.ShapeDtypeStruct((S, nhp * LANES), F32)]
    scratch = [pltpu.VMEM((nq, W, T), F32), pltpu.VMEM((T, W), F32), pltpu.VMEM((T, LANES), F32)]
    if bias:
        in_specs += [_bs((8, T), lambda hp, ki, qi: (hp, qmap(hp, ki, qi))), _bs((T, LANES), lambda hp, ki, qi: (ki, 0))]
        args += [cumT, cum]
        out_specs += [_bs((None, 8, S), lambda hp, ki, qi: (hp, 0, 0)), _bs((None, T, LANES), lambda hp, ki, qi: (hp, ki, 0))]
        out_shape += [jax.ShapeDtypeStruct((nhp, 8, S), F32), jax.ShapeDtypeStruct((nhp, S, LANES), F32)]
        scratch += [pltpu.VMEM((nq, 8, T), F32), pltpu.VMEM((T, LANES), F32)]
    return pl.pallas_call(
        body, name=name, grid=(nhp, nq, nq),
        in_specs=in_specs, out_specs=out_specs, out_shape=out_shape, scratch_shapes=scratch,
        compiler_params=_params(("arbitrary", "arbitrary", "arbitrary")),
    )(*args)


def _gate_lanes(shape):
    lane = lax.broadcasted_iota(jnp.int32, shape, 1)
    return jnp.logical_and(lane < 8 * (FOX_HEADS // 2), lane % 8 < 2)


def _fox_prep(za, b_row, name):
    S = za.shape[0]
    nrow = 8 * (FOX_HEADS // 2)

    def body(tail_ref, b_ref, cum_ref, cumt_ref):
        x = tail_ref[...] + b_ref[...]
        logf = jnp.minimum(x, 0.0) - jnp.log(1.0 + jnp.exp(-jnp.abs(x)))
        y = jnp.where(_gate_lanes((S, LANES)), logf, 0.0)
        row = lax.broadcasted_iota(jnp.int32, (S, LANES), 0)
        k = 1
        while k < S:
            y = y + jnp.where(row >= k, pltpu.roll(y, k, 0), 0.0)
            k *= 2
        cum_ref[...] = y
        cumt_ref[...] = y.T[0:nrow, :]

    return pl.pallas_call(
        body, name=name, grid=(1,),
        in_specs=[_bs((S, LANES), lambda i: (0, TAIL0 // LANES)), _bs((1, LANES), lambda i: (0, 0))],
        out_specs=[_bs((S, LANES), lambda i: (0, 0)), _bs((nrow, S), lambda i: (0, 0))],
        out_shape=[jax.ShapeDtypeStruct((S, LANES), F32), jax.ShapeDtypeStruct((nrow, S), F32)],
        compiler_params=_params(("arbitrary",)),
    )(za, b_row)


def _fox_prep_bwd(za, b_row, dcq, dck, name):
    S = za.shape[0]
    nhp = FOX_HEADS // 2
    nrow = 8 * nhp
    dcq2 = dcq.reshape(nrow, S)

    def body(tail_ref, b_ref, dcq_ref, dck_ref, dt_ref, db_ref):
        x = tail_ref[...] + b_ref[...]
        d = jnp.concatenate([dcq_ref[...], jnp.zeros((LANES - nrow, S), F32)], axis=0).T
        for hp in range(nhp):
            d = d + dck_ref[hp]
        row = lax.broadcasted_iota(jnp.int32, (S, LANES), 0)
        k = 1
        while k < S:
            d = d + jnp.where(row < S - k, pltpu.roll(d, S - k, 0), 0.0)
            k *= 2
        df = jnp.where(_gate_lanes((S, LANES)), d * jax.nn.sigmoid(-x), 0.0)
        dt_ref[...] = df
        db_ref[...] = jnp.sum(df, axis=0, keepdims=True)

    return pl.pallas_call(
        body, name=name, grid=(1,),
        in_specs=[_bs((S, LANES), lambda i: (0, TAIL0 // LANES)), _bs((1, LANES), lambda i: (0, 0)),
                  _bs((nrow, S), lambda i: (0, 0)), _bs((nhp, S, LANES), lambda i: (0, 0, 0))],
        out_specs=[_bs((S, LANES), lambda i: (0, 0)), _bs((1, LANES), lambda i: (0, 0))],
        out_shape=[jax.ShapeDtypeStruct((S, LANES), F32), jax.ShapeDtypeStruct((1, LANES), F32)],
        compiler_params=_params(("arbitrary",)),
    )(za, b_row, dcq2, dck)


def _pool_select(half, lane_lo, vals):
    return jnp.where(lane_lo, jnp.where(half == 0, vals[0], vals[2]), jnp.where(half == 0, vals[1], vals[3]))


def _pool_den(S, half, lane_lo):
    cnt = (lax.broadcasted_iota(jnp.int32, (S, LANES), 0) + 1).astype(F32)
    w = _pool_select(half, lane_lo, [float(x) for x in POOL_WINDOWS])
    return jnp.minimum(cnt, w)


def _pool_fwd(za, wbd, scale, name):
    S = za.shape[0]

    def body(u_ref, w_ref, sc_ref, y_ref, pd_ref):
        half = pl.program_id(0)
        u = u_ref[...]
        row = lax.broadcasted_iota(jnp.int32, (S, LANES), 0)
        lane_lo = lax.broadcasted_iota(jnp.int32, (S, LANES), 1) < POOL_GROUP
        sums = []
        acc = u
        k = 1
        while k < POOL_WINDOWS[-1]:
            acc = acc + jnp.where(row >= k, pltpu.roll(acc, k, 0), 0.0)
            sums.append(acc)
            k *= 2
        pooled = _pool_select(half, lane_lo, sums) / _pool_den(S, half, lane_lo)
        pd = (pooled - u).astype(BF16)
        pd_ref[...] = pd
        y_ref[...] = (_dot(pd, w_ref[...]) * sc_ref[...]).astype(BF16)

    return pl.pallas_call(
        body, name=name, grid=(2,),
        in_specs=[_bs((S, LANES), lambda i: (0, 384 // LANES + i)), _bs((None, LANES, LANES), lambda i: (i, 0, 0)),
                  _bs((1, LANES), lambda i: (0, i))],
        out_specs=[_bs((S, LANES), lambda i: (0, i)), _bs((S, LANES), lambda i: (0, i))],
        out_shape=[jax.ShapeDtypeStruct((S, POOL_WIDTH), BF16), jax.ShapeDtypeStruct((S, POOL_WIDTH), BF16)],
        compiler_params=_params(("parallel",)),
    )(za, wbd, scale)


def _pool_bwd(dyb, pd, wbd, scale, name):
    S = pd.shape[0]

    def body(dy_ref, pd_ref, w_ref, sc_ref, du_ref, dw_ref, dsc_ref):
        half = pl.program_id(0)
        dy = dy_ref[...]
        pd = pd_ref[...]
        w = w_ref[...]
        ypre = _dot(pd, w)
        dsc_ref[...] = jnp.sum(dy * ypre, axis=0, keepdims=True)
        dyp = (dy * sc_ref[...]).astype(BF16)
        dw_ref[...] = _dot_tn(pd, dyp)
        dpd = _dot_nt(dyp, w)
        row = lax.broadcasted_iota(jnp.int32, (S, LANES), 0)
        lane_lo = lax.broadcasted_iota(jnp.int32, (S, LANES), 1) < POOL_GROUP
        acc = dpd / _pool_den(S, half, lane_lo)
        sums = []
        k = 1
        while k < POOL_WINDOWS[-1]:
            acc = acc + jnp.where(row < S - k, pltpu.roll(acc, S - k, 0), 0.0)
            sums.append(acc)
            k *= 2
        du_ref[...] = _pool_select(half, lane_lo, sums) - dpd

    return pl.pallas_call(
        body, name=name, grid=(2,),
        in_specs=[_bs((S, LANES), lambda i: (0, i)), _bs((S, LANES), lambda i: (0, i)),
                  _bs((None, LANES, LANES), lambda i: (i, 0, 0)), _bs((1, LANES), lambda i: (0, i))],
        out_specs=[_bs((S, LANES), lambda i: (0, i)), _bs((None, LANES, LANES), lambda i: (i, 0, 0)),
                   _bs((1, LANES), lambda i: (0, i))],
        out_shape=[jax.ShapeDtypeStruct((S, POOL_WIDTH), F32), jax.ShapeDtypeStruct((2, LANES, LANES), F32),
                   jax.ShapeDtypeStruct((1, POOL_WIDTH), F32)],
        compiler_params=_params(("parallel",)),
    )(dyb, pd, wbd, scale)


def _mix_out_fwd(x, ya, yb, yc, w_out, name):
    S, D = x.shape
    tm = _tile(S, 512)
    K = w_out.shape[0]

    def body(x_ref, ya_ref, yb_ref, yc_ref, w_ref, xo_ref, yc_out):
        ycat = jnp.concatenate([ya_ref[...], yb_ref[...], yc_ref[...]], axis=1)
        yc_out[...] = ycat
        xo_ref[...] = x_ref[...] + _dot(ycat, w_ref[...])

    row = lambda i: (i, 0)
    return pl.pallas_call(
        body, name=name, grid=(S // tm,),
        in_specs=[_bs((tm, D), row), _bs((tm, 384), row), _bs((tm, 256), row), _bs((tm, 384), row),
                  _bs((K, D), lambda i: (0, 0))],
        out_specs=[_bs((tm, D), row), _bs((tm, K), row)],
        out_shape=[jax.ShapeDtypeStruct((S, D), F32), jax.ShapeDtypeStruct((S, K), BF16)],
        compiler_params=_params(("parallel",)),
    )(x, ya, yb, yc, w_out)


def _mix_out_bwd(dy, w_out, name):
    S, D = dy.shape
    tm = _tile(S, 512)
    K = w_out.shape[0]

    def body(dy_ref, w_ref, da_ref, db_ref, dc_ref):
        d = _dot_nt(dy_ref[...].astype(BF16), w_ref[...])
        da_ref[...] = d[:, 0:384].astype(BF16)
        db_ref[...] = d[:, 384:640]
        dc_ref[...] = d[:, 640:1024].astype(BF16)

    row = lambda i: (i, 0)
    return pl.pallas_call(
        body, name=name, grid=(S // tm,),
        in_specs=[_bs((tm, D), row), _bs((K, D), lambda i: (0, 0))],
        out_specs=[_bs((tm, 384), row), _bs((tm, 256), row), _bs((tm, 384), row)],
        out_shape=[jax.ShapeDtypeStruct((S, 384), BF16), jax.ShapeDtypeStruct((S, 256), F32),
                   jax.ShapeDtypeStruct((S, 384), BF16)],
        compiler_params=_params(("parallel",)),
    )(dy, w_out)


def _loss_head(x, gam, target, name):
    S, D = x.shape
    tm = _tile(S, 512)

    def body(x_ref, gam_ref, t_ref, dx_ref, dgam_ref, loss_ref):
        i = pl.program_id(0)
        xv = x_ref[...]
        err = _rms(xv, gam_ref[...]) - t_ref[...]
        part = 0.5 * jnp.sum(jnp.mean(err * err, axis=-1, keepdims=True), axis=0, keepdims=True)
        dxn, dgam = _rms_bwd(err * (1.0 / D), xv, gam_ref[...])
        dx_ref[...] = dxn
        _accum_out(dgam_ref, i == 0, dgam)
        _accum_out(loss_ref, i == 0, jnp.broadcast_to(part, (1, LANES)))

    row = lambda i: (i, 0)
    fix = lambda i: (0, 0)
    return pl.pallas_call(
        body, name=name, grid=(S // tm,),
        in_specs=[_bs((tm, D), row), _bs((1, D), fix), _bs((tm, D), row)],
        out_specs=[_bs((tm, D), row), _bs((1, D), fix), _bs((1, LANES), fix)],
        out_shape=[jax.ShapeDtypeStruct((S, D), F32), jax.ShapeDtypeStruct((1, D), F32),
                   jax.ShapeDtypeStruct((1, LANES), F32)],
        compiler_params=_params(("arbitrary",)),
    )(x, gam, target)


def _adam_math(g, w, m, v):
    m = ADAM_B1 * m + (1.0 - ADAM_B1) * g
    v = ADAM_B2 * v + (1.0 - ADAM_B2) * (g * g)
    m_hat = m / (1.0 - ADAM_B1 ** ADAM_STEP)
    v_hat = v / (1.0 - ADAM_B2 ** ADAM_STEP)
    delta = -ADAM_LR * (m_hat / (jnp.sqrt(v_hat) + ADAM_EPS) + ADAM_WD * w)
    return delta, m, v


def _adam_sum(recv, w, m, v, tr, name):
    L, R, C = w.shape
    Cp = recv.shape[3]
    tr = _tile(R, tr)

    def body(r_ref, w_ref, m_ref, v_ref, g_out, d_out, m_out, v_out):
        g = r_ref[0, :, 0:C].astype(F32)
        for p in range(1, N_DEV):
            g = g + r_ref[p, :, 0:C].astype(F32)
        delta, mn, vn = _adam_math(g, w_ref[...], m_ref[...], v_ref[...])
        g_out[...] = g
        d_out[...] = delta
        m_out[...] = mn
        v_out[...] = vn

    blk = _bs((None, tr, C), lambda l, i: (l, i, 0))
    shp = jax.ShapeDtypeStruct((L, R, C), F32)
    return pl.pallas_call(
        body, name=name, grid=(L, R // tr),
        in_specs=[_bs((None, N_DEV, tr, Cp), lambda l, i: (l, 0, i, 0)), blk, blk, blk],
        out_specs=[blk, blk, blk, blk], out_shape=[shp, shp, shp, shp],
        compiler_params=_params(("parallel", "parallel")),
    )(recv, w, m, v)


def _dev_index(px, py, pc):
    return 4 * px + 2 * py + pc


HBM_SPEC = pl.BlockSpec(memory_space=pl.ANY)


def _all_gather(shards, out_shapes, views, zero_src, zero_views, name):
    n = len(shards)
    nz = len(zero_views)

    def body(*refs):
        ins = refs[:n]
        zsrc = refs[n]
        outs = refs[n + 1:2 * n + 1]
        send_sems, recv_sems, local_sems, zero_sems = refs[2 * n + 1:]
        x, y, c = lax.axis_index("x"), lax.axis_index("y"), lax.axis_index("c")
        me, sibling = (x, y, c), (x, y, 1 - c)
        chips = [(1 - x, y), (x, 1 - y), (1 - x, 1 - y)]

        def rows(a, blk):
            return views[a](outs[a], _dev_index(*blk))

        def copy(a, k, blk, to, src=None):
            return pltpu.make_async_remote_copy(
                src_ref=rows(a, blk) if src is None else src, dst_ref=rows(a, blk),
                send_sem=send_sems.at[a, k], recv_sem=recv_sems.at[a, k], device_id=to, device_id_type=MESH_ID)

        mine = [pltpu.make_async_copy(ins[a], rows(a, me), local_sems.at[a]) for a in range(n)]
        fills = [pltpu.make_async_copy(zsrc, view(outs[a]), zero_sems.at[i]) for i, (a, view) in enumerate(zero_views)]
        for cp in mine + fills:
            cp.start()
        first = []
        for a in range(n):
            first.append(copy(a, 0, me, sibling, src=ins[a]))
            first += [copy(a, 1 + j, me, (*chip, c), src=ins[a]) for j, chip in enumerate(chips)]
        for cp in first:
            cp.start()
        passed = []
        for j, chip in enumerate(chips):
            for a in range(n):
                copy(a, 1 + j, (*chip, c), me).wait_recv()
                cp = copy(a, 4 + j, (*chip, c), sibling)
                cp.start()
                passed.append(cp)
        for a in range(n):
            copy(a, 0, sibling, me).wait_recv()
            for j, chip in enumerate(chips):
                copy(a, 4 + j, (*chip, 1 - c), me).wait_recv()
        for cp in first + passed:
            cp.wait_send()
        for cp in mine + fills:
            cp.wait()

    return pl.pallas_call(
        body, name=name,
        in_specs=[HBM_SPEC] * (n + 1), out_specs=[HBM_SPEC] * n, out_shape=out_shapes,
        scratch_shapes=[pltpu.SemaphoreType.DMA((n, 7)), pltpu.SemaphoreType.DMA((n, 7)),
                        pltpu.SemaphoreType.DMA((n,)), pltpu.SemaphoreType.DMA((max(nz, 1),))],
        compiler_params=pltpu.CompilerParams(has_side_effects=True),
    )(*shards, zero_src)


def _reduce_scatter(srcs, src_views, recv_shapes, recv_of, packs, name):
    n = len(srcs)
    nr = len(recv_shapes)
    masks = [(kx, ky, kc) for kx in (0, 1) for ky in (0, 1) for kc in (0, 1)][1:]

    def body(*refs):
        ins = refs[:n]
        pk = refs[n]
        outs = refs[n + 1:n + 1 + nr]
        pk_out = refs[n + 1 + nr]
        send_sems, recv_sems, local_sems = refs[n + 2 + nr:]
        x, y, c = lax.axis_index("x"), lax.axis_index("y"), lax.axis_index("c")
        my = _dev_index(x, y, c)

        def peer(mask):
            return (1 - x if mask[0] else x, 1 - y if mask[1] else y, 1 - c if mask[2] else c)

        def dst(i, slot):
            if i == n:
                return pk_out.at[slot]
            a, l = recv_of[i]
            return outs[a].at[l, slot]

        def src(i, p):
            return pk if i == n else src_views[i](ins[i], p)

        def copy(i, k, to):
            return pltpu.make_async_remote_copy(
                src_ref=src(i, _dev_index(*to)), dst_ref=dst(i, my),
                send_sem=send_sems.at[i, k], recv_sem=recv_sems.at[i, k], device_id=to, device_id_type=MESH_ID)

        def arrival(i, k, frm):
            slot = dst(i, _dev_index(*frm))
            return pltpu.make_async_remote_copy(
                src_ref=slot, dst_ref=slot, send_sem=send_sems.at[i, k], recv_sem=recv_sems.at[i, k],
                device_id=frm, device_id_type=MESH_ID)

        local = [pltpu.make_async_copy(src(i, my), dst(i, my), local_sems.at[i]) for i in range(n + 1)]
        for cp in local:
            cp.start()
        sends = [copy(i, k, peer(mask)) for k, mask in enumerate(masks) for i in range(n + 1)]
        for cp in sends:
            cp.start()
        for k, mask in enumerate(masks):
            for i in range(n + 1):
                arrival(i, k, peer(mask)).wait_recv()
        for cp in sends:
            cp.wait_send()
        for cp in local:
            cp.wait()

    pk_shape = jax.ShapeDtypeStruct((N_DEV,) + packs.shape, packs.dtype)
    return pl.pallas_call(
        body, name=name,
        in_specs=[HBM_SPEC] * (n + 1), out_specs=[HBM_SPEC] * (nr + 1), out_shape=list(recv_shapes) + [pk_shape],
        scratch_shapes=[pltpu.SemaphoreType.DMA((n + 1, 7)), pltpu.SemaphoreType.DMA((n + 1, 7)),
                        pltpu.SemaphoreType.DMA((n + 1,))],
        compiler_params=pltpu.CompilerParams(has_side_effects=True),
    )(*srcs, packs)


def _pad_w_in(w):
    take = jnp.take(w, np.maximum(_IN_PERM, 0), axis=-1)
    return jnp.where(_IN_PERM >= 0, take, jnp.zeros_like(take))


def _unpad_w_in(g):
    return jnp.take(g, _IN_INV, axis=-1)


def _small_pack(w_q_b, w_kv_b):
    a = jnp.pad(w_q_b, ((0, 0), (0, 0), (0, LANES - w_q_b.shape[2])))
    b = jnp.pad(w_kv_b, ((0, 0), (0, 0), (0, LANES - w_kv_b.shape[2])))
    return jnp.concatenate([a, b], axis=1)


def _small_unpack(p, cq, ckv):
    return p[:, 0:MLA_Q_RANK, 0:cq], p[:, MLA_Q_RANK:, 0:ckv]


def _mla_weights(wsm):
    H = MLA_HEADS
    cq = H * (MLA_NOPE + MLA_ROPE) // N_DEV
    ckv = H * (MLA_NOPE + MLA_V) // N_DEV
    wq = wsm[:, 0:MLA_Q_RANK, 0:cq].transpose(1, 0, 2).reshape(MLA_Q_RANK, H, MLA_NOPE + MLA_ROPE)
    wq = jnp.pad(wq, ((0, 0), (0, 0), (0, HEAD_BLOCK - MLA_NOPE - MLA_ROPE))).reshape(MLA_Q_RANK, H * HEAD_BLOCK)
    wkv = wsm[:, MLA_Q_RANK:, 0:ckv].transpose(1, 0, 2).reshape(MLA_KV_RANK, H, MLA_NOPE + MLA_V)
    wk = jnp.pad(wkv[:, :, 0:MLA_NOPE], ((0, 0), (0, 0), (0, HEAD_BLOCK - MLA_NOPE))).reshape(MLA_KV_RANK, H * HEAD_BLOCK)
    wv = wkv[:, :, MLA_NOPE:].reshape(MLA_KV_RANK, H * MLA_V)
    return wq, jnp.concatenate([wk, wv], axis=1)


def _mla_grads_to_blocks(dwq, dwkv):
    H = MLA_HEADS
    gq = dwq.reshape(MLA_Q_RANK, H, HEAD_BLOCK)[:, :, 0:MLA_NOPE + MLA_ROPE].reshape(MLA_Q_RANK, N_DEV, -1)
    gk = dwkv[:, 0:H * HEAD_BLOCK].reshape(MLA_KV_RANK, H, HEAD_BLOCK)[:, :, 0:MLA_NOPE]
    gv = dwkv[:, H * HEAD_BLOCK:].reshape(MLA_KV_RANK, H, MLA_V)
    gkv = jnp.concatenate([gk, gv], axis=2).reshape(MLA_KV_RANK, N_DEV, -1)
    return _small_pack(gq.transpose(1, 0, 2), gkv.transpose(1, 0, 2)).astype(BF16)


def _pool_blockdiag(pool_w):
    z = jnp.zeros((POOL_GROUP, POOL_GROUP), pool_w.dtype)
    halves = [jnp.concatenate([jnp.concatenate([pool_w[2 * i], z], axis=1),
                               jnp.concatenate([z, pool_w[2 * i + 1]], axis=1)], axis=0) for i in range(2)]
    return jnp.stack(halves)


def _pool_blockdiag_t(dw):
    g = POOL_GROUP
    return jnp.stack([dw[0, 0:g, 0:g], dw[0, g:, g:], dw[1, 0:g, 0:g], dw[1, g:, g:]])


def _gate_row(b):
    return jnp.zeros((LANES,), b.dtype).at[_F_LANES].set(b).reshape(1, LANES)


_SMALL = ("ffn1_norm", "mix_norm", "q_a_norm", "kv_a_norm", "pool_w", "pool_scale", "fox_b_f", "ffn2_norm", "final_norm")


def _pack_small(tree):
    rows, recipe = [], []
    for name in _SMALL:
        a = tree[name]
        flat = a.reshape(-1)
        n = flat.shape[0]
        nrow = -(-n // (8 * LANES)) * 8
        flat = jnp.pad(flat, (0, nrow * LANES - n))
        rows.append(flat.reshape(nrow, LANES))
        recipe.append((name, a.shape, n, nrow))
    return jnp.concatenate(rows, axis=0), recipe


def _unpack_small(packed, recipe):
    out, r0 = {}, 0
    for name, shape, n, nrow in recipe:
        out[name] = packed[r0:r0 + nrow].reshape(-1)[0:n].reshape(shape)
        r0 += nrow
    return out


def _adam_small(packs, w, m, v, name):
    R = w.shape[0]

    def body(p_ref, w_ref, m_ref, v_ref, g_out, d_out, m_out, v_out):
        g = p_ref[0]
        for p in range(1, N_DEV):
            g = g + p_ref[p]
        delta, mn, vn = _adam_math(g, w_ref[...], m_ref[...], v_ref[...])
        g_out[...] = g
        d_out[...] = delta
        m_out[...] = mn
        v_out[...] = vn

    blk = _bs((R, LANES), lambda i: (0, 0))
    shp = jax.ShapeDtypeStruct((R, LANES), F32)
    return pl.pallas_call(
        body, name=name, grid=(1,),
        in_specs=[_bs((N_DEV, R, LANES), lambda i: (0, 0, 0)), blk, blk, blk],
        out_specs=[blk, blk, blk, blk], out_shape=[shp, shp, shp, shp],
        compiler_params=_params(("arbitrary",)),
    )(packs, w, m, v)


def _local_step(x, target, gw, small):
    S, D = x.shape
    tabs = _rope_tables(S)
    nhp_a, nhp_c = MLA_HEADS // 2, FOX_HEADS // 2
    fox_scale = 1.0 / math.sqrt(FOX_HEAD_DIM)
    saved = []
    for l in range(DEPTH):
        s = {}
        s["x0"] = x
        wgu1, wd1 = gw["ffn1_w_gu"][l], gw["ffn1_w_down"][l]
        x1, s["h1"], s["gu1"] = _ffn_fwd(x, small["ffn1_norm"][l][None], wgu1, wd1, f"ffn1_fwd_l{l}")
        s["x1"] = x1
        w_in = gw["w_in"][l]
        s["h2"], za, zf, zkvt = _mix_in_fwd(x1, small["mix_norm"][l][None], w_in, f"mix_in_fwd_l{l}")
        s["za"], s["zf"], s["zkvt"] = za, zf, zkvt
        wq, wkv = _mla_weights(gw["w_small"][l])
        s["wq"], s["wkv"] = wq, wkv
        gq, gkv = small["q_a_norm"][l][None], small["kv_a_norm"][l][None]
        qf, kf, vm, kft, vmt = _mla_prep(za, gq, gkv, wq, wkv, tabs, f"mla_prep_l{l}")
        s["qf"], s["kf"], s["vm"], s["kft"] = qf, kf, vm, kft
        ya, lse_a = _attn_fwd_t(qf, kf, vmt, nhp=nhp_a, dkb=LANES, qoff=0, koff=0, vtoff=0, scale=1.0,
                                cum=None, cumT=None, name=f"mla_attn_fwd_l{l}")
        s["ya"], s["lse_a"] = ya, lse_a
        b_row = _gate_row(small["fox_b_f"][l])
        s["b_row"] = b_row
        cum, cumT = _fox_prep(za, b_row, f"fox_prep_l{l}")
        s["cum"], s["cumT"] = cum, cumT
        yc, lse_c = _attn_fwd_t(zf, zf, zkvt, nhp=nhp_c, dkb=64, qoff=0, koff=nhp_c, vtoff=nhp_c, scale=fox_scale,
                                cum=cum, cumT=cumT, name=f"fox_attn_fwd_l{l}")
        s["yc"], s["lse_c"] = yc, lse_c
        wbd = _pool_blockdiag(small["pool_w"][l]).astype(BF16)
        s["wbd"] = wbd
        psc = small["pool_scale"][l][None]
        yb, s["pd"] = _pool_fwd(za, wbd, psc, f"pool_fwd_l{l}")
        w_out = gw["w_out"][l]
        x2, s["ycat"] = _mix_out_fwd(x1, ya, yb, yc, w_out, f"mix_out_fwd_l{l}")
        s["x2"] = x2
        wgu2, wd2 = gw["ffn2_w_gu"][l], gw["ffn2_w_down"][l]
        x, s["h3"], s["gu2"] = _ffn_fwd(x2, small["ffn2_norm"][l][None], wgu2, wd2, f"ffn2_fwd_l{l}")
        saved.append(s)

    dx, d_final, loss = _loss_head(x, small["final_norm"][None], target, "loss_head")

    grads = [None] * DEPTH
    for l in reversed(range(DEPTH)):
        s = saved[l]
        g = {}
        wgu2, wd2 = gw["ffn2_w_gu"][l], gw["ffn2_w_down"][l]
        nj = wd2.shape[0]
        dy3 = dx
        dx, g["ffn2_norm"], dgu, ah = _ffn_bwd(dy3, s["x2"], small["ffn2_norm"][l][None], s["gu2"], wgu2, wd2,
                                               f"ffn2_bwd_l{l}")
        g["ffn2_w_gu"] = _mm_tn(s["h3"][None], dgu.reshape((2 * nj,) + dgu.shape[2:]), 2 * nj, lambda p: 0, lambda p: p,
                                1024, 768, f"ffn2_dwgu_l{l}")
        g["ffn2_w_down"] = _mm_tn(ah, dy3[None], nj, lambda p: p, lambda p: 0, 768, 1024, f"ffn2_dwd_l{l}")

        w_out = gw["w_out"][l]
        dya, dyb, dyc = _mix_out_bwd(dx, w_out, f"mix_out_bwd_l{l}")
        g["w_out"] = _mm_tn(s["ycat"][None], dx[None], 1, lambda p: 0, lambda p: 0, 1024, 1024, f"dwout_l{l}")[0]

        dvec_a = _attn_dvec(s["ya"], dya, nhp_a, f"mla_dvec_l{l}")
        dqf, dkf, dvm = _attn_bwd_t(s["qf"], s["kf"], s["kft"], s["vm"], dya, s["lse_a"], dvec_a, nhp=nhp_a, dkb=LANES,
                                    qoff=0, koff=0, ktoff=0, voff=0, scale=1.0, cum=None, cumT=None,
                                    name=f"mla_attn_bwd_l{l}")
        zf = s["zf"]
        dvec_c = _attn_dvec(s["yc"], dyc, nhp_c, f"fox_dvec_l{l}")
        dqc, dkc, dvc, dcq, dck = _attn_bwd_t(zf, zf, s["zkvt"], zf, dyc, s["lse_c"], dvec_c, nhp=nhp_c, dkb=64, qoff=0,
                                              koff=nhp_c, ktoff=0, voff=2 * nhp_c, scale=fox_scale, cum=s["cum"],
                                              cumT=s["cumT"], name=f"fox_attn_bwd_l{l}")
        dtail_f, db = _fox_prep_bwd(s["za"], s["b_row"], dcq, dck, f"fox_prep_bwd_l{l}")
        g["fox_b_f"] = db[0, _F_LANES]
        psc = small["pool_scale"][l][None]
        du, dwbd, dpsc = _pool_bwd(dyb, s["pd"], s["wbd"], psc, f"pool_bwd_l{l}")
        g["pool_w"] = _pool_blockdiag_t(dwbd)
        g["pool_scale"] = dpsc[0]
        gq, gkv = small["q_a_norm"][l][None], small["kv_a_norm"][l][None]
        dza, dwq, dwkv, dgq, dgkv = _mla_prep_bwd(s["za"], gq, gkv, s["wq"], s["wkv"], tabs, dqf, dkf, dvm,
                                                   f"mla_prep_bwd_l{l}")
        g["q_a_norm"], g["kv_a_norm"] = dgq[0], dgkv[0]
        g["w_small"] = _mla_grads_to_blocks(dwq, dwkv)
        w_in = gw["w_in"][l]
        dx, g["mix_norm"], dz = _mix_in_bwd(dx, s["x1"], small["mix_norm"][l][None], dza, du, dtail_f, dqc, dkc, dvc,
                                            w_in, f"mix_in_bwd_l{l}")
        g["w_in"] = _unpad_w_in(_mm_tn(s["h2"][None], dz[None], 1, lambda p: 0, lambda p: 0, 1024, 640, f"dwin_l{l}")[0])

        wgu1, wd1 = gw["ffn1_w_gu"][l], gw["ffn1_w_down"][l]
        dy1 = dx
        dx, g["ffn1_norm"], dgu, ah = _ffn_bwd(dy1, s["x0"], small["ffn1_norm"][l][None], s["gu1"], wgu1, wd1,
                                               f"ffn1_bwd_l{l}")
        g["ffn1_w_gu"] = _mm_tn(s["h1"][None], dgu.reshape((2 * nj,) + dgu.shape[2:]), 2 * nj, lambda p: 0, lambda p: p,
                                1024, 768, f"ffn1_dwgu_l{l}")
        g["ffn1_w_down"] = _mm_tn(ah, dy1[None], nj, lambda p: p, lambda p: 0, 768, 1024, f"ffn1_dwd_l{l}")
        for k in ("ffn1_norm", "ffn2_norm", "mix_norm"):
            g[k] = g[k][0]
        grads[l] = g
    return loss, dx, grads, d_final[0]


_BIG = ("ffn1_w_gu", "ffn1_w_down", "w_in", "w_small", "w_out", "ffn2_w_gu", "ffn2_w_down")


def _pad_cols(w, n):
    return jnp.pad(w, ((0, 0),) * (w.ndim - 1) + ((0, n - w.shape[-1]),))


def kernel(x, ffn1_norm, ffn1_w_gu, ffn1_w_down, mix_norm, w_in, q_a_norm, w_q_b, kv_a_norm, w_kv_b, pool_w, pool_scale, fox_b_f, w_out, ffn2_norm, ffn2_w_gu, ffn2_w_down, final_norm, loss_target, m_ffn1_norm, m_ffn1_w_gu, m_ffn1_w_down, m_mix_norm, m_w_in, m_q_a_norm, m_w_q_b, m_kv_a_norm, m_w_kv_b, m_pool_w, m_pool_scale, m_fox_b_f, m_w_out, m_ffn2_norm, m_ffn2_w_gu, m_ffn2_w_down, m_final_norm, v_ffn1_norm, v_ffn1_w_gu, v_ffn1_w_down, v_mix_norm, v_w_in, v_q_a_norm, v_w_q_b, v_kv_a_norm, v_w_kv_b, v_pool_w, v_pool_scale, v_fox_b_f, v_w_out, v_ffn2_norm, v_ffn2_w_gu, v_ffn2_w_down, v_final_norm):
    W = dict(ffn1_norm=ffn1_norm, ffn1_w_gu=ffn1_w_gu, ffn1_w_down=ffn1_w_down, mix_norm=mix_norm, w_in=w_in,
             q_a_norm=q_a_norm, w_q_b=w_q_b, kv_a_norm=kv_a_norm, w_kv_b=w_kv_b, pool_w=pool_w, pool_scale=pool_scale,
             fox_b_f=fox_b_f, w_out=w_out, ffn2_norm=ffn2_norm, ffn2_w_gu=ffn2_w_gu, ffn2_w_down=ffn2_w_down,
             final_norm=final_norm)
    M = dict(ffn1_norm=m_ffn1_norm, ffn1_w_gu=m_ffn1_w_gu, ffn1_w_down=m_ffn1_w_down, mix_norm=m_mix_norm, w_in=m_w_in,
             q_a_norm=m_q_a_norm, w_q_b=m_w_q_b, kv_a_norm=m_kv_a_norm, w_kv_b=m_w_kv_b, pool_w=m_pool_w,
             pool_scale=m_pool_scale, fox_b_f=m_fox_b_f, w_out=m_w_out, ffn2_norm=m_ffn2_norm, ffn2_w_gu=m_ffn2_w_gu,
             ffn2_w_down=m_ffn2_w_down, final_norm=m_final_norm)
    V = dict(ffn1_norm=v_ffn1_norm, ffn1_w_gu=v_ffn1_w_gu, ffn1_w_down=v_ffn1_w_down, mix_norm=v_mix_norm, w_in=v_w_in,
             q_a_norm=v_q_a_norm, w_q_b=v_w_q_b, kv_a_norm=v_kv_a_norm, w_kv_b=v_w_kv_b, pool_w=v_pool_w,
             pool_scale=v_pool_scale, fox_b_f=v_fox_b_f, w_out=v_w_out, ffn2_norm=v_ffn2_norm, ffn2_w_gu=v_ffn2_w_gu,
             ffn2_w_down=v_ffn2_w_down, final_norm=v_final_norm)
    L, D, n_sh = ffn1_w_gu.shape
    f_sh = ffn1_w_down.shape[1]
    assert n_sh == 2 * f_sh and L == DEPTH
    n_pad = -(-n_sh // LANES) * LANES
    nj = N_DEV // 2
    S = x.shape[1]

    shards = [
        _pad_cols(ffn1_w_gu, n_pad).astype(BF16),
        ffn1_w_down.astype(BF16),
        _pad_w_in(w_in).astype(BF16),
        _small_pack(w_q_b, w_kv_b).astype(BF16),
        w_out.astype(BF16),
        _pad_cols(ffn2_w_gu, n_pad).astype(BF16),
        ffn2_w_down.astype(BF16),
    ]
    r_in, r_out = w_in.shape[1], w_out.shape[1]
    out_shapes = [
        jax.ShapeDtypeStruct((L, N_DEV, D, n_pad), BF16),
        jax.ShapeDtypeStruct((L, nj, n_pad, D), BF16),
        jax.ShapeDtypeStruct((L, N_DEV, r_in, N_PAD), BF16),
        jax.ShapeDtypeStruct((L, N_DEV, MLA_Q_RANK + MLA_KV_RANK, LANES), BF16),
        jax.ShapeDtypeStruct((L, N_DEV, r_out, D), BF16),
        jax.ShapeDtypeStruct((L, N_DEV, D, n_pad), BF16),
        jax.ShapeDtypeStruct((L, nj, n_pad, D), BF16),
    ]
    by_dev = lambda ref, p: ref.at[:, p]
    by_half = lambda ref, p: ref.at[:, p // 2, pl.ds(pl.multiple_of((p % 2) * f_sh, 16), f_sh)]
    views = [by_dev, by_half, by_dev, by_dev, by_dev, by_dev, by_half]
    n_zero = n_pad - 2 * f_sh
    zero_src = jnp.zeros((L, max(n_zero, 16), D), BF16)
    zero_views = [(a, (lambda ref, j=j: ref.at[:, j, pl.ds(2 * f_sh, n_zero)])) for a in (1, 6) for j in range(nj)]
    if n_zero == 0:
        zero_views = []
    gathered = _all_gather(shards, out_shapes, views, zero_src, zero_views, "gather_weights")
    gw = dict(zip(_BIG, gathered))
    gw["ffn1_w_gu"] = gw["ffn1_w_gu"].reshape(L, 2, nj, D, n_pad)
    gw["ffn2_w_gu"] = gw["ffn2_w_gu"].reshape(L, 2, nj, D, n_pad)
    gw["w_in"] = gw["w_in"].reshape(L, N_DEV * r_in, N_PAD)
    gw["w_out"] = gw["w_out"].reshape(L, N_DEV * r_out, D)

    small = {k: W[k] for k in _SMALL}
    loss, dx, grads, d_final = _local_step(x[0], loss_target[0], gw, small)

    small_g = {k: jnp.stack([grads[l][k] for l in range(L)]) for k in _SMALL if k != "final_norm"}
    small_g["final_norm"] = d_final
    pack_g, recipe = _pack_small(small_g)
    n_small = pack_g.shape[0]
    loss_row = -(-n_small // 8) * 8
    pack_g = jnp.concatenate([pack_g, jnp.zeros((loss_row - n_small, LANES), F32), jnp.broadcast_to(loss, (8, LANES))],
                             axis=0)

    srcs, src_views, recv_of = [], [], []
    from_dev = lambda ref, p: ref.at[p]
    from_half = lambda ref, p: ref.at[p // 2, pl.ds(pl.multiple_of((p % 2) * f_sh, 16), f_sh)]
    kinds = [("ffn1_w_gu", from_dev), ("ffn1_w_down", from_half), ("w_in", from_dev), ("w_small", from_dev),
             ("w_out", from_dev), ("ffn2_w_gu", from_dev), ("ffn2_w_down", from_half)]
    for a, (k, view) in enumerate(kinds):
        for l in range(L):
            gk = grads[l][k]
            if k in ("w_in", "w_out"):
                gk = gk.reshape((N_DEV, gk.shape[0] // N_DEV) + gk.shape[1:])
            srcs.append(gk)
            src_views.append(view)
            recv_of.append((a, l))
    recv_shapes = [
        jax.ShapeDtypeStruct((L, N_DEV, D, n_pad), BF16),
        jax.ShapeDtypeStruct((L, N_DEV, f_sh, D), BF16),
        jax.ShapeDtypeStruct((L, N_DEV, r_in, N_IN), BF16),
        jax.ShapeDtypeStruct((L, N_DEV, MLA_Q_RANK + MLA_KV_RANK, LANES), BF16),
        jax.ShapeDtypeStruct((L, N_DEV, r_out, D), BF16),
        jax.ShapeDtypeStruct((L, N_DEV, D, n_pad), BF16),
        jax.ShapeDtypeStruct((L, N_DEV, f_sh, D), BF16),
    ]
    *recv, packs = _reduce_scatter(srcs, src_views, recv_shapes, recv_of, pack_g, "scatter_grads")

    out = {}
    sm_w, sm_m, sm_v = (_small_pack(t["w_q_b"], t["w_kv_b"]) for t in (W, M, V))
    big = [("ffn1_w_gu", W["ffn1_w_gu"], M["ffn1_w_gu"], V["ffn1_w_gu"], 256),
           ("ffn1_w_down", W["ffn1_w_down"], M["ffn1_w_down"], V["ffn1_w_down"], 352),
           ("w_in", W["w_in"], M["w_in"], V["w_in"], 128),
           ("w_small", sm_w, sm_m, sm_v, 384),
           ("w_out", W["w_out"], M["w_out"], V["w_out"], 128),
           ("ffn2_w_gu", W["ffn2_w_gu"], M["ffn2_w_gu"], V["ffn2_w_gu"], 256),
           ("ffn2_w_down", W["ffn2_w_down"], M["ffn2_w_down"], V["ffn2_w_down"], 352)]
    for a, (k, w_, m_, v_, tr) in enumerate(big):
        res = _adam_sum(recv[a], w_, m_, v_, tr, f"adam_{k}")
        if k == "w_small":
            cq, ckv = w_q_b.shape[2], w_kv_b.shape[2]
            parts = [_small_unpack(r, cq, ckv) for r in res]
            out["w_q_b"] = [p[0] for p in parts]
            out["w_kv_b"] = [p[1] for p in parts]
        else:
            out[k] = res

    pw, _ = _pack_small({k: W[k] for k in _SMALL})
    pm, _ = _pack_small({k: M[k] for k in _SMALL})
    pv, _ = _pack_small({k: V[k] for k in _SMALL})
    extra = ((0, loss_row + 8 - n_small), (0, 0))
    res = _adam_small(packs, jnp.pad(pw, extra), jnp.pad(pm, extra), jnp.pad(pv, extra), "adam_small")
    loss_total = res[0][loss_row, 0]
    small_out = [_unpack_small(r, recipe) for r in res]
    for k in _SMALL:
        out[k] = [t[k] for t in small_out]

    names = ["ffn1_norm", "ffn1_w_gu", "ffn1_w_down", "mix_norm", "w_in", "q_a_norm", "w_q_b", "kv_a_norm", "w_kv_b",
             "pool_w", "pool_scale", "fox_b_f", "w_out", "ffn2_norm", "ffn2_w_gu", "ffn2_w_down", "final_norm"]
    outs = [loss_total, dx[None]]
    for which in range(4):
        outs += [out[k][which] for k in names]
    return tuple(outs)
```

```python
import functools
import math

import numpy as np
import jax
import jax.numpy as jnp
from jax import lax
from jax.experimental import pallas as pl
from jax.experimental.pallas import tpu as pltpu

F32 = jnp.float32
BF16 = jnp.bfloat16
MESH_ID = pl.DeviceIdType.MESH

N_DEV = 8
EPS = 1e-6
DEPTH = 2

MLA_HEADS = 6
MLA_Q_RANK = 256
MLA_KV_RANK = 128
MLA_NOPE = 64
MLA_ROPE = 32
MLA_V = 64
ROPE_THETA = 10000.0
POOL_WINDOWS = (2, 4, 8, 16)
POOL_GROUP = 64
POOL_WIDTH = 256
FOX_HEADS = 6
FOX_HEAD_DIM = 64
N_IN = 1830

ADAM_LR = 0.001
ADAM_B1 = 0.9
ADAM_B2 = 0.999
ADAM_EPS = 1e-08
ADAM_WD = 0.01
ADAM_STEP = 10

LANES = 128
HEAD_BLOCK = 128
VMEM_LIMIT = 48 * 1024 * 1024
NEG = -1e30

ZA = 768
ZF = 1152
N_PAD = ZA + ZF
TAIL0 = 640
ROPE_LANE0 = 64


def _f_lane(h):
    return 8 * (h // 2) + (h % 2)


def _in_perm():
    perm = -np.ones(N_PAD, np.int32)
    perm[0:256] = np.arange(0, 256)
    perm[256:384] = np.arange(256, 384)
    perm[384:640] = np.arange(416, 672)
    for h in range(FOX_HEADS):
        perm[TAIL0 + _f_lane(h)] = 1824 + h
    perm[TAIL0 + ROPE_LANE0:TAIL0 + ROPE_LANE0 + MLA_ROPE] = np.arange(384, 416)
    perm[ZA:N_PAD] = np.arange(672, 1824)
    inv = np.zeros(N_IN, np.int32)
    for new, old in enumerate(perm):
        if old >= 0:
            inv[old] = new
    return perm, inv


_IN_PERM, _IN_INV = _in_perm()
_F_LANES = np.array([_f_lane(h) for h in range(FOX_HEADS)], np.int32)


def _dot(a, b):
    return jnp.dot(a, b, preferred_element_type=F32)


def _dot_nt(a, b):
    return lax.dot_general(a, b, (((1,), (1,)), ((), ())), preferred_element_type=F32)


def _dot_tn(a, b):
    return lax.dot_general(a, b, (((0,), (0,)), ((), ())), preferred_element_type=F32)


def _rms(x, gam):
    r = lax.rsqrt(jnp.mean(x * x, axis=-1, keepdims=True) + EPS)
    return x * r * gam


def _rms_bwd(dy, x, gam):
    r = lax.rsqrt(jnp.mean(x * x, axis=-1, keepdims=True) + EPS)
    xh = x * r
    dxh = dy * gam
    dx = r * (dxh - xh * jnp.mean(dxh * xh, axis=-1, keepdims=True))
    return dx, jnp.sum(dy * xh, axis=0, keepdims=True)


def _accum_out(ref, first, val):
    @pl.when(first)
    def _():
        ref[...] = val

    @pl.when(jnp.logical_not(first))
    def _():
        ref[...] += val


def _bs(shape, fn):
    return pl.BlockSpec(shape, fn)


def _params(dims):
    return pltpu.CompilerParams(dimension_semantics=dims, vmem_limit_bytes=VMEM_LIMIT)


def _tile(n, t):
    t = min(n, t)
    assert n % t == 0, (n, t)
    return t


HBM_SPEC = pl.BlockSpec(memory_space=pl.ANY)


def _call(body, *, name, grid, in_specs, out_specs, out_shape, scratch, dims, args, rider=None):
    n_in, n_out = len(in_specs), len(out_specs)
    if rider is None:
        outs = pl.pallas_call(body, name=name, grid=grid, in_specs=in_specs, out_specs=out_specs, out_shape=out_shape,
                              scratch_shapes=scratch, compiler_params=_params(dims))(*args)
        return list(outs), []
    k_in, k_out, k_sem = len(rider.srcs), len(rider.out_shapes), len(rider.scratch)

    def riding(*refs):
        a, b, c, d = n_in, n_in + k_in, n_in + k_in + n_out, n_in + k_in + n_out + k_out
        rest = refs[d:]
        sems = rest[len(rest) - k_sem:]
        ids = [pl.program_id(i) for i in range(len(grid))]
        first = functools.reduce(jnp.logical_and, [i == 0 for i in ids])
        last = functools.reduce(jnp.logical_and, [i == g - 1 for i, g in zip(ids, grid)])

        @pl.when(first)
        def _():
            rider.begin(refs[a:b], refs[c:d], sems)

        body(*refs[:a], *refs[b:c], *rest[:len(rest) - k_sem])

        @pl.when(last)
        def _():
            rider.end(refs[a:b], refs[c:d], sems)

    outs = pl.pallas_call(
        riding, name=name, grid=grid, in_specs=list(in_specs) + [HBM_SPEC] * k_in,
        out_specs=list(out_specs) + [HBM_SPEC] * k_out, out_shape=list(out_shape) + list(rider.out_shapes),
        scratch_shapes=list(scratch) + list(rider.scratch),
        compiler_params=_params(("arbitrary",) * len(grid)))(*args, *rider.srcs)
    return list(outs[:n_out]), list(outs[n_out:])


def _ffn_fwd(x, gam, wgu, wd, name, rider=None):
    S, D = x.shape
    _, nj, _, n = wgu.shape
    tm = _tile(S, 512)

    def body(x_ref, gam_ref, wgu_ref, wd_ref, xo_ref, h_ref, gu_ref, hs, acc):
        j = pl.program_id(1)

        @pl.when(j == 0)
        def _():
            hb = _rms(x_ref[...], gam_ref[...]).astype(BF16)
            hs[...] = hb
            h_ref[...] = hb
            acc[...] = jnp.zeros_like(acc)

        h = hs[...]
        g = _dot(h, wgu_ref[0])
        u = _dot(h, wgu_ref[1])
        gu_ref[0] = g.astype(BF16)
        gu_ref[1] = u.astype(BF16)
        a = (g * jax.nn.sigmoid(g) * u).astype(BF16)
        acc[...] += _dot(a, wd_ref[...])

        @pl.when(j == nj - 1)
        def _():
            xo_ref[...] = x_ref[...] + 0.5 * acc[...]

    return _call(
        body, name=name, grid=(S // tm, nj),
        in_specs=[_bs((tm, D), lambda i, j: (i, 0)), _bs((1, D), lambda i, j: (0, 0)),
                  _bs((2, None, D, n), lambda i, j: (0, j, 0, 0)), _bs((None, n, D), lambda i, j: (j, 0, 0))],
        out_specs=[_bs((tm, D), lambda i, j: (i, 0)), _bs((tm, D), lambda i, j: (i, 0)),
                   _bs((2, None, tm, n), lambda i, j: (0, j, i, 0))],
        out_shape=[jax.ShapeDtypeStruct((S, D), F32), jax.ShapeDtypeStruct((S, D), BF16),
                   jax.ShapeDtypeStruct((2, nj, S, n), BF16)],
        scratch=[pltpu.VMEM((tm, D), BF16), pltpu.VMEM((tm, D), F32)],
        dims=("parallel", "arbitrary"), args=(x, gam, wgu, wd), rider=rider)


def _ffn_bwd(dy, x, gam, gu, wgu, wd, name, rider=None):
    S, D = x.shape
    _, nj, _, n = wgu.shape
    tm = _tile(S, 512)

    def body(dy_ref, x_ref, gam_ref, gu_ref, wgu_ref, wd_ref, dx_ref, dgam_ref, dgu_ref, a_ref, dyb, acc):
        i = pl.program_id(0)
        j = pl.program_id(1)

        @pl.when(j == 0)
        def _():
            dyb[...] = dy_ref[...].astype(BF16)
            acc[...] = jnp.zeros_like(acc)

        da = 0.5 * _dot_nt(dyb[...], wd_ref[...])
        g = gu_ref[0].astype(F32)
        u = gu_ref[1].astype(F32)
        sig = jax.nn.sigmoid(g)
        sl = g * sig
        dg = (da * u * (sig * (1.0 + g * (1.0 - sig)))).astype(BF16)
        du = (da * sl).astype(BF16)
        dgu_ref[0] = dg
        dgu_ref[1] = du
        a_ref[...] = (0.5 * sl * u).astype(BF16)
        acc[...] += _dot_nt(dg, wgu_ref[0]) + _dot_nt(du, wgu_ref[1])

        @pl.when(j == nj - 1)
        def _():
            dxn, dgam = _rms_bwd(acc[...], x_ref[...], gam_ref[...])
            dx_ref[...] = dy_ref[...] + dxn
            _accum_out(dgam_ref, i == 0, dgam)

    return _call(
        body, name=name, grid=(S // tm, nj),
        in_specs=[_bs((tm, D), lambda i, j: (i, 0)), _bs((tm, D), lambda i, j: (i, 0)), _bs((1, D), lambda i, j: (0, 0)),
                  _bs((2, None, tm, n), lambda i, j: (0, j, i, 0)),
                  _bs((2, None, D, n), lambda i, j: (0, j, 0, 0)), _bs((None, n, D), lambda i, j: (j, 0, 0))],
        out_specs=[_bs((tm, D), lambda i, j: (i, 0)), _bs((1, D), lambda i, j: (0, 0)),
                   _bs((2, None, tm, n), lambda i, j: (0, j, i, 0)), _bs((None, tm, n), lambda i, j: (j, i, 0))],
        out_shape=[jax.ShapeDtypeStruct((S, D), F32), jax.ShapeDtypeStruct((1, D), F32),
                   jax.ShapeDtypeStruct((2, nj, S, n), BF16), jax.ShapeDtypeStruct((nj, S, n), BF16)],
        scratch=[pltpu.VMEM((tm, D), BF16), pltpu.VMEM((tm, D), F32)],
        dims=("arbitrary", "arbitrary"), args=(dy, x, gam, gu, wgu, wd), rider=rider)


def _mm_tn(a, b, nb, a_of, b_of, tm, tn, name, rider=None):
    _, S, M = a.shape
    N = b.shape[2]
    tm = _tile(M, tm)
    tn = _tile(N, tn)
    ts = _tile(S, 512)
    nk = S // ts

    def body(a_ref, b_ref, o_ref, acc):
        k = pl.program_id(3)

        @pl.when(k == 0)
        def _():
            acc[...] = jnp.zeros_like(acc)

        acc[...] += _dot_tn(a_ref[...].astype(BF16), b_ref[...].astype(BF16))

        @pl.when(k == nk - 1)
        def _():
            o_ref[...] = acc[...].astype(o_ref.dtype)

    (out,), extra = _call(
        body, name=name, grid=(nb, M // tm, N // tn, nk),
        in_specs=[_bs((None, ts, tm), lambda p, i, j, k: (a_of(p), k, i)),
                  _bs((None, ts, tn), lambda p, i, j, k: (b_of(p), k, j))],
        out_specs=[_bs((None, tm, tn), lambda p, i, j, k: (p, i, j))],
        out_shape=[jax.ShapeDtypeStruct((nb, M, N), BF16)],
        scratch=[pltpu.VMEM((tm, tn), F32)],
        dims=("parallel", "parallel", "parallel", "arbitrary"), args=(a, b), rider=rider)
    return (out, extra) if rider is not None else out


def _mix_in_fwd(x, gam, w_in, name):
    S, D = x.shape
    tm = _tile(S, 512)
    nkv = (ZF - 384) // LANES

    def body(x_ref, gam_ref, w_ref, h_ref, za_ref, zf_ref, zt_ref):
        hb = _rms(x_ref[...], gam_ref[...]).astype(BF16)
        h_ref[...] = hb
        za_ref[...] = _dot(hb, w_ref[:, 0:ZA])
        zf = _dot(hb, w_ref[:, ZA:N_PAD])
        zf_ref[...] = zf.astype(BF16)
        for c in range(nkv):
            zt_ref[c * LANES:(c + 1) * LANES, :] = zf[:, 384 + c * LANES:384 + (c + 1) * LANES].T.astype(BF16)

    return pl.pallas_call(
        body, name=name, grid=(S // tm,),
        in_specs=[_bs((tm, D), lambda i: (i, 0)), _bs((1, D), lambda i: (0, 0)), _bs((D, N_PAD), lambda i: (0, 0))],
        out_specs=[_bs((tm, D), lambda i: (i, 0)), _bs((tm, ZA), lambda i: (i, 0)), _bs((tm, ZF), lambda i: (i, 0)),
                   _bs((nkv * LANES, tm), lambda i: (0, i))],
        out_shape=[jax.ShapeDtypeStruct((S, D), BF16), jax.ShapeDtypeStruct((S, ZA), F32),
                   jax.ShapeDtypeStruct((S, ZF), BF16), jax.ShapeDtypeStruct((nkv * LANES, S), BF16)],
        compiler_params=_params(("parallel",)),
    )(x, gam, w_in)


def _mix_in_bwd(dy, x, gam, dza_mla, du, dtail_f, dqf, dkf, dvf, w_in, name):
    S, D = x.shape
    tm = _tile(S, 512)

    def body(dy_ref, x_ref, gam_ref, dza_ref, du_ref, dt_ref, dq_ref, dk_ref, dv_ref, w_ref, dx_ref, dgam_ref, dz_ref):
        i = pl.program_id(0)
        dza = dza_ref[...]
        dz = jnp.concatenate([dza[:, 0:384], du_ref[...], dza[:, TAIL0:ZA] + dt_ref[...],
                              dq_ref[...], dk_ref[...], dv_ref[...]], axis=1).astype(BF16)
        dz_ref[...] = dz
        dh = _dot_nt(dz, w_ref[...])
        dxn, dgam = _rms_bwd(dh, x_ref[...], gam_ref[...])
        dx_ref[...] = dy_ref[...] + dxn
        _accum_out(dgam_ref, i == 0, dgam)

    row = lambda i: (i, 0)
    fix = lambda i: (0, 0)
    return pl.pallas_call(
        body, name=name, grid=(S // tm,),
        in_specs=[_bs((tm, D), row), _bs((tm, D), row), _bs((1, D), fix), _bs((tm, ZA), row), _bs((tm, 256), row),
                  _bs((tm, 128), row), _bs((tm, 384), row), _bs((tm, 384), row), _bs((tm, 384), row),
                  _bs((D, N_PAD), fix)],
        out_specs=[_bs((tm, D), row), _bs((1, D), fix), _bs((tm, N_PAD), row)],
        out_shape=[jax.ShapeDtypeStruct((S, D), F32), jax.ShapeDtypeStruct((1, D), F32),
                   jax.ShapeDtypeStruct((S, N_PAD), BF16)],
        compiler_params=_params(("arbitrary",)),
    )(dy, x, gam, dza_mla, du, dtail_f, dqf, dkf, dvf, w_in)


def _rope_tables(S):
    half = MLA_ROPE // 2
    inv_freq = ROPE_THETA ** (-jnp.arange(0, MLA_ROPE, 2, dtype=F32) / MLA_ROPE)
    ang = jnp.arange(S, dtype=jnp.int32).astype(F32)[:, None] * inv_freq[None, :]
    cos, sin = jnp.cos(ang), jnp.sin(ang)
    one = jnp.ones((S, ROPE_LANE0), F32)
    zero = jnp.zeros((S, ROPE_LANE0), F32)
    pad1 = jnp.ones((S, LANES - ROPE_LANE0 - MLA_ROPE), F32)
    pad0 = jnp.zeros((S, LANES - ROPE_LANE0 - MLA_ROPE), F32)
    zh = jnp.zeros((S, half), F32)
    tab_c = jnp.concatenate([one, cos, cos, pad1], axis=1)
    tab_ck = jnp.concatenate([zero, cos, cos, pad0], axis=1)
    tab_s1 = jnp.concatenate([zero, -sin, zh, pad0], axis=1)
    tab_s2 = jnp.concatenate([zero, zh, sin, pad0], axis=1)
    return tab_c, tab_ck, tab_s1, tab_s2


def _rope(x, c, s1, s2):
    return x * c + pltpu.roll(x, LANES - 16, 1) * s1 + pltpu.roll(x, 16, 1) * s2


def _rope_t(dy, c, s1, s2):
    return dy * c + pltpu.roll(dy * s1, 16, 1) + pltpu.roll(dy * s2, LANES - 16, 1)


_MLA_SCALE = 1.0 / math.sqrt(MLA_NOPE + MLA_ROPE)


def _mla_prep(za, gq, gkv, wq, wkv, tabs, name):
    S = za.shape[0]
    tm = _tile(S, 512)
    H = MLA_HEADS

    def body(zq_ref, tail_ref, gq_ref, gkv_ref, wq_ref, wkv_ref, c_ref, ck_ref, s1_ref, s2_ref,
             qf_ref, kf_ref, v_ref, kft_ref, vt_ref):
        zq = zq_ref[...]
        c, s1, s2 = c_ref[...], s1_ref[...], s2_ref[...]
        qn = _rms(zq[:, 0:256], gq_ref[...]).astype(BF16)
        q = _dot(qn, wq_ref[...])
        for h in range(H):
            blk = _rope(q[:, h * LANES:(h + 1) * LANES], c, s1, s2)
            qf_ref[:, h * LANES:(h + 1) * LANES] = (blk * _MLA_SCALE).astype(BF16)
        kvn = _rms(zq[:, 256:384], gkv_ref[...]).astype(BF16)
        kv = _dot(kvn, wkv_ref[...])
        kt = _rope(tail_ref[...], ck_ref[...], s1, s2)
        for h in range(H):
            sl = slice(h * LANES, (h + 1) * LANES)
            kblk = kv[:, sl] + kt
            kf_ref[:, sl] = kblk.astype(BF16)
            kft_ref[sl, :] = kblk.T.astype(BF16)
        v_ref[...] = kv[:, H * LANES:].astype(BF16)
        for cblk in range(H * MLA_V // LANES):
            sl = slice(cblk * LANES, (cblk + 1) * LANES)
            vt_ref[sl, :] = kv[:, H * LANES + cblk * LANES:H * LANES + (cblk + 1) * LANES].T.astype(BF16)

    row = lambda i: (i, 0)
    col = lambda i: (0, i)
    fix = lambda i: (0, 0)
    return pl.pallas_call(
        body, name=name, grid=(S // tm,),
        in_specs=[_bs((tm, 384), row), _bs((tm, 128), lambda i: (i, TAIL0 // 128)), _bs((1, 256), fix), _bs((1, 128), fix),
                  _bs((256, 768), fix), _bs((128, 1152), fix),
                  _bs((tm, 128), row), _bs((tm, 128), row), _bs((tm, 128), row), _bs((tm, 128), row)],
        out_specs=[_bs((tm, 768), row), _bs((tm, 768), row), _bs((tm, 384), row), _bs((768, tm), col), _bs((384, tm), col)],
        out_shape=[jax.ShapeDtypeStruct((S, 768), BF16), jax.ShapeDtypeStruct((S, 768), BF16),
                   jax.ShapeDtypeStruct((S, 384), BF16), jax.ShapeDtypeStruct((768, S), BF16),
                   jax.ShapeDtypeStruct((384, S), BF16)],
        compiler_params=_params(("parallel",)),
    )(za, za, gq, gkv, wq, wkv, *tabs)


def _mla_prep_bwd(za, gq, gkv, wq, wkv, tabs, dqf, dkf, dvm, name):
    S = za.shape[0]
    tm = _tile(S, 512)
    H = MLA_HEADS

    def body(zq_ref, gq_ref, gkv_ref, wq_ref, wkv_ref, c_ref, ck_ref, s1_ref, s2_ref, dqf_ref, dkf_ref, dvm_ref,
             dza_ref, dwq_ref, dwkv_ref, dgq_ref, dgkv_ref):
        i = pl.program_id(0)
        first = i == 0
        zq = zq_ref[...]
        c, s1, s2 = c_ref[...], s1_ref[...], s2_ref[...]
        lane = lax.broadcasted_iota(jnp.int32, (1, LANES), 1)
        nope = lane < MLA_NOPE
        rope = jnp.logical_and(lane >= ROPE_LANE0, lane < ROPE_LANE0 + MLA_ROPE)

        qa = zq[:, 0:256]
        qn = _rms(qa, gq_ref[...]).astype(BF16)
        dqf = dqf_ref[...]
        dq_pre = jnp.concatenate(
            [_rope_t(dqf[:, h * LANES:(h + 1) * LANES] * _MLA_SCALE, c, s1, s2) for h in range(H)], axis=1).astype(BF16)
        _accum_out(dwq_ref, first, _dot_tn(qn, dq_pre))
        dqa, dgq = _rms_bwd(_dot_nt(dq_pre, wq_ref[...]), qa, gq_ref[...])
        _accum_out(dgq_ref, first, dgq)

        kva = zq[:, 256:384]
        kvn = _rms(kva, gkv_ref[...]).astype(BF16)
        dkf = dkf_ref[...]
        parts = []
        dkt = jnp.zeros((tm, LANES), F32)
        for h in range(H):
            blk = dkf[:, h * LANES:(h + 1) * LANES]
            parts.append(jnp.where(nope, blk, 0.0))
            dkt = dkt + jnp.where(rope, blk, 0.0)
        dkv_pre = jnp.concatenate(parts + [dvm_ref[...]], axis=1).astype(BF16)
        _accum_out(dwkv_ref, first, _dot_tn(kvn, dkv_pre))
        dkva, dgkv = _rms_bwd(_dot_nt(dkv_pre, wkv_ref[...]), kva, gkv_ref[...])
        _accum_out(dgkv_ref, first, dgkv)

        dtail = _rope_t(dkt, ck_ref[...], s1, s2)
        dza_ref[...] = jnp.concatenate([dqa, dkva, jnp.zeros((tm, 256), F32), dtail], axis=1)

    row = lambda i: (i, 0)
    fix = lambda i: (0, 0)
    return pl.pallas_call(
        body, name=name, grid=(S // tm,),
        in_specs=[_bs((tm, 384), row), _bs((1, 256), fix), _bs((1, 128), fix), _bs((256, 768), fix), _bs((128, 1152), fix),
                  _bs((tm, 128), row), _bs((tm, 128), row), _bs((tm, 128), row), _bs((tm, 128), row),
                  _bs((tm, 768), row), _bs((tm, 768), row), _bs((tm, 384), row)],
        out_specs=[_bs((tm, ZA), row), _bs((256, 768), fix), _bs((128, 1152), fix), _bs((1, 256), fix), _bs((1, 128), fix)],
        out_shape=[jax.ShapeDtypeStruct((S, ZA), F32), jax.ShapeDtypeStruct((256, 768), F32),
                   jax.ShapeDtypeStruct((128, 1152), F32), jax.ShapeDtypeStruct((1, 256), F32),
                   jax.ShapeDtypeStruct((1, 128), F32)],
        compiler_params=_params(("arbitrary",)),
    )(za, gq, gkv, wq, wkv, *tabs, dqf, dkf, dvm)


def _head_views(qb, kb, r, dkb, sel):
    if dkb == LANES:
        sl = slice(r * LANES, (r + 1) * LANES)
        return qb[:, sl], kb[:, sl], kb[:, sl]
    return jnp.where(sel, qb, jnp.zeros_like(qb)), kb, jnp.where(sel, kb, jnp.zeros_like(kb))


def _attn_fwd_t(q_arr, k_arr, vt_arr, *, nhp, dkb, qoff, koff, vtoff, scale, cum, cumT, name, rider=None):
    S = q_arr.shape[0]
    T = _tile(S, 512)
    nq = S // T
    W = 2 * dkb
    bias = cum is not None

    def body(*refs):
        if bias:
            q_ref, k_ref, vt_ref, cq_ref, ck_ref, o_ref, lse_ref, m_s, l_s, acc_s = refs
        else:
            q_ref, k_ref, vt_ref, o_ref, lse_ref, m_s, l_s, acc_s = refs
        hp, qi, ki = pl.program_id(0), pl.program_id(1), pl.program_id(2)
        lo_lane = lax.broadcasted_iota(jnp.int32, (1, LANES), 1) < 64
        lo_row = lax.broadcasted_iota(jnp.int32, (LANES, 1), 0) < 64

        @pl.when(ki == 0)
        def _():
            m_s[...] = jnp.full_like(m_s, NEG)
            l_s[...] = jnp.zeros_like(l_s)
            acc_s[...] = jnp.zeros_like(acc_s)

        def step(masked):
            qb, kb, vtb = q_ref[...], k_ref[...], vt_ref[...]
            if masked:
                mask = lax.broadcasted_iota(jnp.int32, (T, T), 0) <= lax.broadcasted_iota(jnp.int32, (T, T), 1)
            if bias:
                li = lax.broadcasted_iota(jnp.int32, (T, LANES), 1)
                ckb = ck_ref[...]
            for r in range(2):
                sel = lo_lane if r == 0 else jnp.logical_not(lo_lane)
                rsel = lo_row if r == 0 else jnp.logical_not(lo_row)
                q, k, _ = _head_views(qb, kb, r, dkb, sel)
                if scale != 1.0:
                    q = q * jnp.asarray(scale, q.dtype)
                s = _dot_nt(k, q)
                if bias:
                    ck = jnp.sum(jnp.where(li == 8 * hp + r, ckb, 0.0), axis=1, keepdims=True)
                    s = s + (cq_ref[r:r + 1, :] - ck)
                if masked:
                    s = jnp.where(mask, s, NEG)
                m_prev = m_s[r:r + 1, :]
                m_new = jnp.maximum(m_prev, jnp.max(s, axis=0, keepdims=True))
                alpha = jnp.exp(m_prev - m_new)
                p = jnp.exp(s - m_new)
                l_s[r:r + 1, :] = alpha * l_s[r:r + 1, :] + jnp.sum(p, axis=0, keepdims=True)
                m_s[r:r + 1, :] = m_new
                pv = _dot(jnp.where(rsel, vtb, jnp.zeros_like(vtb)), p.astype(BF16))
                acc_s[...] = acc_s[...] * jnp.where(rsel, alpha, 1.0) + pv

        @pl.when(ki < qi)
        def _():
            step(False)

        @pl.when(ki == qi)
        def _():
            step(True)

        @pl.when(ki == nq - 1)
        def _():
            inv = jnp.where(lo_row, 1.0 / l_s[0:1, :], 1.0 / l_s[1:2, :])
            o_ref[...] = (acc_s[...] * inv).T.astype(BF16)
            used = lax.broadcasted_iota(jnp.int32, (8, T), 0) < 2
            lse_ref[...] = jnp.where(used, m_s[...] + jnp.log(jnp.where(used, l_s[...], 1.0)), 0.0)

    kmap = lambda hp, qi, ki: jnp.minimum(ki, qi)
    in_specs = [_bs((T, W), lambda hp, qi, ki: (qi, qoff + hp)),
                _bs((T, W), lambda hp, qi, ki: (kmap(hp, qi, ki), koff + hp)),
                _bs((LANES, T), lambda hp, qi, ki: (vtoff + hp, kmap(hp, qi, ki)))]
    args = [q_arr, k_arr, vt_arr]
    if bias:
        in_specs += [_bs((8, T), lambda hp, qi, ki: (hp, qi)), _bs((T, LANES), lambda hp, qi, ki: (kmap(hp, qi, ki), 0))]
        args += [cumT, cum]
    return _call(
        body, name=name, grid=(nhp, nq, nq),
        in_specs=in_specs,
        out_specs=[_bs((T, LANES), lambda hp, qi, ki: (qi, hp)), _bs((None, 8, T), lambda hp, qi, ki: (hp, 0, qi))],
        out_shape=[jax.ShapeDtypeStruct((S, nhp * LANES), BF16), jax.ShapeDtypeStruct((nhp, 8, S), F32)],
        scratch=[pltpu.VMEM((8, T), F32), pltpu.VMEM((8, T), F32), pltpu.VMEM((LANES, T), F32)],
        dims=("parallel", "parallel", "arbitrary"), args=args, rider=rider)


def _attn_dvec(o_arr, do_arr, nhp, name):
    S = o_arr.shape[0]
    T = _tile(S, 512)

    def body(o_ref, do_ref, d_ref):
        prod = do_ref[...].astype(F32) * o_ref[...].astype(F32)
        li = lax.broadcasted_iota(jnp.int32, (T, LANES), 1)
        d0 = jnp.sum(jnp.where(li < 64, prod, 0.0), axis=1, keepdims=True)
        d1 = jnp.sum(jnp.where(li >= 64, prod, 0.0), axis=1, keepdims=True)
        d_ref[...] = jnp.where(li == 0, d0, jnp.where(li == 1, d1, 0.0)).T[0:8, :]

    return pl.pallas_call(
        body, name=name, grid=(nhp, S // T),
        in_specs=[_bs((T, LANES), lambda hp, i: (i, hp)), _bs((T, LANES), lambda hp, i: (i, hp))],
        out_specs=_bs((None, 8, T), lambda hp, i: (hp, 0, i)),
        out_shape=jax.ShapeDtypeStruct((nhp, 8, S), F32),
        compiler_params=_params(("parallel", "parallel")),
    )(o_arr, do_arr)


def _attn_bwd_t(q_arr, k_arr, kt_arr, v_arr, do_arr, lse, dvec, *, nhp, dkb, qoff, koff, ktoff, voff, scale, cum, cumT,
                name, rider=None):
    S = q_arr.shape[0]
    T = _tile(S, 512)
    nq = S // T
    W = 2 * dkb
    bias = cum is not None

    def body(*refs):
        if bias:
            (q_ref, k_ref, kt_ref, v_ref, do_ref, lse_ref, dvec_ref, cq_ref, ck_ref,
             dq_ref, dk_ref, dv_ref, dcq_ref, dck_ref, dqt_s, dk_s, dv_s, dcq_s, dck_s) = refs
        else:
            (q_ref, k_ref, kt_ref, v_ref, do_ref, lse_ref, dvec_ref,
             dq_ref, dk_ref, dv_ref, dqt_s, dk_s, dv_s) = refs
        hp, ki, qi = pl.program_id(0), pl.program_id(1), pl.program_id(2)
        lo_lane = lax.broadcasted_iota(jnp.int32, (1, LANES), 1) < 64
        lo_row = lax.broadcasted_iota(jnp.int32, (LANES, 1), 0) < 64

        @pl.when(jnp.logical_and(ki == 0, qi == 0))
        def _():
            dqt_s[...] = jnp.zeros_like(dqt_s)
            if bias:
                dcq_s[...] = jnp.zeros_like(dcq_s)

        @pl.when(qi == 0)
        def _():
            dk_s[...] = jnp.zeros_like(dk_s)
            dv_s[...] = jnp.zeros_like(dv_s)
            if bias:
                dck_s[...] = jnp.zeros_like(dck_s)

        def step(masked):
            qb, kb, ktb, vb, dob = q_ref[...], k_ref[...], kt_ref[...], v_ref[...], do_ref[...]
            if masked:
                mask = lax.broadcasted_iota(jnp.int32, (T, T), 0) <= lax.broadcasted_iota(jnp.int32, (T, T), 1)
            if bias:
                li = lax.broadcasted_iota(jnp.int32, (T, LANES), 1)
                ckb = ck_ref[...]
            for r in range(2):
                sel = lo_lane if r == 0 else jnp.logical_not(lo_lane)
                rsel = lo_row if r == 0 else jnp.logical_not(lo_row)
                q, k, _ = _head_views(qb, kb, r, dkb, sel)
                if scale != 1.0:
                    q = q * jnp.asarray(scale, q.dtype)
                s = _dot_nt(k, q)
                if bias:
                    ck = jnp.sum(jnp.where(li == 8 * hp + r, ckb, 0.0), axis=1, keepdims=True)
                    s = s + (cq_ref[r:r + 1, :] - ck)
                p = jnp.exp(s - lse_ref[r:r + 1, :])
                if masked:
                    p = jnp.where(mask, p, 0.0)
                do_r = jnp.where(sel, dob, jnp.zeros_like(dob))
                dp = _dot_nt(vb, do_r)
                ds = p * (dp - dvec_ref[r:r + 1, :])
                pb = p.astype(BF16)
                dsb = ds.astype(BF16)
                dv_s[...] += _dot(pb, do_r)
                if dkb == LANES:
                    sl = slice(r * LANES, (r + 1) * LANES)
                    dk_s[:, sl] += _dot(dsb, q)
                    dqt_s[qi, sl, :] += _dot(ktb[sl, :], dsb) * scale
                else:
                    dk_s[...] += _dot(dsb, q)
                    dqt_s[qi] += _dot(jnp.where(rsel, ktb, jnp.zeros_like(ktb)), dsb) * scale
                if bias:
                    dcq_s[qi, r:r + 1, :] += jnp.sum(ds, axis=0, keepdims=True)
                    dck_s[...] -= jnp.where(li == 8 * hp + r, jnp.sum(ds, axis=1, keepdims=True), 0.0)

        @pl.when(qi > ki)
        def _():
            step(False)

        @pl.when(qi == ki)
        def _():
            step(True)

        @pl.when(qi == nq - 1)
        def _():
            dk_ref[...] = dk_s[...]
            dv_ref[...] = dv_s[...]
            if bias:
                dck_ref[...] = dck_s[...]

        @pl.when(jnp.logical_and(ki == nq - 1, qi == nq - 1))
        def _():
            for c in range(nq):
                dq_ref[c * T:(c + 1) * T, :] = dqt_s[c].T
                if bias:
                    dcq_ref[:, c * T:(c + 1) * T] = dcq_s[c]

    qmap = lambda hp, ki, qi: jnp.maximum(qi, ki)
    in_specs = [_bs((T, W), lambda hp, ki, qi: (qmap(hp, ki, qi), qoff + hp)),
                _bs((T, W), lambda hp, ki, qi: (ki, koff + hp)),
                _bs((W, T), lambda hp, ki, qi: (ktoff + hp, ki)),
                _bs((T, LANES), lambda hp, ki, qi: (ki, voff + hp)),
                _bs((T, LANES), lambda hp, ki, qi: (qmap(hp, ki, qi), hp)),
                _bs((None, 8, T), lambda hp, ki, qi: (hp, 0, qmap(hp, ki, qi))),
                _bs((None, 8, T), lambda hp, ki, qi: (hp, 0, qmap(hp, ki, qi)))]
    args = [q_arr, k_arr, kt_arr, v_arr, do_arr, lse, dvec]
    out_specs = [_bs((S, W), lambda hp, ki, qi: (0, hp)), _bs((T, W), lambda hp, ki, qi: (ki, hp)),
                 _bs((T, LANES), lambda hp, ki, qi: (ki, hp))]
    out_shape = [jax.ShapeDtypeStruct((S, nhp * W), F32), jax.ShapeDtypeStruct((S, nhp * W), F32),
                 jax.ShapeDtypeStruct((S, nhp * LANES), F32)]
    scratch = [pltpu.VMEM((nq, W, T), F32), pltpu.VMEM((T, W), F32), pltpu.VMEM((T, LANES), F32)]
    if bias:
        in_specs += [_bs((8, T), lambda hp, ki, qi: (hp, qmap(hp, ki, qi))), _bs((T, LANES), lambda hp, ki, qi: (ki, 0))]
        args += [cumT, cum]
        out_specs += [_bs((None, 8, S), lambda hp, ki, qi: (hp, 0, 0)), _bs((None, T, LANES), lambda hp, ki, qi: (hp, ki, 0))]
        out_shape += [jax.ShapeDtypeStruct((nhp, 8, S), F32), jax.ShapeDtypeStruct((nhp, S, LANES), F32)]
        scratch += [pltpu.VMEM((nq, 8, T), F32), pltpu.VMEM((T, LANES), F32)]
    return _call(body, name=name, grid=(nhp, nq, nq), in_specs=in_specs, out_specs=out_specs, out_shape=out_shape,
                 scratch=scratch, dims=("arbitrary", "arbitrary", "arbitrary"), args=args, rider=rider)


def _gate_lanes(shape):
    lane = lax.broadcasted_iota(jnp.int32, shape, 1)
    return jnp.logical_and(lane < 8 * (FOX_HEADS // 2), lane % 8 < 2)


def _fox_prep(za, b_row, name):
    S = za.shape[0]
    nrow = 8 * (FOX_HEADS // 2)

    def body(tail_ref, b_ref, cum_ref, cumt_ref):
        x = tail_ref[...] + b_ref[...]
        logf = jnp.minimum(x, 0.0) - jnp.log(1.0 + jnp.exp(-jnp.abs(x)))
        y = jnp.where(_gate_lanes((S, LANES)), logf, 0.0)
        row = lax.broadcasted_iota(jnp.int32, (S, LANES), 0)
        k = 1
        while k < S:
            y = y + jnp.where(row >= k, pltpu.roll(y, k, 0), 0.0)
            k *= 2
        cum_ref[...] = y
        cumt_ref[...] = y.T[0:nrow, :]

    return pl.pallas_call(
        body, name=name, grid=(1,),
        in_specs=[_bs((S, LANES), lambda i: (0, TAIL0 // LANES)), _bs((1, LANES), lambda i: (0, 0))],
        out_specs=[_bs((S, LANES), lambda i: (0, 0)), _bs((nrow, S), lambda i: (0, 0))],
        out_shape=[jax.ShapeDtypeStruct((S, LANES), F32), jax.ShapeDtypeStruct((nrow, S), F32)],
        compiler_params=_params(("arbitrary",)),
    )(za, b_row)


def _fox_prep_bwd(za, b_row, dcq, dck, name):
    S = za.shape[0]
    nhp = FOX_HEADS // 2
    nrow = 8 * nhp
    dcq2 = dcq.reshape(nrow, S)

    def body(tail_ref, b_ref, dcq_ref, dck_ref, dt_ref, db_ref):
        x = tail_ref[...] + b_ref[...]
        d = jnp.concatenate([dcq_ref[...], jnp.zeros((LANES - nrow, S), F32)], axis=0).T
        for hp in range(nhp):
            d = d + dck_ref[hp]
        row = lax.broadcasted_iota(jnp.int32, (S, LANES), 0)
        k = 1
        while k < S:
            d = d + jnp.where(row < S - k, pltpu.roll(d, S - k, 0), 0.0)
            k *= 2
        df = jnp.where(_gate_lanes((S, LANES)), d * jax.nn.sigmoid(-x), 0.0)
        dt_ref[...] = df
        db_ref[...] = jnp.sum(df, axis=0, keepdims=True)

    return pl.pallas_call(
        body, name=name, grid=(1,),
        in_specs=[_bs((S, LANES), lambda i: (0, TAIL0 // LANES)), _bs((1, LANES), lambda i: (0, 0)),
                  _bs((nrow, S), lambda i: (0, 0)), _bs((nhp, S, LANES), lambda i: (0, 0, 0))],
        out_specs=[_bs((S, LANES), lambda i: (0, 0)), _bs((1, LANES), lambda i: (0, 0))],
        out_shape=[jax.ShapeDtypeStruct((S, LANES), F32), jax.ShapeDtypeStruct((1, LANES), F32)],
        compiler_params=_params(("arbitrary",)),
    )(za, b_row, dcq2, dck)


def _pool_select(half, lane_lo, vals):
    return jnp.where(lane_lo, jnp.where(half == 0, vals[0], vals[2]), jnp.where(half == 0, vals[1], vals[3]))


def _pool_den(S, half, lane_lo):
    cnt = (lax.broadcasted_iota(jnp.int32, (S, LANES), 0) + 1).astype(F32)
    w = _pool_select(half, lane_lo, [float(x) for x in POOL_WINDOWS])
    return jnp.minimum(cnt, w)


def _pool_fwd(za, wbd, scale, name):
    S = za.shape[0]

    def body(u_ref, w_ref, sc_ref, y_ref, pd_ref):
        half = pl.program_id(0)
        u = u_ref[...]
        row = lax.broadcasted_iota(jnp.int32, (S, LANES), 0)
        lane_lo = lax.broadcasted_iota(jnp.int32, (S, LANES), 1) < POOL_GROUP
        sums = []
        acc = u
        k = 1
        while k < POOL_WINDOWS[-1]:
            acc = acc + jnp.where(row >= k, pltpu.roll(acc, k, 0), 0.0)
            sums.append(acc)
            k *= 2
        pooled = _pool_select(half, lane_lo, sums) / _pool_den(S, half, lane_lo)
        pd = (pooled - u).astype(BF16)
        pd_ref[...] = pd
        y_ref[...] = (_dot(pd, w_ref[...]) * sc_ref[...]).astype(BF16)

    return pl.pallas_call(
        body, name=name, grid=(2,),
        in_specs=[_bs((S, LANES), lambda i: (0, 384 // LANES + i)), _bs((None, LANES, LANES), lambda i: (i, 0, 0)),
                  _bs((1, LANES), lambda i: (0, i))],
        out_specs=[_bs((S, LANES), lambda i: (0, i)), _bs((S, LANES), lambda i: (0, i))],
        out_shape=[jax.ShapeDtypeStruct((S, POOL_WIDTH), BF16), jax.ShapeDtypeStruct((S, POOL_WIDTH), BF16)],
        compiler_params=_params(("parallel",)),
    )(za, wbd, scale)


def _pool_bwd(dyb, pd, wbd, scale, name):
    S = pd.shape[0]

    def body(dy_ref, pd_ref, w_ref, sc_ref, du_ref, dw_ref, dsc_ref):
        half = pl.program_id(0)
        dy = dy_ref[...]
        pd = pd_ref[...]
        w = w_ref[...]
        ypre = _dot(pd, w)
        dsc_ref[...] = jnp.sum(dy * ypre, axis=0, keepdims=True)
        dyp = (dy * sc_ref[...]).astype(BF16)
        dw_ref[...] = _dot_tn(pd, dyp)
        dpd = _dot_nt(dyp, w)
        row = lax.broadcasted_iota(jnp.int32, (S, LANES), 0)
        lane_lo = lax.broadcasted_iota(jnp.int32, (S, LANES), 1) < POOL_GROUP
        acc = dpd / _pool_den(S, half, lane_lo)
        sums = []
        k = 1
        while k < POOL_WINDOWS[-1]:
            acc = acc + jnp.where(row < S - k, pltpu.roll(acc, S - k, 0), 0.0)
            sums.append(acc)
            k *= 2
        du_ref[...] = _pool_select(half, lane_lo, sums) - dpd

    return pl.pallas_call(
        body, name=name, grid=(2,),
        in_specs=[_bs((S, LANES), lambda i: (0, i)), _bs((S, LANES), lambda i: (0, i)),
                  _bs((None, LANES, LANES), lambda i: (i, 0, 0)), _bs((1, LANES), lambda i: (0, i))],
        out_specs=[_bs((S, LANES), lambda i: (0, i)), _bs((None, LANES, LANES), lambda i: (i, 0, 0)),
                   _bs((1, LANES), lambda i: (0, i))],
        out_shape=[jax.ShapeDtypeStruct((S, POOL_WIDTH), F32), jax.ShapeDtypeStruct((2, LANES, LANES), F32),
                   jax.ShapeDtypeStruct((1, POOL_WIDTH), F32)],
        compiler_params=_params(("parallel",)),
    )(dyb, pd, wbd, scale)


def _mix_out_fwd(x, ya, yb, yc, w_out, name):
    S, D = x.shape
    tm = _tile(S, 512)
    K = w_out.shape[0]

    def body(x_ref, ya_ref, yb_ref, yc_ref, w_ref, xo_ref, yc_out):
        ycat = jnp.concatenate([ya_ref[...], yb_ref[...], yc_ref[...]], axis=1)
        yc_out[...] = ycat
        xo_ref[...] = x_ref[...] + _dot(ycat, w_ref[...])

    row = lambda i: (i, 0)
    return pl.pallas_call(
        body, name=name, grid=(S // tm,),
        in_specs=[_bs((tm, D), row), _bs((tm, 384), row), _bs((tm, 256), row), _bs((tm, 384), row),
                  _bs((K, D), lambda i: (0, 0))],
        out_specs=[_bs((tm, D), row), _bs((tm, K), row)],
        out_shape=[jax.ShapeDtypeStruct((S, D), F32), jax.ShapeDtypeStruct((S, K), BF16)],
        compiler_params=_params(("parallel",)),
    )(x, ya, yb, yc, w_out)


def _mix_out_bwd(dy, w_out, name):
    S, D = dy.shape
    tm = _tile(S, 512)
    K = w_out.shape[0]

    def body(dy_ref, w_ref, da_ref, db_ref, dc_ref):
        d = _dot_nt(dy_ref[...].astype(BF16), w_ref[...])
        da_ref[...] = d[:, 0:384].astype(BF16)
        db_ref[...] = d[:, 384:640]
        dc_ref[...] = d[:, 640:1024].astype(BF16)

    row = lambda i: (i, 0)
    return pl.pallas_call(
        body, name=name, grid=(S // tm,),
        in_specs=[_bs((tm, D), row), _bs((K, D), lambda i: (0, 0))],
        out_specs=[_bs((tm, 384), row), _bs((tm, 256), row), _bs((tm, 384), row)],
        out_shape=[jax.ShapeDtypeStruct((S, 384), BF16), jax.ShapeDtypeStruct((S, 256), F32),
                   jax.ShapeDtypeStruct((S, 384), BF16)],
        compiler_params=_params(("parallel",)),
    )(dy, w_out)


def _loss_head(x, gam, target, name):
    S, D = x.shape
    tm = _tile(S, 512)

    def body(x_ref, gam_ref, t_ref, dx_ref, dgam_ref, loss_ref):
        i = pl.program_id(0)
        xv = x_ref[...]
        err = _rms(xv, gam_ref[...]) - t_ref[...]
        part = 0.5 * jnp.sum(jnp.mean(err * err, axis=-1, keepdims=True), axis=0, keepdims=True)
        dxn, dgam = _rms_bwd(err * (1.0 / D), xv, gam_ref[...])
        dx_ref[...] = dxn
        _accum_out(dgam_ref, i == 0, dgam)
        _accum_out(loss_ref, i == 0, jnp.broadcast_to(part, (1, LANES)))

    row = lambda i: (i, 0)
    fix = lambda i: (0, 0)
    return pl.pallas_call(
        body, name=name, grid=(S // tm,),
        in_specs=[_bs((tm, D), row), _bs((1, D), fix), _bs((tm, D), row)],
        out_specs=[_bs((tm, D), row), _bs((1, D), fix), _bs((1, LANES), fix)],
        out_shape=[jax.ShapeDtypeStruct((S, D), F32), jax.ShapeDtypeStruct((1, D), F32),
                   jax.ShapeDtypeStruct((1, LANES), F32)],
        compiler_params=_params(("arbitrary",)),
    )(x, gam, target)


def _adam_math(g, w, m, v):
    m = ADAM_B1 * m + (1.0 - ADAM_B1) * g
    v = ADAM_B2 * v + (1.0 - ADAM_B2) * (g * g)
    m_hat = m / (1.0 - ADAM_B1 ** ADAM_STEP)
    v_hat = v / (1.0 - ADAM_B2 ** ADAM_STEP)
    delta = -ADAM_LR * (m_hat / (jnp.sqrt(v_hat) + ADAM_EPS) + ADAM_WD * w)
    return delta, m, v


def _adam_sum(recv, w, m, v, layer, prev, tr, name):
    L, R, C = w.shape
    Cp = recv.shape[2]
    tr = _tile(R, tr)

    def body(r_ref, w_ref, m_ref, v_ref, *rest):
        g_out, d_out, m_out, v_out = rest[len(rest) - 4:]
        g = r_ref[0, :, 0:C].astype(F32)
        for p in range(1, N_DEV):
            g = g + r_ref[p, :, 0:C].astype(F32)
        delta, mn, vn = _adam_math(g, w_ref[...], m_ref[...], v_ref[...])
        g_out[...] = g
        d_out[...] = delta
        m_out[...] = mn
        v_out[...] = vn

    blk = _bs((None, tr, C), lambda i: (layer, i, 0))
    shp = jax.ShapeDtypeStruct((L, R, C), F32)
    in_specs = [_bs((N_DEV, tr, Cp), lambda i: (0, i, 0)), blk, blk, blk]
    args = [recv, w, m, v]
    aliases = {}
    if prev is not None:
        in_specs += [HBM_SPEC] * 4
        args += list(prev)
        aliases = {4 + k: k for k in range(4)}
    return pl.pallas_call(
        body, name=name, grid=(R // tr,),
        in_specs=in_specs, out_specs=[blk, blk, blk, blk], out_shape=[shp, shp, shp, shp],
        input_output_aliases=aliases, compiler_params=_params(("parallel",)),
    )(*args)


def _dev_index(px, py, pc):
    return 4 * px + 2 * py + pc


class _GatherRider:
    def __init__(self, shards, out_shapes, views, zero_src=None, zero_views=()):
        self.n = len(shards)
        self.views = views
        self.zero_views = list(zero_views) if zero_src is not None else []
        self.srcs = list(shards) + ([zero_src] if self.zero_views else [])
        self.out_shapes = list(out_shapes)
        n, nz = self.n, len(self.zero_views)
        self.scratch = [pltpu.SemaphoreType.DMA((n, 7)), pltpu.SemaphoreType.DMA((n, 7)),
                        pltpu.SemaphoreType.DMA((n,)), pltpu.SemaphoreType.DMA((max(nz, 1),))]

    def _copies(self, ins, outs, sems):
        n = self.n
        send_sems, recv_sems, local_sems, zero_sems = sems
        x, y, c = lax.axis_index("x"), lax.axis_index("y"), lax.axis_index("c")
        me, sibling = (x, y, c), (x, y, 1 - c)
        chips = [(1 - x, y), (x, 1 - y), (1 - x, 1 - y)]

        def rows(a, blk):
            return self.views[a](outs[a], _dev_index(*blk))

        def copy(a, k, blk, to, src=None):
            return pltpu.make_async_remote_copy(
                src_ref=rows(a, blk) if src is None else src, dst_ref=rows(a, blk),
                send_sem=send_sems.at[a, k], recv_sem=recv_sems.at[a, k], device_id=to, device_id_type=MESH_ID)

        local = [pltpu.make_async_copy(ins[a], rows(a, me), local_sems.at[a]) for a in range(n)]
        local += [pltpu.make_async_copy(ins[n], view(outs[a]), zero_sems.at[i])
                  for i, (a, view) in enumerate(self.zero_views)]
        first = []
        for a in range(n):
            first.append(copy(a, 0, me, sibling, src=ins[a]))
            first += [copy(a, 1 + j, me, (*chip, c), src=ins[a]) for j, chip in enumerate(chips)]
        over_ici = [[copy(a, 1 + j, (*chip, c), me) for a in range(n)] for j, chip in enumerate(chips)]
        passed = [[copy(a, 4 + j, (*chip, c), sibling) for a in range(n)] for j, chip in enumerate(chips)]
        from_sibling = [copy(a, 0, sibling, me) for a in range(n)]
        from_sibling += [copy(a, 4 + j, (*chip, 1 - c), me) for a in range(n) for j, chip in enumerate(chips)]
        return local, first, over_ici, passed, from_sibling

    def begin(self, ins, outs, sems):
        local, first, _, _, _ = self._copies(ins, outs, sems)
        for cp in local + first:
            cp.start()

    def end(self, ins, outs, sems):
        local, first, over_ici, passed, from_sibling = self._copies(ins, outs, sems)
        for arrived, onward in zip(over_ici, passed):
            for cp, fwd in zip(arrived, onward):
                cp.wait_recv()
                fwd.start()
        for cp in from_sibling:
            cp.wait_recv()
        for cp in first + [fwd for onward in passed for fwd in onward]:
            cp.wait_send()
        for cp in local:
            cp.wait()


class _ScatterRider:
    _MASKS = [(kx, ky, kc) for kx in (0, 1) for ky in (0, 1) for kc in (0, 1)][1:]

    def __init__(self, srcs, out_shapes, src_of, dst_at):
        self.n = len(srcs)
        self.srcs = list(srcs)
        self.out_shapes = list(out_shapes)
        self.src_of = src_of
        self.dst_at = dst_at
        n = self.n
        self.scratch = [pltpu.SemaphoreType.DMA((n, 7)), pltpu.SemaphoreType.DMA((n, 7)), pltpu.SemaphoreType.DMA((n,))]

    def _copies(self, ins, outs, sems):
        send_sems, recv_sems, local_sems = sems
        x, y, c = lax.axis_index("x"), lax.axis_index("y"), lax.axis_index("c")
        my = _dev_index(x, y, c)
        peers = [(1 - x if kx else x, 1 - y if ky else y, 1 - c if kc else c) for kx, ky, kc in self._MASKS]

        def send(i, k, to):
            return pltpu.make_async_remote_copy(
                src_ref=self.src_of[i](ins[i], _dev_index(*to)), dst_ref=self.dst_at[i](outs[i], my),
                send_sem=send_sems.at[i, k], recv_sem=recv_sems.at[i, k], device_id=to, device_id_type=MESH_ID)

        def arrival(i, k, frm):
            slot = self.dst_at[i](outs[i], _dev_index(*frm))
            return pltpu.make_async_remote_copy(
                src_ref=slot, dst_ref=slot, send_sem=send_sems.at[i, k], recv_sem=recv_sems.at[i, k],
                device_id=frm, device_id_type=MESH_ID)

        local = [pltpu.make_async_copy(self.src_of[i](ins[i], my), self.dst_at[i](outs[i], my), local_sems.at[i])
                 for i in range(self.n)]
        sends = [send(i, k, to) for k, to in enumerate(peers) for i in range(self.n)]
        arrivals = [arrival(i, k, frm) for k, frm in enumerate(peers) for i in range(self.n)]
        return local, sends, arrivals

    def begin(self, ins, outs, sems):
        local, sends, _ = self._copies(ins, outs, sems)
        for cp in local + sends:
            cp.start()

    def end(self, ins, outs, sems):
        local, sends, arrivals = self._copies(ins, outs, sems)
        for cp in arrivals:
            cp.wait_recv()
        for cp in sends:
            cp.wait_send()
        for cp in local:
            cp.wait()


def _comm_call(rider, name):
    k_in = len(rider.srcs)
    k_out = len(rider.out_shapes)

    def body(*refs):
        ins, outs, sems = refs[:k_in], refs[k_in:k_in + k_out], refs[k_in + k_out:]
        rider.begin(ins, outs, sems)
        rider.end(ins, outs, sems)

    return pl.pallas_call(
        body, name=name, in_specs=[HBM_SPEC] * k_in, out_specs=[HBM_SPEC] * k_out, out_shape=rider.out_shapes,
        scratch_shapes=rider.scratch, compiler_params=pltpu.CompilerParams(has_side_effects=True),
    )(*rider.srcs)


def _pad_w_in(w):
    take = jnp.take(w, np.maximum(_IN_PERM, 0), axis=-1)
    return jnp.where(_IN_PERM >= 0, take, jnp.zeros_like(take))


def _unpad_w_in(g):
    return jnp.take(g, _IN_INV, axis=-1)


def _small_pack(w_q_b, w_kv_b):
    a = jnp.pad(w_q_b, ((0, 0), (0, 0), (0, LANES - w_q_b.shape[2])))
    b = jnp.pad(w_kv_b, ((0, 0), (0, 0), (0, LANES - w_kv_b.shape[2])))
    return jnp.concatenate([a, b], axis=1)


def _small_unpack(p, cq, ckv):
    return p[:, 0:MLA_Q_RANK, 0:cq], p[:, MLA_Q_RANK:, 0:ckv]


def _mla_weights(wsm):
    H = MLA_HEADS
    cq = H * (MLA_NOPE + MLA_ROPE) // N_DEV
    ckv = H * (MLA_NOPE + MLA_V) // N_DEV
    wq = wsm[:, 0:MLA_Q_RANK, 0:cq].transpose(1, 0, 2).reshape(MLA_Q_RANK, H, MLA_NOPE + MLA_ROPE)
    wq = jnp.pad(wq, ((0, 0), (0, 0), (0, HEAD_BLOCK - MLA_NOPE - MLA_ROPE))).reshape(MLA_Q_RANK, H * HEAD_BLOCK)
    wkv = wsm[:, MLA_Q_RANK:, 0:ckv].transpose(1, 0, 2).reshape(MLA_KV_RANK, H, MLA_NOPE + MLA_V)
    wk = jnp.pad(wkv[:, :, 0:MLA_NOPE], ((0, 0), (0, 0), (0, HEAD_BLOCK - MLA_NOPE))).reshape(MLA_KV_RANK, H * HEAD_BLOCK)
    wv = wkv[:, :, MLA_NOPE:].reshape(MLA_KV_RANK, H * MLA_V)
    return wq, jnp.concatenate([wk, wv], axis=1)


def _mla_grads_to_blocks(dwq, dwkv):
    H = MLA_HEADS
    gq = dwq.reshape(MLA_Q_RANK, H, HEAD_BLOCK)[:, :, 0:MLA_NOPE + MLA_ROPE].reshape(MLA_Q_RANK, N_DEV, -1)
    gk = dwkv[:, 0:H * HEAD_BLOCK].reshape(MLA_KV_RANK, H, HEAD_BLOCK)[:, :, 0:MLA_NOPE]
    gv = dwkv[:, H * HEAD_BLOCK:].reshape(MLA_KV_RANK, H, MLA_V)
    gkv = jnp.concatenate([gk, gv], axis=2).reshape(MLA_KV_RANK, N_DEV, -1)
    return _small_pack(gq.transpose(1, 0, 2), gkv.transpose(1, 0, 2)).astype(BF16)


def _pool_blockdiag(pool_w):
    z = jnp.zeros((POOL_GROUP, POOL_GROUP), pool_w.dtype)
    halves = [jnp.concatenate([jnp.concatenate([pool_w[2 * i], z], axis=1),
                               jnp.concatenate([z, pool_w[2 * i + 1]], axis=1)], axis=0) for i in range(2)]
    return jnp.stack(halves)


def _pool_blockdiag_t(dw):
    g = POOL_GROUP
    return jnp.stack([dw[0, 0:g, 0:g], dw[0, g:, g:], dw[1, 0:g, 0:g], dw[1, g:, g:]])


def _gate_row(b):
    return jnp.zeros((LANES,), b.dtype).at[_F_LANES].set(b).reshape(1, LANES)


_SMALL = ("ffn1_norm", "mix_norm", "q_a_norm", "kv_a_norm", "pool_w", "pool_scale", "fox_b_f", "ffn2_norm", "final_norm")


def _pack_small(tree):
    rows, recipe = [], []
    for name in _SMALL:
        a = tree[name]
        flat = a.reshape(-1)
        n = flat.shape[0]
        nrow = -(-n // (8 * LANES)) * 8
        flat = jnp.pad(flat, (0, nrow * LANES - n))
        rows.append(flat.reshape(nrow, LANES))
        recipe.append((name, a.shape, n, nrow))
    return jnp.concatenate(rows, axis=0), recipe


def _unpack_small(packed, recipe):
    out, r0 = {}, 0
    for name, shape, n, nrow in recipe:
        out[name] = packed[r0:r0 + nrow].reshape(-1)[0:n].reshape(shape)
        r0 += nrow
    return out


def _adam_small(packs, w, m, v, name):
    R = w.shape[0]

    def body(p_ref, w_ref, m_ref, v_ref, g_out, d_out, m_out, v_out):
        g = p_ref[0]
        for p in range(1, N_DEV):
            g = g + p_ref[p]
        delta, mn, vn = _adam_math(g, w_ref[...], m_ref[...], v_ref[...])
        g_out[...] = g
        d_out[...] = delta
        m_out[...] = mn
        v_out[...] = vn

    blk = _bs((R, LANES), lambda i: (0, 0))
    shp = jax.ShapeDtypeStruct((R, LANES), F32)
    return pl.pallas_call(
        body, name=name, grid=(1,),
        in_specs=[_bs((N_DEV, R, LANES), lambda i: (0, 0, 0)), blk, blk, blk],
        out_specs=[blk, blk, blk, blk], out_shape=[shp, shp, shp, shp],
        compiler_params=_params(("arbitrary",)),
    )(packs, w, m, v)


_GATHER_PLAN = {
    ("first", 0): (("ffn1_w_gu", 0), ("ffn1_w_down", 0)),
    ("ffn1_fwd", 0): (("w_in", 0), ("w_small", 0), ("w_out", 0), ("ffn2_w_down", 0)),
    ("mla_attn_fwd", 0): (("ffn2_w_gu", 0), ("ffn1_w_gu", 1), ("ffn1_w_down", 1)),
    ("fox_attn_fwd", 0): (("w_in", 1), ("w_small", 1), ("w_out", 1), ("ffn2_w_gu", 1), ("ffn2_w_down", 1)),
}
_SCATTER_PLAN = {
    ("mla_attn_bwd", 1): (("ffn2_w_gu", 1), ("ffn2_w_down", 1)),
    ("fox_attn_bwd", 1): (("w_out", 1),),
    ("ffn1_bwd", 1): (("w_in", 1), ("w_small", 1)),
    ("ffn2_bwd", 0): (("ffn1_w_gu", 1),),
    ("mla_attn_bwd", 0): (("ffn1_w_down", 1), ("ffn2_w_gu", 0)),
    ("fox_attn_bwd", 0): (("ffn2_w_down", 0), ("w_out", 0)),
    ("ffn1_bwd", 0): (("w_in", 0), ("w_small", 0)),
    ("ffn1_dwgu", 0): (("ffn1_w_down", 0),),
    ("last", 0): (("ffn1_w_gu", 0),),
}


class _Exchange:
    def __init__(self, shards, D, n_pad, f_sh, r_in, r_out):
        self.shards = shards
        self.D, self.n_pad, self.f_sh, self.r_in, self.r_out = D, n_pad, f_sh, r_in, r_out
        self.nj = N_DEV // 2
        self.n_zero = n_pad - 2 * f_sh
        self.zero_src = jnp.zeros((self.n_zero, D), BF16) if self.n_zero else None
        self.weights, self.grads, self.recv = {}, {}, {}

    def _half(self, ref, p):
        f_sh = self.f_sh
        return ref.at[p // 2, pl.ds(pl.multiple_of((p % 2) * f_sh, 16), f_sh)]

    def _gathered_shape(self, kind):
        D, n_pad = self.D, self.n_pad
        return {"w_gu": (N_DEV, D, n_pad), "w_down": (self.nj, n_pad, D), "w_in": (N_DEV, self.r_in, N_PAD),
                "w_small": (N_DEV, MLA_Q_RANK + MLA_KV_RANK, LANES), "w_out": (N_DEV, self.r_out, D)}[kind]

    def _recv_shape(self, kind):
        D, n_pad = self.D, self.n_pad
        return {"w_gu": (N_DEV, D, n_pad), "w_down": (N_DEV, self.f_sh, D), "w_in": (N_DEV, self.r_in, N_IN),
                "w_small": (N_DEV, MLA_Q_RANK + MLA_KV_RANK, LANES), "w_out": (N_DEV, self.r_out, D)}[kind]

    @staticmethod
    def _kind(name):
        return name[5:] if name.startswith("ffn") else name

    def gather_rider(self, call, l):
        keys = _GATHER_PLAN.get((call, l))
        if not keys:
            return None
        by_dev = lambda ref, p: ref.at[p]
        shards, shapes, views, zero_views = [], [], [], []
        for a, key in enumerate(keys):
            kind = self._kind(key[0])
            shards.append(self.shards[key])
            shapes.append(jax.ShapeDtypeStruct(self._gathered_shape(kind), BF16))
            views.append(self._half if kind == "w_down" else by_dev)
            if kind == "w_down" and self.n_zero:
                zero_views += [(a, (lambda ref, j=j: ref.at[j, pl.ds(2 * self.f_sh, self.n_zero)])) for j in range(self.nj)]
        return _GatherRider(shards, shapes, views, self.zero_src, zero_views)

    def gathered(self, call, l, outs):
        for key, w in zip(_GATHER_PLAN.get((call, l), ()), outs):
            kind = self._kind(key[0])
            if kind == "w_gu":
                w = w.reshape(2, self.nj, self.D, self.n_pad)
            elif kind in ("w_in", "w_out"):
                w = w.reshape((N_DEV * w.shape[1],) + w.shape[2:])
            self.weights[key] = w

    def scatter_rider(self, call, l, pack=None):
        keys = _SCATTER_PLAN.get((call, l))
        if not keys:
            return None
        by_dev = lambda ref, p: ref.at[p]
        srcs, shapes, src_of = [], [], []
        for key in keys:
            kind = self._kind(key[0])
            srcs.append(self.grads[key])
            shapes.append(jax.ShapeDtypeStruct(self._recv_shape(kind), BF16))
            src_of.append(self._half if kind == "w_down" else by_dev)
        if pack is not None:
            srcs.append(pack)
            shapes.append(jax.ShapeDtypeStruct((N_DEV,) + pack.shape, pack.dtype))
            src_of.append(lambda ref, p: ref)
        return _ScatterRider(srcs, shapes, src_of, [by_dev] * len(srcs))

    def scattered(self, call, l, outs):
        for key, r in zip(_SCATTER_PLAN.get((call, l), ()), outs):
            self.recv[key] = r


def _local_step(x, target, ex, small):
    S, D = x.shape
    tabs = _rope_tables(S)
    nhp_a, nhp_c = MLA_HEADS // 2, FOX_HEADS // 2
    fox_scale = 1.0 / math.sqrt(FOX_HEAD_DIM)
    nj = ex.nj
    ex.gathered("first", 0, _comm_call(ex.gather_rider("first", 0), "gather_first"))
    saved = []
    for l in range(DEPTH):
        s = {}
        s["x0"] = x
        wgu1, wd1 = ex.weights[("ffn1_w_gu", l)], ex.weights[("ffn1_w_down", l)]
        (x1, s["h1"], s["gu1"]), got = _ffn_fwd(x, small["ffn1_norm"][l][None], wgu1, wd1, f"ffn1_fwd_l{l}",
                                               rider=ex.gather_rider("ffn1_fwd", l))
        ex.gathered("ffn1_fwd", l, got)
        s["x1"] = x1
        w_in = ex.weights[("w_in", l)]
        s["h2"], za, zf, zkvt = _mix_in_fwd(x1, small["mix_norm"][l][None], w_in, f"mix_in_fwd_l{l}")
        s["za"], s["zf"], s["zkvt"] = za, zf, zkvt
        wq, wkv = _mla_weights(ex.weights[("w_small", l)])
        s["wq"], s["wkv"] = wq, wkv
        gq, gkv = small["q_a_norm"][l][None], small["kv_a_norm"][l][None]
        qf, kf, vm, kft, vmt = _mla_prep(za, gq, gkv, wq, wkv, tabs, f"mla_prep_l{l}")
        s["qf"], s["kf"], s["vm"], s["kft"] = qf, kf, vm, kft
        (ya, lse_a), got = _attn_fwd_t(qf, kf, vmt, nhp=nhp_a, dkb=LANES, qoff=0, koff=0, vtoff=0, scale=1.0,
                                       cum=None, cumT=None, name=f"mla_attn_fwd_l{l}",
                                       rider=ex.gather_rider("mla_attn_fwd", l))
        ex.gathered("mla_attn_fwd", l, got)
        s["ya"], s["lse_a"] = ya, lse_a
        b_row = _gate_row(small["fox_b_f"][l])
        s["b_row"] = b_row
        cum, cumT = _fox_prep(za, b_row, f"fox_prep_l{l}")
        s["cum"], s["cumT"] = cum, cumT
        (yc, lse_c), got = _attn_fwd_t(zf, zf, zkvt, nhp=nhp_c, dkb=64, qoff=0, koff=nhp_c, vtoff=nhp_c, scale=fox_scale,
                                       cum=cum, cumT=cumT, name=f"fox_attn_fwd_l{l}",
                                       rider=ex.gather_rider("fox_attn_fwd", l))
        ex.gathered("fox_attn_fwd", l, got)
        s["yc"], s["lse_c"] = yc, lse_c
        wbd = _pool_blockdiag(small["pool_w"][l]).astype(BF16)
        s["wbd"] = wbd
        psc = small["pool_scale"][l][None]
        yb, s["pd"] = _pool_fwd(za, wbd, psc, f"pool_fwd_l{l}")
        w_out = ex.weights[("w_out", l)]
        x2, s["ycat"] = _mix_out_fwd(x1, ya, yb, yc, w_out, f"mix_out_fwd_l{l}")
        s["x2"] = x2
        wgu2, wd2 = ex.weights[("ffn2_w_gu", l)], ex.weights[("ffn2_w_down", l)]
        (x, s["h3"], s["gu2"]), _ = _ffn_fwd(x2, small["ffn2_norm"][l][None], wgu2, wd2, f"ffn2_fwd_l{l}")
        saved.append(s)

    dx, d_final, loss = _loss_head(x, small["final_norm"][None], target, "loss_head")

    small_grads = [None] * DEPTH
    for l in reversed(range(DEPTH)):
        s = saved[l]
        g = {}
        wgu2, wd2 = ex.weights[("ffn2_w_gu", l)], ex.weights[("ffn2_w_down", l)]
        dy3 = dx
        (dx, g["ffn2_norm"], dgu, ah), got = _ffn_bwd(dy3, s["x2"], small["ffn2_norm"][l][None], s["gu2"], wgu2, wd2,
                                                      f"ffn2_bwd_l{l}", rider=ex.scatter_rider("ffn2_bwd", l))
        ex.scattered("ffn2_bwd", l, got)
        ex.grads[("ffn2_w_gu", l)] = _mm_tn(s["h3"][None], dgu.reshape((2 * nj,) + dgu.shape[2:]), 2 * nj, lambda p: 0,
                                            lambda p: p, 1024, 768, f"ffn2_dwgu_l{l}")
        ex.grads[("ffn2_w_down", l)] = _mm_tn(ah, dy3[None], nj, lambda p: p, lambda p: 0, 768, 1024, f"ffn2_dwd_l{l}")

        w_out = ex.weights[("w_out", l)]
        dya, dyb, dyc = _mix_out_bwd(dx, w_out, f"mix_out_bwd_l{l}")
        dw_out = _mm_tn(s["ycat"][None], dx[None], 1, lambda p: 0, lambda p: 0, 1024, 1024, f"dwout_l{l}")[0]
        ex.grads[("w_out", l)] = dw_out.reshape(N_DEV, ex.r_out, D)

        dvec_a = _attn_dvec(s["ya"], dya, nhp_a, f"mla_dvec_l{l}")
        (dqf, dkf, dvm), got = _attn_bwd_t(s["qf"], s["kf"], s["kft"], s["vm"], dya, s["lse_a"], dvec_a, nhp=nhp_a,
                                           dkb=LANES, qoff=0, koff=0, ktoff=0, voff=0, scale=1.0, cum=None, cumT=None,
                                           name=f"mla_attn_bwd_l{l}", rider=ex.scatter_rider("mla_attn_bwd", l))
        ex.scattered("mla_attn_bwd", l, got)
        zf = s["zf"]
        dvec_c = _attn_dvec(s["yc"], dyc, nhp_c, f"fox_dvec_l{l}")
        (dqc, dkc, dvc, dcq, dck), got = _attn_bwd_t(zf, zf, s["zkvt"], zf, dyc, s["lse_c"], dvec_c, nhp=nhp_c, dkb=64,
                                                     qoff=0, koff=nhp_c, ktoff=0, voff=2 * nhp_c, scale=fox_scale,
                                                     cum=s["cum"], cumT=s["cumT"], name=f"fox_attn_bwd_l{l}",
                                                     rider=ex.scatter_rider("fox_attn_bwd", l))
        ex.scattered("fox_attn_bwd", l, got)
        dtail_f, db = _fox_prep_bwd(s["za"], s["b_row"], dcq, dck, f"fox_prep_bwd_l{l}")
        g["fox_b_f"] = db[0, _F_LANES]
        psc = small["pool_scale"][l][None]
        du, dwbd, dpsc = _pool_bwd(dyb, s["pd"], s["wbd"], psc, f"pool_bwd_l{l}")
        g["pool_w"] = _pool_blockdiag_t(dwbd)
        g["pool_scale"] = dpsc[0]
        gq, gkv = small["q_a_norm"][l][None], small["kv_a_norm"][l][None]
        dza, dwq, dwkv, dgq, dgkv = _mla_prep_bwd(s["za"], gq, gkv, s["wq"], s["wkv"], tabs, dqf, dkf, dvm,
                                                   f"mla_prep_bwd_l{l}")
        g["q_a_norm"], g["kv_a_norm"] = dgq[0], dgkv[0]
        ex.grads[("w_small", l)] = _mla_grads_to_blocks(dwq, dwkv)
        w_in = ex.weights[("w_in", l)]
        dx, g["mix_norm"], dz = _mix_in_bwd(dx, s["x1"], small["mix_norm"][l][None], dza, du, dtail_f, dqc, dkc, dvc,
                                            w_in, f"mix_in_bwd_l{l}")
        dw_in = _unpad_w_in(_mm_tn(s["h2"][None], dz[None], 1, lambda p: 0, lambda p: 0, 1024, 640, f"dwin_l{l}")[0])
        ex.grads[("w_in", l)] = dw_in.reshape(N_DEV, ex.r_in, N_IN)

        wgu1, wd1 = ex.weights[("ffn1_w_gu", l)], ex.weights[("ffn1_w_down", l)]
        dy1 = dx
        (dx, g["ffn1_norm"], dgu, ah), got = _ffn_bwd(dy1, s["x0"], small["ffn1_norm"][l][None], s["gu1"], wgu1, wd1,
                                                      f"ffn1_bwd_l{l}", rider=ex.scatter_rider("ffn1_bwd", l))
        ex.scattered("ffn1_bwd", l, got)
        ex.grads[("ffn1_w_down", l)] = _mm_tn(ah, dy1[None], nj, lambda p: p, lambda p: 0, 768, 1024, f"ffn1_dwd_l{l}")
        rider = ex.scatter_rider("ffn1_dwgu", l)
        dwgu = _mm_tn(s["h1"][None], dgu.reshape((2 * nj,) + dgu.shape[2:]), 2 * nj, lambda p: 0, lambda p: p,
                      1024, 768, f"ffn1_dwgu_l{l}", rider=rider)
        if rider is not None:
            dwgu, got = dwgu
            ex.scattered("ffn1_dwgu", l, got)
        ex.grads[("ffn1_w_gu", l)] = dwgu
        for k in ("ffn1_norm", "ffn2_norm", "mix_norm"):
            g[k] = g[k][0]
        small_grads[l] = g
    return loss, dx, small_grads, d_final[0]


_BIG = ("ffn1_w_gu", "ffn1_w_down", "w_in", "w_small", "w_out", "ffn2_w_gu", "ffn2_w_down")


def _pad_cols(w, n):
    return jnp.pad(w, ((0, 0),) * (w.ndim - 1) + ((0, n - w.shape[-1]),))


def kernel(x, ffn1_norm, ffn1_w_gu, ffn1_w_down, mix_norm, w_in, q_a_norm, w_q_b, kv_a_norm, w_kv_b, pool_w, pool_scale, fox_b_f, w_out, ffn2_norm, ffn2_w_gu, ffn2_w_down, final_norm, loss_target, m_ffn1_norm, m_ffn1_w_gu, m_ffn1_w_down, m_mix_norm, m_w_in, m_q_a_norm, m_w_q_b, m_kv_a_norm, m_w_kv_b, m_pool_w, m_pool_scale, m_fox_b_f, m_w_out, m_ffn2_norm, m_ffn2_w_gu, m_ffn2_w_down, m_final_norm, v_ffn1_norm, v_ffn1_w_gu, v_ffn1_w_down, v_mix_norm, v_w_in, v_q_a_norm, v_w_q_b, v_kv_a_norm, v_w_kv_b, v_pool_w, v_pool_scale, v_fox_b_f, v_w_out, v_ffn2_norm, v_ffn2_w_gu, v_ffn2_w_down, v_final_norm):
    W = dict(ffn1_norm=ffn1_norm, ffn1_w_gu=ffn1_w_gu, ffn1_w_down=ffn1_w_down, mix_norm=mix_norm, w_in=w_in,
             q_a_norm=q_a_norm, w_q_b=w_q_b, kv_a_norm=kv_a_norm, w_kv_b=w_kv_b, pool_w=pool_w, pool_scale=pool_scale,
             fox_b_f=fox_b_f, w_out=w_out, ffn2_norm=ffn2_norm, ffn2_w_gu=ffn2_w_gu, ffn2_w_down=ffn2_w_down,
             final_norm=final_norm)
    M = dict(ffn1_norm=m_ffn1_norm, ffn1_w_gu=m_ffn1_w_gu, ffn1_w_down=m_ffn1_w_down, mix_norm=m_mix_norm, w_in=m_w_in,
             q_a_norm=m_q_a_norm, w_q_b=m_w_q_b, kv_a_norm=m_kv_a_norm, w_kv_b=m_w_kv_b, pool_w=m_pool_w,
             pool_scale=m_pool_scale, fox_b_f=m_fox_b_f, w_out=m_w_out, ffn2_norm=m_ffn2_norm, ffn2_w_gu=m_ffn2_w_gu,
             ffn2_w_down=m_ffn2_w_down, final_norm=m_final_norm)
    V = dict(ffn1_norm=v_ffn1_norm, ffn1_w_gu=v_ffn1_w_gu, ffn1_w_down=v_ffn1_w_down, mix_norm=v_mix_norm, w_in=v_w_in,
             q_a_norm=v_q_a_norm, w_q_b=v_w_q_b, kv_a_norm=v_kv_a_norm, w_kv_b=v_w_kv_b, pool_w=v_pool_w,
             pool_scale=v_pool_scale, fox_b_f=v_fox_b_f, w_out=v_w_out, ffn2_norm=v_ffn2_norm, ffn2_w_gu=v_ffn2_w_gu,
             ffn2_w_down=v_ffn2_w_down, final_norm=v_final_norm)
    L, D, n_sh = ffn1_w_gu.shape
    f_sh = ffn1_w_down.shape[1]
    assert n_sh == 2 * f_sh and L == DEPTH
    n_pad = -(-n_sh // LANES) * LANES
    r_in, r_out = w_in.shape[1], w_out.shape[1]

    big_shards = dict(
        ffn1_w_gu=_pad_cols(ffn1_w_gu, n_pad).astype(BF16), ffn1_w_down=ffn1_w_down.astype(BF16),
        w_in=_pad_w_in(w_in).astype(BF16), w_small=_small_pack(w_q_b, w_kv_b).astype(BF16), w_out=w_out.astype(BF16),
        ffn2_w_gu=_pad_cols(ffn2_w_gu, n_pad).astype(BF16), ffn2_w_down=ffn2_w_down.astype(BF16))
    ex = _Exchange({(k, l): big_shards[k][l] for k in _BIG for l in range(L)}, D, n_pad, f_sh, r_in, r_out)

    small = {k: W[k] for k in _SMALL}
    loss, dx, grads, d_final = _local_step(x[0], loss_target[0], ex, small)

    small_g = {k: jnp.stack([grads[l][k] for l in range(L)]) for k in _SMALL if k != "final_norm"}
    small_g["final_norm"] = d_final
    pack_g, recipe = _pack_small(small_g)
    n_small = pack_g.shape[0]
    loss_row = -(-n_small // 8) * 8
    pack_g = jnp.concatenate([pack_g, jnp.zeros((loss_row - n_small, LANES), F32), jnp.broadcast_to(loss, (8, LANES))],
                             axis=0)
    *got, packs = _comm_call(ex.scatter_rider("last", 0, pack=pack_g), "scatter_last")
    ex.scattered("last", 0, got)

    out = {}
    sm_w, sm_m, sm_v = (_small_pack(t["w_q_b"], t["w_kv_b"]) for t in (W, M, V))
    big = [("ffn1_w_gu", W["ffn1_w_gu"], M["ffn1_w_gu"], V["ffn1_w_gu"], 256),
           ("ffn1_w_down", W["ffn1_w_down"], M["ffn1_w_down"], V["ffn1_w_down"], 352),
           ("w_in", W["w_in"], M["w_in"], V["w_in"], 128),
           ("w_small", sm_w, sm_m, sm_v, 384),
           ("w_out", W["w_out"], M["w_out"], V["w_out"], 128),
           ("ffn2_w_gu", W["ffn2_w_gu"], M["ffn2_w_gu"], V["ffn2_w_gu"], 256),
           ("ffn2_w_down", W["ffn2_w_down"], M["ffn2_w_down"], V["ffn2_w_down"], 352)]
    for k, w_, m_, v_, tr in big:
        res = None
        for l in range(L):
            res = _adam_sum(ex.recv[(k, l)], w_, m_, v_, l, res, tr, f"adam_{k}_l{l}")
        if k == "w_small":
            cq, ckv = w_q_b.shape[2], w_kv_b.shape[2]
            parts = [_small_unpack(r, cq, ckv) for r in res]
            out["w_q_b"] = [p[0] for p in parts]
            out["w_kv_b"] = [p[1] for p in parts]
        else:
            out[k] = res

    pw, _ = _pack_small({k: W[k] for k in _SMALL})
    pm, _ = _pack_small({k: M[k] for k in _SMALL})
    pv, _ = _pack_small({k: V[k] for k in _SMALL})
    extra = ((0, loss_row + 8 - n_small), (0, 0))
    res = _adam_small(packs, jnp.pad(pw, extra), jnp.pad(pm, extra), jnp.pad(pv, extra), "adam_small")
    loss_total = res[0][loss_row, 0]
    small_out = [_unpack_small(r, recipe) for r in res]
    for k in _SMALL:
        out[k] = [t[k] for t in small_out]

    names = ["ffn1_norm", "ffn1_w_gu", "ffn1_w_down", "mix_norm", "w_in", "q_a_norm", "w_q_b", "kv_a_norm", "w_kv_b",
             "pool_w", "pool_scale", "fox_b_f", "w_out", "ffn2_norm", "ffn2_w_gu", "ffn2_w_down", "final_norm"]
    outs = [loss_total, dx[None]]
    for which in range(4):
        outs += [out[k][which] for k in names]
    return tuple(outs)
```

```python
import functools
import math

import numpy as np
import jax
import jax.numpy as jnp
from jax import lax
from jax.experimental import pallas as pl
from jax.experimental.pallas import tpu as pltpu

F32 = jnp.float32
BF16 = jnp.bfloat16
MESH_ID = pl.DeviceIdType.MESH

N_DEV = 8
EPS = 1e-6
DEPTH = 2

MLA_HEADS = 6
MLA_Q_RANK = 256
MLA_KV_RANK = 128
MLA_NOPE = 64
MLA_ROPE = 32
MLA_V = 64
ROPE_THETA = 10000.0
POOL_WINDOWS = (2, 4, 8, 16)
POOL_GROUP = 64
POOL_WIDTH = 256
FOX_HEADS = 6
FOX_HEAD_DIM = 64
N_IN = 1830

ADAM_LR = 0.001
ADAM_B1 = 0.9
ADAM_B2 = 0.999
ADAM_EPS = 1e-08
ADAM_WD = 0.01
ADAM_STEP = 10

LANES = 128
HEAD_BLOCK = 128
VMEM_LIMIT = 48 * 1024 * 1024
NEG = -1e30

ZA = 768
ZF = 1152
N_PAD = ZA + ZF
TAIL0 = 640
ROPE_LANE0 = 64


def _f_lane(h):
    return 8 * (h // 2) + (h % 2)


def _in_perm():
    perm = -np.ones(N_PAD, np.int32)
    perm[0:256] = np.arange(0, 256)
    perm[256:384] = np.arange(256, 384)
    perm[384:640] = np.arange(416, 672)
    for h in range(FOX_HEADS):
        perm[TAIL0 + _f_lane(h)] = 1824 + h
    perm[TAIL0 + ROPE_LANE0:TAIL0 + ROPE_LANE0 + MLA_ROPE] = np.arange(384, 416)
    perm[ZA:N_PAD] = np.arange(672, 1824)
    inv = np.zeros(N_IN, np.int32)
    for new, old in enumerate(perm):
        if old >= 0:
            inv[old] = new
    return perm, inv


_IN_PERM, _IN_INV = _in_perm()
_F_LANES = np.array([_f_lane(h) for h in range(FOX_HEADS)], np.int32)


def _dot(a, b):
    return jnp.dot(a, b, preferred_element_type=F32)


def _dot_nt(a, b):
    return lax.dot_general(a, b, (((1,), (1,)), ((), ())), preferred_element_type=F32)


def _dot_tn(a, b):
    return lax.dot_general(a, b, (((0,), (0,)), ((), ())), preferred_element_type=F32)


def _rms(x, gam):
    r = lax.rsqrt(jnp.mean(x * x, axis=-1, keepdims=True) + EPS)
    return x * r * gam


def _rms_bwd(dy, x, gam):
    r = lax.rsqrt(jnp.mean(x * x, axis=-1, keepdims=True) + EPS)
    xh = x * r
    dxh = dy * gam
    dx = r * (dxh - xh * jnp.mean(dxh * xh, axis=-1, keepdims=True))
    return dx, jnp.sum(dy * xh, axis=0, keepdims=True)


def _accum_out(ref, first, val):
    @pl.when(first)
    def _():
        ref[...] = val

    @pl.when(jnp.logical_not(first))
    def _():
        ref[...] += val


def _bs(shape, fn):
    return pl.BlockSpec(shape, fn)


def _params(dims):
    return pltpu.CompilerParams(dimension_semantics=dims, vmem_limit_bytes=VMEM_LIMIT)


def _tile(n, t):
    t = min(n, t)
    assert n % t == 0, (n, t)
    return t


HBM_SPEC = pl.BlockSpec(memory_space=pl.ANY)


def _call(body, *, name, grid, in_specs, out_specs, out_shape, scratch, dims, args, rider=None):
    n_in, n_out = len(in_specs), len(out_specs)
    if rider is None:
        outs = pl.pallas_call(body, name=name, grid=grid, in_specs=in_specs, out_specs=out_specs, out_shape=out_shape,
                              scratch_shapes=scratch, compiler_params=_params(dims))(*args)
        return list(outs), []
    k_in, k_out, k_sem = len(rider.srcs), len(rider.out_shapes), len(rider.scratch)

    def riding(*refs):
        a, b, c, d = n_in, n_in + k_in, n_in + k_in + n_out, n_in + k_in + n_out + k_out
        rest = refs[d:]
        sems = rest[len(rest) - k_sem:]
        ids = [pl.program_id(i) for i in range(len(grid))]
        first = functools.reduce(jnp.logical_and, [i == 0 for i in ids])
        last = functools.reduce(jnp.logical_and, [i == g - 1 for i, g in zip(ids, grid)])

        @pl.when(first)
        def _():
            rider.begin(refs[a:b], refs[c:d], sems)

        body(*refs[:a], *refs[b:c], *rest[:len(rest) - k_sem])

        @pl.when(last)
        def _():
            rider.end(refs[a:b], refs[c:d], sems)

    outs = pl.pallas_call(
        riding, name=name, grid=grid, in_specs=list(in_specs) + [HBM_SPEC] * k_in,
        out_specs=list(out_specs) + [HBM_SPEC] * k_out, out_shape=list(out_shape) + list(rider.out_shapes),
        scratch_shapes=list(scratch) + list(rider.scratch),
        compiler_params=_params(("arbitrary",) * len(grid)))(*args, *rider.srcs)
    return list(outs[:n_out]), list(outs[n_out:])


def _ffn_fwd(x, gam, wgu, wd, name, rider=None):
    S, D = x.shape
    _, nj, _, n = wgu.shape
    tm = _tile(S, 512)

    def body(x_ref, gam_ref, wgu_ref, wd_ref, xo_ref, h_ref, gu_ref, hs, acc):
        j = pl.program_id(1)

        @pl.when(j == 0)
        def _():
            hb = _rms(x_ref[...], gam_ref[...]).astype(BF16)
            hs[...] = hb
            h_ref[...] = hb
            acc[...] = jnp.zeros_like(acc)

        h = hs[...]
        g = _dot(h, wgu_ref[0])
        u = _dot(h, wgu_ref[1])
        gu_ref[0] = g.astype(BF16)
        gu_ref[1] = u.astype(BF16)
        a = (g * jax.nn.sigmoid(g) * u).astype(BF16)
        acc[...] += _dot(a, wd_ref[...])

        @pl.when(j == nj - 1)
        def _():
            xo_ref[...] = x_ref[...] + 0.5 * acc[...]

    return _call(
        body, name=name, grid=(S // tm, nj),
        in_specs=[_bs((tm, D), lambda i, j: (i, 0)), _bs((1, D), lambda i, j: (0, 0)),
                  _bs((2, None, D, n), lambda i, j: (0, j, 0, 0)), _bs((None, n, D), lambda i, j: (j, 0, 0))],
        out_specs=[_bs((tm, D), lambda i, j: (i, 0)), _bs((tm, D), lambda i, j: (i, 0)),
                   _bs((2, None, tm, n), lambda i, j: (0, j, i, 0))],
        out_shape=[jax.ShapeDtypeStruct((S, D), F32), jax.ShapeDtypeStruct((S, D), BF16),
                   jax.ShapeDtypeStruct((2, nj, S, n), BF16)],
        scratch=[pltpu.VMEM((tm, D), BF16), pltpu.VMEM((tm, D), F32)],
        dims=("parallel", "arbitrary"), args=(x, gam, wgu, wd), rider=rider)


def _ffn_bwd(dy, x, gam, gu, wgu, wd, name, rider=None):
    S, D = x.shape
    _, nj, _, n = wgu.shape
    tm = _tile(S, 512)

    def body(dy_ref, x_ref, gam_ref, gu_ref, wgu_ref, wd_ref, dx_ref, dgam_ref, dgu_ref, a_ref, dyh_ref, acc):
        i = pl.program_id(0)
        j = pl.program_id(1)

        @pl.when(j == 0)
        def _():
            dyh_ref[...] = (0.5 * dy_ref[...]).astype(BF16)
            acc[...] = jnp.zeros_like(acc)

        da = _dot_nt(dyh_ref[...], wd_ref[...])
        g = gu_ref[0].astype(F32)
        u = gu_ref[1].astype(F32)
        sig = jax.nn.sigmoid(g)
        sl = g * sig
        dg = (da * u * (sig * (1.0 + g * (1.0 - sig)))).astype(BF16)
        du = (da * sl).astype(BF16)
        dgu_ref[0] = dg
        dgu_ref[1] = du
        a_ref[...] = (sl * u).astype(BF16)
        acc[...] += _dot_nt(dg, wgu_ref[0]) + _dot_nt(du, wgu_ref[1])

        @pl.when(j == nj - 1)
        def _():
            dxn, dgam = _rms_bwd(acc[...], x_ref[...], gam_ref[...])
            dx_ref[...] = dy_ref[...] + dxn
            _accum_out(dgam_ref, i == 0, dgam)

    return _call(
        body, name=name, grid=(S // tm, nj),
        in_specs=[_bs((tm, D), lambda i, j: (i, 0)), _bs((tm, D), lambda i, j: (i, 0)), _bs((1, D), lambda i, j: (0, 0)),
                  _bs((2, None, tm, n), lambda i, j: (0, j, i, 0)),
                  _bs((2, None, D, n), lambda i, j: (0, j, 0, 0)), _bs((None, n, D), lambda i, j: (j, 0, 0))],
        out_specs=[_bs((tm, D), lambda i, j: (i, 0)), _bs((1, D), lambda i, j: (0, 0)),
                   _bs((2, None, tm, n), lambda i, j: (0, j, i, 0)), _bs((None, tm, n), lambda i, j: (j, i, 0)),
                   _bs((tm, D), lambda i, j: (i, 0))],
        out_shape=[jax.ShapeDtypeStruct((S, D), F32), jax.ShapeDtypeStruct((1, D), F32),
                   jax.ShapeDtypeStruct((2, nj, S, n), BF16), jax.ShapeDtypeStruct((nj, S, n), BF16),
                   jax.ShapeDtypeStruct((S, D), BF16)],
        scratch=[pltpu.VMEM((tm, D), F32)],
        dims=("arbitrary", "arbitrary"), args=(dy, x, gam, gu, wgu, wd), rider=rider)


def _mm_tn(a, b, nb, a_of, b_of, tm, tn, name, rider=None, ts=512):
    _, S, M = a.shape
    N = b.shape[2]
    tm = _tile(M, tm)
    tn = _tile(N, tn)
    ts = _tile(S, ts)
    nk = S // ts

    def body(a_ref, b_ref, o_ref, acc):
        k = pl.program_id(3)

        @pl.when(k == 0)
        def _():
            acc[...] = jnp.zeros_like(acc)

        acc[...] += _dot_tn(a_ref[...].astype(BF16), b_ref[...].astype(BF16))

        @pl.when(k == nk - 1)
        def _():
            o_ref[...] = acc[...].astype(o_ref.dtype)

    (out,), extra = _call(
        body, name=name, grid=(nb, M // tm, N // tn, nk),
        in_specs=[_bs((None, ts, tm), lambda p, i, j, k: (a_of(p), k, i)),
                  _bs((None, ts, tn), lambda p, i, j, k: (b_of(p), k, j))],
        out_specs=[_bs((None, tm, tn), lambda p, i, j, k: (p, i, j))],
        out_shape=[jax.ShapeDtypeStruct((nb, M, N), BF16)],
        scratch=[pltpu.VMEM((tm, tn), F32)],
        dims=("parallel", "parallel", "parallel", "arbitrary"), args=(a, b), rider=rider)
    return (out, extra) if rider is not None else out


def _mix_in_fwd(x, gam, w_in, name):
    S, D = x.shape
    tm = _tile(S, 512)
    nkv = (ZF - 384) // LANES

    def body(x_ref, gam_ref, w_ref, h_ref, za_ref, zf_ref, zt_ref):
        hb = _rms(x_ref[...], gam_ref[...]).astype(BF16)
        h_ref[...] = hb
        za_ref[...] = _dot(hb, w_ref[:, 0:ZA])
        zf = _dot(hb, w_ref[:, ZA:N_PAD])
        zf_ref[...] = zf.astype(BF16)
        for c in range(nkv):
            zt_ref[c * LANES:(c + 1) * LANES, :] = zf[:, 384 + c * LANES:384 + (c + 1) * LANES].T.astype(BF16)

    return pl.pallas_call(
        body, name=name, grid=(S // tm,),
        in_specs=[_bs((tm, D), lambda i: (i, 0)), _bs((1, D), lambda i: (0, 0)), _bs((D, N_PAD), lambda i: (0, 0))],
        out_specs=[_bs((tm, D), lambda i: (i, 0)), _bs((tm, ZA), lambda i: (i, 0)), _bs((tm, ZF), lambda i: (i, 0)),
                   _bs((nkv * LANES, tm), lambda i: (0, i))],
        out_shape=[jax.ShapeDtypeStruct((S, D), BF16), jax.ShapeDtypeStruct((S, ZA), F32),
                   jax.ShapeDtypeStruct((S, ZF), BF16), jax.ShapeDtypeStruct((nkv * LANES, S), BF16)],
        compiler_params=_params(("parallel",)),
    )(x, gam, w_in)


def _mix_in_bwd(dy, x, gam, dza_mla, du, dtail_f, dqf, dkf, dvf, w_in, name):
    S, D = x.shape
    tm = _tile(S, 512)

    def body(dy_ref, x_ref, gam_ref, dza_ref, du_ref, dt_ref, dq_ref, dk_ref, dv_ref, w_ref, dx_ref, dgam_ref, dz_ref):
        i = pl.program_id(0)
        dza = dza_ref[...]
        dz = jnp.concatenate([dza[:, 0:384], du_ref[...], dza[:, TAIL0:ZA] + dt_ref[...],
                              dq_ref[...], dk_ref[...], dv_ref[...]], axis=1).astype(BF16)
        dz_ref[...] = dz
        dh = _dot_nt(dz, w_ref[...])
        dxn, dgam = _rms_bwd(dh, x_ref[...], gam_ref[...])
        dx_ref[...] = dy_ref[...] + dxn
        _accum_out(dgam_ref, i == 0, dgam)

    row = lambda i: (i, 0)
    fix = lambda i: (0, 0)
    return pl.pallas_call(
        body, name=name, grid=(S // tm,),
        in_specs=[_bs((tm, D), row), _bs((tm, D), row), _bs((1, D), fix), _bs((tm, ZA), row), _bs((tm, 256), row),
                  _bs((tm, 128), row), _bs((tm, 384), row), _bs((tm, 384), row), _bs((tm, 384), row),
                  _bs((D, N_PAD), fix)],
        out_specs=[_bs((tm, D), row), _bs((1, D), fix), _bs((tm, N_PAD), row)],
        out_shape=[jax.ShapeDtypeStruct((S, D), F32), jax.ShapeDtypeStruct((1, D), F32),
                   jax.ShapeDtypeStruct((S, N_PAD), BF16)],
        compiler_params=_params(("arbitrary",)),
    )(dy, x, gam, dza_mla, du, dtail_f, dqf, dkf, dvf, w_in)


def _rope_tables(S):
    half = MLA_ROPE // 2
    inv_freq = ROPE_THETA ** (-jnp.arange(0, MLA_ROPE, 2, dtype=F32) / MLA_ROPE)
    ang = jnp.arange(S, dtype=jnp.int32).astype(F32)[:, None] * inv_freq[None, :]
    cos, sin = jnp.cos(ang), jnp.sin(ang)
    one = jnp.ones((S, ROPE_LANE0), F32)
    zero = jnp.zeros((S, ROPE_LANE0), F32)
    pad1 = jnp.ones((S, LANES - ROPE_LANE0 - MLA_ROPE), F32)
    pad0 = jnp.zeros((S, LANES - ROPE_LANE0 - MLA_ROPE), F32)
    zh = jnp.zeros((S, half), F32)
    tab_c = jnp.concatenate([one, cos, cos, pad1], axis=1)
    tab_ck = jnp.concatenate([zero, cos, cos, pad0], axis=1)
    tab_s1 = jnp.concatenate([zero, -sin, zh, pad0], axis=1)
    tab_s2 = jnp.concatenate([zero, zh, sin, pad0], axis=1)
    return tab_c, tab_ck, tab_s1, tab_s2


def _rope(x, c, s1, s2):
    return x * c + pltpu.roll(x, LANES - 16, 1) * s1 + pltpu.roll(x, 16, 1) * s2


def _rope_t(dy, c, s1, s2):
    return dy * c + pltpu.roll(dy * s1, 16, 1) + pltpu.roll(dy * s2, LANES - 16, 1)


_MLA_SCALE = 1.0 / math.sqrt(MLA_NOPE + MLA_ROPE)


def _mla_prep(za, gq, gkv, wq, wkv, tabs, name):
    S = za.shape[0]
    tm = _tile(S, 512)
    H = MLA_HEADS

    def body(zq_ref, tail_ref, gq_ref, gkv_ref, wq_ref, wkv_ref, c_ref, ck_ref, s1_ref, s2_ref,
             qf_ref, kf_ref, v_ref, kft_ref, vt_ref):
        zq = zq_ref[...]
        c, s1, s2 = c_ref[...], s1_ref[...], s2_ref[...]
        qn = _rms(zq[:, 0:256], gq_ref[...]).astype(BF16)
        q = _dot(qn, wq_ref[...])
        for h in range(H):
            blk = _rope(q[:, h * LANES:(h + 1) * LANES], c, s1, s2)
            qf_ref[:, h * LANES:(h + 1) * LANES] = (blk * _MLA_SCALE).astype(BF16)
        kvn = _rms(zq[:, 256:384], gkv_ref[...]).astype(BF16)
        kv = _dot(kvn, wkv_ref[...])
        kt = _rope(tail_ref[...], ck_ref[...], s1, s2)
        for h in range(H):
            sl = slice(h * LANES, (h + 1) * LANES)
            kblk = kv[:, sl] + kt
            kf_ref[:, sl] = kblk.astype(BF16)
            kft_ref[sl, :] = kblk.T.astype(BF16)
        v_ref[...] = kv[:, H * LANES:].astype(BF16)
        for cblk in range(H * MLA_V // LANES):
            sl = slice(cblk * LANES, (cblk + 1) * LANES)
            vt_ref[sl, :] = kv[:, H * LANES + cblk * LANES:H * LANES + (cblk + 1) * LANES].T.astype(BF16)

    row = lambda i: (i, 0)
    col = lambda i: (0, i)
    fix = lambda i: (0, 0)
    return pl.pallas_call(
        body, name=name, grid=(S // tm,),
        in_specs=[_bs((tm, 384), row), _bs((tm, 128), lambda i: (i, TAIL0 // 128)), _bs((1, 256), fix), _bs((1, 128), fix),
                  _bs((256, 768), fix), _bs((128, 1152), fix),
                  _bs((tm, 128), row), _bs((tm, 128), row), _bs((tm, 128), row), _bs((tm, 128), row)],
        out_specs=[_bs((tm, 768), row), _bs((tm, 768), row), _bs((tm, 384), row), _bs((768, tm), col), _bs((384, tm), col)],
        out_shape=[jax.ShapeDtypeStruct((S, 768), BF16), jax.ShapeDtypeStruct((S, 768), BF16),
                   jax.ShapeDtypeStruct((S, 384), BF16), jax.ShapeDtypeStruct((768, S), BF16),
                   jax.ShapeDtypeStruct((384, S), BF16)],
        compiler_params=_params(("parallel",)),
    )(za, za, gq, gkv, wq, wkv, *tabs)


def _mla_prep_bwd(za, gq, gkv, wq, wkv, tabs, dqf, dkf, dvm, name):
    S = za.shape[0]
    tm = _tile(S, 512)
    H = MLA_HEADS

    def body(zq_ref, gq_ref, gkv_ref, wq_ref, wkv_ref, c_ref, ck_ref, s1_ref, s2_ref, dqf_ref, dkf_ref, dvm_ref,
             dza_ref, dwq_ref, dwkv_ref, dgq_ref, dgkv_ref):
        i = pl.program_id(0)
        first = i == 0
        zq = zq_ref[...]
        c, s1, s2 = c_ref[...], s1_ref[...], s2_ref[...]
        lane = lax.broadcasted_iota(jnp.int32, (1, LANES), 1)
        nope = lane < MLA_NOPE
        rope = jnp.logical_and(lane >= ROPE_LANE0, lane < ROPE_LANE0 + MLA_ROPE)

        qa = zq[:, 0:256]
        qn = _rms(qa, gq_ref[...]).astype(BF16)
        dqf = dqf_ref[...]
        dq_pre = jnp.concatenate(
            [_rope_t(dqf[:, h * LANES:(h + 1) * LANES] * _MLA_SCALE, c, s1, s2) for h in range(H)], axis=1).astype(BF16)
        _accum_out(dwq_ref, first, _dot_tn(qn, dq_pre))
        dqa, dgq = _rms_bwd(_dot_nt(dq_pre, wq_ref[...]), qa, gq_ref[...])
        _accum_out(dgq_ref, first, dgq)

        kva = zq[:, 256:384]
        kvn = _rms(kva, gkv_ref[...]).astype(BF16)
        dkf = dkf_ref[...]
        parts = []
        dkt = jnp.zeros((tm, LANES), F32)
        for h in range(H):
            blk = dkf[:, h * LANES:(h + 1) * LANES]
            parts.append(jnp.where(nope, blk, 0.0))
            dkt = dkt + jnp.where(rope, blk, 0.0)
        dkv_pre = jnp.concatenate(parts + [dvm_ref[...]], axis=1).astype(BF16)
        _accum_out(dwkv_ref, first, _dot_tn(kvn, dkv_pre))
        dkva, dgkv = _rms_bwd(_dot_nt(dkv_pre, wkv_ref[...]), kva, gkv_ref[...])
        _accum_out(dgkv_ref, first, dgkv)

        dtail = _rope_t(dkt, ck_ref[...], s1, s2)
        dza_ref[...] = jnp.concatenate([dqa, dkva, jnp.zeros((tm, 256), F32), dtail], axis=1)

    row = lambda i: (i, 0)
    fix = lambda i: (0, 0)
    return pl.pallas_call(
        body, name=name, grid=(S // tm,),
        in_specs=[_bs((tm, 384), row), _bs((1, 256), fix), _bs((1, 128), fix), _bs((256, 768), fix), _bs((128, 1152), fix),
                  _bs((tm, 128), row), _bs((tm, 128), row), _bs((tm, 128), row), _bs((tm, 128), row),
                  _bs((tm, 768), row), _bs((tm, 768), row), _bs((tm, 384), row)],
        out_specs=[_bs((tm, ZA), row), _bs((256, 768), fix), _bs((128, 1152), fix), _bs((1, 256), fix), _bs((1, 128), fix)],
        out_shape=[jax.ShapeDtypeStruct((S, ZA), F32), jax.ShapeDtypeStruct((256, 768), F32),
                   jax.ShapeDtypeStruct((128, 1152), F32), jax.ShapeDtypeStruct((1, 256), F32),
                   jax.ShapeDtypeStruct((1, 128), F32)],
        compiler_params=_params(("arbitrary",)),
    )(za, gq, gkv, wq, wkv, *tabs, dqf, dkf, dvm)


def _head_views(qb, kb, r, dkb, sel):
    if dkb == LANES:
        sl = slice(r * LANES, (r + 1) * LANES)
        return qb[:, sl], kb[:, sl], kb[:, sl]
    return jnp.where(sel, qb, jnp.zeros_like(qb)), kb, jnp.where(sel, kb, jnp.zeros_like(kb))


def _attn_fwd_t(q_arr, k_arr, vt_arr, *, nhp, dkb, qoff, koff, vtoff, scale, cum, cumT, name, rider=None):
    S = q_arr.shape[0]
    T = _tile(S, 512)
    nq = S // T
    W = 2 * dkb
    bias = cum is not None

    def body(*refs):
        if bias:
            q_ref, k_ref, vt_ref, cq_ref, ck_ref, o_ref, lse_ref, m_s, l_s, acc_s = refs
        else:
            q_ref, k_ref, vt_ref, o_ref, lse_ref, m_s, l_s, acc_s = refs
        hp, qi, ki = pl.program_id(0), pl.program_id(1), pl.program_id(2)
        lo_lane = lax.broadcasted_iota(jnp.int32, (1, LANES), 1) < 64
        lo_row = lax.broadcasted_iota(jnp.int32, (LANES, 1), 0) < 64

        @pl.when(ki == 0)
        def _():
            m_s[...] = jnp.full_like(m_s, NEG)
            l_s[...] = jnp.zeros_like(l_s)
            acc_s[...] = jnp.zeros_like(acc_s)

        def step(masked):
            qb, kb, vtb = q_ref[...], k_ref[...], vt_ref[...]
            if masked:
                mask = lax.broadcasted_iota(jnp.int32, (T, T), 0) <= lax.broadcasted_iota(jnp.int32, (T, T), 1)
            if bias:
                li = lax.broadcasted_iota(jnp.int32, (T, LANES), 1)
                ckb = ck_ref[...]
            for r in range(2):
                sel = lo_lane if r == 0 else jnp.logical_not(lo_lane)
                rsel = lo_row if r == 0 else jnp.logical_not(lo_row)
                q, k, _ = _head_views(qb, kb, r, dkb, sel)
                if scale != 1.0:
                    q = q * jnp.asarray(scale, q.dtype)
                s = _dot_nt(k, q)
                if bias:
                    ck = jnp.sum(jnp.where(li == 8 * hp + r, ckb, 0.0), axis=1, keepdims=True)
                    s = s + (cq_ref[r:r + 1, :] - ck)
                if masked:
                    s = jnp.where(mask, s, NEG)
                m_prev = m_s[r:r + 1, :]
                m_new = jnp.maximum(m_prev, jnp.max(s, axis=0, keepdims=True))
                alpha = jnp.exp(m_prev - m_new)
                p = jnp.exp(s - m_new)
                l_s[r:r + 1, :] = alpha * l_s[r:r + 1, :] + jnp.sum(p, axis=0, keepdims=True)
                m_s[r:r + 1, :] = m_new
                pv = _dot(jnp.where(rsel, vtb, jnp.zeros_like(vtb)), p.astype(BF16))
                acc_s[...] = acc_s[...] * jnp.where(rsel, alpha, 1.0) + pv

        @pl.when(ki < qi)
        def _():
            step(False)

        @pl.when(ki == qi)
        def _():
            step(True)

        @pl.when(ki == nq - 1)
        def _():
            inv = jnp.where(lo_row, 1.0 / l_s[0:1, :], 1.0 / l_s[1:2, :])
            o_ref[...] = (acc_s[...] * inv).T.astype(BF16)
            used = lax.broadcasted_iota(jnp.int32, (8, T), 0) < 2
            lse_ref[...] = jnp.where(used, m_s[...] + jnp.log(jnp.where(used, l_s[...], 1.0)), 0.0)

    kmap = lambda hp, qi, ki: jnp.minimum(ki, qi)
    in_specs = [_bs((T, W), lambda hp, qi, ki: (qi, qoff + hp)),
                _bs((T, W), lambda hp, qi, ki: (kmap(hp, qi, ki), koff + hp)),
                _bs((LANES, T), lambda hp, qi, ki: (vtoff + hp, kmap(hp, qi, ki)))]
    args = [q_arr, k_arr, vt_arr]
    if bias:
        in_specs += [_bs((8, T), lambda hp, qi, ki: (hp, qi)), _bs((T, LANES), lambda hp, qi, ki: (kmap(hp, qi, ki), 0))]
        args += [cumT, cum]
    return _call(
        body, name=name, grid=(nhp, nq, nq),
        in_specs=in_specs,
        out_specs=[_bs((T, LANES), lambda hp, qi, ki: (qi, hp)), _bs((None, 8, T), lambda hp, qi, ki: (hp, 0, qi))],
        out_shape=[jax.ShapeDtypeStruct((S, nhp * LANES), BF16), jax.ShapeDtypeStruct((nhp, 8, S), F32)],
        scratch=[pltpu.VMEM((8, T), F32), pltpu.VMEM((8, T), F32), pltpu.VMEM((LANES, T), F32)],
        dims=("parallel", "parallel", "arbitrary"), args=args, rider=rider)


def _attn_dvec(o_arr, do_arr, nhp, name):
    S = o_arr.shape[0]
    T = _tile(S, 512)

    def body(o_ref, do_ref, d_ref):
        prod = do_ref[...].astype(F32) * o_ref[...].astype(F32)
        li = lax.broadcasted_iota(jnp.int32, (T, LANES), 1)
        d0 = jnp.sum(jnp.where(li < 64, prod, 0.0), axis=1, keepdims=True)
        d1 = jnp.sum(jnp.where(li >= 64, prod, 0.0), axis=1, keepdims=True)
        d_ref[...] = jnp.where(li == 0, d0, jnp.where(li == 1, d1, 0.0)).T[0:8, :]

    return pl.pallas_call(
        body, name=name, grid=(nhp, S // T),
        in_specs=[_bs((T, LANES), lambda hp, i: (i, hp)), _bs((T, LANES), lambda hp, i: (i, hp))],
        out_specs=_bs((None, 8, T), lambda hp, i: (hp, 0, i)),
        out_shape=jax.ShapeDtypeStruct((nhp, 8, S), F32),
        compiler_params=_params(("parallel", "parallel")),
    )(o_arr, do_arr)


def _attn_bwd_t(q_arr, k_arr, kt_arr, v_arr, do_arr, lse, dvec, *, nhp, dkb, qoff, koff, ktoff, voff, scale, cum, cumT,
                name, rider=None):
    S = q_arr.shape[0]
    T = _tile(S, 512)
    nq = S // T
    W = 2 * dkb
    bias = cum is not None

    def body(*refs):
        if bias:
            (q_ref, k_ref, kt_ref, v_ref, do_ref, lse_ref, dvec_ref, cq_ref, ck_ref,
             dq_ref, dk_ref, dv_ref, dcq_ref, dck_ref, dqt_s, dk_s, dv_s, dcq_s, dck_s) = refs
        else:
            (q_ref, k_ref, kt_ref, v_ref, do_ref, lse_ref, dvec_ref,
             dq_ref, dk_ref, dv_ref, dqt_s, dk_s, dv_s) = refs
        hp, ki, qi = pl.program_id(0), pl.program_id(1), pl.program_id(2)
        lo_lane = lax.broadcasted_iota(jnp.int32, (1, LANES), 1) < 64
        lo_row = lax.broadcasted_iota(jnp.int32, (LANES, 1), 0) < 64

        @pl.when(jnp.logical_and(ki == 0, qi == 0))
        def _():
            dqt_s[...] = jnp.zeros_like(dqt_s)
            if bias:
                dcq_s[...] = jnp.zeros_like(dcq_s)

        @pl.when(qi == 0)
        def _():
            dk_s[...] = jnp.zeros_like(dk_s)
            dv_s[...] = jnp.zeros_like(dv_s)
            if bias:
                dck_s[...] = jnp.zeros_like(dck_s)

        def step(masked):
            qb, kb, ktb, vb, dob = q_ref[...], k_ref[...], kt_ref[...], v_ref[...], do_ref[...]
            if masked:
                mask = lax.broadcasted_iota(jnp.int32, (T, T), 0) <= lax.broadcasted_iota(jnp.int32, (T, T), 1)
            if bias:
                li = lax.broadcasted_iota(jnp.int32, (T, LANES), 1)
                ckb = ck_ref[...]
            for r in range(2):
                sel = lo_lane if r == 0 else jnp.logical_not(lo_lane)
                rsel = lo_row if r == 0 else jnp.logical_not(lo_row)
                q, k, _ = _head_views(qb, kb, r, dkb, sel)
                if scale != 1.0:
                    q = q * jnp.asarray(scale, q.dtype)
                s = _dot_nt(k, q)
                if bias:
                    ck = jnp.sum(jnp.where(li == 8 * hp + r, ckb, 0.0), axis=1, keepdims=True)
                    s = s + (cq_ref[r:r + 1, :] - ck)
                p = jnp.exp(s - lse_ref[r:r + 1, :])
                if masked:
                    p = jnp.where(mask, p, 0.0)
                do_r = jnp.where(sel, dob, jnp.zeros_like(dob))
                dp = _dot_nt(vb, do_r)
                ds = p * (dp - dvec_ref[r:r + 1, :])
                pb = p.astype(BF16)
                dsb = ds.astype(BF16)
                dv_s[...] += _dot(pb, do_r)
                if dkb == LANES:
                    sl = slice(r * LANES, (r + 1) * LANES)
                    dk_s[:, sl] += _dot(dsb, q)
                    dqt_s[qi, sl, :] += _dot(ktb[sl, :], dsb) * scale
                else:
                    dk_s[...] += _dot(dsb, q)
                    dqt_s[qi] += _dot(jnp.where(rsel, ktb, jnp.zeros_like(ktb)), dsb) * scale
                if bias:
                    dcq_s[qi, r:r + 1, :] += jnp.sum(ds, axis=0, keepdims=True)
                    dck_s[...] -= jnp.where(li == 8 * hp + r, jnp.sum(ds, axis=1, keepdims=True), 0.0)

        @pl.when(qi > ki)
        def _():
            step(False)

        @pl.when(qi == ki)
        def _():
            step(True)

        @pl.when(qi == nq - 1)
        def _():
            dk_ref[...] = dk_s[...]
            dv_ref[...] = dv_s[...]
            if bias:
                dck_ref[...] = dck_s[...]

        @pl.when(jnp.logical_and(ki == nq - 1, qi == nq - 1))
        def _():
            for c in range(nq):
                dq_ref[c * T:(c + 1) * T, :] = dqt_s[c].T
                if bias:
                    dcq_ref[:, c * T:(c + 1) * T] = dcq_s[c]

    qmap = lambda hp, ki, qi: jnp.maximum(qi, ki)
    in_specs = [_bs((T, W), lambda hp, ki, qi: (qmap(hp, ki, qi), qoff + hp)),
                _bs((T, W), lambda hp, ki, qi: (ki, koff + hp)),
                _bs((W, T), lambda hp, ki, qi: (ktoff + hp, ki)),
                _bs((T, LANES), lambda hp, ki, qi: (ki, voff + hp)),
                _bs((T, LANES), lambda hp, ki, qi: (qmap(hp, ki, qi), hp)),
                _bs((None, 8, T), lambda hp, ki, qi: (hp, 0, qmap(hp, ki, qi))),
                _bs((None, 8, T), lambda hp, ki, qi: (hp, 0, qmap(hp, ki, qi)))]
    args = [q_arr, k_arr, kt_arr, v_arr, do_arr, lse, dvec]
    out_specs = [_bs((S, W), lambda hp, ki, qi: (0, hp)), _bs((T, W), lambda hp, ki, qi: (ki, hp)),
                 _bs((T, LANES), lambda hp, ki, qi: (ki, hp))]
    out_shape = [jax.ShapeDtypeStruct((S, nhp * W), F32), jax.ShapeDtypeStruct((S, nhp * W), F32),
                 jax.ShapeDtypeStruct((S, nhp * LANES), F32)]
    scratch = [pltpu.VMEM((nq, W, T), F32), pltpu.VMEM((T, W), F32), pltpu.VMEM((T, LANES), F32)]
    if bias:
        in_specs += [_bs((8, T), lambda hp, ki, qi: (hp, qmap(hp, ki, qi))), _bs((T, LANES), lambda hp, ki, qi: (ki, 0))]
        args += [cumT, cum]
        out_specs += [_bs((None, 8, S), lambda hp, ki, qi: (hp, 0, 0)), _bs((None, T, LANES), lambda hp, ki, qi: (hp, ki, 0))]
        out_shape += [jax.ShapeDtypeStruct((nhp, 8, S), F32), jax.ShapeDtypeStruct((nhp, S, LANES), F32)]
        scratch += [pltpu.VMEM((nq, 8, T), F32), pltpu.VMEM((T, LANES), F32)]
    return _call(body, name=name, grid=(nhp, nq, nq), in_specs=in_specs, out_specs=out_specs, out_shape=out_shape,
                 scratch=scratch, dims=("arbitrary", "arbitrary", "arbitrary"), args=args, rider=rider)


def _gate_lanes(shape):
    lane = lax.broadcasted_iota(jnp.int32, shape, 1)
    return jnp.logical_and(lane < 8 * (FOX_HEADS // 2), lane % 8 < 2)


def _fox_prep(za, b_row, name):
    S = za.shape[0]
    nrow = 8 * (FOX_HEADS // 2)

    def body(tail_ref, b_ref, cum_ref, cumt_ref):
        x = tail_ref[...] + b_ref[...]
        logf = jnp.minimum(x, 0.0) - jnp.log(1.0 + jnp.exp(-jnp.abs(x)))
        y = jnp.where(_gate_lanes((S, LANES)), logf, 0.0)
        row = lax.broadcasted_iota(jnp.int32, (S, LANES), 0)
        k = 1
        while k < S:
            y = y + jnp.where(row >= k, pltpu.roll(y, k, 0), 0.0)
            k *= 2
        cum_ref[...] = y
        cumt_ref[...] = y.T[0:nrow, :]

    return pl.pallas_call(
        body, name=name, grid=(1,),
        in_specs=[_bs((S, LANES), lambda i: (0, TAIL0 // LANES)), _bs((1, LANES), lambda i: (0, 0))],
        out_specs=[_bs((S, LANES), lambda i: (0, 0)), _bs((nrow, S), lambda i: (0, 0))],
        out_shape=[jax.ShapeDtypeStruct((S, LANES), F32), jax.ShapeDtypeStruct((nrow, S), F32)],
        compiler_params=_params(("arbitrary",)),
    )(za, b_row)


def _fox_prep_bwd(za, b_row, dcq, dck, name):
    S = za.shape[0]
    nhp = FOX_HEADS // 2
    nrow = 8 * nhp
    dcq2 = dcq.reshape(nrow, S)

    def body(tail_ref, b_ref, dcq_ref, dck_ref, dt_ref, db_ref):
        x = tail_ref[...] + b_ref[...]
        d = jnp.concatenate([dcq_ref[...], jnp.zeros((LANES - nrow, S), F32)], axis=0).T
        for hp in range(nhp):
            d = d + dck_ref[hp]
        row = lax.broadcasted_iota(jnp.int32, (S, LANES), 0)
        k = 1
        while k < S:
            d = d + jnp.where(row < S - k, pltpu.roll(d, S - k, 0), 0.0)
            k *= 2
        df = jnp.where(_gate_lanes((S, LANES)), d * jax.nn.sigmoid(-x), 0.0)
        dt_ref[...] = df
        db_ref[...] = jnp.sum(df, axis=0, keepdims=True)

    return pl.pallas_call(
        body, name=name, grid=(1,),
        in_specs=[_bs((S, LANES), lambda i: (0, TAIL0 // LANES)), _bs((1, LANES), lambda i: (0, 0)),
                  _bs((nrow, S), lambda i: (0, 0)), _bs((nhp, S, LANES), lambda i: (0, 0, 0))],
        out_specs=[_bs((S, LANES), lambda i: (0, 0)), _bs((1, LANES), lambda i: (0, 0))],
        out_shape=[jax.ShapeDtypeStruct((S, LANES), F32), jax.ShapeDtypeStruct((1, LANES), F32)],
        compiler_params=_params(("arbitrary",)),
    )(za, b_row, dcq2, dck)


def _pool_select(half, lane_lo, vals):
    return jnp.where(lane_lo, jnp.where(half == 0, vals[0], vals[2]), jnp.where(half == 0, vals[1], vals[3]))


def _pool_den(S, half, lane_lo):
    cnt = (lax.broadcasted_iota(jnp.int32, (S, LANES), 0) + 1).astype(F32)
    w = _pool_select(half, lane_lo, [float(x) for x in POOL_WINDOWS])
    return jnp.minimum(cnt, w)


def _pool_fwd(za, wbd, scale, name):
    S = za.shape[0]

    def body(u_ref, w_ref, sc_ref, y_ref, pd_ref):
        half = pl.program_id(0)
        u = u_ref[...]
        row = lax.broadcasted_iota(jnp.int32, (S, LANES), 0)
        lane_lo = lax.broadcasted_iota(jnp.int32, (S, LANES), 1) < POOL_GROUP
        sums = []
        acc = u
        k = 1
        while k < POOL_WINDOWS[-1]:
            acc = acc + jnp.where(row >= k, pltpu.roll(acc, k, 0), 0.0)
            sums.append(acc)
            k *= 2
        pooled = _pool_select(half, lane_lo, sums) / _pool_den(S, half, lane_lo)
        pd = (pooled - u).astype(BF16)
        pd_ref[...] = pd
        y_ref[...] = (_dot(pd, w_ref[...]) * sc_ref[...]).astype(BF16)

    return pl.pallas_call(
        body, name=name, grid=(2,),
        in_specs=[_bs((S, LANES), lambda i: (0, 384 // LANES + i)), _bs((None, LANES, LANES), lambda i: (i, 0, 0)),
                  _bs((1, LANES), lambda i: (0, i))],
        out_specs=[_bs((S, LANES), lambda i: (0, i)), _bs((S, LANES), lambda i: (0, i))],
        out_shape=[jax.ShapeDtypeStruct((S, POOL_WIDTH), BF16), jax.ShapeDtypeStruct((S, POOL_WIDTH), BF16)],
        compiler_params=_params(("parallel",)),
    )(za, wbd, scale)


def _pool_bwd(dyb, pd, wbd, scale, name):
    S = pd.shape[0]

    def body(dy_ref, pd_ref, w_ref, sc_ref, du_ref, dw_ref, dsc_ref):
        half = pl.program_id(0)
        dy = dy_ref[...]
        pd = pd_ref[...]
        w = w_ref[...]
        ypre = _dot(pd, w)
        dsc_ref[...] = jnp.sum(dy * ypre, axis=0, keepdims=True)
        dyp = (dy * sc_ref[...]).astype(BF16)
        dw_ref[...] = _dot_tn(pd, dyp)
        dpd = _dot_nt(dyp, w)
        row = lax.broadcasted_iota(jnp.int32, (S, LANES), 0)
        lane_lo = lax.broadcasted_iota(jnp.int32, (S, LANES), 1) < POOL_GROUP
        acc = dpd / _pool_den(S, half, lane_lo)
        sums = []
        k = 1
        while k < POOL_WINDOWS[-1]:
            acc = acc + jnp.where(row < S - k, pltpu.roll(acc, S - k, 0), 0.0)
            sums.append(acc)
            k *= 2
        du_ref[...] = _pool_select(half, lane_lo, sums) - dpd

    return pl.pallas_call(
        body, name=name, grid=(2,),
        in_specs=[_bs((S, LANES), lambda i: (0, i)), _bs((S, LANES), lambda i: (0, i)),
                  _bs((None, LANES, LANES), lambda i: (i, 0, 0)), _bs((1, LANES), lambda i: (0, i))],
        out_specs=[_bs((S, LANES), lambda i: (0, i)), _bs((None, LANES, LANES), lambda i: (i, 0, 0)),
                   _bs((1, LANES), lambda i: (0, i))],
        out_shape=[jax.ShapeDtypeStruct((S, POOL_WIDTH), F32), jax.ShapeDtypeStruct((2, LANES, LANES), F32),
                   jax.ShapeDtypeStruct((1, POOL_WIDTH), F32)],
        compiler_params=_params(("parallel",)),
    )(dyb, pd, wbd, scale)


def _mix_out_fwd(x, ya, yb, yc, w_out, name):
    S, D = x.shape
    tm = _tile(S, 512)
    K = w_out.shape[0]

    def body(x_ref, ya_ref, yb_ref, yc_ref, w_ref, xo_ref, yc_out):
        ycat = jnp.concatenate([ya_ref[...], yb_ref[...], yc_ref[...]], axis=1)
        yc_out[...] = ycat
        xo_ref[...] = x_ref[...] + _dot(ycat, w_ref[...])

    row = lambda i: (i, 0)
    return pl.pallas_call(
        body, name=name, grid=(S // tm,),
        in_specs=[_bs((tm, D), row), _bs((tm, 384), row), _bs((tm, 256), row), _bs((tm, 384), row),
                  _bs((K, D), lambda i: (0, 0))],
        out_specs=[_bs((tm, D), row), _bs((tm, K), row)],
        out_shape=[jax.ShapeDtypeStruct((S, D), F32), jax.ShapeDtypeStruct((S, K), BF16)],
        compiler_params=_params(("parallel",)),
    )(x, ya, yb, yc, w_out)


def _mix_out_bwd(dy, w_out, name):
    S, D = dy.shape
    tm = _tile(S, 512)
    K = w_out.shape[0]

    def body(dy_ref, w_ref, da_ref, db_ref, dc_ref):
        d = _dot_nt(dy_ref[...].astype(BF16), w_ref[...])
        da_ref[...] = d[:, 0:384].astype(BF16)
        db_ref[...] = d[:, 384:640]
        dc_ref[...] = d[:, 640:1024].astype(BF16)

    row = lambda i: (i, 0)
    return pl.pallas_call(
        body, name=name, grid=(S // tm,),
        in_specs=[_bs((tm, D), row), _bs((K, D), lambda i: (0, 0))],
        out_specs=[_bs((tm, 384), row), _bs((tm, 256), row), _bs((tm, 384), row)],
        out_shape=[jax.ShapeDtypeStruct((S, 384), BF16), jax.ShapeDtypeStruct((S, 256), F32),
                   jax.ShapeDtypeStruct((S, 384), BF16)],
        compiler_params=_params(("parallel",)),
    )(dy, w_out)


def _loss_head(x, gam, target, name):
    S, D = x.shape
    tm = _tile(S, 512)

    def body(x_ref, gam_ref, t_ref, dx_ref, dgam_ref, loss_ref):
        i = pl.program_id(0)
        xv = x_ref[...]
        err = _rms(xv, gam_ref[...]) - t_ref[...]
        part = 0.5 * jnp.sum(jnp.mean(err * err, axis=-1, keepdims=True), axis=0, keepdims=True)
        dxn, dgam = _rms_bwd(err * (1.0 / D), xv, gam_ref[...])
        dx_ref[...] = dxn
        _accum_out(dgam_ref, i == 0, dgam)
        _accum_out(loss_ref, i == 0, jnp.broadcast_to(part, (1, LANES)))

    row = lambda i: (i, 0)
    fix = lambda i: (0, 0)
    return pl.pallas_call(
        body, name=name, grid=(S // tm,),
        in_specs=[_bs((tm, D), row), _bs((1, D), fix), _bs((tm, D), row)],
        out_specs=[_bs((tm, D), row), _bs((1, D), fix), _bs((1, LANES), fix)],
        out_shape=[jax.ShapeDtypeStruct((S, D), F32), jax.ShapeDtypeStruct((1, D), F32),
                   jax.ShapeDtypeStruct((1, LANES), F32)],
        compiler_params=_params(("arbitrary",)),
    )(x, gam, target)


def _adam_math(g, w, m, v):
    m = ADAM_B1 * m + (1.0 - ADAM_B1) * g
    v = ADAM_B2 * v + (1.0 - ADAM_B2) * (g * g)
    m_hat = m / (1.0 - ADAM_B1 ** ADAM_STEP)
    v_hat = v / (1.0 - ADAM_B2 ** ADAM_STEP)
    delta = -ADAM_LR * (m_hat / (jnp.sqrt(v_hat) + ADAM_EPS) + ADAM_WD * w)
    return delta, m, v


def _adam_sum(recv, w, m, v, layer, prev, tr, name):
    L, R, C = w.shape
    Cp = recv.shape[2]
    tr = _tile(R, tr)

    def body(r_ref, w_ref, m_ref, v_ref, *rest):
        g_out, d_out, m_out, v_out = rest[len(rest) - 4:]
        g = r_ref[0, :, 0:C].astype(F32)
        for p in range(1, N_DEV):
            g = g + r_ref[p, :, 0:C].astype(F32)
        delta, mn, vn = _adam_math(g, w_ref[...], m_ref[...], v_ref[...])
        g_out[...] = g
        d_out[...] = delta
        m_out[...] = mn
        v_out[...] = vn

    blk = _bs((None, tr, C), lambda i: (layer, i, 0))
    shp = jax.ShapeDtypeStruct((L, R, C), F32)
    in_specs = [_bs((N_DEV, tr, Cp), lambda i: (0, i, 0)), blk, blk, blk]
    args = [recv, w, m, v]
    aliases = {}
    if prev is not None:
        in_specs += [HBM_SPEC] * 4
        args += list(prev)
        aliases = {4 + k: k for k in range(4)}
    return pl.pallas_call(
        body, name=name, grid=(R // tr,),
        in_specs=in_specs, out_specs=[blk, blk, blk, blk], out_shape=[shp, shp, shp, shp],
        input_output_aliases=aliases, compiler_params=_params(("parallel",)),
    )(*args)


def _dev_index(px, py, pc):
    return 4 * px + 2 * py + pc


class _GatherRider:
    def __init__(self, shards, out_shapes, views, zero_src=None, zero_views=()):
        self.n = len(shards)
        self.views = views
        self.zero_views = list(zero_views) if zero_src is not None else []
        self.srcs = list(shards) + ([zero_src] if self.zero_views else [])
        self.out_shapes = list(out_shapes)
        n, nz = self.n, len(self.zero_views)
        self.scratch = [pltpu.SemaphoreType.DMA((n, 7)), pltpu.SemaphoreType.DMA((n, 7)),
                        pltpu.SemaphoreType.DMA((n,)), pltpu.SemaphoreType.DMA((max(nz, 1),))]

    def _copies(self, ins, outs, sems):
        n = self.n
        send_sems, recv_sems, local_sems, zero_sems = sems
        x, y, c = lax.axis_index("x"), lax.axis_index("y"), lax.axis_index("c")
        me, sibling = (x, y, c), (x, y, 1 - c)
        chips = [(1 - x, y), (x, 1 - y), (1 - x, 1 - y)]

        def rows(a, blk):
            return self.views[a](outs[a], _dev_index(*blk))

        def copy(a, k, blk, to, src=None):
            return pltpu.make_async_remote_copy(
                src_ref=rows(a, blk) if src is None else src, dst_ref=rows(a, blk),
                send_sem=send_sems.at[a, k], recv_sem=recv_sems.at[a, k], device_id=to, device_id_type=MESH_ID)

        local = [pltpu.make_async_copy(ins[a], rows(a, me), local_sems.at[a]) for a in range(n)]
        local += [pltpu.make_async_copy(ins[n], view(outs[a]), zero_sems.at[i])
                  for i, (a, view) in enumerate(self.zero_views)]
        first = []
        for a in range(n):
            first.append(copy(a, 0, me, sibling, src=ins[a]))
            first += [copy(a, 1 + j, me, (*chip, c), src=ins[a]) for j, chip in enumerate(chips)]
        over_ici = [[copy(a, 1 + j, (*chip, c), me) for a in range(n)] for j, chip in enumerate(chips)]
        passed = [[copy(a, 4 + j, (*chip, c), sibling) for a in range(n)] for j, chip in enumerate(chips)]
        from_sibling = [copy(a, 0, sibling, me) for a in range(n)]
        from_sibling += [copy(a, 4 + j, (*chip, 1 - c), me) for a in range(n) for j, chip in enumerate(chips)]
        return local, first, over_ici, passed, from_sibling

    def begin(self, ins, outs, sems):
        local, first, _, _, _ = self._copies(ins, outs, sems)
        for cp in local + first:
            cp.start()

    def end(self, ins, outs, sems):
        local, first, over_ici, passed, from_sibling = self._copies(ins, outs, sems)
        for arrived, onward in zip(over_ici, passed):
            for cp, fwd in zip(arrived, onward):
                cp.wait_recv()
                fwd.start()
        for cp in from_sibling:
            cp.wait_recv()
        for cp in first + [fwd for onward in passed for fwd in onward]:
            cp.wait_send()
        for cp in local:
            cp.wait()


class _ScatterRider:
    _MASKS = [(kx, ky, kc) for kx in (0, 1) for ky in (0, 1) for kc in (0, 1)][1:]

    def __init__(self, srcs, out_shapes, src_of, dst_at):
        self.n = len(srcs)
        self.srcs = list(srcs)
        self.out_shapes = list(out_shapes)
        self.src_of = src_of
        self.dst_at = dst_at
        n = self.n
        self.scratch = [pltpu.SemaphoreType.DMA((n, 7)), pltpu.SemaphoreType.DMA((n, 7)), pltpu.SemaphoreType.DMA((n,))]

    def _copies(self, ins, outs, sems):
        send_sems, recv_sems, local_sems = sems
        x, y, c = lax.axis_index("x"), lax.axis_index("y"), lax.axis_index("c")
        my = _dev_index(x, y, c)
        peers = [(1 - x if kx else x, 1 - y if ky else y, 1 - c if kc else c) for kx, ky, kc in self._MASKS]

        def send(i, k, to):
            return pltpu.make_async_remote_copy(
                src_ref=self.src_of[i](ins[i], _dev_index(*to)), dst_ref=self.dst_at[i](outs[i], my),
                send_sem=send_sems.at[i, k], recv_sem=recv_sems.at[i, k], device_id=to, device_id_type=MESH_ID)

        def arrival(i, k, frm):
            slot = self.dst_at[i](outs[i], _dev_index(*frm))
            return pltpu.make_async_remote_copy(
                src_ref=slot, dst_ref=slot, send_sem=send_sems.at[i, k], recv_sem=recv_sems.at[i, k],
                device_id=frm, device_id_type=MESH_ID)

        local = [pltpu.make_async_copy(self.src_of[i](ins[i], my), self.dst_at[i](outs[i], my), local_sems.at[i])
                 for i in range(self.n)]
        sends = [send(i, k, to) for k, to in enumerate(peers) for i in range(self.n)]
        arrivals = [arrival(i, k, frm) for k, frm in enumerate(peers) for i in range(self.n)]
        return local, sends, arrivals

    def begin(self, ins, outs, sems):
        local, sends, _ = self._copies(ins, outs, sems)
        for cp in local + sends:
            cp.start()

    def end(self, ins, outs, sems):
        local, sends, arrivals = self._copies(ins, outs, sems)
        for cp in arrivals:
            cp.wait_recv()
        for cp in sends:
            cp.wait_send()
        for cp in local:
            cp.wait()


def _comm_call(rider, name):
    k_in = len(rider.srcs)
    k_out = len(rider.out_shapes)

    def body(*refs):
        ins, outs, sems = refs[:k_in], refs[k_in:k_in + k_out], refs[k_in + k_out:]
        rider.begin(ins, outs, sems)
        rider.end(ins, outs, sems)

    return pl.pallas_call(
        body, name=name, in_specs=[HBM_SPEC] * k_in, out_specs=[HBM_SPEC] * k_out, out_shape=rider.out_shapes,
        scratch_shapes=rider.scratch, compiler_params=pltpu.CompilerParams(has_side_effects=True),
    )(*rider.srcs)


def _pad_w_in(w):
    take = jnp.take(w, np.maximum(_IN_PERM, 0), axis=-1)
    return jnp.where(_IN_PERM >= 0, take, jnp.zeros_like(take))


def _unpad_w_in(g):
    return jnp.take(g, _IN_INV, axis=-1)


def _small_pack(w_q_b, w_kv_b):
    a = jnp.pad(w_q_b, ((0, 0), (0, 0), (0, LANES - w_q_b.shape[2])))
    b = jnp.pad(w_kv_b, ((0, 0), (0, 0), (0, LANES - w_kv_b.shape[2])))
    return jnp.concatenate([a, b], axis=1)


def _small_unpack(p, cq, ckv):
    return p[:, 0:MLA_Q_RANK, 0:cq], p[:, MLA_Q_RANK:, 0:ckv]


def _mla_weights(wsm):
    H = MLA_HEADS
    cq = H * (MLA_NOPE + MLA_ROPE) // N_DEV
    ckv = H * (MLA_NOPE + MLA_V) // N_DEV
    wq = wsm[:, 0:MLA_Q_RANK, 0:cq].transpose(1, 0, 2).reshape(MLA_Q_RANK, H, MLA_NOPE + MLA_ROPE)
    wq = jnp.pad(wq, ((0, 0), (0, 0), (0, HEAD_BLOCK - MLA_NOPE - MLA_ROPE))).reshape(MLA_Q_RANK, H * HEAD_BLOCK)
    wkv = wsm[:, MLA_Q_RANK:, 0:ckv].transpose(1, 0, 2).reshape(MLA_KV_RANK, H, MLA_NOPE + MLA_V)
    wk = jnp.pad(wkv[:, :, 0:MLA_NOPE], ((0, 0), (0, 0), (0, HEAD_BLOCK - MLA_NOPE))).reshape(MLA_KV_RANK, H * HEAD_BLOCK)
    wv = wkv[:, :, MLA_NOPE:].reshape(MLA_KV_RANK, H * MLA_V)
    return wq, jnp.concatenate([wk, wv], axis=1)


def _mla_grads_to_blocks(dwq, dwkv):
    H = MLA_HEADS
    gq = dwq.reshape(MLA_Q_RANK, H, HEAD_BLOCK)[:, :, 0:MLA_NOPE + MLA_ROPE].reshape(MLA_Q_RANK, N_DEV, -1)
    gk = dwkv[:, 0:H * HEAD_BLOCK].reshape(MLA_KV_RANK, H, HEAD_BLOCK)[:, :, 0:MLA_NOPE]
    gv = dwkv[:, H * HEAD_BLOCK:].reshape(MLA_KV_RANK, H, MLA_V)
    gkv = jnp.concatenate([gk, gv], axis=2).reshape(MLA_KV_RANK, N_DEV, -1)
    return _small_pack(gq.transpose(1, 0, 2), gkv.transpose(1, 0, 2)).astype(BF16)


def _pool_blockdiag(pool_w):
    z = jnp.zeros((POOL_GROUP, POOL_GROUP), pool_w.dtype)
    halves = [jnp.concatenate([jnp.concatenate([pool_w[2 * i], z], axis=1),
                               jnp.concatenate([z, pool_w[2 * i + 1]], axis=1)], axis=0) for i in range(2)]
    return jnp.stack(halves)


def _pool_blockdiag_t(dw):
    g = POOL_GROUP
    return jnp.stack([dw[0, 0:g, 0:g], dw[0, g:, g:], dw[1, 0:g, 0:g], dw[1, g:, g:]])


def _gate_row(b):
    return jnp.zeros((LANES,), b.dtype).at[_F_LANES].set(b).reshape(1, LANES)


_SMALL = ("ffn1_norm", "mix_norm", "q_a_norm", "kv_a_norm", "pool_w", "pool_scale", "fox_b_f", "ffn2_norm", "final_norm")


def _pack_small(tree):
    rows, recipe = [], []
    for name in _SMALL:
        a = tree[name]
        flat = a.reshape(-1)
        n = flat.shape[0]
        nrow = -(-n // (8 * LANES)) * 8
        flat = jnp.pad(flat, (0, nrow * LANES - n))
        rows.append(flat.reshape(nrow, LANES))
        recipe.append((name, a.shape, n, nrow))
    return jnp.concatenate(rows, axis=0), recipe


def _unpack_small(packed, recipe):
    out, r0 = {}, 0
    for name, shape, n, nrow in recipe:
        out[name] = packed[r0:r0 + nrow].reshape(-1)[0:n].reshape(shape)
        r0 += nrow
    return out


def _adam_small(packs, w, m, v, name):
    R = w.shape[0]

    def body(p_ref, w_ref, m_ref, v_ref, g_out, d_out, m_out, v_out):
        g = p_ref[0]
        for p in range(1, N_DEV):
            g = g + p_ref[p]
        delta, mn, vn = _adam_math(g, w_ref[...], m_ref[...], v_ref[...])
        g_out[...] = g
        d_out[...] = delta
        m_out[...] = mn
        v_out[...] = vn

    blk = _bs((R, LANES), lambda i: (0, 0))
    shp = jax.ShapeDtypeStruct((R, LANES), F32)
    return pl.pallas_call(
        body, name=name, grid=(1,),
        in_specs=[_bs((N_DEV, R, LANES), lambda i: (0, 0, 0)), blk, blk, blk],
        out_specs=[blk, blk, blk, blk], out_shape=[shp, shp, shp, shp],
        compiler_params=_params(("arbitrary",)),
    )(packs, w, m, v)


_GATHER_PLAN = {
    ("first", 0): (("ffn1_w_gu", 0), ("ffn1_w_down", 0)),
    ("ffn1_fwd", 0): (("w_in", 0), ("w_small", 0), ("w_out", 0), ("ffn2_w_down", 0)),
    ("mla_attn_fwd", 0): (("ffn2_w_gu", 0), ("ffn1_w_gu", 1), ("ffn1_w_down", 1)),
    ("fox_attn_fwd", 0): (("w_in", 1), ("w_small", 1), ("w_out", 1), ("ffn2_w_gu", 1), ("ffn2_w_down", 1)),
}
_SCATTER_PLAN = {
    ("mla_attn_bwd", 1): (("ffn2_w_gu", 1), ("ffn2_w_down", 1)),
    ("fox_attn_bwd", 1): (("w_out", 1),),
    ("ffn1_bwd", 1): (("w_in", 1), ("w_small", 1)),
    ("ffn2_bwd", 0): (("ffn1_w_gu", 1),),
    ("mla_attn_bwd", 0): (("ffn1_w_down", 1), ("ffn2_w_gu", 0)),
    ("fox_attn_bwd", 0): (("ffn2_w_down", 0), ("w_out", 0)),
    ("ffn1_bwd", 0): (("w_in", 0), ("w_small", 0)),
    ("ffn1_dwgu", 0): (("ffn1_w_down", 0),),
    ("last", 0): (("ffn1_w_gu", 0),),
}


class _Exchange:
    def __init__(self, shards, D, n_pad, f_sh, r_in, r_out):
        self.shards = shards
        self.D, self.n_pad, self.f_sh, self.r_in, self.r_out = D, n_pad, f_sh, r_in, r_out
        self.nj = N_DEV // 2
        self.n_zero = n_pad - 2 * f_sh
        self.zero_src = jnp.zeros((self.n_zero, D), BF16) if self.n_zero else None
        self.weights, self.grads, self.recv = {}, {}, {}

    def _half(self, ref, p):
        f_sh = self.f_sh
        return ref.at[p // 2, pl.ds(pl.multiple_of((p % 2) * f_sh, 16), f_sh)]

    def _gathered_shape(self, kind):
        D, n_pad = self.D, self.n_pad
        return {"w_gu": (N_DEV, D, n_pad), "w_down": (self.nj, n_pad, D), "w_in": (N_DEV, self.r_in, N_PAD),
                "w_small": (N_DEV, MLA_Q_RANK + MLA_KV_RANK, LANES), "w_out": (N_DEV, self.r_out, D)}[kind]

    def _recv_shape(self, kind):
        D, n_pad = self.D, self.n_pad
        return {"w_gu": (N_DEV, D, n_pad), "w_down": (N_DEV, self.f_sh, D), "w_in": (N_DEV, self.r_in, N_IN),
                "w_small": (N_DEV, MLA_Q_RANK + MLA_KV_RANK, LANES), "w_out": (N_DEV, self.r_out, D)}[kind]

    @staticmethod
    def _kind(name):
        return name[5:] if name.startswith("ffn") else name

    def gather_rider(self, call, l):
        keys = _GATHER_PLAN.get((call, l))
        if not keys:
            return None
        by_dev = lambda ref, p: ref.at[p]
        shards, shapes, views, zero_views = [], [], [], []
        for a, key in enumerate(keys):
            kind = self._kind(key[0])
            shards.append(self.shards[key])
            shapes.append(jax.ShapeDtypeStruct(self._gathered_shape(kind), BF16))
            views.append(self._half if kind == "w_down" else by_dev)
            if kind == "w_down" and self.n_zero:
                zero_views += [(a, (lambda ref, j=j: ref.at[j, pl.ds(2 * self.f_sh, self.n_zero)])) for j in range(self.nj)]
        return _GatherRider(shards, shapes, views, self.zero_src, zero_views)

    def gathered(self, call, l, outs):
        for key, w in zip(_GATHER_PLAN.get((call, l), ()), outs):
            kind = self._kind(key[0])
            if kind == "w_gu":
                w = w.reshape(2, self.nj, self.D, self.n_pad)
            elif kind in ("w_in", "w_out"):
                w = w.reshape((N_DEV * w.shape[1],) + w.shape[2:])
            self.weights[key] = w

    def scatter_rider(self, call, l, pack=None):
        keys = _SCATTER_PLAN.get((call, l))
        if not keys:
            return None
        by_dev = lambda ref, p: ref.at[p]
        srcs, shapes, src_of = [], [], []
        for key in keys:
            kind = self._kind(key[0])
            srcs.append(self.grads[key])
            shapes.append(jax.ShapeDtypeStruct(self._recv_shape(kind), BF16))
            src_of.append(self._half if kind == "w_down" else by_dev)
        if pack is not None:
            srcs.append(pack)
            shapes.append(jax.ShapeDtypeStruct((N_DEV,) + pack.shape, pack.dtype))
            src_of.append(lambda ref, p: ref)
        return _ScatterRider(srcs, shapes, src_of, [by_dev] * len(srcs))

    def scattered(self, call, l, outs):
        for key, r in zip(_SCATTER_PLAN.get((call, l), ()), outs):
            self.recv[key] = r


def _local_step(x, target, ex, small):
    S, D = x.shape
    tabs = _rope_tables(S)
    nhp_a, nhp_c = MLA_HEADS // 2, FOX_HEADS // 2
    fox_scale = 1.0 / math.sqrt(FOX_HEAD_DIM)
    nj = ex.nj
    ex.gathered("first", 0, _comm_call(ex.gather_rider("first", 0), "gather_first"))
    saved = []
    for l in range(DEPTH):
        s = {}
        s["x0"] = x
        wgu1, wd1 = ex.weights[("ffn1_w_gu", l)], ex.weights[("ffn1_w_down", l)]
        (x1, s["h1"], s["gu1"]), got = _ffn_fwd(x, small["ffn1_norm"][l][None], wgu1, wd1, f"ffn1_fwd_l{l}",
                                               rider=ex.gather_rider("ffn1_fwd", l))
        ex.gathered("ffn1_fwd", l, got)
        s["x1"] = x1
        w_in = ex.weights[("w_in", l)]
        s["h2"], za, zf, zkvt = _mix_in_fwd(x1, small["mix_norm"][l][None], w_in, f"mix_in_fwd_l{l}")
        s["za"], s["zf"], s["zkvt"] = za, zf, zkvt
        wq, wkv = _mla_weights(ex.weights[("w_small", l)])
        s["wq"], s["wkv"] = wq, wkv
        gq, gkv = small["q_a_norm"][l][None], small["kv_a_norm"][l][None]
        qf, kf, vm, kft, vmt = _mla_prep(za, gq, gkv, wq, wkv, tabs, f"mla_prep_l{l}")
        s["qf"], s["kf"], s["vm"], s["kft"] = qf, kf, vm, kft
        (ya, lse_a), got = _attn_fwd_t(qf, kf, vmt, nhp=nhp_a, dkb=LANES, qoff=0, koff=0, vtoff=0, scale=1.0,
                                       cum=None, cumT=None, name=f"mla_attn_fwd_l{l}",
                                       rider=ex.gather_rider("mla_attn_fwd", l))
        ex.gathered("mla_attn_fwd", l, got)
        s["ya"], s["lse_a"] = ya, lse_a
        b_row = _gate_row(small["fox_b_f"][l])
        s["b_row"] = b_row
        cum, cumT = _fox_prep(za, b_row, f"fox_prep_l{l}")
        s["cum"], s["cumT"] = cum, cumT
        (yc, lse_c), got = _attn_fwd_t(zf, zf, zkvt, nhp=nhp_c, dkb=64, qoff=0, koff=nhp_c, vtoff=nhp_c, scale=fox_scale,
                                       cum=cum, cumT=cumT, name=f"fox_attn_fwd_l{l}",
                                       rider=ex.gather_rider("fox_attn_fwd", l))
        ex.gathered("fox_attn_fwd", l, got)
        s["yc"], s["lse_c"] = yc, lse_c
        wbd = _pool_blockdiag(small["pool_w"][l]).astype(BF16)
        s["wbd"] = wbd
        psc = small["pool_scale"][l][None]
        yb, s["pd"] = _pool_fwd(za, wbd, psc, f"pool_fwd_l{l}")
        w_out = ex.weights[("w_out", l)]
        x2, s["ycat"] = _mix_out_fwd(x1, ya, yb, yc, w_out, f"mix_out_fwd_l{l}")
        s["x2"] = x2
        wgu2, wd2 = ex.weights[("ffn2_w_gu", l)], ex.weights[("ffn2_w_down", l)]
        (x, s["h3"], s["gu2"]), _ = _ffn_fwd(x2, small["ffn2_norm"][l][None], wgu2, wd2, f"ffn2_fwd_l{l}")
        saved.append(s)

    dx, d_final, loss = _loss_head(x, small["final_norm"][None], target, "loss_head")

    small_grads = [None] * DEPTH
    for l in reversed(range(DEPTH)):
        s = saved[l]
        g = {}
        wgu2, wd2 = ex.weights[("ffn2_w_gu", l)], ex.weights[("ffn2_w_down", l)]
        dy3 = dx
        (dx, g["ffn2_norm"], dgu, act, dyh), got = _ffn_bwd(dy3, s["x2"], small["ffn2_norm"][l][None], s["gu2"], wgu2, wd2,
                                                            f"ffn2_bwd_l{l}", rider=ex.scatter_rider("ffn2_bwd", l))
        ex.scattered("ffn2_bwd", l, got)
        ex.grads[("ffn2_w_gu", l)] = _mm_tn(s["h3"][None], dgu.reshape((2 * nj,) + dgu.shape[2:]), 2 * nj, lambda p: 0,
                                            lambda p: p, 1024, 768, f"ffn2_dwgu_l{l}", ts=4096)
        ex.grads[("ffn2_w_down", l)] = _mm_tn(act, dyh[None], nj, lambda p: p, lambda p: 0, 768, 1024, f"ffn2_dwd_l{l}",
                                              ts=4096)

        w_out = ex.weights[("w_out", l)]
        dya, dyb, dyc = _mix_out_bwd(dx, w_out, f"mix_out_bwd_l{l}")
        dw_out = _mm_tn(s["ycat"][None], dx[None], 1, lambda p: 0, lambda p: 0, 1024, 1024, f"dwout_l{l}", ts=1024)[0]
        ex.grads[("w_out", l)] = dw_out.reshape(N_DEV, ex.r_out, D)

        dvec_a = _attn_dvec(s["ya"], dya, nhp_a, f"mla_dvec_l{l}")
        (dqf, dkf, dvm), got = _attn_bwd_t(s["qf"], s["kf"], s["kft"], s["vm"], dya, s["lse_a"], dvec_a, nhp=nhp_a,
                                           dkb=LANES, qoff=0, koff=0, ktoff=0, voff=0, scale=1.0, cum=None, cumT=None,
                                           name=f"mla_attn_bwd_l{l}", rider=ex.scatter_rider("mla_attn_bwd", l))
        ex.scattered("mla_attn_bwd", l, got)
        zf = s["zf"]
        dvec_c = _attn_dvec(s["yc"], dyc, nhp_c, f"fox_dvec_l{l}")
        (dqc, dkc, dvc, dcq, dck), got = _attn_bwd_t(zf, zf, s["zkvt"], zf, dyc, s["lse_c"], dvec_c, nhp=nhp_c, dkb=64,
                                                     qoff=0, koff=nhp_c, ktoff=0, voff=2 * nhp_c, scale=fox_scale,
                                                     cum=s["cum"], cumT=s["cumT"], name=f"fox_attn_bwd_l{l}",
                                                     rider=ex.scatter_rider("fox_attn_bwd", l))
        ex.scattered("fox_attn_bwd", l, got)
        dtail_f, db = _fox_prep_bwd(s["za"], s["b_row"], dcq, dck, f"fox_prep_bwd_l{l}")
        g["fox_b_f"] = db[0, _F_LANES]
        psc = small["pool_scale"][l][None]
        du, dwbd, dpsc = _pool_bwd(dyb, s["pd"], s["wbd"], psc, f"pool_bwd_l{l}")
        g["pool_w"] = _pool_blockdiag_t(dwbd)
        g["pool_scale"] = dpsc[0]
        gq, gkv = small["q_a_norm"][l][None], small["kv_a_norm"][l][None]
        dza, dwq, dwkv, dgq, dgkv = _mla_prep_bwd(s["za"], gq, gkv, s["wq"], s["wkv"], tabs, dqf, dkf, dvm,
                                                   f"mla_prep_bwd_l{l}")
        g["q_a_norm"], g["kv_a_norm"] = dgq[0], dgkv[0]
        ex.grads[("w_small", l)] = _mla_grads_to_blocks(dwq, dwkv)
        w_in = ex.weights[("w_in", l)]
        dx, g["mix_norm"], dz = _mix_in_bwd(dx, s["x1"], small["mix_norm"][l][None], dza, du, dtail_f, dqc, dkc, dvc,
                                            w_in, f"mix_in_bwd_l{l}")
        dw_in = _unpad_w_in(_mm_tn(s["h2"][None], dz[None], 1, lambda p: 0, lambda p: 0, 1024, 640, f"dwin_l{l}",
                                   ts=4096)[0])
        ex.grads[("w_in", l)] = dw_in.reshape(N_DEV, ex.r_in, N_IN)

        wgu1, wd1 = ex.weights[("ffn1_w_gu", l)], ex.weights[("ffn1_w_down", l)]
        dy1 = dx
        (dx, g["ffn1_norm"], dgu, act, dyh), got = _ffn_bwd(dy1, s["x0"], small["ffn1_norm"][l][None], s["gu1"], wgu1, wd1,
                                                            f"ffn1_bwd_l{l}", rider=ex.scatter_rider("ffn1_bwd", l))
        ex.scattered("ffn1_bwd", l, got)
        ex.grads[("ffn1_w_down", l)] = _mm_tn(act, dyh[None], nj, lambda p: p, lambda p: 0, 768, 1024, f"ffn1_dwd_l{l}",
                                              ts=4096)
        rider = ex.scatter_rider("ffn1_dwgu", l)
        dwgu = _mm_tn(s["h1"][None], dgu.reshape((2 * nj,) + dgu.shape[2:]), 2 * nj, lambda p: 0, lambda p: p,
                      1024, 768, f"ffn1_dwgu_l{l}", rider=rider, ts=4096)
        if rider is not None:
            dwgu, got = dwgu
            ex.scattered("ffn1_dwgu", l, got)
        ex.grads[("ffn1_w_gu", l)] = dwgu
        for k in ("ffn1_norm", "ffn2_norm", "mix_norm"):
            g[k] = g[k][0]
        small_grads[l] = g
    return loss, dx, small_grads, d_final[0]


_BIG = ("ffn1_w_gu", "ffn1_w_down", "w_in", "w_small", "w_out", "ffn2_w_gu", "ffn2_w_down")


def _pad_cols(w, n):
    return jnp.pad(w, ((0, 0),) * (w.ndim - 1) + ((0, n - w.shape[-1]),))


def kernel(x, ffn1_norm, ffn1_w_gu, ffn1_w_down, mix_norm, w_in, q_a_norm, w_q_b, kv_a_norm, w_kv_b, pool_w, pool_scale, fox_b_f, w_out, ffn2_norm, ffn2_w_gu, ffn2_w_down, final_norm, loss_target, m_ffn1_norm, m_ffn1_w_gu, m_ffn1_w_down, m_mix_norm, m_w_in, m_q_a_norm, m_w_q_b, m_kv_a_norm, m_w_kv_b, m_pool_w, m_pool_scale, m_fox_b_f, m_w_out, m_ffn2_norm, m_ffn2_w_gu, m_ffn2_w_down, m_final_norm, v_ffn1_norm, v_ffn1_w_gu, v_ffn1_w_down, v_mix_norm, v_w_in, v_q_a_norm, v_w_q_b, v_kv_a_norm, v_w_kv_b, v_pool_w, v_pool_scale, v_fox_b_f, v_w_out, v_ffn2_norm, v_ffn2_w_gu, v_ffn2_w_down, v_final_norm):
    W = dict(ffn1_norm=ffn1_norm, ffn1_w_gu=ffn1_w_gu, ffn1_w_down=ffn1_w_down, mix_norm=mix_norm, w_in=w_in,
             q_a_norm=q_a_norm, w_q_b=w_q_b, kv_a_norm=kv_a_norm, w_kv_b=w_kv_b, pool_w=pool_w, pool_scale=pool_scale,
             fox_b_f=fox_b_f, w_out=w_out, ffn2_norm=ffn2_norm, ffn2_w_gu=ffn2_w_gu, ffn2_w_down=ffn2_w_down,
             final_norm=final_norm)
    M = dict(ffn1_norm=m_ffn1_norm, ffn1_w_gu=m_ffn1_w_gu, ffn1_w_down=m_ffn1_w_down, mix_norm=m_mix_norm, w_in=m_w_in,
             q_a_norm=m_q_a_norm, w_q_b=m_w_q_b, kv_a_norm=m_kv_a_norm, w_kv_b=m_w_kv_b, pool_w=m_pool_w,
             pool_scale=m_pool_scale, fox_b_f=m_fox_b_f, w_out=m_w_out, ffn2_norm=m_ffn2_norm, ffn2_w_gu=m_ffn2_w_gu,
             ffn2_w_down=m_ffn2_w_down, final_norm=m_final_norm)
    V = dict(ffn1_norm=v_ffn1_norm, ffn1_w_gu=v_ffn1_w_gu, ffn1_w_down=v_ffn1_w_down, mix_norm=v_mix_norm, w_in=v_w_in,
             q_a_norm=v_q_a_norm, w_q_b=v_w_q_b, kv_a_norm=v_kv_a_norm, w_kv_b=v_w_kv_b, pool_w=v_pool_w,
             pool_scale=v_pool_scale, fox_b_f=v_fox_b_f, w_out=v_w_out, ffn2_norm=v_ffn2_norm, ffn2_w_gu=v_ffn2_w_gu,
             ffn2_w_down=v_ffn2_w_down, final_norm=v_final_norm)
    L, D, n_sh = ffn1_w_gu.shape
    f_sh = ffn1_w_down.shape[1]
    assert n_sh == 2 * f_sh and L == DEPTH
    n_pad = -(-n_sh // LANES) * LANES
    r_in, r_out = w_in.shape[1], w_out.shape[1]

    big_shards = dict(
        ffn1_w_gu=_pad_cols(ffn1_w_gu, n_pad).astype(BF16), ffn1_w_down=ffn1_w_down.astype(BF16),
        w_in=_pad_w_in(w_in).astype(BF16), w_small=_small_pack(w_q_b, w_kv_b).astype(BF16), w_out=w_out.astype(BF16),
        ffn2_w_gu=_pad_cols(ffn2_w_gu, n_pad).astype(BF16), ffn2_w_down=ffn2_w_down.astype(BF16))
    ex = _Exchange({(k, l): big_shards[k][l] for k in _BIG for l in range(L)}, D, n_pad, f_sh, r_in, r_out)

    small = {k: W[k] for k in _SMALL}
    loss, dx, grads, d_final = _local_step(x[0], loss_target[0], ex, small)

    small_g = {k: jnp.stack([grads[l][k] for l in range(L)]) for k in _SMALL if k != "final_norm"}
    small_g["final_norm"] = d_final
    pack_g, recipe = _pack_small(small_g)
    n_small = pack_g.shape[0]
    loss_row = -(-n_small // 8) * 8
    pack_g = jnp.concatenate([pack_g, jnp.zeros((loss_row - n_small, LANES), F32), jnp.broadcast_to(loss, (8, LANES))],
                             axis=0)
    *got, packs = _comm_call(ex.scatter_rider("last", 0, pack=pack_g), "scatter_last")
    ex.scattered("last", 0, got)

    out = {}
    sm_w, sm_m, sm_v = (_small_pack(t["w_q_b"], t["w_kv_b"]) for t in (W, M, V))
    big = [("ffn1_w_gu", W["ffn1_w_gu"], M["ffn1_w_gu"], V["ffn1_w_gu"], 256),
           ("ffn1_w_down", W["ffn1_w_down"], M["ffn1_w_down"], V["ffn1_w_down"], 352),
           ("w_in", W["w_in"], M["w_in"], V["w_in"], 128),
           ("w_small", sm_w, sm_m, sm_v, 384),
           ("w_out", W["w_out"], M["w_out"], V["w_out"], 128),
           ("ffn2_w_gu", W["ffn2_w_gu"], M["ffn2_w_gu"], V["ffn2_w_gu"], 256),
           ("ffn2_w_down", W["ffn2_w_down"], M["ffn2_w_down"], V["ffn2_w_down"], 352)]
    for k, w_, m_, v_, tr in big:
        res = None
        for l in range(L):
            res = _adam_sum(ex.recv[(k, l)], w_, m_, v_, l, res, tr, f"adam_{k}_l{l}")
        if k == "w_small":
            cq, ckv = w_q_b.shape[2], w_kv_b.shape[2]
            parts = [_small_unpack(r, cq, ckv) for r in res]
            out["w_q_b"] = [p[0] for p in parts]
            out["w_kv_b"] = [p[1] for p in parts]
        else:
            out[k] = res

    pw, _ = _pack_small({k: W[k] for k in _SMALL})
    pm, _ = _pack_small({k: M[k] for k in _SMALL})
    pv, _ = _pack_small({k: V[k] for k in _SMALL})
    extra = ((0, loss_row + 8 - n_small), (0, 0))
    res = _adam_small(packs, jnp.pad(pw, extra), jnp.pad(pm, extra), jnp.pad(pv, extra), "adam_small")
    loss_total = res[0][loss_row, 0]
    small_out = [_unpack_small(r, recipe) for r in res]
    for k in _SMALL:
        out[k] = [t[k] for t in small_out]

    names = ["ffn1_norm", "ffn1_w_gu", "ffn1_w_down", "mix_norm", "w_in", "q_a_norm", "w_q_b", "kv_a_norm", "w_kv_b",
             "pool_w", "pool_scale", "fox_b_f", "w_out", "ffn2_norm", "ffn2_w_gu", "ffn2_w_down", "final_norm"]
    outs = [loss_total, dx[None]]
    for which in range(4):
        outs += [out[k][which] for k in names]
    return tuple(outs)
```

```python
import functools
import math

import numpy as np
import jax
import jax.numpy as jnp
from jax import lax
from jax.experimental import pallas as pl
from jax.experimental.pallas import tpu as pltpu

F32 = jnp.float32
BF16 = jnp.bfloat16
MESH_ID = pl.DeviceIdType.MESH

N_DEV = 8
EPS = 1e-6
DEPTH = 2

MLA_HEADS = 6
MLA_Q_RANK = 256
MLA_KV_RANK = 128
MLA_NOPE = 64
MLA_ROPE = 32
MLA_V = 64
ROPE_THETA = 10000.0
POOL_WINDOWS = (2, 4, 8, 16)
POOL_GROUP = 64
POOL_WIDTH = 256
FOX_HEADS = 6
FOX_HEAD_DIM = 64
N_IN = 1830

ADAM_LR = 0.001
ADAM_B1 = 0.9
ADAM_B2 = 0.999
ADAM_EPS = 1e-08
ADAM_WD = 0.01
ADAM_STEP = 10

LANES = 128
HEAD_BLOCK = 128
VMEM_LIMIT = 48 * 1024 * 1024
NEG = -1e30

ZA = 768
ZF = 1152
N_PAD = ZA + ZF
TAIL0 = 640
ROPE_LANE0 = 64


def _f_lane(h):
    return 8 * (h // 2) + (h % 2)


def _in_perm():
    perm = -np.ones(N_PAD, np.int32)
    perm[0:256] = np.arange(0, 256)
    perm[256:384] = np.arange(256, 384)
    perm[384:640] = np.arange(416, 672)
    for h in range(FOX_HEADS):
        perm[TAIL0 + _f_lane(h)] = 1824 + h
    perm[TAIL0 + ROPE_LANE0:TAIL0 + ROPE_LANE0 + MLA_ROPE] = np.arange(384, 416)
    perm[ZA:N_PAD] = np.arange(672, 1824)
    inv = np.zeros(N_IN, np.int32)
    for new, old in enumerate(perm):
        if old >= 0:
            inv[old] = new
    return perm, inv


_IN_PERM, _IN_INV = _in_perm()
_F_LANES = np.array([_f_lane(h) for h in range(FOX_HEADS)], np.int32)


def _dot(a, b):
    return jnp.dot(a, b, preferred_element_type=F32)


def _dot_nt(a, b):
    return lax.dot_general(a, b, (((1,), (1,)), ((), ())), preferred_element_type=F32)


def _dot_tn(a, b):
    return lax.dot_general(a, b, (((0,), (0,)), ((), ())), preferred_element_type=F32)


def _rms(x, gam):
    r = lax.rsqrt(jnp.mean(x * x, axis=-1, keepdims=True) + EPS)
    return x * r * gam


def _rms_bwd(dy, x, gam):
    r = lax.rsqrt(jnp.mean(x * x, axis=-1, keepdims=True) + EPS)
    xh = x * r
    dxh = dy * gam
    dx = r * (dxh - xh * jnp.mean(dxh * xh, axis=-1, keepdims=True))
    return dx, jnp.sum(dy * xh, axis=0, keepdims=True)


def _accum_out(ref, first, val):
    @pl.when(first)
    def _():
        ref[...] = val

    @pl.when(jnp.logical_not(first))
    def _():
        ref[...] += val


def _bs(shape, fn):
    return pl.BlockSpec(shape, fn)


def _params(dims):
    return pltpu.CompilerParams(dimension_semantics=dims, vmem_limit_bytes=VMEM_LIMIT)


def _tile(n, t):
    t = min(n, t)
    assert n % t == 0, (n, t)
    return t


HBM_SPEC = pl.BlockSpec(memory_space=pl.ANY)


def _call(body, *, name, grid, in_specs, out_specs, out_shape, scratch, dims, args, rider=None):
    n_in, n_out = len(in_specs), len(out_specs)
    if rider is None:
        outs = pl.pallas_call(body, name=name, grid=grid, in_specs=in_specs, out_specs=out_specs, out_shape=out_shape,
                              scratch_shapes=scratch, compiler_params=_params(dims))(*args)
        return list(outs), []
    k_in, k_out, k_sem = len(rider.srcs), len(rider.out_shapes), len(rider.scratch)

    def riding(*refs):
        a, b, c, d = n_in, n_in + k_in, n_in + k_in + n_out, n_in + k_in + n_out + k_out
        rest = refs[d:]
        sems = rest[len(rest) - k_sem:]
        step = 0
        for i, g in enumerate(grid):
            step = step * g + pl.program_id(i)
        n_steps = math.prod(grid)

        @pl.when(step == 0)
        def _():
            rider.begin(refs[a:b], refs[c:d], sems)

        body(*refs[:a], *refs[b:c], *rest[:len(rest) - k_sem])

        @pl.when(step == (3 * n_steps) // 4)
        def _():
            rider.middle(refs[a:b], refs[c:d], sems)

        @pl.when(step == n_steps - 1)
        def _():
            rider.end(refs[a:b], refs[c:d], sems)

    outs = pl.pallas_call(
        riding, name=name, grid=grid, in_specs=list(in_specs) + [HBM_SPEC] * k_in,
        out_specs=list(out_specs) + [HBM_SPEC] * k_out, out_shape=list(out_shape) + list(rider.out_shapes),
        scratch_shapes=list(scratch) + list(rider.scratch),
        compiler_params=_params(("arbitrary",) * len(grid)))(*args, *rider.srcs)
    return list(outs[:n_out]), list(outs[n_out:])


def _ffn_fwd(x, gam, wgu, wd, name, rider=None):
    S, D = x.shape
    _, nj, n, _ = wgu.shape
    tm = _tile(S, 512)

    def body(x_ref, gam_ref, wgu_ref, wd_ref, xo_ref, h_ref, gu_ref, hs, acc):
        j = pl.program_id(1)

        @pl.when(j == 0)
        def _():
            hb = _rms(x_ref[...], gam_ref[...]).astype(BF16)
            hs[...] = hb
            h_ref[...] = hb
            acc[...] = jnp.zeros_like(acc)

        h = hs[...]
        g = _dot_nt(h, wgu_ref[0])
        u = _dot_nt(h, wgu_ref[1])
        gu_ref[0] = g.astype(BF16)
        gu_ref[1] = u.astype(BF16)
        a = (g * jax.nn.sigmoid(g) * u).astype(BF16)
        acc[...] += _dot(a, wd_ref[...])

        @pl.when(j == nj - 1)
        def _():
            xo_ref[...] = x_ref[...] + 0.5 * acc[...]

    return _call(
        body, name=name, grid=(S // tm, nj),
        in_specs=[_bs((tm, D), lambda i, j: (i, 0)), _bs((1, D), lambda i, j: (0, 0)),
                  _bs((2, None, n, D), lambda i, j: (0, j, 0, 0)), _bs((None, n, D), lambda i, j: (j, 0, 0))],
        out_specs=[_bs((tm, D), lambda i, j: (i, 0)), _bs((tm, D), lambda i, j: (i, 0)),
                   _bs((2, None, tm, n), lambda i, j: (0, j, i, 0))],
        out_shape=[jax.ShapeDtypeStruct((S, D), F32), jax.ShapeDtypeStruct((S, D), BF16),
                   jax.ShapeDtypeStruct((2, nj, S, n), BF16)],
        scratch=[pltpu.VMEM((tm, D), BF16), pltpu.VMEM((tm, D), F32)],
        dims=("parallel", "arbitrary"), args=(x, gam, wgu, wd), rider=rider)


def _ffn_bwd(dy, x, gam, gu, wgu, wd, name, rider=None):
    S, D = x.shape
    _, nj, n, _ = wgu.shape
    tm = _tile(S, 512)

    def body(dy_ref, x_ref, gam_ref, gu_ref, wgu_ref, wd_ref, dx_ref, dgam_ref, dgu_ref, a_ref, dyh_ref, acc):
        j = pl.program_id(1)

        @pl.when(j == 0)
        def _():
            dyh_ref[...] = (0.5 * dy_ref[...]).astype(BF16)
            acc[...] = jnp.zeros_like(acc)

        dg, du, a = _swiglu_bwd(_dot_nt(dyh_ref[...], wd_ref[...]), gu_ref[0], gu_ref[1])
        dgu_ref[0] = dg
        dgu_ref[1] = du
        a_ref[...] = a
        acc[...] += _dot(dg, wgu_ref[0]) + _dot(du, wgu_ref[1])

        @pl.when(j == nj - 1)
        def _():
            dxn, dgam = _rms_bwd(acc[...], x_ref[...], gam_ref[...])
            dx_ref[...] = dy_ref[...] + dxn
            _accum_out(dgam_ref, pl.program_id(0) == 0, dgam)

    row = lambda i, j: (i, 0)
    return _call(
        body, name=name, grid=(S // tm, nj),
        in_specs=[_bs((tm, D), row), _bs((tm, D), row), _bs((1, D), lambda i, j: (0, 0)),
                  _bs((2, None, tm, n), lambda i, j: (0, j, i, 0)),
                  _bs((2, None, n, D), lambda i, j: (0, j, 0, 0)), _bs((None, n, D), lambda i, j: (j, 0, 0))],
        out_specs=[_bs((tm, D), row), _bs((1, D), lambda i, j: (0, 0)),
                   _bs((2, None, tm, n), lambda i, j: (0, j, i, 0)), _bs((None, tm, n), lambda i, j: (j, i, 0)),
                   _bs((tm, D), row)],
        out_shape=[jax.ShapeDtypeStruct((S, D), F32), jax.ShapeDtypeStruct((1, D), F32),
                   jax.ShapeDtypeStruct((2, nj, S, n), BF16), jax.ShapeDtypeStruct((nj, S, n), BF16),
                   jax.ShapeDtypeStruct((S, D), BF16)],
        scratch=[pltpu.VMEM((tm, D), F32)],
        dims=("arbitrary", "arbitrary"), args=(dy, x, gam, gu, wgu, wd), rider=rider)


def _swiglu_bwd(da, g, u):
    g = g.astype(F32)
    u = u.astype(F32)
    sig = jax.nn.sigmoid(g)
    sl = g * sig
    dg = (da * u * (sig * (1.0 + g * (1.0 - sig)))).astype(BF16)
    return dg, (da * sl).astype(BF16), (sl * u).astype(BF16)


def _ffn_bwd_act(dy, gu, wd, name, rider=None):
    S, D = dy.shape
    _, nj, _, n = gu.shape
    tm = _tile(S, 512)

    def body(dy_ref, gu_ref, wd_ref, dgu_ref, a_ref, dyh_ref):
        @pl.when(pl.program_id(1) == 0)
        def _():
            dyh_ref[...] = (0.5 * dy_ref[...]).astype(BF16)

        dg, du, a = _swiglu_bwd(_dot_nt(dyh_ref[...], wd_ref[...]), gu_ref[0], gu_ref[1])
        dgu_ref[0] = dg
        dgu_ref[1] = du
        a_ref[...] = a

    row = lambda i, j: (i, 0)
    return _call(
        body, name=name, grid=(S // tm, nj),
        in_specs=[_bs((tm, D), row), _bs((2, None, tm, n), lambda i, j: (0, j, i, 0)),
                  _bs((None, n, D), lambda i, j: (j, 0, 0))],
        out_specs=[_bs((2, None, tm, n), lambda i, j: (0, j, i, 0)), _bs((None, tm, n), lambda i, j: (j, i, 0)),
                   _bs((tm, D), row)],
        out_shape=[jax.ShapeDtypeStruct((2, nj, S, n), BF16), jax.ShapeDtypeStruct((nj, S, n), BF16),
                   jax.ShapeDtypeStruct((S, D), BF16)],
        scratch=[], dims=("parallel", "arbitrary"), args=(dy, gu, wd), rider=rider)


def _ffn_bwd_in(dy, x, gam, dgu, wgu, name, rider=None):
    S, D = x.shape
    _, nj, n, _ = wgu.shape
    tm = _tile(S, 512)

    def body(dy_ref, x_ref, gam_ref, dgu_ref, wgu_ref, dx_ref, dgam_ref, acc):
        j = pl.program_id(1)

        @pl.when(j == 0)
        def _():
            acc[...] = jnp.zeros_like(acc)

        acc[...] += _dot(dgu_ref[0], wgu_ref[0]) + _dot(dgu_ref[1], wgu_ref[1])

        @pl.when(j == nj - 1)
        def _():
            dxn, dgam = _rms_bwd(acc[...], x_ref[...], gam_ref[...])
            dx_ref[...] = dy_ref[...] + dxn
            _accum_out(dgam_ref, pl.program_id(0) == 0, dgam)

    row = lambda i, j: (i, 0)
    return _call(
        body, name=name, grid=(S // tm, nj),
        in_specs=[_bs((tm, D), row), _bs((tm, D), row), _bs((1, D), lambda i, j: (0, 0)),
                  _bs((2, None, tm, n), lambda i, j: (0, j, i, 0)), _bs((2, None, n, D), lambda i, j: (0, j, 0, 0))],
        out_specs=[_bs((tm, D), row), _bs((1, D), lambda i, j: (0, 0))],
        out_shape=[jax.ShapeDtypeStruct((S, D), F32), jax.ShapeDtypeStruct((1, D), F32)],
        scratch=[pltpu.VMEM((tm, D), F32)],
        dims=("arbitrary", "arbitrary"), args=(dy, x, gam, dgu, wgu), rider=rider)


def _mm_tn(a, b, nb, a_of, b_of, tm, tn, name, rider=None, ts=512):
    _, S, M = a.shape
    N = b.shape[2]
    tm = _tile(M, tm)
    tn = _tile(N, tn)
    ts = _tile(S, ts)
    nk = S // ts

    def body(a_ref, b_ref, o_ref, acc):
        k = pl.program_id(3)

        @pl.when(k == 0)
        def _():
            acc[...] = jnp.zeros_like(acc)

        acc[...] += _dot_tn(a_ref[...].astype(BF16), b_ref[...].astype(BF16))

        @pl.when(k == nk - 1)
        def _():
            o_ref[...] = acc[...].astype(o_ref.dtype)

    (out,), extra = _call(
        body, name=name, grid=(nb, M // tm, N // tn, nk),
        in_specs=[_bs((None, ts, tm), lambda p, i, j, k: (a_of(p), k, i)),
                  _bs((None, ts, tn), lambda p, i, j, k: (b_of(p), k, j))],
        out_specs=[_bs((None, tm, tn), lambda p, i, j, k: (p, i, j))],
        out_shape=[jax.ShapeDtypeStruct((nb, M, N), BF16)],
        scratch=[pltpu.VMEM((tm, tn), F32)],
        dims=("parallel", "parallel", "parallel", "arbitrary"), args=(a, b), rider=rider)
    return (out, extra) if rider is not None else out


def _mix_in_fwd(x, gam, w_in, name):
    S, D = x.shape
    tm = _tile(S, 512)
    nkv = (ZF - 384) // LANES

    def body(x_ref, gam_ref, w_ref, h_ref, za_ref, zf_ref, zt_ref):
        hb = _rms(x_ref[...], gam_ref[...]).astype(BF16)
        h_ref[...] = hb
        za_ref[...] = _dot(hb, w_ref[:, 0:ZA])
        zf = _dot(hb, w_ref[:, ZA:N_PAD])
        zf_ref[...] = zf.astype(BF16)
        for c in range(nkv):
            zt_ref[c * LANES:(c + 1) * LANES, :] = zf[:, 384 + c * LANES:384 + (c + 1) * LANES].T.astype(BF16)

    return pl.pallas_call(
        body, name=name, grid=(S // tm,),
        in_specs=[_bs((tm, D), lambda i: (i, 0)), _bs((1, D), lambda i: (0, 0)), _bs((D, N_PAD), lambda i: (0, 0))],
        out_specs=[_bs((tm, D), lambda i: (i, 0)), _bs((tm, ZA), lambda i: (i, 0)), _bs((tm, ZF), lambda i: (i, 0)),
                   _bs((nkv * LANES, tm), lambda i: (0, i))],
        out_shape=[jax.ShapeDtypeStruct((S, D), BF16), jax.ShapeDtypeStruct((S, ZA), F32),
                   jax.ShapeDtypeStruct((S, ZF), BF16), jax.ShapeDtypeStruct((nkv * LANES, S), BF16)],
        compiler_params=_params(("parallel",)),
    )(x, gam, w_in)


def _mix_in_bwd(dy, x, gam, dza_mla, du, dtail_f, dqf, dkf, dvf, w_in, name):
    S, D = x.shape
    tm = _tile(S, 512)

    def body(dy_ref, x_ref, gam_ref, dza_ref, du_ref, dt_ref, dq_ref, dk_ref, dv_ref, w_ref, dx_ref, dgam_ref, dz_ref):
        i = pl.program_id(0)
        dza = dza_ref[...]
        dz = jnp.concatenate([dza[:, 0:384], du_ref[...], dza[:, TAIL0:ZA] + dt_ref[...],
                              dq_ref[...], dk_ref[...], dv_ref[...]], axis=1).astype(BF16)
        dz_ref[...] = dz
        dh = _dot_nt(dz, w_ref[...])
        dxn, dgam = _rms_bwd(dh, x_ref[...], gam_ref[...])
        dx_ref[...] = dy_ref[...] + dxn
        _accum_out(dgam_ref, i == 0, dgam)

    row = lambda i: (i, 0)
    fix = lambda i: (0, 0)
    return pl.pallas_call(
        body, name=name, grid=(S // tm,),
        in_specs=[_bs((tm, D), row), _bs((tm, D), row), _bs((1, D), fix), _bs((tm, ZA), row), _bs((tm, 256), row),
                  _bs((tm, 128), row), _bs((tm, 384), row), _bs((tm, 384), row), _bs((tm, 384), row),
                  _bs((D, N_PAD), fix)],
        out_specs=[_bs((tm, D), row), _bs((1, D), fix), _bs((tm, N_PAD), row)],
        out_shape=[jax.ShapeDtypeStruct((S, D), F32), jax.ShapeDtypeStruct((1, D), F32),
                   jax.ShapeDtypeStruct((S, N_PAD), BF16)],
        compiler_params=_params(("arbitrary",)),
    )(dy, x, gam, dza_mla, du, dtail_f, dqf, dkf, dvf, w_in)


def _rope_tables(S):
    half = MLA_ROPE // 2
    inv_freq = ROPE_THETA ** (-jnp.arange(0, MLA_ROPE, 2, dtype=F32) / MLA_ROPE)
    ang = jnp.arange(S, dtype=jnp.int32).astype(F32)[:, None] * inv_freq[None, :]
    cos, sin = jnp.cos(ang), jnp.sin(ang)
    one = jnp.ones((S, ROPE_LANE0), F32)
    zero = jnp.zeros((S, ROPE_LANE0), F32)
    pad1 = jnp.ones((S, LANES - ROPE_LANE0 - MLA_ROPE), F32)
    pad0 = jnp.zeros((S, LANES - ROPE_LANE0 - MLA_ROPE), F32)
    zh = jnp.zeros((S, half), F32)
    tab_c = jnp.concatenate([one, cos, cos, pad1], axis=1)
    tab_ck = jnp.concatenate([zero, cos, cos, pad0], axis=1)
    tab_s1 = jnp.concatenate([zero, -sin, zh, pad0], axis=1)
    tab_s2 = jnp.concatenate([zero, zh, sin, pad0], axis=1)
    return tab_c, tab_ck, tab_s1, tab_s2


def _rope(x, c, s1, s2):
    return x * c + pltpu.roll(x, LANES - 16, 1) * s1 + pltpu.roll(x, 16, 1) * s2


def _rope_t(dy, c, s1, s2):
    return dy * c + pltpu.roll(dy * s1, 16, 1) + pltpu.roll(dy * s2, LANES - 16, 1)


_MLA_SCALE = 1.0 / math.sqrt(MLA_NOPE + MLA_ROPE)


def _mla_prep(za, gq, gkv, wq, wkv, tabs, name):
    S = za.shape[0]
    tm = _tile(S, 512)
    H = MLA_HEADS

    def body(zq_ref, tail_ref, gq_ref, gkv_ref, wq_ref, wkv_ref, c_ref, ck_ref, s1_ref, s2_ref,
             qf_ref, kf_ref, v_ref, kft_ref, vt_ref):
        zq = zq_ref[...]
        c, s1, s2 = c_ref[...], s1_ref[...], s2_ref[...]
        qn = _rms(zq[:, 0:256], gq_ref[...]).astype(BF16)
        q = _dot(qn, wq_ref[...])
        for h in range(H):
            blk = _rope(q[:, h * LANES:(h + 1) * LANES], c, s1, s2)
            qf_ref[:, h * LANES:(h + 1) * LANES] = (blk * _MLA_SCALE).astype(BF16)
        kvn = _rms(zq[:, 256:384], gkv_ref[...]).astype(BF16)
        kv = _dot(kvn, wkv_ref[...])
        kt = _rope(tail_ref[...], ck_ref[...], s1, s2)
        for h in range(H):
            sl = slice(h * LANES, (h + 1) * LANES)
            kblk = kv[:, sl] + kt
            kf_ref[:, sl] = kblk.astype(BF16)
            kft_ref[sl, :] = kblk.T.astype(BF16)
        v_ref[...] = kv[:, H * LANES:].astype(BF16)
        for cblk in range(H * MLA_V // LANES):
            sl = slice(cblk * LANES, (cblk + 1) * LANES)
            vt_ref[sl, :] = kv[:, H * LANES + cblk * LANES:H * LANES + (cblk + 1) * LANES].T.astype(BF16)

    row = lambda i: (i, 0)
    col = lambda i: (0, i)
    fix = lambda i: (0, 0)
    return pl.pallas_call(
        body, name=name, grid=(S // tm,),
        in_specs=[_bs((tm, 384), row), _bs((tm, 128), lambda i: (i, TAIL0 // 128)), _bs((1, 256), fix), _bs((1, 128), fix),
                  _bs((256, 768), fix), _bs((128, 1152), fix),
                  _bs((tm, 128), row), _bs((tm, 128), row), _bs((tm, 128), row), _bs((tm, 128), row)],
        out_specs=[_bs((tm, 768), row), _bs((tm, 768), row), _bs((tm, 384), row), _bs((768, tm), col), _bs((384, tm), col)],
        out_shape=[jax.ShapeDtypeStruct((S, 768), BF16), jax.ShapeDtypeStruct((S, 768), BF16),
                   jax.ShapeDtypeStruct((S, 384), BF16), jax.ShapeDtypeStruct((768, S), BF16),
                   jax.ShapeDtypeStruct((384, S), BF16)],
        compiler_params=_params(("parallel",)),
    )(za, za, gq, gkv, wq, wkv, *tabs)


def _mla_prep_bwd(za, gq, gkv, wq, wkv, tabs, dqf, dkf, dvm, name):
    S = za.shape[0]
    tm = _tile(S, 512)
    H = MLA_HEADS

    def body(zq_ref, gq_ref, gkv_ref, wq_ref, wkv_ref, c_ref, ck_ref, s1_ref, s2_ref, dqf_ref, dkf_ref, dvm_ref,
             dza_ref, dwq_ref, dwkv_ref, dgq_ref, dgkv_ref):
        i = pl.program_id(0)
        first = i == 0
        zq = zq_ref[...]
        c, s1, s2 = c_ref[...], s1_ref[...], s2_ref[...]
        lane = lax.broadcasted_iota(jnp.int32, (1, LANES), 1)
        nope = lane < MLA_NOPE
        rope = jnp.logical_and(lane >= ROPE_LANE0, lane < ROPE_LANE0 + MLA_ROPE)

        qa = zq[:, 0:256]
        qn = _rms(qa, gq_ref[...]).astype(BF16)
        dqf = dqf_ref[...]
        dq_pre = jnp.concatenate(
            [_rope_t(dqf[:, h * LANES:(h + 1) * LANES] * _MLA_SCALE, c, s1, s2) for h in range(H)], axis=1).astype(BF16)
        _accum_out(dwq_ref, first, _dot_tn(qn, dq_pre))
        dqa, dgq = _rms_bwd(_dot_nt(dq_pre, wq_ref[...]), qa, gq_ref[...])
        _accum_out(dgq_ref, first, dgq)

        kva = zq[:, 256:384]
        kvn = _rms(kva, gkv_ref[...]).astype(BF16)
        dkf = dkf_ref[...]
        parts = []
        dkt = jnp.zeros((tm, LANES), F32)
        for h in range(H):
            blk = dkf[:, h * LANES:(h + 1) * LANES]
            parts.append(jnp.where(nope, blk, 0.0))
            dkt = dkt + jnp.where(rope, blk, 0.0)
        dkv_pre = jnp.concatenate(parts + [dvm_ref[...]], axis=1).astype(BF16)
        _accum_out(dwkv_ref, first, _dot_tn(kvn, dkv_pre))
        dkva, dgkv = _rms_bwd(_dot_nt(dkv_pre, wkv_ref[...]), kva, gkv_ref[...])
        _accum_out(dgkv_ref, first, dgkv)

        dtail = _rope_t(dkt, ck_ref[...], s1, s2)
        dza_ref[...] = jnp.concatenate([dqa, dkva, jnp.zeros((tm, 256), F32), dtail], axis=1)

    row = lambda i: (i, 0)
    fix = lambda i: (0, 0)
    return pl.pallas_call(
        body, name=name, grid=(S // tm,),
        in_specs=[_bs((tm, 384), row), _bs((1, 256), fix), _bs((1, 128), fix), _bs((256, 768), fix), _bs((128, 1152), fix),
                  _bs((tm, 128), row), _bs((tm, 128), row), _bs((tm, 128), row), _bs((tm, 128), row),
                  _bs((tm, 768), row), _bs((tm, 768), row), _bs((tm, 384), row)],
        out_specs=[_bs((tm, ZA), row), _bs((256, 768), fix), _bs((128, 1152), fix), _bs((1, 256), fix), _bs((1, 128), fix)],
        out_shape=[jax.ShapeDtypeStruct((S, ZA), F32), jax.ShapeDtypeStruct((256, 768), F32),
                   jax.ShapeDtypeStruct((128, 1152), F32), jax.ShapeDtypeStruct((1, 256), F32),
                   jax.ShapeDtypeStruct((1, 128), F32)],
        compiler_params=_params(("arbitrary",)),
    )(za, gq, gkv, wq, wkv, *tabs, dqf, dkf, dvm)


def _head_views(qb, kb, r, dkb, sel):
    if dkb == LANES:
        sl = slice(r * LANES, (r + 1) * LANES)
        return qb[:, sl], kb[:, sl], kb[:, sl]
    return jnp.where(sel, qb, jnp.zeros_like(qb)), kb, jnp.where(sel, kb, jnp.zeros_like(kb))


def _attn_fwd_t(q_arr, k_arr, vt_arr, *, nhp, dkb, qoff, koff, vtoff, scale, cum, cumT, name, rider=None):
    S = q_arr.shape[0]
    T = _tile(S, 512)
    nq = S // T
    W = 2 * dkb
    bias = cum is not None

    def body(*refs):
        if bias:
            q_ref, k_ref, vt_ref, cq_ref, ck_ref, o_ref, lse_ref, m_s, l_s, acc_s = refs
        else:
            q_ref, k_ref, vt_ref, o_ref, lse_ref, m_s, l_s, acc_s = refs
        hp, qi, ki = pl.program_id(0), pl.program_id(1), pl.program_id(2)
        lo_lane = lax.broadcasted_iota(jnp.int32, (1, LANES), 1) < 64
        lo_row = lax.broadcasted_iota(jnp.int32, (LANES, 1), 0) < 64

        @pl.when(ki == 0)
        def _():
            m_s[...] = jnp.full_like(m_s, NEG)
            l_s[...] = jnp.zeros_like(l_s)
            acc_s[...] = jnp.zeros_like(acc_s)

        def step(masked):
            qb, kb, vtb = q_ref[...], k_ref[...], vt_ref[...]
            if masked:
                mask = lax.broadcasted_iota(jnp.int32, (T, T), 0) <= lax.broadcasted_iota(jnp.int32, (T, T), 1)
            if bias:
                li = lax.broadcasted_iota(jnp.int32, (T, LANES), 1)
                ckb = ck_ref[...]
            for r in range(2):
                sel = lo_lane if r == 0 else jnp.logical_not(lo_lane)
                rsel = lo_row if r == 0 else jnp.logical_not(lo_row)
                q, k, _ = _head_views(qb, kb, r, dkb, sel)
                if scale != 1.0:
                    q = q * jnp.asarray(scale, q.dtype)
                s = _dot_nt(k, q)
                if bias:
                    ck = jnp.sum(jnp.where(li == 8 * hp + r, ckb, 0.0), axis=1, keepdims=True)
                    s = s + (cq_ref[r:r + 1, :] - ck)
                if masked:
                    s = jnp.where(mask, s, NEG)
                m_prev = m_s[r:r + 1, :]
                m_new = jnp.maximum(m_prev, jnp.max(s, axis=0, keepdims=True))
                alpha = jnp.exp(m_prev - m_new)
                p = jnp.exp(s - m_new)
                l_s[r:r + 1, :] = alpha * l_s[r:r + 1, :] + jnp.sum(p, axis=0, keepdims=True)
                m_s[r:r + 1, :] = m_new
                pv = _dot(jnp.where(rsel, vtb, jnp.zeros_like(vtb)), p.astype(BF16))
                acc_s[...] = acc_s[...] * jnp.where(rsel, alpha, 1.0) + pv

        @pl.when(ki < qi)
        def _():
            step(False)

        @pl.when(ki == qi)
        def _():
            step(True)

        @pl.when(ki == nq - 1)
        def _():
            inv = jnp.where(lo_row, 1.0 / l_s[0:1, :], 1.0 / l_s[1:2, :])
            o_ref[...] = (acc_s[...] * inv).T.astype(BF16)
            used = lax.broadcasted_iota(jnp.int32, (8, T), 0) < 2
            lse_ref[...] = jnp.where(used, m_s[...] + jnp.log(jnp.where(used, l_s[...], 1.0)), 0.0)

    kmap = lambda hp, qi, ki: jnp.minimum(ki, qi)
    in_specs = [_bs((T, W), lambda hp, qi, ki: (qi, qoff + hp)),
                _bs((T, W), lambda hp, qi, ki: (kmap(hp, qi, ki), koff + hp)),
                _bs((LANES, T), lambda hp, qi, ki: (vtoff + hp, kmap(hp, qi, ki)))]
    args = [q_arr, k_arr, vt_arr]
    if bias:
        in_specs += [_bs((8, T), lambda hp, qi, ki: (hp, qi)), _bs((T, LANES), lambda hp, qi, ki: (kmap(hp, qi, ki), 0))]
        args += [cumT, cum]
    return _call(
        body, name=name, grid=(nhp, nq, nq),
        in_specs=in_specs,
        out_specs=[_bs((T, LANES), lambda hp, qi, ki: (qi, hp)), _bs((None, 8, T), lambda hp, qi, ki: (hp, 0, qi))],
        out_shape=[jax.ShapeDtypeStruct((S, nhp * LANES), BF16), jax.ShapeDtypeStruct((nhp, 8, S), F32)],
        scratch=[pltpu.VMEM((8, T), F32), pltpu.VMEM((8, T), F32), pltpu.VMEM((LANES, T), F32)],
        dims=("parallel", "parallel", "arbitrary"), args=args, rider=rider)


def _attn_dvec(o_arr, do_arr, nhp, name):
    S = o_arr.shape[0]
    T = _tile(S, 512)

    def body(o_ref, do_ref, d_ref):
        prod = do_ref[...].astype(F32) * o_ref[...].astype(F32)
        li = lax.broadcasted_iota(jnp.int32, (T, LANES), 1)
        d0 = jnp.sum(jnp.where(li < 64, prod, 0.0), axis=1, keepdims=True)
        d1 = jnp.sum(jnp.where(li >= 64, prod, 0.0), axis=1, keepdims=True)
        d_ref[...] = jnp.where(li == 0, d0, jnp.where(li == 1, d1, 0.0)).T[0:8, :]

    return pl.pallas_call(
        body, name=name, grid=(nhp, S // T),
        in_specs=[_bs((T, LANES), lambda hp, i: (i, hp)), _bs((T, LANES), lambda hp, i: (i, hp))],
        out_specs=_bs((None, 8, T), lambda hp, i: (hp, 0, i)),
        out_shape=jax.ShapeDtypeStruct((nhp, 8, S), F32),
        compiler_params=_params(("parallel", "parallel")),
    )(o_arr, do_arr)


def _attn_bwd_t(q_arr, k_arr, kt_arr, v_arr, do_arr, lse, dvec, *, nhp, dkb, qoff, koff, ktoff, voff, scale, cum, cumT,
                name, rider=None):
    S = q_arr.shape[0]
    T = _tile(S, 512)
    nq = S // T
    W = 2 * dkb
    bias = cum is not None

    def body(*refs):
        if bias:
            (q_ref, k_ref, kt_ref, v_ref, do_ref, lse_ref, dvec_ref, cq_ref, ck_ref,
             dq_ref, dk_ref, dv_ref, dcq_ref, dck_ref, dqt_s, dk_s, dv_s, dcq_s, dck_s) = refs
        else:
            (q_ref, k_ref, kt_ref, v_ref, do_ref, lse_ref, dvec_ref,
             dq_ref, dk_ref, dv_ref, dqt_s, dk_s, dv_s) = refs
        hp, ki, qi = pl.program_id(0), pl.program_id(1), pl.program_id(2)
        lo_lane = lax.broadcasted_iota(jnp.int32, (1, LANES), 1) < 64
        lo_row = lax.broadcasted_iota(jnp.int32, (LANES, 1), 0) < 64

        @pl.when(jnp.logical_and(ki == 0, qi == 0))
        def _():
            dqt_s[...] = jnp.zeros_like(dqt_s)
            if bias:
                dcq_s[...] = jnp.zeros_like(dcq_s)

        @pl.when(qi == 0)
        def _():
            dk_s[...] = jnp.zeros_like(dk_s)
            dv_s[...] = jnp.zeros_like(dv_s)
            if bias:
                dck_s[...] = jnp.zeros_like(dck_s)

        def step(masked):
            qb, kb, ktb, vb, dob = q_ref[...], k_ref[...], kt_ref[...], v_ref[...], do_ref[...]
            if masked:
                mask = lax.broadcasted_iota(jnp.int32, (T, T), 0) <= lax.broadcasted_iota(jnp.int32, (T, T), 1)
            if bias:
                li = lax.broadcasted_iota(jnp.int32, (T, LANES), 1)
                ckb = ck_ref[...]
            for r in range(2):
                sel = lo_lane if r == 0 else jnp.logical_not(lo_lane)
                rsel = lo_row if r == 0 else jnp.logical_not(lo_row)
                q, k, _ = _head_views(qb, kb, r, dkb, sel)
                if scale != 1.0:
                    q = q * jnp.asarray(scale, q.dtype)
                s = _dot_nt(k, q)
                if bias:
                    ck = jnp.sum(jnp.where(li == 8 * hp + r, ckb, 0.0), axis=1, keepdims=True)
                    s = s + (cq_ref[r:r + 1, :] - ck)
                p = jnp.exp(s - lse_ref[r:r + 1, :])
                if masked:
                    p = jnp.where(mask, p, 0.0)
                do_r = jnp.where(sel, dob, jnp.zeros_like(dob))
                dp = _dot_nt(vb, do_r)
                ds = p * (dp - dvec_ref[r:r + 1, :])
                pb = p.astype(BF16)
                dsb = ds.astype(BF16)
                dv_s[...] += _dot(pb, do_r)
                if dkb == LANES:
                    sl = slice(r * LANES, (r + 1) * LANES)
                    dk_s[:, sl] += _dot(dsb, q)
                    dqt_s[qi, sl, :] += _dot(ktb[sl, :], dsb) * scale
                else:
                    dk_s[...] += _dot(dsb, q)
                    dqt_s[qi] += _dot(jnp.where(rsel, ktb, jnp.zeros_like(ktb)), dsb) * scale
                if bias:
                    dcq_s[qi, r:r + 1, :] += jnp.sum(ds, axis=0, keepdims=True)
                    dck_s[...] -= jnp.where(li == 8 * hp + r, jnp.sum(ds, axis=1, keepdims=True), 0.0)

        @pl.when(qi > ki)
        def _():
            step(False)

        @pl.when(qi == ki)
        def _():
            step(True)

        @pl.when(qi == nq - 1)
        def _():
            dk_ref[...] = dk_s[...]
            dv_ref[...] = dv_s[...]
            if bias:
                dck_ref[...] = dck_s[...]

        @pl.when(jnp.logical_and(ki == nq - 1, qi == nq - 1))
        def _():
            for c in range(nq):
                dq_ref[c * T:(c + 1) * T, :] = dqt_s[c].T
                if bias:
                    dcq_ref[:, c * T:(c + 1) * T] = dcq_s[c]

    qmap = lambda hp, ki, qi: jnp.maximum(qi, ki)
    in_specs = [_bs((T, W), lambda hp, ki, qi: (qmap(hp, ki, qi), qoff + hp)),
                _bs((T, W), lambda hp, ki, qi: (ki, koff + hp)),
                _bs((W, T), lambda hp, ki, qi: (ktoff + hp, ki)),
                _bs((T, LANES), lambda hp, ki, qi: (ki, voff + hp)),
                _bs((T, LANES), lambda hp, ki, qi: (qmap(hp, ki, qi), hp)),
                _bs((None, 8, T), lambda hp, ki, qi: (hp, 0, qmap(hp, ki, qi))),
                _bs((None, 8, T), lambda hp, ki, qi: (hp, 0, qmap(hp, ki, qi)))]
    args = [q_arr, k_arr, kt_arr, v_arr, do_arr, lse, dvec]
    out_specs = [_bs((S, W), lambda hp, ki, qi: (0, hp)), _bs((T, W), lambda hp, ki, qi: (ki, hp)),
                 _bs((T, LANES), lambda hp, ki, qi: (ki, hp))]
    out_shape = [jax.ShapeDtypeStruct((S, nhp * W), F32), jax.ShapeDtypeStruct((S, nhp * W), F32),
                 jax.ShapeDtypeStruct((S, nhp * LANES), F32)]
    scratch = [pltpu.VMEM((nq, W, T), F32), pltpu.VMEM((T, W), F32), pltpu.VMEM((T, LANES), F32)]
    if bias:
        in_specs += [_bs((8, T), lambda hp, ki, qi: (hp, qmap(hp, ki, qi))), _bs((T, LANES), lambda hp, ki, qi: (ki, 0))]
        args += [cumT, cum]
        out_specs += [_bs((None, 8, S), lambda hp, ki, qi: (hp, 0, 0)), _bs((None, T, LANES), lambda hp, ki, qi: (hp, ki, 0))]
        out_shape += [jax.ShapeDtypeStruct((nhp, 8, S), F32), jax.ShapeDtypeStruct((nhp, S, LANES), F32)]
        scratch += [pltpu.VMEM((nq, 8, T), F32), pltpu.VMEM((T, LANES), F32)]
    return _call(body, name=name, grid=(nhp, nq, nq), in_specs=in_specs, out_specs=out_specs, out_shape=out_shape,
                 scratch=scratch, dims=("arbitrary", "arbitrary", "arbitrary"), args=args, rider=rider)


def _gate_lanes(shape):
    lane = lax.broadcasted_iota(jnp.int32, shape, 1)
    return jnp.logical_and(lane < 8 * (FOX_HEADS // 2), lane % 8 < 2)


def _fox_prep(za, b_row, name):
    S = za.shape[0]
    nrow = 8 * (FOX_HEADS // 2)

    def body(tail_ref, b_ref, cum_ref, cumt_ref):
        x = tail_ref[...] + b_ref[...]
        logf = jnp.minimum(x, 0.0) - jnp.log(1.0 + jnp.exp(-jnp.abs(x)))
        y = jnp.where(_gate_lanes((S, LANES)), logf, 0.0)
        row = lax.broadcasted_iota(jnp.int32, (S, LANES), 0)
        k = 1
        while k < S:
            y = y + jnp.where(row >= k, pltpu.roll(y, k, 0), 0.0)
            k *= 2
        cum_ref[...] = y
        cumt_ref[...] = y.T[0:nrow, :]

    return pl.pallas_call(
        body, name=name, grid=(1,),
        in_specs=[_bs((S, LANES), lambda i: (0, TAIL0 // LANES)), _bs((1, LANES), lambda i: (0, 0))],
        out_specs=[_bs((S, LANES), lambda i: (0, 0)), _bs((nrow, S), lambda i: (0, 0))],
        out_shape=[jax.ShapeDtypeStruct((S, LANES), F32), jax.ShapeDtypeStruct((nrow, S), F32)],
        compiler_params=_params(("arbitrary",)),
    )(za, b_row)


def _fox_prep_bwd(za, b_row, dcq, dck, name):
    S = za.shape[0]
    nhp = FOX_HEADS // 2
    nrow = 8 * nhp
    dcq2 = dcq.reshape(nrow, S)

    def body(tail_ref, b_ref, dcq_ref, dck_ref, dt_ref, db_ref):
        x = tail_ref[...] + b_ref[...]
        d = jnp.concatenate([dcq_ref[...], jnp.zeros((LANES - nrow, S), F32)], axis=0).T
        for hp in range(nhp):
            d = d + dck_ref[hp]
        row = lax.broadcasted_iota(jnp.int32, (S, LANES), 0)
        k = 1
        while k < S:
            d = d + jnp.where(row < S - k, pltpu.roll(d, S - k, 0), 0.0)
            k *= 2
        df = jnp.where(_gate_lanes((S, LANES)), d * jax.nn.sigmoid(-x), 0.0)
        dt_ref[...] = df
        db_ref[...] = jnp.sum(df, axis=0, keepdims=True)

    return pl.pallas_call(
        body, name=name, grid=(1,),
        in_specs=[_bs((S, LANES), lambda i: (0, TAIL0 // LANES)), _bs((1, LANES), lambda i: (0, 0)),
                  _bs((nrow, S), lambda i: (0, 0)), _bs((nhp, S, LANES), lambda i: (0, 0, 0))],
        out_specs=[_bs((S, LANES), lambda i: (0, 0)), _bs((1, LANES), lambda i: (0, 0))],
        out_shape=[jax.ShapeDtypeStruct((S, LANES), F32), jax.ShapeDtypeStruct((1, LANES), F32)],
        compiler_params=_params(("arbitrary",)),
    )(za, b_row, dcq2, dck)


def _pool_select(half, lane_lo, vals):
    return jnp.where(lane_lo, jnp.where(half == 0, vals[0], vals[2]), jnp.where(half == 0, vals[1], vals[3]))


def _pool_den(S, half, lane_lo):
    cnt = (lax.broadcasted_iota(jnp.int32, (S, LANES), 0) + 1).astype(F32)
    w = _pool_select(half, lane_lo, [float(x) for x in POOL_WINDOWS])
    return jnp.minimum(cnt, w)


def _pool_fwd(za, wbd, scale, name):
    S = za.shape[0]

    def body(u_ref, w_ref, sc_ref, y_ref, pd_ref):
        half = pl.program_id(0)
        u = u_ref[...]
        row = lax.broadcasted_iota(jnp.int32, (S, LANES), 0)
        lane_lo = lax.broadcasted_iota(jnp.int32, (S, LANES), 1) < POOL_GROUP
        sums = []
        acc = u
        k = 1
        while k < POOL_WINDOWS[-1]:
            acc = acc + jnp.where(row >= k, pltpu.roll(acc, k, 0), 0.0)
            sums.append(acc)
            k *= 2
        pooled = _pool_select(half, lane_lo, sums) / _pool_den(S, half, lane_lo)
        pd = (pooled - u).astype(BF16)
        pd_ref[...] = pd
        y_ref[...] = (_dot(pd, w_ref[...]) * sc_ref[...]).astype(BF16)

    return pl.pallas_call(
        body, name=name, grid=(2,),
        in_specs=[_bs((S, LANES), lambda i: (0, 384 // LANES + i)), _bs((None, LANES, LANES), lambda i: (i, 0, 0)),
                  _bs((1, LANES), lambda i: (0, i))],
        out_specs=[_bs((S, LANES), lambda i: (0, i)), _bs((S, LANES), lambda i: (0, i))],
        out_shape=[jax.ShapeDtypeStruct((S, POOL_WIDTH), BF16), jax.ShapeDtypeStruct((S, POOL_WIDTH), BF16)],
        compiler_params=_params(("parallel",)),
    )(za, wbd, scale)


def _pool_bwd(dyb, pd, wbd, scale, name):
    S = pd.shape[0]

    def body(dy_ref, pd_ref, w_ref, sc_ref, du_ref, dw_ref, dsc_ref):
        half = pl.program_id(0)
        dy = dy_ref[...]
        pd = pd_ref[...]
        w = w_ref[...]
        ypre = _dot(pd, w)
        dsc_ref[...] = jnp.sum(dy * ypre, axis=0, keepdims=True)
        dyp = (dy * sc_ref[...]).astype(BF16)
        dw_ref[...] = _dot_tn(pd, dyp)
        dpd = _dot_nt(dyp, w)
        row = lax.broadcasted_iota(jnp.int32, (S, LANES), 0)
        lane_lo = lax.broadcasted_iota(jnp.int32, (S, LANES), 1) < POOL_GROUP
        acc = dpd / _pool_den(S, half, lane_lo)
        sums = []
        k = 1
        while k < POOL_WINDOWS[-1]:
            acc = acc + jnp.where(row < S - k, pltpu.roll(acc, S - k, 0), 0.0)
            sums.append(acc)
            k *= 2
        du_ref[...] = _pool_select(half, lane_lo, sums) - dpd

    return pl.pallas_call(
        body, name=name, grid=(2,),
        in_specs=[_bs((S, LANES), lambda i: (0, i)), _bs((S, LANES), lambda i: (0, i)),
                  _bs((None, LANES, LANES), lambda i: (i, 0, 0)), _bs((1, LANES), lambda i: (0, i))],
        out_specs=[_bs((S, LANES), lambda i: (0, i)), _bs((None, LANES, LANES), lambda i: (i, 0, 0)),
                   _bs((1, LANES), lambda i: (0, i))],
        out_shape=[jax.ShapeDtypeStruct((S, POOL_WIDTH), F32), jax.ShapeDtypeStruct((2, LANES, LANES), F32),
                   jax.ShapeDtypeStruct((1, POOL_WIDTH), F32)],
        compiler_params=_params(("parallel",)),
    )(dyb, pd, wbd, scale)


def _mix_out_fwd(x, ya, yb, yc, w_out, name):
    S, D = x.shape
    tm = _tile(S, 512)
    K = w_out.shape[0]

    def body(x_ref, ya_ref, yb_ref, yc_ref, w_ref, xo_ref, yc_out):
        ycat = jnp.concatenate([ya_ref[...], yb_ref[...], yc_ref[...]], axis=1)
        yc_out[...] = ycat
        xo_ref[...] = x_ref[...] + _dot(ycat, w_ref[...])

    row = lambda i: (i, 0)
    return pl.pallas_call(
        body, name=name, grid=(S // tm,),
        in_specs=[_bs((tm, D), row), _bs((tm, 384), row), _bs((tm, 256), row), _bs((tm, 384), row),
                  _bs((K, D), lambda i: (0, 0))],
        out_specs=[_bs((tm, D), row), _bs((tm, K), row)],
        out_shape=[jax.ShapeDtypeStruct((S, D), F32), jax.ShapeDtypeStruct((S, K), BF16)],
        compiler_params=_params(("parallel",)),
    )(x, ya, yb, yc, w_out)


def _mix_out_bwd(dy, w_out, name):
    S, D = dy.shape
    tm = _tile(S, 512)
    K = w_out.shape[0]

    def body(dy_ref, w_ref, da_ref, db_ref, dc_ref):
        d = _dot_nt(dy_ref[...].astype(BF16), w_ref[...])
        da_ref[...] = d[:, 0:384].astype(BF16)
        db_ref[...] = d[:, 384:640]
        dc_ref[...] = d[:, 640:1024].astype(BF16)

    row = lambda i: (i, 0)
    return pl.pallas_call(
        body, name=name, grid=(S // tm,),
        in_specs=[_bs((tm, D), row), _bs((K, D), lambda i: (0, 0))],
        out_specs=[_bs((tm, 384), row), _bs((tm, 256), row), _bs((tm, 384), row)],
        out_shape=[jax.ShapeDtypeStruct((S, 384), BF16), jax.ShapeDtypeStruct((S, 256), F32),
                   jax.ShapeDtypeStruct((S, 384), BF16)],
        compiler_params=_params(("parallel",)),
    )(dy, w_out)


def _loss_head(x, gam, target, name):
    S, D = x.shape
    tm = _tile(S, 512)

    def body(x_ref, gam_ref, t_ref, dx_ref, dgam_ref, loss_ref):
        i = pl.program_id(0)
        xv = x_ref[...]
        err = _rms(xv, gam_ref[...]) - t_ref[...]
        part = 0.5 * jnp.sum(jnp.mean(err * err, axis=-1, keepdims=True), axis=0, keepdims=True)
        dxn, dgam = _rms_bwd(err * (1.0 / D), xv, gam_ref[...])
        dx_ref[...] = dxn
        _accum_out(dgam_ref, i == 0, dgam)
        _accum_out(loss_ref, i == 0, jnp.broadcast_to(part, (1, LANES)))

    row = lambda i: (i, 0)
    fix = lambda i: (0, 0)
    return pl.pallas_call(
        body, name=name, grid=(S // tm,),
        in_specs=[_bs((tm, D), row), _bs((1, D), fix), _bs((tm, D), row)],
        out_specs=[_bs((tm, D), row), _bs((1, D), fix), _bs((1, LANES), fix)],
        out_shape=[jax.ShapeDtypeStruct((S, D), F32), jax.ShapeDtypeStruct((1, D), F32),
                   jax.ShapeDtypeStruct((1, LANES), F32)],
        compiler_params=_params(("arbitrary",)),
    )(x, gam, target)


def _adam_math(g, w, m, v):
    m = ADAM_B1 * m + (1.0 - ADAM_B1) * g
    v = ADAM_B2 * v + (1.0 - ADAM_B2) * (g * g)
    m_hat = m / (1.0 - ADAM_B1 ** ADAM_STEP)
    v_hat = v / (1.0 - ADAM_B2 ** ADAM_STEP)
    delta = -ADAM_LR * (m_hat / (jnp.sqrt(v_hat) + ADAM_EPS) + ADAM_WD * w)
    return delta, m, v


def _adam_sum(recv, w, m, v, layer, prev, tr, name):
    L, R, C = w.shape
    Cp = recv.shape[2]
    tr = _tile(R, tr)

    def body(r_ref, w_ref, m_ref, v_ref, *rest):
        g_out, d_out, m_out, v_out = rest[len(rest) - 4:]
        g = r_ref[0, :, 0:C].astype(F32)
        for p in range(1, N_DEV):
            g = g + r_ref[p, :, 0:C].astype(F32)
        delta, mn, vn = _adam_math(g, w_ref[...], m_ref[...], v_ref[...])
        g_out[...] = g
        d_out[...] = delta
        m_out[...] = mn
        v_out[...] = vn

    blk = _bs((None, tr, C), lambda i: (layer, i, 0))
    shp = jax.ShapeDtypeStruct((L, R, C), F32)
    in_specs = [_bs((N_DEV, tr, Cp), lambda i: (0, i, 0)), blk, blk, blk]
    args = [recv, w, m, v]
    aliases = {}
    if prev is not None:
        in_specs += [HBM_SPEC] * 4
        args += list(prev)
        aliases = {4 + k: k for k in range(4)}
    return pl.pallas_call(
        body, name=name, grid=(R // tr,),
        in_specs=in_specs, out_specs=[blk, blk, blk, blk], out_shape=[shp, shp, shp, shp],
        input_output_aliases=aliases, compiler_params=_params(("parallel",)),
    )(*args)


def _dev_index(px, py, pc):
    return 4 * px + 2 * py + pc


class _GatherRider:
    def __init__(self, shards, out_shapes, views, zero_src=None, zero_views=()):
        self.n = len(shards)
        self.views = views
        self.zero_views = list(zero_views) if zero_src is not None else []
        self.srcs = list(shards) + ([zero_src] if self.zero_views else [])
        self.out_shapes = list(out_shapes)
        n, nz = self.n, len(self.zero_views)
        self.scratch = [pltpu.SemaphoreType.DMA((n, 7)), pltpu.SemaphoreType.DMA((n, 7)),
                        pltpu.SemaphoreType.DMA((n,)), pltpu.SemaphoreType.DMA((max(nz, 1),))]

    def _copies(self, ins, outs, sems):
        n = self.n
        send_sems, recv_sems, local_sems, zero_sems = sems
        x, y, c = lax.axis_index("x"), lax.axis_index("y"), lax.axis_index("c")
        me, sibling = (x, y, c), (x, y, 1 - c)
        chips = [(1 - x, y), (x, 1 - y), (1 - x, 1 - y)]

        def rows(a, blk):
            return self.views[a](outs[a], _dev_index(*blk))

        def copy(a, k, blk, to, src=None):
            return pltpu.make_async_remote_copy(
                src_ref=rows(a, blk) if src is None else src, dst_ref=rows(a, blk),
                send_sem=send_sems.at[a, k], recv_sem=recv_sems.at[a, k], device_id=to, device_id_type=MESH_ID)

        local = [pltpu.make_async_copy(ins[a], rows(a, me), local_sems.at[a]) for a in range(n)]
        local += [pltpu.make_async_copy(ins[n], view(outs[a]), zero_sems.at[i])
                  for i, (a, view) in enumerate(self.zero_views)]
        first = []
        for a in range(n):
            first.append(copy(a, 0, me, sibling, src=ins[a]))
            first += [copy(a, 1 + j, me, (*chip, c), src=ins[a]) for j, chip in enumerate(chips)]
        over_ici = [[copy(a, 1 + j, (*chip, c), me) for a in range(n)] for j, chip in enumerate(chips)]
        passed = [[copy(a, 4 + j, (*chip, c), sibling) for a in range(n)] for j, chip in enumerate(chips)]
        from_sibling = [copy(a, 0, sibling, me) for a in range(n)]
        from_sibling += [copy(a, 4 + j, (*chip, 1 - c), me) for a in range(n) for j, chip in enumerate(chips)]
        return local, first, over_ici, passed, from_sibling

    def begin(self, ins, outs, sems):
        local, first, _, _, _ = self._copies(ins, outs, sems)
        for cp in local + first:
            cp.start()

    def middle(self, ins, outs, sems):
        _, _, over_ici, passed, _ = self._copies(ins, outs, sems)
        for arrived, onward in zip(over_ici, passed):
            for cp, fwd in zip(arrived, onward):
                cp.wait_recv()
                fwd.start()

    def end(self, ins, outs, sems):
        local, first, _, passed, from_sibling = self._copies(ins, outs, sems)
        for cp in from_sibling:
            cp.wait_recv()
        for cp in first + [fwd for onward in passed for fwd in onward]:
            cp.wait_send()
        for cp in local:
            cp.wait()


class _ScatterRider:
    _MASKS = [(kx, ky, kc) for kx in (0, 1) for ky in (0, 1) for kc in (0, 1)][1:]

    def __init__(self, srcs, out_shapes, src_of, dst_at):
        self.n = len(srcs)
        self.srcs = list(srcs)
        self.out_shapes = list(out_shapes)
        self.src_of = src_of
        self.dst_at = dst_at
        n = self.n
        self.scratch = [pltpu.SemaphoreType.DMA((n, 7)), pltpu.SemaphoreType.DMA((n, 7)), pltpu.SemaphoreType.DMA((n,))]

    def _copies(self, ins, outs, sems):
        send_sems, recv_sems, local_sems = sems
        x, y, c = lax.axis_index("x"), lax.axis_index("y"), lax.axis_index("c")
        my = _dev_index(x, y, c)
        peers = [(1 - x if kx else x, 1 - y if ky else y, 1 - c if kc else c) for kx, ky, kc in self._MASKS]

        def send(i, k, to):
            return pltpu.make_async_remote_copy(
                src_ref=self.src_of[i](ins[i], _dev_index(*to)), dst_ref=self.dst_at[i](outs[i], my),
                send_sem=send_sems.at[i, k], recv_sem=recv_sems.at[i, k], device_id=to, device_id_type=MESH_ID)

        def arrival(i, k, frm):
            slot = self.dst_at[i](outs[i], _dev_index(*frm))
            return pltpu.make_async_remote_copy(
                src_ref=slot, dst_ref=slot, send_sem=send_sems.at[i, k], recv_sem=recv_sems.at[i, k],
                device_id=frm, device_id_type=MESH_ID)

        local = [pltpu.make_async_copy(self.src_of[i](ins[i], my), self.dst_at[i](outs[i], my), local_sems.at[i])
                 for i in range(self.n)]
        sends = [send(i, k, to) for k, to in enumerate(peers) for i in range(self.n)]
        arrivals = [arrival(i, k, frm) for k, frm in enumerate(peers) for i in range(self.n)]
        return local, sends, arrivals

    def begin(self, ins, outs, sems):
        local, sends, _ = self._copies(ins, outs, sems)
        for cp in local + sends:
            cp.start()

    def middle(self, ins, outs, sems):
        pass

    def end(self, ins, outs, sems):
        local, sends, arrivals = self._copies(ins, outs, sems)
        for cp in arrivals:
            cp.wait_recv()
        for cp in sends:
            cp.wait_send()
        for cp in local:
            cp.wait()


def _comm_call(rider, name):
    k_in = len(rider.srcs)
    k_out = len(rider.out_shapes)

    def body(*refs):
        ins, outs, sems = refs[:k_in], refs[k_in:k_in + k_out], refs[k_in + k_out:]
        rider.begin(ins, outs, sems)
        rider.middle(ins, outs, sems)
        rider.end(ins, outs, sems)

    return pl.pallas_call(
        body, name=name, in_specs=[HBM_SPEC] * k_in, out_specs=[HBM_SPEC] * k_out, out_shape=rider.out_shapes,
        scratch_shapes=rider.scratch, compiler_params=pltpu.CompilerParams(has_side_effects=True),
    )(*rider.srcs)


def _pad_w_in(w):
    take = jnp.take(w, np.maximum(_IN_PERM, 0), axis=-1)
    return jnp.where(_IN_PERM >= 0, take, jnp.zeros_like(take))


def _unpad_w_in(g):
    return jnp.take(g, _IN_INV, axis=-1)


def _small_pack(w_q_b, w_kv_b):
    a = jnp.pad(w_q_b, ((0, 0), (0, 0), (0, LANES - w_q_b.shape[2])))
    b = jnp.pad(w_kv_b, ((0, 0), (0, 0), (0, LANES - w_kv_b.shape[2])))
    return jnp.concatenate([a, b], axis=1)


def _small_unpack(p, cq, ckv):
    return p[:, 0:MLA_Q_RANK, 0:cq], p[:, MLA_Q_RANK:, 0:ckv]


def _mla_weights(wsm):
    H = MLA_HEADS
    cq = H * (MLA_NOPE + MLA_ROPE) // N_DEV
    ckv = H * (MLA_NOPE + MLA_V) // N_DEV
    wq = wsm[:, 0:MLA_Q_RANK, 0:cq].transpose(1, 0, 2).reshape(MLA_Q_RANK, H, MLA_NOPE + MLA_ROPE)
    wq = jnp.pad(wq, ((0, 0), (0, 0), (0, HEAD_BLOCK - MLA_NOPE - MLA_ROPE))).reshape(MLA_Q_RANK, H * HEAD_BLOCK)
    wkv = wsm[:, MLA_Q_RANK:, 0:ckv].transpose(1, 0, 2).reshape(MLA_KV_RANK, H, MLA_NOPE + MLA_V)
    wk = jnp.pad(wkv[:, :, 0:MLA_NOPE], ((0, 0), (0, 0), (0, HEAD_BLOCK - MLA_NOPE))).reshape(MLA_KV_RANK, H * HEAD_BLOCK)
    wv = wkv[:, :, MLA_NOPE:].reshape(MLA_KV_RANK, H * MLA_V)
    return wq, jnp.concatenate([wk, wv], axis=1)


def _mla_grads_to_blocks(dwq, dwkv):
    H = MLA_HEADS
    gq = dwq.reshape(MLA_Q_RANK, H, HEAD_BLOCK)[:, :, 0:MLA_NOPE + MLA_ROPE].reshape(MLA_Q_RANK, N_DEV, -1)
    gk = dwkv[:, 0:H * HEAD_BLOCK].reshape(MLA_KV_RANK, H, HEAD_BLOCK)[:, :, 0:MLA_NOPE]
    gv = dwkv[:, H * HEAD_BLOCK:].reshape(MLA_KV_RANK, H, MLA_V)
    gkv = jnp.concatenate([gk, gv], axis=2).reshape(MLA_KV_RANK, N_DEV, -1)
    return _small_pack(gq.transpose(1, 0, 2), gkv.transpose(1, 0, 2)).astype(BF16)


def _pool_blockdiag(pool_w):
    z = jnp.zeros((POOL_GROUP, POOL_GROUP), pool_w.dtype)
    halves = [jnp.concatenate([jnp.concatenate([pool_w[2 * i], z], axis=1),
                               jnp.concatenate([z, pool_w[2 * i + 1]], axis=1)], axis=0) for i in range(2)]
    return jnp.stack(halves)


def _pool_blockdiag_t(dw):
    g = POOL_GROUP
    return jnp.stack([dw[0, 0:g, 0:g], dw[0, g:, g:], dw[1, 0:g, 0:g], dw[1, g:, g:]])


def _gate_row(b):
    return jnp.zeros((LANES,), b.dtype).at[_F_LANES].set(b).reshape(1, LANES)


_SMALL = ("ffn1_norm", "mix_norm", "q_a_norm", "kv_a_norm", "pool_w", "pool_scale", "fox_b_f", "ffn2_norm", "final_norm")


def _pack_small(tree):
    rows, recipe = [], []
    for name in _SMALL:
        a = tree[name]
        flat = a.reshape(-1)
        n = flat.shape[0]
        nrow = -(-n // (8 * LANES)) * 8
        flat = jnp.pad(flat, (0, nrow * LANES - n))
        rows.append(flat.reshape(nrow, LANES))
        recipe.append((name, a.shape, n, nrow))
    return jnp.concatenate(rows, axis=0), recipe


def _unpack_small(packed, recipe):
    out, r0 = {}, 0
    for name, shape, n, nrow in recipe:
        out[name] = packed[r0:r0 + nrow].reshape(-1)[0:n].reshape(shape)
        r0 += nrow
    return out


def _adam_small(packs, w, m, v, name):
    R = w.shape[0]

    def body(p_ref, w_ref, m_ref, v_ref, g_out, d_out, m_out, v_out):
        g = p_ref[0]
        for p in range(1, N_DEV):
            g = g + p_ref[p]
        delta, mn, vn = _adam_math(g, w_ref[...], m_ref[...], v_ref[...])
        g_out[...] = g
        d_out[...] = delta
        m_out[...] = mn
        v_out[...] = vn

    blk = _bs((R, LANES), lambda i: (0, 0))
    shp = jax.ShapeDtypeStruct((R, LANES), F32)
    return pl.pallas_call(
        body, name=name, grid=(1,),
        in_specs=[_bs((N_DEV, R, LANES), lambda i: (0, 0, 0)), blk, blk, blk],
        out_specs=[blk, blk, blk, blk], out_shape=[shp, shp, shp, shp],
        compiler_params=_params(("arbitrary",)),
    )(packs, w, m, v)


_GATHER_PLAN = {
    ("first", 0): (("ffn1_w_gu", 0), ("ffn1_w_down", 0)),
    ("ffn1_fwd", 0): (("w_in", 0), ("w_small", 0), ("w_out", 0), ("ffn2_w_down", 0)),
    ("mla_attn_fwd", 0): (("ffn2_w_gu", 0), ("ffn1_w_gu", 1)),
    ("fox_attn_fwd", 0): (("ffn1_w_down", 1), ("w_in", 1), ("w_small", 1), ("w_out", 1)),
    ("mla_attn_fwd", 1): (("ffn2_w_gu", 1), ("ffn2_w_down", 1)),
}
_SCATTER_PLAN = {
    ("mla_attn_bwd", 1): (("ffn2_w_gu", 1), ("ffn2_w_down", 1)),
    ("fox_attn_bwd", 1): (("w_out", 1),),
    ("ffn1_bwd", 1): (("w_in", 1), ("w_small", 1)),
    ("ffn2_bwd", 0): (("ffn1_w_gu", 1),),
    ("mla_attn_bwd", 0): (("ffn1_w_down", 1), ("ffn2_w_gu", 0)),
    ("fox_attn_bwd", 0): (("ffn2_w_down", 0), ("w_out", 0)),
    ("ffn1_bwd_a", 0): (("w_in", 0), ("w_small", 0)),
    ("ffn1_dwgu", 0): (("ffn1_w_down", 0),),
    ("ffn1_bwd_b", 0): (("ffn1_w_gu", 0),),
}
_SPLIT_BWD = (("ffn1", 0),)


class _Exchange:
    def __init__(self, shards, D, n_pad, f_sh, r_in, r_out):
        self.shards = shards
        self.D, self.n_pad, self.f_sh, self.r_in, self.r_out = D, n_pad, f_sh, r_in, r_out
        self.nj = N_DEV // 2
        self.n_zero = n_pad - 2 * f_sh
        self.zero_src = jnp.zeros((self.n_zero, D), BF16) if self.n_zero else None
        self.weights, self.grads, self.recv = {}, {}, {}

    def _half(self, ref, p):
        f_sh = self.f_sh
        return ref.at[p // 2, pl.ds(pl.multiple_of((p % 2) * f_sh, 16), f_sh)]

    def _gathered_shape(self, kind):
        D, n_pad = self.D, self.n_pad
        return {"w_gu": (N_DEV, n_pad, D), "w_down": (self.nj, n_pad, D), "w_in": (N_DEV, self.r_in, N_PAD),
                "w_small": (N_DEV, MLA_Q_RANK + MLA_KV_RANK, LANES), "w_out": (N_DEV, self.r_out, D)}[kind]

    def _recv_shape(self, kind):
        D, n_pad = self.D, self.n_pad
        return {"w_gu": (N_DEV, n_pad, D), "w_down": (N_DEV, self.f_sh, D), "w_in": (N_DEV, self.r_in, N_IN),
                "w_small": (N_DEV, MLA_Q_RANK + MLA_KV_RANK, LANES), "w_out": (N_DEV, self.r_out, D)}[kind]

    @staticmethod
    def _kind(name):
        return name[5:] if name.startswith("ffn") else name

    def gather_rider(self, call, l):
        keys = _GATHER_PLAN.get((call, l))
        if not keys:
            return None
        by_dev = lambda ref, p: ref.at[p]
        shards, shapes, views, zero_views = [], [], [], []
        for a, key in enumerate(keys):
            kind = self._kind(key[0])
            shards.append(self.shards[key])
            shapes.append(jax.ShapeDtypeStruct(self._gathered_shape(kind), BF16))
            views.append(self._half if kind == "w_down" else by_dev)
            if kind == "w_down" and self.n_zero:
                zero_views += [(a, (lambda ref, j=j: ref.at[j, pl.ds(2 * self.f_sh, self.n_zero)])) for j in range(self.nj)]
        return _GatherRider(shards, shapes, views, self.zero_src, zero_views)

    def gathered(self, call, l, outs):
        for key, w in zip(_GATHER_PLAN.get((call, l), ()), outs):
            kind = self._kind(key[0])
            if kind == "w_gu":
                w = w.reshape(2, self.nj, self.n_pad, self.D)
            elif kind in ("w_in", "w_out"):
                w = w.reshape((N_DEV * w.shape[1],) + w.shape[2:])
            self.weights[key] = w

    def scatter_rider(self, call, l, pack=None):
        keys = _SCATTER_PLAN.get((call, l), ())
        if not keys and pack is None:
            return None
        by_dev = lambda ref, p: ref.at[p]
        srcs, shapes, src_of = [], [], []
        for key in keys:
            kind = self._kind(key[0])
            srcs.append(self.grads[key])
            shapes.append(jax.ShapeDtypeStruct(self._recv_shape(kind), BF16))
            src_of.append(self._half if kind == "w_down" else by_dev)
        if pack is not None:
            srcs.append(pack)
            shapes.append(jax.ShapeDtypeStruct((N_DEV,) + pack.shape, pack.dtype))
            src_of.append(lambda ref, p: ref)
        return _ScatterRider(srcs, shapes, src_of, [by_dev] * len(srcs))

    def scattered(self, call, l, outs):
        for key, r in zip(_SCATTER_PLAN.get((call, l), ()), outs):
            self.recv[key] = r


def _local_step(x, target, ex, small):
    S, D = x.shape
    tabs = _rope_tables(S)
    nhp_a, nhp_c = MLA_HEADS // 2, FOX_HEADS // 2
    fox_scale = 1.0 / math.sqrt(FOX_HEAD_DIM)
    nj = ex.nj
    ex.gathered("first", 0, _comm_call(ex.gather_rider("first", 0), "gather_first"))
    saved = []
    for l in range(DEPTH):
        s = {}
        s["x0"] = x
        wgu1, wd1 = ex.weights[("ffn1_w_gu", l)], ex.weights[("ffn1_w_down", l)]
        (x1, s["h1"], s["gu1"]), got = _ffn_fwd(x, small["ffn1_norm"][l][None], wgu1, wd1, f"ffn1_fwd_l{l}",
                                               rider=ex.gather_rider("ffn1_fwd", l))
        ex.gathered("ffn1_fwd", l, got)
        s["x1"] = x1
        w_in = ex.weights[("w_in", l)]
        s["h2"], za, zf, zkvt = _mix_in_fwd(x1, small["mix_norm"][l][None], w_in, f"mix_in_fwd_l{l}")
        s["za"], s["zf"], s["zkvt"] = za, zf, zkvt
        wq, wkv = _mla_weights(ex.weights[("w_small", l)])
        s["wq"], s["wkv"] = wq, wkv
        gq, gkv = small["q_a_norm"][l][None], small["kv_a_norm"][l][None]
        qf, kf, vm, kft, vmt = _mla_prep(za, gq, gkv, wq, wkv, tabs, f"mla_prep_l{l}")
        s["qf"], s["kf"], s["vm"], s["kft"] = qf, kf, vm, kft
        (ya, lse_a), got = _attn_fwd_t(qf, kf, vmt, nhp=nhp_a, dkb=LANES, qoff=0, koff=0, vtoff=0, scale=1.0,
                                       cum=None, cumT=None, name=f"mla_attn_fwd_l{l}",
                                       rider=ex.gather_rider("mla_attn_fwd", l))
        ex.gathered("mla_attn_fwd", l, got)
        s["ya"], s["lse_a"] = ya, lse_a
        b_row = _gate_row(small["fox_b_f"][l])
        s["b_row"] = b_row
        cum, cumT = _fox_prep(za, b_row, f"fox_prep_l{l}")
        s["cum"], s["cumT"] = cum, cumT
        (yc, lse_c), got = _attn_fwd_t(zf, zf, zkvt, nhp=nhp_c, dkb=64, qoff=0, koff=nhp_c, vtoff=nhp_c, scale=fox_scale,
                                       cum=cum, cumT=cumT, name=f"fox_attn_fwd_l{l}",
                                       rider=ex.gather_rider("fox_attn_fwd", l))
        ex.gathered("fox_attn_fwd", l, got)
        s["yc"], s["lse_c"] = yc, lse_c
        wbd = _pool_blockdiag(small["pool_w"][l]).astype(BF16)
        s["wbd"] = wbd
        psc = small["pool_scale"][l][None]
        yb, s["pd"] = _pool_fwd(za, wbd, psc, f"pool_fwd_l{l}")
        w_out = ex.weights[("w_out", l)]
        x2, s["ycat"] = _mix_out_fwd(x1, ya, yb, yc, w_out, f"mix_out_fwd_l{l}")
        s["x2"] = x2
        wgu2, wd2 = ex.weights[("ffn2_w_gu", l)], ex.weights[("ffn2_w_down", l)]
        (x, s["h3"], s["gu2"]), _ = _ffn_fwd(x2, small["ffn2_norm"][l][None], wgu2, wd2, f"ffn2_fwd_l{l}")
        saved.append(s)

    dx, d_final, loss = _loss_head(x, small["final_norm"][None], target, "loss_head")

    small_grads = [None] * DEPTH
    for l in reversed(range(DEPTH)):
        s = saved[l]
        g = {}
        wgu2, wd2 = ex.weights[("ffn2_w_gu", l)], ex.weights[("ffn2_w_down", l)]
        dy3 = dx
        (dx, g["ffn2_norm"], dgu, act, dyh), got = _ffn_bwd(dy3, s["x2"], small["ffn2_norm"][l][None], s["gu2"], wgu2, wd2,
                                                            f"ffn2_bwd_l{l}", rider=ex.scatter_rider("ffn2_bwd", l))
        ex.scattered("ffn2_bwd", l, got)
        ex.grads[("ffn2_w_gu", l)] = _mm_tn(dgu.reshape((2 * nj,) + dgu.shape[2:]), s["h3"][None], 2 * nj, lambda p: p,
                                            lambda p: 0, 768, 1024, f"ffn2_dwgu_l{l}", ts=4096)
        ex.grads[("ffn2_w_down", l)] = _mm_tn(act, dyh[None], nj, lambda p: p, lambda p: 0, 768, 1024, f"ffn2_dwd_l{l}",
                                              ts=4096)

        w_out = ex.weights[("w_out", l)]
        dya, dyb, dyc = _mix_out_bwd(dx, w_out, f"mix_out_bwd_l{l}")
        dw_out = _mm_tn(s["ycat"][None], dx[None], 1, lambda p: 0, lambda p: 0, 1024, 1024, f"dwout_l{l}", ts=1024)[0]
        ex.grads[("w_out", l)] = dw_out.reshape(N_DEV, ex.r_out, D)

        dvec_a = _attn_dvec(s["ya"], dya, nhp_a, f"mla_dvec_l{l}")
        (dqf, dkf, dvm), got = _attn_bwd_t(s["qf"], s["kf"], s["kft"], s["vm"], dya, s["lse_a"], dvec_a, nhp=nhp_a,
                                           dkb=LANES, qoff=0, koff=0, ktoff=0, voff=0, scale=1.0, cum=None, cumT=None,
                                           name=f"mla_attn_bwd_l{l}", rider=ex.scatter_rider("mla_attn_bwd", l))
        ex.scattered("mla_attn_bwd", l, got)
        zf = s["zf"]
        dvec_c = _attn_dvec(s["yc"], dyc, nhp_c, f"fox_dvec_l{l}")
        (dqc, dkc, dvc, dcq, dck), got = _attn_bwd_t(zf, zf, s["zkvt"], zf, dyc, s["lse_c"], dvec_c, nhp=nhp_c, dkb=64,
                                                     qoff=0, koff=nhp_c, ktoff=0, voff=2 * nhp_c, scale=fox_scale,
                                                     cum=s["cum"], cumT=s["cumT"], name=f"fox_attn_bwd_l{l}",
                                                     rider=ex.scatter_rider("fox_attn_bwd", l))
        ex.scattered("fox_attn_bwd", l, got)
        dtail_f, db = _fox_prep_bwd(s["za"], s["b_row"], dcq, dck, f"fox_prep_bwd_l{l}")
        g["fox_b_f"] = db[0, _F_LANES]
        psc = small["pool_scale"][l][None]
        du, dwbd, dpsc = _pool_bwd(dyb, s["pd"], s["wbd"], psc, f"pool_bwd_l{l}")
        g["pool_w"] = _pool_blockdiag_t(dwbd)
        g["pool_scale"] = dpsc[0]
        gq, gkv = small["q_a_norm"][l][None], small["kv_a_norm"][l][None]
        dza, dwq, dwkv, dgq, dgkv = _mla_prep_bwd(s["za"], gq, gkv, s["wq"], s["wkv"], tabs, dqf, dkf, dvm,
                                                   f"mla_prep_bwd_l{l}")
        g["q_a_norm"], g["kv_a_norm"] = dgq[0], dgkv[0]
        ex.grads[("w_small", l)] = _mla_grads_to_blocks(dwq, dwkv)
        w_in = ex.weights[("w_in", l)]
        dx, g["mix_norm"], dz = _mix_in_bwd(dx, s["x1"], small["mix_norm"][l][None], dza, du, dtail_f, dqc, dkc, dvc,
                                            w_in, f"mix_in_bwd_l{l}")
        dw_in = _unpad_w_in(_mm_tn(s["h2"][None], dz[None], 1, lambda p: 0, lambda p: 0, 1024, 640, f"dwin_l{l}",
                                   ts=4096)[0])
        ex.grads[("w_in", l)] = dw_in.reshape(N_DEV, ex.r_in, N_IN)

        wgu1, wd1 = ex.weights[("ffn1_w_gu", l)], ex.weights[("ffn1_w_down", l)]
        dy1 = dx
        gam1 = small["ffn1_norm"][l][None]
        split = ("ffn1", l) in _SPLIT_BWD
        if split:
            (dgu, act, dyh), got = _ffn_bwd_act(dy1, s["gu1"], wd1, f"ffn1_bwd_a_l{l}",
                                                rider=ex.scatter_rider("ffn1_bwd_a", l))
            ex.scattered("ffn1_bwd_a", l, got)
        else:
            (dx, g["ffn1_norm"], dgu, act, dyh), got = _ffn_bwd(dy1, s["x0"], gam1, s["gu1"], wgu1, wd1, f"ffn1_bwd_l{l}",
                                                                rider=ex.scatter_rider("ffn1_bwd", l))
            ex.scattered("ffn1_bwd", l, got)
        ex.grads[("ffn1_w_down", l)] = _mm_tn(act, dyh[None], nj, lambda p: p, lambda p: 0, 768, 1024, f"ffn1_dwd_l{l}",
                                              ts=4096)
        rider = ex.scatter_rider("ffn1_dwgu", l)
        dwgu = _mm_tn(dgu.reshape((2 * nj,) + dgu.shape[2:]), s["h1"][None], 2 * nj, lambda p: p, lambda p: 0,
                      768, 1024, f"ffn1_dwgu_l{l}", rider=rider, ts=4096)
        if rider is not None:
            dwgu, got = dwgu
            ex.scattered("ffn1_dwgu", l, got)
        ex.grads[("ffn1_w_gu", l)] = dwgu
        if split:
            (dx, g["ffn1_norm"]), got = _ffn_bwd_in(dy1, s["x0"], gam1, dgu, wgu1, f"ffn1_bwd_b_l{l}",
                                                   rider=ex.scatter_rider("ffn1_bwd_b", l))
            ex.scattered("ffn1_bwd_b", l, got)
        for k in ("ffn1_norm", "ffn2_norm", "mix_norm"):
            g[k] = g[k][0]
        small_grads[l] = g
    return loss, dx, small_grads, d_final[0]


_BIG = ("ffn1_w_gu", "ffn1_w_down", "w_in", "w_small", "w_out", "ffn2_w_gu", "ffn2_w_down")


def _pad_cols(w, n):
    return jnp.pad(w, ((0, 0),) * (w.ndim - 1) + ((0, n - w.shape[-1]),))


def kernel(x, ffn1_norm, ffn1_w_gu, ffn1_w_down, mix_norm, w_in, q_a_norm, w_q_b, kv_a_norm, w_kv_b, pool_w, pool_scale, fox_b_f, w_out, ffn2_norm, ffn2_w_gu, ffn2_w_down, final_norm, loss_target, m_ffn1_norm, m_ffn1_w_gu, m_ffn1_w_down, m_mix_norm, m_w_in, m_q_a_norm, m_w_q_b, m_kv_a_norm, m_w_kv_b, m_pool_w, m_pool_scale, m_fox_b_f, m_w_out, m_ffn2_norm, m_ffn2_w_gu, m_ffn2_w_down, m_final_norm, v_ffn1_norm, v_ffn1_w_gu, v_ffn1_w_down, v_mix_norm, v_w_in, v_q_a_norm, v_w_q_b, v_kv_a_norm, v_w_kv_b, v_pool_w, v_pool_scale, v_fox_b_f, v_w_out, v_ffn2_norm, v_ffn2_w_gu, v_ffn2_w_down, v_final_norm):
    W = dict(ffn1_norm=ffn1_norm, ffn1_w_gu=ffn1_w_gu, ffn1_w_down=ffn1_w_down, mix_norm=mix_norm, w_in=w_in,
             q_a_norm=q_a_norm, w_q_b=w_q_b, kv_a_norm=kv_a_norm, w_kv_b=w_kv_b, pool_w=pool_w, pool_scale=pool_scale,
             fox_b_f=fox_b_f, w_out=w_out, ffn2_norm=ffn2_norm, ffn2_w_gu=ffn2_w_gu, ffn2_w_down=ffn2_w_down,
             final_norm=final_norm)
    M = dict(ffn1_norm=m_ffn1_norm, ffn1_w_gu=m_ffn1_w_gu, ffn1_w_down=m_ffn1_w_down, mix_norm=m_mix_norm, w_in=m_w_in,
             q_a_norm=m_q_a_norm, w_q_b=m_w_q_b, kv_a_norm=m_kv_a_norm, w_kv_b=m_w_kv_b, pool_w=m_pool_w,
             pool_scale=m_pool_scale, fox_b_f=m_fox_b_f, w_out=m_w_out, ffn2_norm=m_ffn2_norm, ffn2_w_gu=m_ffn2_w_gu,
             ffn2_w_down=m_ffn2_w_down, final_norm=m_final_norm)
    V = dict(ffn1_norm=v_ffn1_norm, ffn1_w_gu=v_ffn1_w_gu, ffn1_w_down=v_ffn1_w_down, mix_norm=v_mix_norm, w_in=v_w_in,
             q_a_norm=v_q_a_norm, w_q_b=v_w_q_b, kv_a_norm=v_kv_a_norm, w_kv_b=v_w_kv_b, pool_w=v_pool_w,
             pool_scale=v_pool_scale, fox_b_f=v_fox_b_f, w_out=v_w_out, ffn2_norm=v_ffn2_norm, ffn2_w_gu=v_ffn2_w_gu,
             ffn2_w_down=v_ffn2_w_down, final_norm=v_final_norm)
    L, D, n_sh = ffn1_w_gu.shape
    f_sh = ffn1_w_down.shape[1]
    assert n_sh == 2 * f_sh and L == DEPTH
    n_pad = -(-n_sh // LANES) * LANES
    r_in, r_out = w_in.shape[1], w_out.shape[1]

    tr_in = lambda a: a.transpose(0, 2, 1)
    pad_rows = lambda a: jnp.pad(a, ((0, 0), (0, n_pad - n_sh), (0, 0)))
    big_shards = dict(
        ffn1_w_gu=pad_rows(tr_in(ffn1_w_gu)).astype(BF16), ffn1_w_down=ffn1_w_down.astype(BF16),
        w_in=_pad_w_in(w_in).astype(BF16), w_small=_small_pack(w_q_b, w_kv_b).astype(BF16), w_out=w_out.astype(BF16),
        ffn2_w_gu=pad_rows(tr_in(ffn2_w_gu)).astype(BF16), ffn2_w_down=ffn2_w_down.astype(BF16))
    ex = _Exchange({(k, l): big_shards[k][l] for k in _BIG for l in range(L)}, D, n_pad, f_sh, r_in, r_out)

    small = {k: W[k] for k in _SMALL}
    loss, dx, grads, d_final = _local_step(x[0], loss_target[0], ex, small)

    small_g = {k: jnp.stack([grads[l][k] for l in range(L)]) for k in _SMALL if k != "final_norm"}
    small_g["final_norm"] = d_final
    pack_g, recipe = _pack_small(small_g)
    n_small = pack_g.shape[0]
    loss_row = -(-n_small // 8) * 8
    pack_g = jnp.concatenate([pack_g, jnp.zeros((loss_row - n_small, LANES), F32), jnp.broadcast_to(loss, (8, LANES))],
                             axis=0)
    *got, packs = _comm_call(ex.scatter_rider("last", 0, pack=pack_g), "scatter_last")
    ex.scattered("last", 0, got)

    out = {}
    sm_w, sm_m, sm_v = (_small_pack(t["w_q_b"], t["w_kv_b"]) for t in (W, M, V))
    big = [("ffn1_w_gu", tr_in(W["ffn1_w_gu"]), tr_in(M["ffn1_w_gu"]), tr_in(V["ffn1_w_gu"]), f_sh),
           ("ffn1_w_down", W["ffn1_w_down"], M["ffn1_w_down"], V["ffn1_w_down"], 352),
           ("w_in", W["w_in"], M["w_in"], V["w_in"], 128),
           ("w_small", sm_w, sm_m, sm_v, 384),
           ("w_out", W["w_out"], M["w_out"], V["w_out"], 128),
           ("ffn2_w_gu", tr_in(W["ffn2_w_gu"]), tr_in(M["ffn2_w_gu"]), tr_in(V["ffn2_w_gu"]), f_sh),
           ("ffn2_w_down", W["ffn2_w_down"], M["ffn2_w_down"], V["ffn2_w_down"], 352)]
    for k, w_, m_, v_, tr in big:
        res = None
        for l in range(L):
            res = _adam_sum(ex.recv[(k, l)], w_, m_, v_, l, res, tr, f"adam_{k}_l{l}")
        if k == "w_small":
            cq, ckv = w_q_b.shape[2], w_kv_b.shape[2]
            parts = [_small_unpack(r, cq, ckv) for r in res]
            out["w_q_b"] = [p[0] for p in parts]
            out["w_kv_b"] = [p[1] for p in parts]
        elif k.endswith("w_gu"):
            out[k] = [tr_in(r) for r in res]
        else:
            out[k] = res

    pw, _ = _pack_small({k: W[k] for k in _SMALL})
    pm, _ = _pack_small({k: M[k] for k in _SMALL})
    pv, _ = _pack_small({k: V[k] for k in _SMALL})
    extra = ((0, loss_row + 8 - n_small), (0, 0))
    res = _adam_small(packs, jnp.pad(pw, extra), jnp.pad(pm, extra), jnp.pad(pv, extra), "adam_small")
    loss_total = res[0][loss_row, 0]
    small_out = [_unpack_small(r, recipe) for r in res]
    for k in _SMALL:
        out[k] = [t[k] for t in small_out]

    names = ["ffn1_norm", "ffn1_w_gu", "ffn1_w_down", "mix_norm", "w_in", "q_a_norm", "w_q_b", "kv_a_norm", "w_kv_b",
             "pool_w", "pool_scale", "fox_b_f", "w_out", "ffn2_norm", "ffn2_w_gu", "ffn2_w_down", "final_norm"]
    outs = [loss_total, dx[None]]
    for which in range(4):
        outs += [out[k][which] for k in names]
    return tuple(outs)
```

```python
import functools
import math

import numpy as np
import jax
import jax.numpy as jnp
from jax import lax
from jax.experimental import pallas as pl
from jax.experimental.pallas import tpu as pltpu

F32 = jnp.float32
BF16 = jnp.bfloat16
MESH_ID = pl.DeviceIdType.MESH

N_DEV = 8
EPS = 1e-6
DEPTH = 2

MLA_HEADS = 6
MLA_Q_RANK = 256
MLA_KV_RANK = 128
MLA_NOPE = 64
MLA_ROPE = 32
MLA_V = 64
ROPE_THETA = 10000.0
POOL_WINDOWS = (2, 4, 8, 16)
POOL_GROUP = 64
POOL_WIDTH = 256
FOX_HEADS = 6
FOX_HEAD_DIM = 64
N_IN = 1830

ADAM_LR = 0.001
ADAM_B1 = 0.9
ADAM_B2 = 0.999
ADAM_EPS = 1e-08
ADAM_WD = 0.01
ADAM_STEP = 10

LANES = 128
HEAD_BLOCK = 128
VMEM_LIMIT = 48 * 1024 * 1024
VMEM_LIMIT_BIG = 50 * 1024 * 1024
NEG = -1e30

ZA = 768
ZF = 1152
N_PAD = ZA + ZF
TAIL0 = 640
ROPE_LANE0 = 64


def _f_lane(h):
    return 8 * (h // 2) + (h % 2)


def _in_perm():
    perm = -np.ones(N_PAD, np.int32)
    perm[0:256] = np.arange(0, 256)
    perm[256:384] = np.arange(256, 384)
    perm[384:640] = np.arange(416, 672)
    for h in range(FOX_HEADS):
        perm[TAIL0 + _f_lane(h)] = 1824 + h
    perm[TAIL0 + ROPE_LANE0:TAIL0 + ROPE_LANE0 + MLA_ROPE] = np.arange(384, 416)
    perm[ZA:N_PAD] = np.arange(672, 1824)
    inv = np.zeros(N_IN, np.int32)
    for new, old in enumerate(perm):
        if old >= 0:
            inv[old] = new
    return perm, inv


_IN_PERM, _IN_INV = _in_perm()
_F_LANES = np.array([_f_lane(h) for h in range(FOX_HEADS)], np.int32)


def _dot(a, b):
    return jnp.dot(a, b, preferred_element_type=F32)


def _dot_nt(a, b):
    return lax.dot_general(a, b, (((1,), (1,)), ((), ())), preferred_element_type=F32)


def _dot_tn(a, b):
    return lax.dot_general(a, b, (((0,), (0,)), ((), ())), preferred_element_type=F32)


def _rms(x, gam):
    r = lax.rsqrt(jnp.mean(x * x, axis=-1, keepdims=True) + EPS)
    return x * r * gam


def _rms_bwd(dy, x, gam):
    r = lax.rsqrt(jnp.mean(x * x, axis=-1, keepdims=True) + EPS)
    xh = x * r
    dxh = dy * gam
    dx = r * (dxh - xh * jnp.mean(dxh * xh, axis=-1, keepdims=True))
    return dx, jnp.sum(dy * xh, axis=0, keepdims=True)


def _accum_out(ref, first, val):
    @pl.when(first)
    def _():
        ref[...] = val

    @pl.when(jnp.logical_not(first))
    def _():
        ref[...] += val


def _bs(shape, fn):
    return pl.BlockSpec(shape, fn)


def _params(dims, vmem=VMEM_LIMIT):
    return pltpu.CompilerParams(dimension_semantics=dims, vmem_limit_bytes=vmem)


def _tile(n, t):
    t = min(n, t)
    assert n % t == 0, (n, t)
    return t


HBM_SPEC = pl.BlockSpec(memory_space=pl.ANY)


def _call(body, *, name, grid, in_specs, out_specs, out_shape, scratch, dims, args, rider=None, vmem=VMEM_LIMIT):
    n_in, n_out = len(in_specs), len(out_specs)
    if rider is None:
        outs = pl.pallas_call(body, name=name, grid=grid, in_specs=in_specs, out_specs=out_specs, out_shape=out_shape,
                              scratch_shapes=scratch, compiler_params=_params(dims, vmem))(*args)
        return list(outs), []
    k_in, k_out, k_sem = len(rider.srcs), len(rider.out_shapes), len(rider.scratch)

    def riding(*refs):
        a, b, c, d = n_in, n_in + k_in, n_in + k_in + n_out, n_in + k_in + n_out + k_out
        rest = refs[d:]
        sems = rest[len(rest) - k_sem:]
        step = 0
        for i, g in enumerate(grid):
            step = step * g + pl.program_id(i)
        n_steps = math.prod(grid)

        @pl.when(step == 0)
        def _():
            rider.begin(refs[a:b], refs[c:d], sems)

        body(*refs[:a], *refs[b:c], *rest[:len(rest) - k_sem])

        @pl.when(step == (3 * n_steps) // 4)
        def _():
            rider.middle(refs[a:b], refs[c:d], sems)

        @pl.when(step == n_steps - 1)
        def _():
            rider.end(refs[a:b], refs[c:d], sems)

    outs = pl.pallas_call(
        riding, name=name, grid=grid, in_specs=list(in_specs) + [HBM_SPEC] * k_in,
        out_specs=list(out_specs) + [HBM_SPEC] * k_out, out_shape=list(out_shape) + list(rider.out_shapes),
        scratch_shapes=list(scratch) + list(rider.scratch),
        compiler_params=_params(("arbitrary",) * len(grid), vmem))(*args, *rider.srcs)
    return list(outs[:n_out]), list(outs[n_out:])


_ONCE = pl.Buffered(1)
_FF_CHUNKS = ((0, 1536), (1536, 2816))


def _ffn_fwd_full(x, gam, wgut, wd, tm, name, rider=None):
    S, D = x.shape
    F = wd.shape[0]
    tm = _tile(S, tm)
    chunks = _FF_CHUNKS if F == 2816 else ((0, F),)

    def body(x_ref, gam_ref, wgut_ref, wd_ref, xo_ref, h_ref, gu_ref):
        h = _rms(x_ref[...], gam_ref[...]).astype(BF16)
        h_ref[...] = h
        y = None
        for c0, c1 in chunks:
            g = _dot_nt(h, wgut_ref[c0:c1, :])
            u = _dot_nt(h, wgut_ref[F + c0:F + c1, :])
            gu_ref[:, c0:c1] = g.astype(BF16)
            gu_ref[:, F + c0:F + c1] = u.astype(BF16)
            a = (g * jax.nn.sigmoid(g) * u).astype(BF16)
            part = _dot(a, wd_ref[c0:c1, :])
            y = part if y is None else y + part
        xo_ref[...] = x_ref[...] + 0.5 * y

    row = lambda i: (i, 0)
    fix = lambda i: (0, 0)
    return _call(
        body, name=name, grid=(S // tm,),
        in_specs=[_bs((tm, D), row), _bs((1, D), fix), pl.BlockSpec((2 * F, D), fix, pipeline_mode=_ONCE),
                  pl.BlockSpec((F, D), fix, pipeline_mode=_ONCE)],
        out_specs=[_bs((tm, D), row), _bs((tm, D), row), _bs((tm, 2 * F), row)],
        out_shape=[jax.ShapeDtypeStruct((S, D), F32), jax.ShapeDtypeStruct((S, D), BF16),
                   jax.ShapeDtypeStruct((S, 2 * F), BF16)],
        scratch=[], dims=("parallel",), args=(x, gam, wgut, wd), rider=rider)


def _ffn_bwd_full(dy, x, gam, gu, wgut, wd, tm, name, phase="all", dgu_in=None, rider=None):
    S, D = dy.shape
    F = wd.shape[0] if wd is not None else wgut.shape[0] // 2
    tm = _tile(S, tm)
    chunks = _FF_CHUNKS if F == 2816 else ((0, F),)
    act, inp = phase in ("all", "act"), phase in ("all", "in")

    def body(*refs):
        refs = list(refs)
        dy_ref = refs.pop(0)
        x_ref, gam_ref = (refs.pop(0), refs.pop(0)) if inp else (None, None)
        gu_ref = refs.pop(0)
        wgut_ref = refs.pop(0) if inp else None
        wd_ref = refs.pop(0) if act else None
        if inp:
            dx_ref, dgam_ref = refs.pop(0), refs.pop(0)
        if act:
            dgu_ref, a_ref, dyh_ref = refs.pop(0), refs.pop(0), refs.pop(0)
            dyh = (0.5 * dy_ref[...]).astype(BF16)
            dyh_ref[...] = dyh
        dh = None
        for c0, c1 in chunks:
            if act:
                dg, du, a = _swiglu_bwd(_dot_nt(dyh, wd_ref[c0:c1, :]), gu_ref[:, c0:c1], gu_ref[:, F + c0:F + c1])
                dgu_ref[:, c0:c1] = dg
                dgu_ref[:, F + c0:F + c1] = du
                a_ref[:, c0:c1] = a
            else:
                dg, du = gu_ref[:, c0:c1], gu_ref[:, F + c0:F + c1]
            if inp:
                part = _dot(dg, wgut_ref[c0:c1, :]) + _dot(du, wgut_ref[F + c0:F + c1, :])
                dh = part if dh is None else dh + part
        if inp:
            dxn, dgam = _rms_bwd(dh, x_ref[...], gam_ref[...])
            dx_ref[...] = dy_ref[...] + dxn
            _accum_out(dgam_ref, pl.program_id(0) == 0, dgam)

    row = lambda i: (i, 0)
    fix = lambda i: (0, 0)
    in_specs, args = [_bs((tm, D), row)], [dy]
    if inp:
        in_specs += [_bs((tm, D), row), _bs((1, D), fix)]
        args += [x, gam]
    in_specs += [_bs((tm, 2 * F), row)]
    args += [gu if act else dgu_in]
    if inp:
        in_specs += [pl.BlockSpec((2 * F, D), fix, pipeline_mode=_ONCE)]
        args += [wgut]
    if act:
        in_specs += [pl.BlockSpec((F, D), fix, pipeline_mode=_ONCE)]
        args += [wd]
    out_specs, out_shape = [], []
    if inp:
        out_specs += [_bs((tm, D), row), _bs((1, D), fix)]
        out_shape += [jax.ShapeDtypeStruct((S, D), F32), jax.ShapeDtypeStruct((1, D), F32)]
    if act:
        out_specs += [_bs((tm, 2 * F), row), _bs((tm, F), row), _bs((tm, D), row)]
        out_shape += [jax.ShapeDtypeStruct((S, 2 * F), BF16), jax.ShapeDtypeStruct((S, F), BF16),
                      jax.ShapeDtypeStruct((S, D), BF16)]
    return _call(body, name=name, grid=(S // tm,), in_specs=in_specs, out_specs=out_specs, out_shape=out_shape,
                 scratch=[], dims=("arbitrary",), args=tuple(args), rider=rider, vmem=VMEM_LIMIT_BIG)


def _swiglu_bwd(da, g, u):
    g = g.astype(F32)
    u = u.astype(F32)
    sig = jax.nn.sigmoid(g)
    sl = g * sig
    dg = (da * u * (sig * (1.0 + g * (1.0 - sig)))).astype(BF16)
    return dg, (da * sl).astype(BF16), (sl * u).astype(BF16)


def _mm_tn(a, b, nb, a_of, b_of, tm, tn, name, rider=None, ts=512):
    _, S, M = a.shape
    N = b.shape[2]
    tm = _tile(M, tm)
    tn = _tile(N, tn)
    ts = _tile(S, ts)
    nk = S // ts

    def body(a_ref, b_ref, o_ref, acc):
        k = pl.program_id(3)

        @pl.when(k == 0)
        def _():
            acc[...] = jnp.zeros_like(acc)

        acc[...] += _dot_tn(a_ref[...].astype(BF16), b_ref[...].astype(BF16))

        @pl.when(k == nk - 1)
        def _():
            o_ref[...] = acc[...].astype(o_ref.dtype)

    (out,), extra = _call(
        body, name=name, grid=(nb, M // tm, N // tn, nk),
        in_specs=[_bs((None, ts, tm), lambda p, i, j, k: (a_of(p), k, i)),
                  _bs((None, ts, tn), lambda p, i, j, k: (b_of(p), k, j))],
        out_specs=[_bs((None, tm, tn), lambda p, i, j, k: (p, i, j))],
        out_shape=[jax.ShapeDtypeStruct((nb, M, N), BF16)],
        scratch=[pltpu.VMEM((tm, tn), F32)],
        dims=("parallel", "parallel", "parallel", "arbitrary"), args=(a, b), rider=rider)
    return (out, extra) if rider is not None else out


FFN_FWD_TOKENS = 512
FFN_BWD_TOKENS = 256
FFN_GRAD_ROWS = 1408


def _ffn_weight_grad(a, b, name, rider=None):
    M = a.shape[1]
    tm = max(t for t in range(LANES, FFN_GRAD_ROWS + 1, LANES) if M % t == 0)
    res = _mm_tn(a[None], b[None], 1, lambda p: 0, lambda p: 0, tm, b.shape[1], name, rider=rider, ts=2048)
    return (res[0][0], res[1]) if rider is not None else res[0]


def _mix_in_fwd(x, gam, w_in, name):
    S, D = x.shape
    tm = _tile(S, 512)
    nkv = (ZF - 384) // LANES

    def body(x_ref, gam_ref, w_ref, h_ref, za_ref, zf_ref, zt_ref):
        hb = _rms(x_ref[...], gam_ref[...]).astype(BF16)
        h_ref[...] = hb
        za_ref[...] = _dot(hb, w_ref[:, 0:ZA])
        zf = _dot(hb, w_ref[:, ZA:N_PAD])
        zf_ref[...] = zf.astype(BF16)
        for c in range(nkv):
            zt_ref[c * LANES:(c + 1) * LANES, :] = zf[:, 384 + c * LANES:384 + (c + 1) * LANES].T.astype(BF16)

    return pl.pallas_call(
        body, name=name, grid=(S // tm,),
        in_specs=[_bs((tm, D), lambda i: (i, 0)), _bs((1, D), lambda i: (0, 0)), _bs((D, N_PAD), lambda i: (0, 0))],
        out_specs=[_bs((tm, D), lambda i: (i, 0)), _bs((tm, ZA), lambda i: (i, 0)), _bs((tm, ZF), lambda i: (i, 0)),
                   _bs((nkv * LANES, tm), lambda i: (0, i))],
        out_shape=[jax.ShapeDtypeStruct((S, D), BF16), jax.ShapeDtypeStruct((S, ZA), F32),
                   jax.ShapeDtypeStruct((S, ZF), BF16), jax.ShapeDtypeStruct((nkv * LANES, S), BF16)],
        compiler_params=_params(("parallel",)),
    )(x, gam, w_in)


def _mix_in_bwd(dy, x, gam, dza_mla, du, dtail_f, dqf, dkf, dvf, w_in, name):
    S, D = x.shape
    tm = _tile(S, 512)

    def body(dy_ref, x_ref, gam_ref, dza_ref, du_ref, dt_ref, dq_ref, dk_ref, dv_ref, w_ref, dx_ref, dgam_ref, dz_ref):
        i = pl.program_id(0)
        dza = dza_ref[...]
        dz = jnp.concatenate([dza[:, 0:384], du_ref[...], dza[:, TAIL0:ZA] + dt_ref[...],
                              dq_ref[...], dk_ref[...], dv_ref[...]], axis=1).astype(BF16)
        dz_ref[...] = dz
        dh = _dot_nt(dz, w_ref[...])
        dxn, dgam = _rms_bwd(dh, x_ref[...], gam_ref[...])
        dx_ref[...] = dy_ref[...] + dxn
        _accum_out(dgam_ref, i == 0, dgam)

    row = lambda i: (i, 0)
    fix = lambda i: (0, 0)
    return pl.pallas_call(
        body, name=name, grid=(S // tm,),
        in_specs=[_bs((tm, D), row), _bs((tm, D), row), _bs((1, D), fix), _bs((tm, ZA), row), _bs((tm, 256), row),
                  _bs((tm, 128), row), _bs((tm, 384), row), _bs((tm, 384), row), _bs((tm, 384), row),
                  _bs((D, N_PAD), fix)],
        out_specs=[_bs((tm, D), row), _bs((1, D), fix), _bs((tm, N_PAD), row)],
        out_shape=[jax.ShapeDtypeStruct((S, D), F32), jax.ShapeDtypeStruct((1, D), F32),
                   jax.ShapeDtypeStruct((S, N_PAD), BF16)],
        compiler_params=_params(("arbitrary",)),
    )(dy, x, gam, dza_mla, du, dtail_f, dqf, dkf, dvf, w_in)


def _rope_tables(S):
    half = MLA_ROPE // 2
    inv_freq = ROPE_THETA ** (-jnp.arange(0, MLA_ROPE, 2, dtype=F32) / MLA_ROPE)
    ang = jnp.arange(S, dtype=jnp.int32).astype(F32)[:, None] * inv_freq[None, :]
    cos, sin = jnp.cos(ang), jnp.sin(ang)
    one = jnp.ones((S, ROPE_LANE0), F32)
    zero = jnp.zeros((S, ROPE_LANE0), F32)
    pad1 = jnp.ones((S, LANES - ROPE_LANE0 - MLA_ROPE), F32)
    pad0 = jnp.zeros((S, LANES - ROPE_LANE0 - MLA_ROPE), F32)
    zh = jnp.zeros((S, half), F32)
    tab_c = jnp.concatenate([one, cos, cos, pad1], axis=1)
    tab_ck = jnp.concatenate([zero, cos, cos, pad0], axis=1)
    tab_s1 = jnp.concatenate([zero, -sin, zh, pad0], axis=1)
    tab_s2 = jnp.concatenate([zero, zh, sin, pad0], axis=1)
    return tab_c, tab_ck, tab_s1, tab_s2


def _rope(x, c, s1, s2):
    return x * c + pltpu.roll(x, LANES - 16, 1) * s1 + pltpu.roll(x, 16, 1) * s2


def _rope_t(dy, c, s1, s2):
    return dy * c + pltpu.roll(dy * s1, 16, 1) + pltpu.roll(dy * s2, LANES - 16, 1)


_MLA_SCALE = 1.0 / math.sqrt(MLA_NOPE + MLA_ROPE)


def _mla_prep(za, gq, gkv, wq, wkv, tabs, name):
    S = za.shape[0]
    tm = _tile(S, 512)
    H = MLA_HEADS

    def body(zq_ref, tail_ref, gq_ref, gkv_ref, wq_ref, wkv_ref, c_ref, ck_ref, s1_ref, s2_ref,
             qf_ref, kf_ref, v_ref, kft_ref, vt_ref):
        zq = zq_ref[...]
        c, s1, s2 = c_ref[...], s1_ref[...], s2_ref[...]
        qn = _rms(zq[:, 0:256], gq_ref[...]).astype(BF16)
        q = _dot(qn, wq_ref[...])
        for h in range(H):
            blk = _rope(q[:, h * LANES:(h + 1) * LANES], c, s1, s2)
            qf_ref[:, h * LANES:(h + 1) * LANES] = (blk * _MLA_SCALE).astype(BF16)
        kvn = _rms(zq[:, 256:384], gkv_ref[...]).astype(BF16)
        kv = _dot(kvn, wkv_ref[...])
        kt = _rope(tail_ref[...], ck_ref[...], s1, s2)
        for h in range(H):
            sl = slice(h * LANES, (h + 1) * LANES)
            kblk = kv[:, sl] + kt
            kf_ref[:, sl] = kblk.astype(BF16)
            kft_ref[sl, :] = kblk.T.astype(BF16)
        v_ref[...] = kv[:, H * LANES:].astype(BF16)
        for cblk in range(H * MLA_V // LANES):
            sl = slice(cblk * LANES, (cblk + 1) * LANES)
            vt_ref[sl, :] = kv[:, H * LANES + cblk * LANES:H * LANES + (cblk + 1) * LANES].T.astype(BF16)

    row = lambda i: (i, 0)
    col = lambda i: (0, i)
    fix = lambda i: (0, 0)
    return pl.pallas_call(
        body, name=name, grid=(S // tm,),
        in_specs=[_bs((tm, 384), row), _bs((tm, 128), lambda i: (i, TAIL0 // 128)), _bs((1, 256), fix), _bs((1, 128), fix),
                  _bs((256, 768), fix), _bs((128, 1152), fix),
                  _bs((tm, 128), row), _bs((tm, 128), row), _bs((tm, 128), row), _bs((tm, 128), row)],
        out_specs=[_bs((tm, 768), row), _bs((tm, 768), row), _bs((tm, 384), row), _bs((768, tm), col), _bs((384, tm), col)],
        out_shape=[jax.ShapeDtypeStruct((S, 768), BF16), jax.ShapeDtypeStruct((S, 768), BF16),
                   jax.ShapeDtypeStruct((S, 384), BF16), jax.ShapeDtypeStruct((768, S), BF16),
                   jax.ShapeDtypeStruct((384, S), BF16)],
        compiler_params=_params(("parallel",)),
    )(za, za, gq, gkv, wq, wkv, *tabs)


def _mla_prep_bwd(za, gq, gkv, wq, wkv, tabs, dqf, dkf, dvm, name):
    S = za.shape[0]
    tm = _tile(S, 512)
    H = MLA_HEADS

    def body(zq_ref, gq_ref, gkv_ref, wq_ref, wkv_ref, c_ref, ck_ref, s1_ref, s2_ref, dqf_ref, dkf_ref, dvm_ref,
             dza_ref, dwq_ref, dwkv_ref, dgq_ref, dgkv_ref):
        i = pl.program_id(0)
        first = i == 0
        zq = zq_ref[...]
        c, s1, s2 = c_ref[...], s1_ref[...], s2_ref[...]
        lane = lax.broadcasted_iota(jnp.int32, (1, LANES), 1)
        nope = lane < MLA_NOPE
        rope = jnp.logical_and(lane >= ROPE_LANE0, lane < ROPE_LANE0 + MLA_ROPE)

        qa = zq[:, 0:256]
        qn = _rms(qa, gq_ref[...]).astype(BF16)
        dqf = dqf_ref[...]
        dq_pre = jnp.concatenate(
            [_rope_t(dqf[:, h * LANES:(h + 1) * LANES] * _MLA_SCALE, c, s1, s2) for h in range(H)], axis=1).astype(BF16)
        _accum_out(dwq_ref, first, _dot_tn(qn, dq_pre))
        dqa, dgq = _rms_bwd(_dot_nt(dq_pre, wq_ref[...]), qa, gq_ref[...])
        _accum_out(dgq_ref, first, dgq)

        kva = zq[:, 256:384]
        kvn = _rms(kva, gkv_ref[...]).astype(BF16)
        dkf = dkf_ref[...]
        parts = []
        dkt = jnp.zeros((tm, LANES), F32)
        for h in range(H):
            blk = dkf[:, h * LANES:(h + 1) * LANES]
            parts.append(jnp.where(nope, blk, 0.0))
            dkt = dkt + jnp.where(rope, blk, 0.0)
        dkv_pre = jnp.concatenate(parts + [dvm_ref[...]], axis=1).astype(BF16)
        _accum_out(dwkv_ref, first, _dot_tn(kvn, dkv_pre))
        dkva, dgkv = _rms_bwd(_dot_nt(dkv_pre, wkv_ref[...]), kva, gkv_ref[...])
        _accum_out(dgkv_ref, first, dgkv)

        dtail = _rope_t(dkt, ck_ref[...], s1, s2)
        dza_ref[...] = jnp.concatenate([dqa, dkva, jnp.zeros((tm, 256), F32), dtail], axis=1)

    row = lambda i: (i, 0)
    fix = lambda i: (0, 0)
    return pl.pallas_call(
        body, name=name, grid=(S // tm,),
        in_specs=[_bs((tm, 384), row), _bs((1, 256), fix), _bs((1, 128), fix), _bs((256, 768), fix), _bs((128, 1152), fix),
                  _bs((tm, 128), row), _bs((tm, 128), row), _bs((tm, 128), row), _bs((tm, 128), row),
                  _bs((tm, 768), row), _bs((tm, 768), row), _bs((tm, 384), row)],
        out_specs=[_bs((tm, ZA), row), _bs((256, 768), fix), _bs((128, 1152), fix), _bs((1, 256), fix), _bs((1, 128), fix)],
        out_shape=[jax.ShapeDtypeStruct((S, ZA), F32), jax.ShapeDtypeStruct((256, 768), F32),
                   jax.ShapeDtypeStruct((128, 1152), F32), jax.ShapeDtypeStruct((1, 256), F32),
                   jax.ShapeDtypeStruct((1, 128), F32)],
        compiler_params=_params(("arbitrary",)),
    )(za, gq, gkv, wq, wkv, *tabs, dqf, dkf, dvm)


def _head_views(qb, kb, r, dkb, sel):
    if dkb == LANES:
        sl = slice(r * LANES, (r + 1) * LANES)
        return qb[:, sl], kb[:, sl], kb[:, sl]
    return jnp.where(sel, qb, jnp.zeros_like(qb)), kb, jnp.where(sel, kb, jnp.zeros_like(kb))


def _attn_fwd_t(q_arr, k_arr, vt_arr, *, nhp, dkb, qoff, koff, vtoff, scale, cum, cumT, name, rider=None):
    S = q_arr.shape[0]
    T = _tile(S, 512)
    nq = S // T
    W = 2 * dkb
    bias = cum is not None

    def body(*refs):
        if bias:
            q_ref, k_ref, vt_ref, cq_ref, ck_ref, o_ref, lse_ref, m_s, l_s, acc_s = refs
        else:
            q_ref, k_ref, vt_ref, o_ref, lse_ref, m_s, l_s, acc_s = refs
        hp, qi, ki = pl.program_id(0), pl.program_id(1), pl.program_id(2)
        lo_lane = lax.broadcasted_iota(jnp.int32, (1, LANES), 1) < 64
        lo_row = lax.broadcasted_iota(jnp.int32, (LANES, 1), 0) < 64

        @pl.when(ki == 0)
        def _():
            m_s[...] = jnp.full_like(m_s, NEG)
            l_s[...] = jnp.zeros_like(l_s)
            acc_s[...] = jnp.zeros_like(acc_s)

        def step(masked):
            qb, kb, vtb = q_ref[...], k_ref[...], vt_ref[...]
            if masked:
                mask = lax.broadcasted_iota(jnp.int32, (T, T), 0) <= lax.broadcasted_iota(jnp.int32, (T, T), 1)
            if bias:
                li = lax.broadcasted_iota(jnp.int32, (T, LANES), 1)
                ckb = ck_ref[...]
            for r in range(2):
                sel = lo_lane if r == 0 else jnp.logical_not(lo_lane)
                rsel = lo_row if r == 0 else jnp.logical_not(lo_row)
                q, k, _ = _head_views(qb, kb, r, dkb, sel)
                if scale != 1.0:
                    q = q * jnp.asarray(scale, q.dtype)
                s = _dot_nt(k, q)
                if bias:
                    ck = jnp.sum(jnp.where(li == 8 * hp + r, ckb, 0.0), axis=1, keepdims=True)
                    s = s + (cq_ref[r:r + 1, :] - ck)
                if masked:
                    s = jnp.where(mask, s, NEG)
                m_prev = m_s[r:r + 1, :]
                m_new = jnp.maximum(m_prev, jnp.max(s, axis=0, keepdims=True))
                alpha = jnp.exp(m_prev - m_new)
                p = jnp.exp(s - m_new)
                l_s[r:r + 1, :] = alpha * l_s[r:r + 1, :] + jnp.sum(p, axis=0, keepdims=True)
                m_s[r:r + 1, :] = m_new
                pv = _dot(jnp.where(rsel, vtb, jnp.zeros_like(vtb)), p.astype(BF16))
                acc_s[...] = acc_s[...] * jnp.where(rsel, alpha, 1.0) + pv

        @pl.when(ki < qi)
        def _():
            step(False)

        @pl.when(ki == qi)
        def _():
            step(True)

        @pl.when(ki == nq - 1)
        def _():
            inv = jnp.where(lo_row, 1.0 / l_s[0:1, :], 1.0 / l_s[1:2, :])
            o_ref[...] = (acc_s[...] * inv).T.astype(BF16)
            used = lax.broadcasted_iota(jnp.int32, (8, T), 0) < 2
            lse_ref[...] = jnp.where(used, m_s[...] + jnp.log(jnp.where(used, l_s[...], 1.0)), 0.0)

    kmap = lambda hp, qi, ki: jnp.minimum(ki, qi)
    in_specs = [_bs((T, W), lambda hp, qi, ki: (qi, qoff + hp)),
                _bs((T, W), lambda hp, qi, ki: (kmap(hp, qi, ki), koff + hp)),
                _bs((LANES, T), lambda hp, qi, ki: (vtoff + hp, kmap(hp, qi, ki)))]
    args = [q_arr, k_arr, vt_arr]
    if bias:
        in_specs += [_bs((8, T), lambda hp, qi, ki: (hp, qi)), _bs((T, LANES), lambda hp, qi, ki: (kmap(hp, qi, ki), 0))]
        args += [cumT, cum]
    return _call(
        body, name=name, grid=(nhp, nq, nq),
        in_specs=in_specs,
        out_specs=[_bs((T, LANES), lambda hp, qi, ki: (qi, hp)), _bs((None, 8, T), lambda hp, qi, ki: (hp, 0, qi))],
        out_shape=[jax.ShapeDtypeStruct((S, nhp * LANES), BF16), jax.ShapeDtypeStruct((nhp, 8, S), F32)],
        scratch=[pltpu.VMEM((8, T), F32), pltpu.VMEM((8, T), F32), pltpu.VMEM((LANES, T), F32)],
        dims=("parallel", "parallel", "arbitrary"), args=args, rider=rider)


def _attn_dvec(o_arr, do_arr, nhp, name):
    S = o_arr.shape[0]
    T = _tile(S, 512)

    def body(o_ref, do_ref, d_ref):
        prod = do_ref[...].astype(F32) * o_ref[...].astype(F32)
        li = lax.broadcasted_iota(jnp.int32, (T, LANES), 1)
        d0 = jnp.sum(jnp.where(li < 64, prod, 0.0), axis=1, keepdims=True)
        d1 = jnp.sum(jnp.where(li >= 64, prod, 0.0), axis=1, keepdims=True)
        d_ref[...] = jnp.where(li == 0, d0, jnp.where(li == 1, d1, 0.0)).T[0:8, :]

    return pl.pallas_call(
        body, name=name, grid=(nhp, S // T),
        in_specs=[_bs((T, LANES), lambda hp, i: (i, hp)), _bs((T, LANES), lambda hp, i: (i, hp))],
        out_specs=_bs((None, 8, T), lambda hp, i: (hp, 0, i)),
        out_shape=jax.ShapeDtypeStruct((nhp, 8, S), F32),
        compiler_params=_params(("parallel", "parallel")),
    )(o_arr, do_arr)


def _attn_bwd_t(q_arr, k_arr, kt_arr, v_arr, do_arr, lse, dvec, *, nhp, dkb, qoff, koff, ktoff, voff, scale, cum, cumT,
                name, rider=None):
    S = q_arr.shape[0]
    T = _tile(S, 512)
    nq = S // T
    W = 2 * dkb
    bias = cum is not None

    def body(*refs):
        if bias:
            (q_ref, k_ref, kt_ref, v_ref, do_ref, lse_ref, dvec_ref, cq_ref, ck_ref,
             dq_ref, dk_ref, dv_ref, dcq_ref, dck_ref, dqt_s, dk_s, dv_s, dcq_s, dck_s) = refs
        else:
            (q_ref, k_ref, kt_ref, v_ref, do_ref, lse_ref, dvec_ref,
             dq_ref, dk_ref, dv_ref, dqt_s, dk_s, dv_s) = refs
        hp, ki, qi = pl.program_id(0), pl.program_id(1), pl.program_id(2)
        lo_lane = lax.broadcasted_iota(jnp.int32, (1, LANES), 1) < 64
        lo_row = lax.broadcasted_iota(jnp.int32, (LANES, 1), 0) < 64

        @pl.when(jnp.logical_and(ki == 0, qi == 0))
        def _():
            dqt_s[...] = jnp.zeros_like(dqt_s)
            if bias:
                dcq_s[...] = jnp.zeros_like(dcq_s)

        @pl.when(qi == 0)
        def _():
            dk_s[...] = jnp.zeros_like(dk_s)
            dv_s[...] = jnp.zeros_like(dv_s)
            if bias:
                dck_s[...] = jnp.zeros_like(dck_s)

        def step(masked):
            qb, kb, ktb, vb, dob = q_ref[...], k_ref[...], kt_ref[...], v_ref[...], do_ref[...]
            if masked:
                mask = lax.broadcasted_iota(jnp.int32, (T, T), 0) <= lax.broadcasted_iota(jnp.int32, (T, T), 1)
            if bias:
                li = lax.broadcasted_iota(jnp.int32, (T, LANES), 1)
                ckb = ck_ref[...]
            for r in range(2):
                sel = lo_lane if r == 0 else jnp.logical_not(lo_lane)
                rsel = lo_row if r == 0 else jnp.logical_not(lo_row)
                q, k, _ = _head_views(qb, kb, r, dkb, sel)
                if scale != 1.0:
                    q = q * jnp.asarray(scale, q.dtype)
                s = _dot_nt(k, q)
                if bias:
                    ck = jnp.sum(jnp.where(li == 8 * hp + r, ckb, 0.0), axis=1, keepdims=True)
                    s = s + (cq_ref[r:r + 1, :] - ck)
                p = jnp.exp(s - lse_ref[r:r + 1, :])
                if masked:
                    p = jnp.where(mask, p, 0.0)
                do_r = jnp.where(sel, dob, jnp.zeros_like(dob))
                dp = _dot_nt(vb, do_r)
                ds = p * (dp - dvec_ref[r:r + 1, :])
                pb = p.astype(BF16)
                dsb = ds.astype(BF16)
                dv_s[...] += _dot(pb, do_r)
                if dkb == LANES:
                    sl = slice(r * LANES, (r + 1) * LANES)
                    dk_s[:, sl] += _dot(dsb, q)
                    dqt_s[qi, sl, :] += _dot(ktb[sl, :], dsb) * scale
                else:
                    dk_s[...] += _dot(dsb, q)
                    dqt_s[qi] += _dot(jnp.where(rsel, ktb, jnp.zeros_like(ktb)), dsb) * scale
                if bias:
                    dcq_s[qi, r:r + 1, :] += jnp.sum(ds, axis=0, keepdims=True)
                    dck_s[...] -= jnp.where(li == 8 * hp + r, jnp.sum(ds, axis=1, keepdims=True), 0.0)

        @pl.when(qi > ki)
        def _():
            step(False)

        @pl.when(qi == ki)
        def _():
            step(True)

        @pl.when(qi == nq - 1)
        def _():
            dk_ref[...] = dk_s[...]
            dv_ref[...] = dv_s[...]
            if bias:
                dck_ref[...] = dck_s[...]

        @pl.when(jnp.logical_and(ki == nq - 1, qi == nq - 1))
        def _():
            for c in range(nq):
                dq_ref[c * T:(c + 1) * T, :] = dqt_s[c].T
                if bias:
                    dcq_ref[:, c * T:(c + 1) * T] = dcq_s[c]

    qmap = lambda hp, ki, qi: jnp.maximum(qi, ki)
    in_specs = [_bs((T, W), lambda hp, ki, qi: (qmap(hp, ki, qi), qoff + hp)),
                _bs((T, W), lambda hp, ki, qi: (ki, koff + hp)),
                _bs((W, T), lambda hp, ki, qi: (ktoff + hp, ki)),
                _bs((T, LANES), lambda hp, ki, qi: (ki, voff + hp)),
                _bs((T, LANES), lambda hp, ki, qi: (qmap(hp, ki, qi), hp)),
                _bs((None, 8, T), lambda hp, ki, qi: (hp, 0, qmap(hp, ki, qi))),
                _bs((None, 8, T), lambda hp, ki, qi: (hp, 0, qmap(hp, ki, qi)))]
    args = [q_arr, k_arr, kt_arr, v_arr, do_arr, lse, dvec]
    out_specs = [_bs((S, W), lambda hp, ki, qi: (0, hp)), _bs((T, W), lambda hp, ki, qi: (ki, hp)),
                 _bs((T, LANES), lambda hp, ki, qi: (ki, hp))]
    out_shape = [jax.ShapeDtypeStruct((S, nhp * W), F32), jax.ShapeDtypeStruct((S, nhp * W), F32),
                 jax.ShapeDtypeStruct((S, nhp * LANES), F32)]
    scratch = [pltpu.VMEM((nq, W, T), F32), pltpu.VMEM((T, W), F32), pltpu.VMEM((T, LANES), F32)]
    if bias:
        in_specs += [_bs((8, T), lambda hp, ki, qi: (hp, qmap(hp, ki, qi))), _bs((T, LANES), lambda hp, ki, qi: (ki, 0))]
        args += [cumT, cum]
        out_specs += [_bs((None, 8, S), lambda hp, ki, qi: (hp, 0, 0)), _bs((None, T, LANES), lambda hp, ki, qi: (hp, ki, 0))]
        out_shape += [jax.ShapeDtypeStruct((nhp, 8, S), F32), jax.ShapeDtypeStruct((nhp, S, LANES), F32)]
        scratch += [pltpu.VMEM((nq, 8, T), F32), pltpu.VMEM((T, LANES), F32)]
    return _call(body, name=name, grid=(nhp, nq, nq), in_specs=in_specs, out_specs=out_specs, out_shape=out_shape,
                 scratch=scratch, dims=("arbitrary", "arbitrary", "arbitrary"), args=args, rider=rider)


def _gate_lanes(shape):
    lane = lax.broadcasted_iota(jnp.int32, shape, 1)
    return jnp.logical_and(lane < 8 * (FOX_HEADS // 2), lane % 8 < 2)


def _fox_prep(za, b_row, name):
    S = za.shape[0]
    nrow = 8 * (FOX_HEADS // 2)

    def body(tail_ref, b_ref, cum_ref, cumt_ref):
        x = tail_ref[...] + b_ref[...]
        logf = jnp.minimum(x, 0.0) - jnp.log(1.0 + jnp.exp(-jnp.abs(x)))
        y = jnp.where(_gate_lanes((S, LANES)), logf, 0.0)
        row = lax.broadcasted_iota(jnp.int32, (S, LANES), 0)
        k = 1
        while k < S:
            y = y + jnp.where(row >= k, pltpu.roll(y, k, 0), 0.0)
            k *= 2
        cum_ref[...] = y
        cumt_ref[...] = y.T[0:nrow, :]

    return pl.pallas_call(
        body, name=name, grid=(1,),
        in_specs=[_bs((S, LANES), lambda i: (0, TAIL0 // LANES)), _bs((1, LANES), lambda i: (0, 0))],
        out_specs=[_bs((S, LANES), lambda i: (0, 0)), _bs((nrow, S), lambda i: (0, 0))],
        out_shape=[jax.ShapeDtypeStruct((S, LANES), F32), jax.ShapeDtypeStruct((nrow, S), F32)],
        compiler_params=_params(("arbitrary",)),
    )(za, b_row)


def _fox_prep_bwd(za, b_row, dcq, dck, name):
    S = za.shape[0]
    nhp = FOX_HEADS // 2
    nrow = 8 * nhp
    dcq2 = dcq.reshape(nrow, S)

    def body(tail_ref, b_ref, dcq_ref, dck_ref, dt_ref, db_ref):
        x = tail_ref[...] + b_ref[...]
        d = jnp.concatenate([dcq_ref[...], jnp.zeros((LANES - nrow, S), F32)], axis=0).T
        for hp in range(nhp):
            d = d + dck_ref[hp]
        row = lax.broadcasted_iota(jnp.int32, (S, LANES), 0)
        k = 1
        while k < S:
            d = d + jnp.where(row < S - k, pltpu.roll(d, S - k, 0), 0.0)
            k *= 2
        df = jnp.where(_gate_lanes((S, LANES)), d * jax.nn.sigmoid(-x), 0.0)
        dt_ref[...] = df
        db_ref[...] = jnp.sum(df, axis=0, keepdims=True)

    return pl.pallas_call(
        body, name=name, grid=(1,),
        in_specs=[_bs((S, LANES), lambda i: (0, TAIL0 // LANES)), _bs((1, LANES), lambda i: (0, 0)),
                  _bs((nrow, S), lambda i: (0, 0)), _bs((nhp, S, LANES), lambda i: (0, 0, 0))],
        out_specs=[_bs((S, LANES), lambda i: (0, 0)), _bs((1, LANES), lambda i: (0, 0))],
        out_shape=[jax.ShapeDtypeStruct((S, LANES), F32), jax.ShapeDtypeStruct((1, LANES), F32)],
        compiler_params=_params(("arbitrary",)),
    )(za, b_row, dcq2, dck)


def _pool_select(half, lane_lo, vals):
    return jnp.where(lane_lo, jnp.where(half == 0, vals[0], vals[2]), jnp.where(half == 0, vals[1], vals[3]))


def _pool_den(S, half, lane_lo):
    cnt = (lax.broadcasted_iota(jnp.int32, (S, LANES), 0) + 1).astype(F32)
    w = _pool_select(half, lane_lo, [float(x) for x in POOL_WINDOWS])
    return jnp.minimum(cnt, w)


def _pool_fwd(za, wbd, scale, name):
    S = za.shape[0]

    def body(u_ref, w_ref, sc_ref, y_ref, pd_ref):
        half = pl.program_id(0)
        u = u_ref[...]
        row = lax.broadcasted_iota(jnp.int32, (S, LANES), 0)
        lane_lo = lax.broadcasted_iota(jnp.int32, (S, LANES), 1) < POOL_GROUP
        sums = []
        acc = u
        k = 1
        while k < POOL_WINDOWS[-1]:
            acc = acc + jnp.where(row >= k, pltpu.roll(acc, k, 0), 0.0)
            sums.append(acc)
            k *= 2
        pooled = _pool_select(half, lane_lo, sums) / _pool_den(S, half, lane_lo)
        pd = (pooled - u).astype(BF16)
        pd_ref[...] = pd
        y_ref[...] = (_dot(pd, w_ref[...]) * sc_ref[...]).astype(BF16)

    return pl.pallas_call(
        body, name=name, grid=(2,),
        in_specs=[_bs((S, LANES), lambda i: (0, 384 // LANES + i)), _bs((None, LANES, LANES), lambda i: (i, 0, 0)),
                  _bs((1, LANES), lambda i: (0, i))],
        out_specs=[_bs((S, LANES), lambda i: (0, i)), _bs((S, LANES), lambda i: (0, i))],
        out_shape=[jax.ShapeDtypeStruct((S, POOL_WIDTH), BF16), jax.ShapeDtypeStruct((S, POOL_WIDTH), BF16)],
        compiler_params=_params(("parallel",)),
    )(za, wbd, scale)


def _pool_bwd(dyb, pd, wbd, scale, name):
    S = pd.shape[0]

    def body(dy_ref, pd_ref, w_ref, sc_ref, du_ref, dw_ref, dsc_ref):
        half = pl.program_id(0)
        dy = dy_ref[...]
        pd = pd_ref[...]
        w = w_ref[...]
        ypre = _dot(pd, w)
        dsc_ref[...] = jnp.sum(dy * ypre, axis=0, keepdims=True)
        dyp = (dy * sc_ref[...]).astype(BF16)
        dw_ref[...] = _dot_tn(pd, dyp)
        dpd = _dot_nt(dyp, w)
        row = lax.broadcasted_iota(jnp.int32, (S, LANES), 0)
        lane_lo = lax.broadcasted_iota(jnp.int32, (S, LANES), 1) < POOL_GROUP
        acc = dpd / _pool_den(S, half, lane_lo)
        sums = []
        k = 1
        while k < POOL_WINDOWS[-1]:
            acc = acc + jnp.where(row < S - k, pltpu.roll(acc, S - k, 0), 0.0)
            sums.append(acc)
            k *= 2
        du_ref[...] = _pool_select(half, lane_lo, sums) - dpd

    return pl.pallas_call(
        body, name=name, grid=(2,),
        in_specs=[_bs((S, LANES), lambda i: (0, i)), _bs((S, LANES), lambda i: (0, i)),
                  _bs((None, LANES, LANES), lambda i: (i, 0, 0)), _bs((1, LANES), lambda i: (0, i))],
        out_specs=[_bs((S, LANES), lambda i: (0, i)), _bs((None, LANES, LANES), lambda i: (i, 0, 0)),
                   _bs((1, LANES), lambda i: (0, i))],
        out_shape=[jax.ShapeDtypeStruct((S, POOL_WIDTH), F32), jax.ShapeDtypeStruct((2, LANES, LANES), F32),
                   jax.ShapeDtypeStruct((1, POOL_WIDTH), F32)],
        compiler_params=_params(("parallel",)),
    )(dyb, pd, wbd, scale)


def _mix_out_fwd(x, ya, yb, yc, w_out, name):
    S, D = x.shape
    tm = _tile(S, 512)
    K = w_out.shape[0]

    def body(x_ref, ya_ref, yb_ref, yc_ref, w_ref, xo_ref, yc_out):
        ycat = jnp.concatenate([ya_ref[...], yb_ref[...], yc_ref[...]], axis=1)
        yc_out[...] = ycat
        xo_ref[...] = x_ref[...] + _dot(ycat, w_ref[...])

    row = lambda i: (i, 0)
    return pl.pallas_call(
        body, name=name, grid=(S // tm,),
        in_specs=[_bs((tm, D), row), _bs((tm, 384), row), _bs((tm, 256), row), _bs((tm, 384), row),
                  _bs((K, D), lambda i: (0, 0))],
        out_specs=[_bs((tm, D), row), _bs((tm, K), row)],
        out_shape=[jax.ShapeDtypeStruct((S, D), F32), jax.ShapeDtypeStruct((S, K), BF16)],
        compiler_params=_params(("parallel",)),
    )(x, ya, yb, yc, w_out)


def _mix_out_bwd(dy, w_out, name):
    S, D = dy.shape
    tm = _tile(S, 512)
    K = w_out.shape[0]

    def body(dy_ref, w_ref, da_ref, db_ref, dc_ref):
        d = _dot_nt(dy_ref[...].astype(BF16), w_ref[...])
        da_ref[...] = d[:, 0:384].astype(BF16)
        db_ref[...] = d[:, 384:640]
        dc_ref[...] = d[:, 640:1024].astype(BF16)

    row = lambda i: (i, 0)
    return pl.pallas_call(
        body, name=name, grid=(S // tm,),
        in_specs=[_bs((tm, D), row), _bs((K, D), lambda i: (0, 0))],
        out_specs=[_bs((tm, 384), row), _bs((tm, 256), row), _bs((tm, 384), row)],
        out_shape=[jax.ShapeDtypeStruct((S, 384), BF16), jax.ShapeDtypeStruct((S, 256), F32),
                   jax.ShapeDtypeStruct((S, 384), BF16)],
        compiler_params=_params(("parallel",)),
    )(dy, w_out)


def _loss_head(x, gam, target, name):
    S, D = x.shape
    tm = _tile(S, 512)

    def body(x_ref, gam_ref, t_ref, dx_ref, dgam_ref, loss_ref):
        i = pl.program_id(0)
        xv = x_ref[...]
        err = _rms(xv, gam_ref[...]) - t_ref[...]
        part = 0.5 * jnp.sum(jnp.mean(err * err, axis=-1, keepdims=True), axis=0, keepdims=True)
        dxn, dgam = _rms_bwd(err * (1.0 / D), xv, gam_ref[...])
        dx_ref[...] = dxn
        _accum_out(dgam_ref, i == 0, dgam)
        _accum_out(loss_ref, i == 0, jnp.broadcast_to(part, (1, LANES)))

    row = lambda i: (i, 0)
    fix = lambda i: (0, 0)
    return pl.pallas_call(
        body, name=name, grid=(S // tm,),
        in_specs=[_bs((tm, D), row), _bs((1, D), fix), _bs((tm, D), row)],
        out_specs=[_bs((tm, D), row), _bs((1, D), fix), _bs((1, LANES), fix)],
        out_shape=[jax.ShapeDtypeStruct((S, D), F32), jax.ShapeDtypeStruct((1, D), F32),
                   jax.ShapeDtypeStruct((1, LANES), F32)],
        compiler_params=_params(("arbitrary",)),
    )(x, gam, target)


def _adam_math(g, w, m, v):
    m = ADAM_B1 * m + (1.0 - ADAM_B1) * g
    v = ADAM_B2 * v + (1.0 - ADAM_B2) * (g * g)
    m_hat = m / (1.0 - ADAM_B1 ** ADAM_STEP)
    v_hat = v / (1.0 - ADAM_B2 ** ADAM_STEP)
    delta = -ADAM_LR * (m_hat / (jnp.sqrt(v_hat) + ADAM_EPS) + ADAM_WD * w)
    return delta, m, v


def _adam_sum(recv, w, m, v, layer, prev, tr, name):
    L, R, C = w.shape
    Cp = recv.shape[2]
    tr = _tile(R, tr)

    def body(r_ref, w_ref, m_ref, v_ref, *rest):
        g_out, d_out, m_out, v_out = rest[len(rest) - 4:]
        g = r_ref[0, :, 0:C].astype(F32)
        for p in range(1, N_DEV):
            g = g + r_ref[p, :, 0:C].astype(F32)
        delta, mn, vn = _adam_math(g, w_ref[...], m_ref[...], v_ref[...])
        g_out[...] = g
        d_out[...] = delta
        m_out[...] = mn
        v_out[...] = vn

    blk = _bs((None, tr, C), lambda i: (layer, i, 0))
    shp = jax.ShapeDtypeStruct((L, R, C), F32)
    in_specs = [_bs((N_DEV, tr, Cp), lambda i: (0, i, 0)), blk, blk, blk]
    args = [recv, w, m, v]
    aliases = {}
    if prev is not None:
        in_specs += [HBM_SPEC] * 4
        args += list(prev)
        aliases = {4 + k: k for k in range(4)}
    return pl.pallas_call(
        body, name=name, grid=(R // tr,),
        in_specs=in_specs, out_specs=[blk, blk, blk, blk], out_shape=[shp, shp, shp, shp],
        input_output_aliases=aliases, compiler_params=_params(("parallel",)),
    )(*args)


def _dev_index(px, py, pc):
    return 4 * px + 2 * py + pc


class _GatherRider:
    def __init__(self, shards, out_shapes, views, zero_src=None, zero_views=()):
        self.n = len(shards)
        self.views = views
        self.zero_views = list(zero_views) if zero_src is not None else []
        self.srcs = list(shards) + ([zero_src] if self.zero_views else [])
        self.out_shapes = list(out_shapes)
        n, nz = self.n, len(self.zero_views)
        self.scratch = [pltpu.SemaphoreType.DMA((n, 7)), pltpu.SemaphoreType.DMA((n, 7)),
                        pltpu.SemaphoreType.DMA((n,)), pltpu.SemaphoreType.DMA((max(nz, 1),))]

    def _copies(self, ins, outs, sems):
        n = self.n
        send_sems, recv_sems, local_sems, zero_sems = sems
        x, y, c = lax.axis_index("x"), lax.axis_index("y"), lax.axis_index("c")
        me, sibling = (x, y, c), (x, y, 1 - c)
        chips = [(1 - x, y), (x, 1 - y), (1 - x, 1 - y)]

        def rows(a, blk):
            return self.views[a](outs[a], _dev_index(*blk))

        def copy(a, k, blk, to, src=None):
            return pltpu.make_async_remote_copy(
                src_ref=rows(a, blk) if src is None else src, dst_ref=rows(a, blk),
                send_sem=send_sems.at[a, k], recv_sem=recv_sems.at[a, k], device_id=to, device_id_type=MESH_ID)

        local = [pltpu.make_async_copy(ins[a], rows(a, me), local_sems.at[a]) for a in range(n)]
        local += [pltpu.make_async_copy(ins[n], view(outs[a]), zero_sems.at[i])
                  for i, (a, view) in enumerate(self.zero_views)]
        first = []
        for a in range(n):
            first.append(copy(a, 0, me, sibling, src=ins[a]))
            first += [copy(a, 1 + j, me, (*chip, c), src=ins[a]) for j, chip in enumerate(chips)]
        over_ici = [[copy(a, 1 + j, (*chip, c), me) for a in range(n)] for j, chip in enumerate(chips)]
        passed = [[copy(a, 4 + j, (*chip, c), sibling) for a in range(n)] for j, chip in enumerate(chips)]
        from_sibling = [copy(a, 0, sibling, me) for a in range(n)]
        from_sibling += [copy(a, 4 + j, (*chip, 1 - c), me) for a in range(n) for j, chip in enumerate(chips)]
        return local, first, over_ici, passed, from_sibling

    def begin(self, ins, outs, sems):
        local, first, _, _, _ = self._copies(ins, outs, sems)
        for cp in local + first:
            cp.start()

    def middle(self, ins, outs, sems):
        _, _, over_ici, passed, _ = self._copies(ins, outs, sems)
        for arrived, onward in zip(over_ici, passed):
            for cp, fwd in zip(arrived, onward):
                cp.wait_recv()
                fwd.start()

    def end(self, ins, outs, sems):
        local, first, _, passed, from_sibling = self._copies(ins, outs, sems)
        for cp in from_sibling:
            cp.wait_recv()
        for cp in first + [fwd for onward in passed for fwd in onward]:
            cp.wait_send()
        for cp in local:
            cp.wait()


class _ScatterRider:
    _MASKS = [(kx, ky, kc) for kx in (0, 1) for ky in (0, 1) for kc in (0, 1)][1:]

    def __init__(self, srcs, out_shapes, src_of, dst_at):
        self.n = len(srcs)
        self.srcs = list(srcs)
        self.out_shapes = list(out_shapes)
        self.src_of = src_of
        self.dst_at = dst_at
        n = self.n
        self.scratch = [pltpu.SemaphoreType.DMA((n, 7)), pltpu.SemaphoreType.DMA((n, 7)), pltpu.SemaphoreType.DMA((n,))]

    def _copies(self, ins, outs, sems):
        send_sems, recv_sems, local_sems = sems
        x, y, c = lax.axis_index("x"), lax.axis_index("y"), lax.axis_index("c")
        my = _dev_index(x, y, c)
        peers = [(1 - x if kx else x, 1 - y if ky else y, 1 - c if kc else c) for kx, ky, kc in self._MASKS]

        def send(i, k, to):
            return pltpu.make_async_remote_copy(
                src_ref=self.src_of[i](ins[i], _dev_index(*to)), dst_ref=self.dst_at[i](outs[i], my),
                send_sem=send_sems.at[i, k], recv_sem=recv_sems.at[i, k], device_id=to, device_id_type=MESH_ID)

        def arrival(i, k, frm):
            slot = self.dst_at[i](outs[i], _dev_index(*frm))
            return pltpu.make_async_remote_copy(
                src_ref=slot, dst_ref=slot, send_sem=send_sems.at[i, k], recv_sem=recv_sems.at[i, k],
                device_id=frm, device_id_type=MESH_ID)

        local = [pltpu.make_async_copy(self.src_of[i](ins[i], my), self.dst_at[i](outs[i], my), local_sems.at[i])
                 for i in range(self.n)]
        sends = [send(i, k, to) for k, to in enumerate(peers) for i in range(self.n)]
        arrivals = [arrival(i, k, frm) for k, frm in enumerate(peers) for i in range(self.n)]
        return local, sends, arrivals

    def begin(self, ins, outs, sems):
        local, sends, _ = self._copies(ins, outs, sems)
        for cp in local + sends:
            cp.start()

    def middle(self, ins, outs, sems):
        pass

    def end(self, ins, outs, sems):
        local, sends, arrivals = self._copies(ins, outs, sems)
        for cp in arrivals:
            cp.wait_recv()
        for cp in sends:
            cp.wait_send()
        for cp in local:
            cp.wait()


def _comm_call(rider, name):
    k_in = len(rider.srcs)
    k_out = len(rider.out_shapes)

    def body(*refs):
        ins, outs, sems = refs[:k_in], refs[k_in:k_in + k_out], refs[k_in + k_out:]
        rider.begin(ins, outs, sems)
        rider.middle(ins, outs, sems)
        rider.end(ins, outs, sems)

    return pl.pallas_call(
        body, name=name, in_specs=[HBM_SPEC] * k_in, out_specs=[HBM_SPEC] * k_out, out_shape=rider.out_shapes,
        scratch_shapes=rider.scratch, compiler_params=pltpu.CompilerParams(has_side_effects=True),
    )(*rider.srcs)


def _pad_w_in(w):
    take = jnp.take(w, np.maximum(_IN_PERM, 0), axis=-1)
    return jnp.where(_IN_PERM >= 0, take, jnp.zeros_like(take))


def _unpad_w_in(g):
    return jnp.take(g, _IN_INV, axis=-1)


def _small_pack(w_q_b, w_kv_b):
    a = jnp.pad(w_q_b, ((0, 0), (0, 0), (0, LANES - w_q_b.shape[2])))
    b = jnp.pad(w_kv_b, ((0, 0), (0, 0), (0, LANES - w_kv_b.shape[2])))
    return jnp.concatenate([a, b], axis=1)


def _small_unpack(p, cq, ckv):
    return p[:, 0:MLA_Q_RANK, 0:cq], p[:, MLA_Q_RANK:, 0:ckv]


def _mla_weights(wsm):
    H = MLA_HEADS
    cq = H * (MLA_NOPE + MLA_ROPE) // N_DEV
    ckv = H * (MLA_NOPE + MLA_V) // N_DEV
    wq = wsm[:, 0:MLA_Q_RANK, 0:cq].transpose(1, 0, 2).reshape(MLA_Q_RANK, H, MLA_NOPE + MLA_ROPE)
    wq = jnp.pad(wq, ((0, 0), (0, 0), (0, HEAD_BLOCK - MLA_NOPE - MLA_ROPE))).reshape(MLA_Q_RANK, H * HEAD_BLOCK)
    wkv = wsm[:, MLA_Q_RANK:, 0:ckv].transpose(1, 0, 2).reshape(MLA_KV_RANK, H, MLA_NOPE + MLA_V)
    wk = jnp.pad(wkv[:, :, 0:MLA_NOPE], ((0, 0), (0, 0), (0, HEAD_BLOCK - MLA_NOPE))).reshape(MLA_KV_RANK, H * HEAD_BLOCK)
    wv = wkv[:, :, MLA_NOPE:].reshape(MLA_KV_RANK, H * MLA_V)
    return wq, jnp.concatenate([wk, wv], axis=1)


def _mla_grads_to_blocks(dwq, dwkv):
    H = MLA_HEADS
    gq = dwq.reshape(MLA_Q_RANK, H, HEAD_BLOCK)[:, :, 0:MLA_NOPE + MLA_ROPE].reshape(MLA_Q_RANK, N_DEV, -1)
    gk = dwkv[:, 0:H * HEAD_BLOCK].reshape(MLA_KV_RANK, H, HEAD_BLOCK)[:, :, 0:MLA_NOPE]
    gv = dwkv[:, H * HEAD_BLOCK:].reshape(MLA_KV_RANK, H, MLA_V)
    gkv = jnp.concatenate([gk, gv], axis=2).reshape(MLA_KV_RANK, N_DEV, -1)
    return _small_pack(gq.transpose(1, 0, 2), gkv.transpose(1, 0, 2)).astype(BF16)


def _pool_blockdiag(pool_w):
    z = jnp.zeros((POOL_GROUP, POOL_GROUP), pool_w.dtype)
    halves = [jnp.concatenate([jnp.concatenate([pool_w[2 * i], z], axis=1),
                               jnp.concatenate([z, pool_w[2 * i + 1]], axis=1)], axis=0) for i in range(2)]
    return jnp.stack(halves)


def _pool_blockdiag_t(dw):
    g = POOL_GROUP
    return jnp.stack([dw[0, 0:g, 0:g], dw[0, g:, g:], dw[1, 0:g, 0:g], dw[1, g:, g:]])


def _gate_row(b):
    return jnp.zeros((LANES,), b.dtype).at[_F_LANES].set(b).reshape(1, LANES)


_SMALL = ("ffn1_norm", "mix_norm", "q_a_norm", "kv_a_norm", "pool_w", "pool_scale", "fox_b_f", "ffn2_norm", "final_norm")


def _pack_small(tree):
    rows, recipe = [], []
    for name in _SMALL:
        a = tree[name]
        flat = a.reshape(-1)
        n = flat.shape[0]
        nrow = -(-n // (8 * LANES)) * 8
        flat = jnp.pad(flat, (0, nrow * LANES - n))
        rows.append(flat.reshape(nrow, LANES))
        recipe.append((name, a.shape, n, nrow))
    return jnp.concatenate(rows, axis=0), recipe


def _unpack_small(packed, recipe):
    out, r0 = {}, 0
    for name, shape, n, nrow in recipe:
        out[name] = packed[r0:r0 + nrow].reshape(-1)[0:n].reshape(shape)
        r0 += nrow
    return out


def _adam_small(packs, w, m, v, name):
    R = w.shape[0]

    def body(p_ref, w_ref, m_ref, v_ref, g_out, d_out, m_out, v_out):
        g = p_ref[0]
        for p in range(1, N_DEV):
            g = g + p_ref[p]
        delta, mn, vn = _adam_math(g, w_ref[...], m_ref[...], v_ref[...])
        g_out[...] = g
        d_out[...] = delta
        m_out[...] = mn
        v_out[...] = vn

    blk = _bs((R, LANES), lambda i: (0, 0))
    shp = jax.ShapeDtypeStruct((R, LANES), F32)
    return pl.pallas_call(
        body, name=name, grid=(1,),
        in_specs=[_bs((N_DEV, R, LANES), lambda i: (0, 0, 0)), blk, blk, blk],
        out_specs=[blk, blk, blk, blk], out_shape=[shp, shp, shp, shp],
        compiler_params=_params(("arbitrary",)),
    )(packs, w, m, v)


_GATHER_PLAN = {
    ("first", 0): (("ffn1_w_gu", 0), ("ffn1_w_down", 0)),
    ("ffn1_fwd", 0): (("w_in", 0), ("w_small", 0), ("w_out", 0), ("ffn2_w_down", 0)),
    ("mla_attn_fwd", 0): (("ffn2_w_gu", 0), ("ffn1_w_gu", 1)),
    ("fox_attn_fwd", 0): (("ffn1_w_down", 1), ("w_in", 1), ("w_small", 1), ("w_out", 1)),
    ("mla_attn_fwd", 1): (("ffn2_w_gu", 1), ("ffn2_w_down", 1)),
}
_SCATTER_PLAN = {
    ("mla_attn_bwd", 1): (("ffn2_w_gu", 1), ("ffn2_w_down", 1)),
    ("fox_attn_bwd", 1): (("w_out", 1),),
    ("ffn1_bwd", 1): (("w_in", 1), ("w_small", 1)),
    ("ffn2_bwd", 0): (("ffn1_w_gu", 1),),
    ("mla_attn_bwd", 0): (("ffn1_w_down", 1), ("ffn2_w_gu", 0)),
    ("fox_attn_bwd", 0): (("ffn2_w_down", 0), ("w_out", 0)),
    ("ffn1_bwd_a", 0): (("w_in", 0), ("w_small", 0)),
    ("ffn1_dwgu", 0): (("ffn1_w_down", 0),),
    ("ffn1_bwd_b", 0): (("ffn1_w_gu", 0),),
}
_SPLIT_BWD = (("ffn1", 0),)


class _Exchange:
    def __init__(self, shards, D, f_sh, r_in, r_out):
        self.shards = shards
        self.D, self.f_sh, self.r_in, self.r_out = D, f_sh, r_in, r_out
        self.weights, self.grads, self.recv = {}, {}, {}

    def _rows(self, kind):
        n = {"w_gu": 2 * self.f_sh, "w_down": self.f_sh}[kind]
        return lambda ref, p: ref.at[pl.ds(pl.multiple_of(p * n, 16), n)]

    def _gathered_shape(self, kind):
        D, f_sh = self.D, self.f_sh
        return {"w_gu": (N_DEV * 2 * f_sh, D), "w_down": (N_DEV * f_sh, D), "w_in": (N_DEV, self.r_in, N_PAD),
                "w_small": (N_DEV, MLA_Q_RANK + MLA_KV_RANK, LANES), "w_out": (N_DEV, self.r_out, D)}[kind]

    def _recv_shape(self, kind):
        D, f_sh = self.D, self.f_sh
        return {"w_gu": (N_DEV, 2 * f_sh, D), "w_down": (N_DEV, f_sh, D), "w_in": (N_DEV, self.r_in, N_IN),
                "w_small": (N_DEV, MLA_Q_RANK + MLA_KV_RANK, LANES), "w_out": (N_DEV, self.r_out, D)}[kind]

    @staticmethod
    def _kind(name):
        return name[5:] if name.startswith("ffn") else name

    def gather_rider(self, call, l):
        keys = _GATHER_PLAN.get((call, l))
        if not keys:
            return None
        by_dev = lambda ref, p: ref.at[p]
        shards, shapes, views = [], [], []
        for key in keys:
            kind = self._kind(key[0])
            shards.append(self.shards[key])
            shapes.append(jax.ShapeDtypeStruct(self._gathered_shape(kind), BF16))
            views.append(self._rows(kind) if kind in ("w_gu", "w_down") else by_dev)
        return _GatherRider(shards, shapes, views)

    def gathered(self, call, l, outs):
        for key, w in zip(_GATHER_PLAN.get((call, l), ()), outs):
            if self._kind(key[0]) in ("w_in", "w_out"):
                w = w.reshape((N_DEV * w.shape[1],) + w.shape[2:])
            self.weights[key] = w

    def scatter_rider(self, call, l, pack=None):
        keys = _SCATTER_PLAN.get((call, l), ())
        if not keys and pack is None:
            return None
        by_dev = lambda ref, p: ref.at[p]
        srcs, shapes, src_of = [], [], []
        for key in keys:
            kind = self._kind(key[0])
            srcs.append(self.grads[key])
            shapes.append(jax.ShapeDtypeStruct(self._recv_shape(kind), BF16))
            src_of.append(self._rows(kind) if kind in ("w_gu", "w_down") else by_dev)
        if pack is not None:
            srcs.append(pack)
            shapes.append(jax.ShapeDtypeStruct((N_DEV,) + pack.shape, pack.dtype))
            src_of.append(lambda ref, p: ref)
        return _ScatterRider(srcs, shapes, src_of, [by_dev] * len(srcs))

    def scattered(self, call, l, outs):
        for key, r in zip(_SCATTER_PLAN.get((call, l), ()), outs):
            self.recv[key] = r


def _local_step(x, target, ex, small):
    S, D = x.shape
    tabs = _rope_tables(S)
    nhp_a, nhp_c = MLA_HEADS // 2, FOX_HEADS // 2
    fox_scale = 1.0 / math.sqrt(FOX_HEAD_DIM)
    ex.gathered("first", 0, _comm_call(ex.gather_rider("first", 0), "gather_first"))
    saved = []
    for l in range(DEPTH):
        s = {}
        s["x0"] = x
        wgu1, wd1 = ex.weights[("ffn1_w_gu", l)], ex.weights[("ffn1_w_down", l)]
        (x1, s["h1"], s["gu1"]), got = _ffn_fwd_full(x, small["ffn1_norm"][l][None], wgu1, wd1, FFN_FWD_TOKENS,
                                                    f"ffn1_fwd_l{l}", rider=ex.gather_rider("ffn1_fwd", l))
        ex.gathered("ffn1_fwd", l, got)
        s["x1"] = x1
        w_in = ex.weights[("w_in", l)]
        s["h2"], za, zf, zkvt = _mix_in_fwd(x1, small["mix_norm"][l][None], w_in, f"mix_in_fwd_l{l}")
        s["za"], s["zf"], s["zkvt"] = za, zf, zkvt
        wq, wkv = _mla_weights(ex.weights[("w_small", l)])
        s["wq"], s["wkv"] = wq, wkv
        gq, gkv = small["q_a_norm"][l][None], small["kv_a_norm"][l][None]
        qf, kf, vm, kft, vmt = _mla_prep(za, gq, gkv, wq, wkv, tabs, f"mla_prep_l{l}")
        s["qf"], s["kf"], s["vm"], s["kft"] = qf, kf, vm, kft
        (ya, lse_a), got = _attn_fwd_t(qf, kf, vmt, nhp=nhp_a, dkb=LANES, qoff=0, koff=0, vtoff=0, scale=1.0,
                                       cum=None, cumT=None, name=f"mla_attn_fwd_l{l}",
                                       rider=ex.gather_rider("mla_attn_fwd", l))
        ex.gathered("mla_attn_fwd", l, got)
        s["ya"], s["lse_a"] = ya, lse_a
        b_row = _gate_row(small["fox_b_f"][l])
        s["b_row"] = b_row
        cum, cumT = _fox_prep(za, b_row, f"fox_prep_l{l}")
        s["cum"], s["cumT"] = cum, cumT
        (yc, lse_c), got = _attn_fwd_t(zf, zf, zkvt, nhp=nhp_c, dkb=64, qoff=0, koff=nhp_c, vtoff=nhp_c, scale=fox_scale,
                                       cum=cum, cumT=cumT, name=f"fox_attn_fwd_l{l}",
                                       rider=ex.gather_rider("fox_attn_fwd", l))
        ex.gathered("fox_attn_fwd", l, got)
        s["yc"], s["lse_c"] = yc, lse_c
        wbd = _pool_blockdiag(small["pool_w"][l]).astype(BF16)
        s["wbd"] = wbd
        psc = small["pool_scale"][l][None]
        yb, s["pd"] = _pool_fwd(za, wbd, psc, f"pool_fwd_l{l}")
        w_out = ex.weights[("w_out", l)]
        x2, s["ycat"] = _mix_out_fwd(x1, ya, yb, yc, w_out, f"mix_out_fwd_l{l}")
        s["x2"] = x2
        wgu2, wd2 = ex.weights[("ffn2_w_gu", l)], ex.weights[("ffn2_w_down", l)]
        (x, s["h3"], s["gu2"]), _ = _ffn_fwd_full(x2, small["ffn2_norm"][l][None], wgu2, wd2, FFN_FWD_TOKENS,
                                                  f"ffn2_fwd_l{l}")
        saved.append(s)

    dx, d_final, loss = _loss_head(x, small["final_norm"][None], target, "loss_head")

    small_grads = [None] * DEPTH
    for l in reversed(range(DEPTH)):
        s = saved[l]
        g = {}
        wgu2, wd2 = ex.weights[("ffn2_w_gu", l)], ex.weights[("ffn2_w_down", l)]
        dy3 = dx
        (dx, g["ffn2_norm"], dgu, act, dyh), got = _ffn_bwd_full(
            dy3, s["x2"], small["ffn2_norm"][l][None], s["gu2"], wgu2, wd2, FFN_BWD_TOKENS, f"ffn2_bwd_l{l}",
            rider=ex.scatter_rider("ffn2_bwd", l))
        ex.scattered("ffn2_bwd", l, got)
        ex.grads[("ffn2_w_gu", l)] = _ffn_weight_grad(dgu, s["h3"], f"ffn2_dwgu_l{l}")
        ex.grads[("ffn2_w_down", l)] = _ffn_weight_grad(act, dyh, f"ffn2_dwd_l{l}")

        w_out = ex.weights[("w_out", l)]
        dya, dyb, dyc = _mix_out_bwd(dx, w_out, f"mix_out_bwd_l{l}")
        dw_out = _mm_tn(s["ycat"][None], dx[None], 1, lambda p: 0, lambda p: 0, 1024, 1024, f"dwout_l{l}", ts=1024)[0]
        ex.grads[("w_out", l)] = dw_out.reshape(N_DEV, ex.r_out, D)

        dvec_a = _attn_dvec(s["ya"], dya, nhp_a, f"mla_dvec_l{l}")
        (dqf, dkf, dvm), got = _attn_bwd_t(s["qf"], s["kf"], s["kft"], s["vm"], dya, s["lse_a"], dvec_a, nhp=nhp_a,
                                           dkb=LANES, qoff=0, koff=0, ktoff=0, voff=0, scale=1.0, cum=None, cumT=None,
                                           name=f"mla_attn_bwd_l{l}", rider=ex.scatter_rider("mla_attn_bwd", l))
        ex.scattered("mla_attn_bwd", l, got)
        zf = s["zf"]
        dvec_c = _attn_dvec(s["yc"], dyc, nhp_c, f"fox_dvec_l{l}")
        (dqc, dkc, dvc, dcq, dck), got = _attn_bwd_t(zf, zf, s["zkvt"], zf, dyc, s["lse_c"], dvec_c, nhp=nhp_c, dkb=64,
                                                     qoff=0, koff=nhp_c, ktoff=0, voff=2 * nhp_c, scale=fox_scale,
                                                     cum=s["cum"], cumT=s["cumT"], name=f"fox_attn_bwd_l{l}",
                                                     rider=ex.scatter_rider("fox_attn_bwd", l))
        ex.scattered("fox_attn_bwd", l, got)
        dtail_f, db = _fox_prep_bwd(s["za"], s["b_row"], dcq, dck, f"fox_prep_bwd_l{l}")
        g["fox_b_f"] = db[0, _F_LANES]
        psc = small["pool_scale"][l][None]
        du, dwbd, dpsc = _pool_bwd(dyb, s["pd"], s["wbd"], psc, f"pool_bwd_l{l}")
        g["pool_w"] = _pool_blockdiag_t(dwbd)
        g["pool_scale"] = dpsc[0]
        gq, gkv = small["q_a_norm"][l][None], small["kv_a_norm"][l][None]
        dza, dwq, dwkv, dgq, dgkv = _mla_prep_bwd(s["za"], gq, gkv, s["wq"], s["wkv"], tabs, dqf, dkf, dvm,
                                                   f"mla_prep_bwd_l{l}")
        g["q_a_norm"], g["kv_a_norm"] = dgq[0], dgkv[0]
        ex.grads[("w_small", l)] = _mla_grads_to_blocks(dwq, dwkv)
        w_in = ex.weights[("w_in", l)]
        dx, g["mix_norm"], dz = _mix_in_bwd(dx, s["x1"], small["mix_norm"][l][None], dza, du, dtail_f, dqc, dkc, dvc,
                                            w_in, f"mix_in_bwd_l{l}")
        dw_in = _unpad_w_in(_mm_tn(s["h2"][None], dz[None], 1, lambda p: 0, lambda p: 0, 1024, 640, f"dwin_l{l}",
                                   ts=4096)[0])
        ex.grads[("w_in", l)] = dw_in.reshape(N_DEV, ex.r_in, N_IN)

        wgu1, wd1 = ex.weights[("ffn1_w_gu", l)], ex.weights[("ffn1_w_down", l)]
        dy1 = dx
        gam1 = small["ffn1_norm"][l][None]
        split = ("ffn1", l) in _SPLIT_BWD
        if split:
            (dgu, act, dyh), got = _ffn_bwd_full(dy1, None, None, s["gu1"], None, wd1, FFN_BWD_TOKENS, f"ffn1_bwd_a_l{l}",
                                                 phase="act", rider=ex.scatter_rider("ffn1_bwd_a", l))
            ex.scattered("ffn1_bwd_a", l, got)
        else:
            (dx, g["ffn1_norm"], dgu, act, dyh), got = _ffn_bwd_full(
                dy1, s["x0"], gam1, s["gu1"], wgu1, wd1, FFN_BWD_TOKENS, f"ffn1_bwd_l{l}",
                rider=ex.scatter_rider("ffn1_bwd", l))
            ex.scattered("ffn1_bwd", l, got)
        ex.grads[("ffn1_w_down", l)] = _ffn_weight_grad(act, dyh, f"ffn1_dwd_l{l}")
        rider = ex.scatter_rider("ffn1_dwgu", l)
        dwgu = _ffn_weight_grad(dgu, s["h1"], f"ffn1_dwgu_l{l}", rider=rider)
        if rider is not None:
            dwgu, got = dwgu
            ex.scattered("ffn1_dwgu", l, got)
        ex.grads[("ffn1_w_gu", l)] = dwgu
        if split:
            (dx, g["ffn1_norm"]), got = _ffn_bwd_full(dy1, s["x0"], gam1, None, wgu1, None, FFN_BWD_TOKENS,
                                                     f"ffn1_bwd_b_l{l}", phase="in", dgu_in=dgu,
                                                     rider=ex.scatter_rider("ffn1_bwd_b", l))
            ex.scattered("ffn1_bwd_b", l, got)
        for k in ("ffn1_norm", "ffn2_norm", "mix_norm"):
            g[k] = g[k][0]
        small_grads[l] = g
    return loss, dx, small_grads, d_final[0]


_BIG = ("ffn1_w_gu", "ffn1_w_down", "w_in", "w_small", "w_out", "ffn2_w_gu", "ffn2_w_down")


def kernel(x, ffn1_norm, ffn1_w_gu, ffn1_w_down, mix_norm, w_in, q_a_norm, w_q_b, kv_a_norm, w_kv_b, pool_w, pool_scale, fox_b_f, w_out, ffn2_norm, ffn2_w_gu, ffn2_w_down, final_norm, loss_target, m_ffn1_norm, m_ffn1_w_gu, m_ffn1_w_down, m_mix_norm, m_w_in, m_q_a_norm, m_w_q_b, m_kv_a_norm, m_w_kv_b, m_pool_w, m_pool_scale, m_fox_b_f, m_w_out, m_ffn2_norm, m_ffn2_w_gu, m_ffn2_w_down, m_final_norm, v_ffn1_norm, v_ffn1_w_gu, v_ffn1_w_down, v_mix_norm, v_w_in, v_q_a_norm, v_w_q_b, v_kv_a_norm, v_w_kv_b, v_pool_w, v_pool_scale, v_fox_b_f, v_w_out, v_ffn2_norm, v_ffn2_w_gu, v_ffn2_w_down, v_final_norm):
    W = dict(ffn1_norm=ffn1_norm, ffn1_w_gu=ffn1_w_gu, ffn1_w_down=ffn1_w_down, mix_norm=mix_norm, w_in=w_in,
             q_a_norm=q_a_norm, w_q_b=w_q_b, kv_a_norm=kv_a_norm, w_kv_b=w_kv_b, pool_w=pool_w, pool_scale=pool_scale,
             fox_b_f=fox_b_f, w_out=w_out, ffn2_norm=ffn2_norm, ffn2_w_gu=ffn2_w_gu, ffn2_w_down=ffn2_w_down,
             final_norm=final_norm)
    M = dict(ffn1_norm=m_ffn1_norm, ffn1_w_gu=m_ffn1_w_gu, ffn1_w_down=m_ffn1_w_down, mix_norm=m_mix_norm, w_in=m_w_in,
             q_a_norm=m_q_a_norm, w_q_b=m_w_q_b, kv_a_norm=m_kv_a_norm, w_kv_b=m_w_kv_b, pool_w=m_pool_w,
             pool_scale=m_pool_scale, fox_b_f=m_fox_b_f, w_out=m_w_out, ffn2_norm=m_ffn2_norm, ffn2_w_gu=m_ffn2_w_gu,
             ffn2_w_down=m_ffn2_w_down, final_norm=m_final_norm)
    V = dict(ffn1_norm=v_ffn1_norm, ffn1_w_gu=v_ffn1_w_gu, ffn1_w_down=v_ffn1_w_down, mix_norm=v_mix_norm, w_in=v_w_in,
             q_a_norm=v_q_a_norm, w_q_b=v_w_q_b, kv_a_norm=v_kv_a_norm, w_kv_b=v_w_kv_b, pool_w=v_pool_w,
             pool_scale=v_pool_scale, fox_b_f=v_fox_b_f, w_out=v_w_out, ffn2_norm=v_ffn2_norm, ffn2_w_gu=v_ffn2_w_gu,
             ffn2_w_down=v_ffn2_w_down, final_norm=v_final_norm)
    L, D, n_sh = ffn1_w_gu.shape
    f_sh = ffn1_w_down.shape[1]
    assert n_sh == 2 * f_sh and L == DEPTH
    r_in, r_out = w_in.shape[1], w_out.shape[1]

    tr_in = lambda a: a.transpose(0, 2, 1)
    big_shards = dict(
        ffn1_w_gu=tr_in(ffn1_w_gu).astype(BF16), ffn1_w_down=ffn1_w_down.astype(BF16),
        w_in=_pad_w_in(w_in).astype(BF16), w_small=_small_pack(w_q_b, w_kv_b).astype(BF16), w_out=w_out.astype(BF16),
        ffn2_w_gu=tr_in(ffn2_w_gu).astype(BF16), ffn2_w_down=ffn2_w_down.astype(BF16))
    ex = _Exchange({(k, l): big_shards[k][l] for k in _BIG for l in range(L)}, D, f_sh, r_in, r_out)

    small = {k: W[k] for k in _SMALL}
    loss, dx, grads, d_final = _local_step(x[0], loss_target[0], ex, small)

    small_g = {k: jnp.stack([grads[l][k] for l in range(L)]) for k in _SMALL if k != "final_norm"}
    small_g["final_norm"] = d_final
    pack_g, recipe = _pack_small(small_g)
    n_small = pack_g.shape[0]
    loss_row = -(-n_small // 8) * 8
    pack_g = jnp.concatenate([pack_g, jnp.zeros((loss_row - n_small, LANES), F32), jnp.broadcast_to(loss, (8, LANES))],
                             axis=0)
    *got, packs = _comm_call(ex.scatter_rider("last", 0, pack=pack_g), "scatter_last")
    ex.scattered("last", 0, got)

    out = {}
    sm_w, sm_m, sm_v = (_small_pack(t["w_q_b"], t["w_kv_b"]) for t in (W, M, V))
    big = [("ffn1_w_gu", tr_in(W["ffn1_w_gu"]), tr_in(M["ffn1_w_gu"]), tr_in(V["ffn1_w_gu"]), f_sh),
           ("ffn1_w_down", W["ffn1_w_down"], M["ffn1_w_down"], V["ffn1_w_down"], 352),
           ("w_in", W["w_in"], M["w_in"], V["w_in"], 128),
           ("w_small", sm_w, sm_m, sm_v, 384),
           ("w_out", W["w_out"], M["w_out"], V["w_out"], 128),
           ("ffn2_w_gu", tr_in(W["ffn2_w_gu"]), tr_in(M["ffn2_w_gu"]), tr_in(V["ffn2_w_gu"]), f_sh),
           ("ffn2_w_down", W["ffn2_w_down"], M["ffn2_w_down"], V["ffn2_w_down"], 352)]
    for k, w_, m_, v_, tr in big:
        res = None
        for l in range(L):
            res = _adam_sum(ex.recv[(k, l)], w_, m_, v_, l, res, tr, f"adam_{k}_l{l}")
        if k == "w_small":
            cq, ckv = w_q_b.shape[2], w_kv_b.shape[2]
            parts = [_small_unpack(r, cq, ckv) for r in res]
            out["w_q_b"] = [p[0] for p in parts]
            out["w_kv_b"] = [p[1] for p in parts]
        elif k.endswith("w_gu"):
            out[k] = [tr_in(r) for r in res]
        else:
            out[k] = res

    pw, _ = _pack_small({k: W[k] for k in _SMALL})
    pm, _ = _pack_small({k: M[k] for k in _SMALL})
    pv, _ = _pack_small({k: V[k] for k in _SMALL})
    extra = ((0, loss_row + 8 - n_small), (0, 0))
    res = _adam_small(packs, jnp.pad(pw, extra), jnp.pad(pm, extra), jnp.pad(pv, extra), "adam_small")
    loss_total = res[0][loss_row, 0]
    small_out = [_unpack_small(r, recipe) for r in res]
    for k in _SMALL:
        out[k] = [t[k] for t in small_out]

    names = ["ffn1_norm", "ffn1_w_gu", "ffn1_w_down", "mix_norm", "w_in", "q_a_norm", "w_q_b", "kv_a_norm", "w_kv_b",
             "pool_w", "pool_scale", "fox_b_f", "w_out", "ffn2_norm", "ffn2_w_gu", "ffn2_w_down", "final_norm"]
    outs = [loss_total, dx[None]]
    for which in range(4):
        outs += [out[k][which] for k in names]
    return tuple(outs)
```

```python
import functools
import math

import numpy as np
import jax
import jax.numpy as jnp
from jax import lax
from jax.experimental import pallas as pl
from jax.experimental.pallas import tpu as pltpu

F32 = jnp.float32
BF16 = jnp.bfloat16
MESH_ID = pl.DeviceIdType.MESH

N_DEV = 8
EPS = 1e-6
DEPTH = 2

MLA_HEADS = 6
MLA_Q_RANK = 256
MLA_KV_RANK = 128
MLA_NOPE = 64
MLA_ROPE = 32
MLA_V = 64
ROPE_THETA = 10000.0
POOL_WINDOWS = (2, 4, 8, 16)
POOL_GROUP = 64
POOL_WIDTH = 256
FOX_HEADS = 6
FOX_HEAD_DIM = 64
N_IN = 1830

ADAM_LR = 0.001
ADAM_B1 = 0.9
ADAM_B2 = 0.999
ADAM_EPS = 1e-08
ADAM_WD = 0.01
ADAM_STEP = 10

LANES = 128
HEAD_BLOCK = 128
VMEM_LIMIT = 48 * 1024 * 1024
VMEM_LIMIT_BIG = 50 * 1024 * 1024
NEG = -1e30
ATTN_BLOCK = 1024

ZA = 768
ZF = 1152
N_PAD = ZA + ZF
TAIL0 = 640
ROPE_LANE0 = 64


def _f_lane(h):
    return 8 * (h // 2) + (h % 2)


def _in_perm():
    perm = -np.ones(N_PAD, np.int32)
    perm[0:256] = np.arange(0, 256)
    perm[256:384] = np.arange(256, 384)
    perm[384:640] = np.arange(416, 672)
    for h in range(FOX_HEADS):
        perm[TAIL0 + _f_lane(h)] = 1824 + h
    perm[TAIL0 + ROPE_LANE0:TAIL0 + ROPE_LANE0 + MLA_ROPE] = np.arange(384, 416)
    perm[ZA:N_PAD] = np.arange(672, 1824)
    inv = np.zeros(N_IN, np.int32)
    for new, old in enumerate(perm):
        if old >= 0:
            inv[old] = new
    return perm, inv


_IN_PERM, _IN_INV = _in_perm()
_F_LANES = np.array([_f_lane(h) for h in range(FOX_HEADS)], np.int32)


def _dot(a, b):
    return jnp.dot(a, b, preferred_element_type=F32)


def _dot_nt(a, b):
    return lax.dot_general(a, b, (((1,), (1,)), ((), ())), preferred_element_type=F32)


def _dot_tn(a, b):
    return lax.dot_general(a, b, (((0,), (0,)), ((), ())), preferred_element_type=F32)


def _rms(x, gam):
    r = lax.rsqrt(jnp.mean(x * x, axis=-1, keepdims=True) + EPS)
    return x * r * gam


def _rms_bwd(dy, x, gam):
    r = lax.rsqrt(jnp.mean(x * x, axis=-1, keepdims=True) + EPS)
    xh = x * r
    dxh = dy * gam
    dx = r * (dxh - xh * jnp.mean(dxh * xh, axis=-1, keepdims=True))
    return dx, jnp.sum(dy * xh, axis=0, keepdims=True)


def _accum_out(ref, first, val):
    @pl.when(first)
    def _():
        ref[...] = val

    @pl.when(jnp.logical_not(first))
    def _():
        ref[...] += val


def _bs(shape, fn):
    return pl.BlockSpec(shape, fn)


def _params(dims, vmem=VMEM_LIMIT):
    return pltpu.CompilerParams(dimension_semantics=dims, vmem_limit_bytes=vmem)


def _tile(n, t):
    t = min(n, t)
    assert n % t == 0, (n, t)
    return t


HBM_SPEC = pl.BlockSpec(memory_space=pl.ANY)


def _call(body, *, name, grid, in_specs, out_specs, out_shape, scratch, dims, args, rider=None, vmem=VMEM_LIMIT):
    n_in, n_out = len(in_specs), len(out_specs)
    if rider is None:
        outs = pl.pallas_call(body, name=name, grid=grid, in_specs=in_specs, out_specs=out_specs, out_shape=out_shape,
                              scratch_shapes=scratch, compiler_params=_params(dims, vmem))(*args)
        return list(outs), []
    k_in, k_out, k_sem = len(rider.srcs), len(rider.out_shapes), len(rider.scratch)

    def riding(*refs):
        a, b, c, d = n_in, n_in + k_in, n_in + k_in + n_out, n_in + k_in + n_out + k_out
        rest = refs[d:]
        sems = rest[len(rest) - k_sem:]
        step = 0
        for i, g in enumerate(grid):
            step = step * g + pl.program_id(i)
        n_steps = math.prod(grid)

        @pl.when(step == 0)
        def _():
            rider.begin(refs[a:b], refs[c:d], sems)

        body(*refs[:a], *refs[b:c], *rest[:len(rest) - k_sem])

        @pl.when(step == (3 * n_steps) // 4)
        def _():
            rider.middle(refs[a:b], refs[c:d], sems)

        @pl.when(step == n_steps - 1)
        def _():
            rider.end(refs[a:b], refs[c:d], sems)

    outs = pl.pallas_call(
        riding, name=name, grid=grid, in_specs=list(in_specs) + [HBM_SPEC] * k_in,
        out_specs=list(out_specs) + [HBM_SPEC] * k_out, out_shape=list(out_shape) + list(rider.out_shapes),
        scratch_shapes=list(scratch) + list(rider.scratch),
        compiler_params=_params(("arbitrary",) * len(grid), vmem))(*args, *rider.srcs)
    return list(outs[:n_out]), list(outs[n_out:])


_ONCE = pl.Buffered(1)
_FF_CHUNKS = ((0, 1536), (1536, 2816))


def _ffn_fwd_full(x, gam, wgut, wd, tm, name, rider=None):
    S, D = x.shape
    F = wd.shape[0]
    tm = _tile(S, tm)
    chunks = _FF_CHUNKS if F == 2816 else ((0, F),)

    def body(x_ref, gam_ref, wgut_ref, wd_ref, xo_ref, h_ref, gu_ref):
        h = _rms(x_ref[...], gam_ref[...]).astype(BF16)
        h_ref[...] = h
        y = None
        for c0, c1 in chunks:
            g = _dot_nt(h, wgut_ref[c0:c1, :])
            u = _dot_nt(h, wgut_ref[F + c0:F + c1, :])
            gu_ref[:, c0:c1] = g.astype(BF16)
            gu_ref[:, F + c0:F + c1] = u.astype(BF16)
            a = (g * jax.nn.sigmoid(g) * u).astype(BF16)
            part = _dot(a, wd_ref[c0:c1, :])
            y = part if y is None else y + part
        xo_ref[...] = x_ref[...] + 0.5 * y

    row = lambda i: (i, 0)
    fix = lambda i: (0, 0)
    return _call(
        body, name=name, grid=(S // tm,),
        in_specs=[_bs((tm, D), row), _bs((1, D), fix), pl.BlockSpec((2 * F, D), fix, pipeline_mode=_ONCE),
                  pl.BlockSpec((F, D), fix, pipeline_mode=_ONCE)],
        out_specs=[_bs((tm, D), row), _bs((tm, D), row), _bs((tm, 2 * F), row)],
        out_shape=[jax.ShapeDtypeStruct((S, D), F32), jax.ShapeDtypeStruct((S, D), BF16),
                   jax.ShapeDtypeStruct((S, 2 * F), BF16)],
        scratch=[], dims=("parallel",), args=(x, gam, wgut, wd), rider=rider)


def _ffn_bwd_full(dy, x, gam, gu, wgut, wd, tm, name, phase="all", dgu_in=None, rider=None):
    S, D = dy.shape
    F = wd.shape[0] if wd is not None else wgut.shape[0] // 2
    tm = _tile(S, tm)
    chunks = _FF_CHUNKS if F == 2816 else ((0, F),)
    act, inp = phase in ("all", "act"), phase in ("all", "in")

    def body(*refs):
        refs = list(refs)
        dy_ref = refs.pop(0)
        x_ref, gam_ref = (refs.pop(0), refs.pop(0)) if inp else (None, None)
        gu_ref = refs.pop(0)
        wgut_ref = refs.pop(0) if inp else None
        wd_ref = refs.pop(0) if act else None
        if inp:
            dx_ref, dgam_ref = refs.pop(0), refs.pop(0)
        if act:
            dgu_ref, a_ref, dyh_ref = refs.pop(0), refs.pop(0), refs.pop(0)
            dyh = (0.5 * dy_ref[...]).astype(BF16)
            dyh_ref[...] = dyh
        dh = None
        for c0, c1 in chunks:
            if act:
                dg, du, a = _swiglu_bwd(_dot_nt(dyh, wd_ref[c0:c1, :]), gu_ref[:, c0:c1], gu_ref[:, F + c0:F + c1])
                dgu_ref[:, c0:c1] = dg
                dgu_ref[:, F + c0:F + c1] = du
                a_ref[:, c0:c1] = a
            else:
                dg, du = gu_ref[:, c0:c1], gu_ref[:, F + c0:F + c1]
            if inp:
                part = _dot(dg, wgut_ref[c0:c1, :]) + _dot(du, wgut_ref[F + c0:F + c1, :])
                dh = part if dh is None else dh + part
        if inp:
            dxn, dgam = _rms_bwd(dh, x_ref[...], gam_ref[...])
            dx_ref[...] = dy_ref[...] + dxn
            _accum_out(dgam_ref, pl.program_id(0) == 0, dgam)

    row = lambda i: (i, 0)
    fix = lambda i: (0, 0)
    in_specs, args = [_bs((tm, D), row)], [dy]
    if inp:
        in_specs += [_bs((tm, D), row), _bs((1, D), fix)]
        args += [x, gam]
    in_specs += [_bs((tm, 2 * F), row)]
    args += [gu if act else dgu_in]
    if inp:
        in_specs += [pl.BlockSpec((2 * F, D), fix, pipeline_mode=_ONCE)]
        args += [wgut]
    if act:
        in_specs += [pl.BlockSpec((F, D), fix, pipeline_mode=_ONCE)]
        args += [wd]
    out_specs, out_shape = [], []
    if inp:
        out_specs += [_bs((tm, D), row), _bs((1, D), fix)]
        out_shape += [jax.ShapeDtypeStruct((S, D), F32), jax.ShapeDtypeStruct((1, D), F32)]
    if act:
        out_specs += [_bs((tm, 2 * F), row), _bs((tm, F), row), _bs((tm, D), row)]
        out_shape += [jax.ShapeDtypeStruct((S, 2 * F), BF16), jax.ShapeDtypeStruct((S, F), BF16),
                      jax.ShapeDtypeStruct((S, D), BF16)]
    return _call(body, name=name, grid=(S // tm,), in_specs=in_specs, out_specs=out_specs, out_shape=out_shape,
                 scratch=[], dims=("arbitrary",), args=tuple(args), rider=rider, vmem=VMEM_LIMIT_BIG)


def _swiglu_bwd(da, g, u):
    g = g.astype(F32)
    u = u.astype(F32)
    sig = jax.nn.sigmoid(g)
    sl = g * sig
    dg = (da * u * (sig * (1.0 + g * (1.0 - sig)))).astype(BF16)
    return dg, (da * sl).astype(BF16), (sl * u).astype(BF16)


def _mm_tn(a, b, nb, a_of, b_of, tm, tn, name, rider=None, ts=512):
    _, S, M = a.shape
    N = b.shape[2]
    tm = _tile(M, tm)
    tn = _tile(N, tn)
    ts = _tile(S, ts)
    nk = S // ts

    def body(a_ref, b_ref, o_ref, acc):
        k = pl.program_id(3)

        @pl.when(k == 0)
        def _():
            acc[...] = jnp.zeros_like(acc)

        acc[...] += _dot_tn(a_ref[...].astype(BF16), b_ref[...].astype(BF16))

        @pl.when(k == nk - 1)
        def _():
            o_ref[...] = acc[...].astype(o_ref.dtype)

    (out,), extra = _call(
        body, name=name, grid=(nb, M // tm, N // tn, nk),
        in_specs=[_bs((None, ts, tm), lambda p, i, j, k: (a_of(p), k, i)),
                  _bs((None, ts, tn), lambda p, i, j, k: (b_of(p), k, j))],
        out_specs=[_bs((None, tm, tn), lambda p, i, j, k: (p, i, j))],
        out_shape=[jax.ShapeDtypeStruct((nb, M, N), BF16)],
        scratch=[pltpu.VMEM((tm, tn), F32)],
        dims=("parallel", "parallel", "parallel", "arbitrary"), args=(a, b), rider=rider)
    return (out, extra) if rider is not None else out


FFN_FWD_TOKENS = 512
FFN_BWD_TOKENS = 256
FFN_GRAD_ROWS = 1408


def _ffn_weight_grad(a, b, name, rider=None):
    M = a.shape[1]
    tm = max(t for t in range(LANES, FFN_GRAD_ROWS + 1, LANES) if M % t == 0)
    res = _mm_tn(a[None], b[None], 1, lambda p: 0, lambda p: 0, tm, b.shape[1], name, rider=rider, ts=2048)
    return (res[0][0], res[1]) if rider is not None else res[0]


def _mix_in_fwd(x, gam, w_in, name):
    S, D = x.shape
    tm = _tile(S, 512)
    nkv = (ZF - 384) // LANES

    def body(x_ref, gam_ref, w_ref, h_ref, za_ref, zf_ref, zt_ref):
        hb = _rms(x_ref[...], gam_ref[...]).astype(BF16)
        h_ref[...] = hb
        za_ref[...] = _dot(hb, w_ref[:, 0:ZA])
        zf = _dot(hb, w_ref[:, ZA:N_PAD])
        zf_ref[...] = zf.astype(BF16)
        for c in range(nkv):
            zt_ref[c * LANES:(c + 1) * LANES, :] = zf[:, 384 + c * LANES:384 + (c + 1) * LANES].T.astype(BF16)

    return pl.pallas_call(
        body, name=name, grid=(S // tm,),
        in_specs=[_bs((tm, D), lambda i: (i, 0)), _bs((1, D), lambda i: (0, 0)), _bs((D, N_PAD), lambda i: (0, 0))],
        out_specs=[_bs((tm, D), lambda i: (i, 0)), _bs((tm, ZA), lambda i: (i, 0)), _bs((tm, ZF), lambda i: (i, 0)),
                   _bs((nkv * LANES, tm), lambda i: (0, i))],
        out_shape=[jax.ShapeDtypeStruct((S, D), BF16), jax.ShapeDtypeStruct((S, ZA), F32),
                   jax.ShapeDtypeStruct((S, ZF), BF16), jax.ShapeDtypeStruct((nkv * LANES, S), BF16)],
        compiler_params=_params(("parallel",)),
    )(x, gam, w_in)


def _mix_in_bwd(dy, x, gam, dza_mla, du, dtail_f, dqf, dkf, dvf, w_in, name):
    S, D = x.shape
    tm = _tile(S, 512)

    def body(dy_ref, x_ref, gam_ref, dza_ref, du_ref, dt_ref, dq_ref, dk_ref, dv_ref, w_ref, dx_ref, dgam_ref, dz_ref):
        i = pl.program_id(0)
        dza = dza_ref[...]
        dz = jnp.concatenate([dza[:, 0:384], du_ref[...], dza[:, TAIL0:ZA] + dt_ref[...],
                              dq_ref[...], dk_ref[...], dv_ref[...]], axis=1).astype(BF16)
        dz_ref[...] = dz
        dh = _dot_nt(dz, w_ref[...])
        dxn, dgam = _rms_bwd(dh, x_ref[...], gam_ref[...])
        dx_ref[...] = dy_ref[...] + dxn
        _accum_out(dgam_ref, i == 0, dgam)

    row = lambda i: (i, 0)
    fix = lambda i: (0, 0)
    return pl.pallas_call(
        body, name=name, grid=(S // tm,),
        in_specs=[_bs((tm, D), row), _bs((tm, D), row), _bs((1, D), fix), _bs((tm, ZA), row), _bs((tm, 256), row),
                  _bs((tm, 128), row), _bs((tm, 384), row), _bs((tm, 384), row), _bs((tm, 384), row),
                  _bs((D, N_PAD), fix)],
        out_specs=[_bs((tm, D), row), _bs((1, D), fix), _bs((tm, N_PAD), row)],
        out_shape=[jax.ShapeDtypeStruct((S, D), F32), jax.ShapeDtypeStruct((1, D), F32),
                   jax.ShapeDtypeStruct((S, N_PAD), BF16)],
        compiler_params=_params(("arbitrary",)),
    )(dy, x, gam, dza_mla, du, dtail_f, dqf, dkf, dvf, w_in)


def _rope_tables(S):
    half = MLA_ROPE // 2
    inv_freq = ROPE_THETA ** (-jnp.arange(0, MLA_ROPE, 2, dtype=F32) / MLA_ROPE)
    ang = jnp.arange(S, dtype=jnp.int32).astype(F32)[:, None] * inv_freq[None, :]
    cos, sin = jnp.cos(ang), jnp.sin(ang)
    one = jnp.ones((S, ROPE_LANE0), F32)
    zero = jnp.zeros((S, ROPE_LANE0), F32)
    pad1 = jnp.ones((S, LANES - ROPE_LANE0 - MLA_ROPE), F32)
    pad0 = jnp.zeros((S, LANES - ROPE_LANE0 - MLA_ROPE), F32)
    zh = jnp.zeros((S, half), F32)
    tab_c = jnp.concatenate([one, cos, cos, pad1], axis=1)
    tab_ck = jnp.concatenate([zero, cos, cos, pad0], axis=1)
    tab_s1 = jnp.concatenate([zero, -sin, zh, pad0], axis=1)
    tab_s2 = jnp.concatenate([zero, zh, sin, pad0], axis=1)
    return tab_c, tab_ck, tab_s1, tab_s2


def _rope(x, c, s1, s2):
    return x * c + pltpu.roll(x, LANES - 16, 1) * s1 + pltpu.roll(x, 16, 1) * s2


def _rope_t(dy, c, s1, s2):
    return dy * c + pltpu.roll(dy * s1, 16, 1) + pltpu.roll(dy * s2, LANES - 16, 1)


_MLA_SCALE = 1.0 / math.sqrt(MLA_NOPE + MLA_ROPE)


def _mla_prep(za, gq, gkv, wq, wkv, tabs, name):
    S = za.shape[0]
    tm = _tile(S, 512)
    H = MLA_HEADS

    def body(zq_ref, tail_ref, gq_ref, gkv_ref, wq_ref, wkv_ref, c_ref, ck_ref, s1_ref, s2_ref,
             qf_ref, kf_ref, v_ref, kft_ref, vt_ref):
        zq = zq_ref[...]
        c, s1, s2 = c_ref[...], s1_ref[...], s2_ref[...]
        qn = _rms(zq[:, 0:256], gq_ref[...]).astype(BF16)
        q = _dot(qn, wq_ref[...])
        for h in range(H):
            blk = _rope(q[:, h * LANES:(h + 1) * LANES], c, s1, s2)
            qf_ref[:, h * LANES:(h + 1) * LANES] = (blk * _MLA_SCALE).astype(BF16)
        kvn = _rms(zq[:, 256:384], gkv_ref[...]).astype(BF16)
        kv = _dot(kvn, wkv_ref[...])
        kt = _rope(tail_ref[...], ck_ref[...], s1, s2)
        for h in range(H):
            sl = slice(h * LANES, (h + 1) * LANES)
            kblk = kv[:, sl] + kt
            kf_ref[:, sl] = kblk.astype(BF16)
            kft_ref[sl, :] = kblk.T.astype(BF16)
        v_ref[...] = kv[:, H * LANES:].astype(BF16)
        for cblk in range(H * MLA_V // LANES):
            sl = slice(cblk * LANES, (cblk + 1) * LANES)
            vt_ref[sl, :] = kv[:, H * LANES + cblk * LANES:H * LANES + (cblk + 1) * LANES].T.astype(BF16)

    row = lambda i: (i, 0)
    col = lambda i: (0, i)
    fix = lambda i: (0, 0)
    return pl.pallas_call(
        body, name=name, grid=(S // tm,),
        in_specs=[_bs((tm, 384), row), _bs((tm, 128), lambda i: (i, TAIL0 // 128)), _bs((1, 256), fix), _bs((1, 128), fix),
                  _bs((256, 768), fix), _bs((128, 1152), fix),
                  _bs((tm, 128), row), _bs((tm, 128), row), _bs((tm, 128), row), _bs((tm, 128), row)],
        out_specs=[_bs((tm, 768), row), _bs((tm, 768), row), _bs((tm, 384), row), _bs((768, tm), col), _bs((384, tm), col)],
        out_shape=[jax.ShapeDtypeStruct((S, 768), BF16), jax.ShapeDtypeStruct((S, 768), BF16),
                   jax.ShapeDtypeStruct((S, 384), BF16), jax.ShapeDtypeStruct((768, S), BF16),
                   jax.ShapeDtypeStruct((384, S), BF16)],
        compiler_params=_params(("parallel",)),
    )(za, za, gq, gkv, wq, wkv, *tabs)


def _mla_prep_bwd(za, gq, gkv, wq, wkv, tabs, dqf, dkf, dvm, name):
    S = za.shape[0]
    tm = _tile(S, 512)
    H = MLA_HEADS

    def body(zq_ref, gq_ref, gkv_ref, wq_ref, wkv_ref, c_ref, ck_ref, s1_ref, s2_ref, dqf_ref, dkf_ref, dvm_ref,
             dza_ref, dwq_ref, dwkv_ref, dgq_ref, dgkv_ref):
        i = pl.program_id(0)
        first = i == 0
        zq = zq_ref[...]
        c, s1, s2 = c_ref[...], s1_ref[...], s2_ref[...]
        lane = lax.broadcasted_iota(jnp.int32, (1, LANES), 1)
        nope = lane < MLA_NOPE
        rope = jnp.logical_and(lane >= ROPE_LANE0, lane < ROPE_LANE0 + MLA_ROPE)

        qa = zq[:, 0:256]
        qn = _rms(qa, gq_ref[...]).astype(BF16)
        dqf = dqf_ref[...]
        dq_pre = jnp.concatenate(
            [_rope_t(dqf[:, h * LANES:(h + 1) * LANES] * _MLA_SCALE, c, s1, s2) for h in range(H)], axis=1).astype(BF16)
        _accum_out(dwq_ref, first, _dot_tn(qn, dq_pre))
        dqa, dgq = _rms_bwd(_dot_nt(dq_pre, wq_ref[...]), qa, gq_ref[...])
        _accum_out(dgq_ref, first, dgq)

        kva = zq[:, 256:384]
        kvn = _rms(kva, gkv_ref[...]).astype(BF16)
        dkf = dkf_ref[...]
        parts = []
        dkt = jnp.zeros((tm, LANES), F32)
        for h in range(H):
            blk = dkf[:, h * LANES:(h + 1) * LANES]
            parts.append(jnp.where(nope, blk, 0.0))
            dkt = dkt + jnp.where(rope, blk, 0.0)
        dkv_pre = jnp.concatenate(parts + [dvm_ref[...]], axis=1).astype(BF16)
        _accum_out(dwkv_ref, first, _dot_tn(kvn, dkv_pre))
        dkva, dgkv = _rms_bwd(_dot_nt(dkv_pre, wkv_ref[...]), kva, gkv_ref[...])
        _accum_out(dgkv_ref, first, dgkv)

        dtail = _rope_t(dkt, ck_ref[...], s1, s2)
        dza_ref[...] = jnp.concatenate([dqa, dkva, jnp.zeros((tm, 256), F32), dtail], axis=1)

    row = lambda i: (i, 0)
    fix = lambda i: (0, 0)
    return pl.pallas_call(
        body, name=name, grid=(S // tm,),
        in_specs=[_bs((tm, 384), row), _bs((1, 256), fix), _bs((1, 128), fix), _bs((256, 768), fix), _bs((128, 1152), fix),
                  _bs((tm, 128), row), _bs((tm, 128), row), _bs((tm, 128), row), _bs((tm, 128), row),
                  _bs((tm, 768), row), _bs((tm, 768), row), _bs((tm, 384), row)],
        out_specs=[_bs((tm, ZA), row), _bs((256, 768), fix), _bs((128, 1152), fix), _bs((1, 256), fix), _bs((1, 128), fix)],
        out_shape=[jax.ShapeDtypeStruct((S, ZA), F32), jax.ShapeDtypeStruct((256, 768), F32),
                   jax.ShapeDtypeStruct((128, 1152), F32), jax.ShapeDtypeStruct((1, 256), F32),
                   jax.ShapeDtypeStruct((1, 128), F32)],
        compiler_params=_params(("arbitrary",)),
    )(za, gq, gkv, wq, wkv, *tabs, dqf, dkf, dvm)


def _head_views(qb, kb, r, dkb, sel):
    if dkb == LANES:
        sl = slice(r * LANES, (r + 1) * LANES)
        return qb[:, sl], kb[:, sl], kb[:, sl]
    return jnp.where(sel, qb, jnp.zeros_like(qb)), kb, jnp.where(sel, kb, jnp.zeros_like(kb))


def _attn_fwd_t(q_arr, k_arr, vt_arr, *, nhp, dkb, qoff, koff, vtoff, scale, cum, cumT, name, rider=None):
    S = q_arr.shape[0]
    T = _tile(S, ATTN_BLOCK)
    nq = S // T
    W = 2 * dkb
    bias = cum is not None

    def body(*refs):
        if bias:
            q_ref, k_ref, vt_ref, cq_ref, ck_ref, o_ref, lse_ref, m_s, l_s, acc_s = refs
        else:
            q_ref, k_ref, vt_ref, o_ref, lse_ref, m_s, l_s, acc_s = refs
        hp, qi, ki = pl.program_id(0), pl.program_id(1), pl.program_id(2)
        lo_lane = lax.broadcasted_iota(jnp.int32, (1, LANES), 1) < 64
        lo_row = lax.broadcasted_iota(jnp.int32, (LANES, 1), 0) < 64

        @pl.when(ki == 0)
        def _():
            m_s[...] = jnp.full_like(m_s, NEG)
            l_s[...] = jnp.zeros_like(l_s)
            acc_s[...] = jnp.zeros_like(acc_s)

        def step(masked):
            qb, kb, vtb = q_ref[...], k_ref[...], vt_ref[...]
            if masked:
                mask = lax.broadcasted_iota(jnp.int32, (T, T), 0) <= lax.broadcasted_iota(jnp.int32, (T, T), 1)
            if bias:
                li = lax.broadcasted_iota(jnp.int32, (T, LANES), 1)
                ckb = ck_ref[...]
            for r in range(2):
                sel = lo_lane if r == 0 else jnp.logical_not(lo_lane)
                rsel = lo_row if r == 0 else jnp.logical_not(lo_row)
                q, k, _ = _head_views(qb, kb, r, dkb, sel)
                if scale != 1.0:
                    q = q * jnp.asarray(scale, q.dtype)
                s = _dot_nt(k, q)
                if bias:
                    ck = jnp.sum(jnp.where(li == 8 * hp + r, ckb, 0.0), axis=1, keepdims=True)
                    s = s + (cq_ref[r:r + 1, :] - ck)
                if masked:
                    s = jnp.where(mask, s, NEG)
                m_prev = m_s[r:r + 1, :]
                m_new = jnp.maximum(m_prev, jnp.max(s, axis=0, keepdims=True))
                alpha = jnp.exp(m_prev - m_new)
                p = jnp.exp(s - m_new)
                l_s[r:r + 1, :] = alpha * l_s[r:r + 1, :] + jnp.sum(p, axis=0, keepdims=True)
                m_s[r:r + 1, :] = m_new
                pv = _dot(jnp.where(rsel, vtb, jnp.zeros_like(vtb)), p.astype(BF16))
                acc_s[...] = acc_s[...] * jnp.where(rsel, alpha, 1.0) + pv

        @pl.when(ki < qi)
        def _():
            step(False)

        @pl.when(ki == qi)
        def _():
            step(True)

        @pl.when(ki == nq - 1)
        def _():
            inv = jnp.where(lo_row, 1.0 / l_s[0:1, :], 1.0 / l_s[1:2, :])
            o_ref[...] = (acc_s[...] * inv).T.astype(BF16)
            used = lax.broadcasted_iota(jnp.int32, (8, T), 0) < 2
            lse_ref[...] = jnp.where(used, m_s[...] + jnp.log(jnp.where(used, l_s[...], 1.0)), 0.0)

    kmap = lambda hp, qi, ki: jnp.minimum(ki, qi)
    in_specs = [_bs((T, W), lambda hp, qi, ki: (qi, qoff + hp)),
                _bs((T, W), lambda hp, qi, ki: (kmap(hp, qi, ki), koff + hp)),
                _bs((LANES, T), lambda hp, qi, ki: (vtoff + hp, kmap(hp, qi, ki)))]
    args = [q_arr, k_arr, vt_arr]
    if bias:
        in_specs += [_bs((8, T), lambda hp, qi, ki: (hp, qi)), _bs((T, LANES), lambda hp, qi, ki: (kmap(hp, qi, ki), 0))]
        args += [cumT, cum]
    return _call(
        body, name=name, grid=(nhp, nq, nq),
        in_specs=in_specs,
        out_specs=[_bs((T, LANES), lambda hp, qi, ki: (qi, hp)), _bs((None, 8, T), lambda hp, qi, ki: (hp, 0, qi))],
        out_shape=[jax.ShapeDtypeStruct((S, nhp * LANES), BF16), jax.ShapeDtypeStruct((nhp, 8, S), F32)],
        scratch=[pltpu.VMEM((8, T), F32), pltpu.VMEM((8, T), F32), pltpu.VMEM((LANES, T), F32)],
        dims=("parallel", "parallel", "arbitrary"), args=args, rider=rider)


def _attn_dvec(o_arr, do_arr, nhp, name):
    S = o_arr.shape[0]
    T = _tile(S, 512)

    def body(o_ref, do_ref, d_ref):
        prod = do_ref[...].astype(F32) * o_ref[...].astype(F32)
        li = lax.broadcasted_iota(jnp.int32, (T, LANES), 1)
        d0 = jnp.sum(jnp.where(li < 64, prod, 0.0), axis=1, keepdims=True)
        d1 = jnp.sum(jnp.where(li >= 64, prod, 0.0), axis=1, keepdims=True)
        d_ref[...] = jnp.where(li == 0, d0, jnp.where(li == 1, d1, 0.0)).T[0:8, :]

    return pl.pallas_call(
        body, name=name, grid=(nhp, S // T),
        in_specs=[_bs((T, LANES), lambda hp, i: (i, hp)), _bs((T, LANES), lambda hp, i: (i, hp))],
        out_specs=_bs((None, 8, T), lambda hp, i: (hp, 0, i)),
        out_shape=jax.ShapeDtypeStruct((nhp, 8, S), F32),
        compiler_params=_params(("parallel", "parallel")),
    )(o_arr, do_arr)


def _attn_bwd_t(q_arr, k_arr, kt_arr, v_arr, do_arr, lse, dvec, *, nhp, dkb, qoff, koff, ktoff, voff, scale, cum, cumT,
                name, rider=None):
    S = q_arr.shape[0]
    T = _tile(S, ATTN_BLOCK)
    nq = S // T
    W = 2 * dkb
    bias = cum is not None

    def body(*refs):
        if bias:
            (q_ref, k_ref, kt_ref, v_ref, do_ref, lse_ref, dvec_ref, cq_ref, ck_ref,
             dq_ref, dk_ref, dv_ref, dcq_ref, dck_ref, dqt_s, dk_s, dv_s, dcq_s, dck_s) = refs
        else:
            (q_ref, k_ref, kt_ref, v_ref, do_ref, lse_ref, dvec_ref,
             dq_ref, dk_ref, dv_ref, dqt_s, dk_s, dv_s) = refs
        hp, ki, qi = pl.program_id(0), pl.program_id(1), pl.program_id(2)
        lo_lane = lax.broadcasted_iota(jnp.int32, (1, LANES), 1) < 64
        lo_row = lax.broadcasted_iota(jnp.int32, (LANES, 1), 0) < 64

        @pl.when(jnp.logical_and(ki == 0, qi == 0))
        def _():
            dqt_s[...] = jnp.zeros_like(dqt_s)
            if bias:
                dcq_s[...] = jnp.zeros_like(dcq_s)

        @pl.when(qi == 0)
        def _():
            dk_s[...] = jnp.zeros_like(dk_s)
            dv_s[...] = jnp.zeros_like(dv_s)
            if bias:
                dck_s[...] = jnp.zeros_like(dck_s)

        def step(masked):
            qb, kb, ktb, vb, dob = q_ref[...], k_ref[...], kt_ref[...], v_ref[...], do_ref[...]
            if masked:
                mask = lax.broadcasted_iota(jnp.int32, (T, T), 0) <= lax.broadcasted_iota(jnp.int32, (T, T), 1)
            if bias:
                li = lax.broadcasted_iota(jnp.int32, (T, LANES), 1)
                ckb = ck_ref[...]
            for r in range(2):
                sel = lo_lane if r == 0 else jnp.logical_not(lo_lane)
                rsel = lo_row if r == 0 else jnp.logical_not(lo_row)
                q, k, _ = _head_views(qb, kb, r, dkb, sel)
                if scale != 1.0:
                    q = q * jnp.asarray(scale, q.dtype)
                s = _dot_nt(k, q)
                if bias:
                    ck = jnp.sum(jnp.where(li == 8 * hp + r, ckb, 0.0), axis=1, keepdims=True)
                    s = s + (cq_ref[r:r + 1, :] - ck)
                p = jnp.exp(s - lse_ref[r:r + 1, :])
                if masked:
                    p = jnp.where(mask, p, 0.0)
                do_r = jnp.where(sel, dob, jnp.zeros_like(dob))
                dp = _dot_nt(vb, do_r)
                ds = p * (dp - dvec_ref[r:r + 1, :])
                pb = p.astype(BF16)
                dsb = ds.astype(BF16)
                dv_s[...] += _dot(pb, do_r)
                if dkb == LANES:
                    sl = slice(r * LANES, (r + 1) * LANES)
                    dk_s[:, sl] += _dot(dsb, q)
                    dqt_s[qi, sl, :] += _dot(ktb[sl, :], dsb) * scale
                else:
                    dk_s[...] += _dot(dsb, q)
                    dqt_s[qi] += _dot(jnp.where(rsel, ktb, jnp.zeros_like(ktb)), dsb) * scale
                if bias:
                    dcq_s[qi, r:r + 1, :] += jnp.sum(ds, axis=0, keepdims=True)
                    dck_s[...] -= jnp.where(li == 8 * hp + r, jnp.sum(ds, axis=1, keepdims=True), 0.0)

        @pl.when(qi > ki)
        def _():
            step(False)

        @pl.when(qi == ki)
        def _():
            step(True)

        @pl.when(qi == nq - 1)
        def _():
            dk_ref[...] = dk_s[...]
            dv_ref[...] = dv_s[...]
            if bias:
                dck_ref[...] = dck_s[...]

        @pl.when(jnp.logical_and(ki == nq - 1, qi == nq - 1))
        def _():
            for c in range(nq):
                dq_ref[c * T:(c + 1) * T, :] = dqt_s[c].T
                if bias:
                    dcq_ref[:, c * T:(c + 1) * T] = dcq_s[c]

    qmap = lambda hp, ki, qi: jnp.maximum(qi, ki)
    in_specs = [_bs((T, W), lambda hp, ki, qi: (qmap(hp, ki, qi), qoff + hp)),
                _bs((T, W), lambda hp, ki, qi: (ki, koff + hp)),
                _bs((W, T), lambda hp, ki, qi: (ktoff + hp, ki)),
                _bs((T, LANES), lambda hp, ki, qi: (ki, voff + hp)),
                _bs((T, LANES), lambda hp, ki, qi: (qmap(hp, ki, qi), hp)),
                _bs((None, 8, T), lambda hp, ki, qi: (hp, 0, qmap(hp, ki, qi))),
                _bs((None, 8, T), lambda hp, ki, qi: (hp, 0, qmap(hp, ki, qi)))]
    args = [q_arr, k_arr, kt_arr, v_arr, do_arr, lse, dvec]
    out_specs = [_bs((S, W), lambda hp, ki, qi: (0, hp)), _bs((T, W), lambda hp, ki, qi: (ki, hp)),
                 _bs((T, LANES), lambda hp, ki, qi: (ki, hp))]
    out_shape = [jax.ShapeDtypeStruct((S, nhp * W), F32), jax.ShapeDtypeStruct((S, nhp * W), F32),
                 jax.ShapeDtypeStruct((S, nhp * LANES), F32)]
    scratch = [pltpu.VMEM((nq, W, T), F32), pltpu.VMEM((T, W), F32), pltpu.VMEM((T, LANES), F32)]
    if bias:
        in_specs += [_bs((8, T), lambda hp, ki, qi: (hp, qmap(hp, ki, qi))), _bs((T, LANES), lambda hp, ki, qi: (ki, 0))]
        args += [cumT, cum]
        out_specs += [_bs((None, 8, S), lambda hp, ki, qi: (hp, 0, 0)), _bs((None, T, LANES), lambda hp, ki, qi: (hp, ki, 0))]
        out_shape += [jax.ShapeDtypeStruct((nhp, 8, S), F32), jax.ShapeDtypeStruct((nhp, S, LANES), F32)]
        scratch += [pltpu.VMEM((nq, 8, T), F32), pltpu.VMEM((T, LANES), F32)]
    return _call(body, name=name, grid=(nhp, nq, nq), in_specs=in_specs, out_specs=out_specs, out_shape=out_shape,
                 scratch=scratch, dims=("arbitrary", "arbitrary", "arbitrary"), args=args, rider=rider)


def _gate_lanes(shape):
    lane = lax.broadcasted_iota(jnp.int32, shape, 1)
    return jnp.logical_and(lane < 8 * (FOX_HEADS // 2), lane % 8 < 2)


def _fox_prep(za, b_row, name):
    S = za.shape[0]
    nrow = 8 * (FOX_HEADS // 2)

    def body(tail_ref, b_ref, cum_ref, cumt_ref):
        x = tail_ref[...] + b_ref[...]
        logf = jnp.minimum(x, 0.0) - jnp.log(1.0 + jnp.exp(-jnp.abs(x)))
        y = jnp.where(_gate_lanes((S, LANES)), logf, 0.0)
        row = lax.broadcasted_iota(jnp.int32, (S, LANES), 0)
        k = 1
        while k < S:
            y = y + jnp.where(row >= k, pltpu.roll(y, k, 0), 0.0)
            k *= 2
        cum_ref[...] = y
        cumt_ref[...] = y.T[0:nrow, :]

    return pl.pallas_call(
        body, name=name, grid=(1,),
        in_specs=[_bs((S, LANES), lambda i: (0, TAIL0 // LANES)), _bs((1, LANES), lambda i: (0, 0))],
        out_specs=[_bs((S, LANES), lambda i: (0, 0)), _bs((nrow, S), lambda i: (0, 0))],
        out_shape=[jax.ShapeDtypeStruct((S, LANES), F32), jax.ShapeDtypeStruct((nrow, S), F32)],
        compiler_params=_params(("arbitrary",)),
    )(za, b_row)


def _fox_prep_bwd(za, b_row, dcq, dck, name):
    S = za.shape[0]
    nhp = FOX_HEADS // 2
    nrow = 8 * nhp
    dcq2 = dcq.reshape(nrow, S)

    def body(tail_ref, b_ref, dcq_ref, dck_ref, dt_ref, db_ref):
        x = tail_ref[...] + b_ref[...]
        d = jnp.concatenate([dcq_ref[...], jnp.zeros((LANES - nrow, S), F32)], axis=0).T
        for hp in range(nhp):
            d = d + dck_ref[hp]
        row = lax.broadcasted_iota(jnp.int32, (S, LANES), 0)
        k = 1
        while k < S:
            d = d + jnp.where(row < S - k, pltpu.roll(d, S - k, 0), 0.0)
            k *= 2
        df = jnp.where(_gate_lanes((S, LANES)), d * jax.nn.sigmoid(-x), 0.0)
        dt_ref[...] = df
        db_ref[...] = jnp.sum(df, axis=0, keepdims=True)

    return pl.pallas_call(
        body, name=name, grid=(1,),
        in_specs=[_bs((S, LANES), lambda i: (0, TAIL0 // LANES)), _bs((1, LANES), lambda i: (0, 0)),
                  _bs((nrow, S), lambda i: (0, 0)), _bs((nhp, S, LANES), lambda i: (0, 0, 0))],
        out_specs=[_bs((S, LANES), lambda i: (0, 0)), _bs((1, LANES), lambda i: (0, 0))],
        out_shape=[jax.ShapeDtypeStruct((S, LANES), F32), jax.ShapeDtypeStruct((1, LANES), F32)],
        compiler_params=_params(("arbitrary",)),
    )(za, b_row, dcq2, dck)


def _pool_select(half, lane_lo, vals):
    return jnp.where(lane_lo, jnp.where(half == 0, vals[0], vals[2]), jnp.where(half == 0, vals[1], vals[3]))


def _pool_den(S, half, lane_lo):
    cnt = (lax.broadcasted_iota(jnp.int32, (S, LANES), 0) + 1).astype(F32)
    w = _pool_select(half, lane_lo, [float(x) for x in POOL_WINDOWS])
    return jnp.minimum(cnt, w)


def _pool_fwd(za, wbd, scale, name):
    S = za.shape[0]

    def body(u_ref, w_ref, sc_ref, y_ref, pd_ref):
        half = pl.program_id(0)
        u = u_ref[...]
        row = lax.broadcasted_iota(jnp.int32, (S, LANES), 0)
        lane_lo = lax.broadcasted_iota(jnp.int32, (S, LANES), 1) < POOL_GROUP
        sums = []
        acc = u
        k = 1
        while k < POOL_WINDOWS[-1]:
            acc = acc + jnp.where(row >= k, pltpu.roll(acc, k, 0), 0.0)
            sums.append(acc)
            k *= 2
        pooled = _pool_select(half, lane_lo, sums) / _pool_den(S, half, lane_lo)
        pd = (pooled - u).astype(BF16)
        pd_ref[...] = pd
        y_ref[...] = (_dot(pd, w_ref[...]) * sc_ref[...]).astype(BF16)

    return pl.pallas_call(
        body, name=name, grid=(2,),
        in_specs=[_bs((S, LANES), lambda i: (0, 384 // LANES + i)), _bs((None, LANES, LANES), lambda i: (i, 0, 0)),
                  _bs((1, LANES), lambda i: (0, i))],
        out_specs=[_bs((S, LANES), lambda i: (0, i)), _bs((S, LANES), lambda i: (0, i))],
        out_shape=[jax.ShapeDtypeStruct((S, POOL_WIDTH), BF16), jax.ShapeDtypeStruct((S, POOL_WIDTH), BF16)],
        compiler_params=_params(("parallel",)),
    )(za, wbd, scale)


def _pool_bwd(dyb, pd, wbd, scale, name):
    S = pd.shape[0]

    def body(dy_ref, pd_ref, w_ref, sc_ref, du_ref, dw_ref, dsc_ref):
        half = pl.program_id(0)
        dy = dy_ref[...]
        pd = pd_ref[...]
        w = w_ref[...]
        ypre = _dot(pd, w)
        dsc_ref[...] = jnp.sum(dy * ypre, axis=0, keepdims=True)
        dyp = (dy * sc_ref[...]).astype(BF16)
        dw_ref[...] = _dot_tn(pd, dyp)
        dpd = _dot_nt(dyp, w)
        row = lax.broadcasted_iota(jnp.int32, (S, LANES), 0)
        lane_lo = lax.broadcasted_iota(jnp.int32, (S, LANES), 1) < POOL_GROUP
        acc = dpd / _pool_den(S, half, lane_lo)
        sums = []
        k = 1
        while k < POOL_WINDOWS[-1]:
            acc = acc + jnp.where(row < S - k, pltpu.roll(acc, S - k, 0), 0.0)
            sums.append(acc)
            k *= 2
        du_ref[...] = _pool_select(half, lane_lo, sums) - dpd

    return pl.pallas_call(
        body, name=name, grid=(2,),
        in_specs=[_bs((S, LANES), lambda i: (0, i)), _bs((S, LANES), lambda i: (0, i)),
                  _bs((None, LANES, LANES), lambda i: (i, 0, 0)), _bs((1, LANES), lambda i: (0, i))],
        out_specs=[_bs((S, LANES), lambda i: (0, i)), _bs((None, LANES, LANES), lambda i: (i, 0, 0)),
                   _bs((1, LANES), lambda i: (0, i))],
        out_shape=[jax.ShapeDtypeStruct((S, POOL_WIDTH), F32), jax.ShapeDtypeStruct((2, LANES, LANES), F32),
                   jax.ShapeDtypeStruct((1, POOL_WIDTH), F32)],
        compiler_params=_params(("parallel",)),
    )(dyb, pd, wbd, scale)


def _mix_out_fwd(x, ya, yb, yc, w_out, name):
    S, D = x.shape
    tm = _tile(S, 512)
    K = w_out.shape[0]

    def body(x_ref, ya_ref, yb_ref, yc_ref, w_ref, xo_ref, yc_out):
        ycat = jnp.concatenate([ya_ref[...], yb_ref[...], yc_ref[...]], axis=1)
        yc_out[...] = ycat
        xo_ref[...] = x_ref[...] + _dot(ycat, w_ref[...])

    row = lambda i: (i, 0)
    return pl.pallas_call(
        body, name=name, grid=(S // tm,),
        in_specs=[_bs((tm, D), row), _bs((tm, 384), row), _bs((tm, 256), row), _bs((tm, 384), row),
                  _bs((K, D), lambda i: (0, 0))],
        out_specs=[_bs((tm, D), row), _bs((tm, K), row)],
        out_shape=[jax.ShapeDtypeStruct((S, D), F32), jax.ShapeDtypeStruct((S, K), BF16)],
        compiler_params=_params(("parallel",)),
    )(x, ya, yb, yc, w_out)


def _mix_out_bwd(dy, w_out, name):
    S, D = dy.shape
    tm = _tile(S, 512)
    K = w_out.shape[0]

    def body(dy_ref, w_ref, da_ref, db_ref, dc_ref):
        d = _dot_nt(dy_ref[...].astype(BF16), w_ref[...])
        da_ref[...] = d[:, 0:384].astype(BF16)
        db_ref[...] = d[:, 384:640]
        dc_ref[...] = d[:, 640:1024].astype(BF16)

    row = lambda i: (i, 0)
    return pl.pallas_call(
        body, name=name, grid=(S // tm,),
        in_specs=[_bs((tm, D), row), _bs((K, D), lambda i: (0, 0))],
        out_specs=[_bs((tm, 384), row), _bs((tm, 256), row), _bs((tm, 384), row)],
        out_shape=[jax.ShapeDtypeStruct((S, 384), BF16), jax.ShapeDtypeStruct((S, 256), F32),
                   jax.ShapeDtypeStruct((S, 384), BF16)],
        compiler_params=_params(("parallel",)),
    )(dy, w_out)


def _loss_head(x, gam, target, name):
    S, D = x.shape
    tm = _tile(S, 512)

    def body(x_ref, gam_ref, t_ref, dx_ref, dgam_ref, loss_ref):
        i = pl.program_id(0)
        xv = x_ref[...]
        err = _rms(xv, gam_ref[...]) - t_ref[...]
        part = 0.5 * jnp.sum(jnp.mean(err * err, axis=-1, keepdims=True), axis=0, keepdims=True)
        dxn, dgam = _rms_bwd(err * (1.0 / D), xv, gam_ref[...])
        dx_ref[...] = dxn
        _accum_out(dgam_ref, i == 0, dgam)
        _accum_out(loss_ref, i == 0, jnp.broadcast_to(part, (1, LANES)))

    row = lambda i: (i, 0)
    fix = lambda i: (0, 0)
    return pl.pallas_call(
        body, name=name, grid=(S // tm,),
        in_specs=[_bs((tm, D), row), _bs((1, D), fix), _bs((tm, D), row)],
        out_specs=[_bs((tm, D), row), _bs((1, D), fix), _bs((1, LANES), fix)],
        out_shape=[jax.ShapeDtypeStruct((S, D), F32), jax.ShapeDtypeStruct((1, D), F32),
                   jax.ShapeDtypeStruct((1, LANES), F32)],
        compiler_params=_params(("arbitrary",)),
    )(x, gam, target)


def _adam_math(g, w, m, v):
    m = ADAM_B1 * m + (1.0 - ADAM_B1) * g
    v = ADAM_B2 * v + (1.0 - ADAM_B2) * (g * g)
    m_hat = m / (1.0 - ADAM_B1 ** ADAM_STEP)
    v_hat = v / (1.0 - ADAM_B2 ** ADAM_STEP)
    delta = -ADAM_LR * (m_hat / (jnp.sqrt(v_hat) + ADAM_EPS) + ADAM_WD * w)
    return delta, m, v


def _adam_sum(recv, w, m, v, layer, prev, tr, name):
    L, R, C = w.shape
    Cp = recv.shape[2]
    tr = _tile(R, tr)

    def body(r_ref, w_ref, m_ref, v_ref, *rest):
        g_out, d_out, m_out, v_out = rest[len(rest) - 4:]
        g = r_ref[0, :, 0:C].astype(F32)
        for p in range(1, N_DEV):
            g = g + r_ref[p, :, 0:C].astype(F32)
        delta, mn, vn = _adam_math(g, w_ref[...], m_ref[...], v_ref[...])
        g_out[...] = g
        d_out[...] = delta
        m_out[...] = mn
        v_out[...] = vn

    blk = _bs((None, tr, C), lambda i: (layer, i, 0))
    shp = jax.ShapeDtypeStruct((L, R, C), F32)
    in_specs = [_bs((N_DEV, tr, Cp), lambda i: (0, i, 0)), blk, blk, blk]
    args = [recv, w, m, v]
    aliases = {}
    if prev is not None:
        in_specs += [HBM_SPEC] * 4
        args += list(prev)
        aliases = {4 + k: k for k in range(4)}
    return pl.pallas_call(
        body, name=name, grid=(R // tr,),
        in_specs=in_specs, out_specs=[blk, blk, blk, blk], out_shape=[shp, shp, shp, shp],
        input_output_aliases=aliases, compiler_params=_params(("parallel",)),
    )(*args)


def _dev_index(px, py, pc):
    return 4 * px + 2 * py + pc


class _GatherRider:
    def __init__(self, shards, out_shapes, views, zero_src=None, zero_views=()):
        self.n = len(shards)
        self.views = views
        self.zero_views = list(zero_views) if zero_src is not None else []
        self.srcs = list(shards) + ([zero_src] if self.zero_views else [])
        self.out_shapes = list(out_shapes)
        n, nz = self.n, len(self.zero_views)
        self.scratch = [pltpu.SemaphoreType.DMA((n, 7)), pltpu.SemaphoreType.DMA((n, 7)),
                        pltpu.SemaphoreType.DMA((n,)), pltpu.SemaphoreType.DMA((max(nz, 1),))]

    def _copies(self, ins, outs, sems):
        n = self.n
        send_sems, recv_sems, local_sems, zero_sems = sems
        x, y, c = lax.axis_index("x"), lax.axis_index("y"), lax.axis_index("c")
        me, sibling = (x, y, c), (x, y, 1 - c)
        chips = [(1 - x, y), (x, 1 - y), (1 - x, 1 - y)]

        def rows(a, blk):
            return self.views[a](outs[a], _dev_index(*blk))

        def copy(a, k, blk, to, src=None):
            return pltpu.make_async_remote_copy(
                src_ref=rows(a, blk) if src is None else src, dst_ref=rows(a, blk),
                send_sem=send_sems.at[a, k], recv_sem=recv_sems.at[a, k], device_id=to, device_id_type=MESH_ID)

        local = [pltpu.make_async_copy(ins[a], rows(a, me), local_sems.at[a]) for a in range(n)]
        local += [pltpu.make_async_copy(ins[n], view(outs[a]), zero_sems.at[i])
                  for i, (a, view) in enumerate(self.zero_views)]
        first = []
        for a in range(n):
            first.append(copy(a, 0, me, sibling, src=ins[a]))
            first += [copy(a, 1 + j, me, (*chip, c), src=ins[a]) for j, chip in enumerate(chips)]
        over_ici = [[copy(a, 1 + j, (*chip, c), me) for a in range(n)] for j, chip in enumerate(chips)]
        passed = [[copy(a, 4 + j, (*chip, c), sibling) for a in range(n)] for j, chip in enumerate(chips)]
        from_sibling = [copy(a, 0, sibling, me) for a in range(n)]
        from_sibling += [copy(a, 4 + j, (*chip, 1 - c), me) for a in range(n) for j, chip in enumerate(chips)]
        return local, first, over_ici, passed, from_sibling

    def begin(self, ins, outs, sems):
        local, first, _, _, _ = self._copies(ins, outs, sems)
        for cp in local + first:
            cp.start()

    def middle(self, ins, outs, sems):
        _, _, over_ici, passed, _ = self._copies(ins, outs, sems)
        for arrived, onward in zip(over_ici, passed):
            for cp, fwd in zip(arrived, onward):
                cp.wait_recv()
                fwd.start()

    def end(self, ins, outs, sems):
        local, first, _, passed, from_sibling = self._copies(ins, outs, sems)
        for cp in from_sibling:
            cp.wait_recv()
        for cp in first + [fwd for onward in passed for fwd in onward]:
            cp.wait_send()
        for cp in local:
            cp.wait()


class _ScatterRider:
    _MASKS = [(kx, ky, kc) for kx in (0, 1) for ky in (0, 1) for kc in (0, 1)][1:]

    def __init__(self, srcs, out_shapes, src_of, dst_at):
        self.n = len(srcs)
        self.srcs = list(srcs)
        self.out_shapes = list(out_shapes)
        self.src_of = src_of
        self.dst_at = dst_at
        n = self.n
        self.scratch = [pltpu.SemaphoreType.DMA((n, 7)), pltpu.SemaphoreType.DMA((n, 7)), pltpu.SemaphoreType.DMA((n,))]

    def _copies(self, ins, outs, sems):
        send_sems, recv_sems, local_sems = sems
        x, y, c = lax.axis_index("x"), lax.axis_index("y"), lax.axis_index("c")
        my = _dev_index(x, y, c)
        peers = [(1 - x if kx else x, 1 - y if ky else y, 1 - c if kc else c) for kx, ky, kc in self._MASKS]

        def send(i, k, to):
            return pltpu.make_async_remote_copy(
                src_ref=self.src_of[i](ins[i], _dev_index(*to)), dst_ref=self.dst_at[i](outs[i], my),
                send_sem=send_sems.at[i, k], recv_sem=recv_sems.at[i, k], device_id=to, device_id_type=MESH_ID)

        def arrival(i, k, frm):
            slot = self.dst_at[i](outs[i], _dev_index(*frm))
            return pltpu.make_async_remote_copy(
                src_ref=slot, dst_ref=slot, send_sem=send_sems.at[i, k], recv_sem=recv_sems.at[i, k],
                device_id=frm, device_id_type=MESH_ID)

        local = [pltpu.make_async_copy(self.src_of[i](ins[i], my), self.dst_at[i](outs[i], my), local_sems.at[i])
                 for i in range(self.n)]
        sends = [send(i, k, to) for k, to in enumerate(peers) for i in range(self.n)]
        arrivals = [arrival(i, k, frm) for k, frm in enumerate(peers) for i in range(self.n)]
        return local, sends, arrivals

    def begin(self, ins, outs, sems):
        local, sends, _ = self._copies(ins, outs, sems)
        for cp in local + sends:
            cp.start()

    def middle(self, ins, outs, sems):
        pass

    def end(self, ins, outs, sems):
        local, sends, arrivals = self._copies(ins, outs, sems)
        for cp in arrivals:
            cp.wait_recv()
        for cp in sends:
            cp.wait_send()
        for cp in local:
            cp.wait()


def _comm_call(rider, name):
    k_in = len(rider.srcs)
    k_out = len(rider.out_shapes)

    def body(*refs):
        ins, outs, sems = refs[:k_in], refs[k_in:k_in + k_out], refs[k_in + k_out:]
        rider.begin(ins, outs, sems)
        rider.middle(ins, outs, sems)
        rider.end(ins, outs, sems)

    return pl.pallas_call(
        body, name=name, in_specs=[HBM_SPEC] * k_in, out_specs=[HBM_SPEC] * k_out, out_shape=rider.out_shapes,
        scratch_shapes=rider.scratch, compiler_params=pltpu.CompilerParams(has_side_effects=True),
    )(*rider.srcs)


def _pad_w_in(w):
    take = jnp.take(w, np.maximum(_IN_PERM, 0), axis=-1)
    return jnp.where(_IN_PERM >= 0, take, jnp.zeros_like(take))


def _unpad_w_in(g):
    return jnp.take(g, _IN_INV, axis=-1)


def _small_pack(w_q_b, w_kv_b):
    a = jnp.pad(w_q_b, ((0, 0), (0, 0), (0, LANES - w_q_b.shape[2])))
    b = jnp.pad(w_kv_b, ((0, 0), (0, 0), (0, LANES - w_kv_b.shape[2])))
    return jnp.concatenate([a, b], axis=1)


def _small_unpack(p, cq, ckv):
    return p[:, 0:MLA_Q_RANK, 0:cq], p[:, MLA_Q_RANK:, 0:ckv]


def _mla_weights(wsm):
    H = MLA_HEADS
    cq = H * (MLA_NOPE + MLA_ROPE) // N_DEV
    ckv = H * (MLA_NOPE + MLA_V) // N_DEV
    wq = wsm[:, 0:MLA_Q_RANK, 0:cq].transpose(1, 0, 2).reshape(MLA_Q_RANK, H, MLA_NOPE + MLA_ROPE)
    wq = jnp.pad(wq, ((0, 0), (0, 0), (0, HEAD_BLOCK - MLA_NOPE - MLA_ROPE))).reshape(MLA_Q_RANK, H * HEAD_BLOCK)
    wkv = wsm[:, MLA_Q_RANK:, 0:ckv].transpose(1, 0, 2).reshape(MLA_KV_RANK, H, MLA_NOPE + MLA_V)
    wk = jnp.pad(wkv[:, :, 0:MLA_NOPE], ((0, 0), (0, 0), (0, HEAD_BLOCK - MLA_NOPE))).reshape(MLA_KV_RANK, H * HEAD_BLOCK)
    wv = wkv[:, :, MLA_NOPE:].reshape(MLA_KV_RANK, H * MLA_V)
    return wq, jnp.concatenate([wk, wv], axis=1)


def _mla_grads_to_blocks(dwq, dwkv):
    H = MLA_HEADS
    gq = dwq.reshape(MLA_Q_RANK, H, HEAD_BLOCK)[:, :, 0:MLA_NOPE + MLA_ROPE].reshape(MLA_Q_RANK, N_DEV, -1)
    gk = dwkv[:, 0:H * HEAD_BLOCK].reshape(MLA_KV_RANK, H, HEAD_BLOCK)[:, :, 0:MLA_NOPE]
    gv = dwkv[:, H * HEAD_BLOCK:].reshape(MLA_KV_RANK, H, MLA_V)
    gkv = jnp.concatenate([gk, gv], axis=2).reshape(MLA_KV_RANK, N_DEV, -1)
    return _small_pack(gq.transpose(1, 0, 2), gkv.transpose(1, 0, 2)).astype(BF16)


def _pool_blockdiag(pool_w):
    z = jnp.zeros((POOL_GROUP, POOL_GROUP), pool_w.dtype)
    halves = [jnp.concatenate([jnp.concatenate([pool_w[2 * i], z], axis=1),
                               jnp.concatenate([z, pool_w[2 * i + 1]], axis=1)], axis=0) for i in range(2)]
    return jnp.stack(halves)


def _pool_blockdiag_t(dw):
    g = POOL_GROUP
    return jnp.stack([dw[0, 0:g, 0:g], dw[0, g:, g:], dw[1, 0:g, 0:g], dw[1, g:, g:]])


def _gate_row(b):
    return jnp.zeros((LANES,), b.dtype).at[_F_LANES].set(b).reshape(1, LANES)


_SMALL = ("ffn1_norm", "mix_norm", "q_a_norm", "kv_a_norm", "pool_w", "pool_scale", "fox_b_f", "ffn2_norm", "final_norm")


def _pack_small(tree):
    rows, recipe = [], []
    for name in _SMALL:
        a = tree[name]
        flat = a.reshape(-1)
        n = flat.shape[0]
        nrow = -(-n // (8 * LANES)) * 8
        flat = jnp.pad(flat, (0, nrow * LANES - n))
        rows.append(flat.reshape(nrow, LANES))
        recipe.append((name, a.shape, n, nrow))
    return jnp.concatenate(rows, axis=0), recipe


def _unpack_small(packed, recipe):
    out, r0 = {}, 0
    for name, shape, n, nrow in recipe:
        out[name] = packed[r0:r0 + nrow].reshape(-1)[0:n].reshape(shape)
        r0 += nrow
    return out


def _adam_small(packs, w, m, v, name):
    R = w.shape[0]

    def body(p_ref, w_ref, m_ref, v_ref, g_out, d_out, m_out, v_out):
        g = p_ref[0]
        for p in range(1, N_DEV):
            g = g + p_ref[p]
        delta, mn, vn = _adam_math(g, w_ref[...], m_ref[...], v_ref[...])
        g_out[...] = g
        d_out[...] = delta
        m_out[...] = mn
        v_out[...] = vn

    blk = _bs((R, LANES), lambda i: (0, 0))
    shp = jax.ShapeDtypeStruct((R, LANES), F32)
    return pl.pallas_call(
        body, name=name, grid=(1,),
        in_specs=[_bs((N_DEV, R, LANES), lambda i: (0, 0, 0)), blk, blk, blk],
        out_specs=[blk, blk, blk, blk], out_shape=[shp, shp, shp, shp],
        compiler_params=_params(("arbitrary",)),
    )(packs, w, m, v)


_GATHER_PLAN = {
    ("first", 0): (("ffn1_w_gu", 0), ("ffn1_w_down", 0)),
    ("ffn1_fwd", 0): (("w_in", 0), ("w_small", 0), ("w_out", 0), ("ffn2_w_down", 0)),
    ("mla_attn_fwd", 0): (("ffn2_w_gu", 0), ("ffn1_w_gu", 1)),
    ("fox_attn_fwd", 0): (("ffn1_w_down", 1), ("w_in", 1), ("w_small", 1), ("w_out", 1)),
    ("mla_attn_fwd", 1): (("ffn2_w_gu", 1), ("ffn2_w_down", 1)),
}
_SCATTER_PLAN = {
    ("mla_attn_bwd", 1): (("ffn2_w_gu", 1), ("w_out", 1)),
    ("fox_attn_bwd", 1): (("ffn2_w_down", 1),),
    ("ffn1_bwd", 1): (("w_in", 1), ("w_small", 1)),
    ("ffn2_bwd", 0): (("ffn1_w_down", 1),),
    ("mla_attn_bwd", 0): (("ffn1_w_gu", 1), ("w_out", 0)),
    ("fox_attn_bwd", 0): (("ffn2_w_gu", 0), ("ffn2_w_down", 0)),
    ("ffn1_bwd_a", 0): (("w_in", 0), ("w_small", 0)),
    ("ffn1_dwgu", 0): (("ffn1_w_down", 0),),
    ("ffn1_bwd_b", 0): (("ffn1_w_gu", 0),),
}
_SPLIT_BWD = (("ffn1", 0),)


class _Exchange:
    def __init__(self, shards, D, f_sh, r_in, r_out):
        self.shards = shards
        self.D, self.f_sh, self.r_in, self.r_out = D, f_sh, r_in, r_out
        self.weights, self.grads, self.recv = {}, {}, {}

    def _rows(self, kind):
        n = {"w_gu": 2 * self.f_sh, "w_down": self.f_sh}[kind]
        return lambda ref, p: ref.at[pl.ds(pl.multiple_of(p * n, 16), n)]

    def _gathered_shape(self, kind):
        D, f_sh = self.D, self.f_sh
        return {"w_gu": (N_DEV * 2 * f_sh, D), "w_down": (N_DEV * f_sh, D), "w_in": (N_DEV, self.r_in, N_PAD),
                "w_small": (N_DEV, MLA_Q_RANK + MLA_KV_RANK, LANES), "w_out": (N_DEV, self.r_out, D)}[kind]

    def _recv_shape(self, kind):
        D, f_sh = self.D, self.f_sh
        return {"w_gu": (N_DEV, 2 * f_sh, D), "w_down": (N_DEV, f_sh, D), "w_in": (N_DEV, self.r_in, N_IN),
                "w_small": (N_DEV, MLA_Q_RANK + MLA_KV_RANK, LANES), "w_out": (N_DEV, self.r_out, D)}[kind]

    @staticmethod
    def _kind(name):
        return name[5:] if name.startswith("ffn") else name

    def gather_rider(self, call, l):
        keys = _GATHER_PLAN.get((call, l))
        if not keys:
            return None
        by_dev = lambda ref, p: ref.at[p]
        shards, shapes, views = [], [], []
        for key in keys:
            kind = self._kind(key[0])
            shards.append(self.shards[key])
            shapes.append(jax.ShapeDtypeStruct(self._gathered_shape(kind), BF16))
            views.append(self._rows(kind) if kind in ("w_gu", "w_down") else by_dev)
        return _GatherRider(shards, shapes, views)

    def gathered(self, call, l, outs):
        for key, w in zip(_GATHER_PLAN.get((call, l), ()), outs):
            if self._kind(key[0]) in ("w_in", "w_out"):
                w = w.reshape((N_DEV * w.shape[1],) + w.shape[2:])
            self.weights[key] = w

    def scatter_rider(self, call, l, pack=None):
        keys = _SCATTER_PLAN.get((call, l), ())
        if not keys and pack is None:
            return None
        by_dev = lambda ref, p: ref.at[p]
        srcs, shapes, src_of = [], [], []
        for key in keys:
            kind = self._kind(key[0])
            srcs.append(self.grads[key])
            shapes.append(jax.ShapeDtypeStruct(self._recv_shape(kind), BF16))
            src_of.append(self._rows(kind) if kind in ("w_gu", "w_down") else by_dev)
        if pack is not None:
            srcs.append(pack)
            shapes.append(jax.ShapeDtypeStruct((N_DEV,) + pack.shape, pack.dtype))
            src_of.append(lambda ref, p: ref)
        return _ScatterRider(srcs, shapes, src_of, [by_dev] * len(srcs))

    def scattered(self, call, l, outs):
        for key, r in zip(_SCATTER_PLAN.get((call, l), ()), outs):
            self.recv[key] = r


def _local_step(x, target, ex, small):
    S, D = x.shape
    tabs = _rope_tables(S)
    nhp_a, nhp_c = MLA_HEADS // 2, FOX_HEADS // 2
    fox_scale = 1.0 / math.sqrt(FOX_HEAD_DIM)
    ex.gathered("first", 0, _comm_call(ex.gather_rider("first", 0), "gather_first"))
    saved = []
    for l in range(DEPTH):
        s = {}
        s["x0"] = x
        wgu1, wd1 = ex.weights[("ffn1_w_gu", l)], ex.weights[("ffn1_w_down", l)]
        (x1, s["h1"], s["gu1"]), got = _ffn_fwd_full(x, small["ffn1_norm"][l][None], wgu1, wd1, FFN_FWD_TOKENS,
                                                    f"ffn1_fwd_l{l}", rider=ex.gather_rider("ffn1_fwd", l))
        ex.gathered("ffn1_fwd", l, got)
        s["x1"] = x1
        w_in = ex.weights[("w_in", l)]
        s["h2"], za, zf, zkvt = _mix_in_fwd(x1, small["mix_norm"][l][None], w_in, f"mix_in_fwd_l{l}")
        s["za"], s["zf"], s["zkvt"] = za, zf, zkvt
        wq, wkv = _mla_weights(ex.weights[("w_small", l)])
        s["wq"], s["wkv"] = wq, wkv
        gq, gkv = small["q_a_norm"][l][None], small["kv_a_norm"][l][None]
        qf, kf, vm, kft, vmt = _mla_prep(za, gq, gkv, wq, wkv, tabs, f"mla_prep_l{l}")
        s["qf"], s["kf"], s["vm"], s["kft"] = qf, kf, vm, kft
        (ya, lse_a), got = _attn_fwd_t(qf, kf, vmt, nhp=nhp_a, dkb=LANES, qoff=0, koff=0, vtoff=0, scale=1.0,
                                       cum=None, cumT=None, name=f"mla_attn_fwd_l{l}",
                                       rider=ex.gather_rider("mla_attn_fwd", l))
        ex.gathered("mla_attn_fwd", l, got)
        s["ya"], s["lse_a"] = ya, lse_a
        b_row = _gate_row(small["fox_b_f"][l])
        s["b_row"] = b_row
        cum, cumT = _fox_prep(za, b_row, f"fox_prep_l{l}")
        s["cum"], s["cumT"] = cum, cumT
        (yc, lse_c), got = _attn_fwd_t(zf, zf, zkvt, nhp=nhp_c, dkb=64, qoff=0, koff=nhp_c, vtoff=nhp_c, scale=fox_scale,
                                       cum=cum, cumT=cumT, name=f"fox_attn_fwd_l{l}",
                                       rider=ex.gather_rider("fox_attn_fwd", l))
        ex.gathered("fox_attn_fwd", l, got)
        s["yc"], s["lse_c"] = yc, lse_c
        wbd = _pool_blockdiag(small["pool_w"][l]).astype(BF16)
        s["wbd"] = wbd
        psc = small["pool_scale"][l][None]
        yb, s["pd"] = _pool_fwd(za, wbd, psc, f"pool_fwd_l{l}")
        w_out = ex.weights[("w_out", l)]
        x2, s["ycat"] = _mix_out_fwd(x1, ya, yb, yc, w_out, f"mix_out_fwd_l{l}")
        s["x2"] = x2
        wgu2, wd2 = ex.weights[("ffn2_w_gu", l)], ex.weights[("ffn2_w_down", l)]
        (x, s["h3"], s["gu2"]), _ = _ffn_fwd_full(x2, small["ffn2_norm"][l][None], wgu2, wd2, FFN_FWD_TOKENS,
                                                  f"ffn2_fwd_l{l}")
        saved.append(s)

    dx, d_final, loss = _loss_head(x, small["final_norm"][None], target, "loss_head")

    small_grads = [None] * DEPTH
    for l in reversed(range(DEPTH)):
        s = saved[l]
        g = {}
        wgu2, wd2 = ex.weights[("ffn2_w_gu", l)], ex.weights[("ffn2_w_down", l)]
        dy3 = dx
        (dx, g["ffn2_norm"], dgu, act, dyh), got = _ffn_bwd_full(
            dy3, s["x2"], small["ffn2_norm"][l][None], s["gu2"], wgu2, wd2, FFN_BWD_TOKENS, f"ffn2_bwd_l{l}",
            rider=ex.scatter_rider("ffn2_bwd", l))
        ex.scattered("ffn2_bwd", l, got)
        ex.grads[("ffn2_w_gu", l)] = _ffn_weight_grad(dgu, s["h3"], f"ffn2_dwgu_l{l}")
        ex.grads[("ffn2_w_down", l)] = _ffn_weight_grad(act, dyh, f"ffn2_dwd_l{l}")

        w_out = ex.weights[("w_out", l)]
        dya, dyb, dyc = _mix_out_bwd(dx, w_out, f"mix_out_bwd_l{l}")
        dw_out = _mm_tn(s["ycat"][None], dx[None], 1, lambda p: 0, lambda p: 0, 1024, 1024, f"dwout_l{l}", ts=1024)[0]
        ex.grads[("w_out", l)] = dw_out.reshape(N_DEV, ex.r_out, D)

        dvec_a = _attn_dvec(s["ya"], dya, nhp_a, f"mla_dvec_l{l}")
        (dqf, dkf, dvm), got = _attn_bwd_t(s["qf"], s["kf"], s["kft"], s["vm"], dya, s["lse_a"], dvec_a, nhp=nhp_a,
                                           dkb=LANES, qoff=0, koff=0, ktoff=0, voff=0, scale=1.0, cum=None, cumT=None,
                                           name=f"mla_attn_bwd_l{l}", rider=ex.scatter_rider("mla_attn_bwd", l))
        ex.scattered("mla_attn_bwd", l, got)
        zf = s["zf"]
        dvec_c = _attn_dvec(s["yc"], dyc, nhp_c, f"fox_dvec_l{l}")
        (dqc, dkc, dvc, dcq, dck), got = _attn_bwd_t(zf, zf, s["zkvt"], zf, dyc, s["lse_c"], dvec_c, nhp=nhp_c, dkb=64,
                                                     qoff=0, koff=nhp_c, ktoff=0, voff=2 * nhp_c, scale=fox_scale,
                                                     cum=s["cum"], cumT=s["cumT"], name=f"fox_attn_bwd_l{l}",
                                                     rider=ex.scatter_rider("fox_attn_bwd", l))
        ex.scattered("fox_attn_bwd", l, got)
        dtail_f, db = _fox_prep_bwd(s["za"], s["b_row"], dcq, dck, f"fox_prep_bwd_l{l}")
        g["fox_b_f"] = db[0, _F_LANES]
        psc = small["pool_scale"][l][None]
        du, dwbd, dpsc = _pool_bwd(dyb, s["pd"], s["wbd"], psc, f"pool_bwd_l{l}")
        g["pool_w"] = _pool_blockdiag_t(dwbd)
        g["pool_scale"] = dpsc[0]
        gq, gkv = small["q_a_norm"][l][None], small["kv_a_norm"][l][None]
        dza, dwq, dwkv, dgq, dgkv = _mla_prep_bwd(s["za"], gq, gkv, s["wq"], s["wkv"], tabs, dqf, dkf, dvm,
                                                   f"mla_prep_bwd_l{l}")
        g["q_a_norm"], g["kv_a_norm"] = dgq[0], dgkv[0]
        ex.grads[("w_small", l)] = _mla_grads_to_blocks(dwq, dwkv)
        w_in = ex.weights[("w_in", l)]
        dx, g["mix_norm"], dz = _mix_in_bwd(dx, s["x1"], small["mix_norm"][l][None], dza, du, dtail_f, dqc, dkc, dvc,
                                            w_in, f"mix_in_bwd_l{l}")
        dw_in = _unpad_w_in(_mm_tn(s["h2"][None], dz[None], 1, lambda p: 0, lambda p: 0, 1024, 640, f"dwin_l{l}",
                                   ts=4096)[0])
        ex.grads[("w_in", l)] = dw_in.reshape(N_DEV, ex.r_in, N_IN)

        wgu1, wd1 = ex.weights[("ffn1_w_gu", l)], ex.weights[("ffn1_w_down", l)]
        dy1 = dx
        gam1 = small["ffn1_norm"][l][None]
        split = ("ffn1", l) in _SPLIT_BWD
        if split:
            (dgu, act, dyh), got = _ffn_bwd_full(dy1, None, None, s["gu1"], None, wd1, FFN_BWD_TOKENS, f"ffn1_bwd_a_l{l}",
                                                 phase="act", rider=ex.scatter_rider("ffn1_bwd_a", l))
            ex.scattered("ffn1_bwd_a", l, got)
        else:
            (dx, g["ffn1_norm"], dgu, act, dyh), got = _ffn_bwd_full(
                dy1, s["x0"], gam1, s["gu1"], wgu1, wd1, FFN_BWD_TOKENS, f"ffn1_bwd_l{l}",
                rider=ex.scatter_rider("ffn1_bwd", l))
            ex.scattered("ffn1_bwd", l, got)
        ex.grads[("ffn1_w_down", l)] = _ffn_weight_grad(act, dyh, f"ffn1_dwd_l{l}")
        rider = ex.scatter_rider("ffn1_dwgu", l)
        dwgu = _ffn_weight_grad(dgu, s["h1"], f"ffn1_dwgu_l{l}", rider=rider)
        if rider is not None:
            dwgu, got = dwgu
            ex.scattered("ffn1_dwgu", l, got)
        ex.grads[("ffn1_w_gu", l)] = dwgu
        if split:
            (dx, g["ffn1_norm"]), got = _ffn_bwd_full(dy1, s["x0"], gam1, None, wgu1, None, FFN_BWD_TOKENS,
                                                     f"ffn1_bwd_b_l{l}", phase="in", dgu_in=dgu,
                                                     rider=ex.scatter_rider("ffn1_bwd_b", l))
            ex.scattered("ffn1_bwd_b", l, got)
        for k in ("ffn1_norm", "ffn2_norm", "mix_norm"):
            g[k] = g[k][0]
        small_grads[l] = g
    return loss, dx, small_grads, d_final[0]


_BIG = ("ffn1_w_gu", "ffn1_w_down", "w_in", "w_small", "w_out", "ffn2_w_gu", "ffn2_w_down")


def kernel(x, ffn1_norm, ffn1_w_gu, ffn1_w_down, mix_norm, w_in, q_a_norm, w_q_b, kv_a_norm, w_kv_b, pool_w, pool_scale, fox_b_f, w_out, ffn2_norm, ffn2_w_gu, ffn2_w_down, final_norm, loss_target, m_ffn1_norm, m_ffn1_w_gu, m_ffn1_w_down, m_mix_norm, m_w_in, m_q_a_norm, m_w_q_b, m_kv_a_norm, m_w_kv_b, m_pool_w, m_pool_scale, m_fox_b_f, m_w_out, m_ffn2_norm, m_ffn2_w_gu, m_ffn2_w_down, m_final_norm, v_ffn1_norm, v_ffn1_w_gu, v_ffn1_w_down, v_mix_norm, v_w_in, v_q_a_norm, v_w_q_b, v_kv_a_norm, v_w_kv_b, v_pool_w, v_pool_scale, v_fox_b_f, v_w_out, v_ffn2_norm, v_ffn2_w_gu, v_ffn2_w_down, v_final_norm):
    W = dict(ffn1_norm=ffn1_norm, ffn1_w_gu=ffn1_w_gu, ffn1_w_down=ffn1_w_down, mix_norm=mix_norm, w_in=w_in,
             q_a_norm=q_a_norm, w_q_b=w_q_b, kv_a_norm=kv_a_norm, w_kv_b=w_kv_b, pool_w=pool_w, pool_scale=pool_scale,
             fox_b_f=fox_b_f, w_out=w_out, ffn2_norm=ffn2_norm, ffn2_w_gu=ffn2_w_gu, ffn2_w_down=ffn2_w_down,
             final_norm=final_norm)
    M = dict(ffn1_norm=m_ffn1_norm, ffn1_w_gu=m_ffn1_w_gu, ffn1_w_down=m_ffn1_w_down, mix_norm=m_mix_norm, w_in=m_w_in,
             q_a_norm=m_q_a_norm, w_q_b=m_w_q_b, kv_a_norm=m_kv_a_norm, w_kv_b=m_w_kv_b, pool_w=m_pool_w,
             pool_scale=m_pool_scale, fox_b_f=m_fox_b_f, w_out=m_w_out, ffn2_norm=m_ffn2_norm, ffn2_w_gu=m_ffn2_w_gu,
             ffn2_w_down=m_ffn2_w_down, final_norm=m_final_norm)
    V = dict(ffn1_norm=v_ffn1_norm, ffn1_w_gu=v_ffn1_w_gu, ffn1_w_down=v_ffn1_w_down, mix_norm=v_mix_norm, w_in=v_w_in,
             q_a_norm=v_q_a_norm, w_q_b=v_w_q_b, kv_a_norm=v_kv_a_norm, w_kv_b=v_w_kv_b, pool_w=v_pool_w,
             pool_scale=v_pool_scale, fox_b_f=v_fox_b_f, w_out=v_w_out, ffn2_norm=v_ffn2_norm, ffn2_w_gu=v_ffn2_w_gu,
             ffn2_w_down=v_ffn2_w_down, final_norm=v_final_norm)
    L, D, n_sh = ffn1_w_gu.shape
    f_sh = ffn1_w_down.shape[1]
    assert n_sh == 2 * f_sh and L == DEPTH
    r_in, r_out = w_in.shape[1], w_out.shape[1]

    tr_in = lambda a: a.transpose(0, 2, 1)
    big_shards = dict(
        ffn1_w_gu=tr_in(ffn1_w_gu).astype(BF16), ffn1_w_down=ffn1_w_down.astype(BF16),
        w_in=_pad_w_in(w_in).astype(BF16), w_small=_small_pack(w_q_b, w_kv_b).astype(BF16), w_out=w_out.astype(BF16),
        ffn2_w_gu=tr_in(ffn2_w_gu).astype(BF16), ffn2_w_down=ffn2_w_down.astype(BF16))
    ex = _Exchange({(k, l): big_shards[k][l] for k in _BIG for l in range(L)}, D, f_sh, r_in, r_out)

    small = {k: W[k] for k in _SMALL}
    loss, dx, grads, d_final = _local_step(x[0], loss_target[0], ex, small)

    small_g = {k: jnp.stack([grads[l][k] for l in range(L)]) for k in _SMALL if k != "final_norm"}
    small_g["final_norm"] = d_final
    pack_g, recipe = _pack_small(small_g)
    n_small = pack_g.shape[0]
    loss_row = -(-n_small // 8) * 8
    pack_g = jnp.concatenate([pack_g, jnp.zeros((loss_row - n_small, LANES), F32), jnp.broadcast_to(loss, (8, LANES))],
                             axis=0)
    *got, packs = _comm_call(ex.scatter_rider("last", 0, pack=pack_g), "scatter_last")
    ex.scattered("last", 0, got)

    out = {}
    sm_w, sm_m, sm_v = (_small_pack(t["w_q_b"], t["w_kv_b"]) for t in (W, M, V))
    big = [("ffn1_w_gu", tr_in(W["ffn1_w_gu"]), tr_in(M["ffn1_w_gu"]), tr_in(V["ffn1_w_gu"]), f_sh),
           ("ffn1_w_down", W["ffn1_w_down"], M["ffn1_w_down"], V["ffn1_w_down"], 352),
           ("w_in", W["w_in"], M["w_in"], V["w_in"], 128),
           ("w_small", sm_w, sm_m, sm_v, 384),
           ("w_out", W["w_out"], M["w_out"], V["w_out"], 128),
           ("ffn2_w_gu", tr_in(W["ffn2_w_gu"]), tr_in(M["ffn2_w_gu"]), tr_in(V["ffn2_w_gu"]), f_sh),
           ("ffn2_w_down", W["ffn2_w_down"], M["ffn2_w_down"], V["ffn2_w_down"], 352)]
    for k, w_, m_, v_, tr in big:
        res = None
        for l in range(L):
            res = _adam_sum(ex.recv[(k, l)], w_, m_, v_, l, res, tr, f"adam_{k}_l{l}")
        if k == "w_small":
            cq, ckv = w_q_b.shape[2], w_kv_b.shape[2]
            parts = [_small_unpack(r, cq, ckv) for r in res]
            out["w_q_b"] = [p[0] for p in parts]
            out["w_kv_b"] = [p[1] for p in parts]
        elif k.endswith("w_gu"):
            out[k] = [tr_in(r) for r in res]
        else:
            out[k] = res

    pw, _ = _pack_small({k: W[k] for k in _SMALL})
    pm, _ = _pack_small({k: M[k] for k in _SMALL})
    pv, _ = _pack_small({k: V[k] for k in _SMALL})
    extra = ((0, loss_row + 8 - n_small), (0, 0))
    res = _adam_small(packs, jnp.pad(pw, extra), jnp.pad(pm, extra), jnp.pad(pv, extra), "adam_small")
    loss_total = res[0][loss_row, 0]
    small_out = [_unpack_small(r, recipe) for r in res]
    for k in _SMALL:
        out[k] = [t[k] for t in small_out]

    names = ["ffn1_norm", "ffn1_w_gu", "ffn1_w_down", "mix_norm", "w_in", "q_a_norm", "w_q_b", "kv_a_norm", "w_kv_b",
             "pool_w", "pool_scale", "fox_b_f", "w_out", "ffn2_norm", "ffn2_w_gu", "ffn2_w_down", "final_norm"]
    outs = [loss_total, dx[None]]
    for which in range(4):
        outs += [out[k][which] for k in names]
    return tuple(outs)
```

```python
import functools
import math

import numpy as np
import jax
import jax.numpy as jnp
from jax import lax
from jax.experimental import pallas as pl
from jax.experimental.pallas import tpu as pltpu

F32 = jnp.float32
BF16 = jnp.bfloat16
MESH_ID = pl.DeviceIdType.MESH

N_DEV = 8
EPS = 1e-6
DEPTH = 2

MLA_HEADS = 6
MLA_Q_RANK = 256
MLA_KV_RANK = 128
MLA_NOPE = 64
MLA_ROPE = 32
MLA_V = 64
ROPE_THETA = 10000.0
POOL_WINDOWS = (2, 4, 8, 16)
POOL_GROUP = 64
POOL_WIDTH = 256
FOX_HEADS = 6
FOX_HEAD_DIM = 64
N_IN = 1830

ADAM_LR = 0.001
ADAM_B1 = 0.9
ADAM_B2 = 0.999
ADAM_EPS = 1e-08
ADAM_WD = 0.01
ADAM_STEP = 10

LANES = 128
HEAD_BLOCK = 128
VMEM_LIMIT = 48 * 1024 * 1024
VMEM_LIMIT_BIG = 50 * 1024 * 1024
NEG = -1e30
ATTN_BLOCK = 1024

ZA = 768
ZF = 1152
N_PAD = ZA + ZF
TAIL0 = 640
ROPE_LANE0 = 64


def _f_lane(h):
    return 8 * (h // 2) + (h % 2)


def _in_perm():
    perm = -np.ones(N_PAD, np.int32)
    perm[0:256] = np.arange(0, 256)
    perm[256:384] = np.arange(256, 384)
    perm[384:640] = np.arange(416, 672)
    for h in range(FOX_HEADS):
        perm[TAIL0 + _f_lane(h)] = 1824 + h
    perm[TAIL0 + ROPE_LANE0:TAIL0 + ROPE_LANE0 + MLA_ROPE] = np.arange(384, 416)
    perm[ZA:N_PAD] = np.arange(672, 1824)
    inv = np.zeros(N_IN, np.int32)
    for new, old in enumerate(perm):
        if old >= 0:
            inv[old] = new
    return perm, inv


_IN_PERM, _IN_INV = _in_perm()
_F_LANES = np.array([_f_lane(h) for h in range(FOX_HEADS)], np.int32)


def _dot(a, b):
    return jnp.dot(a, b, preferred_element_type=F32)


def _dot_nt(a, b):
    return lax.dot_general(a, b, (((1,), (1,)), ((), ())), preferred_element_type=F32)


def _dot_tn(a, b):
    return lax.dot_general(a, b, (((0,), (0,)), ((), ())), preferred_element_type=F32)


def _rms(x, gam):
    r = lax.rsqrt(jnp.mean(x * x, axis=-1, keepdims=True) + EPS)
    return x * r * gam


def _rms_bwd(dy, x, gam):
    r = lax.rsqrt(jnp.mean(x * x, axis=-1, keepdims=True) + EPS)
    xh = x * r
    dxh = dy * gam
    dx = r * (dxh - xh * jnp.mean(dxh * xh, axis=-1, keepdims=True))
    return dx, jnp.sum(dy * xh, axis=0, keepdims=True)


def _accum_out(ref, first, val):
    @pl.when(first)
    def _():
        ref[...] = val

    @pl.when(jnp.logical_not(first))
    def _():
        ref[...] += val


def _bs(shape, fn):
    return pl.BlockSpec(shape, fn)


def _params(dims, vmem=VMEM_LIMIT):
    return pltpu.CompilerParams(dimension_semantics=dims, vmem_limit_bytes=vmem)


def _tile(n, t):
    t = min(n, t)
    assert n % t == 0, (n, t)
    return t


HBM_SPEC = pl.BlockSpec(memory_space=pl.ANY)


def _call(body, *, name, grid, in_specs, out_specs, out_shape, scratch, dims, args, rider=None, vmem=VMEM_LIMIT):
    n_in, n_out = len(in_specs), len(out_specs)
    if rider is None:
        outs = pl.pallas_call(body, name=name, grid=grid, in_specs=in_specs, out_specs=out_specs, out_shape=out_shape,
                              scratch_shapes=scratch, compiler_params=_params(dims, vmem))(*args)
        return list(outs), []
    k_in, k_out, k_sem = len(rider.srcs), len(rider.out_shapes), len(rider.scratch)

    def riding(*refs):
        a, b, c, d = n_in, n_in + k_in, n_in + k_in + n_out, n_in + k_in + n_out + k_out
        rest = refs[d:]
        sems = rest[len(rest) - k_sem:]
        step = 0
        for i, g in enumerate(grid):
            step = step * g + pl.program_id(i)
        n_steps = math.prod(grid)

        @pl.when(step == 0)
        def _():
            rider.begin(refs[a:b], refs[c:d], sems)

        body(*refs[:a], *refs[b:c], *rest[:len(rest) - k_sem])

        @pl.when(step == (3 * n_steps) // 4)
        def _():
            rider.middle(refs[a:b], refs[c:d], sems)

        @pl.when(step == n_steps - 1)
        def _():
            rider.end(refs[a:b], refs[c:d], sems)

    outs = pl.pallas_call(
        riding, name=name, grid=grid, in_specs=list(in_specs) + [HBM_SPEC] * k_in,
        out_specs=list(out_specs) + [HBM_SPEC] * k_out, out_shape=list(out_shape) + list(rider.out_shapes),
        scratch_shapes=list(scratch) + list(rider.scratch),
        compiler_params=_params(("arbitrary",) * len(grid), vmem))(*args, *rider.srcs)
    return list(outs[:n_out]), list(outs[n_out:])


_ONCE = pl.Buffered(1)
_FF_CHUNKS = ((0, 1536), (1536, 2816))


def _ffn_fwd_full(x, gam, wgut, wd, tm, name, rider=None):
    S, D = x.shape
    F = wd.shape[0]
    tm = _tile(S, tm)
    chunks = _FF_CHUNKS if F == 2816 else ((0, F),)

    def body(x_ref, gam_ref, wgut_ref, wd_ref, xo_ref, h_ref, gu_ref):
        h = _rms(x_ref[...], gam_ref[...]).astype(BF16)
        h_ref[...] = h
        y = None
        for c0, c1 in chunks:
            g = _dot_nt(h, wgut_ref[c0:c1, :])
            u = _dot_nt(h, wgut_ref[F + c0:F + c1, :])
            gu_ref[:, c0:c1] = g.astype(BF16)
            gu_ref[:, F + c0:F + c1] = u.astype(BF16)
            a = (g * jax.nn.sigmoid(g) * u).astype(BF16)
            part = _dot(a, wd_ref[c0:c1, :])
            y = part if y is None else y + part
        xo_ref[...] = x_ref[...] + 0.5 * y

    row = lambda i: (i, 0)
    fix = lambda i: (0, 0)
    return _call(
        body, name=name, grid=(S // tm,),
        in_specs=[_bs((tm, D), row), _bs((1, D), fix), pl.BlockSpec((2 * F, D), fix, pipeline_mode=_ONCE),
                  pl.BlockSpec((F, D), fix, pipeline_mode=_ONCE)],
        out_specs=[_bs((tm, D), row), _bs((tm, D), row), _bs((tm, 2 * F), row)],
        out_shape=[jax.ShapeDtypeStruct((S, D), F32), jax.ShapeDtypeStruct((S, D), BF16),
                   jax.ShapeDtypeStruct((S, 2 * F), BF16)],
        scratch=[], dims=("parallel",), args=(x, gam, wgut, wd), rider=rider)


def _ffn_bwd_full(dy, x, gam, gu, wgut, wd, tm, name, phase="all", dgu_in=None, rider=None):
    S, D = dy.shape
    F = wd.shape[0] if wd is not None else wgut.shape[0] // 2
    tm = _tile(S, tm)
    chunks = _FF_CHUNKS if F == 2816 else ((0, F),)
    act, inp = phase in ("all", "act"), phase in ("all", "in")

    def body(*refs):
        refs = list(refs)
        dy_ref = refs.pop(0)
        x_ref, gam_ref = (refs.pop(0), refs.pop(0)) if inp else (None, None)
        gu_ref = refs.pop(0)
        wgut_ref = refs.pop(0) if inp else None
        wd_ref = refs.pop(0) if act else None
        if inp:
            dx_ref, dgam_ref = refs.pop(0), refs.pop(0)
        if act:
            dgu_ref, a_ref, dyh_ref = refs.pop(0), refs.pop(0), refs.pop(0)
            dyh = (0.5 * dy_ref[...]).astype(BF16)
            dyh_ref[...] = dyh
        dh = None
        for c0, c1 in chunks:
            if act:
                dg, du, a = _swiglu_bwd(_dot_nt(dyh, wd_ref[c0:c1, :]), gu_ref[:, c0:c1], gu_ref[:, F + c0:F + c1])
                dgu_ref[:, c0:c1] = dg
                dgu_ref[:, F + c0:F + c1] = du
                a_ref[:, c0:c1] = a
            else:
                dg, du = gu_ref[:, c0:c1], gu_ref[:, F + c0:F + c1]
            if inp:
                part = _dot(dg, wgut_ref[c0:c1, :]) + _dot(du, wgut_ref[F + c0:F + c1, :])
                dh = part if dh is None else dh + part
        if inp:
            dxn, dgam = _rms_bwd(dh, x_ref[...], gam_ref[...])
            dx_ref[...] = dy_ref[...] + dxn
            _accum_out(dgam_ref, pl.program_id(0) == 0, dgam)

    row = lambda i: (i, 0)
    fix = lambda i: (0, 0)
    in_specs, args = [_bs((tm, D), row)], [dy]
    if inp:
        in_specs += [_bs((tm, D), row), _bs((1, D), fix)]
        args += [x, gam]
    in_specs += [_bs((tm, 2 * F), row)]
    args += [gu if act else dgu_in]
    if inp:
        in_specs += [pl.BlockSpec((2 * F, D), fix, pipeline_mode=_ONCE)]
        args += [wgut]
    if act:
        in_specs += [pl.BlockSpec((F, D), fix, pipeline_mode=_ONCE)]
        args += [wd]
    out_specs, out_shape = [], []
    if inp:
        out_specs += [_bs((tm, D), row), _bs((1, D), fix)]
        out_shape += [jax.ShapeDtypeStruct((S, D), F32), jax.ShapeDtypeStruct((1, D), F32)]
    if act:
        out_specs += [_bs((tm, 2 * F), row), _bs((tm, F), row), _bs((tm, D), row)]
        out_shape += [jax.ShapeDtypeStruct((S, 2 * F), BF16), jax.ShapeDtypeStruct((S, F), BF16),
                      jax.ShapeDtypeStruct((S, D), BF16)]
    return _call(body, name=name, grid=(S // tm,), in_specs=in_specs, out_specs=out_specs, out_shape=out_shape,
                 scratch=[], dims=("arbitrary",), args=tuple(args), rider=rider, vmem=VMEM_LIMIT_BIG)


def _swiglu_bwd(da, g, u):
    g = g.astype(F32)
    u = u.astype(F32)
    sig = jax.nn.sigmoid(g)
    sl = g * sig
    dg = (da * u * (sig * (1.0 + g * (1.0 - sig)))).astype(BF16)
    return dg, (da * sl).astype(BF16), (sl * u).astype(BF16)


def _mm_tn(a, b, nb, a_of, b_of, tm, tn, name, rider=None, ts=512):
    _, S, M = a.shape
    N = b.shape[2]
    tm = _tile(M, tm)
    tn = _tile(N, tn)
    ts = _tile(S, ts)
    nk = S // ts

    def body(a_ref, b_ref, o_ref, acc):
        k = pl.program_id(3)

        @pl.when(k == 0)
        def _():
            acc[...] = jnp.zeros_like(acc)

        acc[...] += _dot_tn(a_ref[...].astype(BF16), b_ref[...].astype(BF16))

        @pl.when(k == nk - 1)
        def _():
            o_ref[...] = acc[...].astype(o_ref.dtype)

    (out,), extra = _call(
        body, name=name, grid=(nb, M // tm, N // tn, nk),
        in_specs=[_bs((None, ts, tm), lambda p, i, j, k: (a_of(p), k, i)),
                  _bs((None, ts, tn), lambda p, i, j, k: (b_of(p), k, j))],
        out_specs=[_bs((None, tm, tn), lambda p, i, j, k: (p, i, j))],
        out_shape=[jax.ShapeDtypeStruct((nb, M, N), BF16)],
        scratch=[pltpu.VMEM((tm, tn), F32)],
        dims=("parallel", "parallel", "parallel", "arbitrary"), args=(a, b), rider=rider)
    return (out, extra) if rider is not None else out


FFN_FWD_TOKENS = 512
FFN_BWD_TOKENS = 256
FFN_GRAD_ROWS = 1408


def _ffn_weight_grad(a, b, name, rider=None):
    M = a.shape[1]
    tm = max(t for t in range(LANES, FFN_GRAD_ROWS + 1, LANES) if M % t == 0)
    res = _mm_tn(a[None], b[None], 1, lambda p: 0, lambda p: 0, tm, b.shape[1], name, rider=rider, ts=2048)
    return (res[0][0], res[1]) if rider is not None else res[0]


def _mix_in_fwd(x, gam, w_in, name):
    S, D = x.shape
    tm = _tile(S, 512)
    nkv = (ZF - 384) // LANES

    def body(x_ref, gam_ref, w_ref, h_ref, za_ref, zf_ref, zt_ref):
        hb = _rms(x_ref[...], gam_ref[...]).astype(BF16)
        h_ref[...] = hb
        za_ref[...] = _dot(hb, w_ref[:, 0:ZA])
        zf = _dot(hb, w_ref[:, ZA:N_PAD])
        zf_ref[...] = zf.astype(BF16)
        for c in range(nkv):
            zt_ref[c * LANES:(c + 1) * LANES, :] = zf[:, 384 + c * LANES:384 + (c + 1) * LANES].T.astype(BF16)

    return pl.pallas_call(
        body, name=name, grid=(S // tm,),
        in_specs=[_bs((tm, D), lambda i: (i, 0)), _bs((1, D), lambda i: (0, 0)), _bs((D, N_PAD), lambda i: (0, 0))],
        out_specs=[_bs((tm, D), lambda i: (i, 0)), _bs((tm, ZA), lambda i: (i, 0)), _bs((tm, ZF), lambda i: (i, 0)),
                   _bs((nkv * LANES, tm), lambda i: (0, i))],
        out_shape=[jax.ShapeDtypeStruct((S, D), BF16), jax.ShapeDtypeStruct((S, ZA), F32),
                   jax.ShapeDtypeStruct((S, ZF), BF16), jax.ShapeDtypeStruct((nkv * LANES, S), BF16)],
        compiler_params=_params(("parallel",)),
    )(x, gam, w_in)


def _mix_in_bwd(dy, x, gam, dza_mla, du, dtail_f, dqf, dkf, dvf, w_in, name):
    S, D = x.shape
    tm = _tile(S, 512)

    def body(dy_ref, x_ref, gam_ref, dza_ref, du_ref, dt_ref, dq_ref, dk_ref, dv_ref, w_ref, dx_ref, dgam_ref, dz_ref):
        i = pl.program_id(0)
        dza = dza_ref[...]
        dz = jnp.concatenate([dza[:, 0:384], du_ref[...], dza[:, TAIL0:ZA] + dt_ref[...],
                              dq_ref[...], dk_ref[...], dv_ref[...]], axis=1).astype(BF16)
        dz_ref[...] = dz
        dh = _dot_nt(dz, w_ref[...])
        dxn, dgam = _rms_bwd(dh, x_ref[...], gam_ref[...])
        dx_ref[...] = dy_ref[...] + dxn
        _accum_out(dgam_ref, i == 0, dgam)

    row = lambda i: (i, 0)
    fix = lambda i: (0, 0)
    return pl.pallas_call(
        body, name=name, grid=(S // tm,),
        in_specs=[_bs((tm, D), row), _bs((tm, D), row), _bs((1, D), fix), _bs((tm, ZA), row), _bs((tm, 256), row),
                  _bs((tm, 128), row), _bs((tm, 384), row), _bs((tm, 384), row), _bs((tm, 384), row),
                  _bs((D, N_PAD), fix)],
        out_specs=[_bs((tm, D), row), _bs((1, D), fix), _bs((tm, N_PAD), row)],
        out_shape=[jax.ShapeDtypeStruct((S, D), F32), jax.ShapeDtypeStruct((1, D), F32),
                   jax.ShapeDtypeStruct((S, N_PAD), BF16)],
        compiler_params=_params(("arbitrary",)),
    )(dy, x, gam, dza_mla, du, dtail_f, dqf, dkf, dvf, w_in)


def _rope_tables(S):
    half = MLA_ROPE // 2
    inv_freq = ROPE_THETA ** (-jnp.arange(0, MLA_ROPE, 2, dtype=F32) / MLA_ROPE)
    ang = jnp.arange(S, dtype=jnp.int32).astype(F32)[:, None] * inv_freq[None, :]
    cos, sin = jnp.cos(ang), jnp.sin(ang)
    one = jnp.ones((S, ROPE_LANE0), F32)
    zero = jnp.zeros((S, ROPE_LANE0), F32)
    pad1 = jnp.ones((S, LANES - ROPE_LANE0 - MLA_ROPE), F32)
    pad0 = jnp.zeros((S, LANES - ROPE_LANE0 - MLA_ROPE), F32)
    zh = jnp.zeros((S, half), F32)
    tab_c = jnp.concatenate([one, cos, cos, pad1], axis=1)
    tab_ck = jnp.concatenate([zero, cos, cos, pad0], axis=1)
    tab_s1 = jnp.concatenate([zero, -sin, zh, pad0], axis=1)
    tab_s2 = jnp.concatenate([zero, zh, sin, pad0], axis=1)
    return tab_c, tab_ck, tab_s1, tab_s2


def _rope(x, c, s1, s2):
    return x * c + pltpu.roll(x, LANES - 16, 1) * s1 + pltpu.roll(x, 16, 1) * s2


def _rope_t(dy, c, s1, s2):
    return dy * c + pltpu.roll(dy * s1, 16, 1) + pltpu.roll(dy * s2, LANES - 16, 1)


_MLA_SCALE = 1.0 / math.sqrt(MLA_NOPE + MLA_ROPE)


def _mla_prep(za, gq, gkv, wq, wkv, tabs, name):
    S = za.shape[0]
    tm = _tile(S, 512)
    H = MLA_HEADS

    def body(zq_ref, tail_ref, gq_ref, gkv_ref, wq_ref, wkv_ref, c_ref, ck_ref, s1_ref, s2_ref,
             qf_ref, kf_ref, v_ref, kft_ref, vt_ref):
        zq = zq_ref[...]
        c, s1, s2 = c_ref[...], s1_ref[...], s2_ref[...]
        qn = _rms(zq[:, 0:256], gq_ref[...]).astype(BF16)
        q = _dot(qn, wq_ref[...])
        for h in range(H):
            blk = _rope(q[:, h * LANES:(h + 1) * LANES], c, s1, s2)
            qf_ref[:, h * LANES:(h + 1) * LANES] = (blk * _MLA_SCALE).astype(BF16)
        kvn = _rms(zq[:, 256:384], gkv_ref[...]).astype(BF16)
        kv = _dot(kvn, wkv_ref[...])
        kt = _rope(tail_ref[...], ck_ref[...], s1, s2)
        for h in range(H):
            sl = slice(h * LANES, (h + 1) * LANES)
            kblk = kv[:, sl] + kt
            kf_ref[:, sl] = kblk.astype(BF16)
            kft_ref[sl, :] = kblk.T.astype(BF16)
        v_ref[...] = kv[:, H * LANES:].astype(BF16)
        for cblk in range(H * MLA_V // LANES):
            sl = slice(cblk * LANES, (cblk + 1) * LANES)
            vt_ref[sl, :] = kv[:, H * LANES + cblk * LANES:H * LANES + (cblk + 1) * LANES].T.astype(BF16)

    row = lambda i: (i, 0)
    col = lambda i: (0, i)
    fix = lambda i: (0, 0)
    return pl.pallas_call(
        body, name=name, grid=(S // tm,),
        in_specs=[_bs((tm, 384), row), _bs((tm, 128), lambda i: (i, TAIL0 // 128)), _bs((1, 256), fix), _bs((1, 128), fix),
                  _bs((256, 768), fix), _bs((128, 1152), fix),
                  _bs((tm, 128), row), _bs((tm, 128), row), _bs((tm, 128), row), _bs((tm, 128), row)],
        out_specs=[_bs((tm, 768), row), _bs((tm, 768), row), _bs((tm, 384), row), _bs((768, tm), col), _bs((384, tm), col)],
        out_shape=[jax.ShapeDtypeStruct((S, 768), BF16), jax.ShapeDtypeStruct((S, 768), BF16),
                   jax.ShapeDtypeStruct((S, 384), BF16), jax.ShapeDtypeStruct((768, S), BF16),
                   jax.ShapeDtypeStruct((384, S), BF16)],
        compiler_params=_params(("parallel",)),
    )(za, za, gq, gkv, wq, wkv, *tabs)


def _mla_prep_bwd(za, gq, gkv, wq, wkv, tabs, dqf, dkf, dvm, name):
    S = za.shape[0]
    tm = _tile(S, 512)
    H = MLA_HEADS

    def body(zq_ref, gq_ref, gkv_ref, wq_ref, wkv_ref, c_ref, ck_ref, s1_ref, s2_ref, dqf_ref, dkf_ref, dvm_ref,
             dza_ref, dwq_ref, dwkv_ref, dgq_ref, dgkv_ref):
        i = pl.program_id(0)
        first = i == 0
        zq = zq_ref[...]
        c, s1, s2 = c_ref[...], s1_ref[...], s2_ref[...]
        lane = lax.broadcasted_iota(jnp.int32, (1, LANES), 1)
        nope = lane < MLA_NOPE
        rope = jnp.logical_and(lane >= ROPE_LANE0, lane < ROPE_LANE0 + MLA_ROPE)

        qa = zq[:, 0:256]
        qn = _rms(qa, gq_ref[...]).astype(BF16)
        dqf = dqf_ref[...]
        dq_pre = jnp.concatenate(
            [_rope_t(dqf[:, h * LANES:(h + 1) * LANES] * _MLA_SCALE, c, s1, s2) for h in range(H)], axis=1).astype(BF16)
        _accum_out(dwq_ref, first, _dot_tn(qn, dq_pre))
        dqa, dgq = _rms_bwd(_dot_nt(dq_pre, wq_ref[...]), qa, gq_ref[...])
        _accum_out(dgq_ref, first, dgq)

        kva = zq[:, 256:384]
        kvn = _rms(kva, gkv_ref[...]).astype(BF16)
        dkf = dkf_ref[...]
        parts = []
        dkt = jnp.zeros((tm, LANES), F32)
        for h in range(H):
            blk = dkf[:, h * LANES:(h + 1) * LANES]
            parts.append(jnp.where(nope, blk, 0.0))
            dkt = dkt + jnp.where(rope, blk, 0.0)
        dkv_pre = jnp.concatenate(parts + [dvm_ref[...]], axis=1).astype(BF16)
        _accum_out(dwkv_ref, first, _dot_tn(kvn, dkv_pre))
        dkva, dgkv = _rms_bwd(_dot_nt(dkv_pre, wkv_ref[...]), kva, gkv_ref[...])
        _accum_out(dgkv_ref, first, dgkv)

        dtail = _rope_t(dkt, ck_ref[...], s1, s2)
        dza_ref[...] = jnp.concatenate([dqa, dkva, jnp.zeros((tm, 256), F32), dtail], axis=1)

    row = lambda i: (i, 0)
    fix = lambda i: (0, 0)
    return pl.pallas_call(
        body, name=name, grid=(S // tm,),
        in_specs=[_bs((tm, 384), row), _bs((1, 256), fix), _bs((1, 128), fix), _bs((256, 768), fix), _bs((128, 1152), fix),
                  _bs((tm, 128), row), _bs((tm, 128), row), _bs((tm, 128), row), _bs((tm, 128), row),
                  _bs((tm, 768), row), _bs((tm, 768), row), _bs((tm, 384), row)],
        out_specs=[_bs((tm, ZA), row), _bs((256, 768), fix), _bs((128, 1152), fix), _bs((1, 256), fix), _bs((1, 128), fix)],
        out_shape=[jax.ShapeDtypeStruct((S, ZA), F32), jax.ShapeDtypeStruct((256, 768), F32),
                   jax.ShapeDtypeStruct((128, 1152), F32), jax.ShapeDtypeStruct((1, 256), F32),
                   jax.ShapeDtypeStruct((1, 128), F32)],
        compiler_params=_params(("arbitrary",)),
    )(za, gq, gkv, wq, wkv, *tabs, dqf, dkf, dvm)


def _head_views(qb, kb, r, dkb, sel):
    if dkb == LANES:
        sl = slice(r * LANES, (r + 1) * LANES)
        return qb[:, sl], kb[:, sl], kb[:, sl]
    return jnp.where(sel, qb, jnp.zeros_like(qb)), kb, jnp.where(sel, kb, jnp.zeros_like(kb))


def _attn_fwd_t(q_arr, k_arr, vt_arr, *, nhp, dkb, qoff, koff, vtoff, scale, cum, cumT, name, rider=None):
    S = q_arr.shape[0]
    T = _tile(S, ATTN_BLOCK)
    nq = S // T
    W = 2 * dkb
    bias = cum is not None

    def body(*refs):
        if bias:
            q_ref, k_ref, vt_ref, cq_ref, ck_ref, o_ref, lse_ref, m_s, l_s, acc_s = refs
        else:
            q_ref, k_ref, vt_ref, o_ref, lse_ref, m_s, l_s, acc_s = refs
        hp, qi, ki = pl.program_id(0), pl.program_id(1), pl.program_id(2)
        lo_lane = lax.broadcasted_iota(jnp.int32, (1, LANES), 1) < 64
        lo_row = lax.broadcasted_iota(jnp.int32, (LANES, 1), 0) < 64

        @pl.when(ki == 0)
        def _():
            m_s[...] = jnp.full_like(m_s, NEG)
            l_s[...] = jnp.zeros_like(l_s)
            acc_s[...] = jnp.zeros_like(acc_s)

        def step(masked):
            qb, kb, vtb = q_ref[...], k_ref[...], vt_ref[...]
            if masked:
                mask = lax.broadcasted_iota(jnp.int32, (T, T), 0) <= lax.broadcasted_iota(jnp.int32, (T, T), 1)
            if bias:
                li = lax.broadcasted_iota(jnp.int32, (T, LANES), 1)
                ckb = ck_ref[...]
            m_all, l_all = m_s[...], l_s[...]
            scores = []
            for r in range(2):
                sel = lo_lane if r == 0 else jnp.logical_not(lo_lane)
                q, k, _ = _head_views(qb, kb, r, dkb, sel)
                if scale != 1.0:
                    q = q * jnp.asarray(scale, q.dtype)
                scores.append(_dot_nt(k, q))
            m_out, l_out, alphas, pvs = [], [], [], []
            for r in range(2):
                rsel = lo_row if r == 0 else jnp.logical_not(lo_row)
                s = scores[r]
                if bias:
                    ck = jnp.sum(jnp.where(li == 8 * hp + r, ckb, 0.0), axis=1, keepdims=True)
                    s = s + (cq_ref[r:r + 1, :] - ck)
                if masked:
                    s = jnp.where(mask, s, NEG)
                m_prev = m_all[r:r + 1, :]
                m_new = jnp.maximum(m_prev, jnp.max(s, axis=0, keepdims=True))
                alpha = jnp.exp(m_prev - m_new)
                p = jnp.exp(s - m_new)
                m_out.append(m_new)
                l_out.append(alpha * l_all[r:r + 1, :] + jnp.sum(p, axis=0, keepdims=True))
                alphas.append(alpha)
                pvs.append(_dot(jnp.where(rsel, vtb, jnp.zeros_like(vtb)), p.astype(BF16)))
            m_s[0:1, :] = m_out[0]
            m_s[1:2, :] = m_out[1]
            l_s[0:1, :] = l_out[0]
            l_s[1:2, :] = l_out[1]
            acc_s[...] = acc_s[...] * jnp.where(lo_row, alphas[0], alphas[1]) + (pvs[0] + pvs[1])

        @pl.when(ki < qi)
        def _():
            step(False)

        @pl.when(ki == qi)
        def _():
            step(True)

        @pl.when(ki == nq - 1)
        def _():
            inv = jnp.where(lo_row, 1.0 / l_s[0:1, :], 1.0 / l_s[1:2, :])
            o_ref[...] = (acc_s[...] * inv).T.astype(BF16)
            used = lax.broadcasted_iota(jnp.int32, (8, T), 0) < 2
            lse_ref[...] = jnp.where(used, m_s[...] + jnp.log(jnp.where(used, l_s[...], 1.0)), 0.0)

    kmap = lambda hp, qi, ki: jnp.minimum(ki, qi)
    in_specs = [_bs((T, W), lambda hp, qi, ki: (qi, qoff + hp)),
                _bs((T, W), lambda hp, qi, ki: (kmap(hp, qi, ki), koff + hp)),
                _bs((LANES, T), lambda hp, qi, ki: (vtoff + hp, kmap(hp, qi, ki)))]
    args = [q_arr, k_arr, vt_arr]
    if bias:
        in_specs += [_bs((8, T), lambda hp, qi, ki: (hp, qi)), _bs((T, LANES), lambda hp, qi, ki: (kmap(hp, qi, ki), 0))]
        args += [cumT, cum]
    return _call(
        body, name=name, grid=(nhp, nq, nq),
        in_specs=in_specs,
        out_specs=[_bs((T, LANES), lambda hp, qi, ki: (qi, hp)), _bs((None, 8, T), lambda hp, qi, ki: (hp, 0, qi))],
        out_shape=[jax.ShapeDtypeStruct((S, nhp * LANES), BF16), jax.ShapeDtypeStruct((nhp, 8, S), F32)],
        scratch=[pltpu.VMEM((8, T), F32), pltpu.VMEM((8, T), F32), pltpu.VMEM((LANES, T), F32)],
        dims=("parallel", "parallel", "arbitrary"), args=args, rider=rider)


def _attn_bwd_t(q_arr, k_arr, kt_arr, v_arr, do_arr, lse, dvec, *, nhp, dkb, qoff, koff, ktoff, voff, scale, cum, cumT,
                name, rider=None):
    S = q_arr.shape[0]
    T = _tile(S, ATTN_BLOCK)
    nq = S // T
    W = 2 * dkb
    bias = cum is not None

    def body(*refs):
        if bias:
            (q_ref, k_ref, kt_ref, v_ref, do_ref, lse_ref, dvec_ref, cq_ref, ck_ref,
             dq_ref, dk_ref, dv_ref, dcq_ref, dck_ref, dqt_s, dk_s, dv_s, dcq_s, dck_s) = refs
        else:
            (q_ref, k_ref, kt_ref, v_ref, do_ref, lse_ref, dvec_ref,
             dq_ref, dk_ref, dv_ref, dqt_s, dk_s, dv_s) = refs
        hp, ki, qi = pl.program_id(0), pl.program_id(1), pl.program_id(2)
        lo_lane = lax.broadcasted_iota(jnp.int32, (1, LANES), 1) < 64
        lo_row = lax.broadcasted_iota(jnp.int32, (LANES, 1), 0) < 64

        @pl.when(jnp.logical_and(ki == 0, qi == 0))
        def _():
            dqt_s[...] = jnp.zeros_like(dqt_s)
            if bias:
                dcq_s[...] = jnp.zeros_like(dcq_s)

        @pl.when(qi == 0)
        def _():
            dk_s[...] = jnp.zeros_like(dk_s)
            dv_s[...] = jnp.zeros_like(dv_s)
            if bias:
                dck_s[...] = jnp.zeros_like(dck_s)

        def step(masked):
            qb, kb, ktb, vb, dob = q_ref[...], k_ref[...], kt_ref[...], v_ref[...], do_ref[...]
            if masked:
                mask = lax.broadcasted_iota(jnp.int32, (T, T), 0) <= lax.broadcasted_iota(jnp.int32, (T, T), 1)
            if bias:
                li = lax.broadcasted_iota(jnp.int32, (T, LANES), 1)
                ckb = ck_ref[...]
            for r in range(2):
                sel = lo_lane if r == 0 else jnp.logical_not(lo_lane)
                rsel = lo_row if r == 0 else jnp.logical_not(lo_row)
                q, k, _ = _head_views(qb, kb, r, dkb, sel)
                if scale != 1.0:
                    q = q * jnp.asarray(scale, q.dtype)
                s = _dot_nt(k, q)
                if bias:
                    ck = jnp.sum(jnp.where(li == 8 * hp + r, ckb, 0.0), axis=1, keepdims=True)
                    s = s + (cq_ref[r:r + 1, :] - ck)
                p = jnp.exp(s - lse_ref[r:r + 1, :])
                if masked:
                    p = jnp.where(mask, p, 0.0)
                do_r = jnp.where(sel, dob, jnp.zeros_like(dob))
                dp = _dot_nt(vb, do_r)
                ds = p * (dp - dvec_ref[r:r + 1, :])
                pb = p.astype(BF16)
                dsb = ds.astype(BF16)
                dv_s[...] += _dot(pb, do_r)
                if dkb == LANES:
                    sl = slice(r * LANES, (r + 1) * LANES)
                    dk_s[:, sl] += _dot(dsb, q)
                    dqt_s[qi, sl, :] += _dot(ktb[sl, :], dsb) * scale
                else:
                    dk_s[...] += _dot(dsb, q)
                    dqt_s[qi] += _dot(jnp.where(rsel, ktb, jnp.zeros_like(ktb)), dsb) * scale
                if bias:
                    dcq_s[qi, r:r + 1, :] += jnp.sum(ds, axis=0, keepdims=True)
                    dck_s[...] -= jnp.where(li == 8 * hp + r, jnp.sum(ds, axis=1, keepdims=True), 0.0)

        @pl.when(qi > ki)
        def _():
            step(False)

        @pl.when(qi == ki)
        def _():
            step(True)

        @pl.when(qi == nq - 1)
        def _():
            dk_ref[...] = dk_s[...]
            dv_ref[...] = dv_s[...]
            if bias:
                dck_ref[...] = dck_s[...]

        @pl.when(jnp.logical_and(ki == nq - 1, qi == nq - 1))
        def _():
            for c in range(nq):
                dq_ref[c * T:(c + 1) * T, :] = dqt_s[c].T
                if bias:
                    dcq_ref[:, c * T:(c + 1) * T] = dcq_s[c]

    qmap = lambda hp, ki, qi: jnp.maximum(qi, ki)
    in_specs = [_bs((T, W), lambda hp, ki, qi: (qmap(hp, ki, qi), qoff + hp)),
                _bs((T, W), lambda hp, ki, qi: (ki, koff + hp)),
                _bs((W, T), lambda hp, ki, qi: (ktoff + hp, ki)),
                _bs((T, LANES), lambda hp, ki, qi: (ki, voff + hp)),
                _bs((T, LANES), lambda hp, ki, qi: (qmap(hp, ki, qi), hp)),
                _bs((None, 8, T), lambda hp, ki, qi: (hp, 0, qmap(hp, ki, qi))),
                _bs((None, 8, T), lambda hp, ki, qi: (hp, 0, qmap(hp, ki, qi)))]
    args = [q_arr, k_arr, kt_arr, v_arr, do_arr, lse, dvec]
    out_specs = [_bs((S, W), lambda hp, ki, qi: (0, hp)), _bs((T, W), lambda hp, ki, qi: (ki, hp)),
                 _bs((T, LANES), lambda hp, ki, qi: (ki, hp))]
    out_shape = [jax.ShapeDtypeStruct((S, nhp * W), F32), jax.ShapeDtypeStruct((S, nhp * W), F32),
                 jax.ShapeDtypeStruct((S, nhp * LANES), F32)]
    scratch = [pltpu.VMEM((nq, W, T), F32), pltpu.VMEM((T, W), F32), pltpu.VMEM((T, LANES), F32)]
    if bias:
        in_specs += [_bs((8, T), lambda hp, ki, qi: (hp, qmap(hp, ki, qi))), _bs((T, LANES), lambda hp, ki, qi: (ki, 0))]
        args += [cumT, cum]
        out_specs += [_bs((None, 8, S), lambda hp, ki, qi: (hp, 0, 0)), _bs((None, T, LANES), lambda hp, ki, qi: (hp, ki, 0))]
        out_shape += [jax.ShapeDtypeStruct((nhp, 8, S), F32), jax.ShapeDtypeStruct((nhp, S, LANES), F32)]
        scratch += [pltpu.VMEM((nq, 8, T), F32), pltpu.VMEM((T, LANES), F32)]
    return _call(body, name=name, grid=(nhp, nq, nq), in_specs=in_specs, out_specs=out_specs, out_shape=out_shape,
                 scratch=scratch, dims=("arbitrary", "arbitrary", "arbitrary"), args=args, rider=rider)


def _gate_lanes(shape):
    lane = lax.broadcasted_iota(jnp.int32, shape, 1)
    return jnp.logical_and(lane < 8 * (FOX_HEADS // 2), lane % 8 < 2)


def _fox_prep(za, b_row, name):
    S = za.shape[0]
    nrow = 8 * (FOX_HEADS // 2)

    def body(tail_ref, b_ref, cum_ref, cumt_ref):
        x = tail_ref[...] + b_ref[...]
        logf = jnp.minimum(x, 0.0) - jnp.log(1.0 + jnp.exp(-jnp.abs(x)))
        y = jnp.where(_gate_lanes((S, LANES)), logf, 0.0)
        row = lax.broadcasted_iota(jnp.int32, (S, LANES), 0)
        k = 1
        while k < S:
            y = y + jnp.where(row >= k, pltpu.roll(y, k, 0), 0.0)
            k *= 2
        cum_ref[...] = y
        cumt_ref[...] = y.T[0:nrow, :]

    return pl.pallas_call(
        body, name=name, grid=(1,),
        in_specs=[_bs((S, LANES), lambda i: (0, TAIL0 // LANES)), _bs((1, LANES), lambda i: (0, 0))],
        out_specs=[_bs((S, LANES), lambda i: (0, 0)), _bs((nrow, S), lambda i: (0, 0))],
        out_shape=[jax.ShapeDtypeStruct((S, LANES), F32), jax.ShapeDtypeStruct((nrow, S), F32)],
        compiler_params=_params(("arbitrary",)),
    )(za, b_row)


def _fox_prep_bwd(za, b_row, dcq, dck, name):
    S = za.shape[0]
    nhp = FOX_HEADS // 2
    nrow = 8 * nhp
    dcq2 = dcq.reshape(nrow, S)

    def body(tail_ref, b_ref, dcq_ref, dck_ref, dt_ref, db_ref):
        x = tail_ref[...] + b_ref[...]
        d = jnp.concatenate([dcq_ref[...], jnp.zeros((LANES - nrow, S), F32)], axis=0).T
        for hp in range(nhp):
            d = d + dck_ref[hp]
        row = lax.broadcasted_iota(jnp.int32, (S, LANES), 0)
        k = 1
        while k < S:
            d = d + jnp.where(row < S - k, pltpu.roll(d, S - k, 0), 0.0)
            k *= 2
        df = jnp.where(_gate_lanes((S, LANES)), d * jax.nn.sigmoid(-x), 0.0)
        dt_ref[...] = df
        db_ref[...] = jnp.sum(df, axis=0, keepdims=True)

    return pl.pallas_call(
        body, name=name, grid=(1,),
        in_specs=[_bs((S, LANES), lambda i: (0, TAIL0 // LANES)), _bs((1, LANES), lambda i: (0, 0)),
                  _bs((nrow, S), lambda i: (0, 0)), _bs((nhp, S, LANES), lambda i: (0, 0, 0))],
        out_specs=[_bs((S, LANES), lambda i: (0, 0)), _bs((1, LANES), lambda i: (0, 0))],
        out_shape=[jax.ShapeDtypeStruct((S, LANES), F32), jax.ShapeDtypeStruct((1, LANES), F32)],
        compiler_params=_params(("arbitrary",)),
    )(za, b_row, dcq2, dck)


def _pool_select(half, lane_lo, vals):
    return jnp.where(lane_lo, jnp.where(half == 0, vals[0], vals[2]), jnp.where(half == 0, vals[1], vals[3]))


def _pool_den(S, half, lane_lo):
    cnt = (lax.broadcasted_iota(jnp.int32, (S, LANES), 0) + 1).astype(F32)
    w = _pool_select(half, lane_lo, [float(x) for x in POOL_WINDOWS])
    return jnp.minimum(cnt, w)


def _pool_fwd(za, wbd, scale, name):
    S = za.shape[0]

    def body(u_ref, w_ref, sc_ref, y_ref, pd_ref):
        half = pl.program_id(0)
        u = u_ref[...]
        row = lax.broadcasted_iota(jnp.int32, (S, LANES), 0)
        lane_lo = lax.broadcasted_iota(jnp.int32, (S, LANES), 1) < POOL_GROUP
        sums = []
        acc = u
        k = 1
        while k < POOL_WINDOWS[-1]:
            acc = acc + jnp.where(row >= k, pltpu.roll(acc, k, 0), 0.0)
            sums.append(acc)
            k *= 2
        pooled = _pool_select(half, lane_lo, sums) / _pool_den(S, half, lane_lo)
        pd = (pooled - u).astype(BF16)
        pd_ref[...] = pd
        y_ref[...] = (_dot(pd, w_ref[...]) * sc_ref[...]).astype(BF16)

    return pl.pallas_call(
        body, name=name, grid=(2,),
        in_specs=[_bs((S, LANES), lambda i: (0, 384 // LANES + i)), _bs((None, LANES, LANES), lambda i: (i, 0, 0)),
                  _bs((1, LANES), lambda i: (0, i))],
        out_specs=[_bs((S, LANES), lambda i: (0, i)), _bs((S, LANES), lambda i: (0, i))],
        out_shape=[jax.ShapeDtypeStruct((S, POOL_WIDTH), BF16), jax.ShapeDtypeStruct((S, POOL_WIDTH), BF16)],
        compiler_params=_params(("parallel",)),
    )(za, wbd, scale)


def _pool_bwd(dyb, pd, wbd, scale, name):
    S = pd.shape[0]

    def body(dy_ref, pd_ref, w_ref, sc_ref, du_ref, dw_ref, dsc_ref):
        half = pl.program_id(0)
        dy = dy_ref[...]
        pd = pd_ref[...]
        w = w_ref[...]
        ypre = _dot(pd, w)
        dsc_ref[...] = jnp.sum(dy * ypre, axis=0, keepdims=True)
        dyp = (dy * sc_ref[...]).astype(BF16)
        dw_ref[...] = _dot_tn(pd, dyp)
        dpd = _dot_nt(dyp, w)
        row = lax.broadcasted_iota(jnp.int32, (S, LANES), 0)
        lane_lo = lax.broadcasted_iota(jnp.int32, (S, LANES), 1) < POOL_GROUP
        acc = dpd / _pool_den(S, half, lane_lo)
        sums = []
        k = 1
        while k < POOL_WINDOWS[-1]:
            acc = acc + jnp.where(row < S - k, pltpu.roll(acc, S - k, 0), 0.0)
            sums.append(acc)
            k *= 2
        du_ref[...] = _pool_select(half, lane_lo, sums) - dpd

    return pl.pallas_call(
        body, name=name, grid=(2,),
        in_specs=[_bs((S, LANES), lambda i: (0, i)), _bs((S, LANES), lambda i: (0, i)),
                  _bs((None, LANES, LANES), lambda i: (i, 0, 0)), _bs((1, LANES), lambda i: (0, i))],
        out_specs=[_bs((S, LANES), lambda i: (0, i)), _bs((None, LANES, LANES), lambda i: (i, 0, 0)),
                   _bs((1, LANES), lambda i: (0, i))],
        out_shape=[jax.ShapeDtypeStruct((S, POOL_WIDTH), F32), jax.ShapeDtypeStruct((2, LANES, LANES), F32),
                   jax.ShapeDtypeStruct((1, POOL_WIDTH), F32)],
        compiler_params=_params(("parallel",)),
    )(dyb, pd, wbd, scale)


def _mix_out_fwd(x, ya, yb, yc, w_out, name):
    S, D = x.shape
    tm = _tile(S, 512)
    K = w_out.shape[0]

    def body(x_ref, ya_ref, yb_ref, yc_ref, w_ref, xo_ref, yc_out):
        ycat = jnp.concatenate([ya_ref[...], yb_ref[...], yc_ref[...]], axis=1)
        yc_out[...] = ycat
        xo_ref[...] = x_ref[...] + _dot(ycat, w_ref[...])

    row = lambda i: (i, 0)
    return pl.pallas_call(
        body, name=name, grid=(S // tm,),
        in_specs=[_bs((tm, D), row), _bs((tm, 384), row), _bs((tm, 256), row), _bs((tm, 384), row),
                  _bs((K, D), lambda i: (0, 0))],
        out_specs=[_bs((tm, D), row), _bs((tm, K), row)],
        out_shape=[jax.ShapeDtypeStruct((S, D), F32), jax.ShapeDtypeStruct((S, K), BF16)],
        compiler_params=_params(("parallel",)),
    )(x, ya, yb, yc, w_out)


def _mix_out_bwd(dy, w_out, ya, yc, name):
    S, D = dy.shape
    tm = _tile(S, 512)
    K = w_out.shape[0]
    nhp = ya.shape[1] // LANES

    def body(dy_ref, w_ref, ya_ref, yc_ref, da_ref, db_ref, dc_ref, dva_ref, dvc_ref):
        d = _dot_nt(dy_ref[...].astype(BF16), w_ref[...])
        da = d[:, 0:384].astype(BF16)
        dc = d[:, 640:1024].astype(BF16)
        da_ref[...] = da
        db_ref[...] = d[:, 384:640]
        dc_ref[...] = dc
        li = lax.broadcasted_iota(jnp.int32, (tm, LANES), 1)
        for do, o_ref, out_ref in ((da, ya_ref, dva_ref), (dc, yc_ref, dvc_ref)):
            for hp in range(nhp):
                sl = slice(hp * LANES, (hp + 1) * LANES)
                prod = do[:, sl].astype(F32) * o_ref[:, sl].astype(F32)
                d0 = jnp.sum(jnp.where(li < 64, prod, 0.0), axis=1, keepdims=True)
                d1 = jnp.sum(jnp.where(li >= 64, prod, 0.0), axis=1, keepdims=True)
                out_ref[hp] = jnp.where(li == 0, d0, jnp.where(li == 1, d1, 0.0)).T[0:8, :]

    row = lambda i: (i, 0)
    dv_spec = _bs((nhp, 8, tm), lambda i: (0, 0, i))
    dv_shape = jax.ShapeDtypeStruct((nhp, 8, S), F32)
    return pl.pallas_call(
        body, name=name, grid=(S // tm,),
        in_specs=[_bs((tm, D), row), _bs((K, D), lambda i: (0, 0)), _bs((tm, 384), row), _bs((tm, 384), row)],
        out_specs=[_bs((tm, 384), row), _bs((tm, 256), row), _bs((tm, 384), row), dv_spec, dv_spec],
        out_shape=[jax.ShapeDtypeStruct((S, 384), BF16), jax.ShapeDtypeStruct((S, 256), F32),
                   jax.ShapeDtypeStruct((S, 384), BF16), dv_shape, dv_shape],
        compiler_params=_params(("parallel",)),
    )(dy, w_out, ya, yc)


def _loss_head(x, gam, target, name):
    S, D = x.shape
    tm = _tile(S, 512)

    def body(x_ref, gam_ref, t_ref, dx_ref, dgam_ref, loss_ref):
        i = pl.program_id(0)
        xv = x_ref[...]
        err = _rms(xv, gam_ref[...]) - t_ref[...]
        part = 0.5 * jnp.sum(jnp.mean(err * err, axis=-1, keepdims=True), axis=0, keepdims=True)
        dxn, dgam = _rms_bwd(err * (1.0 / D), xv, gam_ref[...])
        dx_ref[...] = dxn
        _accum_out(dgam_ref, i == 0, dgam)
        _accum_out(loss_ref, i == 0, jnp.broadcast_to(part, (1, LANES)))

    row = lambda i: (i, 0)
    fix = lambda i: (0, 0)
    return pl.pallas_call(
        body, name=name, grid=(S // tm,),
        in_specs=[_bs((tm, D), row), _bs((1, D), fix), _bs((tm, D), row)],
        out_specs=[_bs((tm, D), row), _bs((1, D), fix), _bs((1, LANES), fix)],
        out_shape=[jax.ShapeDtypeStruct((S, D), F32), jax.ShapeDtypeStruct((1, D), F32),
                   jax.ShapeDtypeStruct((1, LANES), F32)],
        compiler_params=_params(("arbitrary",)),
    )(x, gam, target)


def _adam_math(g, w, m, v):
    m = ADAM_B1 * m + (1.0 - ADAM_B1) * g
    v = ADAM_B2 * v + (1.0 - ADAM_B2) * (g * g)
    m_hat = m / (1.0 - ADAM_B1 ** ADAM_STEP)
    v_hat = v / (1.0 - ADAM_B2 ** ADAM_STEP)
    delta = -ADAM_LR * (m_hat / (jnp.sqrt(v_hat) + ADAM_EPS) + ADAM_WD * w)
    return delta, m, v


def _adam_sum(recv, w, m, v, layer, prev, tr, name):
    L, R, C = w.shape
    Cp = recv.shape[2]
    tr = _tile(R, tr)

    def body(r_ref, w_ref, m_ref, v_ref, *rest):
        g_out, d_out, m_out, v_out = rest[len(rest) - 4:]
        g = r_ref[0, :, 0:C].astype(F32)
        for p in range(1, N_DEV):
            g = g + r_ref[p, :, 0:C].astype(F32)
        delta, mn, vn = _adam_math(g, w_ref[...], m_ref[...], v_ref[...])
        g_out[...] = g
        d_out[...] = delta
        m_out[...] = mn
        v_out[...] = vn

    blk = _bs((None, tr, C), lambda i: (layer, i, 0))
    shp = jax.ShapeDtypeStruct((L, R, C), F32)
    in_specs = [_bs((N_DEV, tr, Cp), lambda i: (0, i, 0)), blk, blk, blk]
    args = [recv, w, m, v]
    aliases = {}
    if prev is not None:
        in_specs += [HBM_SPEC] * 4
        args += list(prev)
        aliases = {4 + k: k for k in range(4)}
    return pl.pallas_call(
        body, name=name, grid=(R // tr,),
        in_specs=in_specs, out_specs=[blk, blk, blk, blk], out_shape=[shp, shp, shp, shp],
        input_output_aliases=aliases, compiler_params=_params(("parallel",)),
    )(*args)


def _dev_index(px, py, pc):
    return 4 * px + 2 * py + pc


class _GatherRider:
    def __init__(self, shards, out_shapes, views, zero_src=None, zero_views=()):
        self.n = len(shards)
        self.views = views
        self.zero_views = list(zero_views) if zero_src is not None else []
        self.srcs = list(shards) + ([zero_src] if self.zero_views else [])
        self.out_shapes = list(out_shapes)
        n, nz = self.n, len(self.zero_views)
        self.scratch = [pltpu.SemaphoreType.DMA((n, 7)), pltpu.SemaphoreType.DMA((n, 7)),
                        pltpu.SemaphoreType.DMA((n,)), pltpu.SemaphoreType.DMA((max(nz, 1),))]

    def _copies(self, ins, outs, sems):
        n = self.n
        send_sems, recv_sems, local_sems, zero_sems = sems
        x, y, c = lax.axis_index("x"), lax.axis_index("y"), lax.axis_index("c")
        me, sibling = (x, y, c), (x, y, 1 - c)
        chips = [(1 - x, y), (x, 1 - y), (1 - x, 1 - y)]

        def rows(a, blk):
            return self.views[a](outs[a], _dev_index(*blk))

        def copy(a, k, blk, to, src=None):
            return pltpu.make_async_remote_copy(
                src_ref=rows(a, blk) if src is None else src, dst_ref=rows(a, blk),
                send_sem=send_sems.at[a, k], recv_sem=recv_sems.at[a, k], device_id=to, device_id_type=MESH_ID)

        local = [pltpu.make_async_copy(ins[a], rows(a, me), local_sems.at[a]) for a in range(n)]
        local += [pltpu.make_async_copy(ins[n], view(outs[a]), zero_sems.at[i])
                  for i, (a, view) in enumerate(self.zero_views)]
        first = []
        for a in range(n):
            first.append(copy(a, 0, me, sibling, src=ins[a]))
            first += [copy(a, 1 + j, me, (*chip, c), src=ins[a]) for j, chip in enumerate(chips)]
        over_ici = [[copy(a, 1 + j, (*chip, c), me) for a in range(n)] for j, chip in enumerate(chips)]
        passed = [[copy(a, 4 + j, (*chip, c), sibling) for a in range(n)] for j, chip in enumerate(chips)]
        from_sibling = [copy(a, 0, sibling, me) for a in range(n)]
        from_sibling += [copy(a, 4 + j, (*chip, 1 - c), me) for a in range(n) for j, chip in enumerate(chips)]
        return local, first, over_ici, passed, from_sibling

    def begin(self, ins, outs, sems):
        local, first, _, _, _ = self._copies(ins, outs, sems)
        for cp in local + first:
            cp.start()

    def middle(self, ins, outs, sems):
        _, _, over_ici, passed, _ = self._copies(ins, outs, sems)
        for arrived, onward in zip(over_ici, passed):
            for cp, fwd in zip(arrived, onward):
                cp.wait_recv()
                fwd.start()

    def end(self, ins, outs, sems):
        local, first, _, passed, from_sibling = self._copies(ins, outs, sems)
        for cp in from_sibling:
            cp.wait_recv()
        for cp in first + [fwd for onward in passed for fwd in onward]:
            cp.wait_send()
        for cp in local:
            cp.wait()


class _ScatterRider:
    _MASKS = [(kx, ky, kc) for kx in (0, 1) for ky in (0, 1) for kc in (0, 1)][1:]

    def __init__(self, srcs, out_shapes, src_of, dst_at):
        self.n = len(srcs)
        self.srcs = list(srcs)
        self.out_shapes = list(out_shapes)
        self.src_of = src_of
        self.dst_at = dst_at
        n = self.n
        self.scratch = [pltpu.SemaphoreType.DMA((n, 7)), pltpu.SemaphoreType.DMA((n, 7)), pltpu.SemaphoreType.DMA((n,))]

    def _copies(self, ins, outs, sems):
        send_sems, recv_sems, local_sems = sems
        x, y, c = lax.axis_index("x"), lax.axis_index("y"), lax.axis_index("c")
        my = _dev_index(x, y, c)
        peers = [(1 - x if kx else x, 1 - y if ky else y, 1 - c if kc else c) for kx, ky, kc in self._MASKS]

        def send(i, k, to):
            return pltpu.make_async_remote_copy(
                src_ref=self.src_of[i](ins[i], _dev_index(*to)), dst_ref=self.dst_at[i](outs[i], my),
                send_sem=send_sems.at[i, k], recv_sem=recv_sems.at[i, k], device_id=to, device_id_type=MESH_ID)

        def arrival(i, k, frm):
            slot = self.dst_at[i](outs[i], _dev_index(*frm))
            return pltpu.make_async_remote_copy(
                src_ref=slot, dst_ref=slot, send_sem=send_sems.at[i, k], recv_sem=recv_sems.at[i, k],
                device_id=frm, device_id_type=MESH_ID)

        local = [pltpu.make_async_copy(self.src_of[i](ins[i], my), self.dst_at[i](outs[i], my), local_sems.at[i])
                 for i in range(self.n)]
        sends = [send(i, k, to) for k, to in enumerate(peers) for i in range(self.n)]
        arrivals = [arrival(i, k, frm) for k, frm in enumerate(peers) for i in range(self.n)]
        return local, sends, arrivals

    def begin(self, ins, outs, sems):
        local, sends, _ = self._copies(ins, outs, sems)
        for cp in local + sends:
            cp.start()

    def middle(self, ins, outs, sems):
        pass

    def end(self, ins, outs, sems):
        local, sends, arrivals = self._copies(ins, outs, sems)
        for cp in arrivals:
            cp.wait_recv()
        for cp in sends:
            cp.wait_send()
        for cp in local:
            cp.wait()


def _comm_call(rider, name):
    k_in = len(rider.srcs)
    k_out = len(rider.out_shapes)

    def body(*refs):
        ins, outs, sems = refs[:k_in], refs[k_in:k_in + k_out], refs[k_in + k_out:]
        rider.begin(ins, outs, sems)
        rider.middle(ins, outs, sems)
        rider.end(ins, outs, sems)

    return pl.pallas_call(
        body, name=name, in_specs=[HBM_SPEC] * k_in, out_specs=[HBM_SPEC] * k_out, out_shape=rider.out_shapes,
        scratch_shapes=rider.scratch, compiler_params=pltpu.CompilerParams(has_side_effects=True),
    )(*rider.srcs)


def _pad_w_in(w):
    take = jnp.take(w, np.maximum(_IN_PERM, 0), axis=-1)
    return jnp.where(_IN_PERM >= 0, take, jnp.zeros_like(take))


def _unpad_w_in(g):
    return jnp.take(g, _IN_INV, axis=-1)


def _small_pack(w_q_b, w_kv_b):
    a = jnp.pad(w_q_b, ((0, 0), (0, 0), (0, LANES - w_q_b.shape[2])))
    b = jnp.pad(w_kv_b, ((0, 0), (0, 0), (0, LANES - w_kv_b.shape[2])))
    return jnp.concatenate([a, b], axis=1)


def _small_unpack(p, cq, ckv):
    return p[:, 0:MLA_Q_RANK, 0:cq], p[:, MLA_Q_RANK:, 0:ckv]


def _mla_weights(wsm):
    H = MLA_HEADS
    cq = H * (MLA_NOPE + MLA_ROPE) // N_DEV
    ckv = H * (MLA_NOPE + MLA_V) // N_DEV
    wq = wsm[:, 0:MLA_Q_RANK, 0:cq].transpose(1, 0, 2).reshape(MLA_Q_RANK, H, MLA_NOPE + MLA_ROPE)
    wq = jnp.pad(wq, ((0, 0), (0, 0), (0, HEAD_BLOCK - MLA_NOPE - MLA_ROPE))).reshape(MLA_Q_RANK, H * HEAD_BLOCK)
    wkv = wsm[:, MLA_Q_RANK:, 0:ckv].transpose(1, 0, 2).reshape(MLA_KV_RANK, H, MLA_NOPE + MLA_V)
    wk = jnp.pad(wkv[:, :, 0:MLA_NOPE], ((0, 0), (0, 0), (0, HEAD_BLOCK - MLA_NOPE))).reshape(MLA_KV_RANK, H * HEAD_BLOCK)
    wv = wkv[:, :, MLA_NOPE:].reshape(MLA_KV_RANK, H * MLA_V)
    return wq, jnp.concatenate([wk, wv], axis=1)


def _mla_grads_to_blocks(dwq, dwkv):
    H = MLA_HEADS
    gq = dwq.reshape(MLA_Q_RANK, H, HEAD_BLOCK)[:, :, 0:MLA_NOPE + MLA_ROPE].reshape(MLA_Q_RANK, N_DEV, -1)
    gk = dwkv[:, 0:H * HEAD_BLOCK].reshape(MLA_KV_RANK, H, HEAD_BLOCK)[:, :, 0:MLA_NOPE]
    gv = dwkv[:, H * HEAD_BLOCK:].reshape(MLA_KV_RANK, H, MLA_V)
    gkv = jnp.concatenate([gk, gv], axis=2).reshape(MLA_KV_RANK, N_DEV, -1)
    return _small_pack(gq.transpose(1, 0, 2), gkv.transpose(1, 0, 2)).astype(BF16)


def _pool_blockdiag(pool_w):
    z = jnp.zeros((POOL_GROUP, POOL_GROUP), pool_w.dtype)
    halves = [jnp.concatenate([jnp.concatenate([pool_w[2 * i], z], axis=1),
                               jnp.concatenate([z, pool_w[2 * i + 1]], axis=1)], axis=0) for i in range(2)]
    return jnp.stack(halves)


def _pool_blockdiag_t(dw):
    g = POOL_GROUP
    return jnp.stack([dw[0, 0:g, 0:g], dw[0, g:, g:], dw[1, 0:g, 0:g], dw[1, g:, g:]])


def _gate_row(b):
    return jnp.zeros((LANES,), b.dtype).at[_F_LANES].set(b).reshape(1, LANES)


_SMALL = ("ffn1_norm", "mix_norm", "q_a_norm", "kv_a_norm", "pool_w", "pool_scale", "fox_b_f", "ffn2_norm", "final_norm")


def _pack_small(tree):
    rows, recipe = [], []
    for name in _SMALL:
        a = tree[name]
        flat = a.reshape(-1)
        n = flat.shape[0]
        nrow = -(-n // (8 * LANES)) * 8
        flat = jnp.pad(flat, (0, nrow * LANES - n))
        rows.append(flat.reshape(nrow, LANES))
        recipe.append((name, a.shape, n, nrow))
    return jnp.concatenate(rows, axis=0), recipe


def _unpack_small(packed, recipe):
    out, r0 = {}, 0
    for name, shape, n, nrow in recipe:
        out[name] = packed[r0:r0 + nrow].reshape(-1)[0:n].reshape(shape)
        r0 += nrow
    return out


def _adam_small(packs, w, m, v, name):
    R = w.shape[0]

    def body(p_ref, w_ref, m_ref, v_ref, g_out, d_out, m_out, v_out):
        g = p_ref[0]
        for p in range(1, N_DEV):
            g = g + p_ref[p]
        delta, mn, vn = _adam_math(g, w_ref[...], m_ref[...], v_ref[...])
        g_out[...] = g
        d_out[...] = delta
        m_out[...] = mn
        v_out[...] = vn

    blk = _bs((R, LANES), lambda i: (0, 0))
    shp = jax.ShapeDtypeStruct((R, LANES), F32)
    return pl.pallas_call(
        body, name=name, grid=(1,),
        in_specs=[_bs((N_DEV, R, LANES), lambda i: (0, 0, 0)), blk, blk, blk],
        out_specs=[blk, blk, blk, blk], out_shape=[shp, shp, shp, shp],
        compiler_params=_params(("arbitrary",)),
    )(packs, w, m, v)


_GATHER_PLAN = {
    ("first", 0): (("ffn1_w_gu", 0), ("ffn1_w_down", 0)),
    ("ffn1_fwd", 0): (("w_in", 0), ("w_small", 0), ("w_out", 0), ("ffn2_w_down", 0)),
    ("mla_attn_fwd", 0): (("ffn2_w_gu", 0), ("ffn1_w_gu", 1)),
    ("fox_attn_fwd", 0): (("ffn1_w_down", 1), ("w_in", 1), ("w_small", 1), ("w_out", 1)),
    ("mla_attn_fwd", 1): (("ffn2_w_gu", 1), ("ffn2_w_down", 1)),
}
_SCATTER_PLAN = {
    ("mla_attn_bwd", 1): (("ffn2_w_gu", 1), ("w_out", 1)),
    ("fox_attn_bwd", 1): (("ffn2_w_down", 1),),
    ("ffn1_bwd", 1): (("w_in", 1), ("w_small", 1)),
    ("ffn2_bwd", 0): (("ffn1_w_down", 1),),
    ("mla_attn_bwd", 0): (("ffn1_w_gu", 1), ("w_out", 0)),
    ("fox_attn_bwd", 0): (("ffn2_w_gu", 0), ("ffn2_w_down", 0)),
    ("ffn1_bwd_a", 0): (("w_in", 0), ("w_small", 0)),
    ("ffn1_dwgu", 0): (("ffn1_w_down", 0),),
    ("ffn1_bwd_b", 0): (("ffn1_w_gu", 0),),
}
_SPLIT_BWD = (("ffn1", 0),)


class _Exchange:
    def __init__(self, shards, D, f_sh, r_in, r_out):
        self.shards = shards
        self.D, self.f_sh, self.r_in, self.r_out = D, f_sh, r_in, r_out
        self.weights, self.grads, self.recv = {}, {}, {}

    def _rows(self, kind):
        n = {"w_gu": 2 * self.f_sh, "w_down": self.f_sh}[kind]
        return lambda ref, p: ref.at[pl.ds(pl.multiple_of(p * n, 16), n)]

    def _gathered_shape(self, kind):
        D, f_sh = self.D, self.f_sh
        return {"w_gu": (N_DEV * 2 * f_sh, D), "w_down": (N_DEV * f_sh, D), "w_in": (N_DEV, self.r_in, N_PAD),
                "w_small": (N_DEV, MLA_Q_RANK + MLA_KV_RANK, LANES), "w_out": (N_DEV, self.r_out, D)}[kind]

    def _recv_shape(self, kind):
        D, f_sh = self.D, self.f_sh
        return {"w_gu": (N_DEV, 2 * f_sh, D), "w_down": (N_DEV, f_sh, D), "w_in": (N_DEV, self.r_in, N_IN),
                "w_small": (N_DEV, MLA_Q_RANK + MLA_KV_RANK, LANES), "w_out": (N_DEV, self.r_out, D)}[kind]

    @staticmethod
    def _kind(name):
        return name[5:] if name.startswith("ffn") else name

    def gather_rider(self, call, l):
        keys = _GATHER_PLAN.get((call, l))
        if not keys:
            return None
        by_dev = lambda ref, p: ref.at[p]
        shards, shapes, views = [], [], []
        for key in keys:
            kind = self._kind(key[0])
            shards.append(self.shards[key])
            shapes.append(jax.ShapeDtypeStruct(self._gathered_shape(kind), BF16))
            views.append(self._rows(kind) if kind in ("w_gu", "w_down") else by_dev)
        return _GatherRider(shards, shapes, views)

    def gathered(self, call, l, outs):
        for key, w in zip(_GATHER_PLAN.get((call, l), ()), outs):
            if self._kind(key[0]) in ("w_in", "w_out"):
                w = w.reshape((N_DEV * w.shape[1],) + w.shape[2:])
            self.weights[key] = w

    def scatter_rider(self, call, l, pack=None):
        keys = _SCATTER_PLAN.get((call, l), ())
        if not keys and pack is None:
            return None
        by_dev = lambda ref, p: ref.at[p]
        srcs, shapes, src_of = [], [], []
        for key in keys:
            kind = self._kind(key[0])
            srcs.append(self.grads[key])
            shapes.append(jax.ShapeDtypeStruct(self._recv_shape(kind), BF16))
            src_of.append(self._rows(kind) if kind in ("w_gu", "w_down") else by_dev)
        if pack is not None:
            srcs.append(pack)
            shapes.append(jax.ShapeDtypeStruct((N_DEV,) + pack.shape, pack.dtype))
            src_of.append(lambda ref, p: ref)
        return _ScatterRider(srcs, shapes, src_of, [by_dev] * len(srcs))

    def scattered(self, call, l, outs):
        for key, r in zip(_SCATTER_PLAN.get((call, l), ()), outs):
            self.recv[key] = r


def _local_step(x, target, ex, small):
    S, D = x.shape
    tabs = _rope_tables(S)
    nhp_a, nhp_c = MLA_HEADS // 2, FOX_HEADS // 2
    fox_scale = 1.0 / math.sqrt(FOX_HEAD_DIM)
    ex.gathered("first", 0, _comm_call(ex.gather_rider("first", 0), "gather_first"))
    saved = []
    for l in range(DEPTH):
        s = {}
        s["x0"] = x
        wgu1, wd1 = ex.weights[("ffn1_w_gu", l)], ex.weights[("ffn1_w_down", l)]
        (x1, s["h1"], s["gu1"]), got = _ffn_fwd_full(x, small["ffn1_norm"][l][None], wgu1, wd1, FFN_FWD_TOKENS,
                                                    f"ffn1_fwd_l{l}", rider=ex.gather_rider("ffn1_fwd", l))
        ex.gathered("ffn1_fwd", l, got)
        s["x1"] = x1
        w_in = ex.weights[("w_in", l)]
        s["h2"], za, zf, zkvt = _mix_in_fwd(x1, small["mix_norm"][l][None], w_in, f"mix_in_fwd_l{l}")
        s["za"], s["zf"], s["zkvt"] = za, zf, zkvt
        wq, wkv = _mla_weights(ex.weights[("w_small", l)])
        s["wq"], s["wkv"] = wq, wkv
        gq, gkv = small["q_a_norm"][l][None], small["kv_a_norm"][l][None]
        qf, kf, vm, kft, vmt = _mla_prep(za, gq, gkv, wq, wkv, tabs, f"mla_prep_l{l}")
        s["qf"], s["kf"], s["vm"], s["kft"] = qf, kf, vm, kft
        (ya, lse_a), got = _attn_fwd_t(qf, kf, vmt, nhp=nhp_a, dkb=LANES, qoff=0, koff=0, vtoff=0, scale=1.0,
                                       cum=None, cumT=None, name=f"mla_attn_fwd_l{l}",
                                       rider=ex.gather_rider("mla_attn_fwd", l))
        ex.gathered("mla_attn_fwd", l, got)
        s["ya"], s["lse_a"] = ya, lse_a
        b_row = _gate_row(small["fox_b_f"][l])
        s["b_row"] = b_row
        cum, cumT = _fox_prep(za, b_row, f"fox_prep_l{l}")
        s["cum"], s["cumT"] = cum, cumT
        (yc, lse_c), got = _attn_fwd_t(zf, zf, zkvt, nhp=nhp_c, dkb=64, qoff=0, koff=nhp_c, vtoff=nhp_c, scale=fox_scale,
                                       cum=cum, cumT=cumT, name=f"fox_attn_fwd_l{l}",
                                       rider=ex.gather_rider("fox_attn_fwd", l))
        ex.gathered("fox_attn_fwd", l, got)
        s["yc"], s["lse_c"] = yc, lse_c
        wbd = _pool_blockdiag(small["pool_w"][l]).astype(BF16)
        s["wbd"] = wbd
        psc = small["pool_scale"][l][None]
        yb, s["pd"] = _pool_fwd(za, wbd, psc, f"pool_fwd_l{l}")
        w_out = ex.weights[("w_out", l)]
        x2, s["ycat"] = _mix_out_fwd(x1, ya, yb, yc, w_out, f"mix_out_fwd_l{l}")
        s["x2"] = x2
        wgu2, wd2 = ex.weights[("ffn2_w_gu", l)], ex.weights[("ffn2_w_down", l)]
        (x, s["h3"], s["gu2"]), _ = _ffn_fwd_full(x2, small["ffn2_norm"][l][None], wgu2, wd2, FFN_FWD_TOKENS,
                                                  f"ffn2_fwd_l{l}")
        saved.append(s)

    dx, d_final, loss = _loss_head(x, small["final_norm"][None], target, "loss_head")

    small_grads = [None] * DEPTH
    for l in reversed(range(DEPTH)):
        s = saved[l]
        g = {}
        wgu2, wd2 = ex.weights[("ffn2_w_gu", l)], ex.weights[("ffn2_w_down", l)]
        dy3 = dx
        (dx, g["ffn2_norm"], dgu, act, dyh), got = _ffn_bwd_full(
            dy3, s["x2"], small["ffn2_norm"][l][None], s["gu2"], wgu2, wd2, FFN_BWD_TOKENS, f"ffn2_bwd_l{l}",
            rider=ex.scatter_rider("ffn2_bwd", l))
        ex.scattered("ffn2_bwd", l, got)
        ex.grads[("ffn2_w_gu", l)] = _ffn_weight_grad(dgu, s["h3"], f"ffn2_dwgu_l{l}")
        ex.grads[("ffn2_w_down", l)] = _ffn_weight_grad(act, dyh, f"ffn2_dwd_l{l}")

        w_out = ex.weights[("w_out", l)]
        dya, dyb, dyc, dvec_a, dvec_c = _mix_out_bwd(dx, w_out, s["ya"], s["yc"], f"mix_out_bwd_l{l}")
        dw_out = _mm_tn(s["ycat"][None], dx[None], 1, lambda p: 0, lambda p: 0, 1024, 1024, f"dwout_l{l}", ts=1024)[0]
        ex.grads[("w_out", l)] = dw_out.reshape(N_DEV, ex.r_out, D)

        (dqf, dkf, dvm), got = _attn_bwd_t(s["qf"], s["kf"], s["kft"], s["vm"], dya, s["lse_a"], dvec_a, nhp=nhp_a,
                                           dkb=LANES, qoff=0, koff=0, ktoff=0, voff=0, scale=1.0, cum=None, cumT=None,
                                           name=f"mla_attn_bwd_l{l}", rider=ex.scatter_rider("mla_attn_bwd", l))
        ex.scattered("mla_attn_bwd", l, got)
        zf = s["zf"]
        (dqc, dkc, dvc, dcq, dck), got = _attn_bwd_t(zf, zf, s["zkvt"], zf, dyc, s["lse_c"], dvec_c, nhp=nhp_c, dkb=64,
                                                     qoff=0, koff=nhp_c, ktoff=0, voff=2 * nhp_c, scale=fox_scale,
                                                     cum=s["cum"], cumT=s["cumT"], name=f"fox_attn_bwd_l{l}",
                                                     rider=ex.scatter_rider("fox_attn_bwd", l))
        ex.scattered("fox_attn_bwd", l, got)
        dtail_f, db = _fox_prep_bwd(s["za"], s["b_row"], dcq, dck, f"fox_prep_bwd_l{l}")
        g["fox_b_f"] = db[0, _F_LANES]
        psc = small["pool_scale"][l][None]
        du, dwbd, dpsc = _pool_bwd(dyb, s["pd"], s["wbd"], psc, f"pool_bwd_l{l}")
        g["pool_w"] = _pool_blockdiag_t(dwbd)
        g["pool_scale"] = dpsc[0]
        gq, gkv = small["q_a_norm"][l][None], small["kv_a_norm"][l][None]
        dza, dwq, dwkv, dgq, dgkv = _mla_prep_bwd(s["za"], gq, gkv, s["wq"], s["wkv"], tabs, dqf, dkf, dvm,
                                                   f"mla_prep_bwd_l{l}")
        g["q_a_norm"], g["kv_a_norm"] = dgq[0], dgkv[0]
        ex.grads[("w_small", l)] = _mla_grads_to_blocks(dwq, dwkv)
        w_in = ex.weights[("w_in", l)]
        dx, g["mix_norm"], dz = _mix_in_bwd(dx, s["x1"], small["mix_norm"][l][None], dza, du, dtail_f, dqc, dkc, dvc,
                                            w_in, f"mix_in_bwd_l{l}")
        dw_in = _unpad_w_in(_mm_tn(s["h2"][None], dz[None], 1, lambda p: 0, lambda p: 0, 1024, 640, f"dwin_l{l}",
                                   ts=4096)[0])
        ex.grads[("w_in", l)] = dw_in.reshape(N_DEV, ex.r_in, N_IN)

        wgu1, wd1 = ex.weights[("ffn1_w_gu", l)], ex.weights[("ffn1_w_down", l)]
        dy1 = dx
        gam1 = small["ffn1_norm"][l][None]
        split = ("ffn1", l) in _SPLIT_BWD
        if split:
            (dgu, act, dyh), got = _ffn_bwd_full(dy1, None, None, s["gu1"], None, wd1, FFN_BWD_TOKENS, f"ffn1_bwd_a_l{l}",
                                                 phase="act", rider=ex.scatter_rider("ffn1_bwd_a", l))
            ex.scattered("ffn1_bwd_a", l, got)
        else:
            (dx, g["ffn1_norm"], dgu, act, dyh), got = _ffn_bwd_full(
                dy1, s["x0"], gam1, s["gu1"], wgu1, wd1, FFN_BWD_TOKENS, f"ffn1_bwd_l{l}",
                rider=ex.scatter_rider("ffn1_bwd", l))
            ex.scattered("ffn1_bwd", l, got)
        ex.grads[("ffn1_w_down", l)] = _ffn_weight_grad(act, dyh, f"ffn1_dwd_l{l}")
        rider = ex.scatter_rider("ffn1_dwgu", l)
        dwgu = _ffn_weight_grad(dgu, s["h1"], f"ffn1_dwgu_l{l}", rider=rider)
        if rider is not None:
            dwgu, got = dwgu
            ex.scattered("ffn1_dwgu", l, got)
        ex.grads[("ffn1_w_gu", l)] = dwgu
        if split:
            (dx, g["ffn1_norm"]), got = _ffn_bwd_full(dy1, s["x0"], gam1, None, wgu1, None, FFN_BWD_TOKENS,
                                                     f"ffn1_bwd_b_l{l}", phase="in", dgu_in=dgu,
                                                     rider=ex.scatter_rider("ffn1_bwd_b", l))
            ex.scattered("ffn1_bwd_b", l, got)
        for k in ("ffn1_norm", "ffn2_norm", "mix_norm"):
            g[k] = g[k][0]
        small_grads[l] = g
    return loss, dx, small_grads, d_final[0]


_BIG = ("ffn1_w_gu", "ffn1_w_down", "w_in", "w_small", "w_out", "ffn2_w_gu", "ffn2_w_down")


def kernel(x, ffn1_norm, ffn1_w_gu, ffn1_w_down, mix_norm, w_in, q_a_norm, w_q_b, kv_a_norm, w_kv_b, pool_w, pool_scale, fox_b_f, w_out, ffn2_norm, ffn2_w_gu, ffn2_w_down, final_norm, loss_target, m_ffn1_norm, m_ffn1_w_gu, m_ffn1_w_down, m_mix_norm, m_w_in, m_q_a_norm, m_w_q_b, m_kv_a_norm, m_w_kv_b, m_pool_w, m_pool_scale, m_fox_b_f, m_w_out, m_ffn2_norm, m_ffn2_w_gu, m_ffn2_w_down, m_final_norm, v_ffn1_norm, v_ffn1_w_gu, v_ffn1_w_down, v_mix_norm, v_w_in, v_q_a_norm, v_w_q_b, v_kv_a_norm, v_w_kv_b, v_pool_w, v_pool_scale, v_fox_b_f, v_w_out, v_ffn2_norm, v_ffn2_w_gu, v_ffn2_w_down, v_final_norm):
    W = dict(ffn1_norm=ffn1_norm, ffn1_w_gu=ffn1_w_gu, ffn1_w_down=ffn1_w_down, mix_norm=mix_norm, w_in=w_in,
             q_a_norm=q_a_norm, w_q_b=w_q_b, kv_a_norm=kv_a_norm, w_kv_b=w_kv_b, pool_w=pool_w, pool_scale=pool_scale,
             fox_b_f=fox_b_f, w_out=w_out, ffn2_norm=ffn2_norm, ffn2_w_gu=ffn2_w_gu, ffn2_w_down=ffn2_w_down,
             final_norm=final_norm)
    M = dict(ffn1_norm=m_ffn1_norm, ffn1_w_gu=m_ffn1_w_gu, ffn1_w_down=m_ffn1_w_down, mix_norm=m_mix_norm, w_in=m_w_in,
             q_a_norm=m_q_a_norm, w_q_b=m_w_q_b, kv_a_norm=m_kv_a_norm, w_kv_b=m_w_kv_b, pool_w=m_pool_w,
             pool_scale=m_pool_scale, fox_b_f=m_fox_b_f, w_out=m_w_out, ffn2_norm=m_ffn2_norm, ffn2_w_gu=m_ffn2_w_gu,
             ffn2_w_down=m_ffn2_w_down, final_norm=m_final_norm)
    V = dict(ffn1_norm=v_ffn1_norm, ffn1_w_gu=v_ffn1_w_gu, ffn1_w_down=v_ffn1_w_down, mix_norm=v_mix_norm, w_in=v_w_in,
             q_a_norm=v_q_a_norm, w_q_b=v_w_q_b, kv_a_norm=v_kv_a_norm, w_kv_b=v_w_kv_b, pool_w=v_pool_w,
             pool_scale=v_pool_scale, fox_b_f=v_fox_b_f, w_out=v_w_out, ffn2_norm=v_ffn2_norm, ffn2_w_gu=v_ffn2_w_gu,
             ffn2_w_down=v_ffn2_w_down, final_norm=v_final_norm)
    L, D, n_sh = ffn1_w_gu.shape
    f_sh = ffn1_w_down.shape[1]
    assert n_sh == 2 * f_sh and L == DEPTH
    r_in, r_out = w_in.shape[1], w_out.shape[1]

    tr_in = lambda a: a.transpose(0, 2, 1)
    big_shards = dict(
        ffn1_w_gu=tr_in(ffn1_w_gu).astype(BF16), ffn1_w_down=ffn1_w_down.astype(BF16),
        w_in=_pad_w_in(w_in).astype(BF16), w_small=_small_pack(w_q_b, w_kv_b).astype(BF16), w_out=w_out.astype(BF16),
        ffn2_w_gu=tr_in(ffn2_w_gu).astype(BF16), ffn2_w_down=ffn2_w_down.astype(BF16))
    ex = _Exchange({(k, l): big_shards[k][l] for k in _BIG for l in range(L)}, D, f_sh, r_in, r_out)

    small = {k: W[k] for k in _SMALL}
    loss, dx, grads, d_final = _local_step(x[0], loss_target[0], ex, small)

    small_g = {k: jnp.stack([grads[l][k] for l in range(L)]) for k in _SMALL if k != "final_norm"}
    small_g["final_norm"] = d_final
    pack_g, recipe = _pack_small(small_g)
    n_small = pack_g.shape[0]
    loss_row = -(-n_small // 8) * 8
    pack_g = jnp.concatenate([pack_g, jnp.zeros((loss_row - n_small, LANES), F32), jnp.broadcast_to(loss, (8, LANES))],
                             axis=0)
    *got, packs = _comm_call(ex.scatter_rider("last", 0, pack=pack_g), "scatter_last")
    ex.scattered("last", 0, got)

    out = {}
    sm_w, sm_m, sm_v = (_small_pack(t["w_q_b"], t["w_kv_b"]) for t in (W, M, V))
    big = [("ffn1_w_gu", tr_in(W["ffn1_w_gu"]), tr_in(M["ffn1_w_gu"]), tr_in(V["ffn1_w_gu"]), f_sh),
           ("ffn1_w_down", W["ffn1_w_down"], M["ffn1_w_down"], V["ffn1_w_down"], 352),
           ("w_in", W["w_in"], M["w_in"], V["w_in"], 128),
           ("w_small", sm_w, sm_m, sm_v, 384),
           ("w_out", W["w_out"], M["w_out"], V["w_out"], 128),
           ("ffn2_w_gu", tr_in(W["ffn2_w_gu"]), tr_in(M["ffn2_w_gu"]), tr_in(V["ffn2_w_gu"]), f_sh),
           ("ffn2_w_down", W["ffn2_w_down"], M["ffn2_w_down"], V["ffn2_w_down"], 352)]
    for k, w_, m_, v_, tr in big:
        res = None
        for l in range(L):
            res = _adam_sum(ex.recv[(k, l)], w_, m_, v_, l, res, tr, f"adam_{k}_l{l}")
        if k == "w_small":
            cq, ckv = w_q_b.shape[2], w_kv_b.shape[2]
            parts = [_small_unpack(r, cq, ckv) for r in res]
            out["w_q_b"] = [p[0] for p in parts]
            out["w_kv_b"] = [p[1] for p in parts]
        elif k.endswith("w_gu"):
            out[k] = [tr_in(r) for r in res]
        else:
            out[k] = res

    pw, _ = _pack_small({k: W[k] for k in _SMALL})
    pm, _ = _pack_small({k: M[k] for k in _SMALL})
    pv, _ = _pack_small({k: V[k] for k in _SMALL})
    extra = ((0, loss_row + 8 - n_small), (0, 0))
    res = _adam_small(packs, jnp.pad(pw, extra), jnp.pad(pm, extra), jnp.pad(pv, extra), "adam_small")
    loss_total = res[0][loss_row, 0]
    small_out = [_unpack_small(r, recipe) for r in res]
    for k in _SMALL:
        out[k] = [t[k] for t in small_out]

    names = ["ffn1_norm", "ffn1_w_gu", "ffn1_w_down", "mix_norm", "w_in", "q_a_norm", "w_q_b", "kv_a_norm", "w_kv_b",
             "pool_w", "pool_scale", "fox_b_f", "w_out", "ffn2_norm", "ffn2_w_gu", "ffn2_w_down", "final_norm"]
    outs = [loss_total, dx[None]]
    for which in range(4):
        outs += [out[k][which] for k in names]
    return tuple(outs)
```

```python
import functools
import math

import numpy as np
import jax
import jax.numpy as jnp
from jax import lax
from jax.experimental import pallas as pl
from jax.experimental.pallas import tpu as pltpu

F32 = jnp.float32
BF16 = jnp.bfloat16
MESH_ID = pl.DeviceIdType.MESH

N_DEV = 8
EPS = 1e-6
DEPTH = 2

MLA_HEADS = 6
MLA_Q_RANK = 256
MLA_KV_RANK = 128
MLA_NOPE = 64
MLA_ROPE = 32
MLA_V = 64
ROPE_THETA = 10000.0
POOL_WINDOWS = (2, 4, 8, 16)
POOL_GROUP = 64
POOL_WIDTH = 256
FOX_HEADS = 6
FOX_HEAD_DIM = 64
N_IN = 1830

ADAM_LR = 0.001
ADAM_B1 = 0.9
ADAM_B2 = 0.999
ADAM_EPS = 1e-08
ADAM_WD = 0.01
ADAM_STEP = 10

LANES = 128
HEAD_BLOCK = 128
VMEM_LIMIT = 48 * 1024 * 1024
VMEM_LIMIT_BIG = 50 * 1024 * 1024
NEG = -1e30
ATTN_BLOCK = 1024

ZA = 768
ZF = 1152
N_PAD = ZA + ZF
TAIL0 = 640
ROPE_LANE0 = 64


def _f_lane(h):
    return 8 * (h // 2) + (h % 2)


_F_LANES = np.array([_f_lane(h) for h in range(FOX_HEADS)], np.int32)


def _dot(a, b):
    return jnp.dot(a, b, preferred_element_type=F32)


def _dot_nt(a, b):
    return lax.dot_general(a, b, (((1,), (1,)), ((), ())), preferred_element_type=F32)


def _dot_tn(a, b):
    return lax.dot_general(a, b, (((0,), (0,)), ((), ())), preferred_element_type=F32)


def _rms(x, gam):
    r = lax.rsqrt(jnp.mean(x * x, axis=-1, keepdims=True) + EPS)
    return x * r * gam


def _rms_bwd(dy, x, gam):
    r = lax.rsqrt(jnp.mean(x * x, axis=-1, keepdims=True) + EPS)
    xh = x * r
    dxh = dy * gam
    dx = r * (dxh - xh * jnp.mean(dxh * xh, axis=-1, keepdims=True))
    return dx, jnp.sum(dy * xh, axis=0, keepdims=True)


def _accum_out(ref, first, val):
    @pl.when(first)
    def _():
        ref[...] = val

    @pl.when(jnp.logical_not(first))
    def _():
        ref[...] += val


def _bs(shape, fn):
    return pl.BlockSpec(shape, fn)


def _params(dims, vmem=VMEM_LIMIT):
    return pltpu.CompilerParams(dimension_semantics=dims, vmem_limit_bytes=vmem)


def _tile(n, t):
    t = min(n, t)
    assert n % t == 0, (n, t)
    return t


HBM_SPEC = pl.BlockSpec(memory_space=pl.ANY)


def _call(body, *, name, grid, in_specs, out_specs, out_shape, scratch, dims, args, rider=None, vmem=VMEM_LIMIT):
    n_in, n_out = len(in_specs), len(out_specs)
    if rider is None:
        outs = pl.pallas_call(body, name=name, grid=grid, in_specs=in_specs, out_specs=out_specs, out_shape=out_shape,
                              scratch_shapes=scratch, compiler_params=_params(dims, vmem))(*args)
        return list(outs), []
    k_in, k_out, k_sem = len(rider.srcs), len(rider.out_shapes), len(rider.scratch)

    def riding(*refs):
        a, b, c, d = n_in, n_in + k_in, n_in + k_in + n_out, n_in + k_in + n_out + k_out
        rest = refs[d:]
        sems = rest[len(rest) - k_sem:]
        step = 0
        for i, g in enumerate(grid):
            step = step * g + pl.program_id(i)
        n_steps = math.prod(grid)

        @pl.when(step == 0)
        def _():
            rider.begin(refs[a:b], refs[c:d], sems)

        body(*refs[:a], *refs[b:c], *rest[:len(rest) - k_sem])

        @pl.when(step == (3 * n_steps) // 4)
        def _():
            rider.middle(refs[a:b], refs[c:d], sems)

        @pl.when(step == n_steps - 1)
        def _():
            rider.end(refs[a:b], refs[c:d], sems)

    outs = pl.pallas_call(
        riding, name=name, grid=grid, in_specs=list(in_specs) + [HBM_SPEC] * k_in,
        out_specs=list(out_specs) + [HBM_SPEC] * k_out, out_shape=list(out_shape) + list(rider.out_shapes),
        scratch_shapes=list(scratch) + list(rider.scratch),
        compiler_params=_params(("arbitrary",) * len(grid), vmem))(*args, *rider.srcs)
    return list(outs[:n_out]), list(outs[n_out:])


_ONCE = pl.Buffered(1)
_FF_CHUNKS = ((0, 1536), (1536, 2816))


def _ffn_fwd_full(x, gam, wgut, wd, tm, name, rider=None):
    S, D = x.shape
    F = wd.shape[0]
    tm = _tile(S, tm)
    chunks = _FF_CHUNKS if F == 2816 else ((0, F),)

    def body(x_ref, gam_ref, wgut_ref, wd_ref, xo_ref, h_ref, gu_ref):
        h = _rms(x_ref[...], gam_ref[...]).astype(BF16)
        h_ref[...] = h
        y = None
        for c0, c1 in chunks:
            g = _dot_nt(h, wgut_ref[c0:c1, :])
            u = _dot_nt(h, wgut_ref[F + c0:F + c1, :])
            gu_ref[:, c0:c1] = g.astype(BF16)
            gu_ref[:, F + c0:F + c1] = u.astype(BF16)
            a = (g * jax.nn.sigmoid(g) * u).astype(BF16)
            part = _dot(a, wd_ref[c0:c1, :])
            y = part if y is None else y + part
        xo_ref[...] = x_ref[...] + 0.5 * y

    row = lambda i: (i, 0)
    fix = lambda i: (0, 0)
    return _call(
        body, name=name, grid=(S // tm,),
        in_specs=[_bs((tm, D), row), _bs((1, D), fix), pl.BlockSpec((2 * F, D), fix, pipeline_mode=_ONCE),
                  pl.BlockSpec((F, D), fix, pipeline_mode=_ONCE)],
        out_specs=[_bs((tm, D), row), _bs((tm, D), row), _bs((tm, 2 * F), row)],
        out_shape=[jax.ShapeDtypeStruct((S, D), F32), jax.ShapeDtypeStruct((S, D), BF16),
                   jax.ShapeDtypeStruct((S, 2 * F), BF16)],
        scratch=[], dims=("parallel",), args=(x, gam, wgut, wd), rider=rider)


def _ffn_bwd_full(dy, x, gam, gu, wgut, wd, tm, name, phase="all", dgu_in=None, rider=None):
    S, D = dy.shape
    F = wd.shape[0] if wd is not None else wgut.shape[0] // 2
    tm = _tile(S, tm)
    chunks = _FF_CHUNKS if F == 2816 else ((0, F),)
    act, inp = phase in ("all", "act"), phase in ("all", "in")

    def body(*refs):
        refs = list(refs)
        dy_ref = refs.pop(0)
        x_ref, gam_ref = (refs.pop(0), refs.pop(0)) if inp else (None, None)
        gu_ref = refs.pop(0)
        wgut_ref = refs.pop(0) if inp else None
        wd_ref = refs.pop(0) if act else None
        if inp:
            dx_ref, dgam_ref = refs.pop(0), refs.pop(0)
        if act:
            dgu_ref, a_ref, dyh_ref = refs.pop(0), refs.pop(0), refs.pop(0)
            dyh = (0.5 * dy_ref[...]).astype(BF16)
            dyh_ref[...] = dyh
        dh = None
        for c0, c1 in chunks:
            if act:
                dg, du, a = _swiglu_bwd(_dot_nt(dyh, wd_ref[c0:c1, :]), gu_ref[:, c0:c1], gu_ref[:, F + c0:F + c1])
                dgu_ref[:, c0:c1] = dg
                dgu_ref[:, F + c0:F + c1] = du
                a_ref[:, c0:c1] = a
            else:
                dg, du = gu_ref[:, c0:c1], gu_ref[:, F + c0:F + c1]
            if inp:
                part = _dot(dg, wgut_ref[c0:c1, :]) + _dot(du, wgut_ref[F + c0:F + c1, :])
                dh = part if dh is None else dh + part
        if inp:
            dxn, dgam = _rms_bwd(dh, x_ref[...], gam_ref[...])
            dx_ref[...] = dy_ref[...] + dxn
            _accum_out(dgam_ref, pl.program_id(0) == 0, dgam)

    row = lambda i: (i, 0)
    fix = lambda i: (0, 0)
    in_specs, args = [_bs((tm, D), row)], [dy]
    if inp:
        in_specs += [_bs((tm, D), row), _bs((1, D), fix)]
        args += [x, gam]
    in_specs += [_bs((tm, 2 * F), row)]
    args += [gu if act else dgu_in]
    if inp:
        in_specs += [pl.BlockSpec((2 * F, D), fix, pipeline_mode=_ONCE)]
        args += [wgut]
    if act:
        in_specs += [pl.BlockSpec((F, D), fix, pipeline_mode=_ONCE)]
        args += [wd]
    out_specs, out_shape = [], []
    if inp:
        out_specs += [_bs((tm, D), row), _bs((1, D), fix)]
        out_shape += [jax.ShapeDtypeStruct((S, D), F32), jax.ShapeDtypeStruct((1, D), F32)]
    if act:
        out_specs += [_bs((tm, 2 * F), row), _bs((tm, F), row), _bs((tm, D), row)]
        out_shape += [jax.ShapeDtypeStruct((S, 2 * F), BF16), jax.ShapeDtypeStruct((S, F), BF16),
                      jax.ShapeDtypeStruct((S, D), BF16)]
    return _call(body, name=name, grid=(S // tm,), in_specs=in_specs, out_specs=out_specs, out_shape=out_shape,
                 scratch=[], dims=("arbitrary",), args=tuple(args), rider=rider, vmem=VMEM_LIMIT_BIG)


def _swiglu_bwd(da, g, u):
    g = g.astype(F32)
    u = u.astype(F32)
    sig = jax.nn.sigmoid(g)
    sl = g * sig
    dg = (da * u * (sig * (1.0 + g * (1.0 - sig)))).astype(BF16)
    return dg, (da * sl).astype(BF16), (sl * u).astype(BF16)


def _mm_tn(a, b, nb, a_of, b_of, tm, tn, name, rider=None, ts=512):
    _, S, M = a.shape
    N = b.shape[2]
    tm = _tile(M, tm)
    tn = _tile(N, tn)
    ts = _tile(S, ts)
    nk = S // ts

    def body(a_ref, b_ref, o_ref, acc):
        k = pl.program_id(3)

        @pl.when(k == 0)
        def _():
            acc[...] = jnp.zeros_like(acc)

        acc[...] += _dot_tn(a_ref[...].astype(BF16), b_ref[...].astype(BF16))

        @pl.when(k == nk - 1)
        def _():
            o_ref[...] = acc[...].astype(o_ref.dtype)

    (out,), extra = _call(
        body, name=name, grid=(nb, M // tm, N // tn, nk),
        in_specs=[_bs((None, ts, tm), lambda p, i, j, k: (a_of(p), k, i)),
                  _bs((None, ts, tn), lambda p, i, j, k: (b_of(p), k, j))],
        out_specs=[_bs((None, tm, tn), lambda p, i, j, k: (p, i, j))],
        out_shape=[jax.ShapeDtypeStruct((nb, M, N), BF16)],
        scratch=[pltpu.VMEM((tm, tn), F32)],
        dims=("parallel", "parallel", "parallel", "arbitrary"), args=(a, b), rider=rider)
    return (out, extra) if rider is not None else out


FFN_FWD_TOKENS = 512
FFN_BWD_TOKENS = 256
FFN_GRAD_ROWS = 1408


def _ffn_weight_grad(a, b, name, rider=None):
    M = a.shape[1]
    tm = max(t for t in range(LANES, FFN_GRAD_ROWS + 1, LANES) if M % t == 0)
    res = _mm_tn(a[None], b[None], 1, lambda p: 0, lambda p: 0, tm, b.shape[1], name, rider=rider, ts=2048)
    return (res[0][0], res[1]) if rider is not None else res[0]


def _mix_in_fwd(x, gam, w_in, name):
    S, D = x.shape
    tm = _tile(S, 512)
    nkv = (ZF - 384) // LANES

    def body(x_ref, gam_ref, w_ref, h_ref, za_ref, zf_ref, zt_ref):
        hb = _rms(x_ref[...], gam_ref[...]).astype(BF16)
        h_ref[...] = hb
        za_ref[...] = _dot(hb, w_ref[:, 0:ZA])
        zf = _dot(hb, w_ref[:, ZA:N_PAD])
        zf_ref[...] = zf.astype(BF16)
        for c in range(nkv):
            zt_ref[c * LANES:(c + 1) * LANES, :] = zf[:, 384 + c * LANES:384 + (c + 1) * LANES].T.astype(BF16)

    return pl.pallas_call(
        body, name=name, grid=(S // tm,),
        in_specs=[_bs((tm, D), lambda i: (i, 0)), _bs((1, D), lambda i: (0, 0)), _bs((D, N_PAD), lambda i: (0, 0))],
        out_specs=[_bs((tm, D), lambda i: (i, 0)), _bs((tm, ZA), lambda i: (i, 0)), _bs((tm, ZF), lambda i: (i, 0)),
                   _bs((nkv * LANES, tm), lambda i: (0, i))],
        out_shape=[jax.ShapeDtypeStruct((S, D), BF16), jax.ShapeDtypeStruct((S, ZA), F32),
                   jax.ShapeDtypeStruct((S, ZF), BF16), jax.ShapeDtypeStruct((nkv * LANES, S), BF16)],
        compiler_params=_params(("parallel",)),
    )(x, gam, w_in)


def _mix_in_bwd(dy, x, gam, dza_mla, du, dtail_f, dqf, dkf, dvf, w_in, name):
    S, D = x.shape
    tm = _tile(S, 512)

    def body(dy_ref, x_ref, gam_ref, dza_ref, du_ref, dt_ref, dq_ref, dk_ref, dv_ref, w_ref, dx_ref, dgam_ref, dz_ref):
        i = pl.program_id(0)
        dza = dza_ref[...]
        dz = jnp.concatenate([dza[:, 0:384], du_ref[...], dza[:, TAIL0:ZA] + dt_ref[...],
                              dq_ref[...], dk_ref[...], dv_ref[...]], axis=1).astype(BF16)
        dz_ref[...] = dz
        dh = _dot_nt(dz, w_ref[...])
        dxn, dgam = _rms_bwd(dh, x_ref[...], gam_ref[...])
        dx_ref[...] = dy_ref[...] + dxn
        _accum_out(dgam_ref, i == 0, dgam)

    row = lambda i: (i, 0)
    fix = lambda i: (0, 0)
    return pl.pallas_call(
        body, name=name, grid=(S // tm,),
        in_specs=[_bs((tm, D), row), _bs((tm, D), row), _bs((1, D), fix), _bs((tm, ZA), row), _bs((tm, 256), row),
                  _bs((tm, 128), row), _bs((tm, 384), row), _bs((tm, 384), row), _bs((tm, 384), row),
                  _bs((D, N_PAD), fix)],
        out_specs=[_bs((tm, D), row), _bs((1, D), fix), _bs((tm, N_PAD), row)],
        out_shape=[jax.ShapeDtypeStruct((S, D), F32), jax.ShapeDtypeStruct((1, D), F32),
                   jax.ShapeDtypeStruct((S, N_PAD), BF16)],
        compiler_params=_params(("arbitrary",)),
    )(dy, x, gam, dza_mla, du, dtail_f, dqf, dkf, dvf, w_in)


def _rope_tables(S):
    half = MLA_ROPE // 2
    inv_freq = ROPE_THETA ** (-jnp.arange(0, MLA_ROPE, 2, dtype=F32) / MLA_ROPE)
    ang = jnp.arange(S, dtype=jnp.int32).astype(F32)[:, None] * inv_freq[None, :]
    cos, sin = jnp.cos(ang), jnp.sin(ang)
    one = jnp.ones((S, ROPE_LANE0), F32)
    zero = jnp.zeros((S, ROPE_LANE0), F32)
    pad1 = jnp.ones((S, LANES - ROPE_LANE0 - MLA_ROPE), F32)
    pad0 = jnp.zeros((S, LANES - ROPE_LANE0 - MLA_ROPE), F32)
    zh = jnp.zeros((S, half), F32)
    tab_c = jnp.concatenate([one, cos, cos, pad1], axis=1)
    tab_ck = jnp.concatenate([zero, cos, cos, pad0], axis=1)
    tab_s1 = jnp.concatenate([zero, -sin, zh, pad0], axis=1)
    tab_s2 = jnp.concatenate([zero, zh, sin, pad0], axis=1)
    return tab_c, tab_ck, tab_s1, tab_s2


def _rope(x, c, s1, s2):
    return x * c + pltpu.roll(x, LANES - 16, 1) * s1 + pltpu.roll(x, 16, 1) * s2


def _rope_t(dy, c, s1, s2):
    return dy * c + pltpu.roll(dy * s1, 16, 1) + pltpu.roll(dy * s2, LANES - 16, 1)


_MLA_SCALE = 1.0 / math.sqrt(MLA_NOPE + MLA_ROPE)


def _mla_prep(za, gq, gkv, wq, wkv, tabs, name):
    S = za.shape[0]
    tm = _tile(S, 512)
    H = MLA_HEADS

    def body(zq_ref, tail_ref, gq_ref, gkv_ref, wq_ref, wkv_ref, c_ref, ck_ref, s1_ref, s2_ref,
             qf_ref, kf_ref, v_ref, kft_ref, vt_ref):
        zq = zq_ref[...]
        c, s1, s2 = c_ref[...], s1_ref[...], s2_ref[...]
        qn = _rms(zq[:, 0:256], gq_ref[...]).astype(BF16)
        q = _dot(qn, wq_ref[...])
        for h in range(H):
            blk = _rope(q[:, h * LANES:(h + 1) * LANES], c, s1, s2)
            qf_ref[:, h * LANES:(h + 1) * LANES] = (blk * _MLA_SCALE).astype(BF16)
        kvn = _rms(zq[:, 256:384], gkv_ref[...]).astype(BF16)
        kv = _dot(kvn, wkv_ref[...])
        kt = _rope(tail_ref[...], ck_ref[...], s1, s2)
        for h in range(H):
            sl = slice(h * LANES, (h + 1) * LANES)
            kblk = kv[:, sl] + kt
            kf_ref[:, sl] = kblk.astype(BF16)
            kft_ref[sl, :] = kblk.T.astype(BF16)
        v_ref[...] = kv[:, H * LANES:].astype(BF16)
        for cblk in range(H * MLA_V // LANES):
            sl = slice(cblk * LANES, (cblk + 1) * LANES)
            vt_ref[sl, :] = kv[:, H * LANES + cblk * LANES:H * LANES + (cblk + 1) * LANES].T.astype(BF16)

    row = lambda i: (i, 0)
    col = lambda i: (0, i)
    fix = lambda i: (0, 0)
    return pl.pallas_call(
        body, name=name, grid=(S // tm,),
        in_specs=[_bs((tm, 384), row), _bs((tm, 128), lambda i: (i, TAIL0 // 128)), _bs((1, 256), fix), _bs((1, 128), fix),
                  _bs((256, 768), fix), _bs((128, 1152), fix),
                  _bs((tm, 128), row), _bs((tm, 128), row), _bs((tm, 128), row), _bs((tm, 128), row)],
        out_specs=[_bs((tm, 768), row), _bs((tm, 768), row), _bs((tm, 384), row), _bs((768, tm), col), _bs((384, tm), col)],
        out_shape=[jax.ShapeDtypeStruct((S, 768), BF16), jax.ShapeDtypeStruct((S, 768), BF16),
                   jax.ShapeDtypeStruct((S, 384), BF16), jax.ShapeDtypeStruct((768, S), BF16),
                   jax.ShapeDtypeStruct((384, S), BF16)],
        compiler_params=_params(("parallel",)),
    )(za, za, gq, gkv, wq, wkv, *tabs)


def _mla_prep_bwd(za, gq, gkv, wq, wkv, tabs, dqf, dkf, dvm, name):
    S = za.shape[0]
    tm = _tile(S, 512)
    H = MLA_HEADS

    def body(zq_ref, gq_ref, gkv_ref, wq_ref, wkv_ref, c_ref, ck_ref, s1_ref, s2_ref, dqf_ref, dkf_ref, dvm_ref,
             dza_ref, dwq_ref, dwkv_ref, dgq_ref, dgkv_ref):
        i = pl.program_id(0)
        first = i == 0
        zq = zq_ref[...]
        c, s1, s2 = c_ref[...], s1_ref[...], s2_ref[...]
        lane = lax.broadcasted_iota(jnp.int32, (1, LANES), 1)
        nope = lane < MLA_NOPE
        rope = jnp.logical_and(lane >= ROPE_LANE0, lane < ROPE_LANE0 + MLA_ROPE)

        qa = zq[:, 0:256]
        qn = _rms(qa, gq_ref[...]).astype(BF16)
        dqf = dqf_ref[...]
        dq_pre = jnp.concatenate(
            [_rope_t(dqf[:, h * LANES:(h + 1) * LANES] * _MLA_SCALE, c, s1, s2) for h in range(H)], axis=1).astype(BF16)
        _accum_out(dwq_ref, first, _dot_tn(qn, dq_pre))
        dqa, dgq = _rms_bwd(_dot_nt(dq_pre, wq_ref[...]), qa, gq_ref[...])
        _accum_out(dgq_ref, first, dgq)

        kva = zq[:, 256:384]
        kvn = _rms(kva, gkv_ref[...]).astype(BF16)
        dkf = dkf_ref[...]
        parts = []
        dkt = jnp.zeros((tm, LANES), F32)
        for h in range(H):
            blk = dkf[:, h * LANES:(h + 1) * LANES]
            parts.append(jnp.where(nope, blk, 0.0))
            dkt = dkt + jnp.where(rope, blk, 0.0)
        dkv_pre = jnp.concatenate(parts + [dvm_ref[...]], axis=1).astype(BF16)
        _accum_out(dwkv_ref, first, _dot_tn(kvn, dkv_pre))
        dkva, dgkv = _rms_bwd(_dot_nt(dkv_pre, wkv_ref[...]), kva, gkv_ref[...])
        _accum_out(dgkv_ref, first, dgkv)

        dtail = _rope_t(dkt, ck_ref[...], s1, s2)
        dza_ref[...] = jnp.concatenate([dqa, dkva, jnp.zeros((tm, 256), F32), dtail], axis=1)

    row = lambda i: (i, 0)
    fix = lambda i: (0, 0)
    return pl.pallas_call(
        body, name=name, grid=(S // tm,),
        in_specs=[_bs((tm, 384), row), _bs((1, 256), fix), _bs((1, 128), fix), _bs((256, 768), fix), _bs((128, 1152), fix),
                  _bs((tm, 128), row), _bs((tm, 128), row), _bs((tm, 128), row), _bs((tm, 128), row),
                  _bs((tm, 768), row), _bs((tm, 768), row), _bs((tm, 384), row)],
        out_specs=[_bs((tm, ZA), row), _bs((256, 768), fix), _bs((128, 1152), fix), _bs((1, 256), fix), _bs((1, 128), fix)],
        out_shape=[jax.ShapeDtypeStruct((S, ZA), F32), jax.ShapeDtypeStruct((256, 768), F32),
                   jax.ShapeDtypeStruct((128, 1152), F32), jax.ShapeDtypeStruct((1, 256), F32),
                   jax.ShapeDtypeStruct((1, 128), F32)],
        compiler_params=_params(("arbitrary",)),
    )(za, gq, gkv, wq, wkv, *tabs, dqf, dkf, dvm)


def _head_views(qb, kb, r, dkb, sel):
    if dkb == LANES:
        sl = slice(r * LANES, (r + 1) * LANES)
        return qb[:, sl], kb[:, sl], kb[:, sl]
    return jnp.where(sel, qb, jnp.zeros_like(qb)), kb, jnp.where(sel, kb, jnp.zeros_like(kb))


def _attn_fwd_t(q_arr, k_arr, vt_arr, *, nhp, dkb, qoff, koff, vtoff, scale, cum, cumT, name, rider=None):
    S = q_arr.shape[0]
    T = _tile(S, ATTN_BLOCK)
    nq = S // T
    W = 2 * dkb
    bias = cum is not None

    def body(*refs):
        if bias:
            q_ref, k_ref, vt_ref, cq_ref, ck_ref, o_ref, lse_ref, m_s, l_s, acc_s = refs
        else:
            q_ref, k_ref, vt_ref, o_ref, lse_ref, m_s, l_s, acc_s = refs
        hp, qi, ki = pl.program_id(0), pl.program_id(1), pl.program_id(2)
        lo_lane = lax.broadcasted_iota(jnp.int32, (1, LANES), 1) < 64
        lo_row = lax.broadcasted_iota(jnp.int32, (LANES, 1), 0) < 64

        @pl.when(ki == 0)
        def _():
            m_s[...] = jnp.full_like(m_s, NEG)
            l_s[...] = jnp.zeros_like(l_s)
            acc_s[...] = jnp.zeros_like(acc_s)

        def step(masked):
            qb, kb, vtb = q_ref[...], k_ref[...], vt_ref[...]
            if masked:
                mask = lax.broadcasted_iota(jnp.int32, (T, T), 0) <= lax.broadcasted_iota(jnp.int32, (T, T), 1)
            if bias:
                li = lax.broadcasted_iota(jnp.int32, (T, LANES), 1)
                ckb = ck_ref[...]
            m_all, l_all = m_s[...], l_s[...]
            scores = []
            for r in range(2):
                sel = lo_lane if r == 0 else jnp.logical_not(lo_lane)
                q, k, _ = _head_views(qb, kb, r, dkb, sel)
                if scale != 1.0:
                    q = q * jnp.asarray(scale, q.dtype)
                scores.append(_dot_nt(k, q))
            m_out, l_out, alphas, pvs = [], [], [], []
            for r in range(2):
                rsel = lo_row if r == 0 else jnp.logical_not(lo_row)
                s = scores[r]
                if bias:
                    ck = jnp.sum(jnp.where(li == 8 * hp + r, ckb, 0.0), axis=1, keepdims=True)
                    s = s + (cq_ref[r:r + 1, :] - ck)
                if masked:
                    s = jnp.where(mask, s, NEG)
                m_prev = m_all[r:r + 1, :]
                m_new = jnp.maximum(m_prev, jnp.max(s, axis=0, keepdims=True))
                alpha = jnp.exp(m_prev - m_new)
                p = jnp.exp(s - m_new)
                m_out.append(m_new)
                l_out.append(alpha * l_all[r:r + 1, :] + jnp.sum(p, axis=0, keepdims=True))
                alphas.append(alpha)
                pvs.append(_dot(jnp.where(rsel, vtb, jnp.zeros_like(vtb)), p.astype(BF16)))
            m_s[0:1, :] = m_out[0]
            m_s[1:2, :] = m_out[1]
            l_s[0:1, :] = l_out[0]
            l_s[1:2, :] = l_out[1]
            acc_s[...] = acc_s[...] * jnp.where(lo_row, alphas[0], alphas[1]) + (pvs[0] + pvs[1])

        @pl.when(ki < qi)
        def _():
            step(False)

        @pl.when(ki == qi)
        def _():
            step(True)

        @pl.when(ki == nq - 1)
        def _():
            inv = jnp.where(lo_row, 1.0 / l_s[0:1, :], 1.0 / l_s[1:2, :])
            o_ref[...] = (acc_s[...] * inv).T.astype(BF16)
            used = lax.broadcasted_iota(jnp.int32, (8, T), 0) < 2
            lse_ref[...] = jnp.where(used, m_s[...] + jnp.log(jnp.where(used, l_s[...], 1.0)), 0.0)

    kmap = lambda hp, qi, ki: jnp.minimum(ki, qi)
    in_specs = [_bs((T, W), lambda hp, qi, ki: (qi, qoff + hp)),
                _bs((T, W), lambda hp, qi, ki: (kmap(hp, qi, ki), koff + hp)),
                _bs((LANES, T), lambda hp, qi, ki: (vtoff + hp, kmap(hp, qi, ki)))]
    args = [q_arr, k_arr, vt_arr]
    if bias:
        in_specs += [_bs((8, T), lambda hp, qi, ki: (hp, qi)), _bs((T, LANES), lambda hp, qi, ki: (kmap(hp, qi, ki), 0))]
        args += [cumT, cum]
    return _call(
        body, name=name, grid=(nhp, nq, nq),
        in_specs=in_specs,
        out_specs=[_bs((T, LANES), lambda hp, qi, ki: (qi, hp)), _bs((None, 8, T), lambda hp, qi, ki: (hp, 0, qi))],
        out_shape=[jax.ShapeDtypeStruct((S, nhp * LANES), BF16), jax.ShapeDtypeStruct((nhp, 8, S), F32)],
        scratch=[pltpu.VMEM((8, T), F32), pltpu.VMEM((8, T), F32), pltpu.VMEM((LANES, T), F32)],
        dims=("parallel", "parallel", "arbitrary"), args=args, rider=rider)


def _attn_bwd_t(q_arr, k_arr, kt_arr, v_arr, do_arr, lse, dvec, *, nhp, dkb, qoff, koff, ktoff, voff, scale, cum, cumT,
                name, rider=None):
    S = q_arr.shape[0]
    T = _tile(S, ATTN_BLOCK)
    nq = S // T
    W = 2 * dkb
    bias = cum is not None

    def body(*refs):
        if bias:
            (q_ref, k_ref, kt_ref, v_ref, do_ref, lse_ref, dvec_ref, cq_ref, ck_ref,
             dq_ref, dk_ref, dv_ref, dcq_ref, dck_ref, dqt_s, dk_s, dv_s, dcq_s, dck_s) = refs
        else:
            (q_ref, k_ref, kt_ref, v_ref, do_ref, lse_ref, dvec_ref,
             dq_ref, dk_ref, dv_ref, dqt_s, dk_s, dv_s) = refs
        hp, ki, qi = pl.program_id(0), pl.program_id(1), pl.program_id(2)
        lo_lane = lax.broadcasted_iota(jnp.int32, (1, LANES), 1) < 64
        lo_row = lax.broadcasted_iota(jnp.int32, (LANES, 1), 0) < 64

        @pl.when(jnp.logical_and(ki == 0, qi == 0))
        def _():
            dqt_s[...] = jnp.zeros_like(dqt_s)
            if bias:
                dcq_s[...] = jnp.zeros_like(dcq_s)

        @pl.when(qi == 0)
        def _():
            dk_s[...] = jnp.zeros_like(dk_s)
            dv_s[...] = jnp.zeros_like(dv_s)
            if bias:
                dck_s[...] = jnp.zeros_like(dck_s)

        def step(masked):
            qb, kb, ktb, vb, dob = q_ref[...], k_ref[...], kt_ref[...], v_ref[...], do_ref[...]
            if masked:
                mask = lax.broadcasted_iota(jnp.int32, (T, T), 0) <= lax.broadcasted_iota(jnp.int32, (T, T), 1)
            if bias:
                li = lax.broadcasted_iota(jnp.int32, (T, LANES), 1)
                ckb = ck_ref[...]
            for r in range(2):
                sel = lo_lane if r == 0 else jnp.logical_not(lo_lane)
                rsel = lo_row if r == 0 else jnp.logical_not(lo_row)
                q, k, _ = _head_views(qb, kb, r, dkb, sel)
                if scale != 1.0:
                    q = q * jnp.asarray(scale, q.dtype)
                s = _dot_nt(k, q)
                if bias:
                    ck = jnp.sum(jnp.where(li == 8 * hp + r, ckb, 0.0), axis=1, keepdims=True)
                    s = s + (cq_ref[r:r + 1, :] - ck)
                p = jnp.exp(s - lse_ref[r:r + 1, :])
                if masked:
                    p = jnp.where(mask, p, 0.0)
                do_r = jnp.where(sel, dob, jnp.zeros_like(dob))
                dp = _dot_nt(vb, do_r)
                ds = p * (dp - dvec_ref[r:r + 1, :])
                pb = p.astype(BF16)
                dsb = ds.astype(BF16)
                dv_s[...] += _dot(pb, do_r)
                if dkb == LANES:
                    sl = slice(r * LANES, (r + 1) * LANES)
                    dk_s[:, sl] += _dot(dsb, q)
                    dqt_s[qi, sl, :] += _dot(ktb[sl, :], dsb) * scale
                else:
                    dk_s[...] += _dot(dsb, q)
                    dqt_s[qi] += _dot(jnp.where(rsel, ktb, jnp.zeros_like(ktb)), dsb) * scale
                if bias:
                    dcq_s[qi, r:r + 1, :] += jnp.sum(ds, axis=0, keepdims=True)
                    dck_s[...] -= jnp.where(li == 8 * hp + r, jnp.sum(ds, axis=1, keepdims=True), 0.0)

        @pl.when(qi > ki)
        def _():
            step(False)

        @pl.when(qi == ki)
        def _():
            step(True)

        @pl.when(qi == nq - 1)
        def _():
            dk_ref[...] = dk_s[...]
            dv_ref[...] = dv_s[...]
            if bias:
                dck_ref[...] = dck_s[...]

        @pl.when(jnp.logical_and(ki == nq - 1, qi == nq - 1))
        def _():
            for c in range(nq):
                dq_ref[c * T:(c + 1) * T, :] = dqt_s[c].T
                if bias:
                    dcq_ref[:, c * T:(c + 1) * T] = dcq_s[c]

    qmap = lambda hp, ki, qi: jnp.maximum(qi, ki)
    in_specs = [_bs((T, W), lambda hp, ki, qi: (qmap(hp, ki, qi), qoff + hp)),
                _bs((T, W), lambda hp, ki, qi: (ki, koff + hp)),
                _bs((W, T), lambda hp, ki, qi: (ktoff + hp, ki)),
                _bs((T, LANES), lambda hp, ki, qi: (ki, voff + hp)),
                _bs((T, LANES), lambda hp, ki, qi: (qmap(hp, ki, qi), hp)),
                _bs((None, 8, T), lambda hp, ki, qi: (hp, 0, qmap(hp, ki, qi))),
                _bs((None, 8, T), lambda hp, ki, qi: (hp, 0, qmap(hp, ki, qi)))]
    args = [q_arr, k_arr, kt_arr, v_arr, do_arr, lse, dvec]
    out_specs = [_bs((S, W), lambda hp, ki, qi: (0, hp)), _bs((T, W), lambda hp, ki, qi: (ki, hp)),
                 _bs((T, LANES), lambda hp, ki, qi: (ki, hp))]
    out_shape = [jax.ShapeDtypeStruct((S, nhp * W), F32), jax.ShapeDtypeStruct((S, nhp * W), F32),
                 jax.ShapeDtypeStruct((S, nhp * LANES), F32)]
    scratch = [pltpu.VMEM((nq, W, T), F32), pltpu.VMEM((T, W), F32), pltpu.VMEM((T, LANES), F32)]
    if bias:
        in_specs += [_bs((8, T), lambda hp, ki, qi: (hp, qmap(hp, ki, qi))), _bs((T, LANES), lambda hp, ki, qi: (ki, 0))]
        args += [cumT, cum]
        out_specs += [_bs((None, 8, S), lambda hp, ki, qi: (hp, 0, 0)), _bs((None, T, LANES), lambda hp, ki, qi: (hp, ki, 0))]
        out_shape += [jax.ShapeDtypeStruct((nhp, 8, S), F32), jax.ShapeDtypeStruct((nhp, S, LANES), F32)]
        scratch += [pltpu.VMEM((nq, 8, T), F32), pltpu.VMEM((T, LANES), F32)]
    return _call(body, name=name, grid=(nhp, nq, nq), in_specs=in_specs, out_specs=out_specs, out_shape=out_shape,
                 scratch=scratch, dims=("arbitrary", "arbitrary", "arbitrary"), args=args, rider=rider)


def _gate_lanes(shape):
    lane = lax.broadcasted_iota(jnp.int32, shape, 1)
    return jnp.logical_and(lane < 8 * (FOX_HEADS // 2), lane % 8 < 2)


def _fox_prep(za, b_row, name):
    S = za.shape[0]
    nrow = 8 * (FOX_HEADS // 2)

    def body(tail_ref, b_ref, cum_ref, cumt_ref):
        x = tail_ref[...] + b_ref[...]
        logf = jnp.minimum(x, 0.0) - jnp.log(1.0 + jnp.exp(-jnp.abs(x)))
        y = jnp.where(_gate_lanes((S, LANES)), logf, 0.0)
        row = lax.broadcasted_iota(jnp.int32, (S, LANES), 0)
        k = 1
        while k < S:
            y = y + jnp.where(row >= k, pltpu.roll(y, k, 0), 0.0)
            k *= 2
        cum_ref[...] = y
        cumt_ref[...] = y.T[0:nrow, :]

    return pl.pallas_call(
        body, name=name, grid=(1,),
        in_specs=[_bs((S, LANES), lambda i: (0, TAIL0 // LANES)), _bs((1, LANES), lambda i: (0, 0))],
        out_specs=[_bs((S, LANES), lambda i: (0, 0)), _bs((nrow, S), lambda i: (0, 0))],
        out_shape=[jax.ShapeDtypeStruct((S, LANES), F32), jax.ShapeDtypeStruct((nrow, S), F32)],
        compiler_params=_params(("arbitrary",)),
    )(za, b_row)


def _fox_prep_bwd(za, b_row, dcq, dck, name):
    S = za.shape[0]
    nhp = FOX_HEADS // 2
    nrow = 8 * nhp
    dcq2 = dcq.reshape(nrow, S)

    def body(tail_ref, b_ref, dcq_ref, dck_ref, dt_ref, db_ref):
        x = tail_ref[...] + b_ref[...]
        d = jnp.concatenate([dcq_ref[...], jnp.zeros((LANES - nrow, S), F32)], axis=0).T
        for hp in range(nhp):
            d = d + dck_ref[hp]
        row = lax.broadcasted_iota(jnp.int32, (S, LANES), 0)
        k = 1
        while k < S:
            d = d + jnp.where(row < S - k, pltpu.roll(d, S - k, 0), 0.0)
            k *= 2
        df = jnp.where(_gate_lanes((S, LANES)), d * jax.nn.sigmoid(-x), 0.0)
        dt_ref[...] = df
        db_ref[...] = jnp.sum(df, axis=0, keepdims=True)

    return pl.pallas_call(
        body, name=name, grid=(1,),
        in_specs=[_bs((S, LANES), lambda i: (0, TAIL0 // LANES)), _bs((1, LANES), lambda i: (0, 0)),
                  _bs((nrow, S), lambda i: (0, 0)), _bs((nhp, S, LANES), lambda i: (0, 0, 0))],
        out_specs=[_bs((S, LANES), lambda i: (0, 0)), _bs((1, LANES), lambda i: (0, 0))],
        out_shape=[jax.ShapeDtypeStruct((S, LANES), F32), jax.ShapeDtypeStruct((1, LANES), F32)],
        compiler_params=_params(("arbitrary",)),
    )(za, b_row, dcq2, dck)


def _pool_select(half, lane_lo, vals):
    return jnp.where(lane_lo, jnp.where(half == 0, vals[0], vals[2]), jnp.where(half == 0, vals[1], vals[3]))


def _pool_den(S, half, lane_lo):
    cnt = (lax.broadcasted_iota(jnp.int32, (S, LANES), 0) + 1).astype(F32)
    w = _pool_select(half, lane_lo, [float(x) for x in POOL_WINDOWS])
    return jnp.minimum(cnt, w)


def _pool_fwd(za, wbd, scale, name):
    S = za.shape[0]

    def body(u_ref, w_ref, sc_ref, y_ref, pd_ref):
        half = pl.program_id(0)
        u = u_ref[...]
        row = lax.broadcasted_iota(jnp.int32, (S, LANES), 0)
        lane_lo = lax.broadcasted_iota(jnp.int32, (S, LANES), 1) < POOL_GROUP
        sums = []
        acc = u
        k = 1
        while k < POOL_WINDOWS[-1]:
            acc = acc + jnp.where(row >= k, pltpu.roll(acc, k, 0), 0.0)
            sums.append(acc)
            k *= 2
        pooled = _pool_select(half, lane_lo, sums) / _pool_den(S, half, lane_lo)
        pd = (pooled - u).astype(BF16)
        pd_ref[...] = pd
        y_ref[...] = (_dot(pd, w_ref[...]) * sc_ref[...]).astype(BF16)

    return pl.pallas_call(
        body, name=name, grid=(2,),
        in_specs=[_bs((S, LANES), lambda i: (0, 384 // LANES + i)), _bs((None, LANES, LANES), lambda i: (i, 0, 0)),
                  _bs((1, LANES), lambda i: (0, i))],
        out_specs=[_bs((S, LANES), lambda i: (0, i)), _bs((S, LANES), lambda i: (0, i))],
        out_shape=[jax.ShapeDtypeStruct((S, POOL_WIDTH), BF16), jax.ShapeDtypeStruct((S, POOL_WIDTH), BF16)],
        compiler_params=_params(("parallel",)),
    )(za, wbd, scale)


def _pool_bwd(dyb, pd, wbd, scale, name):
    S = pd.shape[0]

    def body(dy_ref, pd_ref, w_ref, sc_ref, du_ref, dw_ref, dsc_ref):
        half = pl.program_id(0)
        dy = dy_ref[...]
        pd = pd_ref[...]
        w = w_ref[...]
        ypre = _dot(pd, w)
        dsc_ref[...] = jnp.sum(dy * ypre, axis=0, keepdims=True)
        dyp = (dy * sc_ref[...]).astype(BF16)
        dw_ref[...] = _dot_tn(pd, dyp)
        dpd = _dot_nt(dyp, w)
        row = lax.broadcasted_iota(jnp.int32, (S, LANES), 0)
        lane_lo = lax.broadcasted_iota(jnp.int32, (S, LANES), 1) < POOL_GROUP
        acc = dpd / _pool_den(S, half, lane_lo)
        sums = []
        k = 1
        while k < POOL_WINDOWS[-1]:
            acc = acc + jnp.where(row < S - k, pltpu.roll(acc, S - k, 0), 0.0)
            sums.append(acc)
            k *= 2
        du_ref[...] = _pool_select(half, lane_lo, sums) - dpd

    return pl.pallas_call(
        body, name=name, grid=(2,),
        in_specs=[_bs((S, LANES), lambda i: (0, i)), _bs((S, LANES), lambda i: (0, i)),
                  _bs((None, LANES, LANES), lambda i: (i, 0, 0)), _bs((1, LANES), lambda i: (0, i))],
        out_specs=[_bs((S, LANES), lambda i: (0, i)), _bs((None, LANES, LANES), lambda i: (i, 0, 0)),
                   _bs((1, LANES), lambda i: (0, i))],
        out_shape=[jax.ShapeDtypeStruct((S, POOL_WIDTH), F32), jax.ShapeDtypeStruct((2, LANES, LANES), F32),
                   jax.ShapeDtypeStruct((1, POOL_WIDTH), F32)],
        compiler_params=_params(("parallel",)),
    )(dyb, pd, wbd, scale)


def _mix_out_fwd(x, ya, yb, yc, w_out, name):
    S, D = x.shape
    tm = _tile(S, 512)
    K = w_out.shape[0]

    def body(x_ref, ya_ref, yb_ref, yc_ref, w_ref, xo_ref, yc_out):
        ycat = jnp.concatenate([ya_ref[...], yb_ref[...], yc_ref[...]], axis=1)
        yc_out[...] = ycat
        xo_ref[...] = x_ref[...] + _dot(ycat, w_ref[...])

    row = lambda i: (i, 0)
    return pl.pallas_call(
        body, name=name, grid=(S // tm,),
        in_specs=[_bs((tm, D), row), _bs((tm, 384), row), _bs((tm, 256), row), _bs((tm, 384), row),
                  _bs((K, D), lambda i: (0, 0))],
        out_specs=[_bs((tm, D), row), _bs((tm, K), row)],
        out_shape=[jax.ShapeDtypeStruct((S, D), F32), jax.ShapeDtypeStruct((S, K), BF16)],
        compiler_params=_params(("parallel",)),
    )(x, ya, yb, yc, w_out)


def _mix_out_bwd(dy, w_out, ya, yc, name):
    S, D = dy.shape
    tm = _tile(S, 512)
    K = w_out.shape[0]
    nhp = ya.shape[1] // LANES

    def body(dy_ref, w_ref, ya_ref, yc_ref, da_ref, db_ref, dc_ref, dva_ref, dvc_ref):
        d = _dot_nt(dy_ref[...].astype(BF16), w_ref[...])
        da = d[:, 0:384].astype(BF16)
        dc = d[:, 640:1024].astype(BF16)
        da_ref[...] = da
        db_ref[...] = d[:, 384:640]
        dc_ref[...] = dc
        li = lax.broadcasted_iota(jnp.int32, (tm, LANES), 1)
        for do, o_ref, out_ref in ((da, ya_ref, dva_ref), (dc, yc_ref, dvc_ref)):
            for hp in range(nhp):
                sl = slice(hp * LANES, (hp + 1) * LANES)
                prod = do[:, sl].astype(F32) * o_ref[:, sl].astype(F32)
                d0 = jnp.sum(jnp.where(li < 64, prod, 0.0), axis=1, keepdims=True)
                d1 = jnp.sum(jnp.where(li >= 64, prod, 0.0), axis=1, keepdims=True)
                out_ref[hp] = jnp.where(li == 0, d0, jnp.where(li == 1, d1, 0.0)).T[0:8, :]

    row = lambda i: (i, 0)
    dv_spec = _bs((nhp, 8, tm), lambda i: (0, 0, i))
    dv_shape = jax.ShapeDtypeStruct((nhp, 8, S), F32)
    return pl.pallas_call(
        body, name=name, grid=(S // tm,),
        in_specs=[_bs((tm, D), row), _bs((K, D), lambda i: (0, 0)), _bs((tm, 384), row), _bs((tm, 384), row)],
        out_specs=[_bs((tm, 384), row), _bs((tm, 256), row), _bs((tm, 384), row), dv_spec, dv_spec],
        out_shape=[jax.ShapeDtypeStruct((S, 384), BF16), jax.ShapeDtypeStruct((S, 256), F32),
                   jax.ShapeDtypeStruct((S, 384), BF16), dv_shape, dv_shape],
        compiler_params=_params(("parallel",)),
    )(dy, w_out, ya, yc)


def _loss_head(x, gam, target, name):
    S, D = x.shape
    tm = _tile(S, 512)

    def body(x_ref, gam_ref, t_ref, dx_ref, dgam_ref, loss_ref):
        i = pl.program_id(0)
        xv = x_ref[...]
        err = _rms(xv, gam_ref[...]) - t_ref[...]
        part = 0.5 * jnp.sum(jnp.mean(err * err, axis=-1, keepdims=True), axis=0, keepdims=True)
        dxn, dgam = _rms_bwd(err * (1.0 / D), xv, gam_ref[...])
        dx_ref[...] = dxn
        _accum_out(dgam_ref, i == 0, dgam)
        _accum_out(loss_ref, i == 0, jnp.broadcast_to(part, (1, LANES)))

    row = lambda i: (i, 0)
    fix = lambda i: (0, 0)
    return pl.pallas_call(
        body, name=name, grid=(S // tm,),
        in_specs=[_bs((tm, D), row), _bs((1, D), fix), _bs((tm, D), row)],
        out_specs=[_bs((tm, D), row), _bs((1, D), fix), _bs((1, LANES), fix)],
        out_shape=[jax.ShapeDtypeStruct((S, D), F32), jax.ShapeDtypeStruct((1, D), F32),
                   jax.ShapeDtypeStruct((1, LANES), F32)],
        compiler_params=_params(("arbitrary",)),
    )(x, gam, target)


def _adam_math(g, w, m, v):
    m = ADAM_B1 * m + (1.0 - ADAM_B1) * g
    v = ADAM_B2 * v + (1.0 - ADAM_B2) * (g * g)
    m_hat = m / (1.0 - ADAM_B1 ** ADAM_STEP)
    v_hat = v / (1.0 - ADAM_B2 ** ADAM_STEP)
    delta = -ADAM_LR * (m_hat / (jnp.sqrt(v_hat) + ADAM_EPS) + ADAM_WD * w)
    return delta, m, v


def _adam_sum(recv, w, m, v, layer, prev, tr, name):
    L, R, C = w.shape
    Cp = recv.shape[2]
    tr = _tile(R, tr)

    def body(r_ref, w_ref, m_ref, v_ref, *rest):
        g_out, d_out, m_out, v_out = rest[len(rest) - 4:]
        g = r_ref[0, :, 0:C].astype(F32)
        for p in range(1, N_DEV):
            g = g + r_ref[p, :, 0:C].astype(F32)
        delta, mn, vn = _adam_math(g, w_ref[...], m_ref[...], v_ref[...])
        g_out[...] = g
        d_out[...] = delta
        m_out[...] = mn
        v_out[...] = vn

    blk = _bs((None, tr, C), lambda i: (layer, i, 0))
    shp = jax.ShapeDtypeStruct((L, R, C), F32)
    in_specs = [_bs((N_DEV, tr, Cp), lambda i: (0, i, 0)), blk, blk, blk]
    args = [recv, w, m, v]
    aliases = {}
    if prev is not None:
        in_specs += [HBM_SPEC] * 4
        args += list(prev)
        aliases = {4 + k: k for k in range(4)}
    return pl.pallas_call(
        body, name=name, grid=(R // tr,),
        in_specs=in_specs, out_specs=[blk, blk, blk, blk], out_shape=[shp, shp, shp, shp],
        input_output_aliases=aliases, compiler_params=_params(("parallel",)),
    )(*args)


def _dev_index(px, py, pc):
    return 4 * px + 2 * py + pc


class _GatherRider:
    def __init__(self, shards, out_shapes, views, zero_src=None, zero_views=()):
        self.n = len(shards)
        self.views = views
        self.zero_views = list(zero_views) if zero_src is not None else []
        self.srcs = list(shards) + ([zero_src] if self.zero_views else [])
        self.out_shapes = list(out_shapes)
        n, nz = self.n, len(self.zero_views)
        self.scratch = [pltpu.SemaphoreType.DMA((n, 7)), pltpu.SemaphoreType.DMA((n, 7)),
                        pltpu.SemaphoreType.DMA((n,)), pltpu.SemaphoreType.DMA((max(nz, 1),))]

    def _copies(self, ins, outs, sems):
        n = self.n
        send_sems, recv_sems, local_sems, zero_sems = sems
        x, y, c = lax.axis_index("x"), lax.axis_index("y"), lax.axis_index("c")
        me, sibling = (x, y, c), (x, y, 1 - c)
        chips = [(1 - x, y), (x, 1 - y), (1 - x, 1 - y)]

        def rows(a, blk):
            return self.views[a](outs[a], _dev_index(*blk))

        def copy(a, k, blk, to, src=None):
            return pltpu.make_async_remote_copy(
                src_ref=rows(a, blk) if src is None else src, dst_ref=rows(a, blk),
                send_sem=send_sems.at[a, k], recv_sem=recv_sems.at[a, k], device_id=to, device_id_type=MESH_ID)

        local = [pltpu.make_async_copy(ins[a], rows(a, me), local_sems.at[a]) for a in range(n)]
        local += [pltpu.make_async_copy(ins[n], view(outs[a]), zero_sems.at[i])
                  for i, (a, view) in enumerate(self.zero_views)]
        first = []
        for a in range(n):
            first.append(copy(a, 0, me, sibling, src=ins[a]))
            first += [copy(a, 1 + j, me, (*chip, c), src=ins[a]) for j, chip in enumerate(chips)]
        over_ici = [[copy(a, 1 + j, (*chip, c), me) for a in range(n)] for j, chip in enumerate(chips)]
        passed = [[copy(a, 4 + j, (*chip, c), sibling) for a in range(n)] for j, chip in enumerate(chips)]
        from_sibling = [copy(a, 0, sibling, me) for a in range(n)]
        from_sibling += [copy(a, 4 + j, (*chip, 1 - c), me) for a in range(n) for j, chip in enumerate(chips)]
        return local, first, over_ici, passed, from_sibling

    def begin(self, ins, outs, sems):
        local, first, _, _, _ = self._copies(ins, outs, sems)
        for cp in local + first:
            cp.start()

    def middle(self, ins, outs, sems):
        _, _, over_ici, passed, _ = self._copies(ins, outs, sems)
        for arrived, onward in zip(over_ici, passed):
            for cp, fwd in zip(arrived, onward):
                cp.wait_recv()
                fwd.start()

    def end(self, ins, outs, sems):
        local, first, _, passed, from_sibling = self._copies(ins, outs, sems)
        for cp in from_sibling:
            cp.wait_recv()
        for cp in first + [fwd for onward in passed for fwd in onward]:
            cp.wait_send()
        for cp in local:
            cp.wait()


class _ScatterRider:
    _MASKS = [(kx, ky, kc) for kx in (0, 1) for ky in (0, 1) for kc in (0, 1)][1:]

    def __init__(self, srcs, out_shapes, src_of, dst_at):
        self.n = len(srcs)
        self.srcs = list(srcs)
        self.out_shapes = list(out_shapes)
        self.src_of = src_of
        self.dst_at = dst_at
        n = self.n
        self.scratch = [pltpu.SemaphoreType.DMA((n, 7)), pltpu.SemaphoreType.DMA((n, 7)), pltpu.SemaphoreType.DMA((n,))]

    def _copies(self, ins, outs, sems):
        send_sems, recv_sems, local_sems = sems
        x, y, c = lax.axis_index("x"), lax.axis_index("y"), lax.axis_index("c")
        my = _dev_index(x, y, c)
        peers = [(1 - x if kx else x, 1 - y if ky else y, 1 - c if kc else c) for kx, ky, kc in self._MASKS]

        def send(i, k, to):
            return pltpu.make_async_remote_copy(
                src_ref=self.src_of[i](ins[i], _dev_index(*to)), dst_ref=self.dst_at[i](outs[i], my),
                send_sem=send_sems.at[i, k], recv_sem=recv_sems.at[i, k], device_id=to, device_id_type=MESH_ID)

        def arrival(i, k, frm):
            slot = self.dst_at[i](outs[i], _dev_index(*frm))
            return pltpu.make_async_remote_copy(
                src_ref=slot, dst_ref=slot, send_sem=send_sems.at[i, k], recv_sem=recv_sems.at[i, k],
                device_id=frm, device_id_type=MESH_ID)

        local = [pltpu.make_async_copy(self.src_of[i](ins[i], my), self.dst_at[i](outs[i], my), local_sems.at[i])
                 for i in range(self.n)]
        sends = [send(i, k, to) for k, to in enumerate(peers) for i in range(self.n)]
        arrivals = [arrival(i, k, frm) for k, frm in enumerate(peers) for i in range(self.n)]
        return local, sends, arrivals

    def begin(self, ins, outs, sems):
        local, sends, _ = self._copies(ins, outs, sems)
        for cp in local + sends:
            cp.start()

    def middle(self, ins, outs, sems):
        pass

    def end(self, ins, outs, sems):
        local, sends, arrivals = self._copies(ins, outs, sems)
        for cp in arrivals:
            cp.wait_recv()
        for cp in sends:
            cp.wait_send()
        for cp in local:
            cp.wait()


def _comm_call(rider, name):
    k_in = len(rider.srcs)
    k_out = len(rider.out_shapes)

    def body(*refs):
        ins, outs, sems = refs[:k_in], refs[k_in:k_in + k_out], refs[k_in + k_out:]
        rider.begin(ins, outs, sems)
        rider.middle(ins, outs, sems)
        rider.end(ins, outs, sems)

    return pl.pallas_call(
        body, name=name, in_specs=[HBM_SPEC] * k_in, out_specs=[HBM_SPEC] * k_out, out_shape=rider.out_shapes,
        scratch_shapes=rider.scratch, compiler_params=pltpu.CompilerParams(has_side_effects=True),
    )(*rider.srcs)


def _pad_w_in(w):
    cols = lambda a, b: w[..., a:b]
    zeros = lambda n: jnp.zeros(w.shape[:-1] + (n,), w.dtype)
    pairs = FOX_HEADS // 2
    parts = [cols(0, 384), cols(416, 672)]
    for hp in range(pairs):
        parts += [cols(1824 + 2 * hp, 1826 + 2 * hp), zeros(6)]
    parts += [zeros(ROPE_LANE0 - 8 * pairs), cols(384, 416), zeros(LANES - ROPE_LANE0 - MLA_ROPE), cols(672, 1824)]
    return jnp.concatenate(parts, axis=-1)


def _unpad_w_in(g):
    cols = lambda a, b: g[..., a:b]
    rope0 = TAIL0 + ROPE_LANE0
    parts = [cols(0, 384), cols(rope0, rope0 + MLA_ROPE), cols(384, 640), cols(ZA, N_PAD)]
    parts += [cols(TAIL0 + 8 * hp, TAIL0 + 8 * hp + 2) for hp in range(FOX_HEADS // 2)]
    return jnp.concatenate(parts, axis=-1)


def _small_pack(w_q_b, w_kv_b):
    a = jnp.pad(w_q_b, ((0, 0), (0, 0), (0, LANES - w_q_b.shape[2])))
    b = jnp.pad(w_kv_b, ((0, 0), (0, 0), (0, LANES - w_kv_b.shape[2])))
    return jnp.concatenate([a, b], axis=1)


def _small_unpack(p, cq, ckv):
    return p[:, 0:MLA_Q_RANK, 0:cq], p[:, MLA_Q_RANK:, 0:ckv]


def _mla_weights(wsm):
    H = MLA_HEADS
    cq = H * (MLA_NOPE + MLA_ROPE) // N_DEV
    ckv = H * (MLA_NOPE + MLA_V) // N_DEV
    wq = wsm[:, 0:MLA_Q_RANK, 0:cq].transpose(1, 0, 2).reshape(MLA_Q_RANK, H, MLA_NOPE + MLA_ROPE)
    wq = jnp.pad(wq, ((0, 0), (0, 0), (0, HEAD_BLOCK - MLA_NOPE - MLA_ROPE))).reshape(MLA_Q_RANK, H * HEAD_BLOCK)
    wkv = wsm[:, MLA_Q_RANK:, 0:ckv].transpose(1, 0, 2).reshape(MLA_KV_RANK, H, MLA_NOPE + MLA_V)
    wk = jnp.pad(wkv[:, :, 0:MLA_NOPE], ((0, 0), (0, 0), (0, HEAD_BLOCK - MLA_NOPE))).reshape(MLA_KV_RANK, H * HEAD_BLOCK)
    wv = wkv[:, :, MLA_NOPE:].reshape(MLA_KV_RANK, H * MLA_V)
    return wq, jnp.concatenate([wk, wv], axis=1)


def _mla_grads_to_blocks(dwq, dwkv):
    H = MLA_HEADS
    gq = dwq.reshape(MLA_Q_RANK, H, HEAD_BLOCK)[:, :, 0:MLA_NOPE + MLA_ROPE].reshape(MLA_Q_RANK, N_DEV, -1)
    gk = dwkv[:, 0:H * HEAD_BLOCK].reshape(MLA_KV_RANK, H, HEAD_BLOCK)[:, :, 0:MLA_NOPE]
    gv = dwkv[:, H * HEAD_BLOCK:].reshape(MLA_KV_RANK, H, MLA_V)
    gkv = jnp.concatenate([gk, gv], axis=2).reshape(MLA_KV_RANK, N_DEV, -1)
    return _small_pack(gq.transpose(1, 0, 2), gkv.transpose(1, 0, 2)).astype(BF16)


def _pool_blockdiag(pool_w):
    z = jnp.zeros((POOL_GROUP, POOL_GROUP), pool_w.dtype)
    halves = [jnp.concatenate([jnp.concatenate([pool_w[2 * i], z], axis=1),
                               jnp.concatenate([z, pool_w[2 * i + 1]], axis=1)], axis=0) for i in range(2)]
    return jnp.stack(halves)


def _pool_blockdiag_t(dw):
    g = POOL_GROUP
    return jnp.stack([dw[0, 0:g, 0:g], dw[0, g:, g:], dw[1, 0:g, 0:g], dw[1, g:, g:]])


def _gate_row(b):
    return jnp.zeros((LANES,), b.dtype).at[_F_LANES].set(b).reshape(1, LANES)


_SMALL = ("ffn1_norm", "mix_norm", "q_a_norm", "kv_a_norm", "pool_w", "pool_scale", "fox_b_f", "ffn2_norm", "final_norm")


def _pack_small(tree):
    rows, recipe = [], []
    for name in _SMALL:
        a = tree[name]
        flat = a.reshape(-1)
        n = flat.shape[0]
        nrow = -(-n // (8 * LANES)) * 8
        flat = jnp.pad(flat, (0, nrow * LANES - n))
        rows.append(flat.reshape(nrow, LANES))
        recipe.append((name, a.shape, n, nrow))
    return jnp.concatenate(rows, axis=0), recipe


def _unpack_small(packed, recipe):
    out, r0 = {}, 0
    for name, shape, n, nrow in recipe:
        out[name] = packed[r0:r0 + nrow].reshape(-1)[0:n].reshape(shape)
        r0 += nrow
    return out


def _adam_small(packs, w, m, v, name):
    R = w.shape[0]

    def body(p_ref, w_ref, m_ref, v_ref, g_out, d_out, m_out, v_out):
        g = p_ref[0]
        for p in range(1, N_DEV):
            g = g + p_ref[p]
        delta, mn, vn = _adam_math(g, w_ref[...], m_ref[...], v_ref[...])
        g_out[...] = g
        d_out[...] = delta
        m_out[...] = mn
        v_out[...] = vn

    blk = _bs((R, LANES), lambda i: (0, 0))
    shp = jax.ShapeDtypeStruct((R, LANES), F32)
    return pl.pallas_call(
        body, name=name, grid=(1,),
        in_specs=[_bs((N_DEV, R, LANES), lambda i: (0, 0, 0)), blk, blk, blk],
        out_specs=[blk, blk, blk, blk], out_shape=[shp, shp, shp, shp],
        compiler_params=_params(("arbitrary",)),
    )(packs, w, m, v)


_GATHER_PLAN = {
    ("first", 0): (("ffn1_w_gu", 0), ("ffn1_w_down", 0)),
    ("ffn1_fwd", 0): (("w_in", 0), ("w_small", 0), ("w_out", 0), ("ffn2_w_down", 0)),
    ("mla_attn_fwd", 0): (("ffn2_w_gu", 0),),
    ("fox_attn_fwd", 0): (("ffn1_w_down", 1), ("w_in", 1), ("w_small", 1), ("w_out", 1)),
    ("ffn2_fwd", 0): (("ffn1_w_gu", 1),),
    ("mla_attn_fwd", 1): (("ffn2_w_gu", 1), ("ffn2_w_down", 1)),
}
_SCATTER_PLAN = {
    ("mla_attn_bwd", 1): (("ffn2_w_gu", 1), ("w_out", 1)),
    ("fox_attn_bwd", 1): (("ffn2_w_down", 1),),
    ("ffn1_bwd", 1): (("w_in", 1), ("w_small", 1)),
    ("ffn2_bwd", 0): (("ffn1_w_down", 1),),
    ("mla_attn_bwd", 0): (("ffn1_w_gu", 1), ("w_out", 0)),
    ("fox_attn_bwd", 0): (("ffn2_w_gu", 0), ("ffn2_w_down", 0)),
    ("ffn1_bwd_a", 0): (("w_in", 0), ("w_small", 0)),
    ("ffn1_dwgu", 0): (("ffn1_w_down", 0),),
    ("ffn1_bwd_b", 0): (("ffn1_w_gu", 0),),
}
_SPLIT_BWD = (("ffn1", 0),)


class _Exchange:
    def __init__(self, shards, D, f_sh, r_in, r_out):
        self.shards = shards
        self.D, self.f_sh, self.r_in, self.r_out = D, f_sh, r_in, r_out
        self.weights, self.grads, self.recv = {}, {}, {}

    def _rows(self, kind):
        n = {"w_gu": 2 * self.f_sh, "w_down": self.f_sh}[kind]
        return lambda ref, p: ref.at[pl.ds(pl.multiple_of(p * n, 16), n)]

    def _gathered_shape(self, kind):
        D, f_sh = self.D, self.f_sh
        return {"w_gu": (N_DEV * 2 * f_sh, D), "w_down": (N_DEV * f_sh, D), "w_in": (N_DEV, self.r_in, N_PAD),
                "w_small": (N_DEV, MLA_Q_RANK + MLA_KV_RANK, LANES), "w_out": (N_DEV, self.r_out, D)}[kind]

    def _recv_shape(self, kind):
        D, f_sh = self.D, self.f_sh
        return {"w_gu": (N_DEV, 2 * f_sh, D), "w_down": (N_DEV, f_sh, D), "w_in": (N_DEV, self.r_in, N_IN),
                "w_small": (N_DEV, MLA_Q_RANK + MLA_KV_RANK, LANES), "w_out": (N_DEV, self.r_out, D)}[kind]

    @staticmethod
    def _kind(name):
        return name[5:] if name.startswith("ffn") else name

    def gather_rider(self, call, l):
        keys = _GATHER_PLAN.get((call, l))
        if not keys:
            return None
        by_dev = lambda ref, p: ref.at[p]
        shards, shapes, views = [], [], []
        for key in keys:
            kind = self._kind(key[0])
            shards.append(self.shards[key])
            shapes.append(jax.ShapeDtypeStruct(self._gathered_shape(kind), BF16))
            views.append(self._rows(kind) if kind in ("w_gu", "w_down") else by_dev)
        return _GatherRider(shards, shapes, views)

    def gathered(self, call, l, outs):
        for key, w in zip(_GATHER_PLAN.get((call, l), ()), outs):
            if self._kind(key[0]) in ("w_in", "w_out"):
                w = w.reshape((N_DEV * w.shape[1],) + w.shape[2:])
            self.weights[key] = w

    def scatter_rider(self, call, l, pack=None):
        keys = _SCATTER_PLAN.get((call, l), ())
        if not keys and pack is None:
            return None
        by_dev = lambda ref, p: ref.at[p]
        srcs, shapes, src_of = [], [], []
        for key in keys:
            kind = self._kind(key[0])
            srcs.append(self.grads[key])
            shapes.append(jax.ShapeDtypeStruct(self._recv_shape(kind), BF16))
            src_of.append(self._rows(kind) if kind in ("w_gu", "w_down") else by_dev)
        if pack is not None:
            srcs.append(pack)
            shapes.append(jax.ShapeDtypeStruct((N_DEV,) + pack.shape, pack.dtype))
            src_of.append(lambda ref, p: ref)
        return _ScatterRider(srcs, shapes, src_of, [by_dev] * len(srcs))

    def scattered(self, call, l, outs):
        for key, r in zip(_SCATTER_PLAN.get((call, l), ()), outs):
            self.recv[key] = r


def _local_step(x, target, ex, small):
    S, D = x.shape
    tabs = _rope_tables(S)
    nhp_a, nhp_c = MLA_HEADS // 2, FOX_HEADS // 2
    fox_scale = 1.0 / math.sqrt(FOX_HEAD_DIM)
    ex.gathered("first", 0, _comm_call(ex.gather_rider("first", 0), "gather_first"))
    saved = []
    for l in range(DEPTH):
        s = {}
        s["x0"] = x
        wgu1, wd1 = ex.weights[("ffn1_w_gu", l)], ex.weights[("ffn1_w_down", l)]
        (x1, s["h1"], s["gu1"]), got = _ffn_fwd_full(x, small["ffn1_norm"][l][None], wgu1, wd1, FFN_FWD_TOKENS,
                                                    f"ffn1_fwd_l{l}", rider=ex.gather_rider("ffn1_fwd", l))
        ex.gathered("ffn1_fwd", l, got)
        s["x1"] = x1
        w_in = ex.weights[("w_in", l)]
        s["h2"], za, zf, zkvt = _mix_in_fwd(x1, small["mix_norm"][l][None], w_in, f"mix_in_fwd_l{l}")
        s["za"], s["zf"], s["zkvt"] = za, zf, zkvt
        wq, wkv = _mla_weights(ex.weights[("w_small", l)])
        s["wq"], s["wkv"] = wq, wkv
        gq, gkv = small["q_a_norm"][l][None], small["kv_a_norm"][l][None]
        qf, kf, vm, kft, vmt = _mla_prep(za, gq, gkv, wq, wkv, tabs, f"mla_prep_l{l}")
        s["qf"], s["kf"], s["vm"], s["kft"] = qf, kf, vm, kft
        (ya, lse_a), got = _attn_fwd_t(qf, kf, vmt, nhp=nhp_a, dkb=LANES, qoff=0, koff=0, vtoff=0, scale=1.0,
                                       cum=None, cumT=None, name=f"mla_attn_fwd_l{l}",
                                       rider=ex.gather_rider("mla_attn_fwd", l))
        ex.gathered("mla_attn_fwd", l, got)
        s["ya"], s["lse_a"] = ya, lse_a
        b_row = _gate_row(small["fox_b_f"][l])
        s["b_row"] = b_row
        cum, cumT = _fox_prep(za, b_row, f"fox_prep_l{l}")
        s["cum"], s["cumT"] = cum, cumT
        (yc, lse_c), got = _attn_fwd_t(zf, zf, zkvt, nhp=nhp_c, dkb=64, qoff=0, koff=nhp_c, vtoff=nhp_c, scale=fox_scale,
                                       cum=cum, cumT=cumT, name=f"fox_attn_fwd_l{l}",
                                       rider=ex.gather_rider("fox_attn_fwd", l))
        ex.gathered("fox_attn_fwd", l, got)
        s["yc"], s["lse_c"] = yc, lse_c
        wbd = _pool_blockdiag(small["pool_w"][l]).astype(BF16)
        s["wbd"] = wbd
        psc = small["pool_scale"][l][None]
        yb, s["pd"] = _pool_fwd(za, wbd, psc, f"pool_fwd_l{l}")
        w_out = ex.weights[("w_out", l)]
        x2, s["ycat"] = _mix_out_fwd(x1, ya, yb, yc, w_out, f"mix_out_fwd_l{l}")
        s["x2"] = x2
        wgu2, wd2 = ex.weights[("ffn2_w_gu", l)], ex.weights[("ffn2_w_down", l)]
        (x, s["h3"], s["gu2"]), got = _ffn_fwd_full(x2, small["ffn2_norm"][l][None], wgu2, wd2, FFN_FWD_TOKENS,
                                                    f"ffn2_fwd_l{l}", rider=ex.gather_rider("ffn2_fwd", l))
        ex.gathered("ffn2_fwd", l, got)
        saved.append(s)

    dx, d_final, loss = _loss_head(x, small["final_norm"][None], target, "loss_head")

    small_grads = [None] * DEPTH
    for l in reversed(range(DEPTH)):
        s = saved[l]
        g = {}
        wgu2, wd2 = ex.weights[("ffn2_w_gu", l)], ex.weights[("ffn2_w_down", l)]
        dy3 = dx
        (dx, g["ffn2_norm"], dgu, act, dyh), got = _ffn_bwd_full(
            dy3, s["x2"], small["ffn2_norm"][l][None], s["gu2"], wgu2, wd2, FFN_BWD_TOKENS, f"ffn2_bwd_l{l}",
            rider=ex.scatter_rider("ffn2_bwd", l))
        ex.scattered("ffn2_bwd", l, got)
        ex.grads[("ffn2_w_gu", l)] = _ffn_weight_grad(dgu, s["h3"], f"ffn2_dwgu_l{l}")
        ex.grads[("ffn2_w_down", l)] = _ffn_weight_grad(act, dyh, f"ffn2_dwd_l{l}")

        w_out = ex.weights[("w_out", l)]
        dya, dyb, dyc, dvec_a, dvec_c = _mix_out_bwd(dx, w_out, s["ya"], s["yc"], f"mix_out_bwd_l{l}")
        dw_out = _mm_tn(s["ycat"][None], dx[None], 1, lambda p: 0, lambda p: 0, 1024, 1024, f"dwout_l{l}", ts=1024)[0]
        ex.grads[("w_out", l)] = dw_out.reshape(N_DEV, ex.r_out, D)

        (dqf, dkf, dvm), got = _attn_bwd_t(s["qf"], s["kf"], s["kft"], s["vm"], dya, s["lse_a"], dvec_a, nhp=nhp_a,
                                           dkb=LANES, qoff=0, koff=0, ktoff=0, voff=0, scale=1.0, cum=None, cumT=None,
                                           name=f"mla_attn_bwd_l{l}", rider=ex.scatter_rider("mla_attn_bwd", l))
        ex.scattered("mla_attn_bwd", l, got)
        zf = s["zf"]
        (dqc, dkc, dvc, dcq, dck), got = _attn_bwd_t(zf, zf, s["zkvt"], zf, dyc, s["lse_c"], dvec_c, nhp=nhp_c, dkb=64,
                                                     qoff=0, koff=nhp_c, ktoff=0, voff=2 * nhp_c, scale=fox_scale,
                                                     cum=s["cum"], cumT=s["cumT"], name=f"fox_attn_bwd_l{l}",
                                                     rider=ex.scatter_rider("fox_attn_bwd", l))
        ex.scattered("fox_attn_bwd", l, got)
        dtail_f, db = _fox_prep_bwd(s["za"], s["b_row"], dcq, dck, f"fox_prep_bwd_l{l}")
        g["fox_b_f"] = db[0, _F_LANES]
        psc = small["pool_scale"][l][None]
        du, dwbd, dpsc = _pool_bwd(dyb, s["pd"], s["wbd"], psc, f"pool_bwd_l{l}")
        g["pool_w"] = _pool_blockdiag_t(dwbd)
        g["pool_scale"] = dpsc[0]
        gq, gkv = small["q_a_norm"][l][None], small["kv_a_norm"][l][None]
        dza, dwq, dwkv, dgq, dgkv = _mla_prep_bwd(s["za"], gq, gkv, s["wq"], s["wkv"], tabs, dqf, dkf, dvm,
                                                   f"mla_prep_bwd_l{l}")
        g["q_a_norm"], g["kv_a_norm"] = dgq[0], dgkv[0]
        ex.grads[("w_small", l)] = _mla_grads_to_blocks(dwq, dwkv)
        w_in = ex.weights[("w_in", l)]
        dx, g["mix_norm"], dz = _mix_in_bwd(dx, s["x1"], small["mix_norm"][l][None], dza, du, dtail_f, dqc, dkc, dvc,
                                            w_in, f"mix_in_bwd_l{l}")
        dw_in = _unpad_w_in(_mm_tn(s["h2"][None], dz[None], 1, lambda p: 0, lambda p: 0, 1024, 640, f"dwin_l{l}",
                                   ts=4096)[0])
        ex.grads[("w_in", l)] = dw_in.reshape(N_DEV, ex.r_in, N_IN)

        wgu1, wd1 = ex.weights[("ffn1_w_gu", l)], ex.weights[("ffn1_w_down", l)]
        dy1 = dx
        gam1 = small["ffn1_norm"][l][None]
        split = ("ffn1", l) in _SPLIT_BWD
        if split:
            (dgu, act, dyh), got = _ffn_bwd_full(dy1, None, None, s["gu1"], None, wd1, FFN_BWD_TOKENS, f"ffn1_bwd_a_l{l}",
                                                 phase="act", rider=ex.scatter_rider("ffn1_bwd_a", l))
            ex.scattered("ffn1_bwd_a", l, got)
        else:
            (dx, g["ffn1_norm"], dgu, act, dyh), got = _ffn_bwd_full(
                dy1, s["x0"], gam1, s["gu1"], wgu1, wd1, FFN_BWD_TOKENS, f"ffn1_bwd_l{l}",
                rider=ex.scatter_rider("ffn1_bwd", l))
            ex.scattered("ffn1_bwd", l, got)
        ex.grads[("ffn1_w_down", l)] = _ffn_weight_grad(act, dyh, f"ffn1_dwd_l{l}")
        rider = ex.scatter_rider("ffn1_dwgu", l)
        dwgu = _ffn_weight_grad(dgu, s["h1"], f"ffn1_dwgu_l{l}", rider=rider)
        if rider is not None:
            dwgu, got = dwgu
            ex.scattered("ffn1_dwgu", l, got)
        ex.grads[("ffn1_w_gu", l)] = dwgu
        if split:
            (dx, g["ffn1_norm"]), got = _ffn_bwd_full(dy1, s["x0"], gam1, None, wgu1, None, FFN_BWD_TOKENS,
                                                     f"ffn1_bwd_b_l{l}", phase="in", dgu_in=dgu,
                                                     rider=ex.scatter_rider("ffn1_bwd_b", l))
            ex.scattered("ffn1_bwd_b", l, got)
        for k in ("ffn1_norm", "ffn2_norm", "mix_norm"):
            g[k] = g[k][0]
        small_grads[l] = g
    return loss, dx, small_grads, d_final[0]


_BIG = ("ffn1_w_gu", "ffn1_w_down", "w_in", "w_small", "w_out", "ffn2_w_gu", "ffn2_w_down")


def kernel(x, ffn1_norm, ffn1_w_gu, ffn1_w_down, mix_norm, w_in, q_a_norm, w_q_b, kv_a_norm, w_kv_b, pool_w, pool_scale, fox_b_f, w_out, ffn2_norm, ffn2_w_gu, ffn2_w_down, final_norm, loss_target, m_ffn1_norm, m_ffn1_w_gu, m_ffn1_w_down, m_mix_norm, m_w_in, m_q_a_norm, m_w_q_b, m_kv_a_norm, m_w_kv_b, m_pool_w, m_pool_scale, m_fox_b_f, m_w_out, m_ffn2_norm, m_ffn2_w_gu, m_ffn2_w_down, m_final_norm, v_ffn1_norm, v_ffn1_w_gu, v_ffn1_w_down, v_mix_norm, v_w_in, v_q_a_norm, v_w_q_b, v_kv_a_norm, v_w_kv_b, v_pool_w, v_pool_scale, v_fox_b_f, v_w_out, v_ffn2_norm, v_ffn2_w_gu, v_ffn2_w_down, v_final_norm):
    W = dict(ffn1_norm=ffn1_norm, ffn1_w_gu=ffn1_w_gu, ffn1_w_down=ffn1_w_down, mix_norm=mix_norm, w_in=w_in,
             q_a_norm=q_a_norm, w_q_b=w_q_b, kv_a_norm=kv_a_norm, w_kv_b=w_kv_b, pool_w=pool_w, pool_scale=pool_scale,
             fox_b_f=fox_b_f, w_out=w_out, ffn2_norm=ffn2_norm, ffn2_w_gu=ffn2_w_gu, ffn2_w_down=ffn2_w_down,
             final_norm=final_norm)
    M = dict(ffn1_norm=m_ffn1_norm, ffn1_w_gu=m_ffn1_w_gu, ffn1_w_down=m_ffn1_w_down, mix_norm=m_mix_norm, w_in=m_w_in,
             q_a_norm=m_q_a_norm, w_q_b=m_w_q_b, kv_a_norm=m_kv_a_norm, w_kv_b=m_w_kv_b, pool_w=m_pool_w,
             pool_scale=m_pool_scale, fox_b_f=m_fox_b_f, w_out=m_w_out, ffn2_norm=m_ffn2_norm, ffn2_w_gu=m_ffn2_w_gu,
             ffn2_w_down=m_ffn2_w_down, final_norm=m_final_norm)
    V = dict(ffn1_norm=v_ffn1_norm, ffn1_w_gu=v_ffn1_w_gu, ffn1_w_down=v_ffn1_w_down, mix_norm=v_mix_norm, w_in=v_w_in,
             q_a_norm=v_q_a_norm, w_q_b=v_w_q_b, kv_a_norm=v_kv_a_norm, w_kv_b=v_w_kv_b, pool_w=v_pool_w,
             pool_scale=v_pool_scale, fox_b_f=v_fox_b_f, w_out=v_w_out, ffn2_norm=v_ffn2_norm, ffn2_w_gu=v_ffn2_w_gu,
             ffn2_w_down=v_ffn2_w_down, final_norm=v_final_norm)
    L, D, n_sh = ffn1_w_gu.shape
    f_sh = ffn1_w_down.shape[1]
    assert n_sh == 2 * f_sh and L == DEPTH
    r_in, r_out = w_in.shape[1], w_out.shape[1]

    tr_in = lambda a: a.transpose(0, 2, 1)
    big_shards = dict(
        ffn1_w_gu=tr_in(ffn1_w_gu).astype(BF16), ffn1_w_down=ffn1_w_down.astype(BF16),
        w_in=_pad_w_in(w_in).astype(BF16), w_small=_small_pack(w_q_b, w_kv_b).astype(BF16), w_out=w_out.astype(BF16),
        ffn2_w_gu=tr_in(ffn2_w_gu).astype(BF16), ffn2_w_down=ffn2_w_down.astype(BF16))
    ex = _Exchange({(k, l): big_shards[k][l] for k in _BIG for l in range(L)}, D, f_sh, r_in, r_out)

    small = {k: W[k] for k in _SMALL}
    loss, dx, grads, d_final = _local_step(x[0], loss_target[0], ex, small)

    small_g = {k: jnp.stack([grads[l][k] for l in range(L)]) for k in _SMALL if k != "final_norm"}
    small_g["final_norm"] = d_final
    pack_g, recipe = _pack_small(small_g)
    n_small = pack_g.shape[0]
    loss_row = -(-n_small // 8) * 8
    pack_g = jnp.concatenate([pack_g, jnp.zeros((loss_row - n_small, LANES), F32), jnp.broadcast_to(loss, (8, LANES))],
                             axis=0)
    *got, packs = _comm_call(ex.scatter_rider("last", 0, pack=pack_g), "scatter_last")
    ex.scattered("last", 0, got)

    out = {}
    sm_w, sm_m, sm_v = (_small_pack(t["w_q_b"], t["w_kv_b"]) for t in (W, M, V))
    big = [("ffn1_w_gu", tr_in(W["ffn1_w_gu"]), tr_in(M["ffn1_w_gu"]), tr_in(V["ffn1_w_gu"]), f_sh),
           ("ffn1_w_down", W["ffn1_w_down"], M["ffn1_w_down"], V["ffn1_w_down"], 352),
           ("w_in", W["w_in"], M["w_in"], V["w_in"], 128),
           ("w_small", sm_w, sm_m, sm_v, 384),
           ("w_out", W["w_out"], M["w_out"], V["w_out"], 128),
           ("ffn2_w_gu", tr_in(W["ffn2_w_gu"]), tr_in(M["ffn2_w_gu"]), tr_in(V["ffn2_w_gu"]), f_sh),
           ("ffn2_w_down", W["ffn2_w_down"], M["ffn2_w_down"], V["ffn2_w_down"], 352)]
    for k, w_, m_, v_, tr in big:
        res = None
        for l in range(L):
            res = _adam_sum(ex.recv[(k, l)], w_, m_, v_, l, res, tr, f"adam_{k}_l{l}")
        if k == "w_small":
            cq, ckv = w_q_b.shape[2], w_kv_b.shape[2]
            parts = [_small_unpack(r, cq, ckv) for r in res]
            out["w_q_b"] = [p[0] for p in parts]
            out["w_kv_b"] = [p[1] for p in parts]
        elif k.endswith("w_gu"):
            out[k] = [tr_in(r) for r in res]
        else:
            out[k] = res

    pw, _ = _pack_small({k: W[k] for k in _SMALL})
    pm, _ = _pack_small({k: M[k] for k in _SMALL})
    pv, _ = _pack_small({k: V[k] for k in _SMALL})
    extra = ((0, loss_row + 8 - n_small), (0, 0))
    res = _adam_small(packs, jnp.pad(pw, extra), jnp.pad(pm, extra), jnp.pad(pv, extra), "adam_small")
    loss_total = res[0][loss_row, 0]
    small_out = [_unpack_small(r, recipe) for r in res]
    for k in _SMALL:
        out[k] = [t[k] for t in small_out]

    names = ["ffn1_norm", "ffn1_w_gu", "ffn1_w_down", "mix_norm", "w_in", "q_a_norm", "w_q_b", "kv_a_norm", "w_kv_b",
             "pool_w", "pool_scale", "fox_b_f", "w_out", "ffn2_norm", "ffn2_w_gu", "ffn2_w_down", "final_norm"]
    outs = [loss_total, dx[None]]
    for which in range(4):
        outs += [out[k][which] for k in names]
    return tuple(outs)
```

```python
import functools
import math

import numpy as np
import jax
import jax.numpy as jnp
from jax import lax
from jax.experimental import pallas as pl
from jax.experimental.pallas import tpu as pltpu

F32 = jnp.float32
BF16 = jnp.bfloat16
MESH_ID = pl.DeviceIdType.MESH

N_DEV = 8
EPS = 1e-6
DEPTH = 2

MLA_HEADS = 6
MLA_Q_RANK = 256
MLA_KV_RANK = 128
MLA_NOPE = 64
MLA_ROPE = 32
MLA_V = 64
ROPE_THETA = 10000.0
POOL_WINDOWS = (2, 4, 8, 16)
POOL_GROUP = 64
POOL_WIDTH = 256
FOX_HEADS = 6
FOX_HEAD_DIM = 64
N_IN = 1830

ADAM_LR = 0.001
ADAM_B1 = 0.9
ADAM_B2 = 0.999
ADAM_EPS = 1e-08
ADAM_WD = 0.01
ADAM_STEP = 10

LANES = 128
HEAD_BLOCK = 128
VMEM_LIMIT = 48 * 1024 * 1024
VMEM_LIMIT_BIG = 50 * 1024 * 1024
NEG = -1e30
ATTN_BLOCK = 1024

ZA = 768
ZF = 1152
N_PAD = ZA + ZF
TAIL0 = 640
ROPE_LANE0 = 64


def _f_lane(h):
    return 8 * (h // 2) + (h % 2)


_F_LANES = np.array([_f_lane(h) for h in range(FOX_HEADS)], np.int32)


def _dot(a, b):
    return jnp.dot(a, b, preferred_element_type=F32)


def _dot_nt(a, b):
    return lax.dot_general(a, b, (((1,), (1,)), ((), ())), preferred_element_type=F32)


def _dot_tn(a, b):
    return lax.dot_general(a, b, (((0,), (0,)), ((), ())), preferred_element_type=F32)


def _rms(x, gam):
    r = lax.rsqrt(jnp.mean(x * x, axis=-1, keepdims=True) + EPS)
    return x * r * gam


def _rms_bwd(dy, x, gam):
    r = lax.rsqrt(jnp.mean(x * x, axis=-1, keepdims=True) + EPS)
    xh = x * r
    dxh = dy * gam
    dx = r * (dxh - xh * jnp.mean(dxh * xh, axis=-1, keepdims=True))
    return dx, jnp.sum(dy * xh, axis=0, keepdims=True)


def _accum_out(ref, first, val):
    @pl.when(first)
    def _():
        ref[...] = val

    @pl.when(jnp.logical_not(first))
    def _():
        ref[...] += val


def _bs(shape, fn):
    return pl.BlockSpec(shape, fn)


def _params(dims, vmem=VMEM_LIMIT):
    return pltpu.CompilerParams(dimension_semantics=dims, vmem_limit_bytes=vmem)


def _tile(n, t):
    t = min(n, t)
    assert n % t == 0, (n, t)
    return t


HBM_SPEC = pl.BlockSpec(memory_space=pl.ANY)


def _call(body, *, name, grid, in_specs, out_specs, out_shape, scratch, dims, args, rider=None, vmem=VMEM_LIMIT):
    n_in, n_out = len(in_specs), len(out_specs)
    if rider is None:
        outs = pl.pallas_call(body, name=name, grid=grid, in_specs=in_specs, out_specs=out_specs, out_shape=out_shape,
                              scratch_shapes=scratch, compiler_params=_params(dims, vmem))(*args)
        return list(outs), []
    k_in, k_out, k_sem = len(rider.srcs), len(rider.out_shapes), len(rider.scratch)

    def riding(*refs):
        a, b, c, d = n_in, n_in + k_in, n_in + k_in + n_out, n_in + k_in + n_out + k_out
        rest = refs[d:]
        sems = rest[len(rest) - k_sem:]
        step = 0
        for i, g in enumerate(grid):
            step = step * g + pl.program_id(i)
        n_steps = math.prod(grid)

        @pl.when(step == 0)
        def _():
            rider.begin(refs[a:b], refs[c:d], sems)

        body(*refs[:a], *refs[b:c], *rest[:len(rest) - k_sem])

        @pl.when(step == (3 * n_steps) // 4)
        def _():
            rider.middle(refs[a:b], refs[c:d], sems)

        @pl.when(step == n_steps - 1)
        def _():
            rider.end(refs[a:b], refs[c:d], sems)

    outs = pl.pallas_call(
        riding, name=name, grid=grid, in_specs=list(in_specs) + [HBM_SPEC] * k_in,
        out_specs=list(out_specs) + [HBM_SPEC] * k_out, out_shape=list(out_shape) + list(rider.out_shapes),
        scratch_shapes=list(scratch) + list(rider.scratch),
        compiler_params=_params(("arbitrary",) * len(grid), vmem))(*args, *rider.srcs)
    return list(outs[:n_out]), list(outs[n_out:])


_ONCE = pl.Buffered(1)
_FF_CHUNKS = ((0, 1536), (1536, 2816))


def _ffn_fwd_full(x, gam, wgut, wd, tm, name, rider=None):
    S, D = x.shape
    F = wd.shape[0]
    tm = _tile(S, tm)
    chunks = _FF_CHUNKS if F == 2816 else ((0, F),)

    def body(x_ref, gam_ref, wgut_ref, wd_ref, xo_ref, h_ref, gu_ref):
        h = _rms(x_ref[...], gam_ref[...]).astype(BF16)
        h_ref[...] = h
        y = None
        for c0, c1 in chunks:
            g = _dot_nt(h, wgut_ref[c0:c1, :])
            u = _dot_nt(h, wgut_ref[F + c0:F + c1, :])
            gu_ref[:, c0:c1] = g.astype(BF16)
            gu_ref[:, F + c0:F + c1] = u.astype(BF16)
            a = (g * jax.nn.sigmoid(g) * u).astype(BF16)
            part = _dot(a, wd_ref[c0:c1, :])
            y = part if y is None else y + part
        xo_ref[...] = x_ref[...] + 0.5 * y

    row = lambda i: (i, 0)
    fix = lambda i: (0, 0)
    return _call(
        body, name=name, grid=(S // tm,),
        in_specs=[_bs((tm, D), row), _bs((1, D), fix), pl.BlockSpec((2 * F, D), fix, pipeline_mode=_ONCE),
                  pl.BlockSpec((F, D), fix, pipeline_mode=_ONCE)],
        out_specs=[_bs((tm, D), row), _bs((tm, D), row), _bs((tm, 2 * F), row)],
        out_shape=[jax.ShapeDtypeStruct((S, D), F32), jax.ShapeDtypeStruct((S, D), BF16),
                   jax.ShapeDtypeStruct((S, 2 * F), BF16)],
        scratch=[], dims=("parallel",), args=(x, gam, wgut, wd), rider=rider)


def _ffn_bwd_full(dy, x, gam, gu, wgut, wd, tm, name, phase="all", dgu_in=None, rider=None):
    S, D = dy.shape
    F = wd.shape[0] if wd is not None else wgut.shape[0] // 2
    tm = _tile(S, tm)
    chunks = _FF_CHUNKS if F == 2816 else ((0, F),)
    act, inp = phase in ("all", "act"), phase in ("all", "in")

    def body(*refs):
        refs = list(refs)
        dy_ref = refs.pop(0)
        x_ref, gam_ref = (refs.pop(0), refs.pop(0)) if inp else (None, None)
        gu_ref = refs.pop(0)
        wgut_ref = refs.pop(0) if inp else None
        wd_ref = refs.pop(0) if act else None
        if inp:
            dx_ref, dgam_ref = refs.pop(0), refs.pop(0)
        if act:
            dgu_ref, a_ref, dyh_ref = refs.pop(0), refs.pop(0), refs.pop(0)
            dyh = (0.5 * dy_ref[...]).astype(BF16)
            dyh_ref[...] = dyh
        dh = None
        for c0, c1 in chunks:
            if act:
                dg, du, a = _swiglu_bwd(_dot_nt(dyh, wd_ref[c0:c1, :]), gu_ref[:, c0:c1], gu_ref[:, F + c0:F + c1])
                dgu_ref[:, c0:c1] = dg
                dgu_ref[:, F + c0:F + c1] = du
                a_ref[:, c0:c1] = a
            else:
                dg, du = gu_ref[:, c0:c1], gu_ref[:, F + c0:F + c1]
            if inp:
                part = _dot(dg, wgut_ref[c0:c1, :]) + _dot(du, wgut_ref[F + c0:F + c1, :])
                dh = part if dh is None else dh + part
        if inp:
            dxn, dgam = _rms_bwd(dh, x_ref[...], gam_ref[...])
            dx_ref[...] = dy_ref[...] + dxn
            _accum_out(dgam_ref, pl.program_id(0) == 0, dgam)

    row = lambda i: (i, 0)
    fix = lambda i: (0, 0)
    in_specs, args = [_bs((tm, D), row)], [dy]
    if inp:
        in_specs += [_bs((tm, D), row), _bs((1, D), fix)]
        args += [x, gam]
    in_specs += [_bs((tm, 2 * F), row)]
    args += [gu if act else dgu_in]
    if inp:
        in_specs += [pl.BlockSpec((2 * F, D), fix, pipeline_mode=_ONCE)]
        args += [wgut]
    if act:
        in_specs += [pl.BlockSpec((F, D), fix, pipeline_mode=_ONCE)]
        args += [wd]
    out_specs, out_shape = [], []
    if inp:
        out_specs += [_bs((tm, D), row), _bs((1, D), fix)]
        out_shape += [jax.ShapeDtypeStruct((S, D), F32), jax.ShapeDtypeStruct((1, D), F32)]
    if act:
        out_specs += [_bs((tm, 2 * F), row), _bs((tm, F), row), _bs((tm, D), row)]
        out_shape += [jax.ShapeDtypeStruct((S, 2 * F), BF16), jax.ShapeDtypeStruct((S, F), BF16),
                      jax.ShapeDtypeStruct((S, D), BF16)]
    return _call(body, name=name, grid=(S // tm,), in_specs=in_specs, out_specs=out_specs, out_shape=out_shape,
                 scratch=[], dims=("arbitrary",), args=tuple(args), rider=rider, vmem=VMEM_LIMIT_BIG)


def _swiglu_bwd(da, g, u):
    g = g.astype(F32)
    u = u.astype(F32)
    sig = jax.nn.sigmoid(g)
    sl = g * sig
    dg = (da * u * (sig * (1.0 + g * (1.0 - sig)))).astype(BF16)
    return dg, (da * sl).astype(BF16), (sl * u).astype(BF16)


def _mm_tn(a, b, nb, a_of, b_of, tm, tn, name, rider=None, ts=512):
    _, S, M = a.shape
    N = b.shape[2]
    tm = _tile(M, tm)
    tn = _tile(N, tn)
    ts = _tile(S, ts)
    nk = S // ts

    def body(a_ref, b_ref, o_ref, acc):
        k = pl.program_id(3)

        @pl.when(k == 0)
        def _():
            acc[...] = jnp.zeros_like(acc)

        acc[...] += _dot_tn(a_ref[...].astype(BF16), b_ref[...].astype(BF16))

        @pl.when(k == nk - 1)
        def _():
            o_ref[...] = acc[...].astype(o_ref.dtype)

    (out,), extra = _call(
        body, name=name, grid=(nb, M // tm, N // tn, nk),
        in_specs=[_bs((None, ts, tm), lambda p, i, j, k: (a_of(p), k, i)),
                  _bs((None, ts, tn), lambda p, i, j, k: (b_of(p), k, j))],
        out_specs=[_bs((None, tm, tn), lambda p, i, j, k: (p, i, j))],
        out_shape=[jax.ShapeDtypeStruct((nb, M, N), BF16)],
        scratch=[pltpu.VMEM((tm, tn), F32)],
        dims=("parallel", "parallel", "parallel", "arbitrary"), args=(a, b), rider=rider)
    return (out, extra) if rider is not None else out


FFN_FWD_TOKENS = 512
FFN_BWD_TOKENS = 256
FFN_GRAD_ROWS = 1408


def _ffn_weight_grad(a, b, name, rider=None):
    M = a.shape[1]
    tm = max(t for t in range(LANES, FFN_GRAD_ROWS + 1, LANES) if M % t == 0)
    res = _mm_tn(a[None], b[None], 1, lambda p: 0, lambda p: 0, tm, b.shape[1], name, rider=rider, ts=2048)
    return (res[0][0], res[1]) if rider is not None else res[0]


def _mix_in_fwd(x, gam, w_in, name):
    S, D = x.shape
    tm = _tile(S, 512)
    nkv = (ZF - 384) // LANES

    def body(x_ref, gam_ref, w_ref, h_ref, za_ref, zf_ref, zt_ref):
        hb = _rms(x_ref[...], gam_ref[...]).astype(BF16)
        h_ref[...] = hb
        za_ref[...] = _dot(hb, w_ref[:, 0:ZA])
        zf = _dot(hb, w_ref[:, ZA:N_PAD])
        zf_ref[...] = zf.astype(BF16)
        for c in range(nkv):
            zt_ref[c * LANES:(c + 1) * LANES, :] = zf[:, 384 + c * LANES:384 + (c + 1) * LANES].T.astype(BF16)

    return pl.pallas_call(
        body, name=name, grid=(S // tm,),
        in_specs=[_bs((tm, D), lambda i: (i, 0)), _bs((1, D), lambda i: (0, 0)), _bs((D, N_PAD), lambda i: (0, 0))],
        out_specs=[_bs((tm, D), lambda i: (i, 0)), _bs((tm, ZA), lambda i: (i, 0)), _bs((tm, ZF), lambda i: (i, 0)),
                   _bs((nkv * LANES, tm), lambda i: (0, i))],
        out_shape=[jax.ShapeDtypeStruct((S, D), BF16), jax.ShapeDtypeStruct((S, ZA), F32),
                   jax.ShapeDtypeStruct((S, ZF), BF16), jax.ShapeDtypeStruct((nkv * LANES, S), BF16)],
        compiler_params=_params(("parallel",)),
    )(x, gam, w_in)


def _mix_in_bwd(dy, x, gam, dza_mla, du, dtail_f, dqf, dkf, dvf, w_in, name):
    S, D = x.shape
    tm = _tile(S, 512)

    def body(dy_ref, x_ref, gam_ref, dza_ref, du_ref, dt_ref, dq_ref, dk_ref, dv_ref, w_ref, dx_ref, dgam_ref, dz_ref):
        i = pl.program_id(0)
        dza = dza_ref[...]
        dz = jnp.concatenate([dza[:, 0:384], du_ref[...], dza[:, TAIL0:ZA] + dt_ref[...],
                              dq_ref[...], dk_ref[...], dv_ref[...]], axis=1).astype(BF16)
        dz_ref[...] = dz
        dh = _dot_nt(dz, w_ref[...])
        dxn, dgam = _rms_bwd(dh, x_ref[...], gam_ref[...])
        dx_ref[...] = dy_ref[...] + dxn
        _accum_out(dgam_ref, i == 0, dgam)

    row = lambda i: (i, 0)
    fix = lambda i: (0, 0)
    return pl.pallas_call(
        body, name=name, grid=(S // tm,),
        in_specs=[_bs((tm, D), row), _bs((tm, D), row), _bs((1, D), fix), _bs((tm, ZA), row), _bs((tm, 256), row),
                  _bs((tm, 128), row), _bs((tm, 384), row), _bs((tm, 384), row), _bs((tm, 384), row),
                  _bs((D, N_PAD), fix)],
        out_specs=[_bs((tm, D), row), _bs((1, D), fix), _bs((tm, N_PAD), row)],
        out_shape=[jax.ShapeDtypeStruct((S, D), F32), jax.ShapeDtypeStruct((1, D), F32),
                   jax.ShapeDtypeStruct((S, N_PAD), BF16)],
        compiler_params=_params(("arbitrary",)),
    )(dy, x, gam, dza_mla, du, dtail_f, dqf, dkf, dvf, w_in)


def _rope_tables(S):
    half = MLA_ROPE // 2
    inv_freq = ROPE_THETA ** (-jnp.arange(0, MLA_ROPE, 2, dtype=F32) / MLA_ROPE)
    ang = jnp.arange(S, dtype=jnp.int32).astype(F32)[:, None] * inv_freq[None, :]
    cos, sin = jnp.cos(ang), jnp.sin(ang)
    one = jnp.ones((S, ROPE_LANE0), F32)
    zero = jnp.zeros((S, ROPE_LANE0), F32)
    pad1 = jnp.ones((S, LANES - ROPE_LANE0 - MLA_ROPE), F32)
    pad0 = jnp.zeros((S, LANES - ROPE_LANE0 - MLA_ROPE), F32)
    zh = jnp.zeros((S, half), F32)
    tab_c = jnp.concatenate([one, cos, cos, pad1], axis=1)
    tab_ck = jnp.concatenate([zero, cos, cos, pad0], axis=1)
    tab_s1 = jnp.concatenate([zero, -sin, zh, pad0], axis=1)
    tab_s2 = jnp.concatenate([zero, zh, sin, pad0], axis=1)
    return tab_c, tab_ck, tab_s1, tab_s2


def _rope(x, c, s1, s2):
    return x * c + pltpu.roll(x, LANES - 16, 1) * s1 + pltpu.roll(x, 16, 1) * s2


def _rope_t(dy, c, s1, s2):
    return dy * c + pltpu.roll(dy * s1, 16, 1) + pltpu.roll(dy * s2, LANES - 16, 1)


_MLA_SCALE = 1.0 / math.sqrt(MLA_NOPE + MLA_ROPE)


def _mla_prep(za, gq, gkv, wq, wkv, tabs, name):
    S = za.shape[0]
    tm = _tile(S, 512)
    H = MLA_HEADS

    def body(zq_ref, tail_ref, gq_ref, gkv_ref, wq_ref, wkv_ref, c_ref, ck_ref, s1_ref, s2_ref,
             qf_ref, kf_ref, v_ref, kft_ref, vt_ref):
        zq = zq_ref[...]
        c, s1, s2 = c_ref[...], s1_ref[...], s2_ref[...]
        qn = _rms(zq[:, 0:256], gq_ref[...]).astype(BF16)
        q = _dot(qn, wq_ref[...])
        for h in range(H):
            blk = _rope(q[:, h * LANES:(h + 1) * LANES], c, s1, s2)
            qf_ref[:, h * LANES:(h + 1) * LANES] = (blk * _MLA_SCALE).astype(BF16)
        kvn = _rms(zq[:, 256:384], gkv_ref[...]).astype(BF16)
        kv = _dot(kvn, wkv_ref[...])
        kt = _rope(tail_ref[...], ck_ref[...], s1, s2)
        for h in range(H):
            sl = slice(h * LANES, (h + 1) * LANES)
            kblk = kv[:, sl] + kt
            kf_ref[:, sl] = kblk.astype(BF16)
            kft_ref[sl, :] = kblk.T.astype(BF16)
        v_ref[...] = kv[:, H * LANES:].astype(BF16)
        for cblk in range(H * MLA_V // LANES):
            sl = slice(cblk * LANES, (cblk + 1) * LANES)
            vt_ref[sl, :] = kv[:, H * LANES + cblk * LANES:H * LANES + (cblk + 1) * LANES].T.astype(BF16)

    row = lambda i: (i, 0)
    col = lambda i: (0, i)
    fix = lambda i: (0, 0)
    return pl.pallas_call(
        body, name=name, grid=(S // tm,),
        in_specs=[_bs((tm, 384), row), _bs((tm, 128), lambda i: (i, TAIL0 // 128)), _bs((1, 256), fix), _bs((1, 128), fix),
                  _bs((256, 768), fix), _bs((128, 1152), fix),
                  _bs((tm, 128), row), _bs((tm, 128), row), _bs((tm, 128), row), _bs((tm, 128), row)],
        out_specs=[_bs((tm, 768), row), _bs((tm, 768), row), _bs((tm, 384), row), _bs((768, tm), col), _bs((384, tm), col)],
        out_shape=[jax.ShapeDtypeStruct((S, 768), BF16), jax.ShapeDtypeStruct((S, 768), BF16),
                   jax.ShapeDtypeStruct((S, 384), BF16), jax.ShapeDtypeStruct((768, S), BF16),
                   jax.ShapeDtypeStruct((384, S), BF16)],
        compiler_params=_params(("parallel",)),
    )(za, za, gq, gkv, wq, wkv, *tabs)


def _mla_prep_bwd(za, gq, gkv, wq, wkv, tabs, dqf, dkf, dvm, name):
    S = za.shape[0]
    tm = _tile(S, 512)
    H = MLA_HEADS

    def body(zq_ref, gq_ref, gkv_ref, wq_ref, wkv_ref, c_ref, ck_ref, s1_ref, s2_ref, dqf_ref, dkf_ref, dvm_ref,
             dza_ref, dwq_ref, dwkv_ref, dgq_ref, dgkv_ref):
        i = pl.program_id(0)
        first = i == 0
        zq = zq_ref[...]
        c, s1, s2 = c_ref[...], s1_ref[...], s2_ref[...]
        lane = lax.broadcasted_iota(jnp.int32, (1, LANES), 1)
        nope = lane < MLA_NOPE
        rope = jnp.logical_and(lane >= ROPE_LANE0, lane < ROPE_LANE0 + MLA_ROPE)

        qa = zq[:, 0:256]
        qn = _rms(qa, gq_ref[...]).astype(BF16)
        dqf = dqf_ref[...]
        dq_pre = jnp.concatenate(
            [_rope_t(dqf[:, h * LANES:(h + 1) * LANES] * _MLA_SCALE, c, s1, s2) for h in range(H)], axis=1).astype(BF16)
        _accum_out(dwq_ref, first, _dot_tn(qn, dq_pre))
        dqa, dgq = _rms_bwd(_dot_nt(dq_pre, wq_ref[...]), qa, gq_ref[...])
        _accum_out(dgq_ref, first, dgq)

        kva = zq[:, 256:384]
        kvn = _rms(kva, gkv_ref[...]).astype(BF16)
        dkf = dkf_ref[...]
        parts = []
        dkt = jnp.zeros((tm, LANES), F32)
        for h in range(H):
            blk = dkf[:, h * LANES:(h + 1) * LANES]
            parts.append(jnp.where(nope, blk, 0.0))
            dkt = dkt + jnp.where(rope, blk, 0.0)
        dkv_pre = jnp.concatenate(parts + [dvm_ref[...]], axis=1).astype(BF16)
        _accum_out(dwkv_ref, first, _dot_tn(kvn, dkv_pre))
        dkva, dgkv = _rms_bwd(_dot_nt(dkv_pre, wkv_ref[...]), kva, gkv_ref[...])
        _accum_out(dgkv_ref, first, dgkv)

        dtail = _rope_t(dkt, ck_ref[...], s1, s2)
        dza_ref[...] = jnp.concatenate([dqa, dkva, jnp.zeros((tm, 256), F32), dtail], axis=1)

    row = lambda i: (i, 0)
    fix = lambda i: (0, 0)
    return pl.pallas_call(
        body, name=name, grid=(S // tm,),
        in_specs=[_bs((tm, 384), row), _bs((1, 256), fix), _bs((1, 128), fix), _bs((256, 768), fix), _bs((128, 1152), fix),
                  _bs((tm, 128), row), _bs((tm, 128), row), _bs((tm, 128), row), _bs((tm, 128), row),
                  _bs((tm, 768), row), _bs((tm, 768), row), _bs((tm, 384), row)],
        out_specs=[_bs((tm, ZA), row), _bs((256, 768), fix), _bs((128, 1152), fix), _bs((1, 256), fix), _bs((1, 128), fix)],
        out_shape=[jax.ShapeDtypeStruct((S, ZA), F32), jax.ShapeDtypeStruct((256, 768), F32),
                   jax.ShapeDtypeStruct((128, 1152), F32), jax.ShapeDtypeStruct((1, 256), F32),
                   jax.ShapeDtypeStruct((1, 128), F32)],
        compiler_params=_params(("arbitrary",)),
    )(za, gq, gkv, wq, wkv, *tabs, dqf, dkf, dvm)


def _head_views(qb, kb, r, dkb, sel):
    if dkb == LANES:
        sl = slice(r * LANES, (r + 1) * LANES)
        return qb[:, sl], kb[:, sl], kb[:, sl]
    return jnp.where(sel, qb, jnp.zeros_like(qb)), kb, jnp.where(sel, kb, jnp.zeros_like(kb))


def _attn_fwd_t(q_arr, k_arr, vt_arr, *, nhp, dkb, qoff, koff, vtoff, scale, cum, cumT, name, rider=None):
    S = q_arr.shape[0]
    T = _tile(S, ATTN_BLOCK)
    nq = S // T
    W = 2 * dkb
    bias = cum is not None

    def body(*refs):
        if bias:
            q_ref, k_ref, vt_ref, cq_ref, ck_ref, o_ref, lse_ref, m_s, l_s, acc_s = refs
        else:
            q_ref, k_ref, vt_ref, o_ref, lse_ref, m_s, l_s, acc_s = refs
        hp, qi, ki = pl.program_id(0), pl.program_id(1), pl.program_id(2)
        lo_lane = lax.broadcasted_iota(jnp.int32, (1, LANES), 1) < 64
        lo_row = lax.broadcasted_iota(jnp.int32, (LANES, 1), 0) < 64

        @pl.when(ki == 0)
        def _():
            m_s[...] = jnp.full_like(m_s, NEG)
            l_s[...] = jnp.zeros_like(l_s)
            acc_s[...] = jnp.zeros_like(acc_s)

        def step(masked):
            qb, kb, vtb = q_ref[...], k_ref[...], vt_ref[...]
            if masked:
                mask = lax.broadcasted_iota(jnp.int32, (T, T), 0) <= lax.broadcasted_iota(jnp.int32, (T, T), 1)
            if bias:
                li = lax.broadcasted_iota(jnp.int32, (T, LANES), 1)
                ckb = ck_ref[...]
            m_all, l_all = m_s[...], l_s[...]
            scores = []
            for r in range(2):
                sel = lo_lane if r == 0 else jnp.logical_not(lo_lane)
                q, k, _ = _head_views(qb, kb, r, dkb, sel)
                if scale != 1.0:
                    q = q * jnp.asarray(scale, q.dtype)
                scores.append(_dot_nt(k, q))
            m_out, l_out, alphas, pvs = [], [], [], []
            for r in range(2):
                rsel = lo_row if r == 0 else jnp.logical_not(lo_row)
                s = scores[r]
                if bias:
                    ck = jnp.sum(jnp.where(li == 8 * hp + r, ckb, 0.0), axis=1, keepdims=True)
                    s = s + (cq_ref[r:r + 1, :] - ck)
                if masked:
                    s = jnp.where(mask, s, NEG)
                m_prev = m_all[r:r + 1, :]
                m_new = jnp.maximum(m_prev, jnp.max(s, axis=0, keepdims=True))
                alpha = jnp.exp(m_prev - m_new)
                p = jnp.exp(s - m_new)
                m_out.append(m_new)
                l_out.append(alpha * l_all[r:r + 1, :] + jnp.sum(p, axis=0, keepdims=True))
                alphas.append(alpha)
                pvs.append(_dot(jnp.where(rsel, vtb, jnp.zeros_like(vtb)), p.astype(BF16)))
            m_s[0:1, :] = m_out[0]
            m_s[1:2, :] = m_out[1]
            l_s[0:1, :] = l_out[0]
            l_s[1:2, :] = l_out[1]
            acc_s[...] = acc_s[...] * jnp.where(lo_row, alphas[0], alphas[1]) + (pvs[0] + pvs[1])

        @pl.when(ki < qi)
        def _():
            step(False)

        @pl.when(ki == qi)
        def _():
            step(True)

        @pl.when(ki == nq - 1)
        def _():
            inv = jnp.where(lo_row, 1.0 / l_s[0:1, :], 1.0 / l_s[1:2, :])
            o_ref[...] = (acc_s[...] * inv).T.astype(BF16)
            used = lax.broadcasted_iota(jnp.int32, (8, T), 0) < 2
            lse_ref[...] = jnp.where(used, m_s[...] + jnp.log(jnp.where(used, l_s[...], 1.0)), 0.0)

    kmap = lambda hp, qi, ki: jnp.minimum(ki, qi)
    in_specs = [_bs((T, W), lambda hp, qi, ki: (qi, qoff + hp)),
                _bs((T, W), lambda hp, qi, ki: (kmap(hp, qi, ki), koff + hp)),
                _bs((LANES, T), lambda hp, qi, ki: (vtoff + hp, kmap(hp, qi, ki)))]
    args = [q_arr, k_arr, vt_arr]
    if bias:
        in_specs += [_bs((8, T), lambda hp, qi, ki: (hp, qi)), _bs((T, LANES), lambda hp, qi, ki: (kmap(hp, qi, ki), 0))]
        args += [cumT, cum]
    return _call(
        body, name=name, grid=(nhp, nq, nq),
        in_specs=in_specs,
        out_specs=[_bs((T, LANES), lambda hp, qi, ki: (qi, hp)), _bs((None, 8, T), lambda hp, qi, ki: (hp, 0, qi))],
        out_shape=[jax.ShapeDtypeStruct((S, nhp * LANES), BF16), jax.ShapeDtypeStruct((nhp, 8, S), F32)],
        scratch=[pltpu.VMEM((8, T), F32), pltpu.VMEM((8, T), F32), pltpu.VMEM((LANES, T), F32)],
        dims=("parallel", "parallel", "arbitrary"), args=args, rider=rider)


def _attn_bwd_t(q_arr, k_arr, kt_arr, v_arr, do_arr, lse, dvec, *, nhp, dkb, qoff, koff, ktoff, voff, scale, cum, cumT,
                name, rider=None, grad_dtype=F32):
    S = q_arr.shape[0]
    T = _tile(S, ATTN_BLOCK)
    nq = S // T
    W = 2 * dkb
    bias = cum is not None

    def body(*refs):
        if bias:
            (q_ref, k_ref, kt_ref, v_ref, do_ref, lse_ref, dvec_ref, cq_ref, ck_ref,
             dq_ref, dk_ref, dv_ref, dcq_ref, dck_ref, dqt_s, dk_s, dv_s, dcq_s, dck_s) = refs
        else:
            (q_ref, k_ref, kt_ref, v_ref, do_ref, lse_ref, dvec_ref,
             dq_ref, dk_ref, dv_ref, dqt_s, dk_s, dv_s) = refs
        hp, ki, qi = pl.program_id(0), pl.program_id(1), pl.program_id(2)
        lo_lane = lax.broadcasted_iota(jnp.int32, (1, LANES), 1) < 64
        lo_row = lax.broadcasted_iota(jnp.int32, (LANES, 1), 0) < 64

        @pl.when(jnp.logical_and(ki == 0, qi == 0))
        def _():
            dqt_s[...] = jnp.zeros_like(dqt_s)
            if bias:
                dcq_s[...] = jnp.zeros_like(dcq_s)

        @pl.when(qi == 0)
        def _():
            dk_s[...] = jnp.zeros_like(dk_s)
            dv_s[...] = jnp.zeros_like(dv_s)
            if bias:
                dck_s[...] = jnp.zeros_like(dck_s)

        def step(masked):
            qb, kb, ktb, vb, dob = q_ref[...], k_ref[...], kt_ref[...], v_ref[...], do_ref[...]
            if masked:
                mask = lax.broadcasted_iota(jnp.int32, (T, T), 0) <= lax.broadcasted_iota(jnp.int32, (T, T), 1)
            if bias:
                li = lax.broadcasted_iota(jnp.int32, (T, LANES), 1)
                ckb = ck_ref[...]
            for r in range(2):
                sel = lo_lane if r == 0 else jnp.logical_not(lo_lane)
                rsel = lo_row if r == 0 else jnp.logical_not(lo_row)
                q, k, _ = _head_views(qb, kb, r, dkb, sel)
                if scale != 1.0:
                    q = q * jnp.asarray(scale, q.dtype)
                s = _dot_nt(k, q)
                if bias:
                    ck = jnp.sum(jnp.where(li == 8 * hp + r, ckb, 0.0), axis=1, keepdims=True)
                    s = s + (cq_ref[r:r + 1, :] - ck)
                p = jnp.exp(s - lse_ref[r:r + 1, :])
                if masked:
                    p = jnp.where(mask, p, 0.0)
                do_r = jnp.where(sel, dob, jnp.zeros_like(dob))
                dp = _dot_nt(vb, do_r)
                ds = p * (dp - dvec_ref[r:r + 1, :])
                pb = p.astype(BF16)
                dsb = ds.astype(BF16)
                dv_s[...] += _dot(pb, do_r)
                if dkb == LANES:
                    sl = slice(r * LANES, (r + 1) * LANES)
                    dk_s[:, sl] += _dot(dsb, q)
                    dqt_s[qi, sl, :] += _dot(ktb[sl, :], dsb) * scale
                else:
                    dk_s[...] += _dot(dsb, q)
                    dqt_s[qi] += _dot(jnp.where(rsel, ktb, jnp.zeros_like(ktb)), dsb) * scale
                if bias:
                    dcq_s[qi, r:r + 1, :] += jnp.sum(ds, axis=0, keepdims=True)
                    dck_s[...] -= jnp.where(li == 8 * hp + r, jnp.sum(ds, axis=1, keepdims=True), 0.0)

        @pl.when(qi > ki)
        def _():
            step(False)

        @pl.when(qi == ki)
        def _():
            step(True)

        @pl.when(qi == nq - 1)
        def _():
            dk_ref[...] = dk_s[...].astype(grad_dtype)
            dv_ref[...] = dv_s[...].astype(grad_dtype)
            if bias:
                dck_ref[...] = dck_s[...]

        @pl.when(jnp.logical_and(ki == nq - 1, qi == nq - 1))
        def _():
            for c in range(nq):
                dq_ref[c * T:(c + 1) * T, :] = dqt_s[c].T.astype(grad_dtype)
                if bias:
                    dcq_ref[:, c * T:(c + 1) * T] = dcq_s[c]

    qmap = lambda hp, ki, qi: jnp.maximum(qi, ki)
    in_specs = [_bs((T, W), lambda hp, ki, qi: (qmap(hp, ki, qi), qoff + hp)),
                _bs((T, W), lambda hp, ki, qi: (ki, koff + hp)),
                _bs((W, T), lambda hp, ki, qi: (ktoff + hp, ki)),
                _bs((T, LANES), lambda hp, ki, qi: (ki, voff + hp)),
                _bs((T, LANES), lambda hp, ki, qi: (qmap(hp, ki, qi), hp)),
                _bs((None, 8, T), lambda hp, ki, qi: (hp, 0, qmap(hp, ki, qi))),
                _bs((None, 8, T), lambda hp, ki, qi: (hp, 0, qmap(hp, ki, qi)))]
    args = [q_arr, k_arr, kt_arr, v_arr, do_arr, lse, dvec]
    out_specs = [_bs((S, W), lambda hp, ki, qi: (0, hp)), _bs((T, W), lambda hp, ki, qi: (ki, hp)),
                 _bs((T, LANES), lambda hp, ki, qi: (ki, hp))]
    out_shape = [jax.ShapeDtypeStruct((S, nhp * W), grad_dtype), jax.ShapeDtypeStruct((S, nhp * W), grad_dtype),
                 jax.ShapeDtypeStruct((S, nhp * LANES), grad_dtype)]
    scratch = [pltpu.VMEM((nq, W, T), F32), pltpu.VMEM((T, W), F32), pltpu.VMEM((T, LANES), F32)]
    if bias:
        in_specs += [_bs((8, T), lambda hp, ki, qi: (hp, qmap(hp, ki, qi))), _bs((T, LANES), lambda hp, ki, qi: (ki, 0))]
        args += [cumT, cum]
        out_specs += [_bs((None, 8, S), lambda hp, ki, qi: (hp, 0, 0)), _bs((None, T, LANES), lambda hp, ki, qi: (hp, ki, 0))]
        out_shape += [jax.ShapeDtypeStruct((nhp, 8, S), F32), jax.ShapeDtypeStruct((nhp, S, LANES), F32)]
        scratch += [pltpu.VMEM((nq, 8, T), F32), pltpu.VMEM((T, LANES), F32)]
    return _call(body, name=name, grid=(nhp, nq, nq), in_specs=in_specs, out_specs=out_specs, out_shape=out_shape,
                 scratch=scratch, dims=("arbitrary", "arbitrary", "arbitrary"), args=args, rider=rider)


def _gate_lanes(shape):
    lane = lax.broadcasted_iota(jnp.int32, shape, 1)
    return jnp.logical_and(lane < 8 * (FOX_HEADS // 2), lane % 8 < 2)


def _fox_prep(za, b_row, name):
    S = za.shape[0]
    nrow = 8 * (FOX_HEADS // 2)

    def body(tail_ref, b_ref, cum_ref, cumt_ref):
        x = tail_ref[...] + b_ref[...]
        logf = jnp.minimum(x, 0.0) - jnp.log(1.0 + jnp.exp(-jnp.abs(x)))
        y = jnp.where(_gate_lanes((S, LANES)), logf, 0.0)
        row = lax.broadcasted_iota(jnp.int32, (S, LANES), 0)
        k = 1
        while k < S:
            y = y + jnp.where(row >= k, pltpu.roll(y, k, 0), 0.0)
            k *= 2
        cum_ref[...] = y
        cumt_ref[...] = y.T[0:nrow, :]

    return pl.pallas_call(
        body, name=name, grid=(1,),
        in_specs=[_bs((S, LANES), lambda i: (0, TAIL0 // LANES)), _bs((1, LANES), lambda i: (0, 0))],
        out_specs=[_bs((S, LANES), lambda i: (0, 0)), _bs((nrow, S), lambda i: (0, 0))],
        out_shape=[jax.ShapeDtypeStruct((S, LANES), F32), jax.ShapeDtypeStruct((nrow, S), F32)],
        compiler_params=_params(("arbitrary",)),
    )(za, b_row)


def _fox_prep_bwd(za, b_row, dcq, dck, name):
    S = za.shape[0]
    nhp = FOX_HEADS // 2
    nrow = 8 * nhp
    dcq2 = dcq.reshape(nrow, S)

    def body(tail_ref, b_ref, dcq_ref, dck_ref, dt_ref, db_ref):
        x = tail_ref[...] + b_ref[...]
        d = jnp.concatenate([dcq_ref[...], jnp.zeros((LANES - nrow, S), F32)], axis=0).T
        for hp in range(nhp):
            d = d + dck_ref[hp]
        row = lax.broadcasted_iota(jnp.int32, (S, LANES), 0)
        k = 1
        while k < S:
            d = d + jnp.where(row < S - k, pltpu.roll(d, S - k, 0), 0.0)
            k *= 2
        df = jnp.where(_gate_lanes((S, LANES)), d * jax.nn.sigmoid(-x), 0.0)
        dt_ref[...] = df
        db_ref[...] = jnp.sum(df, axis=0, keepdims=True)

    return pl.pallas_call(
        body, name=name, grid=(1,),
        in_specs=[_bs((S, LANES), lambda i: (0, TAIL0 // LANES)), _bs((1, LANES), lambda i: (0, 0)),
                  _bs((nrow, S), lambda i: (0, 0)), _bs((nhp, S, LANES), lambda i: (0, 0, 0))],
        out_specs=[_bs((S, LANES), lambda i: (0, 0)), _bs((1, LANES), lambda i: (0, 0))],
        out_shape=[jax.ShapeDtypeStruct((S, LANES), F32), jax.ShapeDtypeStruct((1, LANES), F32)],
        compiler_params=_params(("arbitrary",)),
    )(za, b_row, dcq2, dck)


def _pool_select(half, lane_lo, vals):
    return jnp.where(lane_lo, jnp.where(half == 0, vals[0], vals[2]), jnp.where(half == 0, vals[1], vals[3]))


def _pool_den(S, half, lane_lo):
    cnt = (lax.broadcasted_iota(jnp.int32, (S, LANES), 0) + 1).astype(F32)
    w = _pool_select(half, lane_lo, [float(x) for x in POOL_WINDOWS])
    return jnp.minimum(cnt, w)


def _pool_fwd(za, wbd, scale, name):
    S = za.shape[0]

    def body(u_ref, w_ref, sc_ref, y_ref, pd_ref):
        half = pl.program_id(0)
        u = u_ref[...]
        row = lax.broadcasted_iota(jnp.int32, (S, LANES), 0)
        lane_lo = lax.broadcasted_iota(jnp.int32, (S, LANES), 1) < POOL_GROUP
        sums = []
        acc = u
        k = 1
        while k < POOL_WINDOWS[-1]:
            acc = acc + jnp.where(row >= k, pltpu.roll(acc, k, 0), 0.0)
            sums.append(acc)
            k *= 2
        pooled = _pool_select(half, lane_lo, sums) / _pool_den(S, half, lane_lo)
        pd = (pooled - u).astype(BF16)
        pd_ref[...] = pd
        y_ref[...] = (_dot(pd, w_ref[...]) * sc_ref[...]).astype(BF16)

    return pl.pallas_call(
        body, name=name, grid=(2,),
        in_specs=[_bs((S, LANES), lambda i: (0, 384 // LANES + i)), _bs((None, LANES, LANES), lambda i: (i, 0, 0)),
                  _bs((1, LANES), lambda i: (0, i))],
        out_specs=[_bs((S, LANES), lambda i: (0, i)), _bs((S, LANES), lambda i: (0, i))],
        out_shape=[jax.ShapeDtypeStruct((S, POOL_WIDTH), BF16), jax.ShapeDtypeStruct((S, POOL_WIDTH), BF16)],
        compiler_params=_params(("parallel",)),
    )(za, wbd, scale)


def _pool_bwd(dyb, pd, wbd, scale, name):
    S = pd.shape[0]

    def body(dy_ref, pd_ref, w_ref, sc_ref, du_ref, dw_ref, dsc_ref):
        half = pl.program_id(0)
        dy = dy_ref[...]
        pd = pd_ref[...]
        w = w_ref[...]
        ypre = _dot(pd, w)
        dsc_ref[...] = jnp.sum(dy * ypre, axis=0, keepdims=True)
        dyp = (dy * sc_ref[...]).astype(BF16)
        dw_ref[...] = _dot_tn(pd, dyp)
        dpd = _dot_nt(dyp, w)
        row = lax.broadcasted_iota(jnp.int32, (S, LANES), 0)
        lane_lo = lax.broadcasted_iota(jnp.int32, (S, LANES), 1) < POOL_GROUP
        acc = dpd / _pool_den(S, half, lane_lo)
        sums = []
        k = 1
        while k < POOL_WINDOWS[-1]:
            acc = acc + jnp.where(row < S - k, pltpu.roll(acc, S - k, 0), 0.0)
            sums.append(acc)
            k *= 2
        du_ref[...] = _pool_select(half, lane_lo, sums) - dpd

    return pl.pallas_call(
        body, name=name, grid=(2,),
        in_specs=[_bs((S, LANES), lambda i: (0, i)), _bs((S, LANES), lambda i: (0, i)),
                  _bs((None, LANES, LANES), lambda i: (i, 0, 0)), _bs((1, LANES), lambda i: (0, i))],
        out_specs=[_bs((S, LANES), lambda i: (0, i)), _bs((None, LANES, LANES), lambda i: (i, 0, 0)),
                   _bs((1, LANES), lambda i: (0, i))],
        out_shape=[jax.ShapeDtypeStruct((S, POOL_WIDTH), F32), jax.ShapeDtypeStruct((2, LANES, LANES), F32),
                   jax.ShapeDtypeStruct((1, POOL_WIDTH), F32)],
        compiler_params=_params(("parallel",)),
    )(dyb, pd, wbd, scale)


def _mix_out_fwd(x, ya, yb, yc, w_out, name):
    S, D = x.shape
    tm = _tile(S, 512)
    K = w_out.shape[0]

    def body(x_ref, ya_ref, yb_ref, yc_ref, w_ref, xo_ref, yc_out):
        ycat = jnp.concatenate([ya_ref[...], yb_ref[...], yc_ref[...]], axis=1)
        yc_out[...] = ycat
        xo_ref[...] = x_ref[...] + _dot(ycat, w_ref[...])

    row = lambda i: (i, 0)
    return pl.pallas_call(
        body, name=name, grid=(S // tm,),
        in_specs=[_bs((tm, D), row), _bs((tm, 384), row), _bs((tm, 256), row), _bs((tm, 384), row),
                  _bs((K, D), lambda i: (0, 0))],
        out_specs=[_bs((tm, D), row), _bs((tm, K), row)],
        out_shape=[jax.ShapeDtypeStruct((S, D), F32), jax.ShapeDtypeStruct((S, K), BF16)],
        compiler_params=_params(("parallel",)),
    )(x, ya, yb, yc, w_out)


def _mix_out_bwd(dy, w_out, ya, yc, name):
    S, D = dy.shape
    tm = _tile(S, 512)
    K = w_out.shape[0]
    nhp = ya.shape[1] // LANES

    def body(dy_ref, w_ref, ya_ref, yc_ref, da_ref, db_ref, dc_ref, dva_ref, dvc_ref):
        d = _dot_nt(dy_ref[...].astype(BF16), w_ref[...])
        da = d[:, 0:384].astype(BF16)
        dc = d[:, 640:1024].astype(BF16)
        da_ref[...] = da
        db_ref[...] = d[:, 384:640]
        dc_ref[...] = dc
        li = lax.broadcasted_iota(jnp.int32, (tm, LANES), 1)
        for do, o_ref, out_ref in ((da, ya_ref, dva_ref), (dc, yc_ref, dvc_ref)):
            for hp in range(nhp):
                sl = slice(hp * LANES, (hp + 1) * LANES)
                prod = do[:, sl].astype(F32) * o_ref[:, sl].astype(F32)
                d0 = jnp.sum(jnp.where(li < 64, prod, 0.0), axis=1, keepdims=True)
                d1 = jnp.sum(jnp.where(li >= 64, prod, 0.0), axis=1, keepdims=True)
                out_ref[hp] = jnp.where(li == 0, d0, jnp.where(li == 1, d1, 0.0)).T[0:8, :]

    row = lambda i: (i, 0)
    dv_spec = _bs((nhp, 8, tm), lambda i: (0, 0, i))
    dv_shape = jax.ShapeDtypeStruct((nhp, 8, S), F32)
    return pl.pallas_call(
        body, name=name, grid=(S // tm,),
        in_specs=[_bs((tm, D), row), _bs((K, D), lambda i: (0, 0)), _bs((tm, 384), row), _bs((tm, 384), row)],
        out_specs=[_bs((tm, 384), row), _bs((tm, 256), row), _bs((tm, 384), row), dv_spec, dv_spec],
        out_shape=[jax.ShapeDtypeStruct((S, 384), BF16), jax.ShapeDtypeStruct((S, 256), F32),
                   jax.ShapeDtypeStruct((S, 384), BF16), dv_shape, dv_shape],
        compiler_params=_params(("parallel",)),
    )(dy, w_out, ya, yc)


def _loss_head(x, gam, target, name):
    S, D = x.shape
    tm = _tile(S, 512)

    def body(x_ref, gam_ref, t_ref, dx_ref, dgam_ref, loss_ref):
        i = pl.program_id(0)
        xv = x_ref[...]
        err = _rms(xv, gam_ref[...]) - t_ref[...]
        part = 0.5 * jnp.sum(jnp.mean(err * err, axis=-1, keepdims=True), axis=0, keepdims=True)
        dxn, dgam = _rms_bwd(err * (1.0 / D), xv, gam_ref[...])
        dx_ref[...] = dxn
        _accum_out(dgam_ref, i == 0, dgam)
        _accum_out(loss_ref, i == 0, jnp.broadcast_to(part, (1, LANES)))

    row = lambda i: (i, 0)
    fix = lambda i: (0, 0)
    return pl.pallas_call(
        body, name=name, grid=(S // tm,),
        in_specs=[_bs((tm, D), row), _bs((1, D), fix), _bs((tm, D), row)],
        out_specs=[_bs((tm, D), row), _bs((1, D), fix), _bs((1, LANES), fix)],
        out_shape=[jax.ShapeDtypeStruct((S, D), F32), jax.ShapeDtypeStruct((1, D), F32),
                   jax.ShapeDtypeStruct((1, LANES), F32)],
        compiler_params=_params(("arbitrary",)),
    )(x, gam, target)


def _adam_math(g, w, m, v):
    m = ADAM_B1 * m + (1.0 - ADAM_B1) * g
    v = ADAM_B2 * v + (1.0 - ADAM_B2) * (g * g)
    m_hat = m / (1.0 - ADAM_B1 ** ADAM_STEP)
    v_hat = v / (1.0 - ADAM_B2 ** ADAM_STEP)
    delta = -ADAM_LR * (m_hat / (jnp.sqrt(v_hat) + ADAM_EPS) + ADAM_WD * w)
    return delta, m, v


def _adam_sum(recv, w, m, v, layer, prev, tr, name):
    L, R, C = w.shape
    Cp = recv.shape[2]
    tr = _tile(R, tr)

    def body(r_ref, w_ref, m_ref, v_ref, *rest):
        g_out, d_out, m_out, v_out = rest[len(rest) - 4:]
        g = r_ref[0, :, 0:C].astype(F32)
        for p in range(1, N_DEV):
            g = g + r_ref[p, :, 0:C].astype(F32)
        delta, mn, vn = _adam_math(g, w_ref[...], m_ref[...], v_ref[...])
        g_out[...] = g
        d_out[...] = delta
        m_out[...] = mn
        v_out[...] = vn

    blk = _bs((None, tr, C), lambda i: (layer, i, 0))
    shp = jax.ShapeDtypeStruct((L, R, C), F32)
    in_specs = [_bs((N_DEV, tr, Cp), lambda i: (0, i, 0)), blk, blk, blk]
    args = [recv, w, m, v]
    aliases = {}
    if prev is not None:
        in_specs += [HBM_SPEC] * 4
        args += list(prev)
        aliases = {4 + k: k for k in range(4)}
    return pl.pallas_call(
        body, name=name, grid=(R // tr,),
        in_specs=in_specs, out_specs=[blk, blk, blk, blk], out_shape=[shp, shp, shp, shp],
        input_output_aliases=aliases, compiler_params=_params(("parallel",)),
    )(*args)


def _dev_index(px, py, pc):
    return 4 * px + 2 * py + pc


class _GatherRider:
    def __init__(self, shards, out_shapes, views):
        self.n = len(shards)
        self.views = views
        self.srcs = list(shards)
        self.out_shapes = list(out_shapes)
        n = self.n
        self.scratch = [pltpu.SemaphoreType.DMA((n, 7)), pltpu.SemaphoreType.DMA((n, 7)), pltpu.SemaphoreType.DMA((n,))]

    def _copies(self, ins, outs, sems):
        n = self.n
        send_sems, recv_sems, local_sems = sems
        x, y, c = lax.axis_index("x"), lax.axis_index("y"), lax.axis_index("c")
        me, sibling = (x, y, c), (x, y, 1 - c)
        chips = [(1 - x, y), (x, 1 - y), (1 - x, 1 - y)]

        def rows(a, blk):
            return self.views[a](outs[a], _dev_index(*blk))

        def copy(a, k, blk, to, src=None):
            return pltpu.make_async_remote_copy(
                src_ref=rows(a, blk) if src is None else src, dst_ref=rows(a, blk),
                send_sem=send_sems.at[a, k], recv_sem=recv_sems.at[a, k], device_id=to, device_id_type=MESH_ID)

        local = [pltpu.make_async_copy(ins[a], rows(a, me), local_sems.at[a]) for a in range(n)]
        first = []
        for a in range(n):
            first.append(copy(a, 0, me, sibling, src=ins[a]))
            first += [copy(a, 1 + j, me, (*chip, c), src=ins[a]) for j, chip in enumerate(chips)]
        over_ici = [[copy(a, 1 + j, (*chip, c), me) for a in range(n)] for j, chip in enumerate(chips)]
        passed = [[copy(a, 4 + j, (*chip, c), sibling) for a in range(n)] for j, chip in enumerate(chips)]
        from_sibling = [copy(a, 0, sibling, me) for a in range(n)]
        from_sibling += [copy(a, 4 + j, (*chip, 1 - c), me) for a in range(n) for j, chip in enumerate(chips)]
        return local, first, over_ici, passed, from_sibling

    def begin(self, ins, outs, sems):
        local, first, _, _, _ = self._copies(ins, outs, sems)
        for cp in local + first:
            cp.start()

    def middle(self, ins, outs, sems):
        _, _, over_ici, passed, _ = self._copies(ins, outs, sems)
        for arrived, onward in zip(over_ici, passed):
            for cp, fwd in zip(arrived, onward):
                cp.wait_recv()
                fwd.start()

    def end(self, ins, outs, sems):
        local, first, _, passed, from_sibling = self._copies(ins, outs, sems)
        for cp in from_sibling:
            cp.wait_recv()
        for cp in first + [fwd for onward in passed for fwd in onward]:
            cp.wait_send()
        for cp in local:
            cp.wait()


class _ScatterRider:
    _MASKS = [(kx, ky, kc) for kx in (0, 1) for ky in (0, 1) for kc in (0, 1)][1:]

    def __init__(self, srcs, out_shapes, src_of, dst_at):
        self.n = len(srcs)
        self.srcs = list(srcs)
        self.out_shapes = list(out_shapes)
        self.src_of = src_of
        self.dst_at = dst_at
        n = self.n
        self.scratch = [pltpu.SemaphoreType.DMA((n, 7)), pltpu.SemaphoreType.DMA((n, 7)), pltpu.SemaphoreType.DMA((n,))]

    def _copies(self, ins, outs, sems):
        send_sems, recv_sems, local_sems = sems
        x, y, c = lax.axis_index("x"), lax.axis_index("y"), lax.axis_index("c")
        my = _dev_index(x, y, c)
        peers = [(1 - x if kx else x, 1 - y if ky else y, 1 - c if kc else c) for kx, ky, kc in self._MASKS]

        def send(i, k, to):
            return pltpu.make_async_remote_copy(
                src_ref=self.src_of[i](ins[i], _dev_index(*to)), dst_ref=self.dst_at[i](outs[i], my),
                send_sem=send_sems.at[i, k], recv_sem=recv_sems.at[i, k], device_id=to, device_id_type=MESH_ID)

        def arrival(i, k, frm):
            slot = self.dst_at[i](outs[i], _dev_index(*frm))
            return pltpu.make_async_remote_copy(
                src_ref=slot, dst_ref=slot, send_sem=send_sems.at[i, k], recv_sem=recv_sems.at[i, k],
                device_id=frm, device_id_type=MESH_ID)

        local = [pltpu.make_async_copy(self.src_of[i](ins[i], my), self.dst_at[i](outs[i], my), local_sems.at[i])
                 for i in range(self.n)]
        sends = [send(i, k, to) for k, to in enumerate(peers) for i in range(self.n)]
        arrivals = [arrival(i, k, frm) for k, frm in enumerate(peers) for i in range(self.n)]
        return local, sends, arrivals

    def begin(self, ins, outs, sems):
        local, sends, _ = self._copies(ins, outs, sems)
        for cp in local + sends:
            cp.start()

    def middle(self, ins, outs, sems):
        pass

    def end(self, ins, outs, sems):
        local, sends, arrivals = self._copies(ins, outs, sems)
        for cp in arrivals:
            cp.wait_recv()
        for cp in sends:
            cp.wait_send()
        for cp in local:
            cp.wait()


def _comm_call(rider, name):
    k_in = len(rider.srcs)
    k_out = len(rider.out_shapes)

    def body(*refs):
        ins, outs, sems = refs[:k_in], refs[k_in:k_in + k_out], refs[k_in + k_out:]
        rider.begin(ins, outs, sems)
        rider.middle(ins, outs, sems)
        rider.end(ins, outs, sems)

    return pl.pallas_call(
        body, name=name, in_specs=[HBM_SPEC] * k_in, out_specs=[HBM_SPEC] * k_out, out_shape=rider.out_shapes,
        scratch_shapes=rider.scratch, compiler_params=pltpu.CompilerParams(has_side_effects=True),
    )(*rider.srcs)


def _pad_w_in(w):
    cols = lambda a, b: w[..., a:b]
    zeros = lambda n: jnp.zeros(w.shape[:-1] + (n,), w.dtype)
    pairs = FOX_HEADS // 2
    parts = [cols(0, 384), cols(416, 672)]
    for hp in range(pairs):
        parts += [cols(1824 + 2 * hp, 1826 + 2 * hp), zeros(6)]
    parts += [zeros(ROPE_LANE0 - 8 * pairs), cols(384, 416), zeros(LANES - ROPE_LANE0 - MLA_ROPE), cols(672, 1824)]
    return jnp.concatenate(parts, axis=-1)


def _unpad_w_in(g):
    cols = lambda a, b: g[..., a:b]
    rope0 = TAIL0 + ROPE_LANE0
    parts = [cols(0, 384), cols(rope0, rope0 + MLA_ROPE), cols(384, 640), cols(ZA, N_PAD)]
    parts += [cols(TAIL0 + 8 * hp, TAIL0 + 8 * hp + 2) for hp in range(FOX_HEADS // 2)]
    return jnp.concatenate(parts, axis=-1)


def _small_pack(w_q_b, w_kv_b):
    a = jnp.pad(w_q_b, ((0, 0), (0, 0), (0, LANES - w_q_b.shape[2])))
    b = jnp.pad(w_kv_b, ((0, 0), (0, 0), (0, LANES - w_kv_b.shape[2])))
    return jnp.concatenate([a, b], axis=1)


def _small_unpack(p, cq, ckv):
    return p[:, 0:MLA_Q_RANK, 0:cq], p[:, MLA_Q_RANK:, 0:ckv]


def _mla_weights(wsm):
    H = MLA_HEADS
    cq = H * (MLA_NOPE + MLA_ROPE) // N_DEV
    ckv = H * (MLA_NOPE + MLA_V) // N_DEV
    wq = wsm[:, 0:MLA_Q_RANK, 0:cq].transpose(1, 0, 2).reshape(MLA_Q_RANK, H, MLA_NOPE + MLA_ROPE)
    wq = jnp.pad(wq, ((0, 0), (0, 0), (0, HEAD_BLOCK - MLA_NOPE - MLA_ROPE))).reshape(MLA_Q_RANK, H * HEAD_BLOCK)
    wkv = wsm[:, MLA_Q_RANK:, 0:ckv].transpose(1, 0, 2).reshape(MLA_KV_RANK, H, MLA_NOPE + MLA_V)
    wk = jnp.pad(wkv[:, :, 0:MLA_NOPE], ((0, 0), (0, 0), (0, HEAD_BLOCK - MLA_NOPE))).reshape(MLA_KV_RANK, H * HEAD_BLOCK)
    wv = wkv[:, :, MLA_NOPE:].reshape(MLA_KV_RANK, H * MLA_V)
    return wq, jnp.concatenate([wk, wv], axis=1)


def _mla_grads_to_blocks(dwq, dwkv):
    H = MLA_HEADS
    gq = dwq.reshape(MLA_Q_RANK, H, HEAD_BLOCK)[:, :, 0:MLA_NOPE + MLA_ROPE].reshape(MLA_Q_RANK, N_DEV, -1)
    gk = dwkv[:, 0:H * HEAD_BLOCK].reshape(MLA_KV_RANK, H, HEAD_BLOCK)[:, :, 0:MLA_NOPE]
    gv = dwkv[:, H * HEAD_BLOCK:].reshape(MLA_KV_RANK, H, MLA_V)
    gkv = jnp.concatenate([gk, gv], axis=2).reshape(MLA_KV_RANK, N_DEV, -1)
    return _small_pack(gq.transpose(1, 0, 2), gkv.transpose(1, 0, 2)).astype(BF16)


def _pool_blockdiag(pool_w):
    z = jnp.zeros((POOL_GROUP, POOL_GROUP), pool_w.dtype)
    halves = [jnp.concatenate([jnp.concatenate([pool_w[2 * i], z], axis=1),
                               jnp.concatenate([z, pool_w[2 * i + 1]], axis=1)], axis=0) for i in range(2)]
    return jnp.stack(halves)


def _pool_blockdiag_t(dw):
    g = POOL_GROUP
    return jnp.stack([dw[0, 0:g, 0:g], dw[0, g:, g:], dw[1, 0:g, 0:g], dw[1, g:, g:]])


def _gate_row(b):
    return jnp.zeros((LANES,), b.dtype).at[_F_LANES].set(b).reshape(1, LANES)


_SMALL = ("ffn1_norm", "mix_norm", "q_a_norm", "kv_a_norm", "pool_w", "pool_scale", "fox_b_f", "ffn2_norm", "final_norm")


def _pack_small(tree):
    rows, recipe = [], []
    for name in _SMALL:
        a = tree[name]
        flat = a.reshape(-1)
        n = flat.shape[0]
        nrow = -(-n // (8 * LANES)) * 8
        flat = jnp.pad(flat, (0, nrow * LANES - n))
        rows.append(flat.reshape(nrow, LANES))
        recipe.append((name, a.shape, n, nrow))
    return jnp.concatenate(rows, axis=0), recipe


def _unpack_small(packed, recipe):
    out, r0 = {}, 0
    for name, shape, n, nrow in recipe:
        out[name] = packed[r0:r0 + nrow].reshape(-1)[0:n].reshape(shape)
        r0 += nrow
    return out


def _adam_small(packs, w, m, v, name):
    R = w.shape[0]

    def body(p_ref, w_ref, m_ref, v_ref, g_out, d_out, m_out, v_out):
        g = p_ref[0]
        for p in range(1, N_DEV):
            g = g + p_ref[p]
        delta, mn, vn = _adam_math(g, w_ref[...], m_ref[...], v_ref[...])
        g_out[...] = g
        d_out[...] = delta
        m_out[...] = mn
        v_out[...] = vn

    blk = _bs((R, LANES), lambda i: (0, 0))
    shp = jax.ShapeDtypeStruct((R, LANES), F32)
    return pl.pallas_call(
        body, name=name, grid=(1,),
        in_specs=[_bs((N_DEV, R, LANES), lambda i: (0, 0, 0)), blk, blk, blk],
        out_specs=[blk, blk, blk, blk], out_shape=[shp, shp, shp, shp],
        compiler_params=_params(("arbitrary",)),
    )(packs, w, m, v)


_GATHER_PLAN = {
    ("first", 0): (("ffn1_w_gu", 0), ("ffn1_w_down", 0)),
    ("ffn1_fwd", 0): (("w_in", 0), ("w_small", 0), ("w_out", 0), ("ffn2_w_down", 0)),
    ("mla_attn_fwd", 0): (("ffn2_w_gu", 0),),
    ("fox_attn_fwd", 0): (("ffn1_w_down", 1), ("w_in", 1), ("w_small", 1), ("w_out", 1)),
    ("ffn2_fwd", 0): (("ffn1_w_gu", 1),),
    ("mla_attn_fwd", 1): (("ffn2_w_gu", 1), ("ffn2_w_down", 1)),
}
_SCATTER_PLAN = {
    ("mla_attn_bwd", 1): (("ffn2_w_gu", 1), ("w_out", 1)),
    ("fox_attn_bwd", 1): (("ffn2_w_down", 1),),
    ("ffn1_bwd", 1): (("w_in", 1), ("w_small", 1)),
    ("ffn2_bwd", 0): (("ffn1_w_down", 1),),
    ("mla_attn_bwd", 0): (("ffn1_w_gu", 1), ("w_out", 0)),
    ("fox_attn_bwd", 0): (("ffn2_w_gu", 0), ("ffn2_w_down", 0)),
    ("ffn1_bwd_a", 0): (("w_in", 0), ("w_small", 0)),
    ("ffn1_dwgu", 0): (("ffn1_w_down", 0),),
    ("ffn1_bwd_b", 0): (("ffn1_w_gu", 0),),
}
_SPLIT_BWD = (("ffn1", 0),)


class _Exchange:
    def __init__(self, shards, D, f_sh, r_in, r_out):
        self.shards = shards
        self.D, self.f_sh, self.r_in, self.r_out = D, f_sh, r_in, r_out
        self.weights, self.grads, self.recv = {}, {}, {}

    def _rows(self, kind):
        n = {"w_gu": 2 * self.f_sh, "w_down": self.f_sh}[kind]
        return lambda ref, p: ref.at[pl.ds(pl.multiple_of(p * n, 16), n)]

    def _gathered_shape(self, kind):
        D, f_sh = self.D, self.f_sh
        return {"w_gu": (N_DEV * 2 * f_sh, D), "w_down": (N_DEV * f_sh, D), "w_in": (N_DEV, self.r_in, N_PAD),
                "w_small": (N_DEV, MLA_Q_RANK + MLA_KV_RANK, LANES), "w_out": (N_DEV, self.r_out, D)}[kind]

    def _recv_shape(self, kind):
        D, f_sh = self.D, self.f_sh
        return {"w_gu": (N_DEV, 2 * f_sh, D), "w_down": (N_DEV, f_sh, D), "w_in": (N_DEV, self.r_in, N_IN),
                "w_small": (N_DEV, MLA_Q_RANK + MLA_KV_RANK, LANES), "w_out": (N_DEV, self.r_out, D)}[kind]

    @staticmethod
    def _kind(name):
        return name[5:] if name.startswith("ffn") else name

    def gather_rider(self, call, l):
        keys = _GATHER_PLAN.get((call, l))
        if not keys:
            return None
        by_dev = lambda ref, p: ref.at[p]
        shards, shapes, views = [], [], []
        for key in keys:
            kind = self._kind(key[0])
            shards.append(self.shards[key])
            shapes.append(jax.ShapeDtypeStruct(self._gathered_shape(kind), BF16))
            views.append(self._rows(kind) if kind in ("w_gu", "w_down") else by_dev)
        return _GatherRider(shards, shapes, views)

    def gathered(self, call, l, outs):
        for key, w in zip(_GATHER_PLAN.get((call, l), ()), outs):
            if self._kind(key[0]) in ("w_in", "w_out"):
                w = w.reshape((N_DEV * w.shape[1],) + w.shape[2:])
            self.weights[key] = w

    def scatter_rider(self, call, l, pack=None):
        keys = _SCATTER_PLAN.get((call, l), ())
        if not keys and pack is None:
            return None
        by_dev = lambda ref, p: ref.at[p]
        srcs, shapes, src_of = [], [], []
        for key in keys:
            kind = self._kind(key[0])
            srcs.append(self.grads[key])
            shapes.append(jax.ShapeDtypeStruct(self._recv_shape(kind), BF16))
            src_of.append(self._rows(kind) if kind in ("w_gu", "w_down") else by_dev)
        if pack is not None:
            srcs.append(pack)
            shapes.append(jax.ShapeDtypeStruct((N_DEV,) + pack.shape, pack.dtype))
            src_of.append(lambda ref, p: ref)
        return _ScatterRider(srcs, shapes, src_of, [by_dev] * len(srcs))

    def scattered(self, call, l, outs):
        for key, r in zip(_SCATTER_PLAN.get((call, l), ()), outs):
            self.recv[key] = r


def _local_step(x, target, ex, small):
    S, D = x.shape
    tabs = _rope_tables(S)
    nhp_a, nhp_c = MLA_HEADS // 2, FOX_HEADS // 2
    fox_scale = 1.0 / math.sqrt(FOX_HEAD_DIM)
    ex.gathered("first", 0, _comm_call(ex.gather_rider("first", 0), "gather_first"))
    saved = []
    for l in range(DEPTH):
        s = {}
        s["x0"] = x
        wgu1, wd1 = ex.weights[("ffn1_w_gu", l)], ex.weights[("ffn1_w_down", l)]
        (x1, s["h1"], s["gu1"]), got = _ffn_fwd_full(x, small["ffn1_norm"][l][None], wgu1, wd1, FFN_FWD_TOKENS,
                                                    f"ffn1_fwd_l{l}", rider=ex.gather_rider("ffn1_fwd", l))
        ex.gathered("ffn1_fwd", l, got)
        s["x1"] = x1
        w_in = ex.weights[("w_in", l)]
        s["h2"], za, zf, zkvt = _mix_in_fwd(x1, small["mix_norm"][l][None], w_in, f"mix_in_fwd_l{l}")
        s["za"], s["zf"], s["zkvt"] = za, zf, zkvt
        wq, wkv = _mla_weights(ex.weights[("w_small", l)])
        s["wq"], s["wkv"] = wq, wkv
        gq, gkv = small["q_a_norm"][l][None], small["kv_a_norm"][l][None]
        qf, kf, vm, kft, vmt = _mla_prep(za, gq, gkv, wq, wkv, tabs, f"mla_prep_l{l}")
        s["qf"], s["kf"], s["vm"], s["kft"] = qf, kf, vm, kft
        (ya, lse_a), got = _attn_fwd_t(qf, kf, vmt, nhp=nhp_a, dkb=LANES, qoff=0, koff=0, vtoff=0, scale=1.0,
                                       cum=None, cumT=None, name=f"mla_attn_fwd_l{l}",
                                       rider=ex.gather_rider("mla_attn_fwd", l))
        ex.gathered("mla_attn_fwd", l, got)
        s["ya"], s["lse_a"] = ya, lse_a
        b_row = _gate_row(small["fox_b_f"][l])
        s["b_row"] = b_row
        cum, cumT = _fox_prep(za, b_row, f"fox_prep_l{l}")
        s["cum"], s["cumT"] = cum, cumT
        (yc, lse_c), got = _attn_fwd_t(zf, zf, zkvt, nhp=nhp_c, dkb=64, qoff=0, koff=nhp_c, vtoff=nhp_c, scale=fox_scale,
                                       cum=cum, cumT=cumT, name=f"fox_attn_fwd_l{l}",
                                       rider=ex.gather_rider("fox_attn_fwd", l))
        ex.gathered("fox_attn_fwd", l, got)
        s["yc"], s["lse_c"] = yc, lse_c
        wbd = _pool_blockdiag(small["pool_w"][l]).astype(BF16)
        s["wbd"] = wbd
        psc = small["pool_scale"][l][None]
        yb, s["pd"] = _pool_fwd(za, wbd, psc, f"pool_fwd_l{l}")
        w_out = ex.weights[("w_out", l)]
        x2, s["ycat"] = _mix_out_fwd(x1, ya, yb, yc, w_out, f"mix_out_fwd_l{l}")
        s["x2"] = x2
        wgu2, wd2 = ex.weights[("ffn2_w_gu", l)], ex.weights[("ffn2_w_down", l)]
        (x, s["h3"], s["gu2"]), got = _ffn_fwd_full(x2, small["ffn2_norm"][l][None], wgu2, wd2, FFN_FWD_TOKENS,
                                                    f"ffn2_fwd_l{l}", rider=ex.gather_rider("ffn2_fwd", l))
        ex.gathered("ffn2_fwd", l, got)
        saved.append(s)

    dx, d_final, loss = _loss_head(x, small["final_norm"][None], target, "loss_head")

    small_grads = [None] * DEPTH
    for l in reversed(range(DEPTH)):
        s = saved[l]
        g = {}
        wgu2, wd2 = ex.weights[("ffn2_w_gu", l)], ex.weights[("ffn2_w_down", l)]
        dy3 = dx
        (dx, g["ffn2_norm"], dgu, act, dyh), got = _ffn_bwd_full(
            dy3, s["x2"], small["ffn2_norm"][l][None], s["gu2"], wgu2, wd2, FFN_BWD_TOKENS, f"ffn2_bwd_l{l}",
            rider=ex.scatter_rider("ffn2_bwd", l))
        ex.scattered("ffn2_bwd", l, got)
        ex.grads[("ffn2_w_gu", l)] = _ffn_weight_grad(dgu, s["h3"], f"ffn2_dwgu_l{l}")
        ex.grads[("ffn2_w_down", l)] = _ffn_weight_grad(act, dyh, f"ffn2_dwd_l{l}")

        w_out = ex.weights[("w_out", l)]
        dya, dyb, dyc, dvec_a, dvec_c = _mix_out_bwd(dx, w_out, s["ya"], s["yc"], f"mix_out_bwd_l{l}")
        dw_out = _mm_tn(s["ycat"][None], dx[None], 1, lambda p: 0, lambda p: 0, 1024, 1024, f"dwout_l{l}", ts=1024)[0]
        ex.grads[("w_out", l)] = dw_out.reshape(N_DEV, ex.r_out, D)

        (dqf, dkf, dvm), got = _attn_bwd_t(s["qf"], s["kf"], s["kft"], s["vm"], dya, s["lse_a"], dvec_a, nhp=nhp_a,
                                           dkb=LANES, qoff=0, koff=0, ktoff=0, voff=0, scale=1.0, cum=None, cumT=None,
                                           name=f"mla_attn_bwd_l{l}", rider=ex.scatter_rider("mla_attn_bwd", l))
        ex.scattered("mla_attn_bwd", l, got)
        zf = s["zf"]
        (dqc, dkc, dvc, dcq, dck), got = _attn_bwd_t(zf, zf, s["zkvt"], zf, dyc, s["lse_c"], dvec_c, nhp=nhp_c, dkb=64,
                                                     qoff=0, koff=nhp_c, ktoff=0, voff=2 * nhp_c, scale=fox_scale,
                                                     cum=s["cum"], cumT=s["cumT"], name=f"fox_attn_bwd_l{l}",
                                                     rider=ex.scatter_rider("fox_attn_bwd", l), grad_dtype=BF16)
        ex.scattered("fox_attn_bwd", l, got)
        dtail_f, db = _fox_prep_bwd(s["za"], s["b_row"], dcq, dck, f"fox_prep_bwd_l{l}")
        g["fox_b_f"] = db[0, _F_LANES]
        psc = small["pool_scale"][l][None]
        du, dwbd, dpsc = _pool_bwd(dyb, s["pd"], s["wbd"], psc, f"pool_bwd_l{l}")
        g["pool_w"] = _pool_blockdiag_t(dwbd)
        g["pool_scale"] = dpsc[0]
        gq, gkv = small["q_a_norm"][l][None], small["kv_a_norm"][l][None]
        dza, dwq, dwkv, dgq, dgkv = _mla_prep_bwd(s["za"], gq, gkv, s["wq"], s["wkv"], tabs, dqf, dkf, dvm,
                                                   f"mla_prep_bwd_l{l}")
        g["q_a_norm"], g["kv_a_norm"] = dgq[0], dgkv[0]
        ex.grads[("w_small", l)] = _mla_grads_to_blocks(dwq, dwkv)
        w_in = ex.weights[("w_in", l)]
        dx, g["mix_norm"], dz = _mix_in_bwd(dx, s["x1"], small["mix_norm"][l][None], dza, du, dtail_f, dqc, dkc, dvc,
                                            w_in, f"mix_in_bwd_l{l}")
        dw_in = _unpad_w_in(_mm_tn(s["h2"][None], dz[None], 1, lambda p: 0, lambda p: 0, 1024, 640, f"dwin_l{l}",
                                   ts=4096)[0])
        ex.grads[("w_in", l)] = dw_in.reshape(N_DEV, ex.r_in, N_IN)

        wgu1, wd1 = ex.weights[("ffn1_w_gu", l)], ex.weights[("ffn1_w_down", l)]
        dy1 = dx
        gam1 = small["ffn1_norm"][l][None]
        split = ("ffn1", l) in _SPLIT_BWD
        if split:
            (dgu, act, dyh), got = _ffn_bwd_full(dy1, None, None, s["gu1"], None, wd1, FFN_BWD_TOKENS, f"ffn1_bwd_a_l{l}",
                                                 phase="act", rider=ex.scatter_rider("ffn1_bwd_a", l))
            ex.scattered("ffn1_bwd_a", l, got)
        else:
            (dx, g["ffn1_norm"], dgu, act, dyh), got = _ffn_bwd_full(
                dy1, s["x0"], gam1, s["gu1"], wgu1, wd1, FFN_BWD_TOKENS, f"ffn1_bwd_l{l}",
                rider=ex.scatter_rider("ffn1_bwd", l))
            ex.scattered("ffn1_bwd", l, got)
        ex.grads[("ffn1_w_down", l)] = _ffn_weight_grad(act, dyh, f"ffn1_dwd_l{l}")
        rider = ex.scatter_rider("ffn1_dwgu", l)
        dwgu = _ffn_weight_grad(dgu, s["h1"], f"ffn1_dwgu_l{l}", rider=rider)
        if rider is not None:
            dwgu, got = dwgu
            ex.scattered("ffn1_dwgu", l, got)
        ex.grads[("ffn1_w_gu", l)] = dwgu
        if split:
            (dx, g["ffn1_norm"]), got = _ffn_bwd_full(dy1, s["x0"], gam1, None, wgu1, None, FFN_BWD_TOKENS,
                                                     f"ffn1_bwd_b_l{l}", phase="in", dgu_in=dgu,
                                                     rider=ex.scatter_rider("ffn1_bwd_b", l))
            ex.scattered("ffn1_bwd_b", l, got)
        for k in ("ffn1_norm", "ffn2_norm", "mix_norm"):
            g[k] = g[k][0]
        small_grads[l] = g
    return loss, dx, small_grads, d_final[0]


_BIG = ("ffn1_w_gu", "ffn1_w_down", "w_in", "w_small", "w_out", "ffn2_w_gu", "ffn2_w_down")


def kernel(x, ffn1_norm, ffn1_w_gu, ffn1_w_down, mix_norm, w_in, q_a_norm, w_q_b, kv_a_norm, w_kv_b, pool_w, pool_scale, fox_b_f, w_out, ffn2_norm, ffn2_w_gu, ffn2_w_down, final_norm, loss_target, m_ffn1_norm, m_ffn1_w_gu, m_ffn1_w_down, m_mix_norm, m_w_in, m_q_a_norm, m_w_q_b, m_kv_a_norm, m_w_kv_b, m_pool_w, m_pool_scale, m_fox_b_f, m_w_out, m_ffn2_norm, m_ffn2_w_gu, m_ffn2_w_down, m_final_norm, v_ffn1_norm, v_ffn1_w_gu, v_ffn1_w_down, v_mix_norm, v_w_in, v_q_a_norm, v_w_q_b, v_kv_a_norm, v_w_kv_b, v_pool_w, v_pool_scale, v_fox_b_f, v_w_out, v_ffn2_norm, v_ffn2_w_gu, v_ffn2_w_down, v_final_norm):
    W = dict(ffn1_norm=ffn1_norm, ffn1_w_gu=ffn1_w_gu, ffn1_w_down=ffn1_w_down, mix_norm=mix_norm, w_in=w_in,
             q_a_norm=q_a_norm, w_q_b=w_q_b, kv_a_norm=kv_a_norm, w_kv_b=w_kv_b, pool_w=pool_w, pool_scale=pool_scale,
             fox_b_f=fox_b_f, w_out=w_out, ffn2_norm=ffn2_norm, ffn2_w_gu=ffn2_w_gu, ffn2_w_down=ffn2_w_down,
             final_norm=final_norm)
    M = dict(ffn1_norm=m_ffn1_norm, ffn1_w_gu=m_ffn1_w_gu, ffn1_w_down=m_ffn1_w_down, mix_norm=m_mix_norm, w_in=m_w_in,
             q_a_norm=m_q_a_norm, w_q_b=m_w_q_b, kv_a_norm=m_kv_a_norm, w_kv_b=m_w_kv_b, pool_w=m_pool_w,
             pool_scale=m_pool_scale, fox_b_f=m_fox_b_f, w_out=m_w_out, ffn2_norm=m_ffn2_norm, ffn2_w_gu=m_ffn2_w_gu,
             ffn2_w_down=m_ffn2_w_down, final_norm=m_final_norm)
    V = dict(ffn1_norm=v_ffn1_norm, ffn1_w_gu=v_ffn1_w_gu, ffn1_w_down=v_ffn1_w_down, mix_norm=v_mix_norm, w_in=v_w_in,
             q_a_norm=v_q_a_norm, w_q_b=v_w_q_b, kv_a_norm=v_kv_a_norm, w_kv_b=v_w_kv_b, pool_w=v_pool_w,
             pool_scale=v_pool_scale, fox_b_f=v_fox_b_f, w_out=v_w_out, ffn2_norm=v_ffn2_norm, ffn2_w_gu=v_ffn2_w_gu,
             ffn2_w_down=v_ffn2_w_down, final_norm=v_final_norm)
    L, D, n_sh = ffn1_w_gu.shape
    f_sh = ffn1_w_down.shape[1]
    assert n_sh == 2 * f_sh and L == DEPTH
    r_in, r_out = w_in.shape[1], w_out.shape[1]

    tr_in = lambda a: a.transpose(0, 2, 1)
    big_shards = dict(
        ffn1_w_gu=tr_in(ffn1_w_gu).astype(BF16), ffn1_w_down=ffn1_w_down.astype(BF16),
        w_in=_pad_w_in(w_in).astype(BF16), w_small=_small_pack(w_q_b, w_kv_b).astype(BF16), w_out=w_out.astype(BF16),
        ffn2_w_gu=tr_in(ffn2_w_gu).astype(BF16), ffn2_w_down=ffn2_w_down.astype(BF16))
    ex = _Exchange({(k, l): big_shards[k][l] for k in _BIG for l in range(L)}, D, f_sh, r_in, r_out)

    small = {k: W[k] for k in _SMALL}
    loss, dx, grads, d_final = _local_step(x[0], loss_target[0], ex, small)

    small_g = {k: jnp.stack([grads[l][k] for l in range(L)]) for k in _SMALL if k != "final_norm"}
    small_g["final_norm"] = d_final
    pack_g, recipe = _pack_small(small_g)
    n_small = pack_g.shape[0]
    loss_row = -(-n_small // 8) * 8
    pack_g = jnp.concatenate([pack_g, jnp.zeros((loss_row - n_small, LANES), F32), jnp.broadcast_to(loss, (8, LANES))],
                             axis=0)
    *got, packs = _comm_call(ex.scatter_rider("last", 0, pack=pack_g), "scatter_last")
    ex.scattered("last", 0, got)

    out = {}
    sm_w, sm_m, sm_v = (_small_pack(t["w_q_b"], t["w_kv_b"]) for t in (W, M, V))
    big = [("ffn1_w_gu", tr_in(W["ffn1_w_gu"]), tr_in(M["ffn1_w_gu"]), tr_in(V["ffn1_w_gu"]), f_sh),
           ("ffn1_w_down", W["ffn1_w_down"], M["ffn1_w_down"], V["ffn1_w_down"], 352),
           ("w_in", W["w_in"], M["w_in"], V["w_in"], 128),
           ("w_small", sm_w, sm_m, sm_v, 384),
           ("w_out", W["w_out"], M["w_out"], V["w_out"], 128),
           ("ffn2_w_gu", tr_in(W["ffn2_w_gu"]), tr_in(M["ffn2_w_gu"]), tr_in(V["ffn2_w_gu"]), f_sh),
           ("ffn2_w_down", W["ffn2_w_down"], M["ffn2_w_down"], V["ffn2_w_down"], 352)]
    for k, w_, m_, v_, tr in big:
        res = None
        for l in range(L):
            res = _adam_sum(ex.recv[(k, l)], w_, m_, v_, l, res, tr, f"adam_{k}_l{l}")
        if k == "w_small":
            cq, ckv = w_q_b.shape[2], w_kv_b.shape[2]
            parts = [_small_unpack(r, cq, ckv) for r in res]
            out["w_q_b"] = [p[0] for p in parts]
            out["w_kv_b"] = [p[1] for p in parts]
        elif k.endswith("w_gu"):
            out[k] = [tr_in(r) for r in res]
        else:
            out[k] = res

    pw, _ = _pack_small({k: W[k] for k in _SMALL})
    pm, _ = _pack_small({k: M[k] for k in _SMALL})
    pv, _ = _pack_small({k: V[k] for k in _SMALL})
    extra = ((0, loss_row + 8 - n_small), (0, 0))
    res = _adam_small(packs, jnp.pad(pw, extra), jnp.pad(pm, extra), jnp.pad(pv, extra), "adam_small")
    loss_total = res[0][loss_row, 0]
    small_out = [_unpack_small(r, recipe) for r in res]
    for k in _SMALL:
        out[k] = [t[k] for t in small_out]

    names = ["ffn1_norm", "ffn1_w_gu", "ffn1_w_down", "mix_norm", "w_in", "q_a_norm", "w_q_b", "kv_a_norm", "w_kv_b",
             "pool_w", "pool_scale", "fox_b_f", "w_out", "ffn2_norm", "ffn2_w_gu", "ffn2_w_down", "final_norm"]
    outs = [loss_total, dx[None]]
    for which in range(4):
        outs += [out[k][which] for k in names]
    return tuple(outs)
```

```python
import functools
import math

import numpy as np
import jax
import jax.numpy as jnp
from jax import lax
from jax.experimental import pallas as pl
from jax.experimental.pallas import tpu as pltpu

F32 = jnp.float32
BF16 = jnp.bfloat16
MESH_ID = pl.DeviceIdType.MESH

N_DEV = 8
EPS = 1e-6
DEPTH = 2

MLA_HEADS = 6
MLA_Q_RANK = 256
MLA_KV_RANK = 128
MLA_NOPE = 64
MLA_ROPE = 32
MLA_V = 64
ROPE_THETA = 10000.0
POOL_WINDOWS = (2, 4, 8, 16)
POOL_GROUP = 64
POOL_WIDTH = 256
FOX_HEADS = 6
FOX_HEAD_DIM = 64
N_IN = 1830

ADAM_LR = 0.001
ADAM_B1 = 0.9
ADAM_B2 = 0.999
ADAM_EPS = 1e-08
ADAM_WD = 0.01
ADAM_STEP = 10

LANES = 128
HEAD_BLOCK = 128
VMEM_LIMIT = 48 * 1024 * 1024
VMEM_LIMIT_BIG = 50 * 1024 * 1024
NEG = -1e30
ATTN_BLOCK = 1024

ZA = 768
ZF = 1152
N_PAD = ZA + ZF
TAIL0 = 640
ROPE_LANE0 = 64


def _f_lane(h):
    return 8 * (h // 2) + (h % 2)


_F_LANES = np.array([_f_lane(h) for h in range(FOX_HEADS)], np.int32)


def _dot(a, b):
    return jnp.dot(a, b, preferred_element_type=F32)


def _dot_nt(a, b):
    return lax.dot_general(a, b, (((1,), (1,)), ((), ())), preferred_element_type=F32)


def _dot_tn(a, b):
    return lax.dot_general(a, b, (((0,), (0,)), ((), ())), preferred_element_type=F32)


def _rms(x, gam):
    r = lax.rsqrt(jnp.mean(x * x, axis=-1, keepdims=True) + EPS)
    return x * r * gam


def _rms_bwd(dy, x, gam):
    r = lax.rsqrt(jnp.mean(x * x, axis=-1, keepdims=True) + EPS)
    xh = x * r
    dxh = dy * gam
    dx = r * (dxh - xh * jnp.mean(dxh * xh, axis=-1, keepdims=True))
    return dx, jnp.sum(dy * xh, axis=0, keepdims=True)


def _accum_out(ref, first, val):
    @pl.when(first)
    def _():
        ref[...] = val

    @pl.when(jnp.logical_not(first))
    def _():
        ref[...] += val


def _bs(shape, fn):
    return pl.BlockSpec(shape, fn)


def _params(dims, vmem=VMEM_LIMIT):
    return pltpu.CompilerParams(dimension_semantics=dims, vmem_limit_bytes=vmem)


def _tile(n, t):
    t = min(n, t)
    assert n % t == 0, (n, t)
    return t


HBM_SPEC = pl.BlockSpec(memory_space=pl.ANY)


def _call(body, *, name, grid, in_specs, out_specs, out_shape, scratch, dims, args, rider=None, vmem=VMEM_LIMIT):
    n_in, n_out = len(in_specs), len(out_specs)
    if rider is None:
        outs = pl.pallas_call(body, name=name, grid=grid, in_specs=in_specs, out_specs=out_specs, out_shape=out_shape,
                              scratch_shapes=scratch, compiler_params=_params(dims, vmem))(*args)
        return list(outs), []
    k_in, k_out, k_sem = len(rider.srcs), len(rider.out_shapes), len(rider.scratch)

    def riding(*refs):
        a, b, c, d = n_in, n_in + k_in, n_in + k_in + n_out, n_in + k_in + n_out + k_out
        rest = refs[d:]
        sems = rest[len(rest) - k_sem:]
        step = 0
        for i, g in enumerate(grid):
            step = step * g + pl.program_id(i)
        n_steps = math.prod(grid)

        @pl.when(step == 0)
        def _():
            rider.begin(refs[a:b], refs[c:d], sems)

        body(*refs[:a], *refs[b:c], *rest[:len(rest) - k_sem])

        @pl.when(step == (3 * n_steps) // 4)
        def _():
            rider.middle(refs[a:b], refs[c:d], sems)

        @pl.when(step == n_steps - 1)
        def _():
            rider.end(refs[a:b], refs[c:d], sems)

    outs = pl.pallas_call(
        riding, name=name, grid=grid, in_specs=list(in_specs) + [HBM_SPEC] * k_in,
        out_specs=list(out_specs) + [HBM_SPEC] * k_out, out_shape=list(out_shape) + list(rider.out_shapes),
        scratch_shapes=list(scratch) + list(rider.scratch),
        compiler_params=_params(("arbitrary",) * len(grid), vmem))(*args, *rider.srcs)
    return list(outs[:n_out]), list(outs[n_out:])


_ONCE = pl.Buffered(1)
_FF_CHUNKS = ((0, 1536), (1536, 2816))


def _ffn_fwd_full(x, gam, wgut, wd, tm, name, rider=None):
    S, D = x.shape
    F = wd.shape[0]
    tm = _tile(S, tm)
    chunks = _FF_CHUNKS if F == 2816 else ((0, F),)

    def body(x_ref, gam_ref, wgut_ref, wd_ref, xo_ref, h_ref, gu_ref):
        h = _rms(x_ref[...], gam_ref[...]).astype(BF16)
        h_ref[...] = h
        y = None
        for c0, c1 in chunks:
            g = _dot_nt(h, wgut_ref[c0:c1, :])
            u = _dot_nt(h, wgut_ref[F + c0:F + c1, :])
            gu_ref[:, c0:c1] = g.astype(BF16)
            gu_ref[:, F + c0:F + c1] = u.astype(BF16)
            a = (g * jax.nn.sigmoid(g) * u).astype(BF16)
            part = _dot(a, wd_ref[c0:c1, :])
            y = part if y is None else y + part
        xo_ref[...] = x_ref[...] + 0.5 * y

    row = lambda i: (i, 0)
    fix = lambda i: (0, 0)
    return _call(
        body, name=name, grid=(S // tm,),
        in_specs=[_bs((tm, D), row), _bs((1, D), fix), pl.BlockSpec((2 * F, D), fix, pipeline_mode=_ONCE),
                  pl.BlockSpec((F, D), fix, pipeline_mode=_ONCE)],
        out_specs=[_bs((tm, D), row), _bs((tm, D), row), _bs((tm, 2 * F), row)],
        out_shape=[jax.ShapeDtypeStruct((S, D), F32), jax.ShapeDtypeStruct((S, D), BF16),
                   jax.ShapeDtypeStruct((S, 2 * F), BF16)],
        scratch=[], dims=("parallel",), args=(x, gam, wgut, wd), rider=rider)


def _ffn_bwd_full(dy, x, gam, gu, wgut, wd, tm, name, phase="all", dgu_in=None, rider=None):
    S, D = dy.shape
    F = wd.shape[0] if wd is not None else wgut.shape[0] // 2
    tm = _tile(S, tm)
    chunks = _FF_CHUNKS if F == 2816 else ((0, F),)
    act, inp = phase in ("all", "act"), phase in ("all", "in")

    def body(*refs):
        refs = list(refs)
        dy_ref = refs.pop(0)
        x_ref, gam_ref = (refs.pop(0), refs.pop(0)) if inp else (None, None)
        gu_ref = refs.pop(0)
        wgut_ref = refs.pop(0) if inp else None
        wd_ref = refs.pop(0) if act else None
        if inp:
            dx_ref, dgam_ref = refs.pop(0), refs.pop(0)
        if act:
            dgu_ref, a_ref, dyh_ref = refs.pop(0), refs.pop(0), refs.pop(0)
            dyh = (0.5 * dy_ref[...]).astype(BF16)
            dyh_ref[...] = dyh
        dh = None
        for c0, c1 in chunks:
            if act:
                dg, du, a = _swiglu_bwd(_dot_nt(dyh, wd_ref[c0:c1, :]), gu_ref[:, c0:c1], gu_ref[:, F + c0:F + c1])
                dgu_ref[:, c0:c1] = dg
                dgu_ref[:, F + c0:F + c1] = du
                a_ref[:, c0:c1] = a
            else:
                dg, du = gu_ref[:, c0:c1], gu_ref[:, F + c0:F + c1]
            if inp:
                part = _dot(dg, wgut_ref[c0:c1, :]) + _dot(du, wgut_ref[F + c0:F + c1, :])
                dh = part if dh is None else dh + part
        if inp:
            dxn, dgam = _rms_bwd(dh, x_ref[...], gam_ref[...])
            dx_ref[...] = dy_ref[...] + dxn
            _accum_out(dgam_ref, pl.program_id(0) == 0, dgam)

    row = lambda i: (i, 0)
    fix = lambda i: (0, 0)
    in_specs, args = [_bs((tm, D), row)], [dy]
    if inp:
        in_specs += [_bs((tm, D), row), _bs((1, D), fix)]
        args += [x, gam]
    in_specs += [_bs((tm, 2 * F), row)]
    args += [gu if act else dgu_in]
    if inp:
        in_specs += [pl.BlockSpec((2 * F, D), fix, pipeline_mode=_ONCE)]
        args += [wgut]
    if act:
        in_specs += [pl.BlockSpec((F, D), fix, pipeline_mode=_ONCE)]
        args += [wd]
    out_specs, out_shape = [], []
    if inp:
        out_specs += [_bs((tm, D), row), _bs((1, D), fix)]
        out_shape += [jax.ShapeDtypeStruct((S, D), F32), jax.ShapeDtypeStruct((1, D), F32)]
    if act:
        out_specs += [_bs((tm, 2 * F), row), _bs((tm, F), row), _bs((tm, D), row)]
        out_shape += [jax.ShapeDtypeStruct((S, 2 * F), BF16), jax.ShapeDtypeStruct((S, F), BF16),
                      jax.ShapeDtypeStruct((S, D), BF16)]
    return _call(body, name=name, grid=(S // tm,), in_specs=in_specs, out_specs=out_specs, out_shape=out_shape,
                 scratch=[], dims=("arbitrary",), args=tuple(args), rider=rider, vmem=VMEM_LIMIT_BIG)


def _swiglu_bwd(da, g, u):
    g = g.astype(F32)
    u = u.astype(F32)
    sig = jax.nn.sigmoid(g)
    sl = g * sig
    dg = (da * u * (sig * (1.0 + g * (1.0 - sig)))).astype(BF16)
    return dg, (da * sl).astype(BF16), (sl * u).astype(BF16)


def _mm_tn(a, b, nb, a_of, b_of, tm, tn, name, rider=None, ts=512):
    _, S, M = a.shape
    N = b.shape[2]
    tm = _tile(M, tm)
    tn = _tile(N, tn)
    ts = _tile(S, ts)
    nk = S // ts

    def body(a_ref, b_ref, o_ref, acc):
        k = pl.program_id(3)

        @pl.when(k == 0)
        def _():
            acc[...] = jnp.zeros_like(acc)

        acc[...] += _dot_tn(a_ref[...].astype(BF16), b_ref[...].astype(BF16))

        @pl.when(k == nk - 1)
        def _():
            o_ref[...] = acc[...].astype(o_ref.dtype)

    (out,), extra = _call(
        body, name=name, grid=(nb, M // tm, N // tn, nk),
        in_specs=[_bs((None, ts, tm), lambda p, i, j, k: (a_of(p), k, i)),
                  _bs((None, ts, tn), lambda p, i, j, k: (b_of(p), k, j))],
        out_specs=[_bs((None, tm, tn), lambda p, i, j, k: (p, i, j))],
        out_shape=[jax.ShapeDtypeStruct((nb, M, N), BF16)],
        scratch=[pltpu.VMEM((tm, tn), F32)],
        dims=("parallel", "parallel", "parallel", "arbitrary"), args=(a, b), rider=rider)
    return (out, extra) if rider is not None else out


FFN_FWD_TOKENS = 512
FFN_BWD_TOKENS = 256
FFN_GRAD_ROWS = 1408


def _ffn_weight_grad(a, b, name, rider=None):
    M = a.shape[1]
    tm = max(t for t in range(LANES, FFN_GRAD_ROWS + 1, LANES) if M % t == 0)
    res = _mm_tn(a[None], b[None], 1, lambda p: 0, lambda p: 0, tm, b.shape[1], name, rider=rider, ts=2048)
    return (res[0][0], res[1]) if rider is not None else res[0]


def _mix_in_fwd(x, gam, w_in, name):
    S, D = x.shape
    tm = _tile(S, 512)
    nkv = (ZF - 384) // LANES

    def body(x_ref, gam_ref, w_ref, h_ref, za_ref, zf_ref, zt_ref):
        hb = _rms(x_ref[...], gam_ref[...]).astype(BF16)
        h_ref[...] = hb
        za_ref[...] = _dot(hb, w_ref[:, 0:ZA])
        zf = _dot(hb, w_ref[:, ZA:N_PAD])
        zf_ref[...] = zf.astype(BF16)
        for c in range(nkv):
            zt_ref[c * LANES:(c + 1) * LANES, :] = zf[:, 384 + c * LANES:384 + (c + 1) * LANES].T.astype(BF16)

    return pl.pallas_call(
        body, name=name, grid=(S // tm,),
        in_specs=[_bs((tm, D), lambda i: (i, 0)), _bs((1, D), lambda i: (0, 0)), _bs((D, N_PAD), lambda i: (0, 0))],
        out_specs=[_bs((tm, D), lambda i: (i, 0)), _bs((tm, ZA), lambda i: (i, 0)), _bs((tm, ZF), lambda i: (i, 0)),
                   _bs((nkv * LANES, tm), lambda i: (0, i))],
        out_shape=[jax.ShapeDtypeStruct((S, D), BF16), jax.ShapeDtypeStruct((S, ZA), F32),
                   jax.ShapeDtypeStruct((S, ZF), BF16), jax.ShapeDtypeStruct((nkv * LANES, S), BF16)],
        compiler_params=_params(("parallel",)),
    )(x, gam, w_in)


def _mix_in_bwd(dy, x, gam, dza_mla, dtail_k, du, dtail_f, dqf, dkf, dvf, w_in, name):
    S, D = x.shape
    tm = _tile(S, 512)

    def body(dy_ref, x_ref, gam_ref, dza_ref, dtk_ref, du_ref, dt_ref, dq_ref, dk_ref, dv_ref, w_ref,
             dx_ref, dgam_ref, dz_ref):
        i = pl.program_id(0)
        tail = (dtk_ref[...] + dt_ref[...]).astype(BF16)
        dz = jnp.concatenate([dza_ref[...], du_ref[...], tail, dq_ref[...], dk_ref[...], dv_ref[...]], axis=1)
        dz_ref[...] = dz
        dh = _dot_nt(dz, w_ref[...])
        dxn, dgam = _rms_bwd(dh, x_ref[...], gam_ref[...])
        dx_ref[...] = dy_ref[...] + dxn
        _accum_out(dgam_ref, i == 0, dgam)

    row = lambda i: (i, 0)
    fix = lambda i: (0, 0)
    return pl.pallas_call(
        body, name=name, grid=(S // tm,),
        in_specs=[_bs((tm, D), row), _bs((tm, D), row), _bs((1, D), fix), _bs((tm, 384), row), _bs((tm, 128), row),
                  _bs((tm, 256), row), _bs((tm, 128), row), _bs((tm, 384), row), _bs((tm, 384), row), _bs((tm, 384), row),
                  _bs((D, N_PAD), fix)],
        out_specs=[_bs((tm, D), row), _bs((1, D), fix), _bs((tm, N_PAD), row)],
        out_shape=[jax.ShapeDtypeStruct((S, D), F32), jax.ShapeDtypeStruct((1, D), F32),
                   jax.ShapeDtypeStruct((S, N_PAD), BF16)],
        compiler_params=_params(("arbitrary",)),
    )(dy, x, gam, dza_mla, dtail_k, du, dtail_f, dqf, dkf, dvf, w_in)


def _rope_tables(S):
    half = MLA_ROPE // 2
    inv_freq = ROPE_THETA ** (-jnp.arange(0, MLA_ROPE, 2, dtype=F32) / MLA_ROPE)
    ang = jnp.arange(S, dtype=jnp.int32).astype(F32)[:, None] * inv_freq[None, :]
    cos, sin = jnp.cos(ang), jnp.sin(ang)
    one = jnp.ones((S, ROPE_LANE0), F32)
    zero = jnp.zeros((S, ROPE_LANE0), F32)
    pad1 = jnp.ones((S, LANES - ROPE_LANE0 - MLA_ROPE), F32)
    pad0 = jnp.zeros((S, LANES - ROPE_LANE0 - MLA_ROPE), F32)
    zh = jnp.zeros((S, half), F32)
    tab_c = jnp.concatenate([one, cos, cos, pad1], axis=1)
    tab_ck = jnp.concatenate([zero, cos, cos, pad0], axis=1)
    tab_s1 = jnp.concatenate([zero, -sin, zh, pad0], axis=1)
    tab_s2 = jnp.concatenate([zero, zh, sin, pad0], axis=1)
    return tab_c, tab_ck, tab_s1, tab_s2


def _rope(x, c, s1, s2):
    return x * c + pltpu.roll(x, LANES - 16, 1) * s1 + pltpu.roll(x, 16, 1) * s2


def _rope_t(dy, c, s1, s2):
    return dy * c + pltpu.roll(dy * s1, 16, 1) + pltpu.roll(dy * s2, LANES - 16, 1)


_MLA_SCALE = 1.0 / math.sqrt(MLA_NOPE + MLA_ROPE)


def _mla_prep(za, gq, gkv, wq, wkv, tabs, name):
    S = za.shape[0]
    tm = _tile(S, 512)
    H = MLA_HEADS

    def body(zq_ref, tail_ref, gq_ref, gkv_ref, wq_ref, wkv_ref, c_ref, ck_ref, s1_ref, s2_ref,
             qf_ref, kf_ref, v_ref, kft_ref, vt_ref):
        zq = zq_ref[...]
        c, s1, s2 = c_ref[...], s1_ref[...], s2_ref[...]
        qn = _rms(zq[:, 0:256], gq_ref[...]).astype(BF16)
        q = _dot(qn, wq_ref[...])
        for h in range(H):
            blk = _rope(q[:, h * LANES:(h + 1) * LANES], c, s1, s2)
            qf_ref[:, h * LANES:(h + 1) * LANES] = (blk * _MLA_SCALE).astype(BF16)
        kvn = _rms(zq[:, 256:384], gkv_ref[...]).astype(BF16)
        kv = _dot(kvn, wkv_ref[...])
        kt = _rope(tail_ref[...], ck_ref[...], s1, s2)
        for h in range(H):
            sl = slice(h * LANES, (h + 1) * LANES)
            kblk = kv[:, sl] + kt
            kf_ref[:, sl] = kblk.astype(BF16)
            kft_ref[sl, :] = kblk.T.astype(BF16)
        v_ref[...] = kv[:, H * LANES:].astype(BF16)
        for cblk in range(H * MLA_V // LANES):
            sl = slice(cblk * LANES, (cblk + 1) * LANES)
            vt_ref[sl, :] = kv[:, H * LANES + cblk * LANES:H * LANES + (cblk + 1) * LANES].T.astype(BF16)

    row = lambda i: (i, 0)
    col = lambda i: (0, i)
    fix = lambda i: (0, 0)
    return pl.pallas_call(
        body, name=name, grid=(S // tm,),
        in_specs=[_bs((tm, 384), row), _bs((tm, 128), lambda i: (i, TAIL0 // 128)), _bs((1, 256), fix), _bs((1, 128), fix),
                  _bs((256, 768), fix), _bs((128, 1152), fix),
                  _bs((tm, 128), row), _bs((tm, 128), row), _bs((tm, 128), row), _bs((tm, 128), row)],
        out_specs=[_bs((tm, 768), row), _bs((tm, 768), row), _bs((tm, 384), row), _bs((768, tm), col), _bs((384, tm), col)],
        out_shape=[jax.ShapeDtypeStruct((S, 768), BF16), jax.ShapeDtypeStruct((S, 768), BF16),
                   jax.ShapeDtypeStruct((S, 384), BF16), jax.ShapeDtypeStruct((768, S), BF16),
                   jax.ShapeDtypeStruct((384, S), BF16)],
        compiler_params=_params(("parallel",)),
    )(za, za, gq, gkv, wq, wkv, *tabs)


def _mla_prep_bwd(za, gq, gkv, wq, wkv, tabs, dqf, dkf, dvm, name):
    S = za.shape[0]
    tm = _tile(S, 512)
    H = MLA_HEADS

    def body(zq_ref, gq_ref, gkv_ref, wq_ref, wkv_ref, c_ref, ck_ref, s1_ref, s2_ref, dqf_ref, dkf_ref, dvm_ref,
             dza_ref, dtail_ref, dwq_ref, dwkv_ref, dgq_ref, dgkv_ref):
        i = pl.program_id(0)
        first = i == 0
        zq = zq_ref[...]
        c, s1, s2 = c_ref[...], s1_ref[...], s2_ref[...]
        lane = lax.broadcasted_iota(jnp.int32, (1, LANES), 1)
        nope = lane < MLA_NOPE
        rope = jnp.logical_and(lane >= ROPE_LANE0, lane < ROPE_LANE0 + MLA_ROPE)

        qa = zq[:, 0:256]
        qn = _rms(qa, gq_ref[...]).astype(BF16)
        dqf = dqf_ref[...]
        dq_pre = jnp.concatenate(
            [_rope_t(dqf[:, h * LANES:(h + 1) * LANES] * _MLA_SCALE, c, s1, s2) for h in range(H)], axis=1).astype(BF16)
        _accum_out(dwq_ref, first, _dot_tn(qn, dq_pre))
        dqa, dgq = _rms_bwd(_dot_nt(dq_pre, wq_ref[...]), qa, gq_ref[...])
        _accum_out(dgq_ref, first, dgq)

        kva = zq[:, 256:384]
        kvn = _rms(kva, gkv_ref[...]).astype(BF16)
        dkf = dkf_ref[...]
        parts = []
        dkt = jnp.zeros((tm, LANES), F32)
        for h in range(H):
            blk = dkf[:, h * LANES:(h + 1) * LANES]
            parts.append(jnp.where(nope, blk, 0.0))
            dkt = dkt + jnp.where(rope, blk, 0.0)
        dkv_pre = jnp.concatenate(parts + [dvm_ref[...]], axis=1).astype(BF16)
        _accum_out(dwkv_ref, first, _dot_tn(kvn, dkv_pre))
        dkva, dgkv = _rms_bwd(_dot_nt(dkv_pre, wkv_ref[...]), kva, gkv_ref[...])
        _accum_out(dgkv_ref, first, dgkv)

        dtail_ref[...] = _rope_t(dkt, ck_ref[...], s1, s2)
        dza_ref[...] = jnp.concatenate([dqa, dkva], axis=1).astype(BF16)

    row = lambda i: (i, 0)
    fix = lambda i: (0, 0)
    return pl.pallas_call(
        body, name=name, grid=(S // tm,),
        in_specs=[_bs((tm, 384), row), _bs((1, 256), fix), _bs((1, 128), fix), _bs((256, 768), fix), _bs((128, 1152), fix),
                  _bs((tm, 128), row), _bs((tm, 128), row), _bs((tm, 128), row), _bs((tm, 128), row),
                  _bs((tm, 768), row), _bs((tm, 768), row), _bs((tm, 384), row)],
        out_specs=[_bs((tm, 384), row), _bs((tm, LANES), row), _bs((256, 768), fix), _bs((128, 1152), fix),
                   _bs((1, 256), fix), _bs((1, 128), fix)],
        out_shape=[jax.ShapeDtypeStruct((S, 384), BF16), jax.ShapeDtypeStruct((S, LANES), F32),
                   jax.ShapeDtypeStruct((256, 768), F32),
                   jax.ShapeDtypeStruct((128, 1152), F32), jax.ShapeDtypeStruct((1, 256), F32),
                   jax.ShapeDtypeStruct((1, 128), F32)],
        compiler_params=_params(("arbitrary",)),
    )(za, gq, gkv, wq, wkv, *tabs, dqf, dkf, dvm)


def _head_views(qb, kb, r, dkb, sel):
    if dkb == LANES:
        sl = slice(r * LANES, (r + 1) * LANES)
        return qb[:, sl], kb[:, sl], kb[:, sl]
    return jnp.where(sel, qb, jnp.zeros_like(qb)), kb, jnp.where(sel, kb, jnp.zeros_like(kb))


def _attn_fwd_t(q_arr, k_arr, vt_arr, *, nhp, dkb, qoff, koff, vtoff, scale, cum, cumT, name, rider=None):
    S = q_arr.shape[0]
    T = _tile(S, ATTN_BLOCK)
    nq = S // T
    W = 2 * dkb
    bias = cum is not None

    def body(*refs):
        if bias:
            q_ref, k_ref, vt_ref, cq_ref, ck_ref, o_ref, lse_ref, m_s, l_s, acc_s = refs
        else:
            q_ref, k_ref, vt_ref, o_ref, lse_ref, m_s, l_s, acc_s = refs
        hp, qi, ki = pl.program_id(0), pl.program_id(1), pl.program_id(2)
        lo_lane = lax.broadcasted_iota(jnp.int32, (1, LANES), 1) < 64
        lo_row = lax.broadcasted_iota(jnp.int32, (LANES, 1), 0) < 64

        @pl.when(ki == 0)
        def _():
            m_s[...] = jnp.full_like(m_s, NEG)
            l_s[...] = jnp.zeros_like(l_s)
            acc_s[...] = jnp.zeros_like(acc_s)

        def step(masked):
            qb, kb, vtb = q_ref[...], k_ref[...], vt_ref[...]
            if masked:
                mask = lax.broadcasted_iota(jnp.int32, (T, T), 0) <= lax.broadcasted_iota(jnp.int32, (T, T), 1)
            if bias:
                li = lax.broadcasted_iota(jnp.int32, (T, LANES), 1)
                ckb = ck_ref[...]
            m_all, l_all = m_s[...], l_s[...]
            scores = []
            for r in range(2):
                sel = lo_lane if r == 0 else jnp.logical_not(lo_lane)
                q, k, _ = _head_views(qb, kb, r, dkb, sel)
                if scale != 1.0:
                    q = q * jnp.asarray(scale, q.dtype)
                scores.append(_dot_nt(k, q))
            m_out, l_out, alphas, pvs = [], [], [], []
            for r in range(2):
                rsel = lo_row if r == 0 else jnp.logical_not(lo_row)
                s = scores[r]
                if bias:
                    ck = jnp.sum(jnp.where(li == 8 * hp + r, ckb, 0.0), axis=1, keepdims=True)
                    s = s + (cq_ref[r:r + 1, :] - ck)
                if masked:
                    s = jnp.where(mask, s, NEG)
                m_prev = m_all[r:r + 1, :]
                m_new = jnp.maximum(m_prev, jnp.max(s, axis=0, keepdims=True))
                alpha = jnp.exp(m_prev - m_new)
                p = jnp.exp(s - m_new)
                m_out.append(m_new)
                l_out.append(alpha * l_all[r:r + 1, :] + jnp.sum(p, axis=0, keepdims=True))
                alphas.append(alpha)
                pvs.append(_dot(jnp.where(rsel, vtb, jnp.zeros_like(vtb)), p.astype(BF16)))
            m_s[0:1, :] = m_out[0]
            m_s[1:2, :] = m_out[1]
            l_s[0:1, :] = l_out[0]
            l_s[1:2, :] = l_out[1]
            acc_s[...] = acc_s[...] * jnp.where(lo_row, alphas[0], alphas[1]) + (pvs[0] + pvs[1])

        @pl.when(ki < qi)
        def _():
            step(False)

        @pl.when(ki == qi)
        def _():
            step(True)

        @pl.when(ki == nq - 1)
        def _():
            inv = jnp.where(lo_row, 1.0 / l_s[0:1, :], 1.0 / l_s[1:2, :])
            o_ref[...] = (acc_s[...] * inv).T.astype(BF16)
            used = lax.broadcasted_iota(jnp.int32, (8, T), 0) < 2
            lse_ref[...] = jnp.where(used, m_s[...] + jnp.log(jnp.where(used, l_s[...], 1.0)), 0.0)

    kmap = lambda hp, qi, ki: jnp.minimum(ki, qi)
    in_specs = [_bs((T, W), lambda hp, qi, ki: (qi, qoff + hp)),
                _bs((T, W), lambda hp, qi, ki: (kmap(hp, qi, ki), koff + hp)),
                _bs((LANES, T), lambda hp, qi, ki: (vtoff + hp, kmap(hp, qi, ki)))]
    args = [q_arr, k_arr, vt_arr]
    if bias:
        in_specs += [_bs((8, T), lambda hp, qi, ki: (hp, qi)), _bs((T, LANES), lambda hp, qi, ki: (kmap(hp, qi, ki), 0))]
        args += [cumT, cum]
    return _call(
        body, name=name, grid=(nhp, nq, nq),
        in_specs=in_specs,
        out_specs=[_bs((T, LANES), lambda hp, qi, ki: (qi, hp)), _bs((None, 8, T), lambda hp, qi, ki: (hp, 0, qi))],
        out_shape=[jax.ShapeDtypeStruct((S, nhp * LANES), BF16), jax.ShapeDtypeStruct((nhp, 8, S), F32)],
        scratch=[pltpu.VMEM((8, T), F32), pltpu.VMEM((8, T), F32), pltpu.VMEM((LANES, T), F32)],
        dims=("parallel", "parallel", "arbitrary"), args=args, rider=rider)


def _attn_bwd_t(q_arr, k_arr, kt_arr, v_arr, do_arr, lse, dvec, *, nhp, dkb, qoff, koff, ktoff, voff, scale, cum, cumT,
                name, rider=None, grad_dtype=F32):
    S = q_arr.shape[0]
    T = _tile(S, ATTN_BLOCK)
    nq = S // T
    W = 2 * dkb
    bias = cum is not None

    def body(*refs):
        if bias:
            (q_ref, k_ref, kt_ref, v_ref, do_ref, lse_ref, dvec_ref, cq_ref, ck_ref,
             dq_ref, dk_ref, dv_ref, dcq_ref, dck_ref, dqt_s, dk_s, dv_s, dcq_s, dck_s) = refs
        else:
            (q_ref, k_ref, kt_ref, v_ref, do_ref, lse_ref, dvec_ref,
             dq_ref, dk_ref, dv_ref, dqt_s, dk_s, dv_s) = refs
        hp, ki, qi = pl.program_id(0), pl.program_id(1), pl.program_id(2)
        lo_lane = lax.broadcasted_iota(jnp.int32, (1, LANES), 1) < 64
        lo_row = lax.broadcasted_iota(jnp.int32, (LANES, 1), 0) < 64

        @pl.when(jnp.logical_and(ki == 0, qi == 0))
        def _():
            dqt_s[...] = jnp.zeros_like(dqt_s)
            if bias:
                dcq_s[...] = jnp.zeros_like(dcq_s)

        @pl.when(qi == 0)
        def _():
            dk_s[...] = jnp.zeros_like(dk_s)
            dv_s[...] = jnp.zeros_like(dv_s)
            if bias:
                dck_s[...] = jnp.zeros_like(dck_s)

        def step(masked):
            qb, kb, ktb, vb, dob = q_ref[...], k_ref[...], kt_ref[...], v_ref[...], do_ref[...]
            if masked:
                mask = lax.broadcasted_iota(jnp.int32, (T, T), 0) <= lax.broadcasted_iota(jnp.int32, (T, T), 1)
            if bias:
                li = lax.broadcasted_iota(jnp.int32, (T, LANES), 1)
                ckb = ck_ref[...]
            for r in range(2):
                sel = lo_lane if r == 0 else jnp.logical_not(lo_lane)
                rsel = lo_row if r == 0 else jnp.logical_not(lo_row)
                q, k, _ = _head_views(qb, kb, r, dkb, sel)
                if scale != 1.0:
                    q = q * jnp.asarray(scale, q.dtype)
                s = _dot_nt(k, q)
                if bias:
                    ck = jnp.sum(jnp.where(li == 8 * hp + r, ckb, 0.0), axis=1, keepdims=True)
                    s = s + (cq_ref[r:r + 1, :] - ck)
                p = jnp.exp(s - lse_ref[r:r + 1, :])
                if masked:
                    p = jnp.where(mask, p, 0.0)
                do_r = jnp.where(sel, dob, jnp.zeros_like(dob))
                dp = _dot_nt(vb, do_r)
                ds = p * (dp - dvec_ref[r:r + 1, :])
                pb = p.astype(BF16)
                dsb = ds.astype(BF16)
                dv_s[...] += _dot(pb, do_r)
                if dkb == LANES:
                    sl = slice(r * LANES, (r + 1) * LANES)
                    dk_s[:, sl] += _dot(dsb, q)
                    dqt_s[qi, sl, :] += _dot(ktb[sl, :], dsb) * scale
                else:
                    dk_s[...] += _dot(dsb, q)
                    dqt_s[qi] += _dot(jnp.where(rsel, ktb, jnp.zeros_like(ktb)), dsb) * scale
                if bias:
                    dcq_s[qi, r:r + 1, :] += jnp.sum(ds, axis=0, keepdims=True)
                    dck_s[...] -= jnp.where(li == 8 * hp + r, jnp.sum(ds, axis=1, keepdims=True), 0.0)

        @pl.when(qi > ki)
        def _():
            step(False)

        @pl.when(qi == ki)
        def _():
            step(True)

        @pl.when(qi == nq - 1)
        def _():
            dk_ref[...] = dk_s[...].astype(grad_dtype)
            dv_ref[...] = dv_s[...].astype(grad_dtype)
            if bias:
                dck_ref[...] = dck_s[...]

        @pl.when(jnp.logical_and(ki == nq - 1, qi == nq - 1))
        def _():
            for c in range(nq):
                dq_ref[c * T:(c + 1) * T, :] = dqt_s[c].T.astype(grad_dtype)
                if bias:
                    dcq_ref[:, c * T:(c + 1) * T] = dcq_s[c]

    qmap = lambda hp, ki, qi: jnp.maximum(qi, ki)
    in_specs = [_bs((T, W), lambda hp, ki, qi: (qmap(hp, ki, qi), qoff + hp)),
                _bs((T, W), lambda hp, ki, qi: (ki, koff + hp)),
                _bs((W, T), lambda hp, ki, qi: (ktoff + hp, ki)),
                _bs((T, LANES), lambda hp, ki, qi: (ki, voff + hp)),
                _bs((T, LANES), lambda hp, ki, qi: (qmap(hp, ki, qi), hp)),
                _bs((None, 8, T), lambda hp, ki, qi: (hp, 0, qmap(hp, ki, qi))),
                _bs((None, 8, T), lambda hp, ki, qi: (hp, 0, qmap(hp, ki, qi)))]
    args = [q_arr, k_arr, kt_arr, v_arr, do_arr, lse, dvec]
    out_specs = [_bs((S, W), lambda hp, ki, qi: (0, hp)), _bs((T, W), lambda hp, ki, qi: (ki, hp)),
                 _bs((T, LANES), lambda hp, ki, qi: (ki, hp))]
    out_shape = [jax.ShapeDtypeStruct((S, nhp * W), grad_dtype), jax.ShapeDtypeStruct((S, nhp * W), grad_dtype),
                 jax.ShapeDtypeStruct((S, nhp * LANES), grad_dtype)]
    scratch = [pltpu.VMEM((nq, W, T), F32), pltpu.VMEM((T, W), F32), pltpu.VMEM((T, LANES), F32)]
    if bias:
        in_specs += [_bs((8, T), lambda hp, ki, qi: (hp, qmap(hp, ki, qi))), _bs((T, LANES), lambda hp, ki, qi: (ki, 0))]
        args += [cumT, cum]
        out_specs += [_bs((None, 8, S), lambda hp, ki, qi: (hp, 0, 0)), _bs((None, T, LANES), lambda hp, ki, qi: (hp, ki, 0))]
        out_shape += [jax.ShapeDtypeStruct((nhp, 8, S), F32), jax.ShapeDtypeStruct((nhp, S, LANES), F32)]
        scratch += [pltpu.VMEM((nq, 8, T), F32), pltpu.VMEM((T, LANES), F32)]
    return _call(body, name=name, grid=(nhp, nq, nq), in_specs=in_specs, out_specs=out_specs, out_shape=out_shape,
                 scratch=scratch, dims=("arbitrary", "arbitrary", "arbitrary"), args=args, rider=rider)


def _gate_lanes(shape):
    lane = lax.broadcasted_iota(jnp.int32, shape, 1)
    return jnp.logical_and(lane < 8 * (FOX_HEADS // 2), lane % 8 < 2)


def _fox_prep(za, b_row, name):
    S = za.shape[0]
    nrow = 8 * (FOX_HEADS // 2)

    def body(tail_ref, b_ref, cum_ref, cumt_ref):
        x = tail_ref[...] + b_ref[...]
        logf = jnp.minimum(x, 0.0) - jnp.log(1.0 + jnp.exp(-jnp.abs(x)))
        y = jnp.where(_gate_lanes((S, LANES)), logf, 0.0)
        row = lax.broadcasted_iota(jnp.int32, (S, LANES), 0)
        k = 1
        while k < S:
            y = y + jnp.where(row >= k, pltpu.roll(y, k, 0), 0.0)
            k *= 2
        cum_ref[...] = y
        cumt_ref[...] = y.T[0:nrow, :]

    return pl.pallas_call(
        body, name=name, grid=(1,),
        in_specs=[_bs((S, LANES), lambda i: (0, TAIL0 // LANES)), _bs((1, LANES), lambda i: (0, 0))],
        out_specs=[_bs((S, LANES), lambda i: (0, 0)), _bs((nrow, S), lambda i: (0, 0))],
        out_shape=[jax.ShapeDtypeStruct((S, LANES), F32), jax.ShapeDtypeStruct((nrow, S), F32)],
        compiler_params=_params(("arbitrary",)),
    )(za, b_row)


def _fox_prep_bwd(za, b_row, dcq, dck, name):
    S = za.shape[0]
    nhp = FOX_HEADS // 2
    nrow = 8 * nhp
    dcq2 = dcq.reshape(nrow, S)

    def body(tail_ref, b_ref, dcq_ref, dck_ref, dt_ref, db_ref):
        x = tail_ref[...] + b_ref[...]
        d = jnp.concatenate([dcq_ref[...], jnp.zeros((LANES - nrow, S), F32)], axis=0).T
        for hp in range(nhp):
            d = d + dck_ref[hp]
        row = lax.broadcasted_iota(jnp.int32, (S, LANES), 0)
        k = 1
        while k < S:
            d = d + jnp.where(row < S - k, pltpu.roll(d, S - k, 0), 0.0)
            k *= 2
        df = jnp.where(_gate_lanes((S, LANES)), d * jax.nn.sigmoid(-x), 0.0)
        dt_ref[...] = df
        db_ref[...] = jnp.sum(df, axis=0, keepdims=True)

    return pl.pallas_call(
        body, name=name, grid=(1,),
        in_specs=[_bs((S, LANES), lambda i: (0, TAIL0 // LANES)), _bs((1, LANES), lambda i: (0, 0)),
                  _bs((nrow, S), lambda i: (0, 0)), _bs((nhp, S, LANES), lambda i: (0, 0, 0))],
        out_specs=[_bs((S, LANES), lambda i: (0, 0)), _bs((1, LANES), lambda i: (0, 0))],
        out_shape=[jax.ShapeDtypeStruct((S, LANES), F32), jax.ShapeDtypeStruct((1, LANES), F32)],
        compiler_params=_params(("arbitrary",)),
    )(za, b_row, dcq2, dck)


def _pool_select(half, lane_lo, vals):
    return jnp.where(lane_lo, jnp.where(half == 0, vals[0], vals[2]), jnp.where(half == 0, vals[1], vals[3]))


def _pool_den(S, half, lane_lo):
    cnt = (lax.broadcasted_iota(jnp.int32, (S, LANES), 0) + 1).astype(F32)
    w = _pool_select(half, lane_lo, [float(x) for x in POOL_WINDOWS])
    return jnp.minimum(cnt, w)


def _pool_fwd(za, wbd, scale, name):
    S = za.shape[0]

    def body(u_ref, w_ref, sc_ref, y_ref, pd_ref):
        half = pl.program_id(0)
        u = u_ref[...]
        row = lax.broadcasted_iota(jnp.int32, (S, LANES), 0)
        lane_lo = lax.broadcasted_iota(jnp.int32, (S, LANES), 1) < POOL_GROUP
        sums = []
        acc = u
        k = 1
        while k < POOL_WINDOWS[-1]:
            acc = acc + jnp.where(row >= k, pltpu.roll(acc, k, 0), 0.0)
            sums.append(acc)
            k *= 2
        pooled = _pool_select(half, lane_lo, sums) / _pool_den(S, half, lane_lo)
        pd = (pooled - u).astype(BF16)
        pd_ref[...] = pd
        y_ref[...] = (_dot(pd, w_ref[...]) * sc_ref[...]).astype(BF16)

    return pl.pallas_call(
        body, name=name, grid=(2,),
        in_specs=[_bs((S, LANES), lambda i: (0, 384 // LANES + i)), _bs((None, LANES, LANES), lambda i: (i, 0, 0)),
                  _bs((1, LANES), lambda i: (0, i))],
        out_specs=[_bs((S, LANES), lambda i: (0, i)), _bs((S, LANES), lambda i: (0, i))],
        out_shape=[jax.ShapeDtypeStruct((S, POOL_WIDTH), BF16), jax.ShapeDtypeStruct((S, POOL_WIDTH), BF16)],
        compiler_params=_params(("parallel",)),
    )(za, wbd, scale)


def _pool_bwd(dyb, pd, wbd, scale, name):
    S = pd.shape[0]

    def body(dy_ref, pd_ref, w_ref, sc_ref, du_ref, dw_ref, dsc_ref):
        half = pl.program_id(0)
        dy = dy_ref[...]
        pd = pd_ref[...]
        w = w_ref[...]
        ypre = _dot(pd, w)
        dsc_ref[...] = jnp.sum(dy * ypre, axis=0, keepdims=True)
        dyp = (dy * sc_ref[...]).astype(BF16)
        dw_ref[...] = _dot_tn(pd, dyp)
        dpd = _dot_nt(dyp, w)
        row = lax.broadcasted_iota(jnp.int32, (S, LANES), 0)
        lane_lo = lax.broadcasted_iota(jnp.int32, (S, LANES), 1) < POOL_GROUP
        acc = dpd / _pool_den(S, half, lane_lo)
        sums = []
        k = 1
        while k < POOL_WINDOWS[-1]:
            acc = acc + jnp.where(row < S - k, pltpu.roll(acc, S - k, 0), 0.0)
            sums.append(acc)
            k *= 2
        du_ref[...] = (_pool_select(half, lane_lo, sums) - dpd).astype(BF16)

    return pl.pallas_call(
        body, name=name, grid=(2,),
        in_specs=[_bs((S, LANES), lambda i: (0, i)), _bs((S, LANES), lambda i: (0, i)),
                  _bs((None, LANES, LANES), lambda i: (i, 0, 0)), _bs((1, LANES), lambda i: (0, i))],
        out_specs=[_bs((S, LANES), lambda i: (0, i)), _bs((None, LANES, LANES), lambda i: (i, 0, 0)),
                   _bs((1, LANES), lambda i: (0, i))],
        out_shape=[jax.ShapeDtypeStruct((S, POOL_WIDTH), BF16), jax.ShapeDtypeStruct((2, LANES, LANES), F32),
                   jax.ShapeDtypeStruct((1, POOL_WIDTH), F32)],
        compiler_params=_params(("parallel",)),
    )(dyb, pd, wbd, scale)


def _mix_out_fwd(x, ya, yb, yc, w_out, name):
    S, D = x.shape
    tm = _tile(S, 512)
    K = w_out.shape[0]

    def body(x_ref, ya_ref, yb_ref, yc_ref, w_ref, xo_ref, yc_out):
        ycat = jnp.concatenate([ya_ref[...], yb_ref[...], yc_ref[...]], axis=1)
        yc_out[...] = ycat
        xo_ref[...] = x_ref[...] + _dot(ycat, w_ref[...])

    row = lambda i: (i, 0)
    return pl.pallas_call(
        body, name=name, grid=(S // tm,),
        in_specs=[_bs((tm, D), row), _bs((tm, 384), row), _bs((tm, 256), row), _bs((tm, 384), row),
                  _bs((K, D), lambda i: (0, 0))],
        out_specs=[_bs((tm, D), row), _bs((tm, K), row)],
        out_shape=[jax.ShapeDtypeStruct((S, D), F32), jax.ShapeDtypeStruct((S, K), BF16)],
        compiler_params=_params(("parallel",)),
    )(x, ya, yb, yc, w_out)


def _mix_out_bwd(dy, w_out, ya, yc, name):
    S, D = dy.shape
    tm = _tile(S, 512)
    K = w_out.shape[0]
    nhp = ya.shape[1] // LANES

    def body(dy_ref, w_ref, ya_ref, yc_ref, da_ref, db_ref, dc_ref, dva_ref, dvc_ref):
        d = _dot_nt(dy_ref[...].astype(BF16), w_ref[...])
        da = d[:, 0:384].astype(BF16)
        dc = d[:, 640:1024].astype(BF16)
        da_ref[...] = da
        db_ref[...] = d[:, 384:640]
        dc_ref[...] = dc
        li = lax.broadcasted_iota(jnp.int32, (tm, LANES), 1)
        for do, o_ref, out_ref in ((da, ya_ref, dva_ref), (dc, yc_ref, dvc_ref)):
            for hp in range(nhp):
                sl = slice(hp * LANES, (hp + 1) * LANES)
                prod = do[:, sl].astype(F32) * o_ref[:, sl].astype(F32)
                d0 = jnp.sum(jnp.where(li < 64, prod, 0.0), axis=1, keepdims=True)
                d1 = jnp.sum(jnp.where(li >= 64, prod, 0.0), axis=1, keepdims=True)
                out_ref[hp] = jnp.where(li == 0, d0, jnp.where(li == 1, d1, 0.0)).T[0:8, :]

    row = lambda i: (i, 0)
    dv_spec = _bs((nhp, 8, tm), lambda i: (0, 0, i))
    dv_shape = jax.ShapeDtypeStruct((nhp, 8, S), F32)
    return pl.pallas_call(
        body, name=name, grid=(S // tm,),
        in_specs=[_bs((tm, D), row), _bs((K, D), lambda i: (0, 0)), _bs((tm, 384), row), _bs((tm, 384), row)],
        out_specs=[_bs((tm, 384), row), _bs((tm, 256), row), _bs((tm, 384), row), dv_spec, dv_spec],
        out_shape=[jax.ShapeDtypeStruct((S, 384), BF16), jax.ShapeDtypeStruct((S, 256), F32),
                   jax.ShapeDtypeStruct((S, 384), BF16), dv_shape, dv_shape],
        compiler_params=_params(("parallel",)),
    )(dy, w_out, ya, yc)


def _loss_head(x, gam, target, name):
    S, D = x.shape
    tm = _tile(S, 512)

    def body(x_ref, gam_ref, t_ref, dx_ref, dgam_ref, loss_ref):
        i = pl.program_id(0)
        xv = x_ref[...]
        err = _rms(xv, gam_ref[...]) - t_ref[...]
        part = 0.5 * jnp.sum(jnp.mean(err * err, axis=-1, keepdims=True), axis=0, keepdims=True)
        dxn, dgam = _rms_bwd(err * (1.0 / D), xv, gam_ref[...])
        dx_ref[...] = dxn
        _accum_out(dgam_ref, i == 0, dgam)
        _accum_out(loss_ref, i == 0, jnp.broadcast_to(part, (1, LANES)))

    row = lambda i: (i, 0)
    fix = lambda i: (0, 0)
    return pl.pallas_call(
        body, name=name, grid=(S // tm,),
        in_specs=[_bs((tm, D), row), _bs((1, D), fix), _bs((tm, D), row)],
        out_specs=[_bs((tm, D), row), _bs((1, D), fix), _bs((1, LANES), fix)],
        out_shape=[jax.ShapeDtypeStruct((S, D), F32), jax.ShapeDtypeStruct((1, D), F32),
                   jax.ShapeDtypeStruct((1, LANES), F32)],
        compiler_params=_params(("arbitrary",)),
    )(x, gam, target)


def _adam_math(g, w, m, v):
    m = ADAM_B1 * m + (1.0 - ADAM_B1) * g
    v = ADAM_B2 * v + (1.0 - ADAM_B2) * (g * g)
    m_hat = m / (1.0 - ADAM_B1 ** ADAM_STEP)
    v_hat = v / (1.0 - ADAM_B2 ** ADAM_STEP)
    delta = -ADAM_LR * (m_hat / (jnp.sqrt(v_hat) + ADAM_EPS) + ADAM_WD * w)
    return delta, m, v


def _adam_sum(recv, w, m, v, layer, prev, tr, name):
    L, R, C = w.shape
    Cp = recv.shape[2]
    tr = _tile(R, tr)

    def body(r_ref, w_ref, m_ref, v_ref, *rest):
        g_out, d_out, m_out, v_out = rest[len(rest) - 4:]
        g = r_ref[0, :, 0:C].astype(F32)
        for p in range(1, N_DEV):
            g = g + r_ref[p, :, 0:C].astype(F32)
        delta, mn, vn = _adam_math(g, w_ref[...], m_ref[...], v_ref[...])
        g_out[...] = g
        d_out[...] = delta
        m_out[...] = mn
        v_out[...] = vn

    blk = _bs((None, tr, C), lambda i: (layer, i, 0))
    shp = jax.ShapeDtypeStruct((L, R, C), F32)
    in_specs = [_bs((N_DEV, tr, Cp), lambda i: (0, i, 0)), blk, blk, blk]
    args = [recv, w, m, v]
    aliases = {}
    if prev is not None:
        in_specs += [HBM_SPEC] * 4
        args += list(prev)
        aliases = {4 + k: k for k in range(4)}
    return pl.pallas_call(
        body, name=name, grid=(R // tr,),
        in_specs=in_specs, out_specs=[blk, blk, blk, blk], out_shape=[shp, shp, shp, shp],
        input_output_aliases=aliases, compiler_params=_params(("parallel",)),
    )(*args)


def _dev_index(px, py, pc):
    return 4 * px + 2 * py + pc


class _GatherRider:
    def __init__(self, shards, out_shapes, views):
        self.n = len(shards)
        self.views = views
        self.srcs = list(shards)
        self.out_shapes = list(out_shapes)
        n = self.n
        self.scratch = [pltpu.SemaphoreType.DMA((n, 7)), pltpu.SemaphoreType.DMA((n, 7)), pltpu.SemaphoreType.DMA((n,))]

    def _copies(self, ins, outs, sems):
        n = self.n
        send_sems, recv_sems, local_sems = sems
        x, y, c = lax.axis_index("x"), lax.axis_index("y"), lax.axis_index("c")
        me, sibling = (x, y, c), (x, y, 1 - c)
        chips = [(1 - x, y), (x, 1 - y), (1 - x, 1 - y)]

        def rows(a, blk):
            return self.views[a](outs[a], _dev_index(*blk))

        def copy(a, k, blk, to, src=None):
            return pltpu.make_async_remote_copy(
                src_ref=rows(a, blk) if src is None else src, dst_ref=rows(a, blk),
                send_sem=send_sems.at[a, k], recv_sem=recv_sems.at[a, k], device_id=to, device_id_type=MESH_ID)

        local = [pltpu.make_async_copy(ins[a], rows(a, me), local_sems.at[a]) for a in range(n)]
        first = []
        for a in range(n):
            first.append(copy(a, 0, me, sibling, src=ins[a]))
            first += [copy(a, 1 + j, me, (*chip, c), src=ins[a]) for j, chip in enumerate(chips)]
        over_ici = [[copy(a, 1 + j, (*chip, c), me) for a in range(n)] for j, chip in enumerate(chips)]
        passed = [[copy(a, 4 + j, (*chip, c), sibling) for a in range(n)] for j, chip in enumerate(chips)]
        from_sibling = [copy(a, 0, sibling, me) for a in range(n)]
        from_sibling += [copy(a, 4 + j, (*chip, 1 - c), me) for a in range(n) for j, chip in enumerate(chips)]
        return local, first, over_ici, passed, from_sibling

    def begin(self, ins, outs, sems):
        local, first, _, _, _ = self._copies(ins, outs, sems)
        for cp in local + first:
            cp.start()

    def middle(self, ins, outs, sems):
        _, _, over_ici, passed, _ = self._copies(ins, outs, sems)
        for arrived, onward in zip(over_ici, passed):
            for cp, fwd in zip(arrived, onward):
                cp.wait_recv()
                fwd.start()

    def end(self, ins, outs, sems):
        local, first, _, passed, from_sibling = self._copies(ins, outs, sems)
        for cp in from_sibling:
            cp.wait_recv()
        for cp in first + [fwd for onward in passed for fwd in onward]:
            cp.wait_send()
        for cp in local:
            cp.wait()


class _ScatterRider:
    _MASKS = [(kx, ky, kc) for kx in (0, 1) for ky in (0, 1) for kc in (0, 1)][1:]

    def __init__(self, srcs, out_shapes, src_of, dst_at):
        self.n = len(srcs)
        self.srcs = list(srcs)
        self.out_shapes = list(out_shapes)
        self.src_of = src_of
        self.dst_at = dst_at
        n = self.n
        self.scratch = [pltpu.SemaphoreType.DMA((n, 7)), pltpu.SemaphoreType.DMA((n, 7)), pltpu.SemaphoreType.DMA((n,))]

    def _copies(self, ins, outs, sems):
        send_sems, recv_sems, local_sems = sems
        x, y, c = lax.axis_index("x"), lax.axis_index("y"), lax.axis_index("c")
        my = _dev_index(x, y, c)
        peers = [(1 - x if kx else x, 1 - y if ky else y, 1 - c if kc else c) for kx, ky, kc in self._MASKS]

        def send(i, k, to):
            return pltpu.make_async_remote_copy(
                src_ref=self.src_of[i](ins[i], _dev_index(*to)), dst_ref=self.dst_at[i](outs[i], my),
                send_sem=send_sems.at[i, k], recv_sem=recv_sems.at[i, k], device_id=to, device_id_type=MESH_ID)

        def arrival(i, k, frm):
            slot = self.dst_at[i](outs[i], _dev_index(*frm))
            return pltpu.make_async_remote_copy(
                src_ref=slot, dst_ref=slot, send_sem=send_sems.at[i, k], recv_sem=recv_sems.at[i, k],
                device_id=frm, device_id_type=MESH_ID)

        local = [pltpu.make_async_copy(self.src_of[i](ins[i], my), self.dst_at[i](outs[i], my), local_sems.at[i])
                 for i in range(self.n)]
        sends = [send(i, k, to) for k, to in enumerate(peers) for i in range(self.n)]
        arrivals = [arrival(i, k, frm) for k, frm in enumerate(peers) for i in range(self.n)]
        return local, sends, arrivals

    def begin(self, ins, outs, sems):
        local, sends, _ = self._copies(ins, outs, sems)
        for cp in local + sends:
            cp.start()

    def middle(self, ins, outs, sems):
        pass

    def end(self, ins, outs, sems):
        local, sends, arrivals = self._copies(ins, outs, sems)
        for cp in arrivals:
            cp.wait_recv()
        for cp in sends:
            cp.wait_send()
        for cp in local:
            cp.wait()


def _comm_call(rider, name):
    k_in = len(rider.srcs)
    k_out = len(rider.out_shapes)

    def body(*refs):
        ins, outs, sems = refs[:k_in], refs[k_in:k_in + k_out], refs[k_in + k_out:]
        rider.begin(ins, outs, sems)
        rider.middle(ins, outs, sems)
        rider.end(ins, outs, sems)

    return pl.pallas_call(
        body, name=name, in_specs=[HBM_SPEC] * k_in, out_specs=[HBM_SPEC] * k_out, out_shape=rider.out_shapes,
        scratch_shapes=rider.scratch, compiler_params=pltpu.CompilerParams(has_side_effects=True),
    )(*rider.srcs)


def _pad_w_in(w):
    cols = lambda a, b: w[..., a:b]
    zeros = lambda n: jnp.zeros(w.shape[:-1] + (n,), w.dtype)
    pairs = FOX_HEADS // 2
    parts = [cols(0, 384), cols(416, 672)]
    for hp in range(pairs):
        parts += [cols(1824 + 2 * hp, 1826 + 2 * hp), zeros(6)]
    parts += [zeros(ROPE_LANE0 - 8 * pairs), cols(384, 416), zeros(LANES - ROPE_LANE0 - MLA_ROPE), cols(672, 1824)]
    return jnp.concatenate(parts, axis=-1)


def _unpad_w_in(g):
    cols = lambda a, b: g[..., a:b]
    rope0 = TAIL0 + ROPE_LANE0
    parts = [cols(0, 384), cols(rope0, rope0 + MLA_ROPE), cols(384, 640), cols(ZA, N_PAD)]
    parts += [cols(TAIL0 + 8 * hp, TAIL0 + 8 * hp + 2) for hp in range(FOX_HEADS // 2)]
    return jnp.concatenate(parts, axis=-1)


def _small_pack(w_q_b, w_kv_b):
    a = jnp.pad(w_q_b, ((0, 0), (0, 0), (0, LANES - w_q_b.shape[2])))
    b = jnp.pad(w_kv_b, ((0, 0), (0, 0), (0, LANES - w_kv_b.shape[2])))
    return jnp.concatenate([a, b], axis=1)


def _small_unpack(p, cq, ckv):
    return p[:, 0:MLA_Q_RANK, 0:cq], p[:, MLA_Q_RANK:, 0:ckv]


def _mla_weights(wsm):
    H = MLA_HEADS
    cq = H * (MLA_NOPE + MLA_ROPE) // N_DEV
    ckv = H * (MLA_NOPE + MLA_V) // N_DEV
    wq = wsm[:, 0:MLA_Q_RANK, 0:cq].transpose(1, 0, 2).reshape(MLA_Q_RANK, H, MLA_NOPE + MLA_ROPE)
    wq = jnp.pad(wq, ((0, 0), (0, 0), (0, HEAD_BLOCK - MLA_NOPE - MLA_ROPE))).reshape(MLA_Q_RANK, H * HEAD_BLOCK)
    wkv = wsm[:, MLA_Q_RANK:, 0:ckv].transpose(1, 0, 2).reshape(MLA_KV_RANK, H, MLA_NOPE + MLA_V)
    wk = jnp.pad(wkv[:, :, 0:MLA_NOPE], ((0, 0), (0, 0), (0, HEAD_BLOCK - MLA_NOPE))).reshape(MLA_KV_RANK, H * HEAD_BLOCK)
    wv = wkv[:, :, MLA_NOPE:].reshape(MLA_KV_RANK, H * MLA_V)
    return wq, jnp.concatenate([wk, wv], axis=1)


def _mla_grads_to_blocks(dwq, dwkv):
    H = MLA_HEADS
    gq = dwq.reshape(MLA_Q_RANK, H, HEAD_BLOCK)[:, :, 0:MLA_NOPE + MLA_ROPE].reshape(MLA_Q_RANK, N_DEV, -1)
    gk = dwkv[:, 0:H * HEAD_BLOCK].reshape(MLA_KV_RANK, H, HEAD_BLOCK)[:, :, 0:MLA_NOPE]
    gv = dwkv[:, H * HEAD_BLOCK:].reshape(MLA_KV_RANK, H, MLA_V)
    gkv = jnp.concatenate([gk, gv], axis=2).reshape(MLA_KV_RANK, N_DEV, -1)
    return _small_pack(gq.transpose(1, 0, 2), gkv.transpose(1, 0, 2)).astype(BF16)


def _pool_blockdiag(pool_w):
    z = jnp.zeros((POOL_GROUP, POOL_GROUP), pool_w.dtype)
    halves = [jnp.concatenate([jnp.concatenate([pool_w[2 * i], z], axis=1),
                               jnp.concatenate([z, pool_w[2 * i + 1]], axis=1)], axis=0) for i in range(2)]
    return jnp.stack(halves)


def _pool_blockdiag_t(dw):
    g = POOL_GROUP
    return jnp.stack([dw[0, 0:g, 0:g], dw[0, g:, g:], dw[1, 0:g, 0:g], dw[1, g:, g:]])


def _gate_row(b):
    return jnp.zeros((LANES,), b.dtype).at[_F_LANES].set(b).reshape(1, LANES)


_SMALL = ("ffn1_norm", "mix_norm", "q_a_norm", "kv_a_norm", "pool_w", "pool_scale", "fox_b_f", "ffn2_norm", "final_norm")


def _pack_small(tree):
    rows, recipe = [], []
    for name in _SMALL:
        a = tree[name]
        flat = a.reshape(-1)
        n = flat.shape[0]
        nrow = -(-n // (8 * LANES)) * 8
        flat = jnp.pad(flat, (0, nrow * LANES - n))
        rows.append(flat.reshape(nrow, LANES))
        recipe.append((name, a.shape, n, nrow))
    return jnp.concatenate(rows, axis=0), recipe


def _unpack_small(packed, recipe):
    out, r0 = {}, 0
    for name, shape, n, nrow in recipe:
        out[name] = packed[r0:r0 + nrow].reshape(-1)[0:n].reshape(shape)
        r0 += nrow
    return out


def _adam_small(packs, w, m, v, name):
    R = w.shape[0]

    def body(p_ref, w_ref, m_ref, v_ref, g_out, d_out, m_out, v_out):
        g = p_ref[0]
        for p in range(1, N_DEV):
            g = g + p_ref[p]
        delta, mn, vn = _adam_math(g, w_ref[...], m_ref[...], v_ref[...])
        g_out[...] = g
        d_out[...] = delta
        m_out[...] = mn
        v_out[...] = vn

    blk = _bs((R, LANES), lambda i: (0, 0))
    shp = jax.ShapeDtypeStruct((R, LANES), F32)
    return pl.pallas_call(
        body, name=name, grid=(1,),
        in_specs=[_bs((N_DEV, R, LANES), lambda i: (0, 0, 0)), blk, blk, blk],
        out_specs=[blk, blk, blk, blk], out_shape=[shp, shp, shp, shp],
        compiler_params=_params(("arbitrary",)),
    )(packs, w, m, v)


_GATHER_PLAN = {
    ("first", 0): (("ffn1_w_gu", 0), ("ffn1_w_down", 0)),
    ("ffn1_fwd", 0): (("w_in", 0), ("w_small", 0), ("w_out", 0), ("ffn2_w_down", 0)),
    ("mla_attn_fwd", 0): (("ffn2_w_gu", 0),),
    ("fox_attn_fwd", 0): (("ffn1_w_down", 1), ("w_in", 1), ("w_small", 1), ("w_out", 1)),
    ("ffn2_fwd", 0): (("ffn1_w_gu", 1),),
    ("mla_attn_fwd", 1): (("ffn2_w_gu", 1), ("ffn2_w_down", 1)),
}
_SCATTER_PLAN = {
    ("mla_attn_bwd", 1): (("ffn2_w_gu", 1), ("w_out", 1)),
    ("fox_attn_bwd", 1): (("ffn2_w_down", 1),),
    ("ffn1_bwd", 1): (("w_in", 1), ("w_small", 1)),
    ("ffn2_bwd", 0): (("ffn1_w_down", 1),),
    ("mla_attn_bwd", 0): (("ffn1_w_gu", 1), ("w_out", 0)),
    ("fox_attn_bwd", 0): (("ffn2_w_gu", 0), ("ffn2_w_down", 0)),
    ("ffn1_bwd_a", 0): (("w_in", 0), ("w_small", 0)),
    ("ffn1_dwgu", 0): (("ffn1_w_down", 0),),
    ("ffn1_bwd_b", 0): (("ffn1_w_gu", 0),),
}
_SPLIT_BWD = (("ffn1", 0),)


class _Exchange:
    def __init__(self, shards, D, f_sh, r_in, r_out):
        self.shards = shards
        self.D, self.f_sh, self.r_in, self.r_out = D, f_sh, r_in, r_out
        self.weights, self.grads, self.recv = {}, {}, {}

    def _rows(self, kind):
        n = {"w_gu": 2 * self.f_sh, "w_down": self.f_sh}[kind]
        return lambda ref, p: ref.at[pl.ds(pl.multiple_of(p * n, 16), n)]

    def _gathered_shape(self, kind):
        D, f_sh = self.D, self.f_sh
        return {"w_gu": (N_DEV * 2 * f_sh, D), "w_down": (N_DEV * f_sh, D), "w_in": (N_DEV, self.r_in, N_PAD),
                "w_small": (N_DEV, MLA_Q_RANK + MLA_KV_RANK, LANES), "w_out": (N_DEV, self.r_out, D)}[kind]

    def _recv_shape(self, kind):
        D, f_sh = self.D, self.f_sh
        return {"w_gu": (N_DEV, 2 * f_sh, D), "w_down": (N_DEV, f_sh, D), "w_in": (N_DEV, self.r_in, N_IN),
                "w_small": (N_DEV, MLA_Q_RANK + MLA_KV_RANK, LANES), "w_out": (N_DEV, self.r_out, D)}[kind]

    @staticmethod
    def _kind(name):
        return name[5:] if name.startswith("ffn") else name

    def gather_rider(self, call, l):
        keys = _GATHER_PLAN.get((call, l))
        if not keys:
            return None
        by_dev = lambda ref, p: ref.at[p]
        shards, shapes, views = [], [], []
        for key in keys:
            kind = self._kind(key[0])
            shards.append(self.shards[key])
            shapes.append(jax.ShapeDtypeStruct(self._gathered_shape(kind), BF16))
            views.append(self._rows(kind) if kind in ("w_gu", "w_down") else by_dev)
        return _GatherRider(shards, shapes, views)

    def gathered(self, call, l, outs):
        for key, w in zip(_GATHER_PLAN.get((call, l), ()), outs):
            if self._kind(key[0]) in ("w_in", "w_out"):
                w = w.reshape((N_DEV * w.shape[1],) + w.shape[2:])
            self.weights[key] = w

    def scatter_rider(self, call, l, pack=None):
        keys = _SCATTER_PLAN.get((call, l), ())
        if not keys and pack is None:
            return None
        by_dev = lambda ref, p: ref.at[p]
        srcs, shapes, src_of = [], [], []
        for key in keys:
            kind = self._kind(key[0])
            srcs.append(self.grads[key])
            shapes.append(jax.ShapeDtypeStruct(self._recv_shape(kind), BF16))
            src_of.append(self._rows(kind) if kind in ("w_gu", "w_down") else by_dev)
        if pack is not None:
            srcs.append(pack)
            shapes.append(jax.ShapeDtypeStruct((N_DEV,) + pack.shape, pack.dtype))
            src_of.append(lambda ref, p: ref)
        return _ScatterRider(srcs, shapes, src_of, [by_dev] * len(srcs))

    def scattered(self, call, l, outs):
        for key, r in zip(_SCATTER_PLAN.get((call, l), ()), outs):
            self.recv[key] = r


def _local_step(x, target, ex, small):
    S, D = x.shape
    tabs = _rope_tables(S)
    nhp_a, nhp_c = MLA_HEADS // 2, FOX_HEADS // 2
    fox_scale = 1.0 / math.sqrt(FOX_HEAD_DIM)
    ex.gathered("first", 0, _comm_call(ex.gather_rider("first", 0), "gather_first"))
    saved = []
    for l in range(DEPTH):
        s = {}
        s["x0"] = x
        wgu1, wd1 = ex.weights[("ffn1_w_gu", l)], ex.weights[("ffn1_w_down", l)]
        (x1, s["h1"], s["gu1"]), got = _ffn_fwd_full(x, small["ffn1_norm"][l][None], wgu1, wd1, FFN_FWD_TOKENS,
                                                    f"ffn1_fwd_l{l}", rider=ex.gather_rider("ffn1_fwd", l))
        ex.gathered("ffn1_fwd", l, got)
        s["x1"] = x1
        w_in = ex.weights[("w_in", l)]
        s["h2"], za, zf, zkvt = _mix_in_fwd(x1, small["mix_norm"][l][None], w_in, f"mix_in_fwd_l{l}")
        s["za"], s["zf"], s["zkvt"] = za, zf, zkvt
        wq, wkv = _mla_weights(ex.weights[("w_small", l)])
        s["wq"], s["wkv"] = wq, wkv
        gq, gkv = small["q_a_norm"][l][None], small["kv_a_norm"][l][None]
        qf, kf, vm, kft, vmt = _mla_prep(za, gq, gkv, wq, wkv, tabs, f"mla_prep_l{l}")
        s["qf"], s["kf"], s["vm"], s["kft"] = qf, kf, vm, kft
        (ya, lse_a), got = _attn_fwd_t(qf, kf, vmt, nhp=nhp_a, dkb=LANES, qoff=0, koff=0, vtoff=0, scale=1.0,
                                       cum=None, cumT=None, name=f"mla_attn_fwd_l{l}",
                                       rider=ex.gather_rider("mla_attn_fwd", l))
        ex.gathered("mla_attn_fwd", l, got)
        s["ya"], s["lse_a"] = ya, lse_a
        b_row = _gate_row(small["fox_b_f"][l])
        s["b_row"] = b_row
        cum, cumT = _fox_prep(za, b_row, f"fox_prep_l{l}")
        s["cum"], s["cumT"] = cum, cumT
        (yc, lse_c), got = _attn_fwd_t(zf, zf, zkvt, nhp=nhp_c, dkb=64, qoff=0, koff=nhp_c, vtoff=nhp_c, scale=fox_scale,
                                       cum=cum, cumT=cumT, name=f"fox_attn_fwd_l{l}",
                                       rider=ex.gather_rider("fox_attn_fwd", l))
        ex.gathered("fox_attn_fwd", l, got)
        s["yc"], s["lse_c"] = yc, lse_c
        wbd = _pool_blockdiag(small["pool_w"][l]).astype(BF16)
        s["wbd"] = wbd
        psc = small["pool_scale"][l][None]
        yb, s["pd"] = _pool_fwd(za, wbd, psc, f"pool_fwd_l{l}")
        w_out = ex.weights[("w_out", l)]
        x2, s["ycat"] = _mix_out_fwd(x1, ya, yb, yc, w_out, f"mix_out_fwd_l{l}")
        s["x2"] = x2
        wgu2, wd2 = ex.weights[("ffn2_w_gu", l)], ex.weights[("ffn2_w_down", l)]
        (x, s["h3"], s["gu2"]), got = _ffn_fwd_full(x2, small["ffn2_norm"][l][None], wgu2, wd2, FFN_FWD_TOKENS,
                                                    f"ffn2_fwd_l{l}", rider=ex.gather_rider("ffn2_fwd", l))
        ex.gathered("ffn2_fwd", l, got)
        saved.append(s)

    dx, d_final, loss = _loss_head(x, small["final_norm"][None], target, "loss_head")

    small_grads = [None] * DEPTH
    for l in reversed(range(DEPTH)):
        s = saved[l]
        g = {}
        wgu2, wd2 = ex.weights[("ffn2_w_gu", l)], ex.weights[("ffn2_w_down", l)]
        dy3 = dx
        (dx, g["ffn2_norm"], dgu, act, dyh), got = _ffn_bwd_full(
            dy3, s["x2"], small["ffn2_norm"][l][None], s["gu2"], wgu2, wd2, FFN_BWD_TOKENS, f"ffn2_bwd_l{l}",
            rider=ex.scatter_rider("ffn2_bwd", l))
        ex.scattered("ffn2_bwd", l, got)
        ex.grads[("ffn2_w_gu", l)] = _ffn_weight_grad(dgu, s["h3"], f"ffn2_dwgu_l{l}")
        ex.grads[("ffn2_w_down", l)] = _ffn_weight_grad(act, dyh, f"ffn2_dwd_l{l}")

        w_out = ex.weights[("w_out", l)]
        dya, dyb, dyc, dvec_a, dvec_c = _mix_out_bwd(dx, w_out, s["ya"], s["yc"], f"mix_out_bwd_l{l}")
        dw_out = _mm_tn(s["ycat"][None], dx[None], 1, lambda p: 0, lambda p: 0, 1024, 1024, f"dwout_l{l}", ts=1024)[0]
        ex.grads[("w_out", l)] = dw_out.reshape(N_DEV, ex.r_out, D)

        (dqf, dkf, dvm), got = _attn_bwd_t(s["qf"], s["kf"], s["kft"], s["vm"], dya, s["lse_a"], dvec_a, nhp=nhp_a,
                                           dkb=LANES, qoff=0, koff=0, ktoff=0, voff=0, scale=1.0, cum=None, cumT=None,
                                           name=f"mla_attn_bwd_l{l}", rider=ex.scatter_rider("mla_attn_bwd", l))
        ex.scattered("mla_attn_bwd", l, got)
        zf = s["zf"]
        (dqc, dkc, dvc, dcq, dck), got = _attn_bwd_t(zf, zf, s["zkvt"], zf, dyc, s["lse_c"], dvec_c, nhp=nhp_c, dkb=64,
                                                     qoff=0, koff=nhp_c, ktoff=0, voff=2 * nhp_c, scale=fox_scale,
                                                     cum=s["cum"], cumT=s["cumT"], name=f"fox_attn_bwd_l{l}",
                                                     rider=ex.scatter_rider("fox_attn_bwd", l), grad_dtype=BF16)
        ex.scattered("fox_attn_bwd", l, got)
        dtail_f, db = _fox_prep_bwd(s["za"], s["b_row"], dcq, dck, f"fox_prep_bwd_l{l}")
        g["fox_b_f"] = db[0, _F_LANES]
        psc = small["pool_scale"][l][None]
        du, dwbd, dpsc = _pool_bwd(dyb, s["pd"], s["wbd"], psc, f"pool_bwd_l{l}")
        g["pool_w"] = _pool_blockdiag_t(dwbd)
        g["pool_scale"] = dpsc[0]
        gq, gkv = small["q_a_norm"][l][None], small["kv_a_norm"][l][None]
        dza, dtail_k, dwq, dwkv, dgq, dgkv = _mla_prep_bwd(s["za"], gq, gkv, s["wq"], s["wkv"], tabs, dqf, dkf, dvm,
                                                            f"mla_prep_bwd_l{l}")
        g["q_a_norm"], g["kv_a_norm"] = dgq[0], dgkv[0]
        ex.grads[("w_small", l)] = _mla_grads_to_blocks(dwq, dwkv)
        w_in = ex.weights[("w_in", l)]
        dx, g["mix_norm"], dz = _mix_in_bwd(dx, s["x1"], small["mix_norm"][l][None], dza, dtail_k, du, dtail_f,
                                            dqc, dkc, dvc, w_in, f"mix_in_bwd_l{l}")
        dw_in = _unpad_w_in(_mm_tn(s["h2"][None], dz[None], 1, lambda p: 0, lambda p: 0, 1024, 640, f"dwin_l{l}",
                                   ts=4096)[0])
        ex.grads[("w_in", l)] = dw_in.reshape(N_DEV, ex.r_in, N_IN)

        wgu1, wd1 = ex.weights[("ffn1_w_gu", l)], ex.weights[("ffn1_w_down", l)]
        dy1 = dx
        gam1 = small["ffn1_norm"][l][None]
        split = ("ffn1", l) in _SPLIT_BWD
        if split:
            (dgu, act, dyh), got = _ffn_bwd_full(dy1, None, None, s["gu1"], None, wd1, FFN_BWD_TOKENS, f"ffn1_bwd_a_l{l}",
                                                 phase="act", rider=ex.scatter_rider("ffn1_bwd_a", l))
            ex.scattered("ffn1_bwd_a", l, got)
        else:
            (dx, g["ffn1_norm"], dgu, act, dyh), got = _ffn_bwd_full(
                dy1, s["x0"], gam1, s["gu1"], wgu1, wd1, FFN_BWD_TOKENS, f"ffn1_bwd_l{l}",
                rider=ex.scatter_rider("ffn1_bwd", l))
            ex.scattered("ffn1_bwd", l, got)
        ex.grads[("ffn1_w_down", l)] = _ffn_weight_grad(act, dyh, f"ffn1_dwd_l{l}")
        rider = ex.scatter_rider("ffn1_dwgu", l)
        dwgu = _ffn_weight_grad(dgu, s["h1"], f"ffn1_dwgu_l{l}", rider=rider)
        if rider is not None:
            dwgu, got = dwgu
            ex.scattered("ffn1_dwgu", l, got)
        ex.grads[("ffn1_w_gu", l)] = dwgu
        if split:
            (dx, g["ffn1_norm"]), got = _ffn_bwd_full(dy1, s["x0"], gam1, None, wgu1, None, FFN_BWD_TOKENS,
                                                     f"ffn1_bwd_b_l{l}", phase="in", dgu_in=dgu,
                                                     rider=ex.scatter_rider("ffn1_bwd_b", l))
            ex.scattered("ffn1_bwd_b", l, got)
        for k in ("ffn1_norm", "ffn2_norm", "mix_norm"):
            g[k] = g[k][0]
        small_grads[l] = g
    return loss, dx, small_grads, d_final[0]


_BIG = ("ffn1_w_gu", "ffn1_w_down", "w_in", "w_small", "w_out", "ffn2_w_gu", "ffn2_w_down")


def kernel(x, ffn1_norm, ffn1_w_gu, ffn1_w_down, mix_norm, w_in, q_a_norm, w_q_b, kv_a_norm, w_kv_b, pool_w, pool_scale, fox_b_f, w_out, ffn2_norm, ffn2_w_gu, ffn2_w_down, final_norm, loss_target, m_ffn1_norm, m_ffn1_w_gu, m_ffn1_w_down, m_mix_norm, m_w_in, m_q_a_norm, m_w_q_b, m_kv_a_norm, m_w_kv_b, m_pool_w, m_pool_scale, m_fox_b_f, m_w_out, m_ffn2_norm, m_ffn2_w_gu, m_ffn2_w_down, m_final_norm, v_ffn1_norm, v_ffn1_w_gu, v_ffn1_w_down, v_mix_norm, v_w_in, v_q_a_norm, v_w_q_b, v_kv_a_norm, v_w_kv_b, v_pool_w, v_pool_scale, v_fox_b_f, v_w_out, v_ffn2_norm, v_ffn2_w_gu, v_ffn2_w_down, v_final_norm):
    W = dict(ffn1_norm=ffn1_norm, ffn1_w_gu=ffn1_w_gu, ffn1_w_down=ffn1_w_down, mix_norm=mix_norm, w_in=w_in,
             q_a_norm=q_a_norm, w_q_b=w_q_b, kv_a_norm=kv_a_norm, w_kv_b=w_kv_b, pool_w=pool_w, pool_scale=pool_scale,
             fox_b_f=fox_b_f, w_out=w_out, ffn2_norm=ffn2_norm, ffn2_w_gu=ffn2_w_gu, ffn2_w_down=ffn2_w_down,
             final_norm=final_norm)
    M = dict(ffn1_norm=m_ffn1_norm, ffn1_w_gu=m_ffn1_w_gu, ffn1_w_down=m_ffn1_w_down, mix_norm=m_mix_norm, w_in=m_w_in,
             q_a_norm=m_q_a_norm, w_q_b=m_w_q_b, kv_a_norm=m_kv_a_norm, w_kv_b=m_w_kv_b, pool_w=m_pool_w,
             pool_scale=m_pool_scale, fox_b_f=m_fox_b_f, w_out=m_w_out, ffn2_norm=m_ffn2_norm, ffn2_w_gu=m_ffn2_w_gu,
             ffn2_w_down=m_ffn2_w_down, final_norm=m_final_norm)
    V = dict(ffn1_norm=v_ffn1_norm, ffn1_w_gu=v_ffn1_w_gu, ffn1_w_down=v_ffn1_w_down, mix_norm=v_mix_norm, w_in=v_w_in,
             q_a_norm=v_q_a_norm, w_q_b=v_w_q_b, kv_a_norm=v_kv_a_norm, w_kv_b=v_w_kv_b, pool_w=v_pool_w,
             pool_scale=v_pool_scale, fox_b_f=v_fox_b_f, w_out=v_w_out, ffn2_norm=v_ffn2_norm, ffn2_w_gu=v_ffn2_w_gu,
             ffn2_w_down=v_ffn2_w_down, final_norm=v_final_norm)
    L, D, n_sh = ffn1_w_gu.shape
    f_sh = ffn1_w_down.shape[1]
    assert n_sh == 2 * f_sh and L == DEPTH
    r_in, r_out = w_in.shape[1], w_out.shape[1]

    tr_in = lambda a: a.transpose(0, 2, 1)
    big_shards = dict(
        ffn1_w_gu=tr_in(ffn1_w_gu).astype(BF16), ffn1_w_down=ffn1_w_down.astype(BF16),
        w_in=_pad_w_in(w_in).astype(BF16), w_small=_small_pack(w_q_b, w_kv_b).astype(BF16), w_out=w_out.astype(BF16),
        ffn2_w_gu=tr_in(ffn2_w_gu).astype(BF16), ffn2_w_down=ffn2_w_down.astype(BF16))
    ex = _Exchange({(k, l): big_shards[k][l] for k in _BIG for l in range(L)}, D, f_sh, r_in, r_out)

    small = {k: W[k] for k in _SMALL}
    loss, dx, grads, d_final = _local_step(x[0], loss_target[0], ex, small)

    small_g = {k: jnp.stack([grads[l][k] for l in range(L)]) for k in _SMALL if k != "final_norm"}
    small_g["final_norm"] = d_final
    pack_g, recipe = _pack_small(small_g)
    n_small = pack_g.shape[0]
    loss_row = -(-n_small // 8) * 8
    pack_g = jnp.concatenate([pack_g, jnp.zeros((loss_row - n_small, LANES), F32), jnp.broadcast_to(loss, (8, LANES))],
                             axis=0)
    *got, packs = _comm_call(ex.scatter_rider("last", 0, pack=pack_g), "scatter_last")
    ex.scattered("last", 0, got)

    out = {}
    sm_w, sm_m, sm_v = (_small_pack(t["w_q_b"], t["w_kv_b"]) for t in (W, M, V))
    big = [("ffn1_w_gu", tr_in(W["ffn1_w_gu"]), tr_in(M["ffn1_w_gu"]), tr_in(V["ffn1_w_gu"]), f_sh),
           ("ffn1_w_down", W["ffn1_w_down"], M["ffn1_w_down"], V["ffn1_w_down"], 352),
           ("w_in", W["w_in"], M["w_in"], V["w_in"], 128),
           ("w_small", sm_w, sm_m, sm_v, 384),
           ("w_out", W["w_out"], M["w_out"], V["w_out"], 128),
           ("ffn2_w_gu", tr_in(W["ffn2_w_gu"]), tr_in(M["ffn2_w_gu"]), tr_in(V["ffn2_w_gu"]), f_sh),
           ("ffn2_w_down", W["ffn2_w_down"], M["ffn2_w_down"], V["ffn2_w_down"], 352)]
    for k, w_, m_, v_, tr in big:
        res = None
        for l in range(L):
            res = _adam_sum(ex.recv[(k, l)], w_, m_, v_, l, res, tr, f"adam_{k}_l{l}")
        if k == "w_small":
            cq, ckv = w_q_b.shape[2], w_kv_b.shape[2]
            parts = [_small_unpack(r, cq, ckv) for r in res]
            out["w_q_b"] = [p[0] for p in parts]
            out["w_kv_b"] = [p[1] for p in parts]
        elif k.endswith("w_gu"):
            out[k] = [tr_in(r) for r in res]
        else:
            out[k] = res

    pw, _ = _pack_small({k: W[k] for k in _SMALL})
    pm, _ = _pack_small({k: M[k] for k in _SMALL})
    pv, _ = _pack_small({k: V[k] for k in _SMALL})
    extra = ((0, loss_row + 8 - n_small), (0, 0))
    res = _adam_small(packs, jnp.pad(pw, extra), jnp.pad(pm, extra), jnp.pad(pv, extra), "adam_small")
    loss_total = res[0][loss_row, 0]
    small_out = [_unpack_small(r, recipe) for r in res]
    for k in _SMALL:
        out[k] = [t[k] for t in small_out]

    names = ["ffn1_norm", "ffn1_w_gu", "ffn1_w_down", "mix_norm", "w_in", "q_a_norm", "w_q_b", "kv_a_norm", "w_kv_b",
             "pool_w", "pool_scale", "fox_b_f", "w_out", "ffn2_norm", "ffn2_w_gu", "ffn2_w_down", "final_norm"]
    outs = [loss_total, dx[None]]
    for which in range(4):
        outs += [out[k][which] for k in names]
    return tuple(outs)
```

```python
import functools
import math

import numpy as np
import jax
import jax.numpy as jnp
from jax import lax
from jax.experimental import pallas as pl
from jax.experimental.pallas import tpu as pltpu

F32 = jnp.float32
BF16 = jnp.bfloat16
MESH_ID = pl.DeviceIdType.MESH

N_DEV = 8
EPS = 1e-6
DEPTH = 2

MLA_HEADS = 6
MLA_Q_RANK = 256
MLA_KV_RANK = 128
MLA_NOPE = 64
MLA_ROPE = 32
MLA_V = 64
ROPE_THETA = 10000.0
POOL_WINDOWS = (2, 4, 8, 16)
POOL_GROUP = 64
POOL_WIDTH = 256
FOX_HEADS = 6
FOX_HEAD_DIM = 64
N_IN = 1830

ADAM_LR = 0.001
ADAM_B1 = 0.9
ADAM_B2 = 0.999
ADAM_EPS = 1e-08
ADAM_WD = 0.01
ADAM_STEP = 10

LANES = 128
HEAD_BLOCK = 128
VMEM_LIMIT = 48 * 1024 * 1024
VMEM_LIMIT_BIG = 50 * 1024 * 1024
NEG = -1e30
ATTN_BLOCK = 1024

ZA = 768
ZF = 1152
N_PAD = ZA + ZF
TAIL0 = 640
ROPE_LANE0 = 64


def _f_lane(h):
    return 8 * (h // 2) + (h % 2)


_F_LANES = np.array([_f_lane(h) for h in range(FOX_HEADS)], np.int32)


def _dot(a, b):
    return jnp.dot(a, b, preferred_element_type=F32)


def _dot_nt(a, b):
    return lax.dot_general(a, b, (((1,), (1,)), ((), ())), preferred_element_type=F32)


def _dot_tn(a, b):
    return lax.dot_general(a, b, (((0,), (0,)), ((), ())), preferred_element_type=F32)


def _rms(x, gam):
    r = lax.rsqrt(jnp.mean(x * x, axis=-1, keepdims=True) + EPS)
    return x * r * gam


def _rms_bwd(dy, x, gam):
    r = lax.rsqrt(jnp.mean(x * x, axis=-1, keepdims=True) + EPS)
    xh = x * r
    dxh = dy * gam
    dx = r * (dxh - xh * jnp.mean(dxh * xh, axis=-1, keepdims=True))
    return dx, jnp.sum(dy * xh, axis=0, keepdims=True)


def _accum_out(ref, first, val):
    @pl.when(first)
    def _():
        ref[...] = val

    @pl.when(jnp.logical_not(first))
    def _():
        ref[...] += val


def _bs(shape, fn):
    return pl.BlockSpec(shape, fn)


def _params(dims, vmem=VMEM_LIMIT):
    return pltpu.CompilerParams(dimension_semantics=dims, vmem_limit_bytes=vmem)


def _tile(n, t):
    t = min(n, t)
    assert n % t == 0, (n, t)
    return t


HBM_SPEC = pl.BlockSpec(memory_space=pl.ANY)


def _call(body, *, name, grid, in_specs, out_specs, out_shape, scratch, dims, args, rider=None, vmem=VMEM_LIMIT):
    n_in, n_out = len(in_specs), len(out_specs)
    if rider is None:
        outs = pl.pallas_call(body, name=name, grid=grid, in_specs=in_specs, out_specs=out_specs, out_shape=out_shape,
                              scratch_shapes=scratch, compiler_params=_params(dims, vmem))(*args)
        return list(outs), []
    k_in, k_out, k_sem = len(rider.srcs), len(rider.out_shapes), len(rider.scratch)

    def riding(*refs):
        a, b, c, d = n_in, n_in + k_in, n_in + k_in + n_out, n_in + k_in + n_out + k_out
        rest = refs[d:]
        sems = rest[len(rest) - k_sem:]
        step = 0
        for i, g in enumerate(grid):
            step = step * g + pl.program_id(i)
        n_steps = math.prod(grid)

        @pl.when(step == 0)
        def _():
            rider.begin(refs[a:b], refs[c:d], sems)

        body(*refs[:a], *refs[b:c], *rest[:len(rest) - k_sem])

        @pl.when(step == (3 * n_steps) // 4)
        def _():
            rider.middle(refs[a:b], refs[c:d], sems)

        @pl.when(step == n_steps - 1)
        def _():
            rider.end(refs[a:b], refs[c:d], sems)

    outs = pl.pallas_call(
        riding, name=name, grid=grid, in_specs=list(in_specs) + [HBM_SPEC] * k_in,
        out_specs=list(out_specs) + [HBM_SPEC] * k_out, out_shape=list(out_shape) + list(rider.out_shapes),
        scratch_shapes=list(scratch) + list(rider.scratch),
        compiler_params=_params(("arbitrary",) * len(grid), vmem))(*args, *rider.srcs)
    return list(outs[:n_out]), list(outs[n_out:])


_ONCE = pl.Buffered(1)
_FF_CHUNKS = ((0, 1536), (1536, 2816))


def _ffn_fwd_full(x, gam, wgut, wd, tm, name, rider=None):
    S, D = x.shape
    F = wd.shape[0]
    tm = _tile(S, tm)
    chunks = _FF_CHUNKS if F == 2816 else ((0, F),)

    def body(x_ref, gam_ref, wgut_ref, wd_ref, xo_ref, h_ref, gu_ref):
        h = _rms(x_ref[...], gam_ref[...]).astype(BF16)
        h_ref[...] = h
        y = None
        for c0, c1 in chunks:
            g = _dot_nt(h, wgut_ref[c0:c1, :])
            u = _dot_nt(h, wgut_ref[F + c0:F + c1, :])
            gu_ref[:, c0:c1] = g.astype(BF16)
            gu_ref[:, F + c0:F + c1] = u.astype(BF16)
            a = (g * jax.nn.sigmoid(g) * u).astype(BF16)
            part = _dot(a, wd_ref[c0:c1, :])
            y = part if y is None else y + part
        xo_ref[...] = x_ref[...] + 0.5 * y

    row = lambda i: (i, 0)
    fix = lambda i: (0, 0)
    return _call(
        body, name=name, grid=(S // tm,),
        in_specs=[_bs((tm, D), row), _bs((1, D), fix), pl.BlockSpec((2 * F, D), fix, pipeline_mode=_ONCE),
                  pl.BlockSpec((F, D), fix, pipeline_mode=_ONCE)],
        out_specs=[_bs((tm, D), row), _bs((tm, D), row), _bs((tm, 2 * F), row)],
        out_shape=[jax.ShapeDtypeStruct((S, D), F32), jax.ShapeDtypeStruct((S, D), BF16),
                   jax.ShapeDtypeStruct((S, 2 * F), BF16)],
        scratch=[], dims=("parallel",), args=(x, gam, wgut, wd), rider=rider)


def _ffn_bwd_full(dy, x, gam, gu, wgut, wd, tm, name, phase="all", dgu_in=None, rider=None):
    S, D = dy.shape
    F = wd.shape[0] if wd is not None else wgut.shape[0] // 2
    tm = _tile(S, tm)
    chunks = _FF_CHUNKS if F == 2816 else ((0, F),)
    act, inp = phase in ("all", "act"), phase in ("all", "in")

    def body(*refs):
        refs = list(refs)
        dy_ref = refs.pop(0)
        x_ref, gam_ref = (refs.pop(0), refs.pop(0)) if inp else (None, None)
        gu_ref = refs.pop(0)
        wgut_ref = refs.pop(0) if inp else None
        wd_ref = refs.pop(0) if act else None
        if inp:
            dx_ref, dgam_ref = refs.pop(0), refs.pop(0)
        if act:
            dgu_ref, a_ref, dyh_ref = refs.pop(0), refs.pop(0), refs.pop(0)
            dyh = (0.5 * dy_ref[...]).astype(BF16)
            dyh_ref[...] = dyh
        dh = None
        for c0, c1 in chunks:
            if act:
                dg, du, a = _swiglu_bwd(_dot_nt(dyh, wd_ref[c0:c1, :]), gu_ref[:, c0:c1], gu_ref[:, F + c0:F + c1])
                dgu_ref[:, c0:c1] = dg
                dgu_ref[:, F + c0:F + c1] = du
                a_ref[:, c0:c1] = a
            else:
                dg, du = gu_ref[:, c0:c1], gu_ref[:, F + c0:F + c1]
            if inp:
                part = _dot(dg, wgut_ref[c0:c1, :]) + _dot(du, wgut_ref[F + c0:F + c1, :])
                dh = part if dh is None else dh + part
        if inp:
            dxn, dgam = _rms_bwd(dh, x_ref[...], gam_ref[...])
            dx_ref[...] = dy_ref[...] + dxn
            _accum_out(dgam_ref, pl.program_id(0) == 0, dgam)

    row = lambda i: (i, 0)
    fix = lambda i: (0, 0)
    in_specs, args = [_bs((tm, D), row)], [dy]
    if inp:
        in_specs += [_bs((tm, D), row), _bs((1, D), fix)]
        args += [x, gam]
    in_specs += [_bs((tm, 2 * F), row)]
    args += [gu if act else dgu_in]
    if inp:
        in_specs += [pl.BlockSpec((2 * F, D), fix, pipeline_mode=_ONCE)]
        args += [wgut]
    if act:
        in_specs += [pl.BlockSpec((F, D), fix, pipeline_mode=_ONCE)]
        args += [wd]
    out_specs, out_shape = [], []
    if inp:
        out_specs += [_bs((tm, D), row), _bs((1, D), fix)]
        out_shape += [jax.ShapeDtypeStruct((S, D), F32), jax.ShapeDtypeStruct((1, D), F32)]
    if act:
        out_specs += [_bs((tm, 2 * F), row), _bs((tm, F), row), _bs((tm, D), row)]
        out_shape += [jax.ShapeDtypeStruct((S, 2 * F), BF16), jax.ShapeDtypeStruct((S, F), BF16),
                      jax.ShapeDtypeStruct((S, D), BF16)]
    return _call(body, name=name, grid=(S // tm,), in_specs=in_specs, out_specs=out_specs, out_shape=out_shape,
                 scratch=[], dims=("arbitrary",), args=tuple(args), rider=rider, vmem=VMEM_LIMIT_BIG)


def _swiglu_bwd(da, g, u):
    g = g.astype(F32)
    u = u.astype(F32)
    sig = jax.nn.sigmoid(g)
    sl = g * sig
    dg = (da * u * (sig * (1.0 + g * (1.0 - sig)))).astype(BF16)
    return dg, (da * sl).astype(BF16), (sl * u).astype(BF16)


def _mm_tn(a, b, nb, a_of, b_of, tm, tn, name, rider=None, ts=512):
    _, S, M = a.shape
    N = b.shape[2]
    tm = _tile(M, tm)
    tn = _tile(N, tn)
    ts = _tile(S, ts)
    nk = S // ts

    def body(a_ref, b_ref, o_ref, acc):
        k = pl.program_id(3)

        @pl.when(k == 0)
        def _():
            acc[...] = jnp.zeros_like(acc)

        acc[...] += _dot_tn(a_ref[...].astype(BF16), b_ref[...].astype(BF16))

        @pl.when(k == nk - 1)
        def _():
            o_ref[...] = acc[...].astype(o_ref.dtype)

    (out,), extra = _call(
        body, name=name, grid=(nb, M // tm, N // tn, nk),
        in_specs=[_bs((None, ts, tm), lambda p, i, j, k: (a_of(p), k, i)),
                  _bs((None, ts, tn), lambda p, i, j, k: (b_of(p), k, j))],
        out_specs=[_bs((None, tm, tn), lambda p, i, j, k: (p, i, j))],
        out_shape=[jax.ShapeDtypeStruct((nb, M, N), BF16)],
        scratch=[pltpu.VMEM((tm, tn), F32)],
        dims=("parallel", "parallel", "parallel", "arbitrary"), args=(a, b), rider=rider)
    return (out, extra) if rider is not None else out


FFN_FWD_TOKENS = 512
FFN_BWD_TOKENS = 256
FFN_GRAD_ROWS = 1408


def _ffn_weight_grad(a, b, name, rider=None):
    M = a.shape[1]
    tm = max(t for t in range(LANES, FFN_GRAD_ROWS + 1, LANES) if M % t == 0)
    res = _mm_tn(a[None], b[None], 1, lambda p: 0, lambda p: 0, tm, b.shape[1], name, rider=rider, ts=2048)
    return (res[0][0], res[1]) if rider is not None else res[0]


def _mix_in_fwd(x, gam, w_in, name):
    S, D = x.shape
    tm = _tile(S, 512)
    nkv = (ZF - 384) // LANES

    def body(x_ref, gam_ref, w_ref, h_ref, za_ref, zf_ref, zt_ref):
        hb = _rms(x_ref[...], gam_ref[...]).astype(BF16)
        h_ref[...] = hb
        za_ref[...] = _dot(hb, w_ref[:, 0:ZA])
        zf = _dot(hb, w_ref[:, ZA:N_PAD])
        zf_ref[...] = zf.astype(BF16)
        for c in range(nkv):
            zt_ref[c * LANES:(c + 1) * LANES, :] = zf[:, 384 + c * LANES:384 + (c + 1) * LANES].T.astype(BF16)

    return pl.pallas_call(
        body, name=name, grid=(S // tm,),
        in_specs=[_bs((tm, D), lambda i: (i, 0)), _bs((1, D), lambda i: (0, 0)), _bs((D, N_PAD), lambda i: (0, 0))],
        out_specs=[_bs((tm, D), lambda i: (i, 0)), _bs((tm, ZA), lambda i: (i, 0)), _bs((tm, ZF), lambda i: (i, 0)),
                   _bs((nkv * LANES, tm), lambda i: (0, i))],
        out_shape=[jax.ShapeDtypeStruct((S, D), BF16), jax.ShapeDtypeStruct((S, ZA), F32),
                   jax.ShapeDtypeStruct((S, ZF), BF16), jax.ShapeDtypeStruct((nkv * LANES, S), BF16)],
        compiler_params=_params(("parallel",)),
    )(x, gam, w_in)


def _mix_in_bwd(dy, x, gam, dza_mla, dtail_k, du, dtail_f, dqf, dkf, dvf, w_in, name):
    S, D = x.shape
    tm = _tile(S, 512)

    def body(dy_ref, x_ref, gam_ref, dza_ref, dtk_ref, du_ref, dt_ref, dq_ref, dk_ref, dv_ref, w_ref,
             dx_ref, dgam_ref, dz_ref):
        i = pl.program_id(0)
        tail = (dtk_ref[...] + dt_ref[...]).astype(BF16)
        dz = jnp.concatenate([dza_ref[...], du_ref[...], tail, dq_ref[...], dk_ref[...], dv_ref[...]], axis=1)
        dz_ref[...] = dz
        dh = _dot_nt(dz, w_ref[...])
        dxn, dgam = _rms_bwd(dh, x_ref[...], gam_ref[...])
        dx_ref[...] = dy_ref[...] + dxn
        _accum_out(dgam_ref, i == 0, dgam)

    row = lambda i: (i, 0)
    fix = lambda i: (0, 0)
    return pl.pallas_call(
        body, name=name, grid=(S // tm,),
        in_specs=[_bs((tm, D), row), _bs((tm, D), row), _bs((1, D), fix), _bs((tm, 384), row), _bs((tm, 128), row),
                  _bs((tm, 256), row), _bs((tm, 128), row), _bs((tm, 384), row), _bs((tm, 384), row), _bs((tm, 384), row),
                  _bs((D, N_PAD), fix)],
        out_specs=[_bs((tm, D), row), _bs((1, D), fix), _bs((tm, N_PAD), row)],
        out_shape=[jax.ShapeDtypeStruct((S, D), F32), jax.ShapeDtypeStruct((1, D), F32),
                   jax.ShapeDtypeStruct((S, N_PAD), BF16)],
        compiler_params=_params(("arbitrary",)),
    )(dy, x, gam, dza_mla, dtail_k, du, dtail_f, dqf, dkf, dvf, w_in)


def _rope_tables(S):
    half = MLA_ROPE // 2
    inv_freq = ROPE_THETA ** (-jnp.arange(0, MLA_ROPE, 2, dtype=F32) / MLA_ROPE)
    ang = jnp.arange(S, dtype=jnp.int32).astype(F32)[:, None] * inv_freq[None, :]
    cos, sin = jnp.cos(ang), jnp.sin(ang)
    one = jnp.ones((S, ROPE_LANE0), F32)
    zero = jnp.zeros((S, ROPE_LANE0), F32)
    pad1 = jnp.ones((S, LANES - ROPE_LANE0 - MLA_ROPE), F32)
    pad0 = jnp.zeros((S, LANES - ROPE_LANE0 - MLA_ROPE), F32)
    zh = jnp.zeros((S, half), F32)
    tab_c = jnp.concatenate([one, cos, cos, pad1], axis=1)
    tab_ck = jnp.concatenate([zero, cos, cos, pad0], axis=1)
    tab_s1 = jnp.concatenate([zero, -sin, zh, pad0], axis=1)
    tab_s2 = jnp.concatenate([zero, zh, sin, pad0], axis=1)
    return tab_c, tab_ck, tab_s1, tab_s2


def _rope(x, c, s1, s2):
    return x * c + pltpu.roll(x, LANES - 16, 1) * s1 + pltpu.roll(x, 16, 1) * s2


def _rope_t(dy, c, s1, s2):
    return dy * c + pltpu.roll(dy * s1, 16, 1) + pltpu.roll(dy * s2, LANES - 16, 1)


_MLA_SCALE = 1.0 / math.sqrt(MLA_NOPE + MLA_ROPE)


def _mla_prep(za, gq, gkv, wq, wkv, tabs, name):
    S = za.shape[0]
    tm = _tile(S, 512)
    H = MLA_HEADS

    def body(zq_ref, tail_ref, gq_ref, gkv_ref, wq_ref, wkv_ref, c_ref, ck_ref, s1_ref, s2_ref,
             qf_ref, kf_ref, v_ref, kft_ref, vt_ref):
        zq = zq_ref[...]
        c, s1, s2 = c_ref[...], s1_ref[...], s2_ref[...]
        qn = _rms(zq[:, 0:256], gq_ref[...]).astype(BF16)
        q = _dot(qn, wq_ref[...])
        for h in range(H):
            blk = _rope(q[:, h * LANES:(h + 1) * LANES], c, s1, s2)
            qf_ref[:, h * LANES:(h + 1) * LANES] = (blk * _MLA_SCALE).astype(BF16)
        kvn = _rms(zq[:, 256:384], gkv_ref[...]).astype(BF16)
        kv = _dot(kvn, wkv_ref[...])
        kt = _rope(tail_ref[...], ck_ref[...], s1, s2)
        for h in range(H):
            sl = slice(h * LANES, (h + 1) * LANES)
            kblk = kv[:, sl] + kt
            kf_ref[:, sl] = kblk.astype(BF16)
            kft_ref[sl, :] = kblk.T.astype(BF16)
        v_ref[...] = kv[:, H * LANES:].astype(BF16)
        for cblk in range(H * MLA_V // LANES):
            sl = slice(cblk * LANES, (cblk + 1) * LANES)
            vt_ref[sl, :] = kv[:, H * LANES + cblk * LANES:H * LANES + (cblk + 1) * LANES].T.astype(BF16)

    row = lambda i: (i, 0)
    col = lambda i: (0, i)
    fix = lambda i: (0, 0)
    return pl.pallas_call(
        body, name=name, grid=(S // tm,),
        in_specs=[_bs((tm, 384), row), _bs((tm, 128), lambda i: (i, TAIL0 // 128)), _bs((1, 256), fix), _bs((1, 128), fix),
                  _bs((256, 768), fix), _bs((128, 1152), fix),
                  _bs((tm, 128), row), _bs((tm, 128), row), _bs((tm, 128), row), _bs((tm, 128), row)],
        out_specs=[_bs((tm, 768), row), _bs((tm, 768), row), _bs((tm, 384), row), _bs((768, tm), col), _bs((384, tm), col)],
        out_shape=[jax.ShapeDtypeStruct((S, 768), BF16), jax.ShapeDtypeStruct((S, 768), BF16),
                   jax.ShapeDtypeStruct((S, 384), BF16), jax.ShapeDtypeStruct((768, S), BF16),
                   jax.ShapeDtypeStruct((384, S), BF16)],
        compiler_params=_params(("parallel",)),
    )(za, za, gq, gkv, wq, wkv, *tabs)


def _mla_prep_bwd(za, gq, gkv, wq, wkv, tabs, dqf, dkf, dvm, name):
    S = za.shape[0]
    tm = _tile(S, 512)
    H = MLA_HEADS

    def body(zq_ref, gq_ref, gkv_ref, wq_ref, wkv_ref, c_ref, ck_ref, s1_ref, s2_ref, dqf_ref, dkf_ref, dvm_ref,
             dza_ref, dtail_ref, dwq_ref, dwkv_ref, dgq_ref, dgkv_ref):
        i = pl.program_id(0)
        first = i == 0
        zq = zq_ref[...]
        c, s1, s2 = c_ref[...], s1_ref[...], s2_ref[...]
        lane = lax.broadcasted_iota(jnp.int32, (1, LANES), 1)
        nope = lane < MLA_NOPE
        rope = jnp.logical_and(lane >= ROPE_LANE0, lane < ROPE_LANE0 + MLA_ROPE)

        qa = zq[:, 0:256]
        qn = _rms(qa, gq_ref[...]).astype(BF16)
        dqf = dqf_ref[...]
        dq_pre = jnp.concatenate(
            [_rope_t(dqf[:, h * LANES:(h + 1) * LANES] * _MLA_SCALE, c, s1, s2) for h in range(H)], axis=1).astype(BF16)
        _accum_out(dwq_ref, first, _dot_tn(qn, dq_pre))
        dqa, dgq = _rms_bwd(_dot_nt(dq_pre, wq_ref[...]), qa, gq_ref[...])
        _accum_out(dgq_ref, first, dgq)

        kva = zq[:, 256:384]
        kvn = _rms(kva, gkv_ref[...]).astype(BF16)
        dkf = dkf_ref[...]
        parts = []
        dkt = jnp.zeros((tm, LANES), F32)
        for h in range(H):
            blk = dkf[:, h * LANES:(h + 1) * LANES]
            parts.append(jnp.where(nope, blk, 0.0))
            dkt = dkt + jnp.where(rope, blk, 0.0)
        dkv_pre = jnp.concatenate(parts + [dvm_ref[...]], axis=1).astype(BF16)
        _accum_out(dwkv_ref, first, _dot_tn(kvn, dkv_pre))
        dkva, dgkv = _rms_bwd(_dot_nt(dkv_pre, wkv_ref[...]), kva, gkv_ref[...])
        _accum_out(dgkv_ref, first, dgkv)

        dtail_ref[...] = _rope_t(dkt, ck_ref[...], s1, s2)
        dza_ref[...] = jnp.concatenate([dqa, dkva], axis=1).astype(BF16)

    row = lambda i: (i, 0)
    fix = lambda i: (0, 0)
    return pl.pallas_call(
        body, name=name, grid=(S // tm,),
        in_specs=[_bs((tm, 384), row), _bs((1, 256), fix), _bs((1, 128), fix), _bs((256, 768), fix), _bs((128, 1152), fix),
                  _bs((tm, 128), row), _bs((tm, 128), row), _bs((tm, 128), row), _bs((tm, 128), row),
                  _bs((tm, 768), row), _bs((tm, 768), row), _bs((tm, 384), row)],
        out_specs=[_bs((tm, 384), row), _bs((tm, LANES), row), _bs((256, 768), fix), _bs((128, 1152), fix),
                   _bs((1, 256), fix), _bs((1, 128), fix)],
        out_shape=[jax.ShapeDtypeStruct((S, 384), BF16), jax.ShapeDtypeStruct((S, LANES), F32),
                   jax.ShapeDtypeStruct((256, 768), F32),
                   jax.ShapeDtypeStruct((128, 1152), F32), jax.ShapeDtypeStruct((1, 256), F32),
                   jax.ShapeDtypeStruct((1, 128), F32)],
        compiler_params=_params(("arbitrary",)),
    )(za, gq, gkv, wq, wkv, *tabs, dqf, dkf, dvm)


def _head_views(qb, kb, r, dkb, sel):
    if dkb == LANES:
        sl = slice(r * LANES, (r + 1) * LANES)
        return qb[:, sl], kb[:, sl], kb[:, sl]
    return jnp.where(sel, qb, jnp.zeros_like(qb)), kb, jnp.where(sel, kb, jnp.zeros_like(kb))


def _attn_fwd_t(q_arr, k_arr, vt_arr, *, nhp, dkb, qoff, koff, vtoff, scale, cum, cumT, name, rider=None):
    S = q_arr.shape[0]
    T = _tile(S, ATTN_BLOCK)
    nq = S // T
    W = 2 * dkb
    bias = cum is not None

    def body(*refs):
        if bias:
            q_ref, k_ref, vt_ref, cq_ref, ck_ref, o_ref, lse_ref, m_s, l_s, acc_s = refs
        else:
            q_ref, k_ref, vt_ref, o_ref, lse_ref, m_s, l_s, acc_s = refs
        hp, qi, ki = pl.program_id(0), pl.program_id(1), pl.program_id(2)
        lo_lane = lax.broadcasted_iota(jnp.int32, (1, LANES), 1) < 64
        lo_row = lax.broadcasted_iota(jnp.int32, (LANES, 1), 0) < 64

        @pl.when(ki == 0)
        def _():
            m_s[...] = jnp.full_like(m_s, NEG)
            l_s[...] = jnp.zeros_like(l_s)
            acc_s[...] = jnp.zeros_like(acc_s)

        def step(masked):
            qb, kb, vtb = q_ref[...], k_ref[...], vt_ref[...]
            if masked:
                mask = lax.broadcasted_iota(jnp.int32, (T, T), 0) <= lax.broadcasted_iota(jnp.int32, (T, T), 1)
            if bias:
                li = lax.broadcasted_iota(jnp.int32, (T, LANES), 1)
                ckb = ck_ref[...]
            m_all, l_all = m_s[...], l_s[...]
            scores = []
            for r in range(2):
                sel = lo_lane if r == 0 else jnp.logical_not(lo_lane)
                q, k, _ = _head_views(qb, kb, r, dkb, sel)
                if scale != 1.0:
                    q = q * jnp.asarray(scale, q.dtype)
                scores.append(_dot_nt(k, q))
            m_out, l_out, alphas, pvs = [], [], [], []
            for r in range(2):
                rsel = lo_row if r == 0 else jnp.logical_not(lo_row)
                s = scores[r]
                if bias:
                    ck = jnp.sum(jnp.where(li == 8 * hp + r, ckb, 0.0), axis=1, keepdims=True)
                    s = s + (cq_ref[r:r + 1, :] - ck)
                if masked:
                    s = jnp.where(mask, s, NEG)
                m_prev = m_all[r:r + 1, :]
                m_new = jnp.maximum(m_prev, jnp.max(s, axis=0, keepdims=True))
                alpha = jnp.exp(m_prev - m_new)
                p = jnp.exp(s - m_new)
                m_out.append(m_new)
                l_out.append(alpha * l_all[r:r + 1, :] + jnp.sum(p, axis=0, keepdims=True))
                alphas.append(alpha)
                pvs.append(_dot(jnp.where(rsel, vtb, jnp.zeros_like(vtb)), p.astype(BF16)))
            m_s[0:1, :] = m_out[0]
            m_s[1:2, :] = m_out[1]
            l_s[0:1, :] = l_out[0]
            l_s[1:2, :] = l_out[1]
            acc_s[...] = acc_s[...] * jnp.where(lo_row, alphas[0], alphas[1]) + (pvs[0] + pvs[1])

        @pl.when(ki < qi)
        def _():
            step(False)

        @pl.when(ki == qi)
        def _():
            step(True)

        @pl.when(ki == nq - 1)
        def _():
            inv = jnp.where(lo_row, 1.0 / l_s[0:1, :], 1.0 / l_s[1:2, :])
            o_ref[...] = (acc_s[...] * inv).T.astype(BF16)
            used = lax.broadcasted_iota(jnp.int32, (8, T), 0) < 2
            lse_ref[...] = jnp.where(used, m_s[...] + jnp.log(jnp.where(used, l_s[...], 1.0)), 0.0)

    kmap = lambda hp, qi, ki: jnp.minimum(ki, qi)
    in_specs = [_bs((T, W), lambda hp, qi, ki: (qi, qoff + hp)),
                _bs((T, W), lambda hp, qi, ki: (kmap(hp, qi, ki), koff + hp)),
                _bs((LANES, T), lambda hp, qi, ki: (vtoff + hp, kmap(hp, qi, ki)))]
    args = [q_arr, k_arr, vt_arr]
    if bias:
        in_specs += [_bs((8, T), lambda hp, qi, ki: (hp, qi)), _bs((T, LANES), lambda hp, qi, ki: (kmap(hp, qi, ki), 0))]
        args += [cumT, cum]
    return _call(
        body, name=name, grid=(nhp, nq, nq),
        in_specs=in_specs,
        out_specs=[_bs((T, LANES), lambda hp, qi, ki: (qi, hp)), _bs((None, 8, T), lambda hp, qi, ki: (hp, 0, qi))],
        out_shape=[jax.ShapeDtypeStruct((S, nhp * LANES), BF16), jax.ShapeDtypeStruct((nhp, 8, S), F32)],
        scratch=[pltpu.VMEM((8, T), F32), pltpu.VMEM((8, T), F32), pltpu.VMEM((LANES, T), F32)],
        dims=("parallel", "parallel", "arbitrary"), args=args, rider=rider)


def _attn_bwd_t(q_arr, k_arr, kt_arr, v_arr, do_arr, lse, dvec, *, nhp, dkb, qoff, koff, ktoff, voff, scale, cum, cumT,
                name, rider=None, grad_dtype=F32):
    S = q_arr.shape[0]
    T = _tile(S, ATTN_BLOCK)
    nq = S // T
    W = 2 * dkb
    bias = cum is not None

    def body(*refs):
        if bias:
            (q_ref, k_ref, kt_ref, v_ref, do_ref, lse_ref, dvec_ref, cq_ref, ck_ref,
             dq_ref, dk_ref, dv_ref, dcq_ref, dck_ref, dqt_s, dk_s, dv_s, dcq_s, dck_s) = refs
        else:
            (q_ref, k_ref, kt_ref, v_ref, do_ref, lse_ref, dvec_ref,
             dq_ref, dk_ref, dv_ref, dqt_s, dk_s, dv_s) = refs
        hp, ki, qi = pl.program_id(0), pl.program_id(1), pl.program_id(2)
        lo_lane = lax.broadcasted_iota(jnp.int32, (1, LANES), 1) < 64
        lo_row = lax.broadcasted_iota(jnp.int32, (LANES, 1), 0) < 64

        @pl.when(jnp.logical_and(ki == 0, qi == 0))
        def _():
            dqt_s[...] = jnp.zeros_like(dqt_s)
            if bias:
                dcq_s[...] = jnp.zeros_like(dcq_s)

        @pl.when(qi == 0)
        def _():
            dk_s[...] = jnp.zeros_like(dk_s)
            dv_s[...] = jnp.zeros_like(dv_s)
            if bias:
                dck_s[...] = jnp.zeros_like(dck_s)

        def step(masked):
            qb, kb, ktb, vb, dob = q_ref[...], k_ref[...], kt_ref[...], v_ref[...], do_ref[...]
            if masked:
                mask = lax.broadcasted_iota(jnp.int32, (T, T), 0) <= lax.broadcasted_iota(jnp.int32, (T, T), 1)
            if bias:
                li = lax.broadcasted_iota(jnp.int32, (T, LANES), 1)
                ckb = ck_ref[...]
            for r in range(2):
                sel = lo_lane if r == 0 else jnp.logical_not(lo_lane)
                rsel = lo_row if r == 0 else jnp.logical_not(lo_row)
                q, k, _ = _head_views(qb, kb, r, dkb, sel)
                if scale != 1.0:
                    q = q * jnp.asarray(scale, q.dtype)
                s = _dot_nt(k, q)
                if bias:
                    ck = jnp.sum(jnp.where(li == 8 * hp + r, ckb, 0.0), axis=1, keepdims=True)
                    s = s + (cq_ref[r:r + 1, :] - ck)
                p = jnp.exp(s - lse_ref[r:r + 1, :])
                if masked:
                    p = jnp.where(mask, p, 0.0)
                do_r = jnp.where(sel, dob, jnp.zeros_like(dob))
                dp = _dot_nt(vb, do_r)
                ds = p * (dp - dvec_ref[r:r + 1, :])
                pb = p.astype(BF16)
                dsb = ds.astype(BF16)
                dv_s[...] += _dot(pb, do_r)
                if dkb == LANES:
                    sl = slice(r * LANES, (r + 1) * LANES)
                    dk_s[:, sl] += _dot(dsb, q)
                    dqt_s[qi, sl, :] += _dot(ktb[sl, :], dsb) * scale
                else:
                    dk_s[...] += _dot(dsb, q)
                    dqt_s[qi] += _dot(jnp.where(rsel, ktb, jnp.zeros_like(ktb)), dsb) * scale
                if bias:
                    dcq_s[qi, r:r + 1, :] += jnp.sum(ds, axis=0, keepdims=True)
                    dck_s[...] -= jnp.where(li == 8 * hp + r, jnp.sum(ds, axis=1, keepdims=True), 0.0)

        @pl.when(qi > ki)
        def _():
            step(False)

        @pl.when(qi == ki)
        def _():
            step(True)

        @pl.when(qi == nq - 1)
        def _():
            dk_ref[...] = dk_s[...].astype(grad_dtype)
            dv_ref[...] = dv_s[...].astype(grad_dtype)
            if bias:
                dck_ref[...] = dck_s[...]

        @pl.when(jnp.logical_and(ki == nq - 1, qi == nq - 1))
        def _():
            for c in range(nq):
                dq_ref[c * T:(c + 1) * T, :] = dqt_s[c].T.astype(grad_dtype)
                if bias:
                    dcq_ref[:, c * T:(c + 1) * T] = dcq_s[c]

    qmap = lambda hp, ki, qi: jnp.maximum(qi, ki)
    in_specs = [_bs((T, W), lambda hp, ki, qi: (qmap(hp, ki, qi), qoff + hp)),
                _bs((T, W), lambda hp, ki, qi: (ki, koff + hp)),
                _bs((W, T), lambda hp, ki, qi: (ktoff + hp, ki)),
                _bs((T, LANES), lambda hp, ki, qi: (ki, voff + hp)),
                _bs((T, LANES), lambda hp, ki, qi: (qmap(hp, ki, qi), hp)),
                _bs((None, 8, T), lambda hp, ki, qi: (hp, 0, qmap(hp, ki, qi))),
                _bs((None, 8, T), lambda hp, ki, qi: (hp, 0, qmap(hp, ki, qi)))]
    args = [q_arr, k_arr, kt_arr, v_arr, do_arr, lse, dvec]
    out_specs = [_bs((S, W), lambda hp, ki, qi: (0, hp)), _bs((T, W), lambda hp, ki, qi: (ki, hp)),
                 _bs((T, LANES), lambda hp, ki, qi: (ki, hp))]
    out_shape = [jax.ShapeDtypeStruct((S, nhp * W), grad_dtype), jax.ShapeDtypeStruct((S, nhp * W), grad_dtype),
                 jax.ShapeDtypeStruct((S, nhp * LANES), grad_dtype)]
    scratch = [pltpu.VMEM((nq, W, T), F32), pltpu.VMEM((T, W), F32), pltpu.VMEM((T, LANES), F32)]
    if bias:
        in_specs += [_bs((8, T), lambda hp, ki, qi: (hp, qmap(hp, ki, qi))), _bs((T, LANES), lambda hp, ki, qi: (ki, 0))]
        args += [cumT, cum]
        out_specs += [_bs((None, 8, S), lambda hp, ki, qi: (hp, 0, 0)), _bs((None, T, LANES), lambda hp, ki, qi: (hp, ki, 0))]
        out_shape += [jax.ShapeDtypeStruct((nhp, 8, S), F32), jax.ShapeDtypeStruct((nhp, S, LANES), F32)]
        scratch += [pltpu.VMEM((nq, 8, T), F32), pltpu.VMEM((T, LANES), F32)]
    return _call(body, name=name, grid=(nhp, nq, nq), in_specs=in_specs, out_specs=out_specs, out_shape=out_shape,
                 scratch=scratch, dims=("arbitrary", "arbitrary", "arbitrary"), args=args, rider=rider)


def _gate_lanes(shape):
    lane = lax.broadcasted_iota(jnp.int32, shape, 1)
    return jnp.logical_and(lane < 8 * (FOX_HEADS // 2), lane % 8 < 2)


def _fox_prep(za, b_row, name):
    S = za.shape[0]
    nrow = 8 * (FOX_HEADS // 2)

    def body(tail_ref, b_ref, cum_ref, cumt_ref):
        x = tail_ref[...] + b_ref[...]
        logf = jnp.minimum(x, 0.0) - jnp.log(1.0 + jnp.exp(-jnp.abs(x)))
        y = jnp.where(_gate_lanes((S, LANES)), logf, 0.0)
        row = lax.broadcasted_iota(jnp.int32, (S, LANES), 0)
        k = 1
        while k < S:
            y = y + jnp.where(row >= k, pltpu.roll(y, k, 0), 0.0)
            k *= 2
        cum_ref[...] = y
        cumt_ref[...] = y.T[0:nrow, :]

    return pl.pallas_call(
        body, name=name, grid=(1,),
        in_specs=[_bs((S, LANES), lambda i: (0, TAIL0 // LANES)), _bs((1, LANES), lambda i: (0, 0))],
        out_specs=[_bs((S, LANES), lambda i: (0, 0)), _bs((nrow, S), lambda i: (0, 0))],
        out_shape=[jax.ShapeDtypeStruct((S, LANES), F32), jax.ShapeDtypeStruct((nrow, S), F32)],
        compiler_params=_params(("arbitrary",)),
    )(za, b_row)


def _fox_prep_bwd(za, b_row, dcq, dck, name):
    S = za.shape[0]
    nhp = FOX_HEADS // 2
    nrow = 8 * nhp
    dcq2 = dcq.reshape(nrow, S)

    def body(tail_ref, b_ref, dcq_ref, dck_ref, dt_ref, db_ref):
        x = tail_ref[...] + b_ref[...]
        d = jnp.concatenate([dcq_ref[...], jnp.zeros((LANES - nrow, S), F32)], axis=0).T
        for hp in range(nhp):
            d = d + dck_ref[hp]
        row = lax.broadcasted_iota(jnp.int32, (S, LANES), 0)
        k = 1
        while k < S:
            d = d + jnp.where(row < S - k, pltpu.roll(d, S - k, 0), 0.0)
            k *= 2
        df = jnp.where(_gate_lanes((S, LANES)), d * jax.nn.sigmoid(-x), 0.0)
        dt_ref[...] = df
        db_ref[...] = jnp.sum(df, axis=0, keepdims=True)

    return pl.pallas_call(
        body, name=name, grid=(1,),
        in_specs=[_bs((S, LANES), lambda i: (0, TAIL0 // LANES)), _bs((1, LANES), lambda i: (0, 0)),
                  _bs((nrow, S), lambda i: (0, 0)), _bs((nhp, S, LANES), lambda i: (0, 0, 0))],
        out_specs=[_bs((S, LANES), lambda i: (0, 0)), _bs((1, LANES), lambda i: (0, 0))],
        out_shape=[jax.ShapeDtypeStruct((S, LANES), F32), jax.ShapeDtypeStruct((1, LANES), F32)],
        compiler_params=_params(("arbitrary",)),
    )(za, b_row, dcq2, dck)


def _pool_select(half, lane_lo, vals):
    return jnp.where(lane_lo, jnp.where(half == 0, vals[0], vals[2]), jnp.where(half == 0, vals[1], vals[3]))


def _pool_den(S, half, lane_lo):
    cnt = (lax.broadcasted_iota(jnp.int32, (S, LANES), 0) + 1).astype(F32)
    w = _pool_select(half, lane_lo, [float(x) for x in POOL_WINDOWS])
    return jnp.minimum(cnt, w)


def _pool_fwd(za, wbd, scale, name):
    S = za.shape[0]

    def body(u_ref, w_ref, sc_ref, y_ref, pd_ref):
        half = pl.program_id(0)
        u = u_ref[...]
        row = lax.broadcasted_iota(jnp.int32, (S, LANES), 0)
        lane_lo = lax.broadcasted_iota(jnp.int32, (S, LANES), 1) < POOL_GROUP
        sums = []
        acc = u
        k = 1
        while k < POOL_WINDOWS[-1]:
            acc = acc + jnp.where(row >= k, pltpu.roll(acc, k, 0), 0.0)
            sums.append(acc)
            k *= 2
        pooled = _pool_select(half, lane_lo, sums) / _pool_den(S, half, lane_lo)
        pd = (pooled - u).astype(BF16)
        pd_ref[...] = pd
        y_ref[...] = (_dot(pd, w_ref[...]) * sc_ref[...]).astype(BF16)

    return pl.pallas_call(
        body, name=name, grid=(2,),
        in_specs=[_bs((S, LANES), lambda i: (0, 384 // LANES + i)), _bs((None, LANES, LANES), lambda i: (i, 0, 0)),
                  _bs((1, LANES), lambda i: (0, i))],
        out_specs=[_bs((S, LANES), lambda i: (0, i)), _bs((S, LANES), lambda i: (0, i))],
        out_shape=[jax.ShapeDtypeStruct((S, POOL_WIDTH), BF16), jax.ShapeDtypeStruct((S, POOL_WIDTH), BF16)],
        compiler_params=_params(("parallel",)),
    )(za, wbd, scale)


def _pool_bwd(dyb, pd, wbd, scale, name):
    S = pd.shape[0]

    def body(dy_ref, pd_ref, w_ref, sc_ref, du_ref, dw_ref, dsc_ref):
        half = pl.program_id(0)
        dy = dy_ref[...]
        pd = pd_ref[...]
        w = w_ref[...]
        ypre = _dot(pd, w)
        dsc_ref[...] = jnp.sum(dy * ypre, axis=0, keepdims=True)
        dyp = (dy * sc_ref[...]).astype(BF16)
        dw_ref[...] = _dot_tn(pd, dyp)
        dpd = _dot_nt(dyp, w)
        row = lax.broadcasted_iota(jnp.int32, (S, LANES), 0)
        lane_lo = lax.broadcasted_iota(jnp.int32, (S, LANES), 1) < POOL_GROUP
        acc = dpd / _pool_den(S, half, lane_lo)
        sums = []
        k = 1
        while k < POOL_WINDOWS[-1]:
            acc = acc + jnp.where(row < S - k, pltpu.roll(acc, S - k, 0), 0.0)
            sums.append(acc)
            k *= 2
        du_ref[...] = (_pool_select(half, lane_lo, sums) - dpd).astype(BF16)

    return pl.pallas_call(
        body, name=name, grid=(2,),
        in_specs=[_bs((S, LANES), lambda i: (0, i)), _bs((S, LANES), lambda i: (0, i)),
                  _bs((None, LANES, LANES), lambda i: (i, 0, 0)), _bs((1, LANES), lambda i: (0, i))],
        out_specs=[_bs((S, LANES), lambda i: (0, i)), _bs((None, LANES, LANES), lambda i: (i, 0, 0)),
                   _bs((1, LANES), lambda i: (0, i))],
        out_shape=[jax.ShapeDtypeStruct((S, POOL_WIDTH), BF16), jax.ShapeDtypeStruct((2, LANES, LANES), F32),
                   jax.ShapeDtypeStruct((1, POOL_WIDTH), F32)],
        compiler_params=_params(("parallel",)),
    )(dyb, pd, wbd, scale)


def _mix_out_fwd(x, ya, yb, yc, w_out, name):
    S, D = x.shape
    tm = _tile(S, 512)
    K = w_out.shape[0]

    def body(x_ref, ya_ref, yb_ref, yc_ref, w_ref, xo_ref, yc_out):
        ycat = jnp.concatenate([ya_ref[...], yb_ref[...], yc_ref[...]], axis=1)
        yc_out[...] = ycat
        xo_ref[...] = x_ref[...] + _dot(ycat, w_ref[...])

    row = lambda i: (i, 0)
    return pl.pallas_call(
        body, name=name, grid=(S // tm,),
        in_specs=[_bs((tm, D), row), _bs((tm, 384), row), _bs((tm, 256), row), _bs((tm, 384), row),
                  _bs((K, D), lambda i: (0, 0))],
        out_specs=[_bs((tm, D), row), _bs((tm, K), row)],
        out_shape=[jax.ShapeDtypeStruct((S, D), F32), jax.ShapeDtypeStruct((S, K), BF16)],
        compiler_params=_params(("parallel",)),
    )(x, ya, yb, yc, w_out)


def _mix_out_bwd(dy, w_out, ya, yc, name):
    S, D = dy.shape
    tm = _tile(S, 512)
    K = w_out.shape[0]
    nhp = ya.shape[1] // LANES

    def body(dy_ref, w_ref, ya_ref, yc_ref, da_ref, db_ref, dc_ref, dva_ref, dvc_ref):
        d = _dot_nt(dy_ref[...].astype(BF16), w_ref[...])
        da = d[:, 0:384].astype(BF16)
        dc = d[:, 640:1024].astype(BF16)
        da_ref[...] = da
        db_ref[...] = d[:, 384:640]
        dc_ref[...] = dc
        li = lax.broadcasted_iota(jnp.int32, (tm, LANES), 1)
        for do, o_ref, out_ref in ((da, ya_ref, dva_ref), (dc, yc_ref, dvc_ref)):
            for hp in range(nhp):
                sl = slice(hp * LANES, (hp + 1) * LANES)
                prod = do[:, sl].astype(F32) * o_ref[:, sl].astype(F32)
                d0 = jnp.sum(jnp.where(li < 64, prod, 0.0), axis=1, keepdims=True)
                d1 = jnp.sum(jnp.where(li >= 64, prod, 0.0), axis=1, keepdims=True)
                out_ref[hp] = jnp.where(li == 0, d0, jnp.where(li == 1, d1, 0.0)).T[0:8, :]

    row = lambda i: (i, 0)
    dv_spec = _bs((nhp, 8, tm), lambda i: (0, 0, i))
    dv_shape = jax.ShapeDtypeStruct((nhp, 8, S), F32)
    return pl.pallas_call(
        body, name=name, grid=(S // tm,),
        in_specs=[_bs((tm, D), row), _bs((K, D), lambda i: (0, 0)), _bs((tm, 384), row), _bs((tm, 384), row)],
        out_specs=[_bs((tm, 384), row), _bs((tm, 256), row), _bs((tm, 384), row), dv_spec, dv_spec],
        out_shape=[jax.ShapeDtypeStruct((S, 384), BF16), jax.ShapeDtypeStruct((S, 256), F32),
                   jax.ShapeDtypeStruct((S, 384), BF16), dv_shape, dv_shape],
        compiler_params=_params(("parallel",)),
    )(dy, w_out, ya, yc)


def _loss_head(x, gam, target, name):
    S, D = x.shape
    tm = _tile(S, 512)

    def body(x_ref, gam_ref, t_ref, dx_ref, dgam_ref, loss_ref):
        i = pl.program_id(0)
        xv = x_ref[...]
        err = _rms(xv, gam_ref[...]) - t_ref[...]
        part = 0.5 * jnp.sum(jnp.mean(err * err, axis=-1, keepdims=True), axis=0, keepdims=True)
        dxn, dgam = _rms_bwd(err * (1.0 / D), xv, gam_ref[...])
        dx_ref[...] = dxn
        _accum_out(dgam_ref, i == 0, dgam)
        _accum_out(loss_ref, i == 0, jnp.broadcast_to(part, (1, LANES)))

    row = lambda i: (i, 0)
    fix = lambda i: (0, 0)
    return pl.pallas_call(
        body, name=name, grid=(S // tm,),
        in_specs=[_bs((tm, D), row), _bs((1, D), fix), _bs((tm, D), row)],
        out_specs=[_bs((tm, D), row), _bs((1, D), fix), _bs((1, LANES), fix)],
        out_shape=[jax.ShapeDtypeStruct((S, D), F32), jax.ShapeDtypeStruct((1, D), F32),
                   jax.ShapeDtypeStruct((1, LANES), F32)],
        compiler_params=_params(("arbitrary",)),
    )(x, gam, target)


def _adam_math(g, w, m, v):
    m = ADAM_B1 * m + (1.0 - ADAM_B1) * g
    v = ADAM_B2 * v + (1.0 - ADAM_B2) * (g * g)
    m_hat = m / (1.0 - ADAM_B1 ** ADAM_STEP)
    v_hat = v / (1.0 - ADAM_B2 ** ADAM_STEP)
    delta = -ADAM_LR * (m_hat / (jnp.sqrt(v_hat) + ADAM_EPS) + ADAM_WD * w)
    return delta, m, v


def _adam_sum(recv, w, m, v, layer, prev, tr, name):
    L, R, C = w.shape
    Cp = recv.shape[2]
    tr = _tile(R, tr)

    def body(r_ref, w_ref, m_ref, v_ref, *rest):
        g_out, d_out, m_out, v_out = rest[len(rest) - 4:]
        g = r_ref[0, :, 0:C].astype(F32)
        for p in range(1, N_DEV):
            g = g + r_ref[p, :, 0:C].astype(F32)
        delta, mn, vn = _adam_math(g, w_ref[...], m_ref[...], v_ref[...])
        g_out[...] = g
        d_out[...] = delta
        m_out[...] = mn
        v_out[...] = vn

    blk = _bs((None, tr, C), lambda i: (layer, i, 0))
    shp = jax.ShapeDtypeStruct((L, R, C), F32)
    in_specs = [_bs((N_DEV, tr, Cp), lambda i: (0, i, 0)), blk, blk, blk]
    args = [recv, w, m, v]
    aliases = {}
    if prev is not None:
        in_specs += [HBM_SPEC] * 4
        args += list(prev)
        aliases = {4 + k: k for k in range(4)}
    return pl.pallas_call(
        body, name=name, grid=(R // tr,),
        in_specs=in_specs, out_specs=[blk, blk, blk, blk], out_shape=[shp, shp, shp, shp],
        input_output_aliases=aliases, compiler_params=_params(("parallel",)),
    )(*args)


def _dev_index(px, py, pc):
    return 4 * px + 2 * py + pc


class _GatherRider:
    def __init__(self, shards, out_shapes, views):
        self.n = len(shards)
        self.views = views
        self.srcs = list(shards)
        self.out_shapes = list(out_shapes)
        n = self.n
        self.scratch = [pltpu.SemaphoreType.DMA((n, 7)), pltpu.SemaphoreType.DMA((n, 7)), pltpu.SemaphoreType.DMA((n,))]

    def _copies(self, ins, outs, sems):
        n = self.n
        send_sems, recv_sems, local_sems = sems
        x, y, c = lax.axis_index("x"), lax.axis_index("y"), lax.axis_index("c")
        me, sibling = (x, y, c), (x, y, 1 - c)
        chips = [(1 - x, y), (x, 1 - y), (1 - x, 1 - y)]

        def rows(a, blk):
            return self.views[a](outs[a], _dev_index(*blk))

        def copy(a, k, blk, to, src=None):
            return pltpu.make_async_remote_copy(
                src_ref=rows(a, blk) if src is None else src, dst_ref=rows(a, blk),
                send_sem=send_sems.at[a, k], recv_sem=recv_sems.at[a, k], device_id=to, device_id_type=MESH_ID)

        local = [pltpu.make_async_copy(ins[a], rows(a, me), local_sems.at[a]) for a in range(n)]
        first = []
        for a in range(n):
            first.append(copy(a, 0, me, sibling, src=ins[a]))
            first += [copy(a, 1 + j, me, (*chip, c), src=ins[a]) for j, chip in enumerate(chips)]
        over_ici = [[copy(a, 1 + j, (*chip, c), me) for a in range(n)] for j, chip in enumerate(chips)]
        passed = [[copy(a, 4 + j, (*chip, c), sibling) for a in range(n)] for j, chip in enumerate(chips)]
        from_sibling = [copy(a, 0, sibling, me) for a in range(n)]
        from_sibling += [copy(a, 4 + j, (*chip, 1 - c), me) for a in range(n) for j, chip in enumerate(chips)]
        return local, first, over_ici, passed, from_sibling

    def begin(self, ins, outs, sems):
        local, first, _, _, _ = self._copies(ins, outs, sems)
        for cp in local + first:
            cp.start()

    def middle(self, ins, outs, sems):
        _, _, over_ici, passed, _ = self._copies(ins, outs, sems)
        for arrived, onward in zip(over_ici, passed):
            for cp, fwd in zip(arrived, onward):
                cp.wait_recv()
                fwd.start()

    def end(self, ins, outs, sems):
        local, first, _, passed, from_sibling = self._copies(ins, outs, sems)
        for cp in from_sibling:
            cp.wait_recv()
        for cp in first + [fwd for onward in passed for fwd in onward]:
            cp.wait_send()
        for cp in local:
            cp.wait()


class _ScatterRider:
    _MASKS = [(kx, ky, kc) for kx in (0, 1) for ky in (0, 1) for kc in (0, 1)][1:]

    def __init__(self, srcs, out_shapes, src_of, dst_at):
        self.n = len(srcs)
        self.srcs = list(srcs)
        self.out_shapes = list(out_shapes)
        self.src_of = src_of
        self.dst_at = dst_at
        n = self.n
        self.scratch = [pltpu.SemaphoreType.DMA((n, 7)), pltpu.SemaphoreType.DMA((n, 7)), pltpu.SemaphoreType.DMA((n,))]

    def _copies(self, ins, outs, sems):
        send_sems, recv_sems, local_sems = sems
        x, y, c = lax.axis_index("x"), lax.axis_index("y"), lax.axis_index("c")
        my = _dev_index(x, y, c)
        peers = [(1 - x if kx else x, 1 - y if ky else y, 1 - c if kc else c) for kx, ky, kc in self._MASKS]

        def send(i, k, to):
            return pltpu.make_async_remote_copy(
                src_ref=self.src_of[i](ins[i], _dev_index(*to)), dst_ref=self.dst_at[i](outs[i], my),
                send_sem=send_sems.at[i, k], recv_sem=recv_sems.at[i, k], device_id=to, device_id_type=MESH_ID)

        def arrival(i, k, frm):
            slot = self.dst_at[i](outs[i], _dev_index(*frm))
            return pltpu.make_async_remote_copy(
                src_ref=slot, dst_ref=slot, send_sem=send_sems.at[i, k], recv_sem=recv_sems.at[i, k],
                device_id=frm, device_id_type=MESH_ID)

        local = [pltpu.make_async_copy(self.src_of[i](ins[i], my), self.dst_at[i](outs[i], my), local_sems.at[i])
                 for i in range(self.n)]
        sends = [send(i, k, to) for k, to in enumerate(peers) for i in range(self.n)]
        arrivals = [arrival(i, k, frm) for k, frm in enumerate(peers) for i in range(self.n)]
        return local, sends, arrivals

    def begin(self, ins, outs, sems):
        local, sends, _ = self._copies(ins, outs, sems)
        for cp in local + sends:
            cp.start()

    def middle(self, ins, outs, sems):
        pass

    def end(self, ins, outs, sems):
        local, sends, arrivals = self._copies(ins, outs, sems)
        for cp in arrivals:
            cp.wait_recv()
        for cp in sends:
            cp.wait_send()
        for cp in local:
            cp.wait()


def _comm_call(rider, name):
    k_in = len(rider.srcs)
    k_out = len(rider.out_shapes)

    def body(*refs):
        ins, outs, sems = refs[:k_in], refs[k_in:k_in + k_out], refs[k_in + k_out:]
        rider.begin(ins, outs, sems)
        rider.middle(ins, outs, sems)
        rider.end(ins, outs, sems)

    return pl.pallas_call(
        body, name=name, in_specs=[HBM_SPEC] * k_in, out_specs=[HBM_SPEC] * k_out, out_shape=rider.out_shapes,
        scratch_shapes=rider.scratch, compiler_params=pltpu.CompilerParams(has_side_effects=True),
    )(*rider.srcs)


def _pad_w_in(w):
    cols = lambda a, b: w[..., a:b]
    zeros = lambda n: jnp.zeros(w.shape[:-1] + (n,), w.dtype)
    pairs = FOX_HEADS // 2
    parts = [cols(0, 384), cols(416, 672)]
    for hp in range(pairs):
        parts += [cols(1824 + 2 * hp, 1826 + 2 * hp), zeros(6)]
    parts += [zeros(ROPE_LANE0 - 8 * pairs), cols(384, 416), zeros(LANES - ROPE_LANE0 - MLA_ROPE), cols(672, 1824)]
    return jnp.concatenate(parts, axis=-1)


def _unpad_w_in(g):
    cols = lambda a, b: g[..., a:b]
    rope0 = TAIL0 + ROPE_LANE0
    parts = [cols(0, 384), cols(rope0, rope0 + MLA_ROPE), cols(384, 640), cols(ZA, N_PAD)]
    parts += [cols(TAIL0 + 8 * hp, TAIL0 + 8 * hp + 2) for hp in range(FOX_HEADS // 2)]
    return jnp.concatenate(parts, axis=-1)


def _small_pack(w_q_b, w_kv_b):
    a = jnp.pad(w_q_b, ((0, 0), (0, 0), (0, LANES - w_q_b.shape[2])))
    b = jnp.pad(w_kv_b, ((0, 0), (0, 0), (0, LANES - w_kv_b.shape[2])))
    return jnp.concatenate([a, b], axis=1)


def _small_unpack(p, cq, ckv):
    return p[:, 0:MLA_Q_RANK, 0:cq], p[:, MLA_Q_RANK:, 0:ckv]


def _mla_weights(wsm):
    H = MLA_HEADS
    cq = H * (MLA_NOPE + MLA_ROPE) // N_DEV
    ckv = H * (MLA_NOPE + MLA_V) // N_DEV
    wq = wsm[:, 0:MLA_Q_RANK, 0:cq].transpose(1, 0, 2).reshape(MLA_Q_RANK, H, MLA_NOPE + MLA_ROPE)
    wq = jnp.pad(wq, ((0, 0), (0, 0), (0, HEAD_BLOCK - MLA_NOPE - MLA_ROPE))).reshape(MLA_Q_RANK, H * HEAD_BLOCK)
    wkv = wsm[:, MLA_Q_RANK:, 0:ckv].transpose(1, 0, 2).reshape(MLA_KV_RANK, H, MLA_NOPE + MLA_V)
    wk = jnp.pad(wkv[:, :, 0:MLA_NOPE], ((0, 0), (0, 0), (0, HEAD_BLOCK - MLA_NOPE))).reshape(MLA_KV_RANK, H * HEAD_BLOCK)
    wv = wkv[:, :, MLA_NOPE:].reshape(MLA_KV_RANK, H * MLA_V)
    return wq, jnp.concatenate([wk, wv], axis=1)


def _mla_grads_to_blocks(dwq, dwkv):
    H = MLA_HEADS
    gq = dwq.reshape(MLA_Q_RANK, H, HEAD_BLOCK)[:, :, 0:MLA_NOPE + MLA_ROPE].reshape(MLA_Q_RANK, N_DEV, -1)
    gk = dwkv[:, 0:H * HEAD_BLOCK].reshape(MLA_KV_RANK, H, HEAD_BLOCK)[:, :, 0:MLA_NOPE]
    gv = dwkv[:, H * HEAD_BLOCK:].reshape(MLA_KV_RANK, H, MLA_V)
    gkv = jnp.concatenate([gk, gv], axis=2).reshape(MLA_KV_RANK, N_DEV, -1)
    return _small_pack(gq.transpose(1, 0, 2), gkv.transpose(1, 0, 2)).astype(BF16)


def _pool_blockdiag(pool_w):
    z = jnp.zeros((POOL_GROUP, POOL_GROUP), pool_w.dtype)
    halves = [jnp.concatenate([jnp.concatenate([pool_w[2 * i], z], axis=1),
                               jnp.concatenate([z, pool_w[2 * i + 1]], axis=1)], axis=0) for i in range(2)]
    return jnp.stack(halves)


def _pool_blockdiag_t(dw):
    g = POOL_GROUP
    return jnp.stack([dw[0, 0:g, 0:g], dw[0, g:, g:], dw[1, 0:g, 0:g], dw[1, g:, g:]])


def _gate_row(b):
    return jnp.zeros((LANES,), b.dtype).at[_F_LANES].set(b).reshape(1, LANES)


_SMALL = ("ffn1_norm", "mix_norm", "q_a_norm", "kv_a_norm", "pool_w", "pool_scale", "fox_b_f", "ffn2_norm", "final_norm")


def _pack_small(tree):
    rows, recipe = [], []
    for name in _SMALL:
        a = tree[name]
        flat = a.reshape(-1)
        n = flat.shape[0]
        nrow = -(-n // (8 * LANES)) * 8
        flat = jnp.pad(flat, (0, nrow * LANES - n))
        rows.append(flat.reshape(nrow, LANES))
        recipe.append((name, a.shape, n, nrow))
    return jnp.concatenate(rows, axis=0), recipe


def _unpack_small(packed, recipe):
    out, r0 = {}, 0
    for name, shape, n, nrow in recipe:
        out[name] = packed[r0:r0 + nrow].reshape(-1)[0:n].reshape(shape)
        r0 += nrow
    return out


def _adam_small(packs, w, m, v, name):
    R = w.shape[0]

    def body(p_ref, w_ref, m_ref, v_ref, g_out, d_out, m_out, v_out):
        g = p_ref[0]
        for p in range(1, N_DEV):
            g = g + p_ref[p]
        delta, mn, vn = _adam_math(g, w_ref[...], m_ref[...], v_ref[...])
        g_out[...] = g
        d_out[...] = delta
        m_out[...] = mn
        v_out[...] = vn

    blk = _bs((R, LANES), lambda i: (0, 0))
    shp = jax.ShapeDtypeStruct((R, LANES), F32)
    return pl.pallas_call(
        body, name=name, grid=(1,),
        in_specs=[_bs((N_DEV, R, LANES), lambda i: (0, 0, 0)), blk, blk, blk],
        out_specs=[blk, blk, blk, blk], out_shape=[shp, shp, shp, shp],
        compiler_params=_params(("arbitrary",)),
    )(packs, w, m, v)


_GATHER_PLAN = {
    ("first", 0): (("ffn1_w_gu", 0), ("ffn1_w_down", 0)),
    ("ffn1_fwd", 0): (("w_in", 0), ("w_small", 0), ("w_out", 0), ("ffn2_w_down", 0)),
    ("mla_attn_fwd", 0): (("ffn2_w_gu", 0),),
    ("fox_attn_fwd", 0): (("ffn1_w_down", 1), ("w_in", 1), ("w_small", 1), ("w_out", 1)),
    ("ffn2_fwd", 0): (("ffn1_w_gu", 1),),
    ("mla_attn_fwd", 1): (("ffn2_w_gu", 1), ("ffn2_w_down", 1)),
}
_SCATTER_PLAN = {
    ("mla_attn_bwd", 1): (("ffn2_w_gu", 1), ("w_out", 1)),
    ("fox_attn_bwd", 1): (("ffn2_w_down", 1),),
    ("ffn1_bwd", 1): (("w_in", 1), ("w_small", 1)),
    ("ffn2_bwd", 0): (("ffn1_w_down", 1),),
    ("mla_attn_bwd", 0): (("ffn1_w_gu", 1), ("w_out", 0)),
    ("fox_attn_bwd", 0): (("ffn2_w_gu", 0), ("ffn2_w_down", 0)),
    ("ffn1_bwd_a", 0): (("w_small", 0),),
    ("ffn1_dwd", 0): (("w_in", 0),),
    ("ffn1_dwgu", 0): (("ffn1_w_down", 0),),
    ("ffn1_bwd_b", 0): (("ffn1_w_gu", 0),),
}
_SPLIT_BWD = (("ffn1", 0),)


class _Exchange:
    def __init__(self, shards, D, f_sh, r_in, r_out):
        self.shards = shards
        self.D, self.f_sh, self.r_in, self.r_out = D, f_sh, r_in, r_out
        self.weights, self.grads, self.recv = {}, {}, {}

    def _rows(self, kind):
        n = {"w_gu": 2 * self.f_sh, "w_down": self.f_sh}[kind]
        return lambda ref, p: ref.at[pl.ds(pl.multiple_of(p * n, 16), n)]

    def _gathered_shape(self, kind):
        D, f_sh = self.D, self.f_sh
        return {"w_gu": (N_DEV * 2 * f_sh, D), "w_down": (N_DEV * f_sh, D), "w_in": (N_DEV, self.r_in, N_PAD),
                "w_small": (N_DEV, MLA_Q_RANK + MLA_KV_RANK, LANES), "w_out": (N_DEV, self.r_out, D)}[kind]

    def _recv_shape(self, kind):
        D, f_sh = self.D, self.f_sh
        return {"w_gu": (N_DEV, 2 * f_sh, D), "w_down": (N_DEV, f_sh, D), "w_in": (N_DEV, self.r_in, N_IN),
                "w_small": (N_DEV, MLA_Q_RANK + MLA_KV_RANK, LANES), "w_out": (N_DEV, self.r_out, D)}[kind]

    @staticmethod
    def _kind(name):
        return name[5:] if name.startswith("ffn") else name

    def gather_rider(self, call, l):
        keys = _GATHER_PLAN.get((call, l))
        if not keys:
            return None
        by_dev = lambda ref, p: ref.at[p]
        shards, shapes, views = [], [], []
        for key in keys:
            kind = self._kind(key[0])
            shards.append(self.shards[key])
            shapes.append(jax.ShapeDtypeStruct(self._gathered_shape(kind), BF16))
            views.append(self._rows(kind) if kind in ("w_gu", "w_down") else by_dev)
        return _GatherRider(shards, shapes, views)

    def gathered(self, call, l, outs):
        for key, w in zip(_GATHER_PLAN.get((call, l), ()), outs):
            if self._kind(key[0]) in ("w_in", "w_out"):
                w = w.reshape((N_DEV * w.shape[1],) + w.shape[2:])
            self.weights[key] = w

    def scatter_rider(self, call, l, pack=None):
        keys = _SCATTER_PLAN.get((call, l), ())
        if not keys and pack is None:
            return None
        by_dev = lambda ref, p: ref.at[p]
        srcs, shapes, src_of = [], [], []
        for key in keys:
            kind = self._kind(key[0])
            srcs.append(self.grads[key])
            shapes.append(jax.ShapeDtypeStruct(self._recv_shape(kind), BF16))
            src_of.append(self._rows(kind) if kind in ("w_gu", "w_down") else by_dev)
        if pack is not None:
            srcs.append(pack)
            shapes.append(jax.ShapeDtypeStruct((N_DEV,) + pack.shape, pack.dtype))
            src_of.append(lambda ref, p: ref)
        return _ScatterRider(srcs, shapes, src_of, [by_dev] * len(srcs))

    def scattered(self, call, l, outs):
        for key, r in zip(_SCATTER_PLAN.get((call, l), ()), outs):
            self.recv[key] = r


def _local_step(x, target, ex, small):
    S, D = x.shape
    tabs = _rope_tables(S)
    nhp_a, nhp_c = MLA_HEADS // 2, FOX_HEADS // 2
    fox_scale = 1.0 / math.sqrt(FOX_HEAD_DIM)
    ex.gathered("first", 0, _comm_call(ex.gather_rider("first", 0), "gather_first"))
    saved = []
    for l in range(DEPTH):
        s = {}
        s["x0"] = x
        wgu1, wd1 = ex.weights[("ffn1_w_gu", l)], ex.weights[("ffn1_w_down", l)]
        (x1, s["h1"], s["gu1"]), got = _ffn_fwd_full(x, small["ffn1_norm"][l][None], wgu1, wd1, FFN_FWD_TOKENS,
                                                    f"ffn1_fwd_l{l}", rider=ex.gather_rider("ffn1_fwd", l))
        ex.gathered("ffn1_fwd", l, got)
        s["x1"] = x1
        w_in = ex.weights[("w_in", l)]
        s["h2"], za, zf, zkvt = _mix_in_fwd(x1, small["mix_norm"][l][None], w_in, f"mix_in_fwd_l{l}")
        s["za"], s["zf"], s["zkvt"] = za, zf, zkvt
        wq, wkv = _mla_weights(ex.weights[("w_small", l)])
        s["wq"], s["wkv"] = wq, wkv
        gq, gkv = small["q_a_norm"][l][None], small["kv_a_norm"][l][None]
        qf, kf, vm, kft, vmt = _mla_prep(za, gq, gkv, wq, wkv, tabs, f"mla_prep_l{l}")
        s["qf"], s["kf"], s["vm"], s["kft"] = qf, kf, vm, kft
        (ya, lse_a), got = _attn_fwd_t(qf, kf, vmt, nhp=nhp_a, dkb=LANES, qoff=0, koff=0, vtoff=0, scale=1.0,
                                       cum=None, cumT=None, name=f"mla_attn_fwd_l{l}",
                                       rider=ex.gather_rider("mla_attn_fwd", l))
        ex.gathered("mla_attn_fwd", l, got)
        s["ya"], s["lse_a"] = ya, lse_a
        b_row = _gate_row(small["fox_b_f"][l])
        s["b_row"] = b_row
        cum, cumT = _fox_prep(za, b_row, f"fox_prep_l{l}")
        s["cum"], s["cumT"] = cum, cumT
        (yc, lse_c), got = _attn_fwd_t(zf, zf, zkvt, nhp=nhp_c, dkb=64, qoff=0, koff=nhp_c, vtoff=nhp_c, scale=fox_scale,
                                       cum=cum, cumT=cumT, name=f"fox_attn_fwd_l{l}",
                                       rider=ex.gather_rider("fox_attn_fwd", l))
        ex.gathered("fox_attn_fwd", l, got)
        s["yc"], s["lse_c"] = yc, lse_c
        wbd = _pool_blockdiag(small["pool_w"][l]).astype(BF16)
        s["wbd"] = wbd
        psc = small["pool_scale"][l][None]
        yb, s["pd"] = _pool_fwd(za, wbd, psc, f"pool_fwd_l{l}")
        w_out = ex.weights[("w_out", l)]
        x2, s["ycat"] = _mix_out_fwd(x1, ya, yb, yc, w_out, f"mix_out_fwd_l{l}")
        s["x2"] = x2
        wgu2, wd2 = ex.weights[("ffn2_w_gu", l)], ex.weights[("ffn2_w_down", l)]
        (x, s["h3"], s["gu2"]), got = _ffn_fwd_full(x2, small["ffn2_norm"][l][None], wgu2, wd2, FFN_FWD_TOKENS,
                                                    f"ffn2_fwd_l{l}", rider=ex.gather_rider("ffn2_fwd", l))
        ex.gathered("ffn2_fwd", l, got)
        saved.append(s)

    dx, d_final, loss = _loss_head(x, small["final_norm"][None], target, "loss_head")

    small_grads = [None] * DEPTH
    for l in reversed(range(DEPTH)):
        s = saved[l]
        g = {}
        wgu2, wd2 = ex.weights[("ffn2_w_gu", l)], ex.weights[("ffn2_w_down", l)]
        dy3 = dx
        (dx, g["ffn2_norm"], dgu, act, dyh), got = _ffn_bwd_full(
            dy3, s["x2"], small["ffn2_norm"][l][None], s["gu2"], wgu2, wd2, FFN_BWD_TOKENS, f"ffn2_bwd_l{l}",
            rider=ex.scatter_rider("ffn2_bwd", l))
        ex.scattered("ffn2_bwd", l, got)
        ex.grads[("ffn2_w_gu", l)] = _ffn_weight_grad(dgu, s["h3"], f"ffn2_dwgu_l{l}")
        ex.grads[("ffn2_w_down", l)] = _ffn_weight_grad(act, dyh, f"ffn2_dwd_l{l}")

        w_out = ex.weights[("w_out", l)]
        dya, dyb, dyc, dvec_a, dvec_c = _mix_out_bwd(dx, w_out, s["ya"], s["yc"], f"mix_out_bwd_l{l}")
        dw_out = _mm_tn(s["ycat"][None], dx[None], 1, lambda p: 0, lambda p: 0, 1024, 1024, f"dwout_l{l}", ts=1024)[0]
        ex.grads[("w_out", l)] = dw_out.reshape(N_DEV, ex.r_out, D)

        (dqf, dkf, dvm), got = _attn_bwd_t(s["qf"], s["kf"], s["kft"], s["vm"], dya, s["lse_a"], dvec_a, nhp=nhp_a,
                                           dkb=LANES, qoff=0, koff=0, ktoff=0, voff=0, scale=1.0, cum=None, cumT=None,
                                           name=f"mla_attn_bwd_l{l}", rider=ex.scatter_rider("mla_attn_bwd", l))
        ex.scattered("mla_attn_bwd", l, got)
        zf = s["zf"]
        (dqc, dkc, dvc, dcq, dck), got = _attn_bwd_t(zf, zf, s["zkvt"], zf, dyc, s["lse_c"], dvec_c, nhp=nhp_c, dkb=64,
                                                     qoff=0, koff=nhp_c, ktoff=0, voff=2 * nhp_c, scale=fox_scale,
                                                     cum=s["cum"], cumT=s["cumT"], name=f"fox_attn_bwd_l{l}",
                                                     rider=ex.scatter_rider("fox_attn_bwd", l), grad_dtype=BF16)
        ex.scattered("fox_attn_bwd", l, got)
        dtail_f, db = _fox_prep_bwd(s["za"], s["b_row"], dcq, dck, f"fox_prep_bwd_l{l}")
        g["fox_b_f"] = db[0, _F_LANES]
        psc = small["pool_scale"][l][None]
        du, dwbd, dpsc = _pool_bwd(dyb, s["pd"], s["wbd"], psc, f"pool_bwd_l{l}")
        g["pool_w"] = _pool_blockdiag_t(dwbd)
        g["pool_scale"] = dpsc[0]
        gq, gkv = small["q_a_norm"][l][None], small["kv_a_norm"][l][None]
        dza, dtail_k, dwq, dwkv, dgq, dgkv = _mla_prep_bwd(s["za"], gq, gkv, s["wq"], s["wkv"], tabs, dqf, dkf, dvm,
                                                            f"mla_prep_bwd_l{l}")
        g["q_a_norm"], g["kv_a_norm"] = dgq[0], dgkv[0]
        ex.grads[("w_small", l)] = _mla_grads_to_blocks(dwq, dwkv)
        w_in = ex.weights[("w_in", l)]
        dx, g["mix_norm"], dz = _mix_in_bwd(dx, s["x1"], small["mix_norm"][l][None], dza, dtail_k, du, dtail_f,
                                            dqc, dkc, dvc, w_in, f"mix_in_bwd_l{l}")
        dw_in = _unpad_w_in(_mm_tn(s["h2"][None], dz[None], 1, lambda p: 0, lambda p: 0, 1024, 640, f"dwin_l{l}",
                                   ts=4096)[0])
        ex.grads[("w_in", l)] = dw_in.reshape(N_DEV, ex.r_in, N_IN)

        wgu1, wd1 = ex.weights[("ffn1_w_gu", l)], ex.weights[("ffn1_w_down", l)]
        dy1 = dx
        gam1 = small["ffn1_norm"][l][None]
        split = ("ffn1", l) in _SPLIT_BWD
        if split:
            (dgu, act, dyh), got = _ffn_bwd_full(dy1, None, None, s["gu1"], None, wd1, FFN_BWD_TOKENS, f"ffn1_bwd_a_l{l}",
                                                 phase="act", rider=ex.scatter_rider("ffn1_bwd_a", l))
            ex.scattered("ffn1_bwd_a", l, got)
        else:
            (dx, g["ffn1_norm"], dgu, act, dyh), got = _ffn_bwd_full(
                dy1, s["x0"], gam1, s["gu1"], wgu1, wd1, FFN_BWD_TOKENS, f"ffn1_bwd_l{l}",
                rider=ex.scatter_rider("ffn1_bwd", l))
            ex.scattered("ffn1_bwd", l, got)
        rider = ex.scatter_rider("ffn1_dwd", l)
        dwd = _ffn_weight_grad(act, dyh, f"ffn1_dwd_l{l}", rider=rider)
        if rider is not None:
            dwd, got = dwd
            ex.scattered("ffn1_dwd", l, got)
        ex.grads[("ffn1_w_down", l)] = dwd
        rider = ex.scatter_rider("ffn1_dwgu", l)
        dwgu = _ffn_weight_grad(dgu, s["h1"], f"ffn1_dwgu_l{l}", rider=rider)
        if rider is not None:
            dwgu, got = dwgu
            ex.scattered("ffn1_dwgu", l, got)
        ex.grads[("ffn1_w_gu", l)] = dwgu
        if split:
            (dx, g["ffn1_norm"]), got = _ffn_bwd_full(dy1, s["x0"], gam1, None, wgu1, None, FFN_BWD_TOKENS,
                                                     f"ffn1_bwd_b_l{l}", phase="in", dgu_in=dgu,
                                                     rider=ex.scatter_rider("ffn1_bwd_b", l))
            ex.scattered("ffn1_bwd_b", l, got)
        for k in ("ffn1_norm", "ffn2_norm", "mix_norm"):
            g[k] = g[k][0]
        small_grads[l] = g
    return loss, dx, small_grads, d_final[0]


_BIG = ("ffn1_w_gu", "ffn1_w_down", "w_in", "w_small", "w_out", "ffn2_w_gu", "ffn2_w_down")


def kernel(x, ffn1_norm, ffn1_w_gu, ffn1_w_down, mix_norm, w_in, q_a_norm, w_q_b, kv_a_norm, w_kv_b, pool_w, pool_scale, fox_b_f, w_out, ffn2_norm, ffn2_w_gu, ffn2_w_down, final_norm, loss_target, m_ffn1_norm, m_ffn1_w_gu, m_ffn1_w_down, m_mix_norm, m_w_in, m_q_a_norm, m_w_q_b, m_kv_a_norm, m_w_kv_b, m_pool_w, m_pool_scale, m_fox_b_f, m_w_out, m_ffn2_norm, m_ffn2_w_gu, m_ffn2_w_down, m_final_norm, v_ffn1_norm, v_ffn1_w_gu, v_ffn1_w_down, v_mix_norm, v_w_in, v_q_a_norm, v_w_q_b, v_kv_a_norm, v_w_kv_b, v_pool_w, v_pool_scale, v_fox_b_f, v_w_out, v_ffn2_norm, v_ffn2_w_gu, v_ffn2_w_down, v_final_norm):
    W = dict(ffn1_norm=ffn1_norm, ffn1_w_gu=ffn1_w_gu, ffn1_w_down=ffn1_w_down, mix_norm=mix_norm, w_in=w_in,
             q_a_norm=q_a_norm, w_q_b=w_q_b, kv_a_norm=kv_a_norm, w_kv_b=w_kv_b, pool_w=pool_w, pool_scale=pool_scale,
             fox_b_f=fox_b_f, w_out=w_out, ffn2_norm=ffn2_norm, ffn2_w_gu=ffn2_w_gu, ffn2_w_down=ffn2_w_down,
             final_norm=final_norm)
    M = dict(ffn1_norm=m_ffn1_norm, ffn1_w_gu=m_ffn1_w_gu, ffn1_w_down=m_ffn1_w_down, mix_norm=m_mix_norm, w_in=m_w_in,
             q_a_norm=m_q_a_norm, w_q_b=m_w_q_b, kv_a_norm=m_kv_a_norm, w_kv_b=m_w_kv_b, pool_w=m_pool_w,
             pool_scale=m_pool_scale, fox_b_f=m_fox_b_f, w_out=m_w_out, ffn2_norm=m_ffn2_norm, ffn2_w_gu=m_ffn2_w_gu,
             ffn2_w_down=m_ffn2_w_down, final_norm=m_final_norm)
    V = dict(ffn1_norm=v_ffn1_norm, ffn1_w_gu=v_ffn1_w_gu, ffn1_w_down=v_ffn1_w_down, mix_norm=v_mix_norm, w_in=v_w_in,
             q_a_norm=v_q_a_norm, w_q_b=v_w_q_b, kv_a_norm=v_kv_a_norm, w_kv_b=v_w_kv_b, pool_w=v_pool_w,
             pool_scale=v_pool_scale, fox_b_f=v_fox_b_f, w_out=v_w_out, ffn2_norm=v_ffn2_norm, ffn2_w_gu=v_ffn2_w_gu,
             ffn2_w_down=v_ffn2_w_down, final_norm=v_final_norm)
    L, D, n_sh = ffn1_w_gu.shape
    f_sh = ffn1_w_down.shape[1]
    assert n_sh == 2 * f_sh and L == DEPTH
    r_in, r_out = w_in.shape[1], w_out.shape[1]

    tr_in = lambda a: a.transpose(0, 2, 1)
    big_shards = dict(
        ffn1_w_gu=tr_in(ffn1_w_gu).astype(BF16), ffn1_w_down=ffn1_w_down.astype(BF16),
        w_in=_pad_w_in(w_in).astype(BF16), w_small=_small_pack(w_q_b, w_kv_b).astype(BF16), w_out=w_out.astype(BF16),
        ffn2_w_gu=tr_in(ffn2_w_gu).astype(BF16), ffn2_w_down=ffn2_w_down.astype(BF16))
    ex = _Exchange({(k, l): big_shards[k][l] for k in _BIG for l in range(L)}, D, f_sh, r_in, r_out)

    small = {k: W[k] for k in _SMALL}
    loss, dx, grads, d_final = _local_step(x[0], loss_target[0], ex, small)

    small_g = {k: jnp.stack([grads[l][k] for l in range(L)]) for k in _SMALL if k != "final_norm"}
    small_g["final_norm"] = d_final
    pack_g, recipe = _pack_small(small_g)
    n_small = pack_g.shape[0]
    loss_row = -(-n_small // 8) * 8
    pack_g = jnp.concatenate([pack_g, jnp.zeros((loss_row - n_small, LANES), F32), jnp.broadcast_to(loss, (8, LANES))],
                             axis=0)
    *got, packs = _comm_call(ex.scatter_rider("last", 0, pack=pack_g), "scatter_last")
    ex.scattered("last", 0, got)

    out = {}
    sm_w, sm_m, sm_v = (_small_pack(t["w_q_b"], t["w_kv_b"]) for t in (W, M, V))
    big = [("ffn1_w_gu", tr_in(W["ffn1_w_gu"]), tr_in(M["ffn1_w_gu"]), tr_in(V["ffn1_w_gu"]), f_sh),
           ("ffn1_w_down", W["ffn1_w_down"], M["ffn1_w_down"], V["ffn1_w_down"], 352),
           ("w_in", W["w_in"], M["w_in"], V["w_in"], 128),
           ("w_small", sm_w, sm_m, sm_v, 384),
           ("w_out", W["w_out"], M["w_out"], V["w_out"], 128),
           ("ffn2_w_gu", tr_in(W["ffn2_w_gu"]), tr_in(M["ffn2_w_gu"]), tr_in(V["ffn2_w_gu"]), f_sh),
           ("ffn2_w_down", W["ffn2_w_down"], M["ffn2_w_down"], V["ffn2_w_down"], 352)]
    for k, w_, m_, v_, tr in big:
        res = None
        for l in range(L):
            res = _adam_sum(ex.recv[(k, l)], w_, m_, v_, l, res, tr, f"adam_{k}_l{l}")
        if k == "w_small":
            cq, ckv = w_q_b.shape[2], w_kv_b.shape[2]
            parts = [_small_unpack(r, cq, ckv) for r in res]
            out["w_q_b"] = [p[0] for p in parts]
            out["w_kv_b"] = [p[1] for p in parts]
        elif k.endswith("w_gu"):
            out[k] = [tr_in(r) for r in res]
        else:
            out[k] = res

    pw, _ = _pack_small({k: W[k] for k in _SMALL})
    pm, _ = _pack_small({k: M[k] for k in _SMALL})
    pv, _ = _pack_small({k: V[k] for k in _SMALL})
    extra = ((0, loss_row + 8 - n_small), (0, 0))
    res = _adam_small(packs, jnp.pad(pw, extra), jnp.pad(pm, extra), jnp.pad(pv, extra), "adam_small")
    loss_total = res[0][loss_row, 0]
    small_out = [_unpack_small(r, recipe) for r in res]
    for k in _SMALL:
        out[k] = [t[k] for t in small_out]

    names = ["ffn1_norm", "ffn1_w_gu", "ffn1_w_down", "mix_norm", "w_in", "q_a_norm", "w_q_b", "kv_a_norm", "w_kv_b",
             "pool_w", "pool_scale", "fox_b_f", "w_out", "ffn2_norm", "ffn2_w_gu", "ffn2_w_down", "final_norm"]
    outs = [loss_total, dx[None]]
    for which in range(4):
        outs += [out[k][which] for k in names]
    return tuple(outs)
```

```python
import functools
import math

import numpy as np
import jax
import jax.numpy as jnp
from jax import lax
from jax.experimental import pallas as pl
from jax.experimental.pallas import tpu as pltpu

F32 = jnp.float32
BF16 = jnp.bfloat16
MESH_ID = pl.DeviceIdType.MESH

N_DEV = 8
EPS = 1e-6
DEPTH = 2

MLA_HEADS = 6
MLA_Q_RANK = 256
MLA_KV_RANK = 128
MLA_NOPE = 64
MLA_ROPE = 32
MLA_V = 64
ROPE_THETA = 10000.0
POOL_WINDOWS = (2, 4, 8, 16)
POOL_GROUP = 64
POOL_WIDTH = 256
FOX_HEADS = 6
FOX_HEAD_DIM = 64
N_IN = 1830

ADAM_LR = 0.001
ADAM_B1 = 0.9
ADAM_B2 = 0.999
ADAM_EPS = 1e-08
ADAM_WD = 0.01
ADAM_STEP = 10

LANES = 128
HEAD_BLOCK = 128
VMEM_LIMIT = 48 * 1024 * 1024
VMEM_LIMIT_BIG = 50 * 1024 * 1024
NEG = -1e30
ATTN_BLOCK = 1024

ZA = 768
ZF = 1152
N_PAD = ZA + ZF
TAIL0 = 640
ROPE_LANE0 = 64


def _f_lane(h):
    return 8 * (h // 2) + (h % 2)


_F_LANES = np.array([_f_lane(h) for h in range(FOX_HEADS)], np.int32)


def _dot(a, b):
    return jnp.dot(a, b, preferred_element_type=F32)


def _dot_nt(a, b):
    return lax.dot_general(a, b, (((1,), (1,)), ((), ())), preferred_element_type=F32)


def _dot_tn(a, b):
    return lax.dot_general(a, b, (((0,), (0,)), ((), ())), preferred_element_type=F32)


def _rms(x, gam):
    r = lax.rsqrt(jnp.mean(x * x, axis=-1, keepdims=True) + EPS)
    return x * r * gam


def _rms_bwd(dy, x, gam):
    r = lax.rsqrt(jnp.mean(x * x, axis=-1, keepdims=True) + EPS)
    xh = x * r
    dxh = dy * gam
    dx = r * (dxh - xh * jnp.mean(dxh * xh, axis=-1, keepdims=True))
    return dx, jnp.sum(dy * xh, axis=0, keepdims=True)


def _accum_out(ref, first, val):
    @pl.when(first)
    def _():
        ref[...] = val

    @pl.when(jnp.logical_not(first))
    def _():
        ref[...] += val


def _bs(shape, fn):
    return pl.BlockSpec(shape, fn)


def _params(dims, vmem=VMEM_LIMIT):
    return pltpu.CompilerParams(dimension_semantics=dims, vmem_limit_bytes=vmem)


def _tile(n, t):
    t = min(n, t)
    assert n % t == 0, (n, t)
    return t


HBM_SPEC = pl.BlockSpec(memory_space=pl.ANY)


def _call(body, *, name, grid, in_specs, out_specs, out_shape, scratch, dims, args, rider=None, vmem=VMEM_LIMIT):
    n_in, n_out = len(in_specs), len(out_specs)
    if rider is None:
        outs = pl.pallas_call(body, name=name, grid=grid, in_specs=in_specs, out_specs=out_specs, out_shape=out_shape,
                              scratch_shapes=scratch, compiler_params=_params(dims, vmem))(*args)
        return list(outs), []
    k_in, k_out, k_sem = len(rider.srcs), len(rider.out_shapes), len(rider.scratch)

    def riding(*refs):
        a, b, c, d = n_in, n_in + k_in, n_in + k_in + n_out, n_in + k_in + n_out + k_out
        rest = refs[d:]
        sems = rest[len(rest) - k_sem:]
        step = 0
        for i, g in enumerate(grid):
            step = step * g + pl.program_id(i)
        n_steps = math.prod(grid)

        @pl.when(step == 0)
        def _():
            rider.begin(refs[a:b], refs[c:d], sems)

        body(*refs[:a], *refs[b:c], *rest[:len(rest) - k_sem])

        @pl.when(step == (3 * n_steps) // 4)
        def _():
            rider.middle(refs[a:b], refs[c:d], sems)

        @pl.when(step == n_steps - 1)
        def _():
            rider.end(refs[a:b], refs[c:d], sems)

    outs = pl.pallas_call(
        riding, name=name, grid=grid, in_specs=list(in_specs) + [HBM_SPEC] * k_in,
        out_specs=list(out_specs) + [HBM_SPEC] * k_out, out_shape=list(out_shape) + list(rider.out_shapes),
        scratch_shapes=list(scratch) + list(rider.scratch),
        compiler_params=_params(("arbitrary",) * len(grid), vmem))(*args, *rider.srcs)
    return list(outs[:n_out]), list(outs[n_out:])


_ONCE = pl.Buffered(1)
_FF_CHUNKS = ((0, 1536), (1536, 2816))


def _ffn_fwd_full(x, gam, wgut, wd, tm, name, rider=None):
    S, D = x.shape
    F = wd.shape[0]
    tm = _tile(S, tm)
    chunks = _FF_CHUNKS if F == 2816 else ((0, F),)

    def body(x_ref, gam_ref, wgut_ref, wd_ref, xo_ref, h_ref, gu_ref):
        h = _rms(x_ref[...], gam_ref[...]).astype(BF16)
        h_ref[...] = h
        y = None
        for c0, c1 in chunks:
            g = _dot_nt(h, wgut_ref[c0:c1, :])
            u = _dot_nt(h, wgut_ref[F + c0:F + c1, :])
            gu_ref[:, c0:c1] = g.astype(BF16)
            gu_ref[:, F + c0:F + c1] = u.astype(BF16)
            a = (g * jax.nn.sigmoid(g) * u).astype(BF16)
            part = _dot(a, wd_ref[c0:c1, :])
            y = part if y is None else y + part
        xo_ref[...] = x_ref[...] + 0.5 * y

    row = lambda i: (i, 0)
    fix = lambda i: (0, 0)
    return _call(
        body, name=name, grid=(S // tm,),
        in_specs=[_bs((tm, D), row), _bs((1, D), fix), pl.BlockSpec((2 * F, D), fix, pipeline_mode=_ONCE),
                  pl.BlockSpec((F, D), fix, pipeline_mode=_ONCE)],
        out_specs=[_bs((tm, D), row), _bs((tm, D), row), _bs((tm, 2 * F), row)],
        out_shape=[jax.ShapeDtypeStruct((S, D), F32), jax.ShapeDtypeStruct((S, D), BF16),
                   jax.ShapeDtypeStruct((S, 2 * F), BF16)],
        scratch=[], dims=("parallel",), args=(x, gam, wgut, wd), rider=rider)


def _ffn_bwd_full(dy, x, gam, gu, wgut, wd, tm, name, phase="all", dgu_in=None, rider=None):
    S, D = dy.shape
    F = wd.shape[0] if wd is not None else wgut.shape[0] // 2
    tm = _tile(S, tm)
    chunks = _FF_CHUNKS if F == 2816 else ((0, F),)
    act, inp = phase in ("all", "act"), phase in ("all", "in")

    def body(*refs):
        refs = list(refs)
        dy_ref = refs.pop(0)
        x_ref, gam_ref = (refs.pop(0), refs.pop(0)) if inp else (None, None)
        gu_ref = refs.pop(0)
        wgut_ref = refs.pop(0) if inp else None
        wd_ref = refs.pop(0) if act else None
        if inp:
            dx_ref, dgam_ref = refs.pop(0), refs.pop(0)
        if act:
            dgu_ref, a_ref, dyh_ref = refs.pop(0), refs.pop(0), refs.pop(0)
            dyh = (0.5 * dy_ref[...]).astype(BF16)
            dyh_ref[...] = dyh
        dh = None
        for c0, c1 in chunks:
            if act:
                dg, du, a = _swiglu_bwd(_dot_nt(dyh, wd_ref[c0:c1, :]), gu_ref[:, c0:c1], gu_ref[:, F + c0:F + c1])
                dgu_ref[:, c0:c1] = dg
                dgu_ref[:, F + c0:F + c1] = du
                a_ref[:, c0:c1] = a
            else:
                dg, du = gu_ref[:, c0:c1], gu_ref[:, F + c0:F + c1]
            if inp:
                part = _dot(dg, wgut_ref[c0:c1, :]) + _dot(du, wgut_ref[F + c0:F + c1, :])
                dh = part if dh is None else dh + part
        if inp:
            dxn, dgam = _rms_bwd(dh, x_ref[...], gam_ref[...])
            dx_ref[...] = dy_ref[...] + dxn
            _accum_out(dgam_ref, pl.program_id(0) == 0, dgam)

    row = lambda i: (i, 0)
    fix = lambda i: (0, 0)
    in_specs, args = [_bs((tm, D), row)], [dy]
    if inp:
        in_specs += [_bs((tm, D), row), _bs((1, D), fix)]
        args += [x, gam]
    in_specs += [_bs((tm, 2 * F), row)]
    args += [gu if act else dgu_in]
    if inp:
        in_specs += [pl.BlockSpec((2 * F, D), fix, pipeline_mode=_ONCE)]
        args += [wgut]
    if act:
        in_specs += [pl.BlockSpec((F, D), fix, pipeline_mode=_ONCE)]
        args += [wd]
    out_specs, out_shape = [], []
    if inp:
        out_specs += [_bs((tm, D), row), _bs((1, D), fix)]
        out_shape += [jax.ShapeDtypeStruct((S, D), F32), jax.ShapeDtypeStruct((1, D), F32)]
    if act:
        out_specs += [_bs((tm, 2 * F), row), _bs((tm, F), row), _bs((tm, D), row)]
        out_shape += [jax.ShapeDtypeStruct((S, 2 * F), BF16), jax.ShapeDtypeStruct((S, F), BF16),
                      jax.ShapeDtypeStruct((S, D), BF16)]
    return _call(body, name=name, grid=(S // tm,), in_specs=in_specs, out_specs=out_specs, out_shape=out_shape,
                 scratch=[], dims=("arbitrary",), args=tuple(args), rider=rider, vmem=VMEM_LIMIT_BIG)


def _swiglu_bwd(da, g, u):
    g = g.astype(F32)
    u = u.astype(F32)
    sig = jax.nn.sigmoid(g)
    sl = g * sig
    dg = (da * u * (sig * (1.0 + g * (1.0 - sig)))).astype(BF16)
    return dg, (da * sl).astype(BF16), (sl * u).astype(BF16)


def _mm_tn(a, b, nb, a_of, b_of, tm, tn, name, rider=None, ts=512):
    _, S, M = a.shape
    N = b.shape[2]
    tm = _tile(M, tm)
    tn = _tile(N, tn)
    ts = _tile(S, ts)
    nk = S // ts

    def body(a_ref, b_ref, o_ref, acc):
        k = pl.program_id(3)

        @pl.when(k == 0)
        def _():
            acc[...] = jnp.zeros_like(acc)

        acc[...] += _dot_tn(a_ref[...].astype(BF16), b_ref[...].astype(BF16))

        @pl.when(k == nk - 1)
        def _():
            o_ref[...] = acc[...].astype(o_ref.dtype)

    (out,), extra = _call(
        body, name=name, grid=(nb, M // tm, N // tn, nk),
        in_specs=[_bs((None, ts, tm), lambda p, i, j, k: (a_of(p), k, i)),
                  _bs((None, ts, tn), lambda p, i, j, k: (b_of(p), k, j))],
        out_specs=[_bs((None, tm, tn), lambda p, i, j, k: (p, i, j))],
        out_shape=[jax.ShapeDtypeStruct((nb, M, N), BF16)],
        scratch=[pltpu.VMEM((tm, tn), F32)],
        dims=("parallel", "parallel", "parallel", "arbitrary"), args=(a, b), rider=rider)
    return (out, extra) if rider is not None else out


FFN_FWD_TOKENS = 512
FFN_BWD_TOKENS = 256
FFN_GRAD_ROWS = 1408


def _ffn_weight_grad(a, b, name, rider=None):
    M = a.shape[1]
    tm = max(t for t in range(LANES, FFN_GRAD_ROWS + 1, LANES) if M % t == 0)
    res = _mm_tn(a[None], b[None], 1, lambda p: 0, lambda p: 0, tm, b.shape[1], name, rider=rider, ts=2048)
    return (res[0][0], res[1]) if rider is not None else res[0]


def _mix_in_fwd(x, gam, w_in, name):
    S, D = x.shape
    tm = _tile(S, 512)
    nkv = (ZF - 384) // LANES

    def body(x_ref, gam_ref, w_ref, h_ref, za_ref, zf_ref, zt_ref):
        hb = _rms(x_ref[...], gam_ref[...]).astype(BF16)
        h_ref[...] = hb
        za_ref[...] = _dot(hb, w_ref[:, 0:ZA])
        zf = _dot(hb, w_ref[:, ZA:N_PAD])
        zf_ref[...] = zf.astype(BF16)
        for c in range(nkv):
            zt_ref[c * LANES:(c + 1) * LANES, :] = zf[:, 384 + c * LANES:384 + (c + 1) * LANES].T.astype(BF16)

    return pl.pallas_call(
        body, name=name, grid=(S // tm,),
        in_specs=[_bs((tm, D), lambda i: (i, 0)), _bs((1, D), lambda i: (0, 0)), _bs((D, N_PAD), lambda i: (0, 0))],
        out_specs=[_bs((tm, D), lambda i: (i, 0)), _bs((tm, ZA), lambda i: (i, 0)), _bs((tm, ZF), lambda i: (i, 0)),
                   _bs((nkv * LANES, tm), lambda i: (0, i))],
        out_shape=[jax.ShapeDtypeStruct((S, D), BF16), jax.ShapeDtypeStruct((S, ZA), F32),
                   jax.ShapeDtypeStruct((S, ZF), BF16), jax.ShapeDtypeStruct((nkv * LANES, S), BF16)],
        compiler_params=_params(("parallel",)),
    )(x, gam, w_in)


def _mix_in_bwd(dy, x, gam, dza_mla, dtail_k, du, dtail_f, dqf, dkf, dvf, w_in, name):
    S, D = x.shape
    tm = _tile(S, 512)

    def body(dy_ref, x_ref, gam_ref, dza_ref, dtk_ref, du_ref, dt_ref, dq_ref, dk_ref, dv_ref, w_ref,
             dx_ref, dgam_ref, dz_ref):
        i = pl.program_id(0)
        tail = (dtk_ref[...] + dt_ref[...]).astype(BF16)
        dz = jnp.concatenate([dza_ref[...], du_ref[...], tail, dq_ref[...], dk_ref[...], dv_ref[...]], axis=1)
        dz_ref[...] = dz
        dh = _dot_nt(dz, w_ref[...])
        dxn, dgam = _rms_bwd(dh, x_ref[...], gam_ref[...])
        dx_ref[...] = dy_ref[...] + dxn
        _accum_out(dgam_ref, i == 0, dgam)

    row = lambda i: (i, 0)
    fix = lambda i: (0, 0)
    return pl.pallas_call(
        body, name=name, grid=(S // tm,),
        in_specs=[_bs((tm, D), row), _bs((tm, D), row), _bs((1, D), fix), _bs((tm, 384), row), _bs((tm, 128), row),
                  _bs((tm, 256), row), _bs((tm, 128), row), _bs((tm, 384), row), _bs((tm, 384), row), _bs((tm, 384), row),
                  _bs((D, N_PAD), fix)],
        out_specs=[_bs((tm, D), row), _bs((1, D), fix), _bs((tm, N_PAD), row)],
        out_shape=[jax.ShapeDtypeStruct((S, D), F32), jax.ShapeDtypeStruct((1, D), F32),
                   jax.ShapeDtypeStruct((S, N_PAD), BF16)],
        compiler_params=_params(("arbitrary",)),
    )(dy, x, gam, dza_mla, dtail_k, du, dtail_f, dqf, dkf, dvf, w_in)


def _rope_tables(S):
    half = MLA_ROPE // 2
    inv_freq = ROPE_THETA ** (-jnp.arange(0, MLA_ROPE, 2, dtype=F32) / MLA_ROPE)
    ang = jnp.arange(S, dtype=jnp.int32).astype(F32)[:, None] * inv_freq[None, :]
    cos, sin = jnp.cos(ang), jnp.sin(ang)
    one = jnp.ones((S, ROPE_LANE0), F32)
    zero = jnp.zeros((S, ROPE_LANE0), F32)
    pad1 = jnp.ones((S, LANES - ROPE_LANE0 - MLA_ROPE), F32)
    pad0 = jnp.zeros((S, LANES - ROPE_LANE0 - MLA_ROPE), F32)
    zh = jnp.zeros((S, half), F32)
    tab_c = jnp.concatenate([one, cos, cos, pad1], axis=1)
    tab_ck = jnp.concatenate([zero, cos, cos, pad0], axis=1)
    tab_s1 = jnp.concatenate([zero, -sin, zh, pad0], axis=1)
    tab_s2 = jnp.concatenate([zero, zh, sin, pad0], axis=1)
    return tab_c, tab_ck, tab_s1, tab_s2


def _rope(x, c, s1, s2):
    return x * c + pltpu.roll(x, LANES - 16, 1) * s1 + pltpu.roll(x, 16, 1) * s2


def _rope_t(dy, c, s1, s2):
    return dy * c + pltpu.roll(dy * s1, 16, 1) + pltpu.roll(dy * s2, LANES - 16, 1)


_MLA_SCALE = 1.0 / math.sqrt(MLA_NOPE + MLA_ROPE)


def _mla_prep(za, gq, gkv, wq, wkv, tabs, name):
    S = za.shape[0]
    tm = _tile(S, 512)
    H = MLA_HEADS

    def body(zq_ref, tail_ref, gq_ref, gkv_ref, wq_ref, wkv_ref, c_ref, ck_ref, s1_ref, s2_ref,
             qf_ref, kf_ref, v_ref, kft_ref, vt_ref):
        zq = zq_ref[...]
        c, s1, s2 = c_ref[...], s1_ref[...], s2_ref[...]
        qn = _rms(zq[:, 0:256], gq_ref[...]).astype(BF16)
        q = _dot(qn, wq_ref[...])
        for h in range(H):
            blk = _rope(q[:, h * LANES:(h + 1) * LANES], c, s1, s2)
            qf_ref[:, h * LANES:(h + 1) * LANES] = (blk * _MLA_SCALE).astype(BF16)
        kvn = _rms(zq[:, 256:384], gkv_ref[...]).astype(BF16)
        kv = _dot(kvn, wkv_ref[...])
        kt = _rope(tail_ref[...], ck_ref[...], s1, s2)
        for h in range(H):
            sl = slice(h * LANES, (h + 1) * LANES)
            kblk = kv[:, sl] + kt
            kf_ref[:, sl] = kblk.astype(BF16)
            kft_ref[sl, :] = kblk.T.astype(BF16)
        v_ref[...] = kv[:, H * LANES:].astype(BF16)
        for cblk in range(H * MLA_V // LANES):
            sl = slice(cblk * LANES, (cblk + 1) * LANES)
            vt_ref[sl, :] = kv[:, H * LANES + cblk * LANES:H * LANES + (cblk + 1) * LANES].T.astype(BF16)

    row = lambda i: (i, 0)
    col = lambda i: (0, i)
    fix = lambda i: (0, 0)
    return pl.pallas_call(
        body, name=name, grid=(S // tm,),
        in_specs=[_bs((tm, 384), row), _bs((tm, 128), lambda i: (i, TAIL0 // 128)), _bs((1, 256), fix), _bs((1, 128), fix),
                  _bs((256, 768), fix), _bs((128, 1152), fix),
                  _bs((tm, 128), row), _bs((tm, 128), row), _bs((tm, 128), row), _bs((tm, 128), row)],
        out_specs=[_bs((tm, 768), row), _bs((tm, 768), row), _bs((tm, 384), row), _bs((768, tm), col), _bs((384, tm), col)],
        out_shape=[jax.ShapeDtypeStruct((S, 768), BF16), jax.ShapeDtypeStruct((S, 768), BF16),
                   jax.ShapeDtypeStruct((S, 384), BF16), jax.ShapeDtypeStruct((768, S), BF16),
                   jax.ShapeDtypeStruct((384, S), BF16)],
        compiler_params=_params(("parallel",)),
    )(za, za, gq, gkv, wq, wkv, *tabs)


def _mix_in_mla_fwd(x, gam, w_in, gq, gkv, wq, wkv, tabs, name):
    S, D = x.shape
    tm = _tile(S, 512)
    nkv = (ZF - 384) // LANES
    H = MLA_HEADS

    def body(x_ref, gam_ref, w_ref, gq_ref, gkv_ref, wq_ref, wkv_ref, c_ref, ck_ref, s1_ref, s2_ref,
             h_ref, za_ref, zf_ref, zt_ref, qf_ref, kf_ref, v_ref, kft_ref, vt_ref):
        hb = _rms(x_ref[...], gam_ref[...]).astype(BF16)
        h_ref[...] = hb
        za = _dot(hb, w_ref[:, 0:ZA])
        za_ref[...] = za
        zf = _dot(hb, w_ref[:, ZA:N_PAD])
        zf_ref[...] = zf.astype(BF16)
        for c in range(nkv):
            zt_ref[c * LANES:(c + 1) * LANES, :] = zf[:, 384 + c * LANES:384 + (c + 1) * LANES].T.astype(BF16)

        c, s1, s2 = c_ref[...], s1_ref[...], s2_ref[...]
        qn = _rms(za[:, 0:256], gq_ref[...]).astype(BF16)
        q = _dot(qn, wq_ref[...])
        for h in range(H):
            blk = _rope(q[:, h * LANES:(h + 1) * LANES], c, s1, s2)
            qf_ref[:, h * LANES:(h + 1) * LANES] = (blk * _MLA_SCALE).astype(BF16)
        kvn = _rms(za[:, 256:384], gkv_ref[...]).astype(BF16)
        kv = _dot(kvn, wkv_ref[...])
        kt = _rope(za[:, TAIL0:ZA], ck_ref[...], s1, s2)
        for h in range(H):
            sl = slice(h * LANES, (h + 1) * LANES)
            kblk = kv[:, sl] + kt
            kf_ref[:, sl] = kblk.astype(BF16)
            kft_ref[sl, :] = kblk.T.astype(BF16)
        v_ref[...] = kv[:, H * LANES:].astype(BF16)
        for cblk in range(H * MLA_V // LANES):
            sl = slice(cblk * LANES, (cblk + 1) * LANES)
            vt_ref[sl, :] = kv[:, H * LANES + cblk * LANES:H * LANES + (cblk + 1) * LANES].T.astype(BF16)

    row = lambda i: (i, 0)
    col = lambda i: (0, i)
    fix = lambda i: (0, 0)
    return pl.pallas_call(
        body, name=name, grid=(S // tm,),
        in_specs=[_bs((tm, D), row), _bs((1, D), fix), _bs((D, N_PAD), fix), _bs((1, 256), fix), _bs((1, 128), fix),
                  _bs((256, 768), fix), _bs((128, 1152), fix),
                  _bs((tm, 128), row), _bs((tm, 128), row), _bs((tm, 128), row), _bs((tm, 128), row)],
        out_specs=[_bs((tm, D), row), _bs((tm, ZA), row), _bs((tm, ZF), row), _bs((nkv * LANES, tm), col),
                   _bs((tm, 768), row), _bs((tm, 768), row), _bs((tm, 384), row), _bs((768, tm), col), _bs((384, tm), col)],
        out_shape=[jax.ShapeDtypeStruct((S, D), BF16), jax.ShapeDtypeStruct((S, ZA), F32),
                   jax.ShapeDtypeStruct((S, ZF), BF16), jax.ShapeDtypeStruct((nkv * LANES, S), BF16),
                   jax.ShapeDtypeStruct((S, 768), BF16), jax.ShapeDtypeStruct((S, 768), BF16),
                   jax.ShapeDtypeStruct((S, 384), BF16), jax.ShapeDtypeStruct((768, S), BF16),
                   jax.ShapeDtypeStruct((384, S), BF16)],
        compiler_params=_params(("parallel",)),
    )(x, gam, w_in, gq, gkv, wq, wkv, *tabs)


def _mla_prep_bwd(za, gq, gkv, wq, wkv, tabs, dqf, dkf, dvm, name):
    S = za.shape[0]
    tm = _tile(S, 512)
    H = MLA_HEADS

    def body(zq_ref, gq_ref, gkv_ref, wq_ref, wkv_ref, c_ref, ck_ref, s1_ref, s2_ref, dqf_ref, dkf_ref, dvm_ref,
             dza_ref, dtail_ref, dwq_ref, dwkv_ref, dgq_ref, dgkv_ref):
        i = pl.program_id(0)
        first = i == 0
        zq = zq_ref[...]
        c, s1, s2 = c_ref[...], s1_ref[...], s2_ref[...]
        lane = lax.broadcasted_iota(jnp.int32, (1, LANES), 1)
        nope = lane < MLA_NOPE
        rope = jnp.logical_and(lane >= ROPE_LANE0, lane < ROPE_LANE0 + MLA_ROPE)

        qa = zq[:, 0:256]
        qn = _rms(qa, gq_ref[...]).astype(BF16)
        dqf = dqf_ref[...]
        dq_pre = jnp.concatenate(
            [_rope_t(dqf[:, h * LANES:(h + 1) * LANES] * _MLA_SCALE, c, s1, s2) for h in range(H)], axis=1).astype(BF16)
        _accum_out(dwq_ref, first, _dot_tn(qn, dq_pre))
        dqa, dgq = _rms_bwd(_dot_nt(dq_pre, wq_ref[...]), qa, gq_ref[...])
        _accum_out(dgq_ref, first, dgq)

        kva = zq[:, 256:384]
        kvn = _rms(kva, gkv_ref[...]).astype(BF16)
        dkf = dkf_ref[...]
        parts = []
        dkt = jnp.zeros((tm, LANES), F32)
        for h in range(H):
            blk = dkf[:, h * LANES:(h + 1) * LANES]
            parts.append(jnp.where(nope, blk, 0.0))
            dkt = dkt + jnp.where(rope, blk, 0.0)
        dkv_pre = jnp.concatenate(parts + [dvm_ref[...]], axis=1).astype(BF16)
        _accum_out(dwkv_ref, first, _dot_tn(kvn, dkv_pre))
        dkva, dgkv = _rms_bwd(_dot_nt(dkv_pre, wkv_ref[...]), kva, gkv_ref[...])
        _accum_out(dgkv_ref, first, dgkv)

        dtail_ref[...] = _rope_t(dkt, ck_ref[...], s1, s2)
        dza_ref[...] = jnp.concatenate([dqa, dkva], axis=1).astype(BF16)

    row = lambda i: (i, 0)
    fix = lambda i: (0, 0)
    return pl.pallas_call(
        body, name=name, grid=(S // tm,),
        in_specs=[_bs((tm, 384), row), _bs((1, 256), fix), _bs((1, 128), fix), _bs((256, 768), fix), _bs((128, 1152), fix),
                  _bs((tm, 128), row), _bs((tm, 128), row), _bs((tm, 128), row), _bs((tm, 128), row),
                  _bs((tm, 768), row), _bs((tm, 768), row), _bs((tm, 384), row)],
        out_specs=[_bs((tm, 384), row), _bs((tm, LANES), row), _bs((256, 768), fix), _bs((128, 1152), fix),
                   _bs((1, 256), fix), _bs((1, 128), fix)],
        out_shape=[jax.ShapeDtypeStruct((S, 384), BF16), jax.ShapeDtypeStruct((S, LANES), F32),
                   jax.ShapeDtypeStruct((256, 768), F32),
                   jax.ShapeDtypeStruct((128, 1152), F32), jax.ShapeDtypeStruct((1, 256), F32),
                   jax.ShapeDtypeStruct((1, 128), F32)],
        compiler_params=_params(("arbitrary",)),
    )(za, gq, gkv, wq, wkv, *tabs, dqf, dkf, dvm)


def _head_views(qb, kb, r, dkb, sel):
    if dkb == LANES:
        sl = slice(r * LANES, (r + 1) * LANES)
        return qb[:, sl], kb[:, sl], kb[:, sl]
    return jnp.where(sel, qb, jnp.zeros_like(qb)), kb, jnp.where(sel, kb, jnp.zeros_like(kb))


def _attn_fwd_t(q_arr, k_arr, vt_arr, *, nhp, dkb, qoff, koff, vtoff, scale, cum, cumT, name, rider=None):
    S = q_arr.shape[0]
    T = _tile(S, ATTN_BLOCK)
    nq = S // T
    W = 2 * dkb
    bias = cum is not None

    def body(*refs):
        if bias:
            q_ref, k_ref, vt_ref, cq_ref, ck_ref, o_ref, lse_ref, m_s, l_s, acc_s = refs
        else:
            q_ref, k_ref, vt_ref, o_ref, lse_ref, m_s, l_s, acc_s = refs
        hp, qi, ki = pl.program_id(0), pl.program_id(1), pl.program_id(2)
        lo_lane = lax.broadcasted_iota(jnp.int32, (1, LANES), 1) < 64
        lo_row = lax.broadcasted_iota(jnp.int32, (LANES, 1), 0) < 64

        @pl.when(ki == 0)
        def _():
            m_s[...] = jnp.full_like(m_s, NEG)
            l_s[...] = jnp.zeros_like(l_s)
            acc_s[...] = jnp.zeros_like(acc_s)

        def step(masked):
            qb, kb, vtb = q_ref[...], k_ref[...], vt_ref[...]
            if masked:
                mask = lax.broadcasted_iota(jnp.int32, (T, T), 0) <= lax.broadcasted_iota(jnp.int32, (T, T), 1)
            if bias:
                li = lax.broadcasted_iota(jnp.int32, (T, LANES), 1)
                ckb = ck_ref[...]
            m_all, l_all = m_s[...], l_s[...]
            scores = []
            for r in range(2):
                sel = lo_lane if r == 0 else jnp.logical_not(lo_lane)
                q, k, _ = _head_views(qb, kb, r, dkb, sel)
                if scale != 1.0:
                    q = q * jnp.asarray(scale, q.dtype)
                scores.append(_dot_nt(k, q))
            m_out, l_out, alphas, pvs = [], [], [], []
            for r in range(2):
                rsel = lo_row if r == 0 else jnp.logical_not(lo_row)
                s = scores[r]
                if bias:
                    ck = jnp.sum(jnp.where(li == 8 * hp + r, ckb, 0.0), axis=1, keepdims=True)
                    s = s + (cq_ref[r:r + 1, :] - ck)
                if masked:
                    s = jnp.where(mask, s, NEG)
                m_prev = m_all[r:r + 1, :]
                m_new = jnp.maximum(m_prev, jnp.max(s, axis=0, keepdims=True))
                alpha = jnp.exp(m_prev - m_new)
                p = jnp.exp(s - m_new)
                m_out.append(m_new)
                l_out.append(alpha * l_all[r:r + 1, :] + jnp.sum(p, axis=0, keepdims=True))
                alphas.append(alpha)
                pvs.append(_dot(jnp.where(rsel, vtb, jnp.zeros_like(vtb)), p.astype(BF16)))
            m_s[0:1, :] = m_out[0]
            m_s[1:2, :] = m_out[1]
            l_s[0:1, :] = l_out[0]
            l_s[1:2, :] = l_out[1]
            acc_s[...] = acc_s[...] * jnp.where(lo_row, alphas[0], alphas[1]) + (pvs[0] + pvs[1])

        @pl.when(ki < qi)
        def _():
            step(False)

        @pl.when(ki == qi)
        def _():
            step(True)

        @pl.when(ki == nq - 1)
        def _():
            inv = jnp.where(lo_row, 1.0 / l_s[0:1, :], 1.0 / l_s[1:2, :])
            o_ref[...] = (acc_s[...] * inv).T.astype(BF16)
            used = lax.broadcasted_iota(jnp.int32, (8, T), 0) < 2
            lse_ref[...] = jnp.where(used, m_s[...] + jnp.log(jnp.where(used, l_s[...], 1.0)), 0.0)

    kmap = lambda hp, qi, ki: jnp.minimum(ki, qi)
    in_specs = [_bs((T, W), lambda hp, qi, ki: (qi, qoff + hp)),
                _bs((T, W), lambda hp, qi, ki: (kmap(hp, qi, ki), koff + hp)),
                _bs((LANES, T), lambda hp, qi, ki: (vtoff + hp, kmap(hp, qi, ki)))]
    args = [q_arr, k_arr, vt_arr]
    if bias:
        in_specs += [_bs((8, T), lambda hp, qi, ki: (hp, qi)), _bs((T, LANES), lambda hp, qi, ki: (kmap(hp, qi, ki), 0))]
        args += [cumT, cum]
    return _call(
        body, name=name, grid=(nhp, nq, nq),
        in_specs=in_specs,
        out_specs=[_bs((T, LANES), lambda hp, qi, ki: (qi, hp)), _bs((None, 8, T), lambda hp, qi, ki: (hp, 0, qi))],
        out_shape=[jax.ShapeDtypeStruct((S, nhp * LANES), BF16), jax.ShapeDtypeStruct((nhp, 8, S), F32)],
        scratch=[pltpu.VMEM((8, T), F32), pltpu.VMEM((8, T), F32), pltpu.VMEM((LANES, T), F32)],
        dims=("parallel", "parallel", "arbitrary"), args=args, rider=rider)


def _attn_bwd_t(q_arr, k_arr, kt_arr, v_arr, do_arr, lse, dvec, *, nhp, dkb, qoff, koff, ktoff, voff, scale, cum, cumT,
                name, rider=None, grad_dtype=F32):
    S = q_arr.shape[0]
    T = _tile(S, ATTN_BLOCK)
    nq = S // T
    W = 2 * dkb
    bias = cum is not None

    def body(*refs):
        if bias:
            (q_ref, k_ref, kt_ref, v_ref, do_ref, lse_ref, dvec_ref, cq_ref, ck_ref,
             dq_ref, dk_ref, dv_ref, dcq_ref, dck_ref, dqt_s, dk_s, dv_s, dcq_s, dck_s) = refs
        else:
            (q_ref, k_ref, kt_ref, v_ref, do_ref, lse_ref, dvec_ref,
             dq_ref, dk_ref, dv_ref, dqt_s, dk_s, dv_s) = refs
        hp, ki, qi = pl.program_id(0), pl.program_id(1), pl.program_id(2)
        lo_lane = lax.broadcasted_iota(jnp.int32, (1, LANES), 1) < 64
        lo_row = lax.broadcasted_iota(jnp.int32, (LANES, 1), 0) < 64

        @pl.when(jnp.logical_and(ki == 0, qi == 0))
        def _():
            dqt_s[...] = jnp.zeros_like(dqt_s)
            if bias:
                dcq_s[...] = jnp.zeros_like(dcq_s)

        @pl.when(qi == 0)
        def _():
            dk_s[...] = jnp.zeros_like(dk_s)
            dv_s[...] = jnp.zeros_like(dv_s)
            if bias:
                dck_s[...] = jnp.zeros_like(dck_s)

        def step(masked):
            qb, kb, ktb, vb, dob = q_ref[...], k_ref[...], kt_ref[...], v_ref[...], do_ref[...]
            if masked:
                mask = lax.broadcasted_iota(jnp.int32, (T, T), 0) <= lax.broadcasted_iota(jnp.int32, (T, T), 1)
            if bias:
                li = lax.broadcasted_iota(jnp.int32, (T, LANES), 1)
                ckb = ck_ref[...]
            for r in range(2):
                sel = lo_lane if r == 0 else jnp.logical_not(lo_lane)
                rsel = lo_row if r == 0 else jnp.logical_not(lo_row)
                q, k, _ = _head_views(qb, kb, r, dkb, sel)
                if scale != 1.0:
                    q = q * jnp.asarray(scale, q.dtype)
                s = _dot_nt(k, q)
                if bias:
                    ck = jnp.sum(jnp.where(li == 8 * hp + r, ckb, 0.0), axis=1, keepdims=True)
                    s = s + (cq_ref[r:r + 1, :] - ck)
                p = jnp.exp(s - lse_ref[r:r + 1, :])
                if masked:
                    p = jnp.where(mask, p, 0.0)
                do_r = jnp.where(sel, dob, jnp.zeros_like(dob))
                dp = _dot_nt(vb, do_r)
                ds = p * (dp - dvec_ref[r:r + 1, :])
                pb = p.astype(BF16)
                dsb = ds.astype(BF16)
                dv_s[...] += _dot(pb, do_r)
                if dkb == LANES:
                    sl = slice(r * LANES, (r + 1) * LANES)
                    dk_s[:, sl] += _dot(dsb, q)
                    dqt_s[qi, sl, :] += _dot(ktb[sl, :], dsb) * scale
                else:
                    dk_s[...] += _dot(dsb, q)
                    dqt_s[qi] += _dot(jnp.where(rsel, ktb, jnp.zeros_like(ktb)), dsb) * scale
                if bias:
                    dcq_s[qi, r:r + 1, :] += jnp.sum(ds, axis=0, keepdims=True)
                    dck_s[...] -= jnp.where(li == 8 * hp + r, jnp.sum(ds, axis=1, keepdims=True), 0.0)

        @pl.when(qi > ki)
        def _():
            step(False)

        @pl.when(qi == ki)
        def _():
            step(True)

        @pl.when(qi == nq - 1)
        def _():
            dk_ref[...] = dk_s[...].astype(grad_dtype)
            dv_ref[...] = dv_s[...].astype(grad_dtype)
            if bias:
                dck_ref[...] = dck_s[...]

        @pl.when(jnp.logical_and(ki == nq - 1, qi == nq - 1))
        def _():
            for c in range(nq):
                dq_ref[c * T:(c + 1) * T, :] = dqt_s[c].T.astype(grad_dtype)
                if bias:
                    dcq_ref[:, c * T:(c + 1) * T] = dcq_s[c]

    qmap = lambda hp, ki, qi: jnp.maximum(qi, ki)
    in_specs = [_bs((T, W), lambda hp, ki, qi: (qmap(hp, ki, qi), qoff + hp)),
                _bs((T, W), lambda hp, ki, qi: (ki, koff + hp)),
                _bs((W, T), lambda hp, ki, qi: (ktoff + hp, ki)),
                _bs((T, LANES), lambda hp, ki, qi: (ki, voff + hp)),
                _bs((T, LANES), lambda hp, ki, qi: (qmap(hp, ki, qi), hp)),
                _bs((None, 8, T), lambda hp, ki, qi: (hp, 0, qmap(hp, ki, qi))),
                _bs((None, 8, T), lambda hp, ki, qi: (hp, 0, qmap(hp, ki, qi)))]
    args = [q_arr, k_arr, kt_arr, v_arr, do_arr, lse, dvec]
    out_specs = [_bs((S, W), lambda hp, ki, qi: (0, hp)), _bs((T, W), lambda hp, ki, qi: (ki, hp)),
                 _bs((T, LANES), lambda hp, ki, qi: (ki, hp))]
    out_shape = [jax.ShapeDtypeStruct((S, nhp * W), grad_dtype), jax.ShapeDtypeStruct((S, nhp * W), grad_dtype),
                 jax.ShapeDtypeStruct((S, nhp * LANES), grad_dtype)]
    scratch = [pltpu.VMEM((nq, W, T), F32), pltpu.VMEM((T, W), F32), pltpu.VMEM((T, LANES), F32)]
    if bias:
        in_specs += [_bs((8, T), lambda hp, ki, qi: (hp, qmap(hp, ki, qi))), _bs((T, LANES), lambda hp, ki, qi: (ki, 0))]
        args += [cumT, cum]
        out_specs += [_bs((None, 8, S), lambda hp, ki, qi: (hp, 0, 0)), _bs((None, T, LANES), lambda hp, ki, qi: (hp, ki, 0))]
        out_shape += [jax.ShapeDtypeStruct((nhp, 8, S), F32), jax.ShapeDtypeStruct((nhp, S, LANES), F32)]
        scratch += [pltpu.VMEM((nq, 8, T), F32), pltpu.VMEM((T, LANES), F32)]
    return _call(body, name=name, grid=(nhp, nq, nq), in_specs=in_specs, out_specs=out_specs, out_shape=out_shape,
                 scratch=scratch, dims=("arbitrary", "arbitrary", "arbitrary"), args=args, rider=rider)


def _gate_lanes(shape):
    lane = lax.broadcasted_iota(jnp.int32, shape, 1)
    return jnp.logical_and(lane < 8 * (FOX_HEADS // 2), lane % 8 < 2)


def _fox_prep(za, b_row, name):
    S = za.shape[0]
    nrow = 8 * (FOX_HEADS // 2)

    def body(tail_ref, b_ref, cum_ref, cumt_ref):
        x = tail_ref[...] + b_ref[...]
        logf = jnp.minimum(x, 0.0) - jnp.log(1.0 + jnp.exp(-jnp.abs(x)))
        y = jnp.where(_gate_lanes((S, LANES)), logf, 0.0)
        row = lax.broadcasted_iota(jnp.int32, (S, LANES), 0)
        k = 1
        while k < S:
            y = y + jnp.where(row >= k, pltpu.roll(y, k, 0), 0.0)
            k *= 2
        cum_ref[...] = y
        cumt_ref[...] = y.T[0:nrow, :]

    return pl.pallas_call(
        body, name=name, grid=(1,),
        in_specs=[_bs((S, LANES), lambda i: (0, TAIL0 // LANES)), _bs((1, LANES), lambda i: (0, 0))],
        out_specs=[_bs((S, LANES), lambda i: (0, 0)), _bs((nrow, S), lambda i: (0, 0))],
        out_shape=[jax.ShapeDtypeStruct((S, LANES), F32), jax.ShapeDtypeStruct((nrow, S), F32)],
        compiler_params=_params(("arbitrary",)),
    )(za, b_row)


def _fox_prep_bwd(za, b_row, dcq, dck, name):
    S = za.shape[0]
    nhp = FOX_HEADS // 2
    nrow = 8 * nhp
    dcq2 = dcq.reshape(nrow, S)

    def body(tail_ref, b_ref, dcq_ref, dck_ref, dt_ref, db_ref):
        x = tail_ref[...] + b_ref[...]
        d = jnp.concatenate([dcq_ref[...], jnp.zeros((LANES - nrow, S), F32)], axis=0).T
        for hp in range(nhp):
            d = d + dck_ref[hp]
        row = lax.broadcasted_iota(jnp.int32, (S, LANES), 0)
        k = 1
        while k < S:
            d = d + jnp.where(row < S - k, pltpu.roll(d, S - k, 0), 0.0)
            k *= 2
        df = jnp.where(_gate_lanes((S, LANES)), d * jax.nn.sigmoid(-x), 0.0)
        dt_ref[...] = df
        db_ref[...] = jnp.sum(df, axis=0, keepdims=True)

    return pl.pallas_call(
        body, name=name, grid=(1,),
        in_specs=[_bs((S, LANES), lambda i: (0, TAIL0 // LANES)), _bs((1, LANES), lambda i: (0, 0)),
                  _bs((nrow, S), lambda i: (0, 0)), _bs((nhp, S, LANES), lambda i: (0, 0, 0))],
        out_specs=[_bs((S, LANES), lambda i: (0, 0)), _bs((1, LANES), lambda i: (0, 0))],
        out_shape=[jax.ShapeDtypeStruct((S, LANES), F32), jax.ShapeDtypeStruct((1, LANES), F32)],
        compiler_params=_params(("arbitrary",)),
    )(za, b_row, dcq2, dck)


def _pool_select(half, lane_lo, vals):
    return jnp.where(lane_lo, jnp.where(half == 0, vals[0], vals[2]), jnp.where(half == 0, vals[1], vals[3]))


def _pool_den(S, half, lane_lo):
    cnt = (lax.broadcasted_iota(jnp.int32, (S, LANES), 0) + 1).astype(F32)
    w = _pool_select(half, lane_lo, [float(x) for x in POOL_WINDOWS])
    return jnp.minimum(cnt, w)


def _pool_fwd(za, wbd, scale, name):
    S = za.shape[0]

    def body(u_ref, w_ref, sc_ref, y_ref, pd_ref):
        half = pl.program_id(0)
        u = u_ref[...]
        row = lax.broadcasted_iota(jnp.int32, (S, LANES), 0)
        lane_lo = lax.broadcasted_iota(jnp.int32, (S, LANES), 1) < POOL_GROUP
        sums = []
        acc = u
        k = 1
        while k < POOL_WINDOWS[-1]:
            acc = acc + jnp.where(row >= k, pltpu.roll(acc, k, 0), 0.0)
            sums.append(acc)
            k *= 2
        pooled = _pool_select(half, lane_lo, sums) / _pool_den(S, half, lane_lo)
        pd = (pooled - u).astype(BF16)
        pd_ref[...] = pd
        y_ref[...] = (_dot(pd, w_ref[...]) * sc_ref[...]).astype(BF16)

    return pl.pallas_call(
        body, name=name, grid=(2,),
        in_specs=[_bs((S, LANES), lambda i: (0, 384 // LANES + i)), _bs((None, LANES, LANES), lambda i: (i, 0, 0)),
                  _bs((1, LANES), lambda i: (0, i))],
        out_specs=[_bs((S, LANES), lambda i: (0, i)), _bs((S, LANES), lambda i: (0, i))],
        out_shape=[jax.ShapeDtypeStruct((S, POOL_WIDTH), BF16), jax.ShapeDtypeStruct((S, POOL_WIDTH), BF16)],
        compiler_params=_params(("parallel",)),
    )(za, wbd, scale)


def _pool_bwd(dyb, pd, wbd, scale, name):
    S = pd.shape[0]

    def body(dy_ref, pd_ref, w_ref, sc_ref, du_ref, dw_ref, dsc_ref):
        half = pl.program_id(0)
        dy = dy_ref[...]
        pd = pd_ref[...]
        w = w_ref[...]
        ypre = _dot(pd, w)
        dsc_ref[...] = jnp.sum(dy * ypre, axis=0, keepdims=True)
        dyp = (dy * sc_ref[...]).astype(BF16)
        dw_ref[...] = _dot_tn(pd, dyp)
        dpd = _dot_nt(dyp, w)
        row = lax.broadcasted_iota(jnp.int32, (S, LANES), 0)
        lane_lo = lax.broadcasted_iota(jnp.int32, (S, LANES), 1) < POOL_GROUP
        acc = dpd / _pool_den(S, half, lane_lo)
        sums = []
        k = 1
        while k < POOL_WINDOWS[-1]:
            acc = acc + jnp.where(row < S - k, pltpu.roll(acc, S - k, 0), 0.0)
            sums.append(acc)
            k *= 2
        du_ref[...] = (_pool_select(half, lane_lo, sums) - dpd).astype(BF16)

    return pl.pallas_call(
        body, name=name, grid=(2,),
        in_specs=[_bs((S, LANES), lambda i: (0, i)), _bs((S, LANES), lambda i: (0, i)),
                  _bs((None, LANES, LANES), lambda i: (i, 0, 0)), _bs((1, LANES), lambda i: (0, i))],
        out_specs=[_bs((S, LANES), lambda i: (0, i)), _bs((None, LANES, LANES), lambda i: (i, 0, 0)),
                   _bs((1, LANES), lambda i: (0, i))],
        out_shape=[jax.ShapeDtypeStruct((S, POOL_WIDTH), BF16), jax.ShapeDtypeStruct((2, LANES, LANES), F32),
                   jax.ShapeDtypeStruct((1, POOL_WIDTH), F32)],
        compiler_params=_params(("parallel",)),
    )(dyb, pd, wbd, scale)


def _mix_out_fwd(x, ya, yb, yc, w_out, name):
    S, D = x.shape
    tm = _tile(S, 512)
    K = w_out.shape[0]

    def body(x_ref, ya_ref, yb_ref, yc_ref, w_ref, xo_ref, yc_out):
        ycat = jnp.concatenate([ya_ref[...], yb_ref[...], yc_ref[...]], axis=1)
        yc_out[...] = ycat
        xo_ref[...] = x_ref[...] + _dot(ycat, w_ref[...])

    row = lambda i: (i, 0)
    return pl.pallas_call(
        body, name=name, grid=(S // tm,),
        in_specs=[_bs((tm, D), row), _bs((tm, 384), row), _bs((tm, 256), row), _bs((tm, 384), row),
                  _bs((K, D), lambda i: (0, 0))],
        out_specs=[_bs((tm, D), row), _bs((tm, K), row)],
        out_shape=[jax.ShapeDtypeStruct((S, D), F32), jax.ShapeDtypeStruct((S, K), BF16)],
        compiler_params=_params(("parallel",)),
    )(x, ya, yb, yc, w_out)


def _mix_out_bwd(dy, w_out, ya, yc, name):
    S, D = dy.shape
    tm = _tile(S, 512)
    K = w_out.shape[0]
    nhp = ya.shape[1] // LANES

    def body(dy_ref, w_ref, ya_ref, yc_ref, da_ref, db_ref, dc_ref, dva_ref, dvc_ref):
        d = _dot_nt(dy_ref[...].astype(BF16), w_ref[...])
        da = d[:, 0:384].astype(BF16)
        dc = d[:, 640:1024].astype(BF16)
        da_ref[...] = da
        db_ref[...] = d[:, 384:640]
        dc_ref[...] = dc
        li = lax.broadcasted_iota(jnp.int32, (tm, LANES), 1)
        for do, o_ref, out_ref in ((da, ya_ref, dva_ref), (dc, yc_ref, dvc_ref)):
            for hp in range(nhp):
                sl = slice(hp * LANES, (hp + 1) * LANES)
                prod = do[:, sl].astype(F32) * o_ref[:, sl].astype(F32)
                d0 = jnp.sum(jnp.where(li < 64, prod, 0.0), axis=1, keepdims=True)
                d1 = jnp.sum(jnp.where(li >= 64, prod, 0.0), axis=1, keepdims=True)
                out_ref[hp] = jnp.where(li == 0, d0, jnp.where(li == 1, d1, 0.0)).T[0:8, :]

    row = lambda i: (i, 0)
    dv_spec = _bs((nhp, 8, tm), lambda i: (0, 0, i))
    dv_shape = jax.ShapeDtypeStruct((nhp, 8, S), F32)
    return pl.pallas_call(
        body, name=name, grid=(S // tm,),
        in_specs=[_bs((tm, D), row), _bs((K, D), lambda i: (0, 0)), _bs((tm, 384), row), _bs((tm, 384), row)],
        out_specs=[_bs((tm, 384), row), _bs((tm, 256), row), _bs((tm, 384), row), dv_spec, dv_spec],
        out_shape=[jax.ShapeDtypeStruct((S, 384), BF16), jax.ShapeDtypeStruct((S, 256), F32),
                   jax.ShapeDtypeStruct((S, 384), BF16), dv_shape, dv_shape],
        compiler_params=_params(("parallel",)),
    )(dy, w_out, ya, yc)


def _loss_head(x, gam, target, name):
    S, D = x.shape
    tm = _tile(S, 512)

    def body(x_ref, gam_ref, t_ref, dx_ref, dgam_ref, loss_ref):
        i = pl.program_id(0)
        xv = x_ref[...]
        err = _rms(xv, gam_ref[...]) - t_ref[...]
        part = 0.5 * jnp.sum(jnp.mean(err * err, axis=-1, keepdims=True), axis=0, keepdims=True)
        dxn, dgam = _rms_bwd(err * (1.0 / D), xv, gam_ref[...])
        dx_ref[...] = dxn
        _accum_out(dgam_ref, i == 0, dgam)
        _accum_out(loss_ref, i == 0, jnp.broadcast_to(part, (1, LANES)))

    row = lambda i: (i, 0)
    fix = lambda i: (0, 0)
    return pl.pallas_call(
        body, name=name, grid=(S // tm,),
        in_specs=[_bs((tm, D), row), _bs((1, D), fix), _bs((tm, D), row)],
        out_specs=[_bs((tm, D), row), _bs((1, D), fix), _bs((1, LANES), fix)],
        out_shape=[jax.ShapeDtypeStruct((S, D), F32), jax.ShapeDtypeStruct((1, D), F32),
                   jax.ShapeDtypeStruct((1, LANES), F32)],
        compiler_params=_params(("arbitrary",)),
    )(x, gam, target)


def _adam_math(g, w, m, v):
    m = ADAM_B1 * m + (1.0 - ADAM_B1) * g
    v = ADAM_B2 * v + (1.0 - ADAM_B2) * (g * g)
    m_hat = m / (1.0 - ADAM_B1 ** ADAM_STEP)
    v_hat = v / (1.0 - ADAM_B2 ** ADAM_STEP)
    delta = -ADAM_LR * (m_hat / (jnp.sqrt(v_hat) + ADAM_EPS) + ADAM_WD * w)
    return delta, m, v


def _adam_sum(recv, w, m, v, layer, prev, tr, name):
    L, R, C = w.shape
    Cp = recv.shape[2]
    tr = _tile(R, tr)

    def body(r_ref, w_ref, m_ref, v_ref, *rest):
        g_out, d_out, m_out, v_out = rest[len(rest) - 4:]
        g = r_ref[0, :, 0:C].astype(F32)
        for p in range(1, N_DEV):
            g = g + r_ref[p, :, 0:C].astype(F32)
        delta, mn, vn = _adam_math(g, w_ref[...], m_ref[...], v_ref[...])
        g_out[...] = g
        d_out[...] = delta
        m_out[...] = mn
        v_out[...] = vn

    blk = _bs((None, tr, C), lambda i: (layer, i, 0))
    shp = jax.ShapeDtypeStruct((L, R, C), F32)
    in_specs = [_bs((N_DEV, tr, Cp), lambda i: (0, i, 0)), blk, blk, blk]
    args = [recv, w, m, v]
    aliases = {}
    if prev is not None:
        in_specs += [HBM_SPEC] * 4
        args += list(prev)
        aliases = {4 + k: k for k in range(4)}
    return pl.pallas_call(
        body, name=name, grid=(R // tr,),
        in_specs=in_specs, out_specs=[blk, blk, blk, blk], out_shape=[shp, shp, shp, shp],
        input_output_aliases=aliases, compiler_params=_params(("parallel",)),
    )(*args)


def _dev_index(px, py, pc):
    return 4 * px + 2 * py + pc


class _GatherRider:
    def __init__(self, shards, out_shapes, views):
        self.n = len(shards)
        self.views = views
        self.srcs = list(shards)
        self.out_shapes = list(out_shapes)
        n = self.n
        self.scratch = [pltpu.SemaphoreType.DMA((n, 7)), pltpu.SemaphoreType.DMA((n, 7)), pltpu.SemaphoreType.DMA((n,))]

    def _copies(self, ins, outs, sems):
        n = self.n
        send_sems, recv_sems, local_sems = sems
        x, y, c = lax.axis_index("x"), lax.axis_index("y"), lax.axis_index("c")
        me, sibling = (x, y, c), (x, y, 1 - c)
        chips = [(1 - x, y), (x, 1 - y), (1 - x, 1 - y)]

        def rows(a, blk):
            return self.views[a](outs[a], _dev_index(*blk))

        def copy(a, k, blk, to, src=None):
            return pltpu.make_async_remote_copy(
                src_ref=rows(a, blk) if src is None else src, dst_ref=rows(a, blk),
                send_sem=send_sems.at[a, k], recv_sem=recv_sems.at[a, k], device_id=to, device_id_type=MESH_ID)

        local = [pltpu.make_async_copy(ins[a], rows(a, me), local_sems.at[a]) for a in range(n)]
        first = []
        for a in range(n):
            first.append(copy(a, 0, me, sibling, src=ins[a]))
            first += [copy(a, 1 + j, me, (*chip, c), src=ins[a]) for j, chip in enumerate(chips)]
        over_ici = [[copy(a, 1 + j, (*chip, c), me) for a in range(n)] for j, chip in enumerate(chips)]
        passed = [[copy(a, 4 + j, (*chip, c), sibling) for a in range(n)] for j, chip in enumerate(chips)]
        from_sibling = [copy(a, 0, sibling, me) for a in range(n)]
        from_sibling += [copy(a, 4 + j, (*chip, 1 - c), me) for a in range(n) for j, chip in enumerate(chips)]
        return local, first, over_ici, passed, from_sibling

    def begin(self, ins, outs, sems):
        local, first, _, _, _ = self._copies(ins, outs, sems)
        for cp in local + first:
            cp.start()

    def middle(self, ins, outs, sems):
        _, _, over_ici, passed, _ = self._copies(ins, outs, sems)
        for arrived, onward in zip(over_ici, passed):
            for cp, fwd in zip(arrived, onward):
                cp.wait_recv()
                fwd.start()

    def end(self, ins, outs, sems):
        local, first, _, passed, from_sibling = self._copies(ins, outs, sems)
        for cp in from_sibling:
            cp.wait_recv()
        for cp in first + [fwd for onward in passed for fwd in onward]:
            cp.wait_send()
        for cp in local:
            cp.wait()


class _ScatterRider:
    _MASKS = [(kx, ky, kc) for kx in (0, 1) for ky in (0, 1) for kc in (0, 1)][1:]

    def __init__(self, srcs, out_shapes, src_of, dst_at):
        self.n = len(srcs)
        self.srcs = list(srcs)
        self.out_shapes = list(out_shapes)
        self.src_of = src_of
        self.dst_at = dst_at
        n = self.n
        self.scratch = [pltpu.SemaphoreType.DMA((n, 7)), pltpu.SemaphoreType.DMA((n, 7)), pltpu.SemaphoreType.DMA((n,))]

    def _copies(self, ins, outs, sems):
        send_sems, recv_sems, local_sems = sems
        x, y, c = lax.axis_index("x"), lax.axis_index("y"), lax.axis_index("c")
        my = _dev_index(x, y, c)
        peers = [(1 - x if kx else x, 1 - y if ky else y, 1 - c if kc else c) for kx, ky, kc in self._MASKS]

        def send(i, k, to):
            return pltpu.make_async_remote_copy(
                src_ref=self.src_of[i](ins[i], _dev_index(*to)), dst_ref=self.dst_at[i](outs[i], my),
                send_sem=send_sems.at[i, k], recv_sem=recv_sems.at[i, k], device_id=to, device_id_type=MESH_ID)

        def arrival(i, k, frm):
            slot = self.dst_at[i](outs[i], _dev_index(*frm))
            return pltpu.make_async_remote_copy(
                src_ref=slot, dst_ref=slot, send_sem=send_sems.at[i, k], recv_sem=recv_sems.at[i, k],
                device_id=frm, device_id_type=MESH_ID)

        local = [pltpu.make_async_copy(self.src_of[i](ins[i], my), self.dst_at[i](outs[i], my), local_sems.at[i])
                 for i in range(self.n)]
        sends = [send(i, k, to) for k, to in enumerate(peers) for i in range(self.n)]
        arrivals = [arrival(i, k, frm) for k, frm in enumerate(peers) for i in range(self.n)]
        return local, sends, arrivals

    def begin(self, ins, outs, sems):
        local, sends, _ = self._copies(ins, outs, sems)
        for cp in local + sends:
            cp.start()

    def middle(self, ins, outs, sems):
        pass

    def end(self, ins, outs, sems):
        local, sends, arrivals = self._copies(ins, outs, sems)
        for cp in arrivals:
            cp.wait_recv()
        for cp in sends:
            cp.wait_send()
        for cp in local:
            cp.wait()


def _comm_call(rider, name):
    k_in = len(rider.srcs)
    k_out = len(rider.out_shapes)

    def body(*refs):
        ins, outs, sems = refs[:k_in], refs[k_in:k_in + k_out], refs[k_in + k_out:]
        rider.begin(ins, outs, sems)
        rider.middle(ins, outs, sems)
        rider.end(ins, outs, sems)

    return pl.pallas_call(
        body, name=name, in_specs=[HBM_SPEC] * k_in, out_specs=[HBM_SPEC] * k_out, out_shape=rider.out_shapes,
        scratch_shapes=rider.scratch, compiler_params=pltpu.CompilerParams(has_side_effects=True),
    )(*rider.srcs)


def _pad_w_in(w):
    cols = lambda a, b: w[..., a:b]
    zeros = lambda n: jnp.zeros(w.shape[:-1] + (n,), w.dtype)
    pairs = FOX_HEADS // 2
    parts = [cols(0, 384), cols(416, 672)]
    for hp in range(pairs):
        parts += [cols(1824 + 2 * hp, 1826 + 2 * hp), zeros(6)]
    parts += [zeros(ROPE_LANE0 - 8 * pairs), cols(384, 416), zeros(LANES - ROPE_LANE0 - MLA_ROPE), cols(672, 1824)]
    return jnp.concatenate(parts, axis=-1)


def _unpad_w_in(g):
    cols = lambda a, b: g[..., a:b]
    rope0 = TAIL0 + ROPE_LANE0
    parts = [cols(0, 384), cols(rope0, rope0 + MLA_ROPE), cols(384, 640), cols(ZA, N_PAD)]
    parts += [cols(TAIL0 + 8 * hp, TAIL0 + 8 * hp + 2) for hp in range(FOX_HEADS // 2)]
    return jnp.concatenate(parts, axis=-1)


def _small_pack(w_q_b, w_kv_b):
    a = jnp.pad(w_q_b, ((0, 0), (0, 0), (0, LANES - w_q_b.shape[2])))
    b = jnp.pad(w_kv_b, ((0, 0), (0, 0), (0, LANES - w_kv_b.shape[2])))
    return jnp.concatenate([a, b], axis=1)


def _small_unpack(p, cq, ckv):
    return p[:, 0:MLA_Q_RANK, 0:cq], p[:, MLA_Q_RANK:, 0:ckv]


def _mla_weights(wsm):
    H = MLA_HEADS
    cq = H * (MLA_NOPE + MLA_ROPE) // N_DEV
    ckv = H * (MLA_NOPE + MLA_V) // N_DEV
    wq = wsm[:, 0:MLA_Q_RANK, 0:cq].transpose(1, 0, 2).reshape(MLA_Q_RANK, H, MLA_NOPE + MLA_ROPE)
    wq = jnp.pad(wq, ((0, 0), (0, 0), (0, HEAD_BLOCK - MLA_NOPE - MLA_ROPE))).reshape(MLA_Q_RANK, H * HEAD_BLOCK)
    wkv = wsm[:, MLA_Q_RANK:, 0:ckv].transpose(1, 0, 2).reshape(MLA_KV_RANK, H, MLA_NOPE + MLA_V)
    wk = jnp.pad(wkv[:, :, 0:MLA_NOPE], ((0, 0), (0, 0), (0, HEAD_BLOCK - MLA_NOPE))).reshape(MLA_KV_RANK, H * HEAD_BLOCK)
    wv = wkv[:, :, MLA_NOPE:].reshape(MLA_KV_RANK, H * MLA_V)
    return wq, jnp.concatenate([wk, wv], axis=1)


def _mla_grads_to_blocks(dwq, dwkv):
    H = MLA_HEADS
    gq = dwq.reshape(MLA_Q_RANK, H, HEAD_BLOCK)[:, :, 0:MLA_NOPE + MLA_ROPE].reshape(MLA_Q_RANK, N_DEV, -1)
    gk = dwkv[:, 0:H * HEAD_BLOCK].reshape(MLA_KV_RANK, H, HEAD_BLOCK)[:, :, 0:MLA_NOPE]
    gv = dwkv[:, H * HEAD_BLOCK:].reshape(MLA_KV_RANK, H, MLA_V)
    gkv = jnp.concatenate([gk, gv], axis=2).reshape(MLA_KV_RANK, N_DEV, -1)
    return _small_pack(gq.transpose(1, 0, 2), gkv.transpose(1, 0, 2)).astype(BF16)


def _pool_blockdiag(pool_w):
    z = jnp.zeros((POOL_GROUP, POOL_GROUP), pool_w.dtype)
    halves = [jnp.concatenate([jnp.concatenate([pool_w[2 * i], z], axis=1),
                               jnp.concatenate([z, pool_w[2 * i + 1]], axis=1)], axis=0) for i in range(2)]
    return jnp.stack(halves)


def _pool_blockdiag_t(dw):
    g = POOL_GROUP
    return jnp.stack([dw[0, 0:g, 0:g], dw[0, g:, g:], dw[1, 0:g, 0:g], dw[1, g:, g:]])


def _gate_row(b):
    return jnp.zeros((LANES,), b.dtype).at[_F_LANES].set(b).reshape(1, LANES)


_SMALL = ("ffn1_norm", "mix_norm", "q_a_norm", "kv_a_norm", "pool_w", "pool_scale", "fox_b_f", "ffn2_norm", "final_norm")


def _pack_small(tree):
    rows, recipe = [], []
    for name in _SMALL:
        a = tree[name]
        flat = a.reshape(-1)
        n = flat.shape[0]
        nrow = -(-n // (8 * LANES)) * 8
        flat = jnp.pad(flat, (0, nrow * LANES - n))
        rows.append(flat.reshape(nrow, LANES))
        recipe.append((name, a.shape, n, nrow))
    return jnp.concatenate(rows, axis=0), recipe


def _unpack_small(packed, recipe):
    out, r0 = {}, 0
    for name, shape, n, nrow in recipe:
        out[name] = packed[r0:r0 + nrow].reshape(-1)[0:n].reshape(shape)
        r0 += nrow
    return out


def _adam_small(packs, w, m, v, name):
    R = w.shape[0]

    def body(p_ref, w_ref, m_ref, v_ref, g_out, d_out, m_out, v_out):
        g = p_ref[0]
        for p in range(1, N_DEV):
            g = g + p_ref[p]
        delta, mn, vn = _adam_math(g, w_ref[...], m_ref[...], v_ref[...])
        g_out[...] = g
        d_out[...] = delta
        m_out[...] = mn
        v_out[...] = vn

    blk = _bs((R, LANES), lambda i: (0, 0))
    shp = jax.ShapeDtypeStruct((R, LANES), F32)
    return pl.pallas_call(
        body, name=name, grid=(1,),
        in_specs=[_bs((N_DEV, R, LANES), lambda i: (0, 0, 0)), blk, blk, blk],
        out_specs=[blk, blk, blk, blk], out_shape=[shp, shp, shp, shp],
        compiler_params=_params(("arbitrary",)),
    )(packs, w, m, v)


_GATHER_PLAN = {
    ("first", 0): (("ffn1_w_gu", 0), ("ffn1_w_down", 0)),
    ("ffn1_fwd", 0): (("w_in", 0), ("w_small", 0), ("w_out", 0), ("ffn2_w_down", 0)),
    ("mla_attn_fwd", 0): (("ffn2_w_gu", 0),),
    ("fox_attn_fwd", 0): (("ffn1_w_down", 1), ("w_in", 1), ("w_small", 1), ("w_out", 1)),
    ("ffn2_fwd", 0): (("ffn1_w_gu", 1),),
    ("mla_attn_fwd", 1): (("ffn2_w_gu", 1), ("ffn2_w_down", 1)),
}
_SCATTER_PLAN = {
    ("mla_attn_bwd", 1): (("ffn2_w_gu", 1), ("w_out", 1)),
    ("fox_attn_bwd", 1): (("ffn2_w_down", 1),),
    ("ffn1_bwd", 1): (("w_in", 1), ("w_small", 1)),
    ("ffn2_bwd", 0): (("ffn1_w_down", 1),),
    ("mla_attn_bwd", 0): (("ffn1_w_gu", 1), ("w_out", 0)),
    ("fox_attn_bwd", 0): (("ffn2_w_gu", 0), ("ffn2_w_down", 0)),
    ("ffn1_bwd_a", 0): (("w_in", 0), ("w_small", 0)),
    ("ffn1_dwgu", 0): (("ffn1_w_down", 0),),
    ("ffn1_bwd_b", 0): (("ffn1_w_gu", 0),),
}
_SPLIT_BWD = (("ffn1", 0),)


class _Exchange:
    def __init__(self, shards, D, f_sh, r_in, r_out):
        self.shards = shards
        self.D, self.f_sh, self.r_in, self.r_out = D, f_sh, r_in, r_out
        self.weights, self.grads, self.recv = {}, {}, {}

    def _rows(self, kind):
        n = {"w_gu": 2 * self.f_sh, "w_down": self.f_sh}[kind]
        return lambda ref, p: ref.at[pl.ds(pl.multiple_of(p * n, 16), n)]

    def _gathered_shape(self, kind):
        D, f_sh = self.D, self.f_sh
        return {"w_gu": (N_DEV * 2 * f_sh, D), "w_down": (N_DEV * f_sh, D), "w_in": (N_DEV, self.r_in, N_PAD),
                "w_small": (N_DEV, MLA_Q_RANK + MLA_KV_RANK, LANES), "w_out": (N_DEV, self.r_out, D)}[kind]

    def _recv_shape(self, kind):
        D, f_sh = self.D, self.f_sh
        return {"w_gu": (N_DEV, 2 * f_sh, D), "w_down": (N_DEV, f_sh, D), "w_in": (N_DEV, self.r_in, N_IN),
                "w_small": (N_DEV, MLA_Q_RANK + MLA_KV_RANK, LANES), "w_out": (N_DEV, self.r_out, D)}[kind]

    @staticmethod
    def _kind(name):
        return name[5:] if name.startswith("ffn") else name

    def gather_rider(self, call, l):
        keys = _GATHER_PLAN.get((call, l))
        if not keys:
            return None
        by_dev = lambda ref, p: ref.at[p]
        shards, shapes, views = [], [], []
        for key in keys:
            kind = self._kind(key[0])
            shards.append(self.shards[key])
            shapes.append(jax.ShapeDtypeStruct(self._gathered_shape(kind), BF16))
            views.append(self._rows(kind) if kind in ("w_gu", "w_down") else by_dev)
        return _GatherRider(shards, shapes, views)

    def gathered(self, call, l, outs):
        for key, w in zip(_GATHER_PLAN.get((call, l), ()), outs):
            if self._kind(key[0]) in ("w_in", "w_out"):
                w = w.reshape((N_DEV * w.shape[1],) + w.shape[2:])
            self.weights[key] = w

    def scatter_rider(self, call, l, pack=None):
        keys = _SCATTER_PLAN.get((call, l), ())
        if not keys and pack is None:
            return None
        by_dev = lambda ref, p: ref.at[p]
        srcs, shapes, src_of = [], [], []
        for key in keys:
            kind = self._kind(key[0])
            srcs.append(self.grads[key])
            shapes.append(jax.ShapeDtypeStruct(self._recv_shape(kind), BF16))
            src_of.append(self._rows(kind) if kind in ("w_gu", "w_down") else by_dev)
        if pack is not None:
            srcs.append(pack)
            shapes.append(jax.ShapeDtypeStruct((N_DEV,) + pack.shape, pack.dtype))
            src_of.append(lambda ref, p: ref)
        return _ScatterRider(srcs, shapes, src_of, [by_dev] * len(srcs))

    def scattered(self, call, l, outs):
        for key, r in zip(_SCATTER_PLAN.get((call, l), ()), outs):
            self.recv[key] = r


def _local_step(x, target, ex, small):
    S, D = x.shape
    tabs = _rope_tables(S)
    nhp_a, nhp_c = MLA_HEADS // 2, FOX_HEADS // 2
    fox_scale = 1.0 / math.sqrt(FOX_HEAD_DIM)
    ex.gathered("first", 0, _comm_call(ex.gather_rider("first", 0), "gather_first"))
    saved = []
    for l in range(DEPTH):
        s = {}
        s["x0"] = x
        wgu1, wd1 = ex.weights[("ffn1_w_gu", l)], ex.weights[("ffn1_w_down", l)]
        (x1, s["h1"], s["gu1"]), got = _ffn_fwd_full(x, small["ffn1_norm"][l][None], wgu1, wd1, FFN_FWD_TOKENS,
                                                    f"ffn1_fwd_l{l}", rider=ex.gather_rider("ffn1_fwd", l))
        ex.gathered("ffn1_fwd", l, got)
        s["x1"] = x1
        w_in = ex.weights[("w_in", l)]
        wq, wkv = _mla_weights(ex.weights[("w_small", l)])
        s["wq"], s["wkv"] = wq, wkv
        gq, gkv = small["q_a_norm"][l][None], small["kv_a_norm"][l][None]
        s["h2"], za, zf, zkvt, qf, kf, vm, kft, vmt = _mix_in_mla_fwd(
            x1, small["mix_norm"][l][None], w_in, gq, gkv, wq, wkv, tabs, f"mix_in_mla_fwd_l{l}")
        s["za"], s["zf"], s["zkvt"] = za, zf, zkvt
        s["qf"], s["kf"], s["vm"], s["kft"] = qf, kf, vm, kft
        (ya, lse_a), got = _attn_fwd_t(qf, kf, vmt, nhp=nhp_a, dkb=LANES, qoff=0, koff=0, vtoff=0, scale=1.0,
                                       cum=None, cumT=None, name=f"mla_attn_fwd_l{l}",
                                       rider=ex.gather_rider("mla_attn_fwd", l))
        ex.gathered("mla_attn_fwd", l, got)
        s["ya"], s["lse_a"] = ya, lse_a
        b_row = _gate_row(small["fox_b_f"][l])
        s["b_row"] = b_row
        cum, cumT = _fox_prep(za, b_row, f"fox_prep_l{l}")
        s["cum"], s["cumT"] = cum, cumT
        (yc, lse_c), got = _attn_fwd_t(zf, zf, zkvt, nhp=nhp_c, dkb=64, qoff=0, koff=nhp_c, vtoff=nhp_c, scale=fox_scale,
                                       cum=cum, cumT=cumT, name=f"fox_attn_fwd_l{l}",
                                       rider=ex.gather_rider("fox_attn_fwd", l))
        ex.gathered("fox_attn_fwd", l, got)
        s["yc"], s["lse_c"] = yc, lse_c
        wbd = _pool_blockdiag(small["pool_w"][l]).astype(BF16)
        s["wbd"] = wbd
        psc = small["pool_scale"][l][None]
        yb, s["pd"] = _pool_fwd(za, wbd, psc, f"pool_fwd_l{l}")
        w_out = ex.weights[("w_out", l)]
        x2, s["ycat"] = _mix_out_fwd(x1, ya, yb, yc, w_out, f"mix_out_fwd_l{l}")
        s["x2"] = x2
        wgu2, wd2 = ex.weights[("ffn2_w_gu", l)], ex.weights[("ffn2_w_down", l)]
        (x, s["h3"], s["gu2"]), got = _ffn_fwd_full(x2, small["ffn2_norm"][l][None], wgu2, wd2, FFN_FWD_TOKENS,
                                                    f"ffn2_fwd_l{l}", rider=ex.gather_rider("ffn2_fwd", l))
        ex.gathered("ffn2_fwd", l, got)
        saved.append(s)

    dx, d_final, loss = _loss_head(x, small["final_norm"][None], target, "loss_head")

    small_grads = [None] * DEPTH
    for l in reversed(range(DEPTH)):
        s = saved[l]
        g = {}
        wgu2, wd2 = ex.weights[("ffn2_w_gu", l)], ex.weights[("ffn2_w_down", l)]
        dy3 = dx
        (dx, g["ffn2_norm"], dgu, act, dyh), got = _ffn_bwd_full(
            dy3, s["x2"], small["ffn2_norm"][l][None], s["gu2"], wgu2, wd2, FFN_BWD_TOKENS, f"ffn2_bwd_l{l}",
            rider=ex.scatter_rider("ffn2_bwd", l))
        ex.scattered("ffn2_bwd", l, got)
        ex.grads[("ffn2_w_gu", l)] = _ffn_weight_grad(dgu, s["h3"], f"ffn2_dwgu_l{l}")
        ex.grads[("ffn2_w_down", l)] = _ffn_weight_grad(act, dyh, f"ffn2_dwd_l{l}")

        w_out = ex.weights[("w_out", l)]
        dya, dyb, dyc, dvec_a, dvec_c = _mix_out_bwd(dx, w_out, s["ya"], s["yc"], f"mix_out_bwd_l{l}")
        dw_out = _mm_tn(s["ycat"][None], dx[None], 1, lambda p: 0, lambda p: 0, 1024, 1024, f"dwout_l{l}", ts=1024)[0]
        ex.grads[("w_out", l)] = dw_out.reshape(N_DEV, ex.r_out, D)

        (dqf, dkf, dvm), got = _attn_bwd_t(s["qf"], s["kf"], s["kft"], s["vm"], dya, s["lse_a"], dvec_a, nhp=nhp_a,
                                           dkb=LANES, qoff=0, koff=0, ktoff=0, voff=0, scale=1.0, cum=None, cumT=None,
                                           name=f"mla_attn_bwd_l{l}", rider=ex.scatter_rider("mla_attn_bwd", l))
        ex.scattered("mla_attn_bwd", l, got)
        zf = s["zf"]
        (dqc, dkc, dvc, dcq, dck), got = _attn_bwd_t(zf, zf, s["zkvt"], zf, dyc, s["lse_c"], dvec_c, nhp=nhp_c, dkb=64,
                                                     qoff=0, koff=nhp_c, ktoff=0, voff=2 * nhp_c, scale=fox_scale,
                                                     cum=s["cum"], cumT=s["cumT"], name=f"fox_attn_bwd_l{l}",
                                                     rider=ex.scatter_rider("fox_attn_bwd", l), grad_dtype=BF16)
        ex.scattered("fox_attn_bwd", l, got)
        dtail_f, db = _fox_prep_bwd(s["za"], s["b_row"], dcq, dck, f"fox_prep_bwd_l{l}")
        g["fox_b_f"] = db[0, _F_LANES]
        psc = small["pool_scale"][l][None]
        du, dwbd, dpsc = _pool_bwd(dyb, s["pd"], s["wbd"], psc, f"pool_bwd_l{l}")
        g["pool_w"] = _pool_blockdiag_t(dwbd)
        g["pool_scale"] = dpsc[0]
        gq, gkv = small["q_a_norm"][l][None], small["kv_a_norm"][l][None]
        dza, dtail_k, dwq, dwkv, dgq, dgkv = _mla_prep_bwd(s["za"], gq, gkv, s["wq"], s["wkv"], tabs, dqf, dkf, dvm,
                                                            f"mla_prep_bwd_l{l}")
        g["q_a_norm"], g["kv_a_norm"] = dgq[0], dgkv[0]
        ex.grads[("w_small", l)] = _mla_grads_to_blocks(dwq, dwkv)
        w_in = ex.weights[("w_in", l)]
        dx, g["mix_norm"], dz = _mix_in_bwd(dx, s["x1"], small["mix_norm"][l][None], dza, dtail_k, du, dtail_f,
                                            dqc, dkc, dvc, w_in, f"mix_in_bwd_l{l}")
        dw_in = _unpad_w_in(_mm_tn(s["h2"][None], dz[None], 1, lambda p: 0, lambda p: 0, 1024, 640, f"dwin_l{l}",
                                   ts=4096)[0])
        ex.grads[("w_in", l)] = dw_in.reshape(N_DEV, ex.r_in, N_IN)

        wgu1, wd1 = ex.weights[("ffn1_w_gu", l)], ex.weights[("ffn1_w_down", l)]
        dy1 = dx
        gam1 = small["ffn1_norm"][l][None]
        split = ("ffn1", l) in _SPLIT_BWD
        if split:
            (dgu, act, dyh), got = _ffn_bwd_full(dy1, None, None, s["gu1"], None, wd1, FFN_BWD_TOKENS, f"ffn1_bwd_a_l{l}",
                                                 phase="act", rider=ex.scatter_rider("ffn1_bwd_a", l))
            ex.scattered("ffn1_bwd_a", l, got)
        else:
            (dx, g["ffn1_norm"], dgu, act, dyh), got = _ffn_bwd_full(
                dy1, s["x0"], gam1, s["gu1"], wgu1, wd1, FFN_BWD_TOKENS, f"ffn1_bwd_l{l}",
                rider=ex.scatter_rider("ffn1_bwd", l))
            ex.scattered("ffn1_bwd", l, got)
        ex.grads[("ffn1_w_down", l)] = _ffn_weight_grad(act, dyh, f"ffn1_dwd_l{l}")
        rider = ex.scatter_rider("ffn1_dwgu", l)
        dwgu = _ffn_weight_grad(dgu, s["h1"], f"ffn1_dwgu_l{l}", rider=rider)
        if rider is not None:
            dwgu, got = dwgu
            ex.scattered("ffn1_dwgu", l, got)
        ex.grads[("ffn1_w_gu", l)] = dwgu
        if split:
            (dx, g["ffn1_norm"]), got = _ffn_bwd_full(dy1, s["x0"], gam1, None, wgu1, None, FFN_BWD_TOKENS,
                                                     f"ffn1_bwd_b_l{l}", phase="in", dgu_in=dgu,
                                                     rider=ex.scatter_rider("ffn1_bwd_b", l))
            ex.scattered("ffn1_bwd_b", l, got)
        for k in ("ffn1_norm", "ffn2_norm", "mix_norm"):
            g[k] = g[k][0]
        small_grads[l] = g
    return loss, dx, small_grads, d_final[0]


_BIG = ("ffn1_w_gu", "ffn1_w_down", "w_in", "w_small", "w_out", "ffn2_w_gu", "ffn2_w_down")


def kernel(x, ffn1_norm, ffn1_w_gu, ffn1_w_down, mix_norm, w_in, q_a_norm, w_q_b, kv_a_norm, w_kv_b, pool_w, pool_scale, fox_b_f, w_out, ffn2_norm, ffn2_w_gu, ffn2_w_down, final_norm, loss_target, m_ffn1_norm, m_ffn1_w_gu, m_ffn1_w_down, m_mix_norm, m_w_in, m_q_a_norm, m_w_q_b, m_kv_a_norm, m_w_kv_b, m_pool_w, m_pool_scale, m_fox_b_f, m_w_out, m_ffn2_norm, m_ffn2_w_gu, m_ffn2_w_down, m_final_norm, v_ffn1_norm, v_ffn1_w_gu, v_ffn1_w_down, v_mix_norm, v_w_in, v_q_a_norm, v_w_q_b, v_kv_a_norm, v_w_kv_b, v_pool_w, v_pool_scale, v_fox_b_f, v_w_out, v_ffn2_norm, v_ffn2_w_gu, v_ffn2_w_down, v_final_norm):
    W = dict(ffn1_norm=ffn1_norm, ffn1_w_gu=ffn1_w_gu, ffn1_w_down=ffn1_w_down, mix_norm=mix_norm, w_in=w_in,
             q_a_norm=q_a_norm, w_q_b=w_q_b, kv_a_norm=kv_a_norm, w_kv_b=w_kv_b, pool_w=pool_w, pool_scale=pool_scale,
             fox_b_f=fox_b_f, w_out=w_out, ffn2_norm=ffn2_norm, ffn2_w_gu=ffn2_w_gu, ffn2_w_down=ffn2_w_down,
             final_norm=final_norm)
    M = dict(ffn1_norm=m_ffn1_norm, ffn1_w_gu=m_ffn1_w_gu, ffn1_w_down=m_ffn1_w_down, mix_norm=m_mix_norm, w_in=m_w_in,
             q_a_norm=m_q_a_norm, w_q_b=m_w_q_b, kv_a_norm=m_kv_a_norm, w_kv_b=m_w_kv_b, pool_w=m_pool_w,
             pool_scale=m_pool_scale, fox_b_f=m_fox_b_f, w_out=m_w_out, ffn2_norm=m_ffn2_norm, ffn2_w_gu=m_ffn2_w_gu,
             ffn2_w_down=m_ffn2_w_down, final_norm=m_final_norm)
    V = dict(ffn1_norm=v_ffn1_norm, ffn1_w_gu=v_ffn1_w_gu, ffn1_w_down=v_ffn1_w_down, mix_norm=v_mix_norm, w_in=v_w_in,
             q_a_norm=v_q_a_norm, w_q_b=v_w_q_b, kv_a_norm=v_kv_a_norm, w_kv_b=v_w_kv_b, pool_w=v_pool_w,
             pool_scale=v_pool_scale, fox_b_f=v_fox_b_f, w_out=v_w_out, ffn2_norm=v_ffn2_norm, ffn2_w_gu=v_ffn2_w_gu,
             ffn2_w_down=v_ffn2_w_down, final_norm=v_final_norm)
    L, D, n_sh = ffn1_w_gu.shape
    f_sh = ffn1_w_down.shape[1]
    assert n_sh == 2 * f_sh and L == DEPTH
    r_in, r_out = w_in.shape[1], w_out.shape[1]

    tr_in = lambda a: a.transpose(0, 2, 1)
    big_shards = dict(
        ffn1_w_gu=tr_in(ffn1_w_gu).astype(BF16), ffn1_w_down=ffn1_w_down.astype(BF16),
        w_in=_pad_w_in(w_in).astype(BF16), w_small=_small_pack(w_q_b, w_kv_b).astype(BF16), w_out=w_out.astype(BF16),
        ffn2_w_gu=tr_in(ffn2_w_gu).astype(BF16), ffn2_w_down=ffn2_w_down.astype(BF16))
    ex = _Exchange({(k, l): big_shards[k][l] for k in _BIG for l in range(L)}, D, f_sh, r_in, r_out)

    small = {k: W[k] for k in _SMALL}
    loss, dx, grads, d_final = _local_step(x[0], loss_target[0], ex, small)

    small_g = {k: jnp.stack([grads[l][k] for l in range(L)]) for k in _SMALL if k != "final_norm"}
    small_g["final_norm"] = d_final
    pack_g, recipe = _pack_small(small_g)
    n_small = pack_g.shape[0]
    loss_row = -(-n_small // 8) * 8
    pack_g = jnp.concatenate([pack_g, jnp.zeros((loss_row - n_small, LANES), F32), jnp.broadcast_to(loss, (8, LANES))],
                             axis=0)
    *got, packs = _comm_call(ex.scatter_rider("last", 0, pack=pack_g), "scatter_last")
    ex.scattered("last", 0, got)

    out = {}
    sm_w, sm_m, sm_v = (_small_pack(t["w_q_b"], t["w_kv_b"]) for t in (W, M, V))
    big = [("ffn1_w_gu", tr_in(W["ffn1_w_gu"]), tr_in(M["ffn1_w_gu"]), tr_in(V["ffn1_w_gu"]), f_sh),
           ("ffn1_w_down", W["ffn1_w_down"], M["ffn1_w_down"], V["ffn1_w_down"], 352),
           ("w_in", W["w_in"], M["w_in"], V["w_in"], 128),
           ("w_small", sm_w, sm_m, sm_v, 384),
           ("w_out", W["w_out"], M["w_out"], V["w_out"], 128),
           ("ffn2_w_gu", tr_in(W["ffn2_w_gu"]), tr_in(M["ffn2_w_gu"]), tr_in(V["ffn2_w_gu"]), f_sh),
           ("ffn2_w_down", W["ffn2_w_down"], M["ffn2_w_down"], V["ffn2_w_down"], 352)]
    for k, w_, m_, v_, tr in big:
        res = None
        for l in range(L):
            res = _adam_sum(ex.recv[(k, l)], w_, m_, v_, l, res, tr, f"adam_{k}_l{l}")
        if k == "w_small":
            cq, ckv = w_q_b.shape[2], w_kv_b.shape[2]
            parts = [_small_unpack(r, cq, ckv) for r in res]
            out["w_q_b"] = [p[0] for p in parts]
            out["w_kv_b"] = [p[1] for p in parts]
        elif k.endswith("w_gu"):
            out[k] = [tr_in(r) for r in res]
        else:
            out[k] = res

    pw, _ = _pack_small({k: W[k] for k in _SMALL})
    pm, _ = _pack_small({k: M[k] for k in _SMALL})
    pv, _ = _pack_small({k: V[k] for k in _SMALL})
    extra = ((0, loss_row + 8 - n_small), (0, 0))
    res = _adam_small(packs, jnp.pad(pw, extra), jnp.pad(pm, extra), jnp.pad(pv, extra), "adam_small")
    loss_total = res[0][loss_row, 0]
    small_out = [_unpack_small(r, recipe) for r in res]
    for k in _SMALL:
        out[k] = [t[k] for t in small_out]

    names = ["ffn1_norm", "ffn1_w_gu", "ffn1_w_down", "mix_norm", "w_in", "q_a_norm", "w_q_b", "kv_a_norm", "w_kv_b",
             "pool_w", "pool_scale", "fox_b_f", "w_out", "ffn2_norm", "ffn2_w_gu", "ffn2_w_down", "final_norm"]
    outs = [loss_total, dx[None]]
    for which in range(4):
        outs += [out[k][which] for k in names]
    return tuple(outs)
```
